```python
import math
import jax, jax.numpy as jnp
from jax import lax
import numpy as np

D_MODEL = 1024
BATCH = 2
SEQ = 8192
DEPTH = 2

N_MIXERS = 2
EPS = 1e-6
SSM_EXPAND = 2
SSM_D_INNER = SSM_EXPAND * D_MODEL
SSM_HEAD_DIM = 64
SSM_HEADS = SSM_D_INNER // SSM_HEAD_DIM
SSM_GROUPS = 4
SSM_STATE = 128
SSM_CONV = 4
SSM_CHUNK = 128
SSM_GN = SSM_GROUPS * SSM_STATE
SSM_CONV_DIM = SSM_D_INNER + 2 * SSM_GN
SSM_IN_DIM = SSM_D_INNER + SSM_CONV_DIM + SSM_HEADS
ATTN_HEAD_DIM = 64
ATTN_Q_HEADS = D_MODEL // ATTN_HEAD_DIM
ATTN_KV_HEADS = 4
ATTN_Q_PER_KV = ATTN_Q_HEADS // ATTN_KV_HEADS
WINDOW = 128
ATTN_QKV_DIM = (ATTN_Q_HEADS + 2 * ATTN_KV_HEADS) * ATTN_HEAD_DIM
REL_BUCKETS = 32
REL_MAX_DIST = 128
N_EXPERTS = 32
TOP_K = 4
EXPERT_FF = D_MODEL
SWIGLU_ALPHA = 1.702
SWIGLU_LIMIT = 7.0
MOE_BLOCK = 128
N_SSM_LAYERS = (DEPTH + 1) // 2
N_ATTN_LAYERS = DEPTH // 2

kernel_name = "hybrid_ssd_swa_moe_adaln"


def rms_norm(x, g):
    xf = x.astype(jnp.float32)
    xf = xf * lax.rsqrt(jnp.mean(xf * xf, axis=-1, keepdims=True) + EPS)
    return (xf * g.astype(jnp.float32)).astype(x.dtype)


def modulate(x, g, shift, scale):
    return rms_norm(x, g) * (1 + scale[:, None, :]) + shift[:, None, :]


def ssd_chunked(xs, dt, A, Bm, Cm):
    b, s, h, p = xs.shape
    g, n = Bm.shape[2], Bm.shape[3]
    r = h // g
    nc = s // SSM_CHUNK
    X = (xs.astype(jnp.float32) * dt[..., None]).reshape(b, nc, SSM_CHUNK, g, r, p)
    dA = (dt * A).reshape(b, nc, SSM_CHUNK, g, r).transpose(0, 3, 4, 1, 2)
    Bc = Bm.astype(jnp.float32).reshape(b, nc, SSM_CHUNK, g, n)
    Cc = Cm.astype(jnp.float32).reshape(b, nc, SSM_CHUNK, g, n)
    A_cs = jnp.cumsum(dA, axis=-1)
    causal = jnp.tril(jnp.ones((SSM_CHUNK, SSM_CHUNK), dtype=bool))
    seg = A_cs[..., :, None] - A_cs[..., None, :]
    L = jnp.exp(jnp.where(causal, seg, -jnp.inf))
    CB = jnp.einsum("bclgn,bcsgn->bcgls", Cc, Bc)
    y_diag = jnp.einsum("bcgls,bgrcls,bcsgrp->bclgrp", CB, L, X)
    decay_states = jnp.exp(A_cs[..., -1:] - A_cs)
    states = jnp.einsum("bclgn,bgrcl,bclgrp->bcgrpn", Bc, decay_states, X)
    chunk_decay = jnp.exp(A_cs[..., -1])

    def step(h_prev, inp):
        st, dec = inp
        return h_prev * dec[..., None, None] + st, h_prev

    init = jnp.zeros((b, g, r, p, n), jnp.float32)
    _, h_in = lax.scan(step, init, (jnp.moveaxis(states, 1, 0), jnp.moveaxis(chunk_decay, 3, 0)))
    h_in = jnp.moveaxis(h_in, 0, 1)
    y_off = jnp.einsum("bclgn,bcgrpn,bgrcl->bclgrp", Cc, h_in, jnp.exp(A_cs))
    return (y_diag + y_off).reshape(b, s, h, p)


def mamba2_mixer(h, in_w, conv_w, conv_b, dt_bias, A_log, D_skip, norm_g, out_w):
    b, s, _ = h.shape
    zxbcdt = h @ in_w
    z = zxbcdt[..., :SSM_D_INNER]
    xbc = zxbcdt[..., SSM_D_INNER:SSM_D_INNER + SSM_CONV_DIM]
    dt_raw = zxbcdt[..., SSM_D_INNER + SSM_CONV_DIM:]
    xbc = lax.conv_general_dilated(xbc, conv_w[:, None, :].astype(xbc.dtype), (1,), [(SSM_CONV - 1, 0)],
                                   dimension_numbers=("NWC", "WIO", "NWC"),
                                   feature_group_count=SSM_CONV_DIM) + conv_b
    xbc = jax.nn.silu(xbc)
    xs = xbc[..., :SSM_D_INNER].reshape(b, s, SSM_HEADS, SSM_HEAD_DIM)
    Bm = xbc[..., SSM_D_INNER:SSM_D_INNER + SSM_GN].reshape(b, s, SSM_GROUPS, SSM_STATE)
    Cm = xbc[..., SSM_D_INNER + SSM_GN:].reshape(b, s, SSM_GROUPS, SSM_STATE)
    dt = jax.nn.softplus(dt_raw.astype(jnp.float32) + dt_bias.astype(jnp.float32))
    A = -jnp.exp(A_log.astype(jnp.float32))
    y = ssd_chunked(xs, dt, A, Bm, Cm) + D_skip.astype(jnp.float32)[:, None] * xs.astype(jnp.float32)
    y = y.reshape(b, s, SSM_D_INNER) * jax.nn.silu(z.astype(jnp.float32))
    y = y.reshape(b, s, SSM_GROUPS, SSM_D_INNER // SSM_GROUPS)
    y = y * lax.rsqrt(jnp.mean(y * y, axis=-1, keepdims=True) + EPS)
    y = (y.reshape(b, s, SSM_D_INNER) * norm_g.astype(jnp.float32)).astype(h.dtype)
    return y @ out_w


def t5_causal_bucket(dist):
    max_exact = REL_BUCKETS // 2
    d = jnp.maximum(dist, 1).astype(jnp.float32)
    large = max_exact + (jnp.log(d / max_exact) / math.log(REL_MAX_DIST / max_exact)
                         * (REL_BUCKETS - max_exact)).astype(jnp.int32)
    large = jnp.minimum(large, REL_BUCKETS - 1)
    return jnp.where(dist < max_exact, dist, large)


def head_rms(x, g):
    xf = x.astype(jnp.float32)
    return xf * lax.rsqrt(jnp.mean(xf * xf, axis=-1, keepdims=True) + EPS) * g.astype(jnp.float32)


def swa_mixer(h, qkv_w, q_norm_g, k_norm_g, sinks, rel_bias, out_w):
    b, s, _ = h.shape
    nb = s // WINDOW
    qkv = h @ qkv_w
    q_end = ATTN_Q_HEADS * ATTN_HEAD_DIM
    k_end = q_end + ATTN_KV_HEADS * ATTN_HEAD_DIM
    q = head_rms(qkv[..., :q_end].reshape(b, s, ATTN_KV_HEADS, ATTN_Q_PER_KV, ATTN_HEAD_DIM), q_norm_g)
    k = head_rms(qkv[..., q_end:k_end].reshape(b, s, ATTN_KV_HEADS, ATTN_HEAD_DIM), k_norm_g)
    v = qkv[..., k_end:].reshape(b, s, ATTN_KV_HEADS, ATTN_HEAD_DIM).astype(jnp.float32)
    qb = q.reshape(b, nb, WINDOW, ATTN_KV_HEADS, ATTN_Q_PER_KV, ATTN_HEAD_DIM)
    kb = k.reshape(b, nb, WINDOW, ATTN_KV_HEADS, ATTN_HEAD_DIM)
    vb = v.reshape(b, nb, WINDOW, ATTN_KV_HEADS, ATTN_HEAD_DIM)
    pad = ((0, 0), (1, 0), (0, 0), (0, 0), (0, 0))
    kk = jnp.concatenate([jnp.pad(kb, pad)[:, :-1], kb], axis=2)
    vv = jnp.concatenate([jnp.pad(vb, pad)[:, :-1], vb], axis=2)
    scores = jnp.einsum("bnqgrd,bnkgd->bngrqk", qb, kk) * (ATTN_HEAD_DIM ** -0.5)
    qi = jnp.arange(WINDOW)[:, None]
    kj = jnp.arange(2 * WINDOW)[None, :]
    dist = qi + WINDOW - kj
    in_band = (dist >= 0) & (dist < WINDOW)
    key_pos = jnp.arange(nb)[:, None, None] * WINDOW + kj[None] - WINDOW
    valid = in_band[None] & (key_pos >= 0)
    bias = rel_bias.astype(jnp.float32)[t5_causal_bucket(jnp.maximum(dist, 0))]
    bias = bias.transpose(2, 0, 1).reshape(ATTN_KV_HEADS, ATTN_Q_PER_KV, WINDOW, 2 * WINDOW)
    scores = jnp.where(valid[None, :, None, None], scores + bias[None, None], -jnp.inf)
    sink = sinks.astype(jnp.float32).reshape(ATTN_KV_HEADS, ATTN_Q_PER_KV)[None, None, :, :, None, None]
    m = jnp.maximum(jnp.max(scores, axis=-1, keepdims=True), sink)
    p = jnp.exp(scores - m)
    probs = p / (jnp.sum(p, axis=-1, keepdims=True) + jnp.exp(sink - m))
    out = jnp.einsum("bngrqk,bnkgd->bnqgrd", probs, vv)
    out = out.reshape(b, s, ATTN_Q_HEADS * ATTN_HEAD_DIM).astype(h.dtype)
    return out @ out_w


def clamped_swiglu(hb):
    x_glu = jnp.minimum(hb[..., ::2], SWIGLU_LIMIT)
    x_lin = jnp.clip(hb[..., 1::2], -SWIGLU_LIMIT, SWIGLU_LIMIT)
    return x_glu * jax.nn.sigmoid(SWIGLU_ALPHA * x_glu) * (x_lin + 1)


def moe(h, r_w, r_b, w1, b1, w2, b2):
    b, s, d = h.shape
    T = b * s
    xt = h.reshape(T, d)
    logits = (xt @ r_w + r_b).astype(jnp.float32)
    top_val, top_idx = lax.top_k(logits, TOP_K)
    gates = jax.nn.softmax(top_val, axis=-1)
    n_assign = T * TOP_K
    e_flat = top_idx.reshape(-1)
    tok_flat = jnp.arange(n_assign, dtype=jnp.int32) // TOP_K
    order = jnp.argsort(e_flat)
    sorted_e = e_flat[order]
    sorted_tok = tok_flat[order]
    sorted_g = gates.reshape(-1)[order]
    counts = jnp.bincount(e_flat, length=N_EXPERTS)
    padded = ((counts + MOE_BLOCK - 1) // MOE_BLOCK) * MOE_BLOCK
    pend = jnp.cumsum(padded)
    pstart = pend - padded
    ustart = jnp.cumsum(counts) - counts
    pos = pstart[sorted_e] + (jnp.arange(n_assign, dtype=jnp.int32) - ustart[sorted_e])
    cap = n_assign + N_EXPERTS * MOE_BLOCK
    n_blk = cap // MOE_BLOCK
    tok_pad = jnp.full((cap,), T, jnp.int32).at[pos].set(sorted_tok)
    gate_pad = jnp.zeros((cap,), jnp.float32).at[pos].set(sorted_g)
    blk_start = jnp.arange(n_blk, dtype=jnp.int32) * MOE_BLOCK
    blk_e = jnp.minimum(jnp.sum(blk_start[:, None] >= pend[None, :], axis=1), N_EXPERTS - 1)
    x_ext = jnp.concatenate([xt, jnp.zeros((1, d), xt.dtype)], axis=0)
    xg = x_ext[tok_pad].reshape(n_blk, MOE_BLOCK, d)

    def expert_block(args):
        xb, e = args
        hb = clamped_swiglu(xb @ w1[e] + b1[e])
        return hb @ w2[e] + b2[e]

    yg = lax.map(expert_block, (xg, blk_e)).reshape(cap, d)
    yg = yg * gate_pad[:, None].astype(yg.dtype)
    out = jnp.zeros((T + 1, d), yg.dtype).at[tok_pad].add(yg)
    return out[:T].reshape(b, s, d)


def setup_inputs(seed: int = 0) -> dict:
    key = jax.random.key(seed)
    ks = jax.random.split(key, 32)
    f32 = jnp.float32

    def nrm(k, shape, scale):
        return jax.random.normal(k, shape, f32) * scale

    dt0 = jnp.exp(jax.random.uniform(ks[9], (N_SSM_LAYERS, SSM_HEADS), f32,
                                     minval=math.log(1e-3), maxval=math.log(1e-1)))
    return {
        "x": nrm(ks[0], (BATCH, SEQ, D_MODEL), 1.0),
        "c": nrm(ks[1], (BATCH, D_MODEL), 1.0),
        "ada_w": nrm(ks[2], (DEPTH, D_MODEL, 6 * D_MODEL), 0.01),
        "ada_b": nrm(ks[3], (DEPTH, 6 * D_MODEL), 0.01),
        "norm1_g": 1.0 + nrm(ks[4], (DEPTH, D_MODEL), 0.02),
        "norm2_g": 1.0 + nrm(ks[5], (DEPTH, D_MODEL), 0.02),
        "m_in_w": nrm(ks[6], (N_SSM_LAYERS, D_MODEL, SSM_IN_DIM), D_MODEL ** -0.5),
        "m_conv_w": nrm(ks[7], (N_SSM_LAYERS, SSM_CONV, SSM_CONV_DIM), SSM_CONV ** -0.5),
        "m_conv_b": nrm(ks[8], (N_SSM_LAYERS, SSM_CONV_DIM), 0.02),
        "m_dt_bias": dt0 + jnp.log(-jnp.expm1(-dt0)),
        "m_A_log": jnp.log(jax.random.uniform(ks[10], (N_SSM_LAYERS, SSM_HEADS), f32, minval=1.0, maxval=16.0)),
        "m_D": 1.0 + nrm(ks[11], (N_SSM_LAYERS, SSM_HEADS), 0.1),
        "m_norm_g": 1.0 + nrm(ks[12], (N_SSM_LAYERS, SSM_D_INNER), 0.02),
        "m_out_w": nrm(ks[13], (N_SSM_LAYERS, SSM_D_INNER, D_MODEL), SSM_D_INNER ** -0.5),
        "a_qkv_w": nrm(ks[14], (N_ATTN_LAYERS, D_MODEL, ATTN_QKV_DIM), D_MODEL ** -0.5),
        "a_q_norm_g": 1.0 + nrm(ks[15], (N_ATTN_LAYERS, ATTN_HEAD_DIM), 0.02),
        "a_k_norm_g": 1.0 + nrm(ks[16], (N_ATTN_LAYERS, ATTN_HEAD_DIM), 0.02),
        "a_sinks": nrm(ks[17], (N_ATTN_LAYERS, ATTN_Q_HEADS), 0.5),
        "a_out_w": nrm(ks[18], (N_ATTN_LAYERS, ATTN_Q_HEADS * ATTN_HEAD_DIM, D_MODEL),
                        (ATTN_Q_HEADS * ATTN_HEAD_DIM) ** -0.5),
        "rel_bias": nrm(ks[19], (REL_BUCKETS, ATTN_Q_HEADS), 0.5),
        "r_w": nrm(ks[20], (DEPTH, D_MODEL, N_EXPERTS), D_MODEL ** -0.5),
        "r_b": nrm(ks[21], (DEPTH, N_EXPERTS), 0.01),
        "e_w1": nrm(ks[22], (DEPTH, N_EXPERTS, D_MODEL, 2 * EXPERT_FF), D_MODEL ** -0.5),
        "e_b1": nrm(ks[23], (DEPTH, N_EXPERTS, 2 * EXPERT_FF), 0.01),
        "e_w2": nrm(ks[24], (DEPTH, N_EXPERTS, EXPERT_FF, D_MODEL), EXPERT_FF ** -0.5),
        "e_b2": nrm(ks[25], (DEPTH, N_EXPERTS, D_MODEL), 0.01),
    }


def reference(x, c, ada_w, ada_b, norm1_g, norm2_g, m_in_w, m_conv_w, m_conv_b, m_dt_bias, m_A_log, m_D,
              m_norm_g, m_out_w, a_qkv_w, a_q_norm_g, a_k_norm_g, a_sinks, a_out_w, rel_bias,
              r_w, r_b, e_w1, e_b1, e_w2, e_b2):
    c_act = jax.nn.silu(c)
    for i in range(DEPTH):
        mod = c_act @ ada_w[i] + ada_b[i]
        sh1, sc1, g1, sh2, sc2, g2 = jnp.split(mod, 6, axis=-1)
        h = modulate(x, norm1_g[i], sh1, sc1)
        j = i // N_MIXERS
        if i % N_MIXERS == 0:
            y = mamba2_mixer(h, m_in_w[j], m_conv_w[j], m_conv_b[j], m_dt_bias[j], m_A_log[j], m_D[j],
                             m_norm_g[j], m_out_w[j])
        else:
            y = swa_mixer(h, a_qkv_w[j], a_q_norm_g[j], a_k_norm_g[j], a_sinks[j], rel_bias, a_out_w[j])
        x = x + g1[:, None, :] * y
        h = modulate(x, norm2_g[i], sh2, sc2)
        x = x + g2[:, None, :] * moe(h, r_w[i], r_b[i], e_w1[i], e_b1[i], e_w2[i], e_b2[i])
    return x
```

```python
import functools
import math

import jax
import jax.numpy as jnp
from jax import lax
from jax.experimental import pallas as pl
from jax.experimental.pallas import tpu as pltpu

D_MODEL = 1024
EPS = 1e-6
LANES = 128
SUBLANES = 8
ROW_TILES = D_MODEL // LANES

SSM_D_INNER = 2048
SSM_HEAD_DIM = 64
SSM_HEADS = 32
SSM_GROUPS = 4
SSM_STATE = 128
SSM_CONV = 4
SSM_CHUNK = 128
SSM_GN = SSM_GROUPS * SSM_STATE
SSM_IN_PAD = 5248
SSM_GROUP_W = SSM_D_INNER // SSM_GROUPS

ATTN_HEAD_DIM = 64
ATTN_Q_HEADS = 16
ATTN_KV_HEADS = 4
ATTN_Q_PER_KV = 4
WINDOW = 128
REL_BUCKETS = 32
REL_MAX_DIST = 128

N_EXPERTS = 32
TOP_K = 4
SWIGLU_ALPHA = 1.702
SWIGLU_LIMIT = 7.0
MOE_BLOCK = 256

VMEM_LIMIT = 56 * 1024 * 1024
HI = lax.Precision.HIGHEST
F32 = jnp.float32
BF16 = jnp.bfloat16


def _cparams(*sem):
    return pltpu.CompilerParams(dimension_semantics=sem, vmem_limit_bytes=VMEM_LIMIT)


def _norm_modulate(x, g, shift, scale):
    ms = jnp.mean(x * x, axis=-1, keepdims=True)
    return x * lax.rsqrt(ms + EPS) * g * (1.0 + scale) + shift


def _adaln_kernel(c_ref, w_ref, b_ref, o_ref):
    c = c_ref[...]
    c_act = c * jax.nn.sigmoid(c)
    o_ref[0] = jnp.dot(c_act, w_ref[0], preferred_element_type=F32, precision=HI) + b_ref[0]


def _adaln(c_pad, ada_w, ada_b):
    depth, d, n = ada_w.shape
    tn = 1536
    return pl.pallas_call(
        _adaln_kernel,
        grid=(depth, n // tn),
        in_specs=[
            pl.BlockSpec((SUBLANES, d), lambda i, j: (0, 0)),
            pl.BlockSpec((1, d, tn), lambda i, j: (i, 0, j)),
            pl.BlockSpec((1, 1, tn), lambda i, j: (i, 0, j)),
        ],
        out_specs=pl.BlockSpec((1, SUBLANES, tn), lambda i, j: (i, 0, j)),
        out_shape=jax.ShapeDtypeStruct((depth, SUBLANES, n), F32),
        compiler_params=_cparams("arbitrary", "arbitrary"),
        name="adaln",
    )(c_pad, ada_w, ada_b.reshape(depth, 1, n))


def _norm_matmul_kernel(x_ref, g_ref, sh_ref, sc_ref, w_ref, o_ref):
    h = _norm_modulate(x_ref[...], g_ref[...], sh_ref[0], sc_ref[0])
    o_ref[...] = jnp.dot(h.astype(BF16), w_ref[...], preferred_element_type=F32)


def _norm_matmul(x, g, shift, scale, w_bf16, seq, tm=256):
    t, d = x.shape
    n = w_bf16.shape[1]
    per_b = seq // tm
    return pl.pallas_call(
        _norm_matmul_kernel,
        grid=(t // tm,),
        in_specs=[
            pl.BlockSpec((tm, d), lambda i: (i, 0)),
            pl.BlockSpec((1, d), lambda i: (0, 0)),
            pl.BlockSpec((1, 1, d), lambda i: (i // per_b, 0, 0)),
            pl.BlockSpec((1, 1, d), lambda i: (i // per_b, 0, 0)),
            pl.BlockSpec((d, n), lambda i: (0, 0)),
        ],
        out_specs=pl.BlockSpec((tm, n), lambda i: (i, 0)),
        out_shape=jax.ShapeDtypeStruct((t, n), F32),
        compiler_params=_cparams("arbitrary"),
        name="norm_matmul",
    )(x, g.reshape(1, d), shift, scale, w_bf16)


def _matmul_residual_kernel(y_ref, w_ref, x_ref, gate_ref, o_ref):
    acc = jnp.dot(y_ref[...], w_ref[...], preferred_element_type=F32)
    o_ref[...] = x_ref[...] + gate_ref[0] * acc


def _matmul_residual(y_bf16, w_bf16, x, gate, seq, tm=512):
    t, k = y_bf16.shape
    d = x.shape[1]
    per_b = seq // tm
    return pl.pallas_call(
        _matmul_residual_kernel,
        grid=(t // tm,),
        in_specs=[
            pl.BlockSpec((tm, k), lambda i: (i, 0)),
            pl.BlockSpec((k, d), lambda i: (0, 0)),
            pl.BlockSpec((tm, d), lambda i: (i, 0)),
            pl.BlockSpec((1, 1, d), lambda i: (i // per_b, 0, 0)),
        ],
        out_specs=pl.BlockSpec((tm, d), lambda i: (i, 0)),
        out_shape=jax.ShapeDtypeStruct((t, d), F32),
        compiler_params=_cparams("arbitrary"),
        name="matmul_residual",
    )(y_bf16, w_bf16, x, gate)


def _causal_conv_silu(cur, prev_tail, w, b):
    rows = lax.broadcasted_iota(jnp.int32, (SUBLANES, cur.shape[1]), 0)
    acc = b + w[SSM_CONV - 1:SSM_CONV] * cur
    for d in range(1, SSM_CONV):
        rolled = pltpu.roll(cur, d, axis=0)
        top = jnp.where(rows < d, pltpu.roll(prev_tail, d, axis=0), rolled[0:SUBLANES])
        shifted = jnp.concatenate([top, rolled[SUBLANES:]], axis=0)
        acc = acc + w[SSM_CONV - 1 - d:SSM_CONV - d] * shifted
    return acc * jax.nn.sigmoid(acc)


def _ssd_kernel(z_ref, xs_ref, bc_ref, dt_ref, cwx_ref, cbx_ref, cwb_ref, cbb_ref, dtb_ref, alog_ref,
                dskip_ref, ng_ref, hexp_ref, lexp_ref, o_ref, tailx_ref, tailb_ref, state_ref):
    c = pl.program_id(1)

    @pl.when(c == 0)
    def _():
        tailx_ref[...] = jnp.zeros_like(tailx_ref)
        tailb_ref[...] = jnp.zeros_like(tailb_ref)
        state_ref[...] = jnp.zeros_like(state_ref)

    xs_raw = xs_ref[...]
    bc_raw = bc_ref[...]
    xs = _causal_conv_silu(xs_raw, tailx_ref[...], cwx_ref[...], cbx_ref[...])
    bc = _causal_conv_silu(bc_raw, tailb_ref[...], cwb_ref[...], cbb_ref[...])
    tailx_ref[...] = xs_raw[SSM_CHUNK - SUBLANES:]
    tailb_ref[...] = bc_raw[SSM_CHUNK - SUBLANES:]

    dt_in = dt_ref[...][:, :SSM_HEADS] + dtb_ref[...]
    dt = jnp.maximum(dt_in, 0.0) + jnp.log1p(jnp.exp(-jnp.abs(dt_in)))
    a_neg = -jnp.exp(alog_ref[...])
    d_a = dt * a_neg
    li = lax.broadcasted_iota(jnp.int32, (SSM_CHUNK, SSM_CHUNK), 0)
    si = lax.broadcasted_iota(jnp.int32, (SSM_CHUNK, SSM_CHUNK), 1)
    causal = li >= si
    tri = causal.astype(F32)
    a_cs = jnp.dot(tri, d_a, preferred_element_type=F32, precision=HI)
    a_cs_t = lax.dot_general(d_a, tri, (((0,), (1,)), ((), ())),
                             preferred_element_type=F32, precision=HI)
    a_last = a_cs[SSM_CHUNK - 1:SSM_CHUNK]
    e_out = jnp.exp(a_cs)
    e_state = jnp.exp(a_last - a_cs) * dt
    small = jnp.concatenate([dt, e_out, e_state], axis=0)
    wide = jnp.dot(small, hexp_ref[...], preferred_element_type=F32, precision=HI)
    dt_w = wide[0:SSM_CHUNK]
    e_out_w = wide[SSM_CHUNK:2 * SSM_CHUNK]
    e_state_w = wide[2 * SSM_CHUNK:]
    a_col = jnp.dot(a_cs, lexp_ref[...], preferred_element_type=F32, precision=HI)

    x_dt = (xs * dt_w).astype(BF16)
    x_state = (xs * e_state_w).astype(BF16)
    chunk_decay_w = e_out_w[SSM_CHUNK - 1:SSM_CHUNK]

    heads_per_group = SSM_HEADS // SSM_GROUPS
    y_parts = []
    for g in range(SSM_GROUPS):
        b_g = bc[:, g * SSM_STATE:(g + 1) * SSM_STATE].astype(BF16)
        c_g = bc[:, SSM_GN + g * SSM_STATE:SSM_GN + (g + 1) * SSM_STATE].astype(BF16)
        cb = lax.dot_general(c_g, b_g, (((1,), (1,)), ((), ())), preferred_element_type=F32)
        gsl = slice(g * SSM_GROUP_W, (g + 1) * SSM_GROUP_W)
        h_prev = state_ref[g]
        y_off = jnp.dot(c_g, h_prev.astype(BF16), preferred_element_type=F32) * e_out_w[:, gsl]
        diag = []
        for r in range(heads_per_group):
            h = g * heads_per_group + r
            seg = a_col[:, h * SSM_CHUNK:(h + 1) * SSM_CHUNK] - a_cs_t[h:h + 1, :]
            decay = jnp.exp(jnp.where(causal, seg, -jnp.inf))
            m = (cb * decay).astype(BF16)
            diag.append(jnp.dot(m, x_dt[:, h * SSM_HEAD_DIM:(h + 1) * SSM_HEAD_DIM],
                                preferred_element_type=F32))
        y_parts.append(jnp.concatenate(diag, axis=1) + y_off)
        upd = lax.dot_general(b_g, x_state[:, gsl], (((0,), (0,)), ((), ())), preferred_element_type=F32)
        state_ref[g] = h_prev * chunk_decay_w[:, gsl] + upd

    y = jnp.concatenate(y_parts, axis=1) + dskip_ref[...] * xs
    z = z_ref[...]
    y = y * (z * jax.nn.sigmoid(z))
    normed = []
    for g in range(SSM_GROUPS):
        y_g = y[:, g * SSM_GROUP_W:(g + 1) * SSM_GROUP_W]
        normed.append(y_g * lax.rsqrt(jnp.mean(y_g * y_g, axis=-1, keepdims=True) + EPS))
    o_ref[...] = (jnp.concatenate(normed, axis=1) * ng_ref[...]).astype(o_ref.dtype)


def _ssd_mixer(zxbcdt, conv_w, conv_b, dt_bias, a_log, d_skip, norm_g, batch, seq):
    t = zxbcdt.shape[0]
    nc = seq // SSM_CHUNK
    head_expand = jnp.repeat(jnp.eye(SSM_HEADS, dtype=F32), SSM_HEAD_DIM, axis=1)
    lane_expand = jnp.repeat(jnp.eye(SSM_HEADS, dtype=F32), SSM_CHUNK, axis=1)
    row = lambda b, c: (b * nc + c, 0)
    const2 = lambda b, c: (0, 0)
    bc_w = 2 * SSM_GN
    return pl.pallas_call(
        _ssd_kernel,
        grid=(batch, nc),
        in_specs=[
            pl.BlockSpec((SSM_CHUNK, SSM_D_INNER), row),
            pl.BlockSpec((SSM_CHUNK, SSM_D_INNER), lambda b, c: (b * nc + c, 1)),
            pl.BlockSpec((SSM_CHUNK, bc_w), lambda b, c: (b * nc + c, 2 * SSM_D_INNER // bc_w)),
            pl.BlockSpec((SSM_CHUNK, LANES), lambda b, c: (b * nc + c, (2 * SSM_D_INNER + bc_w) // LANES)),
            pl.BlockSpec((SSM_CONV, SSM_D_INNER), const2),
            pl.BlockSpec((1, SSM_D_INNER), const2),
            pl.BlockSpec((SSM_CONV, bc_w), const2),
            pl.BlockSpec((1, bc_w), const2),
            pl.BlockSpec((1, SSM_HEADS), const2),
            pl.BlockSpec((1, SSM_HEADS), const2),
            pl.BlockSpec((1, SSM_D_INNER), const2),
            pl.BlockSpec((1, SSM_D_INNER), const2),
            pl.BlockSpec((SSM_HEADS, SSM_D_INNER), const2),
            pl.BlockSpec((SSM_HEADS, SSM_HEADS * SSM_CHUNK), const2),
        ],
        out_specs=pl.BlockSpec((SSM_CHUNK, SSM_D_INNER), row),
        out_shape=jax.ShapeDtypeStruct((t, SSM_D_INNER), BF16),
        scratch_shapes=[
            pltpu.VMEM((SUBLANES, SSM_D_INNER), F32),
            pltpu.VMEM((SUBLANES, bc_w), F32),
            pltpu.VMEM((SSM_GROUPS, SSM_STATE, SSM_GROUP_W), F32),
        ],
        compiler_params=_cparams("arbitrary", "arbitrary"),
        name="ssd_mixer",
    )(zxbcdt, zxbcdt, zxbcdt, zxbcdt,
      conv_w[:, :SSM_D_INNER], conv_b[:SSM_D_INNER].reshape(1, -1),
      conv_w[:, SSM_D_INNER:], conv_b[SSM_D_INNER:].reshape(1, -1),
      dt_bias.reshape(1, -1), a_log.reshape(1, -1),
      jnp.repeat(d_skip, SSM_HEAD_DIM).reshape(1, -1), norm_g.reshape(1, -1),
      head_expand, lane_expand)


def _head_rms(x, g):
    return x * lax.rsqrt(jnp.mean(x * x, axis=-1, keepdims=True) + EPS) * g


def _swa_kernel(q_ref, kvc_ref, kvp_ref, bucket_ref, qg_ref, kg_ref, rel_ref, sink_ref, o_ref, bias_ref):
    b = pl.program_id(0)
    i = pl.program_id(1)

    @pl.when((b == 0) & (i == 0))
    def _():
        bucket = bucket_ref[...]
        for h in range(ATTN_Q_HEADS):
            acc = jnp.zeros(bucket.shape, F32)
            for k in range(REL_BUCKETS):
                acc = jnp.where(bucket == k, rel_ref[k, h], acc)
            bias_ref[h] = acc

    qi = lax.broadcasted_iota(jnp.int32, (WINDOW, 2 * WINDOW), 0)
    kj = lax.broadcasted_iota(jnp.int32, (WINDOW, 2 * WINDOW), 1)
    dist = qi + WINDOW - kj
    valid = (dist >= 0) & (dist < WINDOW) & ((kj >= WINDOW) | (i > 0))

    q_all = q_ref[...]
    kv_c = kvc_ref[...]
    kv_p = kvp_ref[...]
    kv_w = ATTN_KV_HEADS * ATTN_HEAD_DIM
    outs = []
    for g in range(ATTN_KV_HEADS):
        ksl = slice(g * ATTN_HEAD_DIM, (g + 1) * ATTN_HEAD_DIM)
        vsl = slice(kv_w + g * ATTN_HEAD_DIM, kv_w + (g + 1) * ATTN_HEAD_DIM)
        k = jnp.concatenate([kv_p[:, ksl], kv_c[:, ksl]], axis=0)
        v = jnp.concatenate([kv_p[:, vsl], kv_c[:, vsl]], axis=0).astype(BF16)
        k = _head_rms(k, kg_ref[...]).astype(BF16)
        for r in range(ATTN_Q_PER_KV):
            h = g * ATTN_Q_PER_KV + r
            q = _head_rms(q_all[:, h * ATTN_HEAD_DIM:(h + 1) * ATTN_HEAD_DIM], qg_ref[...])
            s = lax.dot_general(q.astype(BF16), k, (((1,), (1,)), ((), ())), preferred_element_type=F32)
            s = jnp.where(valid, s * (ATTN_HEAD_DIM ** -0.5) + bias_ref[h], -jnp.inf)
            sink = sink_ref[h]
            m = jnp.maximum(jnp.max(s, axis=-1, keepdims=True), sink)
            p = jnp.exp(s - m)
            denom = jnp.sum(p, axis=-1, keepdims=True) + jnp.exp(sink - m)
            pv = jnp.dot(p.astype(BF16), v, preferred_element_type=F32)
            outs.append(pv / denom)
    o_ref[...] = jnp.concatenate(outs, axis=1).astype(o_ref.dtype)


def _t5_causal_bucket(dist):
    max_exact = REL_BUCKETS // 2
    d = jnp.maximum(dist, 1).astype(F32)
    large = max_exact + (jnp.log(d / max_exact) / math.log(REL_MAX_DIST / max_exact)
                         * (REL_BUCKETS - max_exact)).astype(jnp.int32)
    large = jnp.minimum(large, REL_BUCKETS - 1)
    return jnp.where(dist < max_exact, dist, large)


def _swa_mixer(qkv, q_norm_g, k_norm_g, sinks, rel_bias, batch, seq):
    t = qkv.shape[0]
    nb = seq // WINDOW
    q_w = ATTN_Q_HEADS * ATTN_HEAD_DIM
    kv_w2 = 2 * ATTN_KV_HEADS * ATTN_HEAD_DIM
    qi = jnp.arange(WINDOW)[:, None]
    kj = jnp.arange(2 * WINDOW)[None, :]
    bucket = _t5_causal_bucket(jnp.maximum(qi + WINDOW - kj, 0)).astype(jnp.int32)
    const2 = lambda b, i: (0, 0)
    smem = pl.BlockSpec(memory_space=pltpu.SMEM)
    return pl.pallas_call(
        _swa_kernel,
        grid=(batch, nb),
        in_specs=[
            pl.BlockSpec((WINDOW, q_w), lambda b, i: (b * nb + i, 0)),
            pl.BlockSpec((WINDOW, kv_w2), lambda b, i: (b * nb + i, q_w // kv_w2)),
            pl.BlockSpec((WINDOW, kv_w2), lambda b, i: (b * nb + jnp.maximum(i - 1, 0), q_w // kv_w2)),
            pl.BlockSpec((WINDOW, 2 * WINDOW), const2),
            pl.BlockSpec((1, ATTN_HEAD_DIM), const2),
            pl.BlockSpec((1, ATTN_HEAD_DIM), const2),
            smem,
            smem,
        ],
        out_specs=pl.BlockSpec((WINDOW, q_w), lambda b, i: (b * nb + i, 0)),
        out_shape=jax.ShapeDtypeStruct((t, q_w), BF16),
        scratch_shapes=[pltpu.VMEM((ATTN_Q_HEADS, WINDOW, 2 * WINDOW), F32)],
        compiler_params=_cparams("arbitrary", "arbitrary"),
        name="swa_mixer",
    )(qkv, qkv, qkv, bucket, q_norm_g.reshape(1, -1), k_norm_g.reshape(1, -1), rel_bias, sinks)


def _router_kernel(x_ref, g_ref, sh_ref, sc_ref, rwt_ref, rb_ref,
                   hrow_ref, idx_ref, gate_ref, rank_ref, cnt_ref, carry_ref):
    i = pl.program_id(0)
    tm = x_ref.shape[0]

    @pl.when(i == 0)
    def _():
        carry_ref[...] = jnp.zeros_like(carry_ref)

    h = _norm_modulate(x_ref[...], g_ref[...], sh_ref[0], sc_ref[0])
    for s in range(ROW_TILES):
        hrow_ref[pl.ds(s, tm, stride=ROW_TILES), :] = h[:, s * LANES:(s + 1) * LANES]

    logits = lax.dot_general(rwt_ref[...], h, (((1,), (1,)), ((), ())),
                             preferred_element_type=F32, precision=HI) + rb_ref[...]
    e_iota = lax.broadcasted_iota(jnp.int32, logits.shape, 0)
    work = logits
    sels, vals = [], []
    for k in range(TOP_K):
        m = jnp.max(work, axis=0, keepdims=True)
        idx = jnp.min(jnp.where(work == m, e_iota, N_EXPERTS), axis=0, keepdims=True)
        sel = e_iota == idx
        work = jnp.where(sel, -jnp.inf, work)
        idx_ref[k:k + 1, :] = idx
        sels.append(sel)
        vals.append(m)
    exps = [jnp.exp(v - vals[0]) for v in vals]
    denom = exps[0] + exps[1] + exps[2] + exps[3]
    for k in range(TOP_K):
        gate_ref[k:k + 1, :] = exps[k] / denom

    chosen = sels[0] | sels[1] | sels[2] | sels[3]
    t_row = lax.broadcasted_iota(jnp.int32, (tm, tm), 0)
    t_col = lax.broadcasted_iota(jnp.int32, (tm, tm), 1)
    before = (t_row < t_col).astype(BF16)
    prior = jnp.dot(chosen.astype(BF16), before, preferred_element_type=F32)
    running = carry_ref[...] + prior
    for k in range(TOP_K):
        rank_ref[k:k + 1, :] = jnp.sum(jnp.where(sels[k], running, 0.0), axis=0, keepdims=True).astype(jnp.int32)
    total = carry_ref[...] + jnp.sum(chosen.astype(F32), axis=1, keepdims=True)
    carry_ref[...] = total
    cnt_ref[...] = total.astype(jnp.int32)


def _router(x, g, shift, scale, r_w, r_b, seq, tm=256):
    t, d = x.shape
    per_b = seq // tm
    return pl.pallas_call(
        _router_kernel,
        grid=(t // tm,),
        in_specs=[
            pl.BlockSpec((tm, d), lambda i: (i, 0)),
            pl.BlockSpec((1, d), lambda i: (0, 0)),
            pl.BlockSpec((1, 1, d), lambda i: (i // per_b, 0, 0)),
            pl.BlockSpec((1, 1, d), lambda i: (i // per_b, 0, 0)),
            pl.BlockSpec((N_EXPERTS, d), lambda i: (0, 0)),
            pl.BlockSpec((N_EXPERTS, 1), lambda i: (0, 0)),
        ],
        out_specs=[
            pl.BlockSpec((tm * ROW_TILES, LANES), lambda i: (i, 0)),
            pl.BlockSpec((TOP_K, tm), lambda i: (0, i)),
            pl.BlockSpec((TOP_K, tm), lambda i: (0, i)),
            pl.BlockSpec((TOP_K, tm), lambda i: (0, i)),
            pl.BlockSpec((N_EXPERTS, 1), lambda i: (0, 0)),
        ],
        out_shape=[
            jax.ShapeDtypeStruct((t * ROW_TILES, LANES), F32),
            jax.ShapeDtypeStruct((TOP_K, t), jnp.int32),
            jax.ShapeDtypeStruct((TOP_K, t), F32),
            jax.ShapeDtypeStruct((TOP_K, t), jnp.int32),
            jax.ShapeDtypeStruct((N_EXPERTS, 1), jnp.int32),
        ],
        scratch_shapes=[pltpu.VMEM((N_EXPERTS, 1), F32)],
        compiler_params=_cparams("arbitrary"),
        name="moe_router",
    )(x, g.reshape(1, d), shift, scale, r_w.T, r_b.reshape(-1, 1))


def _slot_kernel(start_ref, idx_ref, rank_ref, pos_ref):
    idx = idx_ref[...]
    acc = rank_ref[...]
    for e in range(N_EXPERTS):
        acc = acc + jnp.where(idx == e, start_ref[e], 0)
    pos_ref[...] = acc


def _slots(group_start, idx_t, rank_t):
    return pl.pallas_call(
        _slot_kernel,
        in_specs=[pl.BlockSpec(memory_space=pltpu.SMEM),
                  pl.BlockSpec(memory_space=pltpu.VMEM),
                  pl.BlockSpec(memory_space=pltpu.VMEM)],
        out_specs=pl.BlockSpec(memory_space=pltpu.VMEM),
        out_shape=jax.ShapeDtypeStruct(idx_t.shape, jnp.int32),
        name="moe_slots",
    )(group_start, idx_t, rank_t)


def _row_copy(src_ref, src_row, dst_ref, dst_row, sem):
    return pltpu.make_async_copy(
        src_ref.at[pl.ds(pl.multiple_of(src_row * ROW_TILES, ROW_TILES), ROW_TILES)],
        dst_ref.at[pl.ds(pl.multiple_of(dst_row * ROW_TILES, ROW_TILES), ROW_TILES)],
        sem)


def _block_copy(src_ref, dst_ref, dst_blk, sem):
    rows = MOE_BLOCK * ROW_TILES
    return pltpu.make_async_copy(src_ref, dst_ref.at[pl.ds(pl.multiple_of(dst_blk * rows, rows), rows)], sem)


def _dispatch_kernel(pad_lo_ref, pad_hi_ref, nused_ref, pos_ref, hrow_ref, xg_ref, zero_ref, sem):
    i = pl.program_id(0)
    tm = pos_ref.shape[1]
    nblk = xg_ref.shape[0] // (MOE_BLOCK * ROW_TILES)

    @pl.when(i == 0)
    def _():
        zero_ref[...] = jnp.zeros_like(zero_ref)

        def fill(b, carry):
            _block_copy(zero_ref, xg_ref, b, sem).start()
            return carry

        def fill_wait(b, carry):
            _block_copy(zero_ref, xg_ref, b, sem).wait()
            return carry

        lax.fori_loop(nused_ref[0], nblk, fill, 0)
        lax.fori_loop(nused_ref[0], nblk, fill_wait, 0)

    @pl.when(i < N_EXPERTS)
    def _():
        def pad(r, carry):
            _row_copy(zero_ref, 0, xg_ref, r, sem).start()
            return carry

        def pad_wait(r, carry):
            _row_copy(zero_ref, 0, xg_ref, r, sem).wait()
            return carry

        lax.fori_loop(pad_lo_ref[i], pad_hi_ref[i], pad, 0)
        lax.fori_loop(pad_lo_ref[i], pad_hi_ref[i], pad_wait, 0)

    def issue(t, carry):
        for k in range(TOP_K):
            _row_copy(hrow_ref, i * tm + t, xg_ref, pos_ref[k, t], sem).start()
        return carry

    lax.fori_loop(0, tm, issue, 0)

    def drain(t, carry):
        for k in range(TOP_K):
            _row_copy(hrow_ref, i * tm + t, xg_ref, pos_ref[k, t], sem).wait()
        return carry

    lax.fori_loop(0, tm, drain, 0)


def _dispatch(pad_lo, pad_hi, nused, pos_t, h_rows, cap, tm=256):
    t = pos_t.shape[1]
    assert t // tm >= N_EXPERTS
    grid_spec = pltpu.PrefetchScalarGridSpec(
        num_scalar_prefetch=3,
        grid=(t // tm,),
        in_specs=[
            pl.BlockSpec((TOP_K, tm), lambda i, *_: (0, i), memory_space=pltpu.SMEM),
            pl.BlockSpec(memory_space=pl.ANY),
        ],
        out_specs=pl.BlockSpec(memory_space=pl.ANY),
        scratch_shapes=[pltpu.VMEM((MOE_BLOCK * ROW_TILES, LANES), F32), pltpu.SemaphoreType.DMA],
    )
    return pl.pallas_call(
        _dispatch_kernel,
        grid_spec=grid_spec,
        out_shape=jax.ShapeDtypeStruct((cap * ROW_TILES, LANES), F32),
        compiler_params=pltpu.CompilerParams(dimension_semantics=("arbitrary",), has_side_effects=True),
        name="moe_dispatch",
    )(pad_lo, pad_hi, nused, pos_t, h_rows)


def _expert_kernel(blk_e_ref, nused_ref, x_ref, wg_ref, wl_ref, w2_ref, bg_ref, bl_ref, b2_ref, o_ref):
    blk = pl.program_id(0)

    @pl.when(blk < nused_ref[0])
    def _():
        x = jnp.concatenate(
            [x_ref[pl.ds(s, MOE_BLOCK, stride=ROW_TILES), :] for s in range(ROW_TILES)], axis=1).astype(BF16)
        glu = jnp.dot(x, wg_ref[0], preferred_element_type=F32) + bg_ref[0]
        lin = jnp.dot(x, wl_ref[0], preferred_element_type=F32) + bl_ref[0]
        glu = jnp.minimum(glu, SWIGLU_LIMIT)
        lin = jnp.clip(lin, -SWIGLU_LIMIT, SWIGLU_LIMIT)
        act = glu * jax.nn.sigmoid(SWIGLU_ALPHA * glu) * (lin + 1.0)
        y = jnp.dot(act.astype(BF16), w2_ref[0], preferred_element_type=F32) + b2_ref[0]
        for s in range(ROW_TILES):
            o_ref[pl.ds(s, MOE_BLOCK, stride=ROW_TILES), :] = y[:, s * LANES:(s + 1) * LANES]

    @pl.when(blk >= nused_ref[0])
    def _():
        o_ref[...] = jnp.zeros_like(o_ref)


def _experts(blk_e, nused, xg_rows, w_glu, w_lin, w2, b_glu, b_lin, b2):
    nblk = blk_e.shape[0]
    d = D_MODEL
    ff = w_glu.shape[2]

    def blk_map(b, blk_e_ref, nused_ref):
        return (jnp.minimum(b, nused_ref[0] - 1), 0)

    def e_map(b, blk_e_ref, nused_ref):
        return (blk_e_ref[jnp.minimum(b, nused_ref[0] - 1)], 0, 0)

    grid_spec = pltpu.PrefetchScalarGridSpec(
        num_scalar_prefetch=2,
        grid=(nblk,),
        in_specs=[
            pl.BlockSpec((MOE_BLOCK * ROW_TILES, LANES), blk_map),
            pl.BlockSpec((1, d, ff), e_map),
            pl.BlockSpec((1, d, ff), e_map),
            pl.BlockSpec((1, ff, d), e_map),
            pl.BlockSpec((1, 1, ff), e_map),
            pl.BlockSpec((1, 1, ff), e_map),
            pl.BlockSpec((1, 1, d), e_map),
        ],
        out_specs=pl.BlockSpec((MOE_BLOCK * ROW_TILES, LANES), lambda b, *_: (b, 0)),
    )
    return pl.pallas_call(
        _expert_kernel,
        grid_spec=grid_spec,
        out_shape=jax.ShapeDtypeStruct(xg_rows.shape, F32),
        compiler_params=_cparams("arbitrary"),
        name="moe_experts",
    )(blk_e, nused, xg_rows, w_glu, w_lin, w2, b_glu, b_lin, b2)


def _combine_kernel(pos_ref, yg_ref, gate_ref, x_ref, g2_ref, o_ref, buf_ref, sem):
    tm = x_ref.shape[0]

    def issue(t, carry):
        for k in range(TOP_K):
            _row_copy(yg_ref, pos_ref[k, t], buf_ref.at[k], t, sem).start()
        return carry

    lax.fori_loop(0, tm, issue, 0)

    eye = (lax.broadcasted_iota(jnp.int32, (tm, tm), 0) ==
           lax.broadcasted_iota(jnp.int32, (tm, tm), 1)).astype(F32)
    gate_cols = lax.dot_general(eye, gate_ref[...], (((1,), (1,)), ((), ())),
                                preferred_element_type=F32, precision=HI)

    def drain(t, carry):
        for k in range(TOP_K):
            _row_copy(yg_ref, pos_ref[k, t], buf_ref.at[k], t, sem).wait()
        return carry

    lax.fori_loop(0, tm, drain, 0)

    g2 = g2_ref[0]
    for s in range(ROW_TILES):
        lanes = slice(s * LANES, (s + 1) * LANES)
        acc = jnp.zeros((tm, LANES), F32)
        for k in range(TOP_K):
            acc = acc + gate_cols[:, k:k + 1] * buf_ref[k, pl.ds(s, tm, stride=ROW_TILES), :]
        o_ref[:, lanes] = x_ref[:, lanes] + g2[:, lanes] * acc


def _combine(pos_t, yg_rows, gate_t, x, gate2, seq, tm=128):
    t, d = x.shape
    per_b = seq // tm
    return pl.pallas_call(
        _combine_kernel,
        grid=(t // tm,),
        in_specs=[
            pl.BlockSpec((TOP_K, tm), lambda i: (0, i), memory_space=pltpu.SMEM),
            pl.BlockSpec(memory_space=pl.ANY),
            pl.BlockSpec((TOP_K, tm), lambda i: (0, i)),
            pl.BlockSpec((tm, d), lambda i: (i, 0)),
            pl.BlockSpec((1, 1, d), lambda i: (i // per_b, 0, 0)),
        ],
        out_specs=pl.BlockSpec((tm, d), lambda i: (i, 0)),
        out_shape=jax.ShapeDtypeStruct((t, d), F32),
        scratch_shapes=[pltpu.VMEM((TOP_K, tm * ROW_TILES, LANES), F32), pltpu.SemaphoreType.DMA],
        compiler_params=_cparams("arbitrary"),
        name="moe_combine",
    )(pos_t, yg_rows, gate_t, x, gate2)


def _moe(x, g, shift, scale, gate2, r_w, r_b, w1, b1, w2, b2, seq):
    t = x.shape[0]
    h_rows, idx_t, gate_t, rank_t, counts = _router(x, g, shift, scale, r_w, r_b, seq)
    counts = counts.reshape(-1)
    padded = ((counts + MOE_BLOCK - 1) // MOE_BLOCK) * MOE_BLOCK
    group_end = jnp.cumsum(padded)
    group_start = (group_end - padded).astype(jnp.int32)
    cap = t * TOP_K + N_EXPERTS * MOE_BLOCK
    nblk = cap // MOE_BLOCK
    blk_start = jnp.arange(nblk, dtype=jnp.int32) * MOE_BLOCK
    blk_e = jnp.minimum(jnp.sum(blk_start[:, None] >= group_end[None, :], axis=1), N_EXPERTS - 1).astype(jnp.int32)
    nused = (group_end[-1:] // MOE_BLOCK).astype(jnp.int32)

    pos_t = _slots(group_start, idx_t, rank_t)
    pad_lo = (group_start + counts).astype(jnp.int32)
    pad_hi = group_end.astype(jnp.int32)
    xg_rows = _dispatch(pad_lo, pad_hi, nused, pos_t, h_rows, cap)
    yg_rows = _experts(blk_e, nused, xg_rows,
                       w1[:, :, 0::2].astype(BF16), w1[:, :, 1::2].astype(BF16), w2.astype(BF16),
                       b1[:, None, 0::2], b1[:, None, 1::2], b2[:, None, :])
    return _combine(pos_t, yg_rows, gate_t, x, gate2, seq)


def kernel(x, c, ada_w, ada_b, norm1_g, norm2_g, m_in_w, m_conv_w, m_conv_b, m_dt_bias, m_A_log, m_D, m_norm_g, m_out_w, a_qkv_w, a_q_norm_g, a_k_norm_g, a_sinks, a_out_w, rel_bias, r_w, r_b, e_w1, e_b1, e_w2, e_b2):
    batch, seq, d = x.shape
    depth = ada_w.shape[0]
    t = batch * seq
    xf = x.reshape(t, d)

    c_pad = jnp.zeros((SUBLANES, d), F32).at[:batch].set(c)
    mod = _adaln(c_pad, ada_w, ada_b)[:, :batch]

    for i in range(depth):
        parts = [mod[i, :, p * d:(p + 1) * d].reshape(batch, 1, d) for p in range(6)]
        sh1, sc1, g1, sh2, sc2, g2 = parts
        j = i // 2
        if i % 2 == 0:
            w_in = jnp.pad(m_in_w[j], ((0, 0), (0, SSM_IN_PAD - m_in_w.shape[2]))).astype(BF16)
            zxbcdt = _norm_matmul(xf, norm1_g[i], sh1, sc1, w_in, seq)
            y = _ssd_mixer(zxbcdt, m_conv_w[j], m_conv_b[j], m_dt_bias[j], m_A_log[j], m_D[j],
                           m_norm_g[j], batch, seq)
            xf = _matmul_residual(y, m_out_w[j].astype(BF16), xf, g1, seq)
        else:
            qkv = _norm_matmul(xf, norm1_g[i], sh1, sc1, a_qkv_w[j].astype(BF16), seq)
            y = _swa_mixer(qkv, a_q_norm_g[j], a_k_norm_g[j], a_sinks[j], rel_bias, batch, seq)
            xf = _matmul_residual(y, a_out_w[j].astype(BF16), xf, g1, seq)
        xf = _moe(xf, norm2_g[i], sh2, sc2, g2, r_w[i], r_b[i], e_w1[i], e_b1[i], e_w2[i], e_b2[i], seq)
    return xf.reshape(batch, seq, d)
```

```python
import functools
import math

import jax
import jax.numpy as jnp
from jax import lax
from jax.experimental import pallas as pl
from jax.experimental.pallas import tpu as pltpu

D_MODEL = 1024
EPS = 1e-6
LANES = 128
SUBLANES = 8
ROW_TILES = D_MODEL // LANES

SSM_D_INNER = 2048
SSM_HEAD_DIM = 64
SSM_HEADS = 32
SSM_GROUPS = 4
SSM_STATE = 128
SSM_CONV = 4
SSM_CHUNK = 128
SSM_GN = SSM_GROUPS * SSM_STATE
SSM_IN_PAD = 5248
SSM_GROUP_W = SSM_D_INNER // SSM_GROUPS

ATTN_HEAD_DIM = 64
ATTN_Q_HEADS = 16
ATTN_KV_HEADS = 4
ATTN_Q_PER_KV = 4
WINDOW = 128
REL_BUCKETS = 32
REL_MAX_DIST = 128

N_EXPERTS = 32
TOP_K = 4
SWIGLU_ALPHA = 1.702
SWIGLU_LIMIT = 7.0
MOE_BLOCK = 256
SPLIT_W = 256

VMEM_LIMIT = 56 * 1024 * 1024
HI = lax.Precision.HIGHEST
F32 = jnp.float32
BF16 = jnp.bfloat16


def _cparams(*sem):
    return pltpu.CompilerParams(dimension_semantics=sem, vmem_limit_bytes=VMEM_LIMIT)


def _norm_modulate(x, g, shift, scale):
    ms = jnp.mean(x * x, axis=-1, keepdims=True)
    return x * lax.rsqrt(ms + EPS) * g * (1.0 + scale) + shift


def _adaln_kernel(c_ref, w_ref, b_ref, o_ref):
    c = c_ref[...]
    c_act = c * jax.nn.sigmoid(c)
    o_ref[0] = jnp.dot(c_act, w_ref[0], preferred_element_type=F32, precision=HI) + b_ref[0]


def _adaln(c_pad, ada_w, ada_b):
    depth, d, n = ada_w.shape
    tn = 1536
    return pl.pallas_call(
        _adaln_kernel,
        grid=(depth, n // tn),
        in_specs=[
            pl.BlockSpec((SUBLANES, d), lambda i, j: (0, 0)),
            pl.BlockSpec((1, d, tn), lambda i, j: (i, 0, j)),
            pl.BlockSpec((1, 1, tn), lambda i, j: (i, 0, j)),
        ],
        out_specs=pl.BlockSpec((1, SUBLANES, tn), lambda i, j: (i, 0, j)),
        out_shape=jax.ShapeDtypeStruct((depth, SUBLANES, n), F32),
        compiler_params=_cparams("arbitrary", "arbitrary"),
        name="adaln",
    )(c_pad, ada_w, ada_b.reshape(depth, 1, n))


def _norm_matmul_kernel(x_ref, g_ref, sh_ref, sc_ref, w_ref, o_ref):
    h = _norm_modulate(x_ref[...], g_ref[...], sh_ref[0], sc_ref[0])
    o_ref[...] = jnp.dot(h.astype(BF16), w_ref[...], preferred_element_type=F32)


def _norm_matmul(x, g, shift, scale, w_bf16, seq, tm=256):
    t, d = x.shape
    n = w_bf16.shape[1]
    per_b = seq // tm
    return pl.pallas_call(
        _norm_matmul_kernel,
        grid=(t // tm,),
        in_specs=[
            pl.BlockSpec((tm, d), lambda i: (i, 0)),
            pl.BlockSpec((1, d), lambda i: (0, 0)),
            pl.BlockSpec((1, 1, d), lambda i: (i // per_b, 0, 0)),
            pl.BlockSpec((1, 1, d), lambda i: (i // per_b, 0, 0)),
            pl.BlockSpec((d, n), lambda i: (0, 0)),
        ],
        out_specs=pl.BlockSpec((tm, n), lambda i: (i, 0)),
        out_shape=jax.ShapeDtypeStruct((t, n), F32),
        compiler_params=_cparams("arbitrary"),
        name="norm_matmul",
    )(x, g.reshape(1, d), shift, scale, w_bf16)


def _matmul_residual_kernel(y_ref, w_ref, x_ref, gate_ref, o_ref):
    acc = jnp.dot(y_ref[...], w_ref[...], preferred_element_type=F32)
    o_ref[...] = x_ref[...] + gate_ref[0] * acc


def _matmul_residual(y_bf16, w_bf16, x, gate, seq, tm=512):
    t, k = y_bf16.shape
    d = x.shape[1]
    per_b = seq // tm
    return pl.pallas_call(
        _matmul_residual_kernel,
        grid=(t // tm,),
        in_specs=[
            pl.BlockSpec((tm, k), lambda i: (i, 0)),
            pl.BlockSpec((k, d), lambda i: (0, 0)),
            pl.BlockSpec((tm, d), lambda i: (i, 0)),
            pl.BlockSpec((1, 1, d), lambda i: (i // per_b, 0, 0)),
        ],
        out_specs=pl.BlockSpec((tm, d), lambda i: (i, 0)),
        out_shape=jax.ShapeDtypeStruct((t, d), F32),
        compiler_params=_cparams("arbitrary"),
        name="matmul_residual",
    )(y_bf16, w_bf16, x, gate)


def _causal_conv_silu(cur, prev_tail, w, b):
    rows = lax.broadcasted_iota(jnp.int32, (SUBLANES, cur.shape[1]), 0)
    acc = b + w[SSM_CONV - 1:SSM_CONV] * cur
    for d in range(1, SSM_CONV):
        rolled = pltpu.roll(cur, d, axis=0)
        top = jnp.where(rows < d, pltpu.roll(prev_tail, d, axis=0), rolled[0:SUBLANES])
        shifted = jnp.concatenate([top, rolled[SUBLANES:]], axis=0)
        acc = acc + w[SSM_CONV - 1 - d:SSM_CONV - d] * shifted
    return acc * jax.nn.sigmoid(acc)


def _ssd_kernel(z_ref, xs_ref, bc_ref, dt_ref, cwx_ref, cbx_ref, cwb_ref, cbb_ref, dtb_ref, alog_ref,
                dskip_ref, ng_ref, hexp_ref, lexp_ref, o_ref, tailx_ref, tailb_ref, state_ref):
    c = pl.program_id(1)

    @pl.when(c == 0)
    def _():
        tailx_ref[...] = jnp.zeros_like(tailx_ref)
        tailb_ref[...] = jnp.zeros_like(tailb_ref)
        state_ref[...] = jnp.zeros_like(state_ref)

    xs_raw = xs_ref[...]
    bc_raw = bc_ref[...]
    xs = _causal_conv_silu(xs_raw, tailx_ref[...], cwx_ref[...], cbx_ref[...])
    bc = _causal_conv_silu(bc_raw, tailb_ref[...], cwb_ref[...], cbb_ref[...])
    tailx_ref[...] = xs_raw[SSM_CHUNK - SUBLANES:]
    tailb_ref[...] = bc_raw[SSM_CHUNK - SUBLANES:]

    dt_in = dt_ref[...][:, :SSM_HEADS] + dtb_ref[...]
    dt = jnp.maximum(dt_in, 0.0) + jnp.log1p(jnp.exp(-jnp.abs(dt_in)))
    a_neg = -jnp.exp(alog_ref[...])
    d_a = dt * a_neg
    li = lax.broadcasted_iota(jnp.int32, (SSM_CHUNK, SSM_CHUNK), 0)
    si = lax.broadcasted_iota(jnp.int32, (SSM_CHUNK, SSM_CHUNK), 1)
    causal = li >= si
    tri = causal.astype(F32)
    a_cs = jnp.dot(tri, d_a, preferred_element_type=F32, precision=HI)
    a_cs_t = lax.dot_general(d_a, tri, (((0,), (1,)), ((), ())),
                             preferred_element_type=F32, precision=HI)
    a_last = a_cs[SSM_CHUNK - 1:SSM_CHUNK]
    e_out = jnp.exp(a_cs)
    e_state = jnp.exp(a_last - a_cs) * dt
    small = jnp.concatenate([dt, e_out, e_state], axis=0)
    wide = jnp.dot(small, hexp_ref[...], preferred_element_type=F32, precision=HI)
    dt_w = wide[0:SSM_CHUNK]
    e_out_w = wide[SSM_CHUNK:2 * SSM_CHUNK]
    e_state_w = wide[2 * SSM_CHUNK:]
    a_col = jnp.dot(a_cs, lexp_ref[...], preferred_element_type=F32, precision=HI)

    x_dt = (xs * dt_w).astype(BF16)
    x_state = (xs * e_state_w).astype(BF16)
    chunk_decay_w = e_out_w[SSM_CHUNK - 1:SSM_CHUNK]

    heads_per_group = SSM_HEADS // SSM_GROUPS
    y_parts = []
    for g in range(SSM_GROUPS):
        b_g = bc[:, g * SSM_STATE:(g + 1) * SSM_STATE].astype(BF16)
        c_g = bc[:, SSM_GN + g * SSM_STATE:SSM_GN + (g + 1) * SSM_STATE].astype(BF16)
        cb = lax.dot_general(c_g, b_g, (((1,), (1,)), ((), ())), preferred_element_type=F32)
        gsl = slice(g * SSM_GROUP_W, (g + 1) * SSM_GROUP_W)
        h_prev = state_ref[g]
        y_off = jnp.dot(c_g, h_prev.astype(BF16), preferred_element_type=F32) * e_out_w[:, gsl]
        diag = []
        for r in range(heads_per_group):
            h = g * heads_per_group + r
            seg = a_col[:, h * SSM_CHUNK:(h + 1) * SSM_CHUNK] - a_cs_t[h:h + 1, :]
            decay = jnp.exp(jnp.where(causal, seg, -jnp.inf))
            m = (cb * decay).astype(BF16)
            diag.append(jnp.dot(m, x_dt[:, h * SSM_HEAD_DIM:(h + 1) * SSM_HEAD_DIM],
                                preferred_element_type=F32))
        y_parts.append(jnp.concatenate(diag, axis=1) + y_off)
        upd = lax.dot_general(b_g, x_state[:, gsl], (((0,), (0,)), ((), ())), preferred_element_type=F32)
        state_ref[g] = h_prev * chunk_decay_w[:, gsl] + upd

    y = jnp.concatenate(y_parts, axis=1) + dskip_ref[...] * xs
    z = z_ref[...]
    y = y * (z * jax.nn.sigmoid(z))
    normed = []
    for g in range(SSM_GROUPS):
        y_g = y[:, g * SSM_GROUP_W:(g + 1) * SSM_GROUP_W]
        normed.append(y_g * lax.rsqrt(jnp.mean(y_g * y_g, axis=-1, keepdims=True) + EPS))
    o_ref[...] = (jnp.concatenate(normed, axis=1) * ng_ref[...]).astype(o_ref.dtype)


def _ssd_mixer(zxbcdt, conv_w, conv_b, dt_bias, a_log, d_skip, norm_g, batch, seq):
    t = zxbcdt.shape[0]
    nc = seq // SSM_CHUNK
    head_expand = jnp.repeat(jnp.eye(SSM_HEADS, dtype=F32), SSM_HEAD_DIM, axis=1)
    lane_expand = jnp.repeat(jnp.eye(SSM_HEADS, dtype=F32), SSM_CHUNK, axis=1)
    row = lambda b, c: (b * nc + c, 0)
    const2 = lambda b, c: (0, 0)
    bc_w = 2 * SSM_GN
    return pl.pallas_call(
        _ssd_kernel,
        grid=(batch, nc),
        in_specs=[
            pl.BlockSpec((SSM_CHUNK, SSM_D_INNER), row),
            pl.BlockSpec((SSM_CHUNK, SSM_D_INNER), lambda b, c: (b * nc + c, 1)),
            pl.BlockSpec((SSM_CHUNK, bc_w), lambda b, c: (b * nc + c, 2 * SSM_D_INNER // bc_w)),
            pl.BlockSpec((SSM_CHUNK, LANES), lambda b, c: (b * nc + c, (2 * SSM_D_INNER + bc_w) // LANES)),
            pl.BlockSpec((SSM_CONV, SSM_D_INNER), const2),
            pl.BlockSpec((1, SSM_D_INNER), const2),
            pl.BlockSpec((SSM_CONV, bc_w), const2),
            pl.BlockSpec((1, bc_w), const2),
            pl.BlockSpec((1, SSM_HEADS), const2),
            pl.BlockSpec((1, SSM_HEADS), const2),
            pl.BlockSpec((1, SSM_D_INNER), const2),
            pl.BlockSpec((1, SSM_D_INNER), const2),
            pl.BlockSpec((SSM_HEADS, SSM_D_INNER), const2),
            pl.BlockSpec((SSM_HEADS, SSM_HEADS * SSM_CHUNK), const2),
        ],
        out_specs=pl.BlockSpec((SSM_CHUNK, SSM_D_INNER), row),
        out_shape=jax.ShapeDtypeStruct((t, SSM_D_INNER), BF16),
        scratch_shapes=[
            pltpu.VMEM((SUBLANES, SSM_D_INNER), F32),
            pltpu.VMEM((SUBLANES, bc_w), F32),
            pltpu.VMEM((SSM_GROUPS, SSM_STATE, SSM_GROUP_W), F32),
        ],
        compiler_params=_cparams("arbitrary", "arbitrary"),
        name="ssd_mixer",
    )(zxbcdt, zxbcdt, zxbcdt, zxbcdt,
      conv_w[:, :SSM_D_INNER], conv_b[:SSM_D_INNER].reshape(1, -1),
      conv_w[:, SSM_D_INNER:], conv_b[SSM_D_INNER:].reshape(1, -1),
      dt_bias.reshape(1, -1), a_log.reshape(1, -1),
      jnp.repeat(d_skip, SSM_HEAD_DIM).reshape(1, -1), norm_g.reshape(1, -1),
      head_expand, lane_expand)


def _head_rms(x, g):
    return x * lax.rsqrt(jnp.mean(x * x, axis=-1, keepdims=True) + EPS) * g


def _swa_kernel(q_ref, kvc_ref, kvp_ref, bucket_ref, qg_ref, kg_ref, rel_ref, sink_ref, o_ref, bias_ref):
    b = pl.program_id(0)
    i = pl.program_id(1)

    @pl.when((b == 0) & (i == 0))
    def _():
        bucket = bucket_ref[...]
        for h in range(ATTN_Q_HEADS):
            acc = jnp.zeros(bucket.shape, F32)
            for k in range(REL_BUCKETS):
                acc = jnp.where(bucket == k, rel_ref[k, h], acc)
            bias_ref[h] = acc

    qi = lax.broadcasted_iota(jnp.int32, (WINDOW, 2 * WINDOW), 0)
    kj = lax.broadcasted_iota(jnp.int32, (WINDOW, 2 * WINDOW), 1)
    dist = qi + WINDOW - kj
    valid = (dist >= 0) & (dist < WINDOW) & ((kj >= WINDOW) | (i > 0))

    q_all = q_ref[...]
    kv_c = kvc_ref[...]
    kv_p = kvp_ref[...]
    kv_w = ATTN_KV_HEADS * ATTN_HEAD_DIM
    outs = []
    for g in range(ATTN_KV_HEADS):
        ksl = slice(g * ATTN_HEAD_DIM, (g + 1) * ATTN_HEAD_DIM)
        vsl = slice(kv_w + g * ATTN_HEAD_DIM, kv_w + (g + 1) * ATTN_HEAD_DIM)
        k = jnp.concatenate([kv_p[:, ksl], kv_c[:, ksl]], axis=0)
        v = jnp.concatenate([kv_p[:, vsl], kv_c[:, vsl]], axis=0).astype(BF16)
        k = _head_rms(k, kg_ref[...]).astype(BF16)
        for r in range(ATTN_Q_PER_KV):
            h = g * ATTN_Q_PER_KV + r
            q = _head_rms(q_all[:, h * ATTN_HEAD_DIM:(h + 1) * ATTN_HEAD_DIM], qg_ref[...])
            s = lax.dot_general(q.astype(BF16), k, (((1,), (1,)), ((), ())), preferred_element_type=F32)
            s = jnp.where(valid, s * (ATTN_HEAD_DIM ** -0.5) + bias_ref[h], -jnp.inf)
            sink = sink_ref[h]
            m = jnp.maximum(jnp.max(s, axis=-1, keepdims=True), sink)
            p = jnp.exp(s - m)
            denom = jnp.sum(p, axis=-1, keepdims=True) + jnp.exp(sink - m)
            pv = jnp.dot(p.astype(BF16), v, preferred_element_type=F32)
            outs.append(pv / denom)
    o_ref[...] = jnp.concatenate(outs, axis=1).astype(o_ref.dtype)


def _t5_causal_bucket(dist):
    max_exact = REL_BUCKETS // 2
    d = jnp.maximum(dist, 1).astype(F32)
    large = max_exact + (jnp.log(d / max_exact) / math.log(REL_MAX_DIST / max_exact)
                         * (REL_BUCKETS - max_exact)).astype(jnp.int32)
    large = jnp.minimum(large, REL_BUCKETS - 1)
    return jnp.where(dist < max_exact, dist, large)


def _swa_mixer(qkv, q_norm_g, k_norm_g, sinks, rel_bias, batch, seq):
    t = qkv.shape[0]
    nb = seq // WINDOW
    q_w = ATTN_Q_HEADS * ATTN_HEAD_DIM
    kv_w2 = 2 * ATTN_KV_HEADS * ATTN_HEAD_DIM
    qi = jnp.arange(WINDOW)[:, None]
    kj = jnp.arange(2 * WINDOW)[None, :]
    bucket = _t5_causal_bucket(jnp.maximum(qi + WINDOW - kj, 0)).astype(jnp.int32)
    const2 = lambda b, i: (0, 0)
    smem = pl.BlockSpec(memory_space=pltpu.SMEM)
    return pl.pallas_call(
        _swa_kernel,
        grid=(batch, nb),
        in_specs=[
            pl.BlockSpec((WINDOW, q_w), lambda b, i: (b * nb + i, 0)),
            pl.BlockSpec((WINDOW, kv_w2), lambda b, i: (b * nb + i, q_w // kv_w2)),
            pl.BlockSpec((WINDOW, kv_w2), lambda b, i: (b * nb + jnp.maximum(i - 1, 0), q_w // kv_w2)),
            pl.BlockSpec((WINDOW, 2 * WINDOW), const2),
            pl.BlockSpec((1, ATTN_HEAD_DIM), const2),
            pl.BlockSpec((1, ATTN_HEAD_DIM), const2),
            smem,
            smem,
        ],
        out_specs=pl.BlockSpec((WINDOW, q_w), lambda b, i: (b * nb + i, 0)),
        out_shape=jax.ShapeDtypeStruct((t, q_w), BF16),
        scratch_shapes=[pltpu.VMEM((ATTN_Q_HEADS, WINDOW, 2 * WINDOW), F32)],
        compiler_params=_cparams("arbitrary", "arbitrary"),
        name="swa_mixer",
    )(qkv, qkv, qkv, bucket, q_norm_g.reshape(1, -1), k_norm_g.reshape(1, -1), rel_bias, sinks)


def _router_kernel(x_ref, g_ref, sh_ref, sc_ref, rwt_ref, rb_ref,
                   hrow_ref, idx_ref, gate_ref, rank_ref, cnt_ref, carry_ref):
    i = pl.program_id(0)
    tm = x_ref.shape[0]

    @pl.when(i == 0)
    def _():
        carry_ref[...] = jnp.zeros_like(carry_ref)

    h = _norm_modulate(x_ref[...], g_ref[...], sh_ref[0], sc_ref[0])
    for s in range(ROW_TILES):
        hrow_ref[pl.ds(s, tm, stride=ROW_TILES), :] = h[:, s * LANES:(s + 1) * LANES]

    logits = lax.dot_general(rwt_ref[...], h, (((1,), (1,)), ((), ())),
                             preferred_element_type=F32, precision=HI) + rb_ref[...]
    e_iota = lax.broadcasted_iota(jnp.int32, logits.shape, 0)
    work = logits
    sels, vals = [], []
    for k in range(TOP_K):
        m = jnp.max(work, axis=0, keepdims=True)
        idx = jnp.min(jnp.where(work == m, e_iota, N_EXPERTS), axis=0, keepdims=True)
        sel = e_iota == idx
        work = jnp.where(sel, -jnp.inf, work)
        idx_ref[k:k + 1, :] = idx
        sels.append(sel)
        vals.append(m)
    exps = [jnp.exp(v - vals[0]) for v in vals]
    denom = exps[0] + exps[1] + exps[2] + exps[3]
    for k in range(TOP_K):
        gate_ref[k:k + 1, :] = exps[k] / denom

    chosen = sels[0] | sels[1] | sels[2] | sels[3]
    t_row = lax.broadcasted_iota(jnp.int32, (tm, tm), 0)
    t_col = lax.broadcasted_iota(jnp.int32, (tm, tm), 1)
    before = (t_row < t_col).astype(BF16)
    prior = jnp.dot(chosen.astype(BF16), before, preferred_element_type=F32)
    running = carry_ref[...] + prior
    for k in range(TOP_K):
        rank_ref[k:k + 1, :] = jnp.sum(jnp.where(sels[k], running, 0.0), axis=0, keepdims=True).astype(jnp.int32)
    total = carry_ref[...] + jnp.sum(chosen.astype(F32), axis=1, keepdims=True)
    carry_ref[...] = total
    cnt_ref[...] = total.astype(jnp.int32)


def _router(x, g, shift, scale, r_w, r_b, seq, tm=256):
    t, d = x.shape
    per_b = seq // tm
    return pl.pallas_call(
        _router_kernel,
        grid=(t // tm,),
        in_specs=[
            pl.BlockSpec((tm, d), lambda i: (i, 0)),
            pl.BlockSpec((1, d), lambda i: (0, 0)),
            pl.BlockSpec((1, 1, d), lambda i: (i // per_b, 0, 0)),
            pl.BlockSpec((1, 1, d), lambda i: (i // per_b, 0, 0)),
            pl.BlockSpec((N_EXPERTS, d), lambda i: (0, 0)),
            pl.BlockSpec((N_EXPERTS, 1), lambda i: (0, 0)),
        ],
        out_specs=[
            pl.BlockSpec((tm * ROW_TILES, LANES), lambda i: (i, 0)),
            pl.BlockSpec((TOP_K, tm), lambda i: (0, i)),
            pl.BlockSpec((TOP_K, tm), lambda i: (0, i)),
            pl.BlockSpec((TOP_K, tm), lambda i: (0, i)),
            pl.BlockSpec((N_EXPERTS, 1), lambda i: (0, 0)),
        ],
        out_shape=[
            jax.ShapeDtypeStruct((t * ROW_TILES, LANES), F32),
            jax.ShapeDtypeStruct((TOP_K, t), jnp.int32),
            jax.ShapeDtypeStruct((TOP_K, t), F32),
            jax.ShapeDtypeStruct((TOP_K, t), jnp.int32),
            jax.ShapeDtypeStruct((N_EXPERTS, 1), jnp.int32),
        ],
        scratch_shapes=[pltpu.VMEM((N_EXPERTS, 1), F32)],
        compiler_params=_cparams("arbitrary"),
        name="moe_router",
    )(x, g.reshape(1, d), shift, scale, r_w.T, r_b.reshape(-1, 1))


def _slot_kernel(start_ref, idx_ref, rank_ref, pos_ref):
    idx = idx_ref[...]
    acc = rank_ref[...]
    for e in range(N_EXPERTS):
        acc = acc + jnp.where(idx == e, start_ref[e], 0)
    pos_ref[...] = acc


def _slots(group_start, idx_t, rank_t):
    return pl.pallas_call(
        _slot_kernel,
        in_specs=[pl.BlockSpec(memory_space=pltpu.SMEM),
                  pl.BlockSpec(memory_space=pltpu.VMEM),
                  pl.BlockSpec(memory_space=pltpu.VMEM)],
        out_specs=pl.BlockSpec(memory_space=pltpu.VMEM),
        out_shape=jax.ShapeDtypeStruct(idx_t.shape, jnp.int32),
        name="moe_slots",
    )(group_start, idx_t, rank_t)


def _row_copy(src_ref, src_row, dst_ref, dst_row, sem):
    return pltpu.make_async_copy(
        src_ref.at[pl.ds(pl.multiple_of(src_row * ROW_TILES, ROW_TILES), ROW_TILES)],
        dst_ref.at[pl.ds(pl.multiple_of(dst_row * ROW_TILES, ROW_TILES), ROW_TILES)],
        sem)


def _block_copy(src_ref, dst_ref, dst_blk, sem):
    rows = MOE_BLOCK * ROW_TILES
    return pltpu.make_async_copy(src_ref, dst_ref.at[pl.ds(pl.multiple_of(dst_blk * rows, rows), rows)], sem)


def _dispatch_kernel(pad_lo_ref, pad_hi_ref, nused_ref, pos_ref, hrow_ref, xg_ref, zero_ref, sem):
    i = pl.program_id(0)
    tm = pos_ref.shape[1]
    nblk = xg_ref.shape[0] // (MOE_BLOCK * ROW_TILES)

    @pl.when(i == 0)
    def _():
        zero_ref[...] = jnp.zeros_like(zero_ref)

        def fill(b, carry):
            _block_copy(zero_ref, xg_ref, b, sem).start()
            return carry

        def fill_wait(b, carry):
            _block_copy(zero_ref, xg_ref, b, sem).wait()
            return carry

        lax.fori_loop(nused_ref[0], nblk, fill, 0)
        lax.fori_loop(nused_ref[0], nblk, fill_wait, 0)

    @pl.when(i < N_EXPERTS)
    def _():
        def pad(r, carry):
            _row_copy(zero_ref, 0, xg_ref, r, sem).start()
            return carry

        def pad_wait(r, carry):
            _row_copy(zero_ref, 0, xg_ref, r, sem).wait()
            return carry

        lax.fori_loop(pad_lo_ref[i], pad_hi_ref[i], pad, 0)
        lax.fori_loop(pad_lo_ref[i], pad_hi_ref[i], pad_wait, 0)

    def issue(t, carry):
        for k in range(TOP_K):
            _row_copy(hrow_ref, t, xg_ref, pos_ref[k, t], sem).start()
        return carry

    lax.fori_loop(0, tm, issue, 0)

    def drain(t, carry):
        for k in range(TOP_K):
            _row_copy(hrow_ref, t, xg_ref, pos_ref[k, t], sem).wait()
        return carry

    lax.fori_loop(0, tm, drain, 0)


def _dispatch(pad_lo, pad_hi, nused, pos_t, h_rows, cap, tm=256):
    t = pos_t.shape[1]
    assert t // tm >= N_EXPERTS
    grid_spec = pltpu.PrefetchScalarGridSpec(
        num_scalar_prefetch=3,
        grid=(t // tm,),
        in_specs=[
            pl.BlockSpec((TOP_K, tm), lambda i, *_: (0, i), memory_space=pltpu.SMEM),
            pl.BlockSpec((tm * ROW_TILES, LANES), lambda i, *_: (i, 0)),
        ],
        out_specs=pl.BlockSpec(memory_space=pl.ANY),
        scratch_shapes=[pltpu.VMEM((MOE_BLOCK * ROW_TILES, LANES), F32), pltpu.SemaphoreType.DMA],
    )
    return pl.pallas_call(
        _dispatch_kernel,
        grid_spec=grid_spec,
        out_shape=jax.ShapeDtypeStruct((cap * ROW_TILES, LANES), F32),
        compiler_params=pltpu.CompilerParams(dimension_semantics=("arbitrary",), has_side_effects=True),
        name="moe_dispatch",
    )(pad_lo, pad_hi, nused, pos_t, h_rows)


def _expert_kernel(blk_e_ref, nused_ref, x_ref, w1_ref, w2_ref, bg_ref, bl_ref, b2_ref, sel_ref, o_ref,
                   wg_s, wl_s, w2_s):
    blk = pl.program_id(0)
    active = blk < nused_ref[0]
    new_expert = (blk == 0) | (blk_e_ref[blk] != blk_e_ref[jnp.maximum(blk - 1, 0)])

    @pl.when(active & new_expert)
    def _():
        for j in range(w1_ref.shape[2] // SPLIT_W):
            chunk = w1_ref[0, :, j * SPLIT_W:(j + 1) * SPLIT_W].astype(BF16)
            split = jnp.dot(chunk, sel_ref[...], preferred_element_type=F32)
            half = SPLIT_W // 2
            wg_s[:, j * half:(j + 1) * half] = split[:, :half].astype(BF16)
            wl_s[:, j * half:(j + 1) * half] = split[:, half:].astype(BF16)
        w2_s[...] = w2_ref[0].astype(BF16)

    @pl.when(active)
    def _():
        x = jnp.concatenate(
            [x_ref[pl.ds(s, MOE_BLOCK, stride=ROW_TILES), :] for s in range(ROW_TILES)], axis=1).astype(BF16)
        glu = jnp.dot(x, wg_s[...], preferred_element_type=F32) + bg_ref[0]
        lin = jnp.dot(x, wl_s[...], preferred_element_type=F32) + bl_ref[0]
        glu = jnp.minimum(glu, SWIGLU_LIMIT)
        lin = jnp.clip(lin, -SWIGLU_LIMIT, SWIGLU_LIMIT)
        act = glu * jax.nn.sigmoid(SWIGLU_ALPHA * glu) * (lin + 1.0)
        y = jnp.dot(act.astype(BF16), w2_s[...], preferred_element_type=F32) + b2_ref[0]
        for s in range(ROW_TILES):
            o_ref[pl.ds(s, MOE_BLOCK, stride=ROW_TILES), :] = y[:, s * LANES:(s + 1) * LANES]

    @pl.when(jnp.logical_not(active))
    def _():
        o_ref[...] = jnp.zeros_like(o_ref)


def _experts(blk_e, nused, xg_rows, w1, w2, b_glu, b_lin, b2):
    nblk = blk_e.shape[0]
    d = D_MODEL
    ff = w2.shape[1]
    col = jnp.arange(SPLIT_W)
    sel = (jnp.arange(SPLIT_W)[None, :] == ((col % 2) * (SPLIT_W // 2) + col // 2)[:, None]).astype(BF16)

    def blk_map(b, blk_e_ref, nused_ref):
        return (jnp.minimum(b, nused_ref[0] - 1), 0)

    def e_map(b, blk_e_ref, nused_ref):
        return (blk_e_ref[jnp.minimum(b, nused_ref[0] - 1)], 0, 0)

    grid_spec = pltpu.PrefetchScalarGridSpec(
        num_scalar_prefetch=2,
        grid=(nblk,),
        in_specs=[
            pl.BlockSpec((MOE_BLOCK * ROW_TILES, LANES), blk_map),
            pl.BlockSpec((1, d, 2 * ff), e_map),
            pl.BlockSpec((1, ff, d), e_map),
            pl.BlockSpec((1, 1, ff), e_map),
            pl.BlockSpec((1, 1, ff), e_map),
            pl.BlockSpec((1, 1, d), e_map),
            pl.BlockSpec((SPLIT_W, SPLIT_W), lambda b, *_: (0, 0)),
        ],
        out_specs=pl.BlockSpec((MOE_BLOCK * ROW_TILES, LANES), lambda b, *_: (b, 0)),
        scratch_shapes=[pltpu.VMEM((d, ff), BF16), pltpu.VMEM((d, ff), BF16), pltpu.VMEM((ff, d), BF16)],
    )
    return pl.pallas_call(
        _expert_kernel,
        grid_spec=grid_spec,
        out_shape=jax.ShapeDtypeStruct(xg_rows.shape, F32),
        compiler_params=_cparams("arbitrary"),
        name="moe_experts",
    )(blk_e, nused, xg_rows, w1, w2, b_glu, b_lin, b2, sel)


def _combine_kernel(pos_ref, yg_ref, gate_ref, x_ref, g2_ref, o_ref, buf_ref, sem):
    tm = x_ref.shape[0]

    def issue(t, carry):
        for k in range(TOP_K):
            _row_copy(yg_ref, pos_ref[k, t], buf_ref.at[k], t, sem).start()
        return carry

    lax.fori_loop(0, tm, issue, 0)

    eye = (lax.broadcasted_iota(jnp.int32, (tm, tm), 0) ==
           lax.broadcasted_iota(jnp.int32, (tm, tm), 1)).astype(F32)
    gate_cols = lax.dot_general(eye, gate_ref[...], (((1,), (1,)), ((), ())),
                                preferred_element_type=F32, precision=HI)

    def drain(t, carry):
        for k in range(TOP_K):
            _row_copy(yg_ref, pos_ref[k, t], buf_ref.at[k], t, sem).wait()
        return carry

    lax.fori_loop(0, tm, drain, 0)

    g2 = g2_ref[0]
    for s in range(ROW_TILES):
        lanes = slice(s * LANES, (s + 1) * LANES)
        acc = jnp.zeros((tm, LANES), F32)
        for k in range(TOP_K):
            acc = acc + gate_cols[:, k:k + 1] * buf_ref[k, pl.ds(s, tm, stride=ROW_TILES), :]
        o_ref[:, lanes] = x_ref[:, lanes] + g2[:, lanes] * acc


def _combine(pos_t, yg_rows, gate_t, x, gate2, seq, tm=128):
    t, d = x.shape
    per_b = seq // tm
    return pl.pallas_call(
        _combine_kernel,
        grid=(t // tm,),
        in_specs=[
            pl.BlockSpec((TOP_K, tm), lambda i: (0, i), memory_space=pltpu.SMEM),
            pl.BlockSpec(memory_space=pl.ANY),
            pl.BlockSpec((TOP_K, tm), lambda i: (0, i)),
            pl.BlockSpec((tm, d), lambda i: (i, 0)),
            pl.BlockSpec((1, 1, d), lambda i: (i // per_b, 0, 0)),
        ],
        out_specs=pl.BlockSpec((tm, d), lambda i: (i, 0)),
        out_shape=jax.ShapeDtypeStruct((t, d), F32),
        scratch_shapes=[pltpu.VMEM((TOP_K, tm * ROW_TILES, LANES), F32), pltpu.SemaphoreType.DMA],
        compiler_params=_cparams("arbitrary"),
        name="moe_combine",
    )(pos_t, yg_rows, gate_t, x, gate2)


def _moe(x, g, shift, scale, gate2, r_w, r_b, w1, b1, w2, b2, seq):
    t = x.shape[0]
    h_rows, idx_t, gate_t, rank_t, counts = _router(x, g, shift, scale, r_w, r_b, seq)
    counts = counts.reshape(-1)
    padded = ((counts + MOE_BLOCK - 1) // MOE_BLOCK) * MOE_BLOCK
    group_end = jnp.cumsum(padded)
    group_start = (group_end - padded).astype(jnp.int32)
    cap = t * TOP_K + N_EXPERTS * MOE_BLOCK
    nblk = cap // MOE_BLOCK
    blk_start = jnp.arange(nblk, dtype=jnp.int32) * MOE_BLOCK
    blk_e = jnp.minimum(jnp.sum(blk_start[:, None] >= group_end[None, :], axis=1), N_EXPERTS - 1).astype(jnp.int32)
    nused = (group_end[-1:] // MOE_BLOCK).astype(jnp.int32)

    pos_t = _slots(group_start, idx_t, rank_t)
    pad_lo = (group_start + counts).astype(jnp.int32)
    pad_hi = group_end.astype(jnp.int32)
    xg_rows = _dispatch(pad_lo, pad_hi, nused, pos_t, h_rows, cap)
    yg_rows = _experts(blk_e, nused, xg_rows, w1, w2, b1[:, None, 0::2], b1[:, None, 1::2], b2[:, None, :])
    return _combine(pos_t, yg_rows, gate_t, x, gate2, seq)


def kernel(x, c, ada_w, ada_b, norm1_g, norm2_g, m_in_w, m_conv_w, m_conv_b, m_dt_bias, m_A_log, m_D, m_norm_g, m_out_w, a_qkv_w, a_q_norm_g, a_k_norm_g, a_sinks, a_out_w, rel_bias, r_w, r_b, e_w1, e_b1, e_w2, e_b2):
    batch, seq, d = x.shape
    depth = ada_w.shape[0]
    t = batch * seq
    xf = x.reshape(t, d)

    c_pad = jnp.zeros((SUBLANES, d), F32).at[:batch].set(c)
    mod = _adaln(c_pad, ada_w, ada_b)[:, :batch]

    for i in range(depth):
        parts = [mod[i, :, p * d:(p + 1) * d].reshape(batch, 1, d) for p in range(6)]
        sh1, sc1, g1, sh2, sc2, g2 = parts
        j = i // 2
        if i % 2 == 0:
            w_in = jnp.pad(m_in_w[j], ((0, 0), (0, SSM_IN_PAD - m_in_w.shape[2]))).astype(BF16)
            zxbcdt = _norm_matmul(xf, norm1_g[i], sh1, sc1, w_in, seq)
            y = _ssd_mixer(zxbcdt, m_conv_w[j], m_conv_b[j], m_dt_bias[j], m_A_log[j], m_D[j],
                           m_norm_g[j], batch, seq)
            xf = _matmul_residual(y, m_out_w[j].astype(BF16), xf, g1, seq)
        else:
            qkv = _norm_matmul(xf, norm1_g[i], sh1, sc1, a_qkv_w[j].astype(BF16), seq)
            y = _swa_mixer(qkv, a_q_norm_g[j], a_k_norm_g[j], a_sinks[j], rel_bias, batch, seq)
            xf = _matmul_residual(y, a_out_w[j].astype(BF16), xf, g1, seq)
        xf = _moe(xf, norm2_g[i], sh2, sc2, g2, r_w[i], r_b[i], e_w1[i], e_b1[i], e_w2[i], e_b2[i], seq)
    return xf.reshape(batch, seq, d)
```

```python
import functools
import math

import jax
import jax.numpy as jnp
from jax import lax
from jax.experimental import pallas as pl
from jax.experimental.pallas import tpu as pltpu

D_MODEL = 1024
EPS = 1e-6
LANES = 128
SUBLANES = 8
ROW_TILES = D_MODEL // LANES

SSM_D_INNER = 2048
SSM_HEAD_DIM = 64
SSM_HEADS = 32
SSM_GROUPS = 4
SSM_STATE = 128
SSM_CONV = 4
SSM_CHUNK = 128
SSM_GN = SSM_GROUPS * SSM_STATE
SSM_IN_PAD = 5248
SSM_GROUP_W = SSM_D_INNER // SSM_GROUPS

ATTN_HEAD_DIM = 64
ATTN_Q_HEADS = 16
ATTN_KV_HEADS = 4
ATTN_Q_PER_KV = 4
WINDOW = 128
REL_BUCKETS = 32
REL_MAX_DIST = 128

N_EXPERTS = 32
TOP_K = 4
SWIGLU_ALPHA = 1.702
SWIGLU_LIMIT = 7.0
MOE_BLOCK = 256
SPLIT_W = 256

VMEM_LIMIT = 56 * 1024 * 1024
HI = lax.Precision.HIGHEST
F32 = jnp.float32
BF16 = jnp.bfloat16


def _cparams(*sem):
    return pltpu.CompilerParams(dimension_semantics=sem, vmem_limit_bytes=VMEM_LIMIT)


def _norm_modulate(x, g, shift, scale):
    ms = jnp.mean(x * x, axis=-1, keepdims=True)
    return x * lax.rsqrt(ms + EPS) * g * (1.0 + scale) + shift


def _adaln_kernel(c_ref, w_ref, b_ref, o_ref):
    c = c_ref[...]
    c_act = c * jax.nn.sigmoid(c)
    o_ref[0] = jnp.dot(c_act, w_ref[0], preferred_element_type=F32, precision=HI) + b_ref[0]


def _adaln(c_pad, ada_w, ada_b):
    depth, d, n = ada_w.shape
    tn = 1536
    return pl.pallas_call(
        _adaln_kernel,
        grid=(depth, n // tn),
        in_specs=[
            pl.BlockSpec((SUBLANES, d), lambda i, j: (0, 0)),
            pl.BlockSpec((1, d, tn), lambda i, j: (i, 0, j)),
            pl.BlockSpec((1, 1, tn), lambda i, j: (i, 0, j)),
        ],
        out_specs=pl.BlockSpec((1, SUBLANES, tn), lambda i, j: (i, 0, j)),
        out_shape=jax.ShapeDtypeStruct((depth, SUBLANES, n), F32),
        compiler_params=_cparams("arbitrary", "arbitrary"),
        name="adaln",
    )(c_pad, ada_w, ada_b.reshape(depth, 1, n))


def _norm_matmul_kernel(x_ref, g_ref, sh_ref, sc_ref, w_ref, o_ref):
    h = _norm_modulate(x_ref[...], g_ref[...], sh_ref[0], sc_ref[0])
    o_ref[...] = jnp.dot(h.astype(BF16), w_ref[...], preferred_element_type=F32)


def _norm_matmul(x, g, shift, scale, w_bf16, seq, tm=256):
    t, d = x.shape
    n = w_bf16.shape[1]
    per_b = seq // tm
    return pl.pallas_call(
        _norm_matmul_kernel,
        grid=(t // tm,),
        in_specs=[
            pl.BlockSpec((tm, d), lambda i: (i, 0)),
            pl.BlockSpec((1, d), lambda i: (0, 0)),
            pl.BlockSpec((1, 1, d), lambda i: (i // per_b, 0, 0)),
            pl.BlockSpec((1, 1, d), lambda i: (i // per_b, 0, 0)),
            pl.BlockSpec((d, n), lambda i: (0, 0)),
        ],
        out_specs=pl.BlockSpec((tm, n), lambda i: (i, 0)),
        out_shape=jax.ShapeDtypeStruct((t, n), F32),
        compiler_params=_cparams("arbitrary"),
        name="norm_matmul",
    )(x, g.reshape(1, d), shift, scale, w_bf16)


def _matmul_residual_kernel(y_ref, w_ref, x_ref, gate_ref, o_ref):
    acc = jnp.dot(y_ref[...], w_ref[...], preferred_element_type=F32)
    o_ref[...] = x_ref[...] + gate_ref[0] * acc


def _matmul_residual(y_bf16, w_bf16, x, gate, seq, tm=512):
    t, k = y_bf16.shape
    d = x.shape[1]
    per_b = seq // tm
    return pl.pallas_call(
        _matmul_residual_kernel,
        grid=(t // tm,),
        in_specs=[
            pl.BlockSpec((tm, k), lambda i: (i, 0)),
            pl.BlockSpec((k, d), lambda i: (0, 0)),
            pl.BlockSpec((tm, d), lambda i: (i, 0)),
            pl.BlockSpec((1, 1, d), lambda i: (i // per_b, 0, 0)),
        ],
        out_specs=pl.BlockSpec((tm, d), lambda i: (i, 0)),
        out_shape=jax.ShapeDtypeStruct((t, d), F32),
        compiler_params=_cparams("arbitrary"),
        name="matmul_residual",
    )(y_bf16, w_bf16, x, gate)


def _causal_conv_silu(cur, prev_tail, w, b):
    rows = lax.broadcasted_iota(jnp.int32, (SUBLANES, cur.shape[1]), 0)
    acc = b + w[SSM_CONV - 1:SSM_CONV] * cur
    for d in range(1, SSM_CONV):
        rolled = pltpu.roll(cur, d, axis=0)
        top = jnp.where(rows < d, pltpu.roll(prev_tail, d, axis=0), rolled[0:SUBLANES])
        shifted = jnp.concatenate([top, rolled[SUBLANES:]], axis=0)
        acc = acc + w[SSM_CONV - 1 - d:SSM_CONV - d] * shifted
    return acc * jax.nn.sigmoid(acc)


def _split3(x):
    hi = x.astype(BF16)
    rem = x - hi.astype(F32)
    mid = rem.astype(BF16)
    lo = (rem - mid.astype(F32)).astype(BF16)
    return jnp.concatenate([hi, mid, lo], axis=1)


def _ssd_kernel(z_ref, xs_ref, bc_ref, dt_ref, cwx_ref, cbx_ref, cwb_ref, cbb_ref, dtb_ref, alog_ref,
                dskip_ref, ng_ref, hexp_ref, lexp_ref, o_ref, tailx_ref, tailb_ref, state_ref):
    c = pl.program_id(1)

    @pl.when(c == 0)
    def _():
        tailx_ref[...] = jnp.zeros_like(tailx_ref)
        tailb_ref[...] = jnp.zeros_like(tailb_ref)
        state_ref[...] = jnp.zeros_like(state_ref)

    xs_raw = xs_ref[...]
    bc_raw = bc_ref[...]
    xs = _causal_conv_silu(xs_raw, tailx_ref[...], cwx_ref[...], cbx_ref[...])
    bc = _causal_conv_silu(bc_raw, tailb_ref[...], cwb_ref[...], cbb_ref[...])
    tailx_ref[...] = xs_raw[SSM_CHUNK - SUBLANES:]
    tailb_ref[...] = bc_raw[SSM_CHUNK - SUBLANES:]

    dt_in = dt_ref[...][:, :SSM_HEADS] + dtb_ref[...]
    dt = jnp.maximum(dt_in, 0.0) + jnp.log1p(jnp.exp(-jnp.abs(dt_in)))
    a_neg = -jnp.exp(alog_ref[...])
    d_a = dt * a_neg
    li = lax.broadcasted_iota(jnp.int32, (SSM_CHUNK, SSM_CHUNK), 0)
    si = lax.broadcasted_iota(jnp.int32, (SSM_CHUNK, SSM_CHUNK), 1)
    causal = li >= si
    tri = causal.astype(F32)
    a_cs = jnp.dot(tri, d_a, preferred_element_type=F32, precision=HI)
    a_cs_t = lax.dot_general(d_a, tri, (((0,), (1,)), ((), ())),
                             preferred_element_type=F32, precision=HI)
    a_last = a_cs[SSM_CHUNK - 1:SSM_CHUNK]
    e_out = jnp.exp(a_cs)
    e_state = jnp.exp(a_last - a_cs) * dt
    small = jnp.concatenate([dt, e_out, e_state], axis=0)
    wide = jnp.dot(_split3(small), hexp_ref[...], preferred_element_type=F32)
    dt_w = wide[0:SSM_CHUNK]
    e_out_w = wide[SSM_CHUNK:2 * SSM_CHUNK]
    e_state_w = wide[2 * SSM_CHUNK:]
    a_col = jnp.dot(_split3(a_cs), lexp_ref[...], preferred_element_type=F32)

    x_dt = (xs * dt_w).astype(BF16)
    x_state = (xs * e_state_w).astype(BF16)
    chunk_decay_w = e_out_w[SSM_CHUNK - 1:SSM_CHUNK]

    heads_per_group = SSM_HEADS // SSM_GROUPS
    y_parts = []
    for g in range(SSM_GROUPS):
        b_g = bc[:, g * SSM_STATE:(g + 1) * SSM_STATE].astype(BF16)
        c_g = bc[:, SSM_GN + g * SSM_STATE:SSM_GN + (g + 1) * SSM_STATE].astype(BF16)
        cb = lax.dot_general(c_g, b_g, (((1,), (1,)), ((), ())), preferred_element_type=F32)
        gsl = slice(g * SSM_GROUP_W, (g + 1) * SSM_GROUP_W)
        h_prev = state_ref[g]
        y_off = jnp.dot(c_g, h_prev.astype(BF16), preferred_element_type=F32) * e_out_w[:, gsl]
        diag = []
        for r in range(heads_per_group):
            h = g * heads_per_group + r
            seg = a_col[:, h * SSM_CHUNK:(h + 1) * SSM_CHUNK] - a_cs_t[h:h + 1, :]
            decay = jnp.exp(jnp.where(causal, seg, -jnp.inf))
            m = (cb * decay).astype(BF16)
            diag.append(jnp.dot(m, x_dt[:, h * SSM_HEAD_DIM:(h + 1) * SSM_HEAD_DIM],
                                preferred_element_type=F32))
        y_parts.append(jnp.concatenate(diag, axis=1) + y_off)
        upd = lax.dot_general(b_g, x_state[:, gsl], (((0,), (0,)), ((), ())), preferred_element_type=F32)
        state_ref[g] = h_prev * chunk_decay_w[:, gsl] + upd

    y = jnp.concatenate(y_parts, axis=1) + dskip_ref[...] * xs
    z = z_ref[...]
    y = y * (z * jax.nn.sigmoid(z))
    normed = []
    for g in range(SSM_GROUPS):
        y_g = y[:, g * SSM_GROUP_W:(g + 1) * SSM_GROUP_W]
        normed.append(y_g * lax.rsqrt(jnp.mean(y_g * y_g, axis=-1, keepdims=True) + EPS))
    o_ref[...] = (jnp.concatenate(normed, axis=1) * ng_ref[...]).astype(o_ref.dtype)


def _ssd_mixer(zxbcdt, conv_w, conv_b, dt_bias, a_log, d_skip, norm_g, batch, seq):
    t = zxbcdt.shape[0]
    nc = seq // SSM_CHUNK
    head_expand = jnp.tile(jnp.repeat(jnp.eye(SSM_HEADS, dtype=BF16), SSM_HEAD_DIM, axis=1), (3, 1))
    lane_expand = jnp.tile(jnp.repeat(jnp.eye(SSM_HEADS, dtype=BF16), SSM_CHUNK, axis=1), (3, 1))
    row = lambda b, c: (b * nc + c, 0)
    const2 = lambda b, c: (0, 0)
    bc_w = 2 * SSM_GN
    return pl.pallas_call(
        _ssd_kernel,
        grid=(batch, nc),
        in_specs=[
            pl.BlockSpec((SSM_CHUNK, SSM_D_INNER), row),
            pl.BlockSpec((SSM_CHUNK, SSM_D_INNER), lambda b, c: (b * nc + c, 1)),
            pl.BlockSpec((SSM_CHUNK, bc_w), lambda b, c: (b * nc + c, 2 * SSM_D_INNER // bc_w)),
            pl.BlockSpec((SSM_CHUNK, LANES), lambda b, c: (b * nc + c, (2 * SSM_D_INNER + bc_w) // LANES)),
            pl.BlockSpec((SSM_CONV, SSM_D_INNER), const2),
            pl.BlockSpec((1, SSM_D_INNER), const2),
            pl.BlockSpec((SSM_CONV, bc_w), const2),
            pl.BlockSpec((1, bc_w), const2),
            pl.BlockSpec((1, SSM_HEADS), const2),
            pl.BlockSpec((1, SSM_HEADS), const2),
            pl.BlockSpec((1, SSM_D_INNER), const2),
            pl.BlockSpec((1, SSM_D_INNER), const2),
            pl.BlockSpec((3 * SSM_HEADS, SSM_D_INNER), const2),
            pl.BlockSpec((3 * SSM_HEADS, SSM_HEADS * SSM_CHUNK), const2),
        ],
        out_specs=pl.BlockSpec((SSM_CHUNK, SSM_D_INNER), row),
        out_shape=jax.ShapeDtypeStruct((t, SSM_D_INNER), BF16),
        scratch_shapes=[
            pltpu.VMEM((SUBLANES, SSM_D_INNER), F32),
            pltpu.VMEM((SUBLANES, bc_w), F32),
            pltpu.VMEM((SSM_GROUPS, SSM_STATE, SSM_GROUP_W), F32),
        ],
        compiler_params=_cparams("arbitrary", "arbitrary"),
        name="ssd_mixer",
    )(zxbcdt, zxbcdt, zxbcdt, zxbcdt,
      conv_w[:, :SSM_D_INNER], conv_b[:SSM_D_INNER].reshape(1, -1),
      conv_w[:, SSM_D_INNER:], conv_b[SSM_D_INNER:].reshape(1, -1),
      dt_bias.reshape(1, -1), a_log.reshape(1, -1),
      jnp.repeat(d_skip, SSM_HEAD_DIM).reshape(1, -1), norm_g.reshape(1, -1),
      head_expand, lane_expand)


def _head_rms(x, g):
    return x * lax.rsqrt(jnp.mean(x * x, axis=-1, keepdims=True) + EPS) * g


def _swa_kernel(q_ref, kvc_ref, kvp_ref, bucket_ref, qg_ref, kg_ref, rel_ref, sink_ref, o_ref,
                bias_ref, sinkcol_ref):
    b = pl.program_id(0)
    i = pl.program_id(1)

    @pl.when((b == 0) & (i == 0))
    def _():
        bucket = bucket_ref[...]
        qi = lax.broadcasted_iota(jnp.int32, (WINDOW, 2 * WINDOW), 0)
        kj = lax.broadcasted_iota(jnp.int32, (WINDOW, 2 * WINDOW), 1)
        dist = qi + WINDOW - kj
        band = (dist >= 0) & (dist < WINDOW)
        for h in range(ATTN_Q_HEADS):
            g, r = divmod(h, ATTN_Q_PER_KV)
            acc = jnp.zeros(bucket.shape, F32)
            for k in range(REL_BUCKETS):
                acc = jnp.where(bucket == k, rel_ref[k, h], acc)
            rows = slice(r * WINDOW, (r + 1) * WINDOW)
            bias_ref[0, g, rows, :] = jnp.where(band, acc, -jnp.inf)
            bias_ref[1, g, rows, :] = jnp.where(band & (kj >= WINDOW), acc, -jnp.inf)
            sinkcol_ref[g, rows, :] = jnp.full((WINDOW, 1), sink_ref[h], F32)

    first = (i == 0).astype(jnp.int32)
    q_all = q_ref[...]
    kv_c = kvc_ref[...]
    kv_p = kvp_ref[...]
    kv_w = ATTN_KV_HEADS * ATTN_HEAD_DIM
    q_gain = qg_ref[...] * (ATTN_HEAD_DIM ** -0.5)
    outs = []
    for g in range(ATTN_KV_HEADS):
        ksl = slice(g * ATTN_HEAD_DIM, (g + 1) * ATTN_HEAD_DIM)
        vsl = slice(kv_w + g * ATTN_HEAD_DIM, kv_w + (g + 1) * ATTN_HEAD_DIM)
        k = jnp.concatenate([kv_p[:, ksl], kv_c[:, ksl]], axis=0)
        v = jnp.concatenate([kv_p[:, vsl], kv_c[:, vsl]], axis=0).astype(BF16)
        k = _head_rms(k, kg_ref[...]).astype(BF16)
        q = jnp.concatenate(
            [q_all[:, (g * ATTN_Q_PER_KV + r) * ATTN_HEAD_DIM:(g * ATTN_Q_PER_KV + r + 1) * ATTN_HEAD_DIM]
             for r in range(ATTN_Q_PER_KV)], axis=0)
        q = _head_rms(q, q_gain).astype(BF16)
        s = lax.dot_general(q, k, (((1,), (1,)), ((), ())), preferred_element_type=F32)
        s = s + bias_ref[first, g]
        sink = sinkcol_ref[g]
        m = jnp.maximum(jnp.max(s, axis=-1, keepdims=True), sink)
        p = jnp.exp(s - m)
        denom = jnp.sum(p, axis=-1, keepdims=True) + jnp.exp(sink - m)
        pv = jnp.dot(p.astype(BF16), v, preferred_element_type=F32) * (1.0 / denom)
        outs.extend(pv[r * WINDOW:(r + 1) * WINDOW] for r in range(ATTN_Q_PER_KV))
    o_ref[...] = jnp.concatenate(outs, axis=1).astype(o_ref.dtype)


def _t5_causal_bucket(dist):
    max_exact = REL_BUCKETS // 2
    d = jnp.maximum(dist, 1).astype(F32)
    large = max_exact + (jnp.log(d / max_exact) / math.log(REL_MAX_DIST / max_exact)
                         * (REL_BUCKETS - max_exact)).astype(jnp.int32)
    large = jnp.minimum(large, REL_BUCKETS - 1)
    return jnp.where(dist < max_exact, dist, large)


def _swa_mixer(qkv, q_norm_g, k_norm_g, sinks, rel_bias, batch, seq):
    t = qkv.shape[0]
    nb = seq // WINDOW
    q_w = ATTN_Q_HEADS * ATTN_HEAD_DIM
    kv_w2 = 2 * ATTN_KV_HEADS * ATTN_HEAD_DIM
    qi = jnp.arange(WINDOW)[:, None]
    kj = jnp.arange(2 * WINDOW)[None, :]
    bucket = _t5_causal_bucket(jnp.maximum(qi + WINDOW - kj, 0)).astype(jnp.int32)
    const2 = lambda b, i: (0, 0)
    smem = pl.BlockSpec(memory_space=pltpu.SMEM)
    return pl.pallas_call(
        _swa_kernel,
        grid=(batch, nb),
        in_specs=[
            pl.BlockSpec((WINDOW, q_w), lambda b, i: (b * nb + i, 0)),
            pl.BlockSpec((WINDOW, kv_w2), lambda b, i: (b * nb + i, q_w // kv_w2)),
            pl.BlockSpec((WINDOW, kv_w2), lambda b, i: (b * nb + jnp.maximum(i - 1, 0), q_w // kv_w2)),
            pl.BlockSpec((WINDOW, 2 * WINDOW), const2),
            pl.BlockSpec((1, ATTN_HEAD_DIM), const2),
            pl.BlockSpec((1, ATTN_HEAD_DIM), const2),
            smem,
            smem,
        ],
        out_specs=pl.BlockSpec((WINDOW, q_w), lambda b, i: (b * nb + i, 0)),
        out_shape=jax.ShapeDtypeStruct((t, q_w), BF16),
        scratch_shapes=[pltpu.VMEM((2, ATTN_KV_HEADS, ATTN_Q_PER_KV * WINDOW, 2 * WINDOW), F32),
                        pltpu.VMEM((ATTN_KV_HEADS, ATTN_Q_PER_KV * WINDOW, 1), F32)],
        compiler_params=_cparams("arbitrary", "arbitrary"),
        name="swa_mixer",
    )(qkv, qkv, qkv, bucket, q_norm_g.reshape(1, -1), k_norm_g.reshape(1, -1), rel_bias, sinks)


def _router_kernel(x_ref, g_ref, sh_ref, sc_ref, rwt_ref, rb_ref,
                   hrow_ref, idx_ref, gate_ref, rank_ref, cnt_ref, carry_ref):
    i = pl.program_id(0)
    tm = x_ref.shape[0]

    @pl.when(i == 0)
    def _():
        carry_ref[...] = jnp.zeros_like(carry_ref)

    h = _norm_modulate(x_ref[...], g_ref[...], sh_ref[0], sc_ref[0])
    for s in range(ROW_TILES):
        hrow_ref[pl.ds(s, tm, stride=ROW_TILES), :] = h[:, s * LANES:(s + 1) * LANES]

    logits = lax.dot_general(rwt_ref[...], h, (((1,), (1,)), ((), ())),
                             preferred_element_type=F32, precision=HI) + rb_ref[...]
    e_iota = lax.broadcasted_iota(jnp.int32, logits.shape, 0)
    work = logits
    sels, vals = [], []
    for k in range(TOP_K):
        m = jnp.max(work, axis=0, keepdims=True)
        idx = jnp.min(jnp.where(work == m, e_iota, N_EXPERTS), axis=0, keepdims=True)
        sel = e_iota == idx
        work = jnp.where(sel, -jnp.inf, work)
        idx_ref[k:k + 1, :] = idx
        sels.append(sel)
        vals.append(m)
    exps = [jnp.exp(v - vals[0]) for v in vals]
    denom = exps[0] + exps[1] + exps[2] + exps[3]
    for k in range(TOP_K):
        gate_ref[k:k + 1, :] = exps[k] / denom

    chosen = sels[0] | sels[1] | sels[2] | sels[3]
    t_row = lax.broadcasted_iota(jnp.int32, (tm, tm), 0)
    t_col = lax.broadcasted_iota(jnp.int32, (tm, tm), 1)
    before = (t_row < t_col).astype(BF16)
    prior = jnp.dot(chosen.astype(BF16), before, preferred_element_type=F32)
    running = carry_ref[...] + prior
    for k in range(TOP_K):
        rank_ref[k:k + 1, :] = jnp.sum(jnp.where(sels[k], running, 0.0), axis=0, keepdims=True).astype(jnp.int32)
    total = carry_ref[...] + jnp.sum(chosen.astype(F32), axis=1, keepdims=True)
    carry_ref[...] = total
    cnt_ref[...] = total.astype(jnp.int32)


def _router(x, g, shift, scale, r_w, r_b, seq, tm=256):
    t, d = x.shape
    per_b = seq // tm
    return pl.pallas_call(
        _router_kernel,
        grid=(t // tm,),
        in_specs=[
            pl.BlockSpec((tm, d), lambda i: (i, 0)),
            pl.BlockSpec((1, d), lambda i: (0, 0)),
            pl.BlockSpec((1, 1, d), lambda i: (i // per_b, 0, 0)),
            pl.BlockSpec((1, 1, d), lambda i: (i // per_b, 0, 0)),
            pl.BlockSpec((N_EXPERTS, d), lambda i: (0, 0)),
            pl.BlockSpec((N_EXPERTS, 1), lambda i: (0, 0)),
        ],
        out_specs=[
            pl.BlockSpec((tm * ROW_TILES, LANES), lambda i: (i, 0)),
            pl.BlockSpec((TOP_K, tm), lambda i: (0, i)),
            pl.BlockSpec((TOP_K, tm), lambda i: (0, i)),
            pl.BlockSpec((TOP_K, tm), lambda i: (0, i)),
            pl.BlockSpec((N_EXPERTS, 1), lambda i: (0, 0)),
        ],
        out_shape=[
            jax.ShapeDtypeStruct((t * ROW_TILES, LANES), F32),
            jax.ShapeDtypeStruct((TOP_K, t), jnp.int32),
            jax.ShapeDtypeStruct((TOP_K, t), F32),
            jax.ShapeDtypeStruct((TOP_K, t), jnp.int32),
            jax.ShapeDtypeStruct((N_EXPERTS, 1), jnp.int32),
        ],
        scratch_shapes=[pltpu.VMEM((N_EXPERTS, 1), F32)],
        compiler_params=_cparams("arbitrary"),
        name="moe_router",
    )(x, g.reshape(1, d), shift, scale, r_w.T, r_b.reshape(-1, 1))


def _slot_kernel(start_ref, idx_ref, rank_ref, pos_ref):
    idx = idx_ref[...]
    acc = rank_ref[...]
    for e in range(N_EXPERTS):
        acc = acc + jnp.where(idx == e, start_ref[e], 0)
    pos_ref[...] = acc


def _slots(group_start, idx_t, rank_t):
    return pl.pallas_call(
        _slot_kernel,
        in_specs=[pl.BlockSpec(memory_space=pltpu.SMEM),
                  pl.BlockSpec(memory_space=pltpu.VMEM),
                  pl.BlockSpec(memory_space=pltpu.VMEM)],
        out_specs=pl.BlockSpec(memory_space=pltpu.VMEM),
        out_shape=jax.ShapeDtypeStruct(idx_t.shape, jnp.int32),
        name="moe_slots",
    )(group_start, idx_t, rank_t)


def _row_copy(src_ref, src_row, dst_ref, dst_row, sem):
    return pltpu.make_async_copy(
        src_ref.at[pl.ds(pl.multiple_of(src_row * ROW_TILES, ROW_TILES), ROW_TILES)],
        dst_ref.at[pl.ds(pl.multiple_of(dst_row * ROW_TILES, ROW_TILES), ROW_TILES)],
        sem)


def _block_copy(src_ref, dst_ref, dst_blk, sem):
    rows = MOE_BLOCK * ROW_TILES
    return pltpu.make_async_copy(src_ref, dst_ref.at[pl.ds(pl.multiple_of(dst_blk * rows, rows), rows)], sem)


def _dispatch_kernel(pad_lo_ref, pad_hi_ref, nused_ref, pos_ref, hrow_ref, xg_ref, zero_ref, sem):
    i = pl.program_id(0)
    tm = pos_ref.shape[1]
    nblk = xg_ref.shape[0] // (MOE_BLOCK * ROW_TILES)

    @pl.when(i == 0)
    def _():
        zero_ref[...] = jnp.zeros_like(zero_ref)

        def fill(b, carry):
            _block_copy(zero_ref, xg_ref, b, sem).start()
            return carry

        def fill_wait(b, carry):
            _block_copy(zero_ref, xg_ref, b, sem).wait()
            return carry

        lax.fori_loop(nused_ref[0], nblk, fill, 0)
        lax.fori_loop(nused_ref[0], nblk, fill_wait, 0)

    @pl.when(i < N_EXPERTS)
    def _():
        def pad(r, carry):
            _row_copy(zero_ref, 0, xg_ref, r, sem).start()
            return carry

        def pad_wait(r, carry):
            _row_copy(zero_ref, 0, xg_ref, r, sem).wait()
            return carry

        lax.fori_loop(pad_lo_ref[i], pad_hi_ref[i], pad, 0)
        lax.fori_loop(pad_lo_ref[i], pad_hi_ref[i], pad_wait, 0)

    def issue(t, carry):
        for k in range(TOP_K):
            _row_copy(hrow_ref, t, xg_ref, pos_ref[k, t], sem).start(priority=k % 2)
        return carry

    lax.fori_loop(0, tm, issue, 0)

    def drain(t, carry):
        for k in range(TOP_K):
            _row_copy(hrow_ref, t, xg_ref, pos_ref[k, t], sem).wait()
        return carry

    lax.fori_loop(0, tm, drain, 0)


def _dispatch(pad_lo, pad_hi, nused, pos_t, h_rows, cap, tm=256):
    t = pos_t.shape[1]
    assert t // tm >= N_EXPERTS
    grid_spec = pltpu.PrefetchScalarGridSpec(
        num_scalar_prefetch=3,
        grid=(t // tm,),
        in_specs=[
            pl.BlockSpec((TOP_K, tm), lambda i, *_: (0, i), memory_space=pltpu.SMEM),
            pl.BlockSpec((tm * ROW_TILES, LANES), lambda i, *_: (i, 0)),
        ],
        out_specs=pl.BlockSpec(memory_space=pl.ANY),
        scratch_shapes=[pltpu.VMEM((MOE_BLOCK * ROW_TILES, LANES), F32), pltpu.SemaphoreType.DMA],
    )
    return pl.pallas_call(
        _dispatch_kernel,
        grid_spec=grid_spec,
        out_shape=jax.ShapeDtypeStruct((cap * ROW_TILES, LANES), F32),
        compiler_params=pltpu.CompilerParams(dimension_semantics=("arbitrary",), has_side_effects=True),
        name="moe_dispatch",
    )(pad_lo, pad_hi, nused, pos_t, h_rows)


def _weight_copies(w1_hbm, w2_hbm, w1buf, w2buf, sems, layer, e, slot):
    return (pltpu.make_async_copy(w1_hbm.at[layer, e], w1buf.at[slot], sems.at[0, slot]),
            pltpu.make_async_copy(w2_hbm.at[layer, e], w2buf.at[slot], sems.at[1, slot]))


def _expert_kernel(blk_e_ref, nused_ref, next_e_ref, slot_ref, x_ref, w1_hbm, w2_hbm, bg_ref, bl_ref, b2_ref,
                   sel_ref, o_ref, w1buf, w2buf, wg_s, wl_s, w2_s, sems, *, layer):
    blk = pl.program_id(0)
    active = blk < nused_ref[0]
    e = blk_e_ref[blk]
    slot = slot_ref[e]
    new_expert = (blk == 0) | (e != blk_e_ref[jnp.maximum(blk - 1, 0)])
    copies = functools.partial(_weight_copies, w1_hbm, w2_hbm, w1buf, w2buf, sems, layer)

    @pl.when(blk == 0)
    def _():
        for cp in copies(e, slot):
            cp.start()

    @pl.when(active & new_expert)
    def _():
        nxt = next_e_ref[e]

        @pl.when(nxt >= 0)
        def _():
            for cp in copies(nxt, 1 - slot):
                cp.start()

        for cp in copies(e, slot):
            cp.wait()
        half = SPLIT_W // 2
        for j in range(w1buf.shape[2] // SPLIT_W):
            chunk = w1buf[slot, :, j * SPLIT_W:(j + 1) * SPLIT_W].astype(BF16)
            split = jnp.dot(chunk, sel_ref[...], preferred_element_type=F32)
            wg_s[:, j * half:(j + 1) * half] = split[:, :half].astype(BF16)
            wl_s[:, j * half:(j + 1) * half] = split[:, half:].astype(BF16)
        w2_s[...] = w2buf[slot].astype(BF16)

    @pl.when(active)
    def _():
        x = jnp.concatenate(
            [x_ref[pl.ds(s, MOE_BLOCK, stride=ROW_TILES), :] for s in range(ROW_TILES)], axis=1).astype(BF16)
        glu = jnp.dot(x, wg_s[...], preferred_element_type=F32) + bg_ref[0]
        lin = jnp.dot(x, wl_s[...], preferred_element_type=F32) + bl_ref[0]
        glu = jnp.minimum(glu, SWIGLU_LIMIT)
        lin = jnp.clip(lin, -SWIGLU_LIMIT, SWIGLU_LIMIT)
        act = glu * jax.nn.sigmoid(SWIGLU_ALPHA * glu) * (lin + 1.0)
        y = jnp.dot(act.astype(BF16), w2_s[...], preferred_element_type=F32) + b2_ref[0]
        for s in range(ROW_TILES):
            o_ref[pl.ds(s, MOE_BLOCK, stride=ROW_TILES), :] = y[:, s * LANES:(s + 1) * LANES]

    @pl.when(jnp.logical_not(active))
    def _():
        o_ref[...] = jnp.zeros_like(o_ref)


def _experts(layer, blk_e, nused, next_e, e_slot, xg_rows, w1_all, w2_all, b_glu, b_lin, b2):
    nblk = blk_e.shape[0]
    d = D_MODEL
    ff = w2_all.shape[2]
    col = jnp.arange(SPLIT_W)
    sel = (jnp.arange(SPLIT_W)[None, :] == ((col % 2) * (SPLIT_W // 2) + col // 2)[:, None]).astype(BF16)

    def blk_map(b, blk_e_ref, nused_ref, *_):
        return (jnp.minimum(b, nused_ref[0] - 1), 0)

    def e_map(b, blk_e_ref, nused_ref, *_):
        return (blk_e_ref[jnp.minimum(b, nused_ref[0] - 1)], 0, 0)

    grid_spec = pltpu.PrefetchScalarGridSpec(
        num_scalar_prefetch=4,
        grid=(nblk,),
        in_specs=[
            pl.BlockSpec((MOE_BLOCK * ROW_TILES, LANES), blk_map),
            pl.BlockSpec(memory_space=pl.ANY),
            pl.BlockSpec(memory_space=pl.ANY),
            pl.BlockSpec((1, 1, ff), e_map),
            pl.BlockSpec((1, 1, ff), e_map),
            pl.BlockSpec((1, 1, d), e_map),
            pl.BlockSpec((SPLIT_W, SPLIT_W), lambda b, *_: (0, 0)),
        ],
        out_specs=pl.BlockSpec((MOE_BLOCK * ROW_TILES, LANES), lambda b, *_: (b, 0)),
        scratch_shapes=[
            pltpu.VMEM((2, d, 2 * ff), F32), pltpu.VMEM((2, ff, d), F32),
            pltpu.VMEM((d, ff), BF16), pltpu.VMEM((d, ff), BF16), pltpu.VMEM((ff, d), BF16),
            pltpu.SemaphoreType.DMA((2, 2)),
        ],
    )
    return pl.pallas_call(
        functools.partial(_expert_kernel, layer=layer),
        grid_spec=grid_spec,
        out_shape=jax.ShapeDtypeStruct(xg_rows.shape, F32),
        compiler_params=_cparams("arbitrary"),
        name="moe_experts",
    )(blk_e, nused, next_e, e_slot, xg_rows, w1_all, w2_all, b_glu, b_lin, b2, sel)


def _combine_kernel(pos_ref, yg_ref, gate_ref, x_ref, g2_ref, o_ref, buf_ref, sem):
    tm = x_ref.shape[0]

    def issue(t, carry):
        for k in range(TOP_K):
            _row_copy(yg_ref, pos_ref[k, t], buf_ref.at[k], t, sem).start(priority=k % 2)
        return carry

    lax.fori_loop(0, tm, issue, 0)

    eye = (lax.broadcasted_iota(jnp.int32, (tm, tm), 0) ==
           lax.broadcasted_iota(jnp.int32, (tm, tm), 1)).astype(F32)
    gate_cols = lax.dot_general(eye, gate_ref[...], (((1,), (1,)), ((), ())),
                                preferred_element_type=F32, precision=HI)

    def drain(t, carry):
        for k in range(TOP_K):
            _row_copy(yg_ref, pos_ref[k, t], buf_ref.at[k], t, sem).wait()
        return carry

    lax.fori_loop(0, tm, drain, 0)

    g2 = g2_ref[0]
    for s in range(ROW_TILES):
        lanes = slice(s * LANES, (s + 1) * LANES)
        acc = jnp.zeros((tm, LANES), F32)
        for k in range(TOP_K):
            acc = acc + gate_cols[:, k:k + 1] * buf_ref[k, pl.ds(s, tm, stride=ROW_TILES), :]
        o_ref[:, lanes] = x_ref[:, lanes] + g2[:, lanes] * acc


def _combine(pos_t, yg_rows, gate_t, x, gate2, seq, tm=128):
    t, d = x.shape
    per_b = seq // tm
    return pl.pallas_call(
        _combine_kernel,
        grid=(t // tm,),
        in_specs=[
            pl.BlockSpec((TOP_K, tm), lambda i: (0, i), memory_space=pltpu.SMEM),
            pl.BlockSpec(memory_space=pl.ANY),
            pl.BlockSpec((TOP_K, tm), lambda i: (0, i)),
            pl.BlockSpec((tm, d), lambda i: (i, 0)),
            pl.BlockSpec((1, 1, d), lambda i: (i // per_b, 0, 0)),
        ],
        out_specs=pl.BlockSpec((tm, d), lambda i: (i, 0)),
        out_shape=jax.ShapeDtypeStruct((t, d), F32),
        scratch_shapes=[pltpu.VMEM((TOP_K, tm * ROW_TILES, LANES), F32), pltpu.SemaphoreType.DMA],
        compiler_params=_cparams("arbitrary"),
        name="moe_combine",
    )(pos_t, yg_rows, gate_t, x, gate2)


def _moe(layer, x, g, shift, scale, gate2, r_w, r_b, w1_all, b1, w2_all, b2, seq):
    t = x.shape[0]
    h_rows, idx_t, gate_t, rank_t, counts = _router(x, g, shift, scale, r_w, r_b, seq)
    counts = counts.reshape(-1)
    padded = ((counts + MOE_BLOCK - 1) // MOE_BLOCK) * MOE_BLOCK
    group_end = jnp.cumsum(padded)
    group_start = (group_end - padded).astype(jnp.int32)
    cap = t * TOP_K + N_EXPERTS * MOE_BLOCK
    nblk = cap // MOE_BLOCK
    blk_start = jnp.arange(nblk, dtype=jnp.int32) * MOE_BLOCK
    blk_e = jnp.minimum(jnp.sum(blk_start[:, None] >= group_end[None, :], axis=1), N_EXPERTS - 1).astype(jnp.int32)
    nused = (group_end[-1:] // MOE_BLOCK).astype(jnp.int32)

    pos_t = _slots(group_start, idx_t, rank_t)
    pad_lo = (group_start + counts).astype(jnp.int32)
    pad_hi = group_end.astype(jnp.int32)
    xg_rows = _dispatch(pad_lo, pad_hi, nused, pos_t, h_rows, cap)
    nonempty = counts > 0
    e_ids = jnp.arange(N_EXPERTS, dtype=jnp.int32)
    later = jnp.where(nonempty[None, :] & (e_ids[None, :] > e_ids[:, None]), e_ids[None, :], N_EXPERTS)
    next_e = jnp.min(later, axis=1)
    next_e = jnp.where(next_e == N_EXPERTS, -1, next_e).astype(jnp.int32)
    e_slot = ((jnp.cumsum(nonempty.astype(jnp.int32)) - 1) % 2).astype(jnp.int32)
    yg_rows = _experts(layer, blk_e, nused, next_e, e_slot, xg_rows, w1_all, w2_all,
                       b1[:, None, 0::2], b1[:, None, 1::2], b2[:, None, :])
    return _combine(pos_t, yg_rows, gate_t, x, gate2, seq)


def kernel(x, c, ada_w, ada_b, norm1_g, norm2_g, m_in_w, m_conv_w, m_conv_b, m_dt_bias, m_A_log, m_D, m_norm_g, m_out_w, a_qkv_w, a_q_norm_g, a_k_norm_g, a_sinks, a_out_w, rel_bias, r_w, r_b, e_w1, e_b1, e_w2, e_b2):
    batch, seq, d = x.shape
    depth = ada_w.shape[0]
    t = batch * seq
    xf = x.reshape(t, d)

    c_pad = jnp.zeros((SUBLANES, d), F32).at[:batch].set(c)
    mod = _adaln(c_pad, ada_w, ada_b)[:, :batch]

    for i in range(depth):
        parts = [mod[i, :, p * d:(p + 1) * d].reshape(batch, 1, d) for p in range(6)]
        sh1, sc1, g1, sh2, sc2, g2 = parts
        j = i // 2
        if i % 2 == 0:
            w_in = jnp.pad(m_in_w[j], ((0, 0), (0, SSM_IN_PAD - m_in_w.shape[2]))).astype(BF16)
            zxbcdt = _norm_matmul(xf, norm1_g[i], sh1, sc1, w_in, seq)
            y = _ssd_mixer(zxbcdt, m_conv_w[j], m_conv_b[j], m_dt_bias[j], m_A_log[j], m_D[j],
                           m_norm_g[j], batch, seq)
            xf = _matmul_residual(y, m_out_w[j].astype(BF16), xf, g1, seq)
        else:
            qkv = _norm_matmul(xf, norm1_g[i], sh1, sc1, a_qkv_w[j].astype(BF16), seq)
            y = _swa_mixer(qkv, a_q_norm_g[j], a_k_norm_g[j], a_sinks[j], rel_bias, batch, seq)
            xf = _matmul_residual(y, a_out_w[j].astype(BF16), xf, g1, seq)
        xf = _moe(i, xf, norm2_g[i], sh2, sc2, g2, r_w[i], r_b[i], e_w1, e_b1[i], e_w2, e_b2[i], seq)
    return xf.reshape(batch, seq, d)
```

```python
import functools
import math

import jax
import jax.numpy as jnp
from jax import lax
from jax.experimental import pallas as pl
from jax.experimental.pallas import tpu as pltpu

D_MODEL = 1024
EPS = 1e-6
LANES = 128
SUBLANES = 8
ROW_TILES = D_MODEL // LANES

SSM_D_INNER = 2048
SSM_HEAD_DIM = 64
SSM_HEADS = 32
SSM_GROUPS = 4
SSM_STATE = 128
SSM_CONV = 4
SSM_CHUNK = 128
SSM_GN = SSM_GROUPS * SSM_STATE
SSM_IN_PAD = 5248
SSM_GROUP_W = SSM_D_INNER // SSM_GROUPS

ATTN_HEAD_DIM = 64
ATTN_Q_HEADS = 16
ATTN_KV_HEADS = 4
ATTN_Q_PER_KV = 4
WINDOW = 128
REL_BUCKETS = 32
REL_MAX_DIST = 128

N_EXPERTS = 32
TOP_K = 4
SWIGLU_ALPHA = 1.702
SWIGLU_LIMIT = 7.0
MOE_BLOCK = 256
MOE_TILE = 256
SPLIT_W = 256

VMEM_LIMIT = 56 * 1024 * 1024
HI = lax.Precision.HIGHEST
F32 = jnp.float32
BF16 = jnp.bfloat16


def _cparams(*sem):
    return pltpu.CompilerParams(dimension_semantics=sem, vmem_limit_bytes=VMEM_LIMIT)


def _norm_modulate(x, g, shift, scale):
    ms = jnp.mean(x * x, axis=-1, keepdims=True)
    return x * lax.rsqrt(ms + EPS) * g * (1.0 + scale) + shift


def _adaln_kernel(c_ref, w_ref, b_ref, o_ref):
    c = c_ref[...]
    c_act = c * jax.nn.sigmoid(c)
    o_ref[0] = jnp.dot(c_act, w_ref[0], preferred_element_type=F32, precision=HI) + b_ref[0]


def _adaln(c_pad, ada_w, ada_b):
    depth, d, n = ada_w.shape
    tn = 1536
    return pl.pallas_call(
        _adaln_kernel,
        grid=(depth, n // tn),
        in_specs=[
            pl.BlockSpec((SUBLANES, d), lambda i, j: (0, 0)),
            pl.BlockSpec((1, d, tn), lambda i, j: (i, 0, j)),
            pl.BlockSpec((1, 1, tn), lambda i, j: (i, 0, j)),
        ],
        out_specs=pl.BlockSpec((1, SUBLANES, tn), lambda i, j: (i, 0, j)),
        out_shape=jax.ShapeDtypeStruct((depth, SUBLANES, n), F32),
        compiler_params=_cparams("arbitrary", "arbitrary"),
        name="adaln",
    )(c_pad, ada_w, ada_b.reshape(depth, 1, n))


def _norm_matmul_kernel(x_ref, g_ref, sh_ref, sc_ref, w_ref, o_ref):
    h = _norm_modulate(x_ref[...], g_ref[...], sh_ref[0], sc_ref[0])
    o_ref[...] = jnp.dot(h.astype(BF16), w_ref[...], preferred_element_type=F32)


def _norm_matmul(x, g, shift, scale, w_bf16, seq, tm=256):
    t, d = x.shape
    n = w_bf16.shape[1]
    per_b = seq // tm
    return pl.pallas_call(
        _norm_matmul_kernel,
        grid=(t // tm,),
        in_specs=[
            pl.BlockSpec((tm, d), lambda i: (i, 0)),
            pl.BlockSpec((1, d), lambda i: (0, 0)),
            pl.BlockSpec((1, 1, d), lambda i: (i // per_b, 0, 0)),
            pl.BlockSpec((1, 1, d), lambda i: (i // per_b, 0, 0)),
            pl.BlockSpec((d, n), lambda i: (0, 0)),
        ],
        out_specs=pl.BlockSpec((tm, n), lambda i: (i, 0)),
        out_shape=jax.ShapeDtypeStruct((t, n), F32),
        compiler_params=_cparams("arbitrary"),
        name="norm_matmul",
    )(x, g.reshape(1, d), shift, scale, w_bf16)


def _matmul_residual_kernel(y_ref, w_ref, x_ref, gate_ref, o_ref):
    acc = jnp.dot(y_ref[...], w_ref[...], preferred_element_type=F32)
    o_ref[...] = x_ref[...] + gate_ref[0] * acc


def _matmul_residual(y_bf16, w_bf16, x, gate, seq, tm=512):
    t, k = y_bf16.shape
    d = x.shape[1]
    per_b = seq // tm
    return pl.pallas_call(
        _matmul_residual_kernel,
        grid=(t // tm,),
        in_specs=[
            pl.BlockSpec((tm, k), lambda i: (i, 0)),
            pl.BlockSpec((k, d), lambda i: (0, 0)),
            pl.BlockSpec((tm, d), lambda i: (i, 0)),
            pl.BlockSpec((1, 1, d), lambda i: (i // per_b, 0, 0)),
        ],
        out_specs=pl.BlockSpec((tm, d), lambda i: (i, 0)),
        out_shape=jax.ShapeDtypeStruct((t, d), F32),
        compiler_params=_cparams("arbitrary"),
        name="matmul_residual",
    )(y_bf16, w_bf16, x, gate)


def _causal_conv_silu(cur, prev_tail, w, b):
    rows = lax.broadcasted_iota(jnp.int32, (SUBLANES, cur.shape[1]), 0)
    acc = b + w[SSM_CONV - 1:SSM_CONV] * cur
    for d in range(1, SSM_CONV):
        rolled = pltpu.roll(cur, d, axis=0)
        top = jnp.where(rows < d, pltpu.roll(prev_tail, d, axis=0), rolled[0:SUBLANES])
        shifted = jnp.concatenate([top, rolled[SUBLANES:]], axis=0)
        acc = acc + w[SSM_CONV - 1 - d:SSM_CONV - d] * shifted
    return acc * jax.nn.sigmoid(acc)


def _split3(x):
    hi = x.astype(BF16)
    rem = x - hi.astype(F32)
    mid = rem.astype(BF16)
    lo = (rem - mid.astype(F32)).astype(BF16)
    return jnp.concatenate([hi, mid, lo], axis=1)


def _ssd_kernel(z_ref, xs_ref, bc_ref, dt_ref, cwx_ref, cbx_ref, cwb_ref, cbb_ref, dtb_ref, alog_ref,
                dskip_ref, ng_ref, hexp_ref, lexp_ref, o_ref, tailx_ref, tailb_ref, state_ref):
    c = pl.program_id(1)

    @pl.when(c == 0)
    def _():
        tailx_ref[...] = jnp.zeros_like(tailx_ref)
        tailb_ref[...] = jnp.zeros_like(tailb_ref)
        state_ref[...] = jnp.zeros_like(state_ref)

    xs_raw = xs_ref[...]
    bc_raw = bc_ref[...]
    xs = _causal_conv_silu(xs_raw, tailx_ref[...], cwx_ref[...], cbx_ref[...])
    bc = _causal_conv_silu(bc_raw, tailb_ref[...], cwb_ref[...], cbb_ref[...])
    tailx_ref[...] = xs_raw[SSM_CHUNK - SUBLANES:]
    tailb_ref[...] = bc_raw[SSM_CHUNK - SUBLANES:]

    dt_in = dt_ref[...][:, :SSM_HEADS] + dtb_ref[...]
    dt = jnp.maximum(dt_in, 0.0) + jnp.log1p(jnp.exp(-jnp.abs(dt_in)))
    a_neg = -jnp.exp(alog_ref[...])
    d_a = dt * a_neg
    li = lax.broadcasted_iota(jnp.int32, (SSM_CHUNK, SSM_CHUNK), 0)
    si = lax.broadcasted_iota(jnp.int32, (SSM_CHUNK, SSM_CHUNK), 1)
    causal = li >= si
    tri = causal.astype(F32)
    a_cs = jnp.dot(tri, d_a, preferred_element_type=F32, precision=HI)
    a_cs_t = lax.dot_general(d_a, tri, (((0,), (1,)), ((), ())),
                             preferred_element_type=F32, precision=HI)
    a_last = a_cs[SSM_CHUNK - 1:SSM_CHUNK]
    e_out = jnp.exp(a_cs)
    e_state = jnp.exp(a_last - a_cs) * dt
    small = jnp.concatenate([dt, e_out, e_state], axis=0)
    wide = jnp.dot(_split3(small), hexp_ref[...], preferred_element_type=F32)
    dt_w = wide[0:SSM_CHUNK]
    e_out_w = wide[SSM_CHUNK:2 * SSM_CHUNK]
    e_state_w = wide[2 * SSM_CHUNK:]
    a_col = jnp.dot(_split3(a_cs), lexp_ref[...], preferred_element_type=F32)

    x_dt = (xs * dt_w).astype(BF16)
    x_state = (xs * e_state_w).astype(BF16)
    chunk_decay_w = e_out_w[SSM_CHUNK - 1:SSM_CHUNK]

    heads_per_group = SSM_HEADS // SSM_GROUPS
    y_parts = []
    for g in range(SSM_GROUPS):
        b_g = bc[:, g * SSM_STATE:(g + 1) * SSM_STATE].astype(BF16)
        c_g = bc[:, SSM_GN + g * SSM_STATE:SSM_GN + (g + 1) * SSM_STATE].astype(BF16)
        cb = lax.dot_general(c_g, b_g, (((1,), (1,)), ((), ())), preferred_element_type=F32)
        gsl = slice(g * SSM_GROUP_W, (g + 1) * SSM_GROUP_W)
        h_prev = state_ref[g]
        y_off = jnp.dot(c_g, h_prev.astype(BF16), preferred_element_type=F32) * e_out_w[:, gsl]
        diag = []
        for r in range(heads_per_group):
            h = g * heads_per_group + r
            seg = a_col[:, h * SSM_CHUNK:(h + 1) * SSM_CHUNK] - a_cs_t[h:h + 1, :]
            decay = jnp.exp(jnp.where(causal, seg, -jnp.inf))
            m = (cb * decay).astype(BF16)
            diag.append(jnp.dot(m, x_dt[:, h * SSM_HEAD_DIM:(h + 1) * SSM_HEAD_DIM],
                                preferred_element_type=F32))
        y_parts.append(jnp.concatenate(diag, axis=1) + y_off)
        upd = lax.dot_general(b_g, x_state[:, gsl], (((0,), (0,)), ((), ())), preferred_element_type=F32)
        state_ref[g] = h_prev * chunk_decay_w[:, gsl] + upd

    y = jnp.concatenate(y_parts, axis=1) + dskip_ref[...] * xs
    z = z_ref[...]
    y = y * (z * jax.nn.sigmoid(z))
    normed = []
    for g in range(SSM_GROUPS):
        y_g = y[:, g * SSM_GROUP_W:(g + 1) * SSM_GROUP_W]
        normed.append(y_g * lax.rsqrt(jnp.mean(y_g * y_g, axis=-1, keepdims=True) + EPS))
    o_ref[...] = (jnp.concatenate(normed, axis=1) * ng_ref[...]).astype(o_ref.dtype)


def _ssd_mixer(zxbcdt, conv_w, conv_b, dt_bias, a_log, d_skip, norm_g, batch, seq):
    t = zxbcdt.shape[0]
    nc = seq // SSM_CHUNK
    head_expand = jnp.tile(jnp.repeat(jnp.eye(SSM_HEADS, dtype=BF16), SSM_HEAD_DIM, axis=1), (3, 1))
    lane_expand = jnp.tile(jnp.repeat(jnp.eye(SSM_HEADS, dtype=BF16), SSM_CHUNK, axis=1), (3, 1))
    row = lambda b, c: (b * nc + c, 0)
    const2 = lambda b, c: (0, 0)
    bc_w = 2 * SSM_GN
    return pl.pallas_call(
        _ssd_kernel,
        grid=(batch, nc),
        in_specs=[
            pl.BlockSpec((SSM_CHUNK, SSM_D_INNER), row),
            pl.BlockSpec((SSM_CHUNK, SSM_D_INNER), lambda b, c: (b * nc + c, 1)),
            pl.BlockSpec((SSM_CHUNK, bc_w), lambda b, c: (b * nc + c, 2 * SSM_D_INNER // bc_w)),
            pl.BlockSpec((SSM_CHUNK, LANES), lambda b, c: (b * nc + c, (2 * SSM_D_INNER + bc_w) // LANES)),
            pl.BlockSpec((SSM_CONV, SSM_D_INNER), const2),
            pl.BlockSpec((1, SSM_D_INNER), const2),
            pl.BlockSpec((SSM_CONV, bc_w), const2),
            pl.BlockSpec((1, bc_w), const2),
            pl.BlockSpec((1, SSM_HEADS), const2),
            pl.BlockSpec((1, SSM_HEADS), const2),
            pl.BlockSpec((1, SSM_D_INNER), const2),
            pl.BlockSpec((1, SSM_D_INNER), const2),
            pl.BlockSpec((3 * SSM_HEADS, SSM_D_INNER), const2),
            pl.BlockSpec((3 * SSM_HEADS, SSM_HEADS * SSM_CHUNK), const2),
        ],
        out_specs=pl.BlockSpec((SSM_CHUNK, SSM_D_INNER), row),
        out_shape=jax.ShapeDtypeStruct((t, SSM_D_INNER), BF16),
        scratch_shapes=[
            pltpu.VMEM((SUBLANES, SSM_D_INNER), F32),
            pltpu.VMEM((SUBLANES, bc_w), F32),
            pltpu.VMEM((SSM_GROUPS, SSM_STATE, SSM_GROUP_W), F32),
        ],
        compiler_params=_cparams("arbitrary", "arbitrary"),
        name="ssd_mixer",
    )(zxbcdt, zxbcdt, zxbcdt, zxbcdt,
      conv_w[:, :SSM_D_INNER], conv_b[:SSM_D_INNER].reshape(1, -1),
      conv_w[:, SSM_D_INNER:], conv_b[SSM_D_INNER:].reshape(1, -1),
      dt_bias.reshape(1, -1), a_log.reshape(1, -1),
      jnp.repeat(d_skip, SSM_HEAD_DIM).reshape(1, -1), norm_g.reshape(1, -1),
      head_expand, lane_expand)


def _head_rms(x, g):
    return x * lax.rsqrt(jnp.mean(x * x, axis=-1, keepdims=True) + EPS) * g


def _swa_kernel(q_ref, kvc_ref, kvp_ref, bucket_ref, qg_ref, kg_ref, rel_ref, sink_ref, o_ref,
                bias_ref, sinkcol_ref):
    b = pl.program_id(0)
    i = pl.program_id(1)

    @pl.when((b == 0) & (i == 0))
    def _():
        bucket = bucket_ref[...]
        qi = lax.broadcasted_iota(jnp.int32, (WINDOW, 2 * WINDOW), 0)
        kj = lax.broadcasted_iota(jnp.int32, (WINDOW, 2 * WINDOW), 1)
        dist = qi + WINDOW - kj
        band = (dist >= 0) & (dist < WINDOW)
        for h in range(ATTN_Q_HEADS):
            g, r = divmod(h, ATTN_Q_PER_KV)
            acc = jnp.zeros(bucket.shape, F32)
            for k in range(REL_BUCKETS):
                acc = jnp.where(bucket == k, rel_ref[k, h], acc)
            rows = slice(r * WINDOW, (r + 1) * WINDOW)
            bias_ref[0, g, rows, :] = jnp.where(band, acc, -jnp.inf)
            bias_ref[1, g, rows, :] = jnp.where(band & (kj >= WINDOW), acc, -jnp.inf)
            sinkcol_ref[g, rows, :] = jnp.full((WINDOW, 1), sink_ref[h], F32)

    first = (i == 0).astype(jnp.int32)
    q_all = q_ref[...]
    kv_c = kvc_ref[...]
    kv_p = kvp_ref[...]
    kv_w = ATTN_KV_HEADS * ATTN_HEAD_DIM
    q_gain = qg_ref[...] * (ATTN_HEAD_DIM ** -0.5)
    outs = []
    for g in range(ATTN_KV_HEADS):
        ksl = slice(g * ATTN_HEAD_DIM, (g + 1) * ATTN_HEAD_DIM)
        vsl = slice(kv_w + g * ATTN_HEAD_DIM, kv_w + (g + 1) * ATTN_HEAD_DIM)
        k = jnp.concatenate([kv_p[:, ksl], kv_c[:, ksl]], axis=0)
        v = jnp.concatenate([kv_p[:, vsl], kv_c[:, vsl]], axis=0).astype(BF16)
        k = _head_rms(k, kg_ref[...]).astype(BF16)
        q = jnp.concatenate(
            [q_all[:, (g * ATTN_Q_PER_KV + r) * ATTN_HEAD_DIM:(g * ATTN_Q_PER_KV + r + 1) * ATTN_HEAD_DIM]
             for r in range(ATTN_Q_PER_KV)], axis=0)
        q = _head_rms(q, q_gain).astype(BF16)
        s = lax.dot_general(q, k, (((1,), (1,)), ((), ())), preferred_element_type=F32)
        s = s + bias_ref[first, g]
        sink = sinkcol_ref[g]
        m = jnp.maximum(jnp.max(s, axis=-1, keepdims=True), sink)
        p = jnp.exp(s - m)
        denom = jnp.sum(p, axis=-1, keepdims=True) + jnp.exp(sink - m)
        pv = jnp.dot(p.astype(BF16), v, preferred_element_type=F32) * (1.0 / denom)
        outs.extend(pv[r * WINDOW:(r + 1) * WINDOW] for r in range(ATTN_Q_PER_KV))
    o_ref[...] = jnp.concatenate(outs, axis=1).astype(o_ref.dtype)


def _t5_causal_bucket(dist):
    max_exact = REL_BUCKETS // 2
    d = jnp.maximum(dist, 1).astype(F32)
    large = max_exact + (jnp.log(d / max_exact) / math.log(REL_MAX_DIST / max_exact)
                         * (REL_BUCKETS - max_exact)).astype(jnp.int32)
    large = jnp.minimum(large, REL_BUCKETS - 1)
    return jnp.where(dist < max_exact, dist, large)


def _swa_mixer(qkv, q_norm_g, k_norm_g, sinks, rel_bias, batch, seq):
    t = qkv.shape[0]
    nb = seq // WINDOW
    q_w = ATTN_Q_HEADS * ATTN_HEAD_DIM
    kv_w2 = 2 * ATTN_KV_HEADS * ATTN_HEAD_DIM
    qi = jnp.arange(WINDOW)[:, None]
    kj = jnp.arange(2 * WINDOW)[None, :]
    bucket = _t5_causal_bucket(jnp.maximum(qi + WINDOW - kj, 0)).astype(jnp.int32)
    const2 = lambda b, i: (0, 0)
    smem = pl.BlockSpec(memory_space=pltpu.SMEM)
    return pl.pallas_call(
        _swa_kernel,
        grid=(batch, nb),
        in_specs=[
            pl.BlockSpec((WINDOW, q_w), lambda b, i: (b * nb + i, 0)),
            pl.BlockSpec((WINDOW, kv_w2), lambda b, i: (b * nb + i, q_w // kv_w2)),
            pl.BlockSpec((WINDOW, kv_w2), lambda b, i: (b * nb + jnp.maximum(i - 1, 0), q_w // kv_w2)),
            pl.BlockSpec((WINDOW, 2 * WINDOW), const2),
            pl.BlockSpec((1, ATTN_HEAD_DIM), const2),
            pl.BlockSpec((1, ATTN_HEAD_DIM), const2),
            smem,
            smem,
        ],
        out_specs=pl.BlockSpec((WINDOW, q_w), lambda b, i: (b * nb + i, 0)),
        out_shape=jax.ShapeDtypeStruct((t, q_w), BF16),
        scratch_shapes=[pltpu.VMEM((2, ATTN_KV_HEADS, ATTN_Q_PER_KV * WINDOW, 2 * WINDOW), F32),
                        pltpu.VMEM((ATTN_KV_HEADS, ATTN_Q_PER_KV * WINDOW, 1), F32)],
        compiler_params=_cparams("arbitrary", "arbitrary"),
        name="swa_mixer",
    )(qkv, qkv, qkv, bucket, q_norm_g.reshape(1, -1), k_norm_g.reshape(1, -1), rel_bias, sinks)


def _router_kernel(x_ref, g_ref, sh_ref, sc_ref, rwt_ref, rb_ref,
                   h_ref, lp_ref, gate_ref, tile_n_ref, tile_carry_ref, cnt_ref, carry_ref):
    i = pl.program_id(0)
    tm = x_ref.shape[0]

    @pl.when(i == 0)
    def _():
        carry_ref[...] = jnp.zeros_like(carry_ref)

    h = _norm_modulate(x_ref[...], g_ref[...], sh_ref[0], sc_ref[0])
    h_ref[...] = h.astype(h_ref.dtype)

    logits = lax.dot_general(rwt_ref[...], h, (((1,), (1,)), ((), ())),
                             preferred_element_type=F32, precision=HI) + rb_ref[...]
    e_iota = lax.broadcasted_iota(jnp.int32, logits.shape, 0)
    work = logits
    sels, vals = [], []
    for k in range(TOP_K):
        m = jnp.max(work, axis=0, keepdims=True)
        idx = jnp.min(jnp.where(work == m, e_iota, N_EXPERTS), axis=0, keepdims=True)
        sel = e_iota == idx
        work = jnp.where(sel, -jnp.inf, work)
        sels.append(sel)
        vals.append(m)
    exps = [jnp.exp(v - vals[0]) for v in vals]
    denom = exps[0] + exps[1] + exps[2] + exps[3]
    for k in range(TOP_K):
        gate_ref[k:k + 1, :] = exps[k] / denom

    chosen = sels[0] | sels[1] | sels[2] | sels[3]
    t_row = lax.broadcasted_iota(jnp.int32, (tm, tm), 0)
    t_col = lax.broadcasted_iota(jnp.int32, (tm, tm), 1)
    before = (t_row < t_col).astype(BF16)
    prior = jnp.dot(chosen.astype(BF16), before, preferred_element_type=F32)
    n = jnp.sum(chosen.astype(F32), axis=1, keepdims=True)
    e_row = lax.broadcasted_iota(jnp.int32, (N_EXPERTS, N_EXPERTS), 0)
    e_col = lax.broadcasted_iota(jnp.int32, (N_EXPERTS, N_EXPERTS), 1)
    run_start = jnp.dot((e_col < e_row).astype(F32), jnp.broadcast_to(n, (N_EXPERTS, LANES)),
                        preferred_element_type=F32, precision=HI)[:, :1]
    local = run_start + prior
    for k in range(TOP_K):
        lp_ref[k:k + 1, :] = jnp.sum(jnp.where(sels[k], local, 0.0), axis=0, keepdims=True).astype(jnp.int32)
    tile_n_ref[0] = n.astype(jnp.int32)
    tile_carry_ref[0] = carry_ref[...].astype(jnp.int32)
    total = carry_ref[...] + n
    carry_ref[...] = total
    cnt_ref[...] = total.astype(jnp.int32)


def _router(x, g, shift, scale, r_w, r_b, seq, tm):
    t, d = x.shape
    per_b = seq // tm
    ntiles = t // tm
    return pl.pallas_call(
        _router_kernel,
        grid=(t // tm,),
        in_specs=[
            pl.BlockSpec((tm, d), lambda i: (i, 0)),
            pl.BlockSpec((1, d), lambda i: (0, 0)),
            pl.BlockSpec((1, 1, d), lambda i: (i // per_b, 0, 0)),
            pl.BlockSpec((1, 1, d), lambda i: (i // per_b, 0, 0)),
            pl.BlockSpec((N_EXPERTS, d), lambda i: (0, 0)),
            pl.BlockSpec((N_EXPERTS, 1), lambda i: (0, 0)),
        ],
        out_specs=[
            pl.BlockSpec((tm, d), lambda i: (i, 0)),
            pl.BlockSpec((TOP_K, tm), lambda i: (0, i)),
            pl.BlockSpec((TOP_K, tm), lambda i: (0, i)),
            pl.BlockSpec((1, N_EXPERTS, 1), lambda i: (i, 0, 0)),
            pl.BlockSpec((1, N_EXPERTS, 1), lambda i: (i, 0, 0)),
            pl.BlockSpec((N_EXPERTS, 1), lambda i: (0, 0)),
        ],
        out_shape=[
            jax.ShapeDtypeStruct((t, d), BF16),
            jax.ShapeDtypeStruct((TOP_K, t), jnp.int32),
            jax.ShapeDtypeStruct((TOP_K, t), F32),
            jax.ShapeDtypeStruct((ntiles, N_EXPERTS, 1), jnp.int32),
            jax.ShapeDtypeStruct((ntiles, N_EXPERTS, 1), jnp.int32),
            jax.ShapeDtypeStruct((N_EXPERTS, 1), jnp.int32),
        ],
        scratch_shapes=[pltpu.VMEM((N_EXPERTS, 1), F32)],
        compiler_params=_cparams("arbitrary"),
        name="moe_router",
    )(x, g.reshape(1, d), shift, scale, r_w.T, r_b.reshape(-1, 1))


def _run_copies(n, src_ref, src_row, dst_ref, dst_row, sem, max_rows, fn):
    for b in range(max_rows.bit_length()):
        size = 1 << b

        @pl.when((n & size) != 0)
        def _():
            lo = n & (size - 1)
            fn(pltpu.make_async_copy(
                src_ref.at[pl.ds(pl.multiple_of((src_row + lo) * ROW_TILES, ROW_TILES), size * ROW_TILES)],
                dst_ref.at[pl.ds(pl.multiple_of((dst_row + lo) * ROW_TILES, ROW_TILES), size * ROW_TILES)],
                sem))


def _tile_runs(tile, n_ref, carry_ref, gstart_ref, sorted_ref, grouped_ref, sem, tm, fn, to_grouped):
    def body(e, run_start):
        n = n_ref[tile * N_EXPERTS + e]
        slot0 = gstart_ref[e] + carry_ref[tile * N_EXPERTS + e]
        if to_grouped:
            _run_copies(n, sorted_ref, run_start, grouped_ref, slot0, sem, tm, fn)
        else:
            _run_copies(n, grouped_ref, slot0, sorted_ref, run_start, sem, tm, fn)
        return run_start + n

    lax.fori_loop(0, N_EXPERTS, body, 0)


def _row_copy(src_ref, src_row, dst_ref, dst_row, sem):
    return pltpu.make_async_copy(
        src_ref.at[pl.ds(pl.multiple_of(src_row * ROW_TILES, ROW_TILES), ROW_TILES)],
        dst_ref.at[pl.ds(pl.multiple_of(dst_row * ROW_TILES, ROW_TILES), ROW_TILES)],
        sem)


def _block_copy(src_ref, dst_ref, dst_blk, sem):
    rows = MOE_BLOCK * ROW_TILES
    return pltpu.make_async_copy(src_ref, dst_ref.at[pl.ds(pl.multiple_of(dst_blk * rows, rows), rows)], sem)


def _dispatch_kernel(n_ref, carry_ref, gstart_ref, pad_lo_ref, pad_hi_ref, nused_ref, lp_ref, h_ref, xg_ref,
                     sorted_ref, zero_ref, run_sem, sem):
    i = pl.program_id(0)
    last = pl.num_programs(0) - 1
    tm = h_ref.shape[0]
    rows = TOP_K * tm
    slot = i % 2
    nblk = xg_ref.shape[0] // (MOE_BLOCK * ROW_TILES)

    @pl.when(i == 0)
    def _():
        zero_ref[...] = jnp.zeros_like(zero_ref)

        def fill(b, carry):
            _block_copy(zero_ref, xg_ref, b, sem).start()
            return carry

        def fill_wait(b, carry):
            _block_copy(zero_ref, xg_ref, b, sem).wait()
            return carry

        lax.fori_loop(nused_ref[0], nblk, fill, 0)
        lax.fori_loop(nused_ref[0], nblk, fill_wait, 0)

    @pl.when(i < N_EXPERTS)
    def _():
        def pad(r, carry):
            _row_copy(zero_ref, 0, xg_ref, r, sem).start()
            return carry

        def pad_wait(r, carry):
            _row_copy(zero_ref, 0, xg_ref, r, sem).wait()
            return carry

        lax.fori_loop(pad_lo_ref[i], pad_hi_ref[i], pad, 0)
        lax.fori_loop(pad_lo_ref[i], pad_hi_ref[i], pad_wait, 0)

    row_id = lax.broadcasted_iota(jnp.int32, (rows, tm), 0)
    perm = row_id == lp_ref[0:1, :]
    for k in range(1, TOP_K):
        perm = perm | (row_id == lp_ref[k:k + 1, :])
    srt = jnp.dot(jnp.where(perm, 1.0, 0.0).astype(BF16), h_ref[...], preferred_element_type=F32)
    for s in range(ROW_TILES):
        sorted_ref[slot, pl.ds(s, rows, stride=ROW_TILES), :] = srt[:, s * LANES:(s + 1) * LANES]

    def runs(tile, buf, fn):
        _tile_runs(tile, n_ref, carry_ref, gstart_ref, sorted_ref.at[buf], xg_ref, run_sem, tm, fn, True)

    @pl.when(i > 0)
    def _():
        runs(i - 1, 1 - slot, lambda cp: cp.wait())

    runs(i, slot, lambda cp: cp.start())

    @pl.when(i == last)
    def _():
        runs(i, slot, lambda cp: cp.wait())


def _dispatch(tile_n, tile_carry, group_start, pad_lo, pad_hi, nused, lp_t, h, cap, tm):
    t, d = h.shape
    assert t // tm >= N_EXPERTS
    grid_spec = pltpu.PrefetchScalarGridSpec(
        num_scalar_prefetch=6,
        grid=(t // tm,),
        in_specs=[
            pl.BlockSpec((TOP_K, tm), lambda i, *_: (0, i)),
            pl.BlockSpec((tm, d), lambda i, *_: (i, 0)),
        ],
        out_specs=pl.BlockSpec(memory_space=pl.ANY),
        scratch_shapes=[
            pltpu.VMEM((2, TOP_K * tm * ROW_TILES, LANES), F32),
            pltpu.VMEM((MOE_BLOCK * ROW_TILES, LANES), F32),
            pltpu.SemaphoreType.DMA, pltpu.SemaphoreType.DMA,
        ],
    )
    return pl.pallas_call(
        _dispatch_kernel,
        grid_spec=grid_spec,
        out_shape=jax.ShapeDtypeStruct((cap * ROW_TILES, LANES), F32),
        compiler_params=pltpu.CompilerParams(dimension_semantics=("arbitrary",), has_side_effects=True,
                                             vmem_limit_bytes=VMEM_LIMIT),
        name="moe_dispatch",
    )(tile_n, tile_carry, group_start, pad_lo, pad_hi, nused, lp_t, h)


def _weight_copies(w1_hbm, w2_hbm, w1buf, w2buf, sems, layer, e, slot):
    return (pltpu.make_async_copy(w1_hbm.at[layer, e], w1buf.at[slot], sems.at[0, slot]),
            pltpu.make_async_copy(w2_hbm.at[layer, e], w2buf.at[slot], sems.at[1, slot]))


def _expert_kernel(blk_e_ref, nused_ref, next_e_ref, slot_ref, x_ref, w1_hbm, w2_hbm, bg_ref, bl_ref, b2_ref,
                   sel_ref, o_ref, w1buf, w2buf, wg_s, wl_s, w2_s, sems, *, layer):
    blk = pl.program_id(0)
    active = blk < nused_ref[0]
    e = blk_e_ref[blk]
    slot = slot_ref[e]
    new_expert = (blk == 0) | (e != blk_e_ref[jnp.maximum(blk - 1, 0)])
    copies = functools.partial(_weight_copies, w1_hbm, w2_hbm, w1buf, w2buf, sems, layer)

    @pl.when(blk == 0)
    def _():
        for cp in copies(e, slot):
            cp.start()

    @pl.when(active & new_expert)
    def _():
        nxt = next_e_ref[e]

        @pl.when(nxt >= 0)
        def _():
            for cp in copies(nxt, 1 - slot):
                cp.start()

        for cp in copies(e, slot):
            cp.wait()
        half = SPLIT_W // 2
        for j in range(w1buf.shape[2] // SPLIT_W):
            chunk = w1buf[slot, :, j * SPLIT_W:(j + 1) * SPLIT_W].astype(BF16)
            split = jnp.dot(chunk, sel_ref[...], preferred_element_type=F32)
            wg_s[:, j * half:(j + 1) * half] = split[:, :half].astype(BF16)
            wl_s[:, j * half:(j + 1) * half] = split[:, half:].astype(BF16)
        w2_s[...] = w2buf[slot].astype(BF16)

    @pl.when(active)
    def _():
        x = jnp.concatenate(
            [x_ref[pl.ds(s, MOE_BLOCK, stride=ROW_TILES), :] for s in range(ROW_TILES)], axis=1).astype(BF16)
        glu = jnp.dot(x, wg_s[...], preferred_element_type=F32) + bg_ref[0]
        lin = jnp.dot(x, wl_s[...], preferred_element_type=F32) + bl_ref[0]
        glu = jnp.minimum(glu, SWIGLU_LIMIT)
        lin = jnp.clip(lin, -SWIGLU_LIMIT, SWIGLU_LIMIT)
        act = glu * jax.nn.sigmoid(SWIGLU_ALPHA * glu) * (lin + 1.0)
        y = jnp.dot(act.astype(BF16), w2_s[...], preferred_element_type=F32) + b2_ref[0]
        for s in range(ROW_TILES):
            o_ref[pl.ds(s, MOE_BLOCK, stride=ROW_TILES), :] = y[:, s * LANES:(s + 1) * LANES]

    @pl.when(jnp.logical_not(active))
    def _():
        o_ref[...] = jnp.zeros_like(o_ref)


def _experts(layer, blk_e, nused, next_e, e_slot, xg_rows, w1_all, w2_all, b_glu, b_lin, b2):
    nblk = blk_e.shape[0]
    d = D_MODEL
    ff = w2_all.shape[2]
    col = jnp.arange(SPLIT_W)
    sel = (jnp.arange(SPLIT_W)[None, :] == ((col % 2) * (SPLIT_W // 2) + col // 2)[:, None]).astype(BF16)

    def blk_map(b, blk_e_ref, nused_ref, *_):
        return (jnp.minimum(b, nused_ref[0] - 1), 0)

    def e_map(b, blk_e_ref, nused_ref, *_):
        return (blk_e_ref[jnp.minimum(b, nused_ref[0] - 1)], 0, 0)

    grid_spec = pltpu.PrefetchScalarGridSpec(
        num_scalar_prefetch=4,
        grid=(nblk,),
        in_specs=[
            pl.BlockSpec((MOE_BLOCK * ROW_TILES, LANES), blk_map),
            pl.BlockSpec(memory_space=pl.ANY),
            pl.BlockSpec(memory_space=pl.ANY),
            pl.BlockSpec((1, 1, ff), e_map),
            pl.BlockSpec((1, 1, ff), e_map),
            pl.BlockSpec((1, 1, d), e_map),
            pl.BlockSpec((SPLIT_W, SPLIT_W), lambda b, *_: (0, 0)),
        ],
        out_specs=pl.BlockSpec((MOE_BLOCK * ROW_TILES, LANES), lambda b, *_: (b, 0)),
        scratch_shapes=[
            pltpu.VMEM((2, d, 2 * ff), F32), pltpu.VMEM((2, ff, d), F32),
            pltpu.VMEM((d, ff), BF16), pltpu.VMEM((d, ff), BF16), pltpu.VMEM((ff, d), BF16),
            pltpu.SemaphoreType.DMA((2, 2)),
        ],
    )
    return pl.pallas_call(
        functools.partial(_expert_kernel, layer=layer),
        grid_spec=grid_spec,
        out_shape=jax.ShapeDtypeStruct(xg_rows.shape, F32),
        compiler_params=_cparams("arbitrary"),
        name="moe_experts",
    )(blk_e, nused, next_e, e_slot, xg_rows, w1_all, w2_all, b_glu, b_lin, b2, sel)


def _combine_kernel(n_ref, carry_ref, gstart_ref, yg_ref, lp_ref, gate_ref, x_ref, g2_ref, o_ref, sorted_ref, sem):
    i = pl.program_id(0)
    ntiles = pl.num_programs(0)
    tm = x_ref.shape[0]
    rows = TOP_K * tm
    slot = i % 2

    def runs(tile, buf, fn):
        _tile_runs(tile, n_ref, carry_ref, gstart_ref, sorted_ref.at[buf], yg_ref, sem.at[buf], tm, fn, False)

    @pl.when(i == 0)
    def _():
        runs(0, 0, lambda cp: cp.start())

    @pl.when(i + 1 < ntiles)
    def _():
        runs(i + 1, 1 - slot, lambda cp: cp.start())

    eye = (lax.broadcasted_iota(jnp.int32, (tm, tm), 0) ==
           lax.broadcasted_iota(jnp.int32, (tm, tm), 1)).astype(F32)
    rows_t = jnp.concatenate([gate_ref[...], lp_ref[...].astype(F32)], axis=0)
    cols = lax.dot_general(eye, rows_t, (((1,), (1,)), ((), ())), preferred_element_type=F32, precision=HI)
    col_id = lax.broadcasted_iota(jnp.int32, (tm, rows), 1)
    weights = jnp.zeros((tm, rows), F32)
    for k in range(TOP_K):
        lp_col = (cols[:, TOP_K + k:TOP_K + k + 1] + 0.5).astype(jnp.int32)
        weights = weights + jnp.where(col_id == lp_col, cols[:, k:k + 1], 0.0)

    runs(i, slot, lambda cp: cp.wait())
    y = jnp.concatenate(
        [sorted_ref[slot, pl.ds(s, rows, stride=ROW_TILES), :] for s in range(ROW_TILES)], axis=1).astype(BF16)
    acc = jnp.dot(weights.astype(BF16), y, preferred_element_type=F32)
    o_ref[...] = x_ref[...] + g2_ref[0] * acc


def _combine(tile_n, tile_carry, group_start, yg_rows, lp_t, gate_t, x, gate2, seq, tm):
    t, d = x.shape
    per_b = seq // tm
    grid_spec = pltpu.PrefetchScalarGridSpec(
        num_scalar_prefetch=3,
        grid=(t // tm,),
        in_specs=[
            pl.BlockSpec(memory_space=pl.ANY),
            pl.BlockSpec((TOP_K, tm), lambda i, *_: (0, i)),
            pl.BlockSpec((TOP_K, tm), lambda i, *_: (0, i)),
            pl.BlockSpec((tm, d), lambda i, *_: (i, 0)),
            pl.BlockSpec((1, 1, d), lambda i, *_: (i // per_b, 0, 0)),
        ],
        out_specs=pl.BlockSpec((tm, d), lambda i, *_: (i, 0)),
        scratch_shapes=[pltpu.VMEM((2, TOP_K * tm * ROW_TILES, LANES), F32), pltpu.SemaphoreType.DMA((2,))],
    )
    return pl.pallas_call(
        _combine_kernel,
        grid_spec=grid_spec,
        out_shape=jax.ShapeDtypeStruct((t, d), F32),
        compiler_params=_cparams("arbitrary"),
        name="moe_combine",
    )(tile_n, tile_carry, group_start, yg_rows, lp_t, gate_t, x, gate2)


def _moe(layer, x, g, shift, scale, gate2, r_w, r_b, w1_all, b1, w2_all, b2, seq):
    t = x.shape[0]
    h, lp_t, gate_t, tile_n, tile_carry, counts = _router(x, g, shift, scale, r_w, r_b, seq, MOE_TILE)
    tile_n = tile_n.reshape(-1)
    tile_carry = tile_carry.reshape(-1)
    counts = counts.reshape(-1)
    padded = ((counts + MOE_BLOCK - 1) // MOE_BLOCK) * MOE_BLOCK
    group_end = jnp.cumsum(padded)
    group_start = (group_end - padded).astype(jnp.int32)
    cap = t * TOP_K + N_EXPERTS * MOE_BLOCK
    nblk = cap // MOE_BLOCK
    blk_start = jnp.arange(nblk, dtype=jnp.int32) * MOE_BLOCK
    blk_e = jnp.minimum(jnp.sum(blk_start[:, None] >= group_end[None, :], axis=1), N_EXPERTS - 1).astype(jnp.int32)
    nused = (group_end[-1:] // MOE_BLOCK).astype(jnp.int32)

    pad_lo = (group_start + counts).astype(jnp.int32)
    pad_hi = group_end.astype(jnp.int32)
    xg_rows = _dispatch(tile_n, tile_carry, group_start, pad_lo, pad_hi, nused, lp_t, h, cap, MOE_TILE)
    nonempty = counts > 0
    e_ids = jnp.arange(N_EXPERTS, dtype=jnp.int32)
    later = jnp.where(nonempty[None, :] & (e_ids[None, :] > e_ids[:, None]), e_ids[None, :], N_EXPERTS)
    next_e = jnp.min(later, axis=1)
    next_e = jnp.where(next_e == N_EXPERTS, -1, next_e).astype(jnp.int32)
    e_slot = ((jnp.cumsum(nonempty.astype(jnp.int32)) - 1) % 2).astype(jnp.int32)
    yg_rows = _experts(layer, blk_e, nused, next_e, e_slot, xg_rows, w1_all, w2_all,
                       b1[:, None, 0::2], b1[:, None, 1::2], b2[:, None, :])
    return _combine(tile_n, tile_carry, group_start, yg_rows, lp_t, gate_t, x, gate2, seq, MOE_TILE)


def kernel(x, c, ada_w, ada_b, norm1_g, norm2_g, m_in_w, m_conv_w, m_conv_b, m_dt_bias, m_A_log, m_D, m_norm_g, m_out_w, a_qkv_w, a_q_norm_g, a_k_norm_g, a_sinks, a_out_w, rel_bias, r_w, r_b, e_w1, e_b1, e_w2, e_b2):
    batch, seq, d = x.shape
    depth = ada_w.shape[0]
    t = batch * seq
    xf = x.reshape(t, d)

    c_pad = jnp.zeros((SUBLANES, d), F32).at[:batch].set(c)
    mod = _adaln(c_pad, ada_w, ada_b)[:, :batch]

    for i in range(depth):
        parts = [mod[i, :, p * d:(p + 1) * d].reshape(batch, 1, d) for p in range(6)]
        sh1, sc1, g1, sh2, sc2, g2 = parts
        j = i // 2
        if i % 2 == 0:
            w_in = jnp.pad(m_in_w[j], ((0, 0), (0, SSM_IN_PAD - m_in_w.shape[2]))).astype(BF16)
            zxbcdt = _norm_matmul(xf, norm1_g[i], sh1, sc1, w_in, seq)
            y = _ssd_mixer(zxbcdt, m_conv_w[j], m_conv_b[j], m_dt_bias[j], m_A_log[j], m_D[j],
                           m_norm_g[j], batch, seq)
            xf = _matmul_residual(y, m_out_w[j].astype(BF16), xf, g1, seq)
        else:
            qkv = _norm_matmul(xf, norm1_g[i], sh1, sc1, a_qkv_w[j].astype(BF16), seq)
            y = _swa_mixer(qkv, a_q_norm_g[j], a_k_norm_g[j], a_sinks[j], rel_bias, batch, seq)
            xf = _matmul_residual(y, a_out_w[j].astype(BF16), xf, g1, seq)
        xf = _moe(i, xf, norm2_g[i], sh2, sc2, g2, r_w[i], r_b[i], e_w1, e_b1[i], e_w2, e_b2[i], seq)
    return xf.reshape(batch, seq, d)
```

```python
import functools
import math

import jax
import jax.numpy as jnp
from jax import lax
from jax.experimental import pallas as pl
from jax.experimental.pallas import tpu as pltpu

D_MODEL = 1024
EPS = 1e-6
LANES = 128
SUBLANES = 8
ROW_TILES = D_MODEL // LANES

SSM_D_INNER = 2048
SSM_HEAD_DIM = 64
SSM_HEADS = 32
SSM_GROUPS = 4
SSM_STATE = 128
SSM_CONV = 4
SSM_CHUNK = 128
SSM_GN = SSM_GROUPS * SSM_STATE
SSM_IN_PAD = 5248
SSM_GROUP_W = SSM_D_INNER // SSM_GROUPS

ATTN_HEAD_DIM = 64
ATTN_Q_HEADS = 16
ATTN_KV_HEADS = 4
ATTN_Q_PER_KV = 4
WINDOW = 128
REL_BUCKETS = 32
REL_MAX_DIST = 128

N_EXPERTS = 32
TOP_K = 4
SWIGLU_ALPHA = 1.702
SWIGLU_LIMIT = 7.0
MOE_BLOCK = 256
MOE_TILE = 256
SPLIT_W = 256

VMEM_LIMIT = 56 * 1024 * 1024
HI = lax.Precision.HIGHEST
F32 = jnp.float32
BF16 = jnp.bfloat16


def _cparams(*sem):
    return pltpu.CompilerParams(dimension_semantics=sem, vmem_limit_bytes=VMEM_LIMIT)


def _norm_modulate(x, g, shift, scale):
    ms = jnp.mean(x * x, axis=-1, keepdims=True)
    return x * lax.rsqrt(ms + EPS) * g * (1.0 + scale) + shift


def _adaln_kernel(c_ref, w_ref, b_ref, o_ref):
    c = c_ref[...]
    c_act = c * jax.nn.sigmoid(c)
    o_ref[0] = jnp.dot(c_act, w_ref[0], preferred_element_type=F32, precision=HI) + b_ref[0]


def _adaln(c_pad, ada_w, ada_b):
    depth, d, n = ada_w.shape
    tn = 1536
    return pl.pallas_call(
        _adaln_kernel,
        grid=(depth, n // tn),
        in_specs=[
            pl.BlockSpec((SUBLANES, d), lambda i, j: (0, 0)),
            pl.BlockSpec((1, d, tn), lambda i, j: (i, 0, j)),
            pl.BlockSpec((1, 1, tn), lambda i, j: (i, 0, j)),
        ],
        out_specs=pl.BlockSpec((1, SUBLANES, tn), lambda i, j: (i, 0, j)),
        out_shape=jax.ShapeDtypeStruct((depth, SUBLANES, n), F32),
        compiler_params=_cparams("arbitrary", "arbitrary"),
        name="adaln",
    )(c_pad, ada_w, ada_b.reshape(depth, 1, n))


def _norm_matmul_kernel(x_ref, g_ref, sh_ref, sc_ref, w_ref, o_ref):
    h = _norm_modulate(x_ref[...], g_ref[...], sh_ref[0], sc_ref[0])
    o_ref[...] = jnp.dot(h.astype(BF16), w_ref[...], preferred_element_type=F32)


def _norm_matmul(x, g, shift, scale, w_bf16, seq, tm=256):
    t, d = x.shape
    n = w_bf16.shape[1]
    per_b = seq // tm
    return pl.pallas_call(
        _norm_matmul_kernel,
        grid=(t // tm,),
        in_specs=[
            pl.BlockSpec((tm, d), lambda i: (i, 0)),
            pl.BlockSpec((1, d), lambda i: (0, 0)),
            pl.BlockSpec((1, 1, d), lambda i: (i // per_b, 0, 0)),
            pl.BlockSpec((1, 1, d), lambda i: (i // per_b, 0, 0)),
            pl.BlockSpec((d, n), lambda i: (0, 0)),
        ],
        out_specs=pl.BlockSpec((tm, n), lambda i: (i, 0)),
        out_shape=jax.ShapeDtypeStruct((t, n), F32),
        compiler_params=_cparams("arbitrary"),
        name="norm_matmul",
    )(x, g.reshape(1, d), shift, scale, w_bf16)


def _matmul_residual_kernel(y_ref, w_ref, x_ref, gate_ref, o_ref):
    acc = jnp.dot(y_ref[...], w_ref[...], preferred_element_type=F32)
    o_ref[...] = x_ref[...] + gate_ref[0] * acc


def _matmul_residual(y_bf16, w_bf16, x, gate, seq, tm=512):
    t, k = y_bf16.shape
    d = x.shape[1]
    per_b = seq // tm
    return pl.pallas_call(
        _matmul_residual_kernel,
        grid=(t // tm,),
        in_specs=[
            pl.BlockSpec((tm, k), lambda i: (i, 0)),
            pl.BlockSpec((k, d), lambda i: (0, 0)),
            pl.BlockSpec((tm, d), lambda i: (i, 0)),
            pl.BlockSpec((1, 1, d), lambda i: (i // per_b, 0, 0)),
        ],
        out_specs=pl.BlockSpec((tm, d), lambda i: (i, 0)),
        out_shape=jax.ShapeDtypeStruct((t, d), F32),
        compiler_params=_cparams("arbitrary"),
        name="matmul_residual",
    )(y_bf16, w_bf16, x, gate)


def _causal_conv_silu(cur, prev_tail, w, b):
    rows = lax.broadcasted_iota(jnp.int32, (SUBLANES, cur.shape[1]), 0)
    acc = b + w[SSM_CONV - 1:SSM_CONV] * cur
    for d in range(1, SSM_CONV):
        rolled = pltpu.roll(cur, d, axis=0)
        top = jnp.where(rows < d, pltpu.roll(prev_tail, d, axis=0), rolled[0:SUBLANES])
        shifted = jnp.concatenate([top, rolled[SUBLANES:]], axis=0)
        acc = acc + w[SSM_CONV - 1 - d:SSM_CONV - d] * shifted
    return acc * jax.nn.sigmoid(acc)


def _split3(x):
    hi = x.astype(BF16)
    rem = x - hi.astype(F32)
    mid = rem.astype(BF16)
    lo = (rem - mid.astype(F32)).astype(BF16)
    return jnp.concatenate([hi, mid, lo], axis=1)


def _ssd_kernel(z_ref, xs_ref, bc_ref, dt_ref, cwx_ref, cbx_ref, cwb_ref, cbb_ref, dtb_ref, alog_ref,
                dskip_ref, ng_ref, hexp_ref, lexp_ref, o_ref, tailx_ref, tailb_ref, state_ref):
    c = pl.program_id(1)

    @pl.when(c == 0)
    def _():
        tailx_ref[...] = jnp.zeros_like(tailx_ref)
        tailb_ref[...] = jnp.zeros_like(tailb_ref)
        state_ref[...] = jnp.zeros_like(state_ref)

    xs_raw = xs_ref[...]
    bc_raw = bc_ref[...]
    xs = _causal_conv_silu(xs_raw, tailx_ref[...], cwx_ref[...], cbx_ref[...])
    bc = _causal_conv_silu(bc_raw, tailb_ref[...], cwb_ref[...], cbb_ref[...])
    tailx_ref[...] = xs_raw[SSM_CHUNK - SUBLANES:]
    tailb_ref[...] = bc_raw[SSM_CHUNK - SUBLANES:]

    dt_in = dt_ref[...][:, :SSM_HEADS] + dtb_ref[...]
    dt = jnp.maximum(dt_in, 0.0) + jnp.log1p(jnp.exp(-jnp.abs(dt_in)))
    a_neg = -jnp.exp(alog_ref[...])
    d_a = dt * a_neg
    li = lax.broadcasted_iota(jnp.int32, (SSM_CHUNK, SSM_CHUNK), 0)
    si = lax.broadcasted_iota(jnp.int32, (SSM_CHUNK, SSM_CHUNK), 1)
    causal = li >= si
    tri = causal.astype(F32)
    a_cs = jnp.dot(tri, d_a, preferred_element_type=F32, precision=HI)
    a_cs_t = lax.dot_general(d_a, tri, (((0,), (1,)), ((), ())),
                             preferred_element_type=F32, precision=HI)
    a_last = a_cs[SSM_CHUNK - 1:SSM_CHUNK]
    e_out = jnp.exp(a_cs)
    e_state = jnp.exp(a_last - a_cs) * dt
    small = jnp.concatenate([dt, e_out, e_state], axis=0)
    wide = jnp.dot(_split3(small), hexp_ref[...], preferred_element_type=F32)
    dt_w = wide[0:SSM_CHUNK]
    e_out_w = wide[SSM_CHUNK:2 * SSM_CHUNK]
    e_state_w = wide[2 * SSM_CHUNK:]
    a_col = jnp.dot(_split3(a_cs), lexp_ref[...], preferred_element_type=F32)

    x_dt = (xs * dt_w).astype(BF16)
    x_state = (xs * e_state_w).astype(BF16)
    chunk_decay_w = e_out_w[SSM_CHUNK - 1:SSM_CHUNK]

    heads_per_group = SSM_HEADS // SSM_GROUPS
    y_parts = []
    for g in range(SSM_GROUPS):
        b_g = bc[:, g * SSM_STATE:(g + 1) * SSM_STATE].astype(BF16)
        c_g = bc[:, SSM_GN + g * SSM_STATE:SSM_GN + (g + 1) * SSM_STATE].astype(BF16)
        cb = lax.dot_general(c_g, b_g, (((1,), (1,)), ((), ())), preferred_element_type=F32)
        gsl = slice(g * SSM_GROUP_W, (g + 1) * SSM_GROUP_W)
        h_prev = state_ref[g]
        y_off = jnp.dot(c_g, h_prev.astype(BF16), preferred_element_type=F32) * e_out_w[:, gsl]
        diag = []
        for r in range(heads_per_group):
            h = g * heads_per_group + r
            seg = a_col[:, h * SSM_CHUNK:(h + 1) * SSM_CHUNK] - a_cs_t[h:h + 1, :]
            decay = jnp.exp(jnp.where(causal, seg, -jnp.inf))
            m = (cb * decay).astype(BF16)
            diag.append(jnp.dot(m, x_dt[:, h * SSM_HEAD_DIM:(h + 1) * SSM_HEAD_DIM],
                                preferred_element_type=F32))
        y_parts.append(jnp.concatenate(diag, axis=1) + y_off)
        upd = lax.dot_general(b_g, x_state[:, gsl], (((0,), (0,)), ((), ())), preferred_element_type=F32)
        state_ref[g] = h_prev * chunk_decay_w[:, gsl] + upd

    y = jnp.concatenate(y_parts, axis=1) + dskip_ref[...] * xs
    z = z_ref[...]
    y = y * (z * jax.nn.sigmoid(z))
    normed = []
    for g in range(SSM_GROUPS):
        y_g = y[:, g * SSM_GROUP_W:(g + 1) * SSM_GROUP_W]
        normed.append(y_g * lax.rsqrt(jnp.mean(y_g * y_g, axis=-1, keepdims=True) + EPS))
    o_ref[...] = (jnp.concatenate(normed, axis=1) * ng_ref[...]).astype(o_ref.dtype)


def _ssd_mixer(zxbcdt, conv_w, conv_b, dt_bias, a_log, d_skip, norm_g, batch, seq):
    t = zxbcdt.shape[0]
    nc = seq // SSM_CHUNK
    head_expand = jnp.tile(jnp.repeat(jnp.eye(SSM_HEADS, dtype=BF16), SSM_HEAD_DIM, axis=1), (3, 1))
    lane_expand = jnp.tile(jnp.repeat(jnp.eye(SSM_HEADS, dtype=BF16), SSM_CHUNK, axis=1), (3, 1))
    row = lambda b, c: (b * nc + c, 0)
    const2 = lambda b, c: (0, 0)
    bc_w = 2 * SSM_GN
    return pl.pallas_call(
        _ssd_kernel,
        grid=(batch, nc),
        in_specs=[
            pl.BlockSpec((SSM_CHUNK, SSM_D_INNER), row),
            pl.BlockSpec((SSM_CHUNK, SSM_D_INNER), lambda b, c: (b * nc + c, 1)),
            pl.BlockSpec((SSM_CHUNK, bc_w), lambda b, c: (b * nc + c, 2 * SSM_D_INNER // bc_w)),
            pl.BlockSpec((SSM_CHUNK, LANES), lambda b, c: (b * nc + c, (2 * SSM_D_INNER + bc_w) // LANES)),
            pl.BlockSpec((SSM_CONV, SSM_D_INNER), const2),
            pl.BlockSpec((1, SSM_D_INNER), const2),
            pl.BlockSpec((SSM_CONV, bc_w), const2),
            pl.BlockSpec((1, bc_w), const2),
            pl.BlockSpec((1, SSM_HEADS), const2),
            pl.BlockSpec((1, SSM_HEADS), const2),
            pl.BlockSpec((1, SSM_D_INNER), const2),
            pl.BlockSpec((1, SSM_D_INNER), const2),
            pl.BlockSpec((3 * SSM_HEADS, SSM_D_INNER), const2),
            pl.BlockSpec((3 * SSM_HEADS, SSM_HEADS * SSM_CHUNK), const2),
        ],
        out_specs=pl.BlockSpec((SSM_CHUNK, SSM_D_INNER), row),
        out_shape=jax.ShapeDtypeStruct((t, SSM_D_INNER), BF16),
        scratch_shapes=[
            pltpu.VMEM((SUBLANES, SSM_D_INNER), F32),
            pltpu.VMEM((SUBLANES, bc_w), F32),
            pltpu.VMEM((SSM_GROUPS, SSM_STATE, SSM_GROUP_W), F32),
        ],
        compiler_params=_cparams("arbitrary", "arbitrary"),
        name="ssd_mixer",
    )(zxbcdt, zxbcdt, zxbcdt, zxbcdt,
      conv_w[:, :SSM_D_INNER], conv_b[:SSM_D_INNER].reshape(1, -1),
      conv_w[:, SSM_D_INNER:], conv_b[SSM_D_INNER:].reshape(1, -1),
      dt_bias.reshape(1, -1), a_log.reshape(1, -1),
      jnp.repeat(d_skip, SSM_HEAD_DIM).reshape(1, -1), norm_g.reshape(1, -1),
      head_expand, lane_expand)


def _head_rms(x, g):
    return x * lax.rsqrt(jnp.mean(x * x, axis=-1, keepdims=True) + EPS) * g


def _swa_kernel(q_ref, kvc_ref, kvp_ref, bucket_ref, qg_ref, kg_ref, rel_ref, sink_ref, o_ref,
                bias_ref, sinkrow_ref):
    b = pl.program_id(0)
    i = pl.program_id(1)

    @pl.when((b == 0) & (i == 0))
    def _():
        bucket = bucket_ref[...]
        kj = lax.broadcasted_iota(jnp.int32, (2 * WINDOW, WINDOW), 0)
        qi = lax.broadcasted_iota(jnp.int32, (2 * WINDOW, WINDOW), 1)
        dist = qi + WINDOW - kj
        band = (dist >= 0) & (dist < WINDOW)
        for h in range(ATTN_Q_HEADS):
            g, r = divmod(h, ATTN_Q_PER_KV)
            acc = jnp.zeros(bucket.shape, F32)
            for k in range(REL_BUCKETS):
                acc = jnp.where(bucket == k, rel_ref[k, h], acc)
            cols = slice(r * WINDOW, (r + 1) * WINDOW)
            bias_ref[0, g, :, cols] = jnp.where(band, acc, -jnp.inf)
            bias_ref[1, g, :, cols] = jnp.where(band & (kj >= WINDOW), acc, -jnp.inf)
            sinkrow_ref[g, :, cols] = jnp.full((1, WINDOW), sink_ref[h], F32)

    first = (i == 0).astype(jnp.int32)
    q_t = q_ref[...].T
    kv_c = kvc_ref[...]
    kv_p = kvp_ref[...]
    kv_w = ATTN_KV_HEADS * ATTN_HEAD_DIM
    q_gain = qg_ref[...]
    outs = []
    for g in range(ATTN_KV_HEADS):
        ksl = slice(g * ATTN_HEAD_DIM, (g + 1) * ATTN_HEAD_DIM)
        vsl = slice(kv_w + g * ATTN_HEAD_DIM, kv_w + (g + 1) * ATTN_HEAD_DIM)
        k = jnp.concatenate([kv_p[:, ksl], kv_c[:, ksl]], axis=0)
        v = jnp.concatenate([kv_p[:, vsl], kv_c[:, vsl]], axis=0).astype(BF16)
        k = _head_rms(k, kg_ref[...]).astype(BF16)
        q_heads = []
        for r in range(ATTN_Q_PER_KV):
            h = g * ATTN_Q_PER_KV + r
            q_h = q_t[h * ATTN_HEAD_DIM:(h + 1) * ATTN_HEAD_DIM]
            inv = lax.rsqrt(jnp.mean(q_h * q_h, axis=0, keepdims=True) + EPS)
            q_heads.append((q_h * inv * q_gain).astype(BF16))
        q = jnp.concatenate(q_heads, axis=1)
        s = jnp.dot(k, q, preferred_element_type=F32) + bias_ref[first, g]
        sink = sinkrow_ref[g]
        m = jnp.maximum(jnp.max(s, axis=0, keepdims=True), sink)
        p = jnp.exp(s - m)
        denom = jnp.sum(p, axis=0, keepdims=True) + jnp.exp(sink - m)
        pv = lax.dot_general(v, p.astype(BF16), (((0,), (0,)), ((), ())), preferred_element_type=F32)
        pv = pv * (1.0 / denom)
        outs.extend(pv[:, r * WINDOW:(r + 1) * WINDOW] for r in range(ATTN_Q_PER_KV))
    o_ref[...] = jnp.concatenate(outs, axis=0).T.astype(o_ref.dtype)


def _t5_causal_bucket(dist):
    max_exact = REL_BUCKETS // 2
    d = jnp.maximum(dist, 1).astype(F32)
    large = max_exact + (jnp.log(d / max_exact) / math.log(REL_MAX_DIST / max_exact)
                         * (REL_BUCKETS - max_exact)).astype(jnp.int32)
    large = jnp.minimum(large, REL_BUCKETS - 1)
    return jnp.where(dist < max_exact, dist, large)


def _swa_mixer(qkv, q_norm_g, k_norm_g, sinks, rel_bias, batch, seq):
    t = qkv.shape[0]
    nb = seq // WINDOW
    q_w = ATTN_Q_HEADS * ATTN_HEAD_DIM
    kv_w2 = 2 * ATTN_KV_HEADS * ATTN_HEAD_DIM
    kj = jnp.arange(2 * WINDOW)[:, None]
    qi = jnp.arange(WINDOW)[None, :]
    bucket = _t5_causal_bucket(jnp.maximum(qi + WINDOW - kj, 0)).astype(jnp.int32)
    q_gain = jnp.broadcast_to((q_norm_g * (ATTN_HEAD_DIM ** -0.5))[:, None], (ATTN_HEAD_DIM, WINDOW))
    const2 = lambda b, i: (0, 0)
    smem = pl.BlockSpec(memory_space=pltpu.SMEM)
    return pl.pallas_call(
        _swa_kernel,
        grid=(batch, nb),
        in_specs=[
            pl.BlockSpec((WINDOW, q_w), lambda b, i: (b * nb + i, 0)),
            pl.BlockSpec((WINDOW, kv_w2), lambda b, i: (b * nb + i, q_w // kv_w2)),
            pl.BlockSpec((WINDOW, kv_w2), lambda b, i: (b * nb + jnp.maximum(i - 1, 0), q_w // kv_w2)),
            pl.BlockSpec((2 * WINDOW, WINDOW), const2),
            pl.BlockSpec((ATTN_HEAD_DIM, WINDOW), const2),
            pl.BlockSpec((1, ATTN_HEAD_DIM), const2),
            smem,
            smem,
        ],
        out_specs=pl.BlockSpec((WINDOW, q_w), lambda b, i: (b * nb + i, 0)),
        out_shape=jax.ShapeDtypeStruct((t, q_w), BF16),
        scratch_shapes=[pltpu.VMEM((2, ATTN_KV_HEADS, 2 * WINDOW, ATTN_Q_PER_KV * WINDOW), F32),
                        pltpu.VMEM((ATTN_KV_HEADS, 1, ATTN_Q_PER_KV * WINDOW), F32)],
        compiler_params=_cparams("arbitrary", "arbitrary"),
        name="swa_mixer",
    )(qkv, qkv, qkv, bucket, q_gain, k_norm_g.reshape(1, -1), rel_bias, sinks)


def _router_kernel(x_ref, g_ref, sh_ref, sc_ref, rwt_ref, rb_ref,
                   h_ref, lp_ref, gate_ref, tile_n_ref, tile_carry_ref, cnt_ref, carry_ref):
    i = pl.program_id(0)
    tm = x_ref.shape[0]

    @pl.when(i == 0)
    def _():
        carry_ref[...] = jnp.zeros_like(carry_ref)

    h = _norm_modulate(x_ref[...], g_ref[...], sh_ref[0], sc_ref[0])
    h_ref[...] = h.astype(h_ref.dtype)

    logits = lax.dot_general(rwt_ref[...], h, (((1,), (1,)), ((), ())),
                             preferred_element_type=F32, precision=HI) + rb_ref[...]
    e_iota = lax.broadcasted_iota(jnp.int32, logits.shape, 0)
    work = logits
    sels, vals = [], []
    for k in range(TOP_K):
        m = jnp.max(work, axis=0, keepdims=True)
        idx = jnp.min(jnp.where(work == m, e_iota, N_EXPERTS), axis=0, keepdims=True)
        sel = e_iota == idx
        work = jnp.where(sel, -jnp.inf, work)
        sels.append(sel)
        vals.append(m)
    exps = [jnp.exp(v - vals[0]) for v in vals]
    denom = exps[0] + exps[1] + exps[2] + exps[3]
    for k in range(TOP_K):
        gate_ref[k:k + 1, :] = exps[k] / denom

    chosen = sels[0] | sels[1] | sels[2] | sels[3]
    t_row = lax.broadcasted_iota(jnp.int32, (tm, tm), 0)
    t_col = lax.broadcasted_iota(jnp.int32, (tm, tm), 1)
    before = (t_row < t_col).astype(BF16)
    prior = jnp.dot(chosen.astype(BF16), before, preferred_element_type=F32)
    n = jnp.sum(chosen.astype(F32), axis=1, keepdims=True)
    e_row = lax.broadcasted_iota(jnp.int32, (N_EXPERTS, N_EXPERTS), 0)
    e_col = lax.broadcasted_iota(jnp.int32, (N_EXPERTS, N_EXPERTS), 1)
    run_start = jnp.dot((e_col < e_row).astype(F32), jnp.broadcast_to(n, (N_EXPERTS, LANES)),
                        preferred_element_type=F32, precision=HI)[:, :1]
    local = run_start + prior
    for k in range(TOP_K):
        lp_ref[k:k + 1, :] = jnp.sum(jnp.where(sels[k], local, 0.0), axis=0, keepdims=True).astype(jnp.int32)
    tile_n_ref[0] = n.astype(jnp.int32)
    tile_carry_ref[0] = carry_ref[...].astype(jnp.int32)
    total = carry_ref[...] + n
    carry_ref[...] = total
    cnt_ref[...] = total.astype(jnp.int32)


def _router(x, g, shift, scale, r_w, r_b, seq, tm):
    t, d = x.shape
    per_b = seq // tm
    ntiles = t // tm
    return pl.pallas_call(
        _router_kernel,
        grid=(t // tm,),
        in_specs=[
            pl.BlockSpec((tm, d), lambda i: (i, 0)),
            pl.BlockSpec((1, d), lambda i: (0, 0)),
            pl.BlockSpec((1, 1, d), lambda i: (i // per_b, 0, 0)),
            pl.BlockSpec((1, 1, d), lambda i: (i // per_b, 0, 0)),
            pl.BlockSpec((N_EXPERTS, d), lambda i: (0, 0)),
            pl.BlockSpec((N_EXPERTS, 1), lambda i: (0, 0)),
        ],
        out_specs=[
            pl.BlockSpec((tm, d), lambda i: (i, 0)),
            pl.BlockSpec((TOP_K, tm), lambda i: (0, i)),
            pl.BlockSpec((TOP_K, tm), lambda i: (0, i)),
            pl.BlockSpec((1, N_EXPERTS, 1), lambda i: (i, 0, 0)),
            pl.BlockSpec((1, N_EXPERTS, 1), lambda i: (i, 0, 0)),
            pl.BlockSpec((N_EXPERTS, 1), lambda i: (0, 0)),
        ],
        out_shape=[
            jax.ShapeDtypeStruct((t, d), BF16),
            jax.ShapeDtypeStruct((TOP_K, t), jnp.int32),
            jax.ShapeDtypeStruct((TOP_K, t), F32),
            jax.ShapeDtypeStruct((ntiles, N_EXPERTS, 1), jnp.int32),
            jax.ShapeDtypeStruct((ntiles, N_EXPERTS, 1), jnp.int32),
            jax.ShapeDtypeStruct((N_EXPERTS, 1), jnp.int32),
        ],
        scratch_shapes=[pltpu.VMEM((N_EXPERTS, 1), F32)],
        compiler_params=_cparams("arbitrary"),
        name="moe_router",
    )(x, g.reshape(1, d), shift, scale, r_w.T, r_b.reshape(-1, 1))


def _run_copies(n, src_ref, src_row, dst_ref, dst_row, sem, max_rows, fn):
    for b in range(max_rows.bit_length()):
        size = 1 << b

        @pl.when((n & size) != 0)
        def _():
            lo = n & (size - 1)
            fn(pltpu.make_async_copy(
                src_ref.at[pl.ds(pl.multiple_of((src_row + lo) * ROW_TILES, ROW_TILES), size * ROW_TILES)],
                dst_ref.at[pl.ds(pl.multiple_of((dst_row + lo) * ROW_TILES, ROW_TILES), size * ROW_TILES)],
                sem))


def _tile_runs(tile, n_ref, carry_ref, gstart_ref, sorted_ref, grouped_ref, sem, tm, fn, to_grouped):
    def body(e, run_start):
        n = n_ref[tile * N_EXPERTS + e]
        slot0 = gstart_ref[e] + carry_ref[tile * N_EXPERTS + e]
        if to_grouped:
            _run_copies(n, sorted_ref, run_start, grouped_ref, slot0, sem, tm, fn)
        else:
            _run_copies(n, grouped_ref, slot0, sorted_ref, run_start, sem, tm, fn)
        return run_start + n

    lax.fori_loop(0, N_EXPERTS, body, 0)


def _row_copy(src_ref, src_row, dst_ref, dst_row, sem):
    return pltpu.make_async_copy(
        src_ref.at[pl.ds(pl.multiple_of(src_row * ROW_TILES, ROW_TILES), ROW_TILES)],
        dst_ref.at[pl.ds(pl.multiple_of(dst_row * ROW_TILES, ROW_TILES), ROW_TILES)],
        sem)


def _block_copy(src_ref, dst_ref, dst_blk, sem):
    rows = MOE_BLOCK * ROW_TILES
    return pltpu.make_async_copy(src_ref, dst_ref.at[pl.ds(pl.multiple_of(dst_blk * rows, rows), rows)], sem)


def _dispatch_kernel(n_ref, carry_ref, gstart_ref, pad_lo_ref, pad_hi_ref, nused_ref, lp_ref, h_ref, xg_ref,
                     sorted_ref, zero_ref, run_sem, sem):
    i = pl.program_id(0)
    last = pl.num_programs(0) - 1
    tm = h_ref.shape[0]
    rows = TOP_K * tm
    slot = i % 2
    nblk = xg_ref.shape[0] // (MOE_BLOCK * ROW_TILES)

    @pl.when(i == 0)
    def _():
        zero_ref[...] = jnp.zeros_like(zero_ref)

        def fill(b, carry):
            _block_copy(zero_ref, xg_ref, b, sem).start()
            return carry

        def fill_wait(b, carry):
            _block_copy(zero_ref, xg_ref, b, sem).wait()
            return carry

        lax.fori_loop(nused_ref[0], nblk, fill, 0)
        lax.fori_loop(nused_ref[0], nblk, fill_wait, 0)

    @pl.when(i < N_EXPERTS)
    def _():
        def pad(r, carry):
            _row_copy(zero_ref, 0, xg_ref, r, sem).start()
            return carry

        def pad_wait(r, carry):
            _row_copy(zero_ref, 0, xg_ref, r, sem).wait()
            return carry

        lax.fori_loop(pad_lo_ref[i], pad_hi_ref[i], pad, 0)
        lax.fori_loop(pad_lo_ref[i], pad_hi_ref[i], pad_wait, 0)

    row_id = lax.broadcasted_iota(jnp.int32, (rows, tm), 0)
    perm = row_id == lp_ref[0:1, :]
    for k in range(1, TOP_K):
        perm = perm | (row_id == lp_ref[k:k + 1, :])
    srt = jnp.dot(jnp.where(perm, 1.0, 0.0).astype(BF16), h_ref[...], preferred_element_type=F32)
    for s in range(ROW_TILES):
        sorted_ref[slot, pl.ds(s, rows, stride=ROW_TILES), :] = srt[:, s * LANES:(s + 1) * LANES]

    def runs(tile, buf, fn):
        _tile_runs(tile, n_ref, carry_ref, gstart_ref, sorted_ref.at[buf], xg_ref, run_sem, tm, fn, True)

    def wait_tile(buf):
        pltpu.make_async_copy(sorted_ref.at[buf], xg_ref.at[pl.ds(0, rows * ROW_TILES)], run_sem).wait()

    @pl.when(i > 0)
    def _():
        wait_tile(1 - slot)

    runs(i, slot, lambda cp: cp.start())

    @pl.when(i == last)
    def _():
        wait_tile(slot)


def _dispatch(tile_n, tile_carry, group_start, pad_lo, pad_hi, nused, lp_t, h, cap, tm):
    t, d = h.shape
    assert t // tm >= N_EXPERTS
    grid_spec = pltpu.PrefetchScalarGridSpec(
        num_scalar_prefetch=6,
        grid=(t // tm,),
        in_specs=[
            pl.BlockSpec((TOP_K, tm), lambda i, *_: (0, i)),
            pl.BlockSpec((tm, d), lambda i, *_: (i, 0)),
        ],
        out_specs=pl.BlockSpec(memory_space=pl.ANY),
        scratch_shapes=[
            pltpu.VMEM((2, TOP_K * tm * ROW_TILES, LANES), F32),
            pltpu.VMEM((MOE_BLOCK * ROW_TILES, LANES), F32),
            pltpu.SemaphoreType.DMA, pltpu.SemaphoreType.DMA,
        ],
    )
    return pl.pallas_call(
        _dispatch_kernel,
        grid_spec=grid_spec,
        out_shape=jax.ShapeDtypeStruct((cap * ROW_TILES, LANES), F32),
        compiler_params=pltpu.CompilerParams(dimension_semantics=("arbitrary",), has_side_effects=True,
                                             vmem_limit_bytes=VMEM_LIMIT),
        name="moe_dispatch",
    )(tile_n, tile_carry, group_start, pad_lo, pad_hi, nused, lp_t, h)


def _weight_copies(w1_hbm, w2_hbm, w1buf, w2buf, sems, layer, e, slot):
    return (pltpu.make_async_copy(w1_hbm.at[layer, e], w1buf.at[slot], sems.at[0, slot]),
            pltpu.make_async_copy(w2_hbm.at[layer, e], w2buf.at[slot], sems.at[1, slot]))


def _expert_kernel(blk_e_ref, nused_ref, next_e_ref, slot_ref, x_ref, w1_hbm, w2_hbm, bg_ref, bl_ref, b2_ref,
                   sel_ref, o_ref, w1buf, w2buf, wg_s, wl_s, w2_s, sems, *, layer):
    blk = pl.program_id(0)
    active = blk < nused_ref[0]
    e = blk_e_ref[blk]
    slot = slot_ref[e]
    new_expert = (blk == 0) | (e != blk_e_ref[jnp.maximum(blk - 1, 0)])
    copies = functools.partial(_weight_copies, w1_hbm, w2_hbm, w1buf, w2buf, sems, layer)

    @pl.when(blk == 0)
    def _():
        for cp in copies(e, slot):
            cp.start()

    @pl.when(active & new_expert)
    def _():
        nxt = next_e_ref[e]

        @pl.when(nxt >= 0)
        def _():
            for cp in copies(nxt, 1 - slot):
                cp.start()

        for cp in copies(e, slot):
            cp.wait()
        half = SPLIT_W // 2
        for j in range(w1buf.shape[2] // SPLIT_W):
            chunk = w1buf[slot, :, j * SPLIT_W:(j + 1) * SPLIT_W].astype(BF16)
            split = jnp.dot(chunk, sel_ref[...], preferred_element_type=F32)
            wg_s[:, j * half:(j + 1) * half] = split[:, :half].astype(BF16)
            wl_s[:, j * half:(j + 1) * half] = split[:, half:].astype(BF16)
        w2_s[...] = w2buf[slot].astype(BF16)

    @pl.when(active)
    def _():
        x = jnp.concatenate(
            [x_ref[pl.ds(s, MOE_BLOCK, stride=ROW_TILES), :] for s in range(ROW_TILES)], axis=1).astype(BF16)
        glu = jnp.dot(x, wg_s[...], preferred_element_type=F32) + bg_ref[0]
        lin = jnp.dot(x, wl_s[...], preferred_element_type=F32) + bl_ref[0]
        glu = jnp.minimum(glu, SWIGLU_LIMIT)
        lin = jnp.clip(lin, -SWIGLU_LIMIT, SWIGLU_LIMIT)
        act = glu * jax.nn.sigmoid(SWIGLU_ALPHA * glu) * (lin + 1.0)
        y = jnp.dot(act.astype(BF16), w2_s[...], preferred_element_type=F32) + b2_ref[0]
        for s in range(ROW_TILES):
            o_ref[pl.ds(s, MOE_BLOCK, stride=ROW_TILES), :] = y[:, s * LANES:(s + 1) * LANES]

    @pl.when(jnp.logical_not(active))
    def _():
        o_ref[...] = jnp.zeros_like(o_ref)


def _experts(layer, blk_e, nused, next_e, e_slot, xg_rows, w1_all, w2_all, b_glu, b_lin, b2):
    nblk = blk_e.shape[0]
    d = D_MODEL
    ff = w2_all.shape[2]
    col = jnp.arange(SPLIT_W)
    sel = (jnp.arange(SPLIT_W)[None, :] == ((col % 2) * (SPLIT_W // 2) + col // 2)[:, None]).astype(BF16)

    def blk_map(b, blk_e_ref, nused_ref, *_):
        return (jnp.minimum(b, nused_ref[0] - 1), 0)

    def e_map(b, blk_e_ref, nused_ref, *_):
        return (blk_e_ref[jnp.minimum(b, nused_ref[0] - 1)], 0, 0)

    grid_spec = pltpu.PrefetchScalarGridSpec(
        num_scalar_prefetch=4,
        grid=(nblk,),
        in_specs=[
            pl.BlockSpec((MOE_BLOCK * ROW_TILES, LANES), blk_map),
            pl.BlockSpec(memory_space=pl.ANY),
            pl.BlockSpec(memory_space=pl.ANY),
            pl.BlockSpec((1, 1, ff), e_map),
            pl.BlockSpec((1, 1, ff), e_map),
            pl.BlockSpec((1, 1, d), e_map),
            pl.BlockSpec((SPLIT_W, SPLIT_W), lambda b, *_: (0, 0)),
        ],
        out_specs=pl.BlockSpec((MOE_BLOCK * ROW_TILES, LANES), lambda b, *_: (b, 0)),
        scratch_shapes=[
            pltpu.VMEM((2, d, 2 * ff), F32), pltpu.VMEM((2, ff, d), F32),
            pltpu.VMEM((d, ff), BF16), pltpu.VMEM((d, ff), BF16), pltpu.VMEM((ff, d), BF16),
            pltpu.SemaphoreType.DMA((2, 2)),
        ],
    )
    return pl.pallas_call(
        functools.partial(_expert_kernel, layer=layer),
        grid_spec=grid_spec,
        out_shape=jax.ShapeDtypeStruct(xg_rows.shape, F32),
        compiler_params=_cparams("arbitrary"),
        name="moe_experts",
    )(blk_e, nused, next_e, e_slot, xg_rows, w1_all, w2_all, b_glu, b_lin, b2, sel)


def _combine_kernel(n_ref, carry_ref, gstart_ref, yg_ref, lp_ref, gate_ref, x_ref, g2_ref, o_ref, sorted_ref, sem):
    i = pl.program_id(0)
    ntiles = pl.num_programs(0)
    tm = x_ref.shape[0]
    rows = TOP_K * tm
    slot = i % 2

    def runs(tile, buf, fn):
        _tile_runs(tile, n_ref, carry_ref, gstart_ref, sorted_ref.at[buf], yg_ref, sem.at[buf], tm, fn, False)

    @pl.when(i == 0)
    def _():
        runs(0, 0, lambda cp: cp.start())

    @pl.when(i + 1 < ntiles)
    def _():
        runs(i + 1, 1 - slot, lambda cp: cp.start())

    eye = (lax.broadcasted_iota(jnp.int32, (tm, tm), 0) ==
           lax.broadcasted_iota(jnp.int32, (tm, tm), 1)).astype(F32)
    rows_t = jnp.concatenate([gate_ref[...], lp_ref[...].astype(F32)], axis=0)
    cols = lax.dot_general(eye, rows_t, (((1,), (1,)), ((), ())), preferred_element_type=F32, precision=HI)
    col_id = lax.broadcasted_iota(jnp.int32, (tm, rows), 1)
    weights = jnp.zeros((tm, rows), F32)
    for k in range(TOP_K):
        lp_col = (cols[:, TOP_K + k:TOP_K + k + 1] + 0.5).astype(jnp.int32)
        weights = weights + jnp.where(col_id == lp_col, cols[:, k:k + 1], 0.0)

    pltpu.make_async_copy(yg_ref.at[pl.ds(0, rows * ROW_TILES)], sorted_ref.at[slot], sem.at[slot]).wait()
    y = jnp.concatenate(
        [sorted_ref[slot, pl.ds(s, rows, stride=ROW_TILES), :] for s in range(ROW_TILES)], axis=1).astype(BF16)
    acc = jnp.dot(weights.astype(BF16), y, preferred_element_type=F32)
    o_ref[...] = x_ref[...] + g2_ref[0] * acc


def _combine(tile_n, tile_carry, group_start, yg_rows, lp_t, gate_t, x, gate2, seq, tm):
    t, d = x.shape
    per_b = seq // tm
    grid_spec = pltpu.PrefetchScalarGridSpec(
        num_scalar_prefetch=3,
        grid=(t // tm,),
        in_specs=[
            pl.BlockSpec(memory_space=pl.ANY),
            pl.BlockSpec((TOP_K, tm), lambda i, *_: (0, i)),
            pl.BlockSpec((TOP_K, tm), lambda i, *_: (0, i)),
            pl.BlockSpec((tm, d), lambda i, *_: (i, 0)),
            pl.BlockSpec((1, 1, d), lambda i, *_: (i // per_b, 0, 0)),
        ],
        out_specs=pl.BlockSpec((tm, d), lambda i, *_: (i, 0)),
        scratch_shapes=[pltpu.VMEM((2, TOP_K * tm * ROW_TILES, LANES), F32), pltpu.SemaphoreType.DMA((2,))],
    )
    return pl.pallas_call(
        _combine_kernel,
        grid_spec=grid_spec,
        out_shape=jax.ShapeDtypeStruct((t, d), F32),
        compiler_params=_cparams("arbitrary"),
        name="moe_combine",
    )(tile_n, tile_carry, group_start, yg_rows, lp_t, gate_t, x, gate2)


def _moe(layer, x, g, shift, scale, gate2, r_w, r_b, w1_all, b1, w2_all, b2, seq):
    t = x.shape[0]
    h, lp_t, gate_t, tile_n, tile_carry, counts = _router(x, g, shift, scale, r_w, r_b, seq, MOE_TILE)
    tile_n = tile_n.reshape(-1)
    tile_carry = tile_carry.reshape(-1)
    counts = counts.reshape(-1)
    padded = ((counts + MOE_BLOCK - 1) // MOE_BLOCK) * MOE_BLOCK
    group_end = jnp.cumsum(padded)
    group_start = (group_end - padded).astype(jnp.int32)
    cap = t * TOP_K + N_EXPERTS * MOE_BLOCK
    nblk = cap // MOE_BLOCK
    blk_start = jnp.arange(nblk, dtype=jnp.int32) * MOE_BLOCK
    blk_e = jnp.minimum(jnp.sum(blk_start[:, None] >= group_end[None, :], axis=1), N_EXPERTS - 1).astype(jnp.int32)
    nused = (group_end[-1:] // MOE_BLOCK).astype(jnp.int32)

    pad_lo = (group_start + counts).astype(jnp.int32)
    pad_hi = group_end.astype(jnp.int32)
    xg_rows = _dispatch(tile_n, tile_carry, group_start, pad_lo, pad_hi, nused, lp_t, h, cap, MOE_TILE)
    nonempty = counts > 0
    e_ids = jnp.arange(N_EXPERTS, dtype=jnp.int32)
    later = jnp.where(nonempty[None, :] & (e_ids[None, :] > e_ids[:, None]), e_ids[None, :], N_EXPERTS)
    next_e = jnp.min(later, axis=1)
    next_e = jnp.where(next_e == N_EXPERTS, -1, next_e).astype(jnp.int32)
    e_slot = ((jnp.cumsum(nonempty.astype(jnp.int32)) - 1) % 2).astype(jnp.int32)
    yg_rows = _experts(layer, blk_e, nused, next_e, e_slot, xg_rows, w1_all, w2_all,
                       b1[:, None, 0::2], b1[:, None, 1::2], b2[:, None, :])
    return _combine(tile_n, tile_carry, group_start, yg_rows, lp_t, gate_t, x, gate2, seq, MOE_TILE)


def kernel(x, c, ada_w, ada_b, norm1_g, norm2_g, m_in_w, m_conv_w, m_conv_b, m_dt_bias, m_A_log, m_D, m_norm_g, m_out_w, a_qkv_w, a_q_norm_g, a_k_norm_g, a_sinks, a_out_w, rel_bias, r_w, r_b, e_w1, e_b1, e_w2, e_b2):
    batch, seq, d = x.shape
    depth = ada_w.shape[0]
    t = batch * seq
    xf = x.reshape(t, d)

    c_pad = jnp.zeros((SUBLANES, d), F32).at[:batch].set(c)
    mod = _adaln(c_pad, ada_w, ada_b)[:, :batch]

    for i in range(depth):
        parts = [mod[i, :, p * d:(p + 1) * d].reshape(batch, 1, d) for p in range(6)]
        sh1, sc1, g1, sh2, sc2, g2 = parts
        j = i // 2
        if i % 2 == 0:
            w_in = jnp.pad(m_in_w[j], ((0, 0), (0, SSM_IN_PAD - m_in_w.shape[2]))).astype(BF16)
            zxbcdt = _norm_matmul(xf, norm1_g[i], sh1, sc1, w_in, seq)
            y = _ssd_mixer(zxbcdt, m_conv_w[j], m_conv_b[j], m_dt_bias[j], m_A_log[j], m_D[j],
                           m_norm_g[j], batch, seq)
            xf = _matmul_residual(y, m_out_w[j].astype(BF16), xf, g1, seq)
        else:
            qkv = _norm_matmul(xf, norm1_g[i], sh1, sc1, a_qkv_w[j].astype(BF16), seq)
            y = _swa_mixer(qkv, a_q_norm_g[j], a_k_norm_g[j], a_sinks[j], rel_bias, batch, seq)
            xf = _matmul_residual(y, a_out_w[j].astype(BF16), xf, g1, seq)
        xf = _moe(i, xf, norm2_g[i], sh2, sc2, g2, r_w[i], r_b[i], e_w1, e_b1[i], e_w2, e_b2[i], seq)
    return xf.reshape(batch, seq, d)
```

```python
import functools
import math

import jax
import jax.numpy as jnp
from jax import lax
from jax.experimental import pallas as pl
from jax.experimental.pallas import tpu as pltpu

D_MODEL = 1024
EPS = 1e-6
LANES = 128
SUBLANES = 8
ROW_TILES = D_MODEL // LANES

SSM_D_INNER = 2048
SSM_HEAD_DIM = 64
SSM_HEADS = 32
SSM_GROUPS = 4
SSM_STATE = 128
SSM_CONV = 4
SSM_CHUNK = 128
SSM_GN = SSM_GROUPS * SSM_STATE
SSM_IN_PAD = 5248
SSM_GROUP_W = SSM_D_INNER // SSM_GROUPS

ATTN_HEAD_DIM = 64
ATTN_Q_HEADS = 16
ATTN_KV_HEADS = 4
ATTN_Q_PER_KV = 4
WINDOW = 128
REL_BUCKETS = 32
REL_MAX_DIST = 128

N_EXPERTS = 32
TOP_K = 4
SWIGLU_ALPHA = 1.702
SWIGLU_LIMIT = 7.0
MOE_BLOCK = 512
MOE_SUB = 256
MOE_TILE = 256
SPLIT_W = 256

VMEM_LIMIT = 56 * 1024 * 1024
HI = lax.Precision.HIGHEST
F32 = jnp.float32
BF16 = jnp.bfloat16


def _cparams(*sem):
    return pltpu.CompilerParams(dimension_semantics=sem, vmem_limit_bytes=VMEM_LIMIT)


def _norm_modulate(x, g, shift, scale):
    ms = jnp.mean(x * x, axis=-1, keepdims=True)
    return x * lax.rsqrt(ms + EPS) * g * (1.0 + scale) + shift


def _adaln_kernel(c_ref, w_ref, b_ref, o_ref):
    c = c_ref[...]
    c_act = c * jax.nn.sigmoid(c)
    o_ref[0] = jnp.dot(c_act, w_ref[0], preferred_element_type=F32, precision=HI) + b_ref[0]


def _adaln(c_pad, ada_w, ada_b):
    depth, d, n = ada_w.shape
    tn = 1536
    return pl.pallas_call(
        _adaln_kernel,
        grid=(depth, n // tn),
        in_specs=[
            pl.BlockSpec((SUBLANES, d), lambda i, j: (0, 0)),
            pl.BlockSpec((1, d, tn), lambda i, j: (i, 0, j)),
            pl.BlockSpec((1, 1, tn), lambda i, j: (i, 0, j)),
        ],
        out_specs=pl.BlockSpec((1, SUBLANES, tn), lambda i, j: (i, 0, j)),
        out_shape=jax.ShapeDtypeStruct((depth, SUBLANES, n), F32),
        compiler_params=_cparams("arbitrary", "arbitrary"),
        name="adaln",
    )(c_pad, ada_w, ada_b.reshape(depth, 1, n))


def _norm_matmul_kernel(x_ref, g_ref, sh_ref, sc_ref, w_ref, o_ref):
    h = _norm_modulate(x_ref[...], g_ref[...], sh_ref[0], sc_ref[0])
    o_ref[...] = jnp.dot(h.astype(BF16), w_ref[...], preferred_element_type=F32)


def _norm_matmul(x, g, shift, scale, w_bf16, seq, tm=256):
    t, d = x.shape
    n = w_bf16.shape[1]
    per_b = seq // tm
    return pl.pallas_call(
        _norm_matmul_kernel,
        grid=(t // tm,),
        in_specs=[
            pl.BlockSpec((tm, d), lambda i: (i, 0)),
            pl.BlockSpec((1, d), lambda i: (0, 0)),
            pl.BlockSpec((1, 1, d), lambda i: (i // per_b, 0, 0)),
            pl.BlockSpec((1, 1, d), lambda i: (i // per_b, 0, 0)),
            pl.BlockSpec((d, n), lambda i: (0, 0)),
        ],
        out_specs=pl.BlockSpec((tm, n), lambda i: (i, 0)),
        out_shape=jax.ShapeDtypeStruct((t, n), F32),
        compiler_params=_cparams("arbitrary"),
        name="norm_matmul",
    )(x, g.reshape(1, d), shift, scale, w_bf16)


def _matmul_residual_kernel(y_ref, w_ref, x_ref, gate_ref, o_ref):
    acc = jnp.dot(y_ref[...], w_ref[...], preferred_element_type=F32)
    o_ref[...] = x_ref[...] + gate_ref[0] * acc


def _matmul_residual(y_bf16, w_bf16, x, gate, seq, tm=512):
    t, k = y_bf16.shape
    d = x.shape[1]
    per_b = seq // tm
    return pl.pallas_call(
        _matmul_residual_kernel,
        grid=(t // tm,),
        in_specs=[
            pl.BlockSpec((tm, k), lambda i: (i, 0)),
            pl.BlockSpec((k, d), lambda i: (0, 0)),
            pl.BlockSpec((tm, d), lambda i: (i, 0)),
            pl.BlockSpec((1, 1, d), lambda i: (i // per_b, 0, 0)),
        ],
        out_specs=pl.BlockSpec((tm, d), lambda i: (i, 0)),
        out_shape=jax.ShapeDtypeStruct((t, d), F32),
        compiler_params=_cparams("arbitrary"),
        name="matmul_residual",
    )(y_bf16, w_bf16, x, gate)


def _causal_conv_silu(cur, prev_tail, w, b):
    rows = lax.broadcasted_iota(jnp.int32, (SUBLANES, cur.shape[1]), 0)
    acc = b + w[SSM_CONV - 1:SSM_CONV] * cur
    for d in range(1, SSM_CONV):
        rolled = pltpu.roll(cur, d, axis=0)
        top = jnp.where(rows < d, pltpu.roll(prev_tail, d, axis=0), rolled[0:SUBLANES])
        shifted = jnp.concatenate([top, rolled[SUBLANES:]], axis=0)
        acc = acc + w[SSM_CONV - 1 - d:SSM_CONV - d] * shifted
    return acc * jax.nn.sigmoid(acc)


def _split3(x):
    hi = x.astype(BF16)
    rem = x - hi.astype(F32)
    mid = rem.astype(BF16)
    lo = (rem - mid.astype(F32)).astype(BF16)
    return jnp.concatenate([hi, mid, lo], axis=1)


def _ssd_kernel(z_ref, xs_ref, bc_ref, dt_ref, cwx_ref, cbx_ref, cwb_ref, cbb_ref, dtb_ref, alog_ref,
                dskip_ref, ng_ref, hexp_ref, lexp_ref, o_ref, tailx_ref, tailb_ref, state_ref):
    c = pl.program_id(1)

    @pl.when(c == 0)
    def _():
        tailx_ref[...] = jnp.zeros_like(tailx_ref)
        tailb_ref[...] = jnp.zeros_like(tailb_ref)
        state_ref[...] = jnp.zeros_like(state_ref)

    xs_raw = xs_ref[...]
    bc_raw = bc_ref[...]
    xs = _causal_conv_silu(xs_raw, tailx_ref[...], cwx_ref[...], cbx_ref[...])
    bc = _causal_conv_silu(bc_raw, tailb_ref[...], cwb_ref[...], cbb_ref[...])
    tailx_ref[...] = xs_raw[SSM_CHUNK - SUBLANES:]
    tailb_ref[...] = bc_raw[SSM_CHUNK - SUBLANES:]

    dt_in = dt_ref[...][:, :SSM_HEADS] + dtb_ref[...]
    dt = jnp.maximum(dt_in, 0.0) + jnp.log1p(jnp.exp(-jnp.abs(dt_in)))
    a_neg = -jnp.exp(alog_ref[...])
    d_a = dt * a_neg
    li = lax.broadcasted_iota(jnp.int32, (SSM_CHUNK, SSM_CHUNK), 0)
    si = lax.broadcasted_iota(jnp.int32, (SSM_CHUNK, SSM_CHUNK), 1)
    causal = li >= si
    tri = causal.astype(F32)
    a_cs = jnp.dot(tri, d_a, preferred_element_type=F32, precision=HI)
    a_cs_t = lax.dot_general(d_a, tri, (((0,), (1,)), ((), ())),
                             preferred_element_type=F32, precision=HI)
    a_last = a_cs[SSM_CHUNK - 1:SSM_CHUNK]
    e_out = jnp.exp(a_cs)
    e_state = jnp.exp(a_last - a_cs) * dt
    small = jnp.concatenate([dt, e_out, e_state], axis=0)
    wide = jnp.dot(_split3(small), hexp_ref[...], preferred_element_type=F32)
    dt_w = wide[0:SSM_CHUNK]
    e_out_w = wide[SSM_CHUNK:2 * SSM_CHUNK]
    e_state_w = wide[2 * SSM_CHUNK:]
    a_col = jnp.dot(_split3(a_cs), lexp_ref[...], preferred_element_type=F32)

    x_dt = (xs * dt_w).astype(BF16)
    x_state = (xs * e_state_w).astype(BF16)
    chunk_decay_w = e_out_w[SSM_CHUNK - 1:SSM_CHUNK]

    heads_per_group = SSM_HEADS // SSM_GROUPS
    y_parts = []
    for g in range(SSM_GROUPS):
        b_g = bc[:, g * SSM_STATE:(g + 1) * SSM_STATE].astype(BF16)
        c_g = bc[:, SSM_GN + g * SSM_STATE:SSM_GN + (g + 1) * SSM_STATE].astype(BF16)
        cb = lax.dot_general(c_g, b_g, (((1,), (1,)), ((), ())), preferred_element_type=F32)
        gsl = slice(g * SSM_GROUP_W, (g + 1) * SSM_GROUP_W)
        h_prev = state_ref[g]
        y_off = jnp.dot(c_g, h_prev.astype(BF16), preferred_element_type=F32) * e_out_w[:, gsl]
        diag = []
        for r in range(heads_per_group):
            h = g * heads_per_group + r
            seg = a_col[:, h * SSM_CHUNK:(h + 1) * SSM_CHUNK] - a_cs_t[h:h + 1, :]
            decay = jnp.exp(jnp.where(causal, seg, -jnp.inf))
            m = (cb * decay).astype(BF16)
            diag.append(jnp.dot(m, x_dt[:, h * SSM_HEAD_DIM:(h + 1) * SSM_HEAD_DIM],
                                preferred_element_type=F32))
        y_parts.append(jnp.concatenate(diag, axis=1) + y_off)
        upd = lax.dot_general(b_g, x_state[:, gsl], (((0,), (0,)), ((), ())), preferred_element_type=F32)
        state_ref[g] = h_prev * chunk_decay_w[:, gsl] + upd

    y = jnp.concatenate(y_parts, axis=1) + dskip_ref[...] * xs
    z = z_ref[...]
    y = y * (z * jax.nn.sigmoid(z))
    normed = []
    for g in range(SSM_GROUPS):
        y_g = y[:, g * SSM_GROUP_W:(g + 1) * SSM_GROUP_W]
        normed.append(y_g * lax.rsqrt(jnp.mean(y_g * y_g, axis=-1, keepdims=True) + EPS))
    o_ref[...] = (jnp.concatenate(normed, axis=1) * ng_ref[...]).astype(o_ref.dtype)


def _ssd_mixer(zxbcdt, conv_w, conv_b, dt_bias, a_log, d_skip, norm_g, batch, seq):
    t = zxbcdt.shape[0]
    nc = seq // SSM_CHUNK
    head_expand = jnp.tile(jnp.repeat(jnp.eye(SSM_HEADS, dtype=BF16), SSM_HEAD_DIM, axis=1), (3, 1))
    lane_expand = jnp.tile(jnp.repeat(jnp.eye(SSM_HEADS, dtype=BF16), SSM_CHUNK, axis=1), (3, 1))
    row = lambda b, c: (b * nc + c, 0)
    const2 = lambda b, c: (0, 0)
    bc_w = 2 * SSM_GN
    return pl.pallas_call(
        _ssd_kernel,
        grid=(batch, nc),
        in_specs=[
            pl.BlockSpec((SSM_CHUNK, SSM_D_INNER), row),
            pl.BlockSpec((SSM_CHUNK, SSM_D_INNER), lambda b, c: (b * nc + c, 1)),
            pl.BlockSpec((SSM_CHUNK, bc_w), lambda b, c: (b * nc + c, 2 * SSM_D_INNER // bc_w)),
            pl.BlockSpec((SSM_CHUNK, LANES), lambda b, c: (b * nc + c, (2 * SSM_D_INNER + bc_w) // LANES)),
            pl.BlockSpec((SSM_CONV, SSM_D_INNER), const2),
            pl.BlockSpec((1, SSM_D_INNER), const2),
            pl.BlockSpec((SSM_CONV, bc_w), const2),
            pl.BlockSpec((1, bc_w), const2),
            pl.BlockSpec((1, SSM_HEADS), const2),
            pl.BlockSpec((1, SSM_HEADS), const2),
            pl.BlockSpec((1, SSM_D_INNER), const2),
            pl.BlockSpec((1, SSM_D_INNER), const2),
            pl.BlockSpec((3 * SSM_HEADS, SSM_D_INNER), const2),
            pl.BlockSpec((3 * SSM_HEADS, SSM_HEADS * SSM_CHUNK), const2),
        ],
        out_specs=pl.BlockSpec((SSM_CHUNK, SSM_D_INNER), row),
        out_shape=jax.ShapeDtypeStruct((t, SSM_D_INNER), BF16),
        scratch_shapes=[
            pltpu.VMEM((SUBLANES, SSM_D_INNER), F32),
            pltpu.VMEM((SUBLANES, bc_w), F32),
            pltpu.VMEM((SSM_GROUPS, SSM_STATE, SSM_GROUP_W), F32),
        ],
        compiler_params=_cparams("arbitrary", "arbitrary"),
        name="ssd_mixer",
    )(zxbcdt, zxbcdt, zxbcdt, zxbcdt,
      conv_w[:, :SSM_D_INNER], conv_b[:SSM_D_INNER].reshape(1, -1),
      conv_w[:, SSM_D_INNER:], conv_b[SSM_D_INNER:].reshape(1, -1),
      dt_bias.reshape(1, -1), a_log.reshape(1, -1),
      jnp.repeat(d_skip, SSM_HEAD_DIM).reshape(1, -1), norm_g.reshape(1, -1),
      head_expand, lane_expand)


def _head_rms(x, g):
    return x * lax.rsqrt(jnp.mean(x * x, axis=-1, keepdims=True) + EPS) * g


def _swa_kernel(q_ref, kvc_ref, kvp_ref, bucket_ref, qg_ref, kg_ref, rel_ref, sink_ref, o_ref,
                bias_ref, sinkrow_ref):
    b = pl.program_id(0)
    i = pl.program_id(1)

    @pl.when((b == 0) & (i == 0))
    def _():
        bucket = bucket_ref[...]
        kj = lax.broadcasted_iota(jnp.int32, (2 * WINDOW, WINDOW), 0)
        qi = lax.broadcasted_iota(jnp.int32, (2 * WINDOW, WINDOW), 1)
        dist = qi + WINDOW - kj
        band = (dist >= 0) & (dist < WINDOW)
        for h in range(ATTN_Q_HEADS):
            g, r = divmod(h, ATTN_Q_PER_KV)
            acc = jnp.zeros(bucket.shape, F32)
            for k in range(REL_BUCKETS):
                acc = jnp.where(bucket == k, rel_ref[k, h], acc)
            cols = slice(r * WINDOW, (r + 1) * WINDOW)
            bias_ref[0, g, :, cols] = jnp.where(band, acc, -jnp.inf)
            bias_ref[1, g, :, cols] = jnp.where(band & (kj >= WINDOW), acc, -jnp.inf)
            sinkrow_ref[g, :, cols] = jnp.full((1, WINDOW), sink_ref[h], F32)

    first = (i == 0).astype(jnp.int32)
    q_t = q_ref[...].T
    kv_c = kvc_ref[...]
    kv_p = kvp_ref[...]
    kv_w = ATTN_KV_HEADS * ATTN_HEAD_DIM
    q_gain = qg_ref[...]
    outs = []
    for g in range(ATTN_KV_HEADS):
        ksl = slice(g * ATTN_HEAD_DIM, (g + 1) * ATTN_HEAD_DIM)
        vsl = slice(kv_w + g * ATTN_HEAD_DIM, kv_w + (g + 1) * ATTN_HEAD_DIM)
        k = jnp.concatenate([kv_p[:, ksl], kv_c[:, ksl]], axis=0)
        v = jnp.concatenate([kv_p[:, vsl], kv_c[:, vsl]], axis=0).astype(BF16)
        k = _head_rms(k, kg_ref[...]).astype(BF16)
        q_heads = []
        for r in range(ATTN_Q_PER_KV):
            h = g * ATTN_Q_PER_KV + r
            q_h = q_t[h * ATTN_HEAD_DIM:(h + 1) * ATTN_HEAD_DIM]
            inv = lax.rsqrt(jnp.mean(q_h * q_h, axis=0, keepdims=True) + EPS)
            q_heads.append((q_h * inv * q_gain).astype(BF16))
        q = jnp.concatenate(q_heads, axis=1)
        s = jnp.dot(k, q, preferred_element_type=F32) + bias_ref[first, g]
        sink = sinkrow_ref[g]
        m = jnp.maximum(jnp.max(s, axis=0, keepdims=True), sink)
        p = jnp.exp(s - m)
        denom = jnp.sum(p, axis=0, keepdims=True) + jnp.exp(sink - m)
        pv = lax.dot_general(v, p.astype(BF16), (((0,), (0,)), ((), ())), preferred_element_type=F32)
        pv = pv * (1.0 / denom)
        outs.extend(pv[:, r * WINDOW:(r + 1) * WINDOW] for r in range(ATTN_Q_PER_KV))
    o_ref[...] = jnp.concatenate(outs, axis=0).T.astype(o_ref.dtype)


def _t5_causal_bucket(dist):
    max_exact = REL_BUCKETS // 2
    d = jnp.maximum(dist, 1).astype(F32)
    large = max_exact + (jnp.log(d / max_exact) / math.log(REL_MAX_DIST / max_exact)
                         * (REL_BUCKETS - max_exact)).astype(jnp.int32)
    large = jnp.minimum(large, REL_BUCKETS - 1)
    return jnp.where(dist < max_exact, dist, large)


def _swa_mixer(qkv, q_norm_g, k_norm_g, sinks, rel_bias, batch, seq):
    t = qkv.shape[0]
    nb = seq // WINDOW
    q_w = ATTN_Q_HEADS * ATTN_HEAD_DIM
    kv_w2 = 2 * ATTN_KV_HEADS * ATTN_HEAD_DIM
    kj = jnp.arange(2 * WINDOW)[:, None]
    qi = jnp.arange(WINDOW)[None, :]
    bucket = _t5_causal_bucket(jnp.maximum(qi + WINDOW - kj, 0)).astype(jnp.int32)
    q_gain = jnp.broadcast_to((q_norm_g * (ATTN_HEAD_DIM ** -0.5))[:, None], (ATTN_HEAD_DIM, WINDOW))
    const2 = lambda b, i: (0, 0)
    smem = pl.BlockSpec(memory_space=pltpu.SMEM)
    return pl.pallas_call(
        _swa_kernel,
        grid=(batch, nb),
        in_specs=[
            pl.BlockSpec((WINDOW, q_w), lambda b, i: (b * nb + i, 0)),
            pl.BlockSpec((WINDOW, kv_w2), lambda b, i: (b * nb + i, q_w // kv_w2)),
            pl.BlockSpec((WINDOW, kv_w2), lambda b, i: (b * nb + jnp.maximum(i - 1, 0), q_w // kv_w2)),
            pl.BlockSpec((2 * WINDOW, WINDOW), const2),
            pl.BlockSpec((ATTN_HEAD_DIM, WINDOW), const2),
            pl.BlockSpec((1, ATTN_HEAD_DIM), const2),
            smem,
            smem,
        ],
        out_specs=pl.BlockSpec((WINDOW, q_w), lambda b, i: (b * nb + i, 0)),
        out_shape=jax.ShapeDtypeStruct((t, q_w), BF16),
        scratch_shapes=[pltpu.VMEM((2, ATTN_KV_HEADS, 2 * WINDOW, ATTN_Q_PER_KV * WINDOW), F32),
                        pltpu.VMEM((ATTN_KV_HEADS, 1, ATTN_Q_PER_KV * WINDOW), F32)],
        compiler_params=_cparams("arbitrary", "arbitrary"),
        name="swa_mixer",
    )(qkv, qkv, qkv, bucket, q_gain, k_norm_g.reshape(1, -1), rel_bias, sinks)


def _router_kernel(x_ref, g_ref, sh_ref, sc_ref, rwt_ref, rb_ref,
                   h_ref, lp_ref, gate_ref, tile_n_ref, tile_carry_ref, cnt_ref, carry_ref):
    i = pl.program_id(0)
    tm = x_ref.shape[0]

    @pl.when(i == 0)
    def _():
        carry_ref[...] = jnp.zeros_like(carry_ref)

    h = _norm_modulate(x_ref[...], g_ref[...], sh_ref[0], sc_ref[0])
    h_ref[...] = h.astype(h_ref.dtype)

    logits = lax.dot_general(rwt_ref[...], h, (((1,), (1,)), ((), ())),
                             preferred_element_type=F32, precision=HI) + rb_ref[...]
    e_iota = lax.broadcasted_iota(jnp.int32, logits.shape, 0)
    work = logits
    sels, vals = [], []
    for k in range(TOP_K):
        m = jnp.max(work, axis=0, keepdims=True)
        idx = jnp.min(jnp.where(work == m, e_iota, N_EXPERTS), axis=0, keepdims=True)
        sel = e_iota == idx
        work = jnp.where(sel, -jnp.inf, work)
        sels.append(sel)
        vals.append(m)
    exps = [jnp.exp(v - vals[0]) for v in vals]
    denom = exps[0] + exps[1] + exps[2] + exps[3]
    for k in range(TOP_K):
        gate_ref[k:k + 1, :] = exps[k] / denom

    chosen = sels[0] | sels[1] | sels[2] | sels[3]
    t_row = lax.broadcasted_iota(jnp.int32, (tm, tm), 0)
    t_col = lax.broadcasted_iota(jnp.int32, (tm, tm), 1)
    before = (t_row < t_col).astype(BF16)
    prior = jnp.dot(chosen.astype(BF16), before, preferred_element_type=F32)
    n = jnp.sum(chosen.astype(F32), axis=1, keepdims=True)
    e_row = lax.broadcasted_iota(jnp.int32, (N_EXPERTS, N_EXPERTS), 0)
    e_col = lax.broadcasted_iota(jnp.int32, (N_EXPERTS, N_EXPERTS), 1)
    run_start = jnp.dot((e_col < e_row).astype(F32), jnp.broadcast_to(n, (N_EXPERTS, LANES)),
                        preferred_element_type=F32, precision=HI)[:, :1]
    local = run_start + prior
    for k in range(TOP_K):
        lp_ref[k:k + 1, :] = jnp.sum(jnp.where(sels[k], local, 0.0), axis=0, keepdims=True).astype(jnp.int32)
    tile_n_ref[0] = n.astype(jnp.int32)
    tile_carry_ref[0] = carry_ref[...].astype(jnp.int32)
    total = carry_ref[...] + n
    carry_ref[...] = total
    cnt_ref[...] = total.astype(jnp.int32)


def _router(x, g, shift, scale, r_w, r_b, seq, tm):
    t, d = x.shape
    per_b = seq // tm
    ntiles = t // tm
    return pl.pallas_call(
        _router_kernel,
        grid=(t // tm,),
        in_specs=[
            pl.BlockSpec((tm, d), lambda i: (i, 0)),
            pl.BlockSpec((1, d), lambda i: (0, 0)),
            pl.BlockSpec((1, 1, d), lambda i: (i // per_b, 0, 0)),
            pl.BlockSpec((1, 1, d), lambda i: (i // per_b, 0, 0)),
            pl.BlockSpec((N_EXPERTS, d), lambda i: (0, 0)),
            pl.BlockSpec((N_EXPERTS, 1), lambda i: (0, 0)),
        ],
        out_specs=[
            pl.BlockSpec((tm, d), lambda i: (i, 0)),
            pl.BlockSpec((TOP_K, tm), lambda i: (0, i)),
            pl.BlockSpec((TOP_K, tm), lambda i: (0, i)),
            pl.BlockSpec((1, N_EXPERTS, 1), lambda i: (i, 0, 0)),
            pl.BlockSpec((1, N_EXPERTS, 1), lambda i: (i, 0, 0)),
            pl.BlockSpec((N_EXPERTS, 1), lambda i: (0, 0)),
        ],
        out_shape=[
            jax.ShapeDtypeStruct((t, d), BF16),
            jax.ShapeDtypeStruct((TOP_K, t), jnp.int32),
            jax.ShapeDtypeStruct((TOP_K, t), F32),
            jax.ShapeDtypeStruct((ntiles, N_EXPERTS, 1), jnp.int32),
            jax.ShapeDtypeStruct((ntiles, N_EXPERTS, 1), jnp.int32),
            jax.ShapeDtypeStruct((N_EXPERTS, 1), jnp.int32),
        ],
        scratch_shapes=[pltpu.VMEM((N_EXPERTS, 1), F32)],
        compiler_params=_cparams("arbitrary"),
        name="moe_router",
    )(x, g.reshape(1, d), shift, scale, r_w.T, r_b.reshape(-1, 1))


def _run_copies(n, src_ref, src_row, dst_ref, dst_row, sem, max_rows, fn):
    for b in range(max_rows.bit_length()):
        size = 1 << b

        @pl.when((n & size) != 0)
        def _():
            lo = n & (size - 1)
            fn(pltpu.make_async_copy(
                src_ref.at[pl.ds(pl.multiple_of((src_row + lo) * ROW_TILES, ROW_TILES), size * ROW_TILES)],
                dst_ref.at[pl.ds(pl.multiple_of((dst_row + lo) * ROW_TILES, ROW_TILES), size * ROW_TILES)],
                sem))


def _tile_runs(tile, n_ref, carry_ref, gstart_ref, sorted_ref, grouped_ref, sem, tm, fn, to_grouped):
    def body(e, run_start):
        n = n_ref[tile * N_EXPERTS + e]
        slot0 = gstart_ref[e] + carry_ref[tile * N_EXPERTS + e]
        if to_grouped:
            _run_copies(n, sorted_ref, run_start, grouped_ref, slot0, sem, tm, fn)
        else:
            _run_copies(n, grouped_ref, slot0, sorted_ref, run_start, sem, tm, fn)
        return run_start + n

    lax.fori_loop(0, N_EXPERTS, body, 0)


def _block_copy(src_ref, dst_ref, dst_blk, sem):
    rows = MOE_BLOCK * ROW_TILES
    return pltpu.make_async_copy(src_ref, dst_ref.at[pl.ds(pl.multiple_of(dst_blk * rows, rows), rows)], sem)


def _dispatch_kernel(n_ref, carry_ref, gstart_ref, pad_lo_ref, pad_hi_ref, nused_ref, lp_ref, h_ref, xg_ref,
                     sorted_ref, zero_ref, run_sem, sem):
    i = pl.program_id(0)
    last = pl.num_programs(0) - 1
    tm = h_ref.shape[0]
    rows = TOP_K * tm
    slot = i % 2
    nblk = xg_ref.shape[0] // (MOE_BLOCK * ROW_TILES)

    @pl.when(i == 0)
    def _():
        zero_ref[...] = jnp.zeros_like(zero_ref)

        def fill(b, carry):
            _block_copy(zero_ref, xg_ref, b, sem).start()
            return carry

        def fill_wait(b, carry):
            _block_copy(zero_ref, xg_ref, b, sem).wait()
            return carry

        lax.fori_loop(nused_ref[0], nblk, fill, 0)
        lax.fori_loop(nused_ref[0], nblk, fill_wait, 0)

    @pl.when(i < N_EXPERTS)
    def _():
        n_pad = pad_hi_ref[i] - pad_lo_ref[i]
        for fn in (lambda cp: cp.start(), lambda cp: cp.wait()):
            _run_copies(n_pad, zero_ref, 0, xg_ref, pad_lo_ref[i], sem, MOE_BLOCK - 1, fn)

    row_id = lax.broadcasted_iota(jnp.int32, (rows, tm), 0)
    perm = row_id == lp_ref[0:1, :]
    for k in range(1, TOP_K):
        perm = perm | (row_id == lp_ref[k:k + 1, :])
    srt = jnp.dot(jnp.where(perm, 1.0, 0.0).astype(BF16), h_ref[...], preferred_element_type=F32)
    for s in range(ROW_TILES):
        sorted_ref[slot, pl.ds(s, rows, stride=ROW_TILES), :] = srt[:, s * LANES:(s + 1) * LANES]

    def runs(tile, buf, fn):
        _tile_runs(tile, n_ref, carry_ref, gstart_ref, sorted_ref.at[buf], xg_ref, run_sem, tm, fn, True)

    def wait_tile(buf):
        pltpu.make_async_copy(sorted_ref.at[buf], xg_ref.at[pl.ds(0, rows * ROW_TILES)], run_sem).wait()

    @pl.when(i > 0)
    def _():
        wait_tile(1 - slot)

    runs(i, slot, lambda cp: cp.start())

    @pl.when(i == last)
    def _():
        wait_tile(slot)


def _dispatch(tile_n, tile_carry, group_start, pad_lo, pad_hi, nused, lp_t, h, cap, tm):
    t, d = h.shape
    assert t // tm >= N_EXPERTS
    grid_spec = pltpu.PrefetchScalarGridSpec(
        num_scalar_prefetch=6,
        grid=(t // tm,),
        in_specs=[
            pl.BlockSpec((TOP_K, tm), lambda i, *_: (0, i)),
            pl.BlockSpec((tm, d), lambda i, *_: (i, 0)),
        ],
        out_specs=pl.BlockSpec(memory_space=pl.ANY),
        scratch_shapes=[
            pltpu.VMEM((2, TOP_K * tm * ROW_TILES, LANES), F32),
            pltpu.VMEM((MOE_BLOCK * ROW_TILES, LANES), F32),
            pltpu.SemaphoreType.DMA, pltpu.SemaphoreType.DMA,
        ],
    )
    return pl.pallas_call(
        _dispatch_kernel,
        grid_spec=grid_spec,
        out_shape=jax.ShapeDtypeStruct((cap * ROW_TILES, LANES), F32),
        compiler_params=pltpu.CompilerParams(dimension_semantics=("arbitrary",), has_side_effects=True,
                                             vmem_limit_bytes=VMEM_LIMIT),
        name="moe_dispatch",
    )(tile_n, tile_carry, group_start, pad_lo, pad_hi, nused, lp_t, h)


def _weight_copies(w1_hbm, w2_hbm, w1buf, w2buf, sems, layer, e, slot):
    return (pltpu.make_async_copy(w1_hbm.at[layer, e], w1buf.at[slot], sems.at[0, slot]),
            pltpu.make_async_copy(w2_hbm.at[layer, e], w2buf.at[slot], sems.at[1, slot]))


def _expert_kernel(blk_e_ref, nused_ref, next_e_ref, slot_ref, blk_rows_ref, x_ref, w1_hbm, w2_hbm, bg_ref, bl_ref, b2_ref,
                   sel_ref, o_ref, w1buf, w2buf, wg_s, wl_s, w2_s, sems, *, layer):
    blk = pl.program_id(0)
    active = blk < nused_ref[0]
    e = blk_e_ref[blk]
    slot = slot_ref[e]
    new_expert = (blk == 0) | (e != blk_e_ref[jnp.maximum(blk - 1, 0)])
    copies = functools.partial(_weight_copies, w1_hbm, w2_hbm, w1buf, w2buf, sems, layer)

    @pl.when(blk == 0)
    def _():
        for cp in copies(e, slot):
            cp.start()

    @pl.when(active & new_expert)
    def _():
        nxt = next_e_ref[e]

        @pl.when(nxt >= 0)
        def _():
            for cp in copies(nxt, 1 - slot):
                cp.start()

        for cp in copies(e, slot):
            cp.wait()
        half = SPLIT_W // 2
        for j in range(w1buf.shape[2] // SPLIT_W):
            chunk = w1buf[slot, :, j * SPLIT_W:(j + 1) * SPLIT_W].astype(BF16)
            split = jnp.dot(chunk, sel_ref[...], preferred_element_type=F32)
            wg_s[:, j * half:(j + 1) * half] = split[:, :half].astype(BF16)
            wl_s[:, j * half:(j + 1) * half] = split[:, half:].astype(BF16)
        w2_s[...] = w2buf[slot].astype(BF16)

    valid_rows = jnp.where(active, blk_rows_ref[blk], 0)
    for u in range(MOE_BLOCK // MOE_SUB):
        base = u * MOE_SUB * ROW_TILES

        @pl.when(valid_rows > u * MOE_SUB)
        def _():
            x = jnp.concatenate([x_ref[pl.ds(base + s, MOE_SUB, stride=ROW_TILES), :]
                                 for s in range(ROW_TILES)], axis=1).astype(BF16)
            glu = jnp.dot(x, wg_s[...], preferred_element_type=F32) + bg_ref[0]
            lin = jnp.dot(x, wl_s[...], preferred_element_type=F32) + bl_ref[0]
            glu = jnp.minimum(glu, SWIGLU_LIMIT)
            lin = jnp.clip(lin, -SWIGLU_LIMIT, SWIGLU_LIMIT)
            act = glu * jax.nn.sigmoid(SWIGLU_ALPHA * glu) * (lin + 1.0)
            y = jnp.dot(act.astype(BF16), w2_s[...], preferred_element_type=F32) + b2_ref[0]
            for s in range(ROW_TILES):
                o_ref[pl.ds(base + s, MOE_SUB, stride=ROW_TILES), :] = y[:, s * LANES:(s + 1) * LANES]

        @pl.when(valid_rows <= u * MOE_SUB)
        def _():
            o_ref[pl.ds(base, MOE_SUB * ROW_TILES), :] = jnp.zeros((MOE_SUB * ROW_TILES, LANES), o_ref.dtype)


def _experts(layer, blk_e, nused, next_e, e_slot, blk_rows, xg_rows, w1_all, w2_all, b_glu, b_lin, b2):
    nblk = blk_e.shape[0]
    d = D_MODEL
    ff = w2_all.shape[2]
    col = jnp.arange(SPLIT_W)
    sel = (jnp.arange(SPLIT_W)[None, :] == ((col % 2) * (SPLIT_W // 2) + col // 2)[:, None]).astype(BF16)

    def blk_map(b, blk_e_ref, nused_ref, *_):
        return (jnp.minimum(b, nused_ref[0] - 1), 0)

    def e_map(b, blk_e_ref, nused_ref, *_):
        return (blk_e_ref[jnp.minimum(b, nused_ref[0] - 1)], 0, 0)

    grid_spec = pltpu.PrefetchScalarGridSpec(
        num_scalar_prefetch=5,
        grid=(nblk,),
        in_specs=[
            pl.BlockSpec((MOE_BLOCK * ROW_TILES, LANES), blk_map),
            pl.BlockSpec(memory_space=pl.ANY),
            pl.BlockSpec(memory_space=pl.ANY),
            pl.BlockSpec((1, 1, ff), e_map),
            pl.BlockSpec((1, 1, ff), e_map),
            pl.BlockSpec((1, 1, d), e_map),
            pl.BlockSpec((SPLIT_W, SPLIT_W), lambda b, *_: (0, 0)),
        ],
        out_specs=pl.BlockSpec((MOE_BLOCK * ROW_TILES, LANES), lambda b, *_: (b, 0)),
        scratch_shapes=[
            pltpu.VMEM((2, d, 2 * ff), F32), pltpu.VMEM((2, ff, d), F32),
            pltpu.VMEM((d, ff), BF16), pltpu.VMEM((d, ff), BF16), pltpu.VMEM((ff, d), BF16),
            pltpu.SemaphoreType.DMA((2, 2)),
        ],
    )
    return pl.pallas_call(
        functools.partial(_expert_kernel, layer=layer),
        grid_spec=grid_spec,
        out_shape=jax.ShapeDtypeStruct(xg_rows.shape, F32),
        compiler_params=_cparams("arbitrary"),
        name="moe_experts",
    )(blk_e, nused, next_e, e_slot, blk_rows, xg_rows, w1_all, w2_all, b_glu, b_lin, b2, sel)


def _combine_kernel(n_ref, carry_ref, gstart_ref, yg_ref, lp_ref, gate_ref, x_ref, g2_ref, o_ref, sorted_ref, sem):
    i = pl.program_id(0)
    ntiles = pl.num_programs(0)
    tm = x_ref.shape[0]
    rows = TOP_K * tm
    slot = i % 2

    def runs(tile, buf, fn):
        _tile_runs(tile, n_ref, carry_ref, gstart_ref, sorted_ref.at[buf], yg_ref, sem.at[buf], tm, fn, False)

    @pl.when(i == 0)
    def _():
        runs(0, 0, lambda cp: cp.start())

    @pl.when(i + 1 < ntiles)
    def _():
        runs(i + 1, 1 - slot, lambda cp: cp.start())

    eye = (lax.broadcasted_iota(jnp.int32, (tm, tm), 0) ==
           lax.broadcasted_iota(jnp.int32, (tm, tm), 1)).astype(F32)
    rows_t = jnp.concatenate([gate_ref[...], lp_ref[...].astype(F32)], axis=0)
    cols = lax.dot_general(eye, rows_t, (((1,), (1,)), ((), ())), preferred_element_type=F32, precision=HI)
    col_id = lax.broadcasted_iota(jnp.int32, (tm, rows), 1)
    weights = jnp.zeros((tm, rows), F32)
    for k in range(TOP_K):
        lp_col = (cols[:, TOP_K + k:TOP_K + k + 1] + 0.5).astype(jnp.int32)
        weights = weights + jnp.where(col_id == lp_col, cols[:, k:k + 1], 0.0)

    pltpu.make_async_copy(yg_ref.at[pl.ds(0, rows * ROW_TILES)], sorted_ref.at[slot], sem.at[slot]).wait()
    y = jnp.concatenate(
        [sorted_ref[slot, pl.ds(s, rows, stride=ROW_TILES), :] for s in range(ROW_TILES)], axis=1).astype(BF16)
    acc = jnp.dot(weights.astype(BF16), y, preferred_element_type=F32)
    o_ref[...] = x_ref[...] + g2_ref[0] * acc


def _combine(tile_n, tile_carry, group_start, yg_rows, lp_t, gate_t, x, gate2, seq, tm):
    t, d = x.shape
    per_b = seq // tm
    grid_spec = pltpu.PrefetchScalarGridSpec(
        num_scalar_prefetch=3,
        grid=(t // tm,),
        in_specs=[
            pl.BlockSpec(memory_space=pl.ANY),
            pl.BlockSpec((TOP_K, tm), lambda i, *_: (0, i)),
            pl.BlockSpec((TOP_K, tm), lambda i, *_: (0, i)),
            pl.BlockSpec((tm, d), lambda i, *_: (i, 0)),
            pl.BlockSpec((1, 1, d), lambda i, *_: (i // per_b, 0, 0)),
        ],
        out_specs=pl.BlockSpec((tm, d), lambda i, *_: (i, 0)),
        scratch_shapes=[pltpu.VMEM((2, TOP_K * tm * ROW_TILES, LANES), F32), pltpu.SemaphoreType.DMA((2,))],
    )
    return pl.pallas_call(
        _combine_kernel,
        grid_spec=grid_spec,
        out_shape=jax.ShapeDtypeStruct((t, d), F32),
        compiler_params=_cparams("arbitrary"),
        name="moe_combine",
    )(tile_n, tile_carry, group_start, yg_rows, lp_t, gate_t, x, gate2)


def _moe(layer, x, g, shift, scale, gate2, r_w, r_b, w1_all, b1, w2_all, b2, seq):
    t = x.shape[0]
    h, lp_t, gate_t, tile_n, tile_carry, counts = _router(x, g, shift, scale, r_w, r_b, seq, MOE_TILE)
    tile_n = tile_n.reshape(-1)
    tile_carry = tile_carry.reshape(-1)
    counts = counts.reshape(-1)
    padded = ((counts + MOE_BLOCK - 1) // MOE_BLOCK) * MOE_BLOCK
    group_end = jnp.cumsum(padded)
    group_start = (group_end - padded).astype(jnp.int32)
    cap = t * TOP_K + N_EXPERTS * MOE_BLOCK
    nblk = cap // MOE_BLOCK
    blk_start = jnp.arange(nblk, dtype=jnp.int32) * MOE_BLOCK
    blk_e = jnp.minimum(jnp.sum(blk_start[:, None] >= group_end[None, :], axis=1), N_EXPERTS - 1).astype(jnp.int32)
    nused = (group_end[-1:] // MOE_BLOCK).astype(jnp.int32)

    pad_lo = (group_start + counts).astype(jnp.int32)
    pad_hi = group_end.astype(jnp.int32)
    xg_rows = _dispatch(tile_n, tile_carry, group_start, pad_lo, pad_hi, nused, lp_t, h, cap, MOE_TILE)
    nonempty = counts > 0
    e_ids = jnp.arange(N_EXPERTS, dtype=jnp.int32)
    later = jnp.where(nonempty[None, :] & (e_ids[None, :] > e_ids[:, None]), e_ids[None, :], N_EXPERTS)
    next_e = jnp.min(later, axis=1)
    next_e = jnp.where(next_e == N_EXPERTS, -1, next_e).astype(jnp.int32)
    e_slot = ((jnp.cumsum(nonempty.astype(jnp.int32)) - 1) % 2).astype(jnp.int32)
    blk_rows = jnp.clip(pad_lo[blk_e] - blk_start, 0, MOE_BLOCK).astype(jnp.int32)
    yg_rows = _experts(layer, blk_e, nused, next_e, e_slot, blk_rows, xg_rows, w1_all, w2_all,
                       b1[:, None, 0::2], b1[:, None, 1::2], b2[:, None, :])
    return _combine(tile_n, tile_carry, group_start, yg_rows, lp_t, gate_t, x, gate2, seq, MOE_TILE)


def kernel(x, c, ada_w, ada_b, norm1_g, norm2_g, m_in_w, m_conv_w, m_conv_b, m_dt_bias, m_A_log, m_D, m_norm_g, m_out_w, a_qkv_w, a_q_norm_g, a_k_norm_g, a_sinks, a_out_w, rel_bias, r_w, r_b, e_w1, e_b1, e_w2, e_b2):
    batch, seq, d = x.shape
    depth = ada_w.shape[0]
    t = batch * seq
    xf = x.reshape(t, d)

    c_pad = jnp.zeros((SUBLANES, d), F32).at[:batch].set(c)
    mod = _adaln(c_pad, ada_w, ada_b)[:, :batch]

    for i in range(depth):
        parts = [mod[i, :, p * d:(p + 1) * d].reshape(batch, 1, d) for p in range(6)]
        sh1, sc1, g1, sh2, sc2, g2 = parts
        j = i // 2
        if i % 2 == 0:
            w_in = jnp.pad(m_in_w[j], ((0, 0), (0, SSM_IN_PAD - m_in_w.shape[2]))).astype(BF16)
            zxbcdt = _norm_matmul(xf, norm1_g[i], sh1, sc1, w_in, seq)
            y = _ssd_mixer(zxbcdt, m_conv_w[j], m_conv_b[j], m_dt_bias[j], m_A_log[j], m_D[j],
                           m_norm_g[j], batch, seq)
            xf = _matmul_residual(y, m_out_w[j].astype(BF16), xf, g1, seq)
        else:
            qkv = _norm_matmul(xf, norm1_g[i], sh1, sc1, a_qkv_w[j].astype(BF16), seq)
            y = _swa_mixer(qkv, a_q_norm_g[j], a_k_norm_g[j], a_sinks[j], rel_bias, batch, seq)
            xf = _matmul_residual(y, a_out_w[j].astype(BF16), xf, g1, seq)
        xf = _moe(i, xf, norm2_g[i], sh2, sc2, g2, r_w[i], r_b[i], e_w1, e_b1[i], e_w2, e_b2[i], seq)
    return xf.reshape(batch, seq, d)
```

```python
import functools
import math

import jax
import jax.numpy as jnp
from jax import lax
from jax.experimental import pallas as pl
from jax.experimental.pallas import tpu as pltpu

D_MODEL = 1024
EPS = 1e-6
LANES = 128
SUBLANES = 8
ROW_TILES = D_MODEL // LANES

SSM_D_INNER = 2048
SSM_HEAD_DIM = 64
SSM_HEADS = 32
SSM_GROUPS = 4
SSM_STATE = 128
SSM_CONV = 4
SSM_CHUNK = 128
SSM_GN = SSM_GROUPS * SSM_STATE
SSM_IN_PAD = 5248
SSM_GROUP_W = SSM_D_INNER // SSM_GROUPS

ATTN_HEAD_DIM = 64
ATTN_Q_HEADS = 16
ATTN_KV_HEADS = 4
ATTN_Q_PER_KV = 4
WINDOW = 128
REL_BUCKETS = 32
REL_MAX_DIST = 128

N_EXPERTS = 32
TOP_K = 4
SWIGLU_ALPHA = 1.702
SWIGLU_LIMIT = 7.0
MOE_BLOCK = 512
MOE_SUB = 256
MOE_TILE = 256
SPLIT_W = 256

VMEM_LIMIT = 56 * 1024 * 1024
HI = lax.Precision.HIGHEST
F32 = jnp.float32
BF16 = jnp.bfloat16


def _cparams(*sem):
    return pltpu.CompilerParams(dimension_semantics=sem, vmem_limit_bytes=VMEM_LIMIT)


def _norm_modulate(x, g, shift, scale):
    ms = jnp.mean(x * x, axis=-1, keepdims=True)
    return x * lax.rsqrt(ms + EPS) * g * (1.0 + scale) + shift


def _adaln_kernel(c_ref, w_ref, b_ref, o_ref):
    c = c_ref[...]
    c_act = c * jax.nn.sigmoid(c)
    o_ref[0] = jnp.dot(c_act, w_ref[0], preferred_element_type=F32, precision=HI) + b_ref[0]


def _adaln(c_pad, ada_w, ada_b):
    depth, d, n = ada_w.shape
    tn = 1536
    return pl.pallas_call(
        _adaln_kernel,
        grid=(depth, n // tn),
        in_specs=[
            pl.BlockSpec((SUBLANES, d), lambda i, j: (0, 0)),
            pl.BlockSpec((1, d, tn), lambda i, j: (i, 0, j)),
            pl.BlockSpec((1, 1, tn), lambda i, j: (i, 0, j)),
        ],
        out_specs=pl.BlockSpec((1, SUBLANES, tn), lambda i, j: (i, 0, j)),
        out_shape=jax.ShapeDtypeStruct((depth, SUBLANES, n), F32),
        compiler_params=_cparams("arbitrary", "arbitrary"),
        name="adaln",
    )(c_pad, ada_w, ada_b.reshape(depth, 1, n))


def _norm_matmul_kernel(x_ref, g_ref, sh_ref, sc_ref, w_ref, o_ref):
    h = _norm_modulate(x_ref[...], g_ref[...], sh_ref[0], sc_ref[0])
    o_ref[...] = jnp.dot(h.astype(BF16), w_ref[...], preferred_element_type=F32)


def _norm_matmul(x, g, shift, scale, w_bf16, seq, tm=512):
    t, d = x.shape
    n = w_bf16.shape[1]
    per_b = seq // tm
    return pl.pallas_call(
        _norm_matmul_kernel,
        grid=(t // tm,),
        in_specs=[
            pl.BlockSpec((tm, d), lambda i: (i, 0)),
            pl.BlockSpec((1, d), lambda i: (0, 0)),
            pl.BlockSpec((1, 1, d), lambda i: (i // per_b, 0, 0)),
            pl.BlockSpec((1, 1, d), lambda i: (i // per_b, 0, 0)),
            pl.BlockSpec((d, n), lambda i: (0, 0), pipeline_mode=pl.Buffered(1)),
        ],
        out_specs=pl.BlockSpec((tm, n), lambda i: (i, 0)),
        out_shape=jax.ShapeDtypeStruct((t, n), F32),
        compiler_params=_cparams("arbitrary"),
        name="norm_matmul",
    )(x, g.reshape(1, d), shift, scale, w_bf16)


def _matmul_residual_kernel(y_ref, w_ref, x_ref, gate_ref, o_ref):
    acc = jnp.dot(y_ref[...], w_ref[...], preferred_element_type=F32)
    o_ref[...] = x_ref[...] + gate_ref[0] * acc


def _matmul_residual(y_bf16, w_bf16, x, gate, seq, tm=512):
    t, k = y_bf16.shape
    d = x.shape[1]
    per_b = seq // tm
    return pl.pallas_call(
        _matmul_residual_kernel,
        grid=(t // tm,),
        in_specs=[
            pl.BlockSpec((tm, k), lambda i: (i, 0)),
            pl.BlockSpec((k, d), lambda i: (0, 0)),
            pl.BlockSpec((tm, d), lambda i: (i, 0)),
            pl.BlockSpec((1, 1, d), lambda i: (i // per_b, 0, 0)),
        ],
        out_specs=pl.BlockSpec((tm, d), lambda i: (i, 0)),
        out_shape=jax.ShapeDtypeStruct((t, d), F32),
        compiler_params=_cparams("arbitrary"),
        name="matmul_residual",
    )(y_bf16, w_bf16, x, gate)


def _causal_conv_silu(cur, prev_tail, w, b):
    rows = lax.broadcasted_iota(jnp.int32, (SUBLANES, cur.shape[1]), 0)
    acc = b + w[SSM_CONV - 1:SSM_CONV] * cur
    for d in range(1, SSM_CONV):
        rolled = pltpu.roll(cur, d, axis=0)
        top = jnp.where(rows < d, pltpu.roll(prev_tail, d, axis=0), rolled[0:SUBLANES])
        shifted = jnp.concatenate([top, rolled[SUBLANES:]], axis=0)
        acc = acc + w[SSM_CONV - 1 - d:SSM_CONV - d] * shifted
    return acc * jax.nn.sigmoid(acc)


def _split3(x):
    hi = x.astype(BF16)
    rem = x - hi.astype(F32)
    mid = rem.astype(BF16)
    lo = (rem - mid.astype(F32)).astype(BF16)
    return jnp.concatenate([hi, mid, lo], axis=1)


def _ssd_kernel(z_ref, xs_ref, bc_ref, dt_ref, cwx_ref, cbx_ref, cwb_ref, cbb_ref, dtb_ref, alog_ref,
                dskip_ref, ng_ref, hexp_ref, lexp_ref, o_ref, tailx_ref, tailb_ref, state_ref):
    c = pl.program_id(1)

    @pl.when(c == 0)
    def _():
        tailx_ref[...] = jnp.zeros_like(tailx_ref)
        tailb_ref[...] = jnp.zeros_like(tailb_ref)
        state_ref[...] = jnp.zeros_like(state_ref)

    xs_raw = xs_ref[...]
    bc_raw = bc_ref[...]
    xs = _causal_conv_silu(xs_raw, tailx_ref[...], cwx_ref[...], cbx_ref[...])
    bc = _causal_conv_silu(bc_raw, tailb_ref[...], cwb_ref[...], cbb_ref[...])
    tailx_ref[...] = xs_raw[SSM_CHUNK - SUBLANES:]
    tailb_ref[...] = bc_raw[SSM_CHUNK - SUBLANES:]

    dt_in = dt_ref[...][:, :SSM_HEADS] + dtb_ref[...]
    dt = jnp.maximum(dt_in, 0.0) + jnp.log1p(jnp.exp(-jnp.abs(dt_in)))
    a_neg = -jnp.exp(alog_ref[...])
    d_a = dt * a_neg
    li = lax.broadcasted_iota(jnp.int32, (SSM_CHUNK, SSM_CHUNK), 0)
    si = lax.broadcasted_iota(jnp.int32, (SSM_CHUNK, SSM_CHUNK), 1)
    causal = li >= si
    tri = causal.astype(F32)
    a_cs = jnp.dot(tri, d_a, preferred_element_type=F32, precision=HI)
    a_cs_t = lax.dot_general(d_a, tri, (((0,), (1,)), ((), ())),
                             preferred_element_type=F32, precision=HI)
    a_last = a_cs[SSM_CHUNK - 1:SSM_CHUNK]
    e_out = jnp.exp(a_cs)
    e_state = jnp.exp(a_last - a_cs) * dt
    small = jnp.concatenate([dt, e_out, e_state], axis=0)
    wide = jnp.dot(_split3(small), hexp_ref[...], preferred_element_type=F32)
    dt_w = wide[0:SSM_CHUNK]
    e_out_w = wide[SSM_CHUNK:2 * SSM_CHUNK]
    e_state_w = wide[2 * SSM_CHUNK:]
    a_col = jnp.dot(_split3(a_cs), lexp_ref[...], preferred_element_type=F32)

    x_dt = (xs * dt_w).astype(BF16)
    x_state = (xs * e_state_w).astype(BF16)
    chunk_decay_w = e_out_w[SSM_CHUNK - 1:SSM_CHUNK]

    heads_per_group = SSM_HEADS // SSM_GROUPS
    y_parts = []
    for g in range(SSM_GROUPS):
        b_g = bc[:, g * SSM_STATE:(g + 1) * SSM_STATE].astype(BF16)
        c_g = bc[:, SSM_GN + g * SSM_STATE:SSM_GN + (g + 1) * SSM_STATE].astype(BF16)
        cb = lax.dot_general(c_g, b_g, (((1,), (1,)), ((), ())), preferred_element_type=F32)
        gsl = slice(g * SSM_GROUP_W, (g + 1) * SSM_GROUP_W)
        h_prev = state_ref[g]
        y_off = jnp.dot(c_g, h_prev.astype(BF16), preferred_element_type=F32) * e_out_w[:, gsl]
        diag = []
        for r in range(heads_per_group):
            h = g * heads_per_group + r
            seg = a_col[:, h * SSM_CHUNK:(h + 1) * SSM_CHUNK] - a_cs_t[h:h + 1, :]
            decay = jnp.exp(jnp.where(causal, seg, -jnp.inf))
            m = (cb * decay).astype(BF16)
            diag.append(jnp.dot(m, x_dt[:, h * SSM_HEAD_DIM:(h + 1) * SSM_HEAD_DIM],
                                preferred_element_type=F32))
        y_parts.append(jnp.concatenate(diag, axis=1) + y_off)
        upd = lax.dot_general(b_g, x_state[:, gsl], (((0,), (0,)), ((), ())), preferred_element_type=F32)
        state_ref[g] = h_prev * chunk_decay_w[:, gsl] + upd

    y = jnp.concatenate(y_parts, axis=1) + dskip_ref[...] * xs
    z = z_ref[...]
    y = y * (z * jax.nn.sigmoid(z))
    normed = []
    for g in range(SSM_GROUPS):
        y_g = y[:, g * SSM_GROUP_W:(g + 1) * SSM_GROUP_W]
        normed.append(y_g * lax.rsqrt(jnp.mean(y_g * y_g, axis=-1, keepdims=True) + EPS))
    o_ref[...] = (jnp.concatenate(normed, axis=1) * ng_ref[...]).astype(o_ref.dtype)


def _ssd_mixer(zxbcdt, conv_w, conv_b, dt_bias, a_log, d_skip, norm_g, batch, seq):
    t = zxbcdt.shape[0]
    nc = seq // SSM_CHUNK
    head_expand = jnp.tile(jnp.repeat(jnp.eye(SSM_HEADS, dtype=BF16), SSM_HEAD_DIM, axis=1), (3, 1))
    lane_expand = jnp.tile(jnp.repeat(jnp.eye(SSM_HEADS, dtype=BF16), SSM_CHUNK, axis=1), (3, 1))
    row = lambda b, c: (b * nc + c, 0)
    const2 = lambda b, c: (0, 0)
    bc_w = 2 * SSM_GN
    return pl.pallas_call(
        _ssd_kernel,
        grid=(batch, nc),
        in_specs=[
            pl.BlockSpec((SSM_CHUNK, SSM_D_INNER), row),
            pl.BlockSpec((SSM_CHUNK, SSM_D_INNER), lambda b, c: (b * nc + c, 1)),
            pl.BlockSpec((SSM_CHUNK, bc_w), lambda b, c: (b * nc + c, 2 * SSM_D_INNER // bc_w)),
            pl.BlockSpec((SSM_CHUNK, LANES), lambda b, c: (b * nc + c, (2 * SSM_D_INNER + bc_w) // LANES)),
            pl.BlockSpec((SSM_CONV, SSM_D_INNER), const2),
            pl.BlockSpec((1, SSM_D_INNER), const2),
            pl.BlockSpec((SSM_CONV, bc_w), const2),
            pl.BlockSpec((1, bc_w), const2),
            pl.BlockSpec((1, SSM_HEADS), const2),
            pl.BlockSpec((1, SSM_HEADS), const2),
            pl.BlockSpec((1, SSM_D_INNER), const2),
            pl.BlockSpec((1, SSM_D_INNER), const2),
            pl.BlockSpec((3 * SSM_HEADS, SSM_D_INNER), const2),
            pl.BlockSpec((3 * SSM_HEADS, SSM_HEADS * SSM_CHUNK), const2),
        ],
        out_specs=pl.BlockSpec((SSM_CHUNK, SSM_D_INNER), row),
        out_shape=jax.ShapeDtypeStruct((t, SSM_D_INNER), BF16),
        scratch_shapes=[
            pltpu.VMEM((SUBLANES, SSM_D_INNER), F32),
            pltpu.VMEM((SUBLANES, bc_w), F32),
            pltpu.VMEM((SSM_GROUPS, SSM_STATE, SSM_GROUP_W), F32),
        ],
        compiler_params=_cparams("arbitrary", "arbitrary"),
        name="ssd_mixer",
    )(zxbcdt, zxbcdt, zxbcdt, zxbcdt,
      conv_w[:, :SSM_D_INNER], conv_b[:SSM_D_INNER].reshape(1, -1),
      conv_w[:, SSM_D_INNER:], conv_b[SSM_D_INNER:].reshape(1, -1),
      dt_bias.reshape(1, -1), a_log.reshape(1, -1),
      jnp.repeat(d_skip, SSM_HEAD_DIM).reshape(1, -1), norm_g.reshape(1, -1),
      head_expand, lane_expand)


def _head_rms(x, g):
    return x * lax.rsqrt(jnp.mean(x * x, axis=-1, keepdims=True) + EPS) * g


def _swa_kernel(q_ref, kvc_ref, kvp_ref, bucket_ref, qg_ref, kg_ref, rel_ref, sink_ref, o_ref,
                bias_ref, sinkrow_ref):
    b = pl.program_id(0)
    i = pl.program_id(1)

    @pl.when((b == 0) & (i == 0))
    def _():
        bucket = bucket_ref[...]
        kj = lax.broadcasted_iota(jnp.int32, (2 * WINDOW, WINDOW), 0)
        qi = lax.broadcasted_iota(jnp.int32, (2 * WINDOW, WINDOW), 1)
        dist = qi + WINDOW - kj
        band = (dist >= 0) & (dist < WINDOW)
        for h in range(ATTN_Q_HEADS):
            g, r = divmod(h, ATTN_Q_PER_KV)
            acc = jnp.zeros(bucket.shape, F32)
            for k in range(REL_BUCKETS):
                acc = jnp.where(bucket == k, rel_ref[k, h], acc)
            cols = slice(r * WINDOW, (r + 1) * WINDOW)
            bias_ref[0, g, :, cols] = jnp.where(band, acc, -jnp.inf)
            bias_ref[1, g, :, cols] = jnp.where(band & (kj >= WINDOW), acc, -jnp.inf)
            sinkrow_ref[g, :, cols] = jnp.full((1, WINDOW), sink_ref[h], F32)

    first = (i == 0).astype(jnp.int32)
    q_t = q_ref[...].T
    kv_c = kvc_ref[...]
    kv_p = kvp_ref[...]
    kv_w = ATTN_KV_HEADS * ATTN_HEAD_DIM
    q_gain = qg_ref[...]
    outs = []
    for g in range(ATTN_KV_HEADS):
        ksl = slice(g * ATTN_HEAD_DIM, (g + 1) * ATTN_HEAD_DIM)
        vsl = slice(kv_w + g * ATTN_HEAD_DIM, kv_w + (g + 1) * ATTN_HEAD_DIM)
        k = jnp.concatenate([kv_p[:, ksl], kv_c[:, ksl]], axis=0)
        v = jnp.concatenate([kv_p[:, vsl], kv_c[:, vsl]], axis=0).astype(BF16)
        k = _head_rms(k, kg_ref[...]).astype(BF16)
        q_heads = []
        for r in range(ATTN_Q_PER_KV):
            h = g * ATTN_Q_PER_KV + r
            q_h = q_t[h * ATTN_HEAD_DIM:(h + 1) * ATTN_HEAD_DIM]
            inv = lax.rsqrt(jnp.mean(q_h * q_h, axis=0, keepdims=True) + EPS)
            q_heads.append((q_h * inv * q_gain).astype(BF16))
        q = jnp.concatenate(q_heads, axis=1)
        s = jnp.dot(k, q, preferred_element_type=F32) + bias_ref[first, g]
        sink = sinkrow_ref[g]
        m = jnp.maximum(jnp.max(s, axis=0, keepdims=True), sink)
        p = jnp.exp(s - m)
        denom = jnp.sum(p, axis=0, keepdims=True) + jnp.exp(sink - m)
        pv = lax.dot_general(v, p.astype(BF16), (((0,), (0,)), ((), ())), preferred_element_type=F32)
        pv = pv * (1.0 / denom)
        outs.extend(pv[:, r * WINDOW:(r + 1) * WINDOW] for r in range(ATTN_Q_PER_KV))
    o_ref[...] = jnp.concatenate(outs, axis=0).T.astype(o_ref.dtype)


def _t5_causal_bucket(dist):
    max_exact = REL_BUCKETS // 2
    d = jnp.maximum(dist, 1).astype(F32)
    large = max_exact + (jnp.log(d / max_exact) / math.log(REL_MAX_DIST / max_exact)
                         * (REL_BUCKETS - max_exact)).astype(jnp.int32)
    large = jnp.minimum(large, REL_BUCKETS - 1)
    return jnp.where(dist < max_exact, dist, large)


def _swa_mixer(qkv, q_norm_g, k_norm_g, sinks, rel_bias, batch, seq):
    t = qkv.shape[0]
    nb = seq // WINDOW
    q_w = ATTN_Q_HEADS * ATTN_HEAD_DIM
    kv_w2 = 2 * ATTN_KV_HEADS * ATTN_HEAD_DIM
    kj = jnp.arange(2 * WINDOW)[:, None]
    qi = jnp.arange(WINDOW)[None, :]
    bucket = _t5_causal_bucket(jnp.maximum(qi + WINDOW - kj, 0)).astype(jnp.int32)
    q_gain = jnp.broadcast_to((q_norm_g * (ATTN_HEAD_DIM ** -0.5))[:, None], (ATTN_HEAD_DIM, WINDOW))
    const2 = lambda b, i: (0, 0)
    smem = pl.BlockSpec(memory_space=pltpu.SMEM)
    return pl.pallas_call(
        _swa_kernel,
        grid=(batch, nb),
        in_specs=[
            pl.BlockSpec((WINDOW, q_w), lambda b, i: (b * nb + i, 0)),
            pl.BlockSpec((WINDOW, kv_w2), lambda b, i: (b * nb + i, q_w // kv_w2)),
            pl.BlockSpec((WINDOW, kv_w2), lambda b, i: (b * nb + jnp.maximum(i - 1, 0), q_w // kv_w2)),
            pl.BlockSpec((2 * WINDOW, WINDOW), const2),
            pl.BlockSpec((ATTN_HEAD_DIM, WINDOW), const2),
            pl.BlockSpec((1, ATTN_HEAD_DIM), const2),
            smem,
            smem,
        ],
        out_specs=pl.BlockSpec((WINDOW, q_w), lambda b, i: (b * nb + i, 0)),
        out_shape=jax.ShapeDtypeStruct((t, q_w), BF16),
        scratch_shapes=[pltpu.VMEM((2, ATTN_KV_HEADS, 2 * WINDOW, ATTN_Q_PER_KV * WINDOW), F32),
                        pltpu.VMEM((ATTN_KV_HEADS, 1, ATTN_Q_PER_KV * WINDOW), F32)],
        compiler_params=_cparams("arbitrary", "arbitrary"),
        name="swa_mixer",
    )(qkv, qkv, qkv, bucket, q_gain, k_norm_g.reshape(1, -1), rel_bias, sinks)


def _router_kernel(x_ref, g_ref, sh_ref, sc_ref, rwt_ref, rb_ref,
                   h_ref, lp_ref, gate_ref, tile_n_ref, tile_carry_ref, cnt_ref, carry_ref):
    i = pl.program_id(0)
    tm = x_ref.shape[0]

    @pl.when(i == 0)
    def _():
        carry_ref[...] = jnp.zeros_like(carry_ref)

    h = _norm_modulate(x_ref[...], g_ref[...], sh_ref[0], sc_ref[0])
    h_ref[...] = h.astype(h_ref.dtype)

    logits = lax.dot_general(rwt_ref[...], h, (((1,), (1,)), ((), ())),
                             preferred_element_type=F32, precision=HI) + rb_ref[...]
    e_iota = lax.broadcasted_iota(jnp.int32, logits.shape, 0)
    work = logits
    sels, vals = [], []
    for k in range(TOP_K):
        m = jnp.max(work, axis=0, keepdims=True)
        idx = jnp.min(jnp.where(work == m, e_iota, N_EXPERTS), axis=0, keepdims=True)
        sel = e_iota == idx
        work = jnp.where(sel, -jnp.inf, work)
        sels.append(sel)
        vals.append(m)
    exps = [jnp.exp(v - vals[0]) for v in vals]
    denom = exps[0] + exps[1] + exps[2] + exps[3]
    for k in range(TOP_K):
        gate_ref[k:k + 1, :] = exps[k] / denom

    chosen = sels[0] | sels[1] | sels[2] | sels[3]
    t_row = lax.broadcasted_iota(jnp.int32, (tm, tm), 0)
    t_col = lax.broadcasted_iota(jnp.int32, (tm, tm), 1)
    before = (t_row < t_col).astype(BF16)
    prior = jnp.dot(chosen.astype(BF16), before, preferred_element_type=F32)
    n = jnp.sum(chosen.astype(F32), axis=1, keepdims=True)
    e_row = lax.broadcasted_iota(jnp.int32, (N_EXPERTS, N_EXPERTS), 0)
    e_col = lax.broadcasted_iota(jnp.int32, (N_EXPERTS, N_EXPERTS), 1)
    run_start = jnp.dot((e_col < e_row).astype(F32), jnp.broadcast_to(n, (N_EXPERTS, LANES)),
                        preferred_element_type=F32, precision=HI)[:, :1]
    local = run_start + prior
    for k in range(TOP_K):
        lp_ref[k:k + 1, :] = jnp.sum(jnp.where(sels[k], local, 0.0), axis=0, keepdims=True).astype(jnp.int32)
    tile_n_ref[0] = n.astype(jnp.int32)
    tile_carry_ref[0] = carry_ref[...].astype(jnp.int32)
    total = carry_ref[...] + n
    carry_ref[...] = total
    cnt_ref[...] = total.astype(jnp.int32)


def _router(x, g, shift, scale, r_w, r_b, seq, tm):
    t, d = x.shape
    per_b = seq // tm
    ntiles = t // tm
    return pl.pallas_call(
        _router_kernel,
        grid=(t // tm,),
        in_specs=[
            pl.BlockSpec((tm, d), lambda i: (i, 0)),
            pl.BlockSpec((1, d), lambda i: (0, 0)),
            pl.BlockSpec((1, 1, d), lambda i: (i // per_b, 0, 0)),
            pl.BlockSpec((1, 1, d), lambda i: (i // per_b, 0, 0)),
            pl.BlockSpec((N_EXPERTS, d), lambda i: (0, 0)),
            pl.BlockSpec((N_EXPERTS, 1), lambda i: (0, 0)),
        ],
        out_specs=[
            pl.BlockSpec((tm, d), lambda i: (i, 0)),
            pl.BlockSpec((TOP_K, tm), lambda i: (0, i)),
            pl.BlockSpec((TOP_K, tm), lambda i: (0, i)),
            pl.BlockSpec((1, N_EXPERTS, 1), lambda i: (i, 0, 0)),
            pl.BlockSpec((1, N_EXPERTS, 1), lambda i: (i, 0, 0)),
            pl.BlockSpec((N_EXPERTS, 1), lambda i: (0, 0)),
        ],
        out_shape=[
            jax.ShapeDtypeStruct((t, d), BF16),
            jax.ShapeDtypeStruct((TOP_K, t), jnp.int32),
            jax.ShapeDtypeStruct((TOP_K, t), F32),
            jax.ShapeDtypeStruct((ntiles, N_EXPERTS, 1), jnp.int32),
            jax.ShapeDtypeStruct((ntiles, N_EXPERTS, 1), jnp.int32),
            jax.ShapeDtypeStruct((N_EXPERTS, 1), jnp.int32),
        ],
        scratch_shapes=[pltpu.VMEM((N_EXPERTS, 1), F32)],
        compiler_params=_cparams("arbitrary"),
        name="moe_router",
    )(x, g.reshape(1, d), shift, scale, r_w.T, r_b.reshape(-1, 1))


def _run_copies(n, src_ref, src_row, dst_ref, dst_row, sem, max_rows, fn):
    for b in range(max_rows.bit_length()):
        size = 1 << b

        @pl.when((n & size) != 0)
        def _():
            lo = n & (size - 1)
            fn(pltpu.make_async_copy(
                src_ref.at[pl.ds(pl.multiple_of((src_row + lo) * ROW_TILES, ROW_TILES), size * ROW_TILES)],
                dst_ref.at[pl.ds(pl.multiple_of((dst_row + lo) * ROW_TILES, ROW_TILES), size * ROW_TILES)],
                sem))


def _tile_runs(tile, n_ref, carry_ref, gstart_ref, sorted_ref, grouped_ref, sem, tm, fn, to_grouped):
    def body(e, run_start):
        n = n_ref[tile * N_EXPERTS + e]
        slot0 = gstart_ref[e] + carry_ref[tile * N_EXPERTS + e]
        if to_grouped:
            _run_copies(n, sorted_ref, run_start, grouped_ref, slot0, sem, tm, fn)
        else:
            _run_copies(n, grouped_ref, slot0, sorted_ref, run_start, sem, tm, fn)
        return run_start + n

    lax.fori_loop(0, N_EXPERTS, body, 0)


def _block_copy(src_ref, dst_ref, dst_blk, sem):
    rows = MOE_BLOCK * ROW_TILES
    return pltpu.make_async_copy(src_ref, dst_ref.at[pl.ds(pl.multiple_of(dst_blk * rows, rows), rows)], sem)


def _dispatch_kernel(n_ref, carry_ref, gstart_ref, pad_lo_ref, pad_hi_ref, nused_ref, lp_ref, h_ref, xg_ref,
                     sorted_ref, zero_ref, run_sem, sem):
    i = pl.program_id(0)
    last = pl.num_programs(0) - 1
    tm = h_ref.shape[0]
    rows = TOP_K * tm
    slot = i % 2
    nblk = xg_ref.shape[0] // (MOE_BLOCK * ROW_TILES)

    @pl.when(i == 0)
    def _():
        zero_ref[...] = jnp.zeros_like(zero_ref)

        def fill(b, carry):
            _block_copy(zero_ref, xg_ref, b, sem).start()
            return carry

        def fill_wait(b, carry):
            _block_copy(zero_ref, xg_ref, b, sem).wait()
            return carry

        lax.fori_loop(nused_ref[0], nblk, fill, 0)
        lax.fori_loop(nused_ref[0], nblk, fill_wait, 0)

    @pl.when(i < N_EXPERTS)
    def _():
        n_pad = pad_hi_ref[i] - pad_lo_ref[i]
        for fn in (lambda cp: cp.start(), lambda cp: cp.wait()):
            _run_copies(n_pad, zero_ref, 0, xg_ref, pad_lo_ref[i], sem, MOE_BLOCK - 1, fn)

    row_id = lax.broadcasted_iota(jnp.int32, (rows, tm), 0)
    perm = row_id == lp_ref[0:1, :]
    for k in range(1, TOP_K):
        perm = perm | (row_id == lp_ref[k:k + 1, :])
    srt = jnp.dot(jnp.where(perm, 1.0, 0.0).astype(BF16), h_ref[...], preferred_element_type=F32)
    for s in range(ROW_TILES):
        sorted_ref[slot, pl.ds(s, rows, stride=ROW_TILES), :] = srt[:, s * LANES:(s + 1) * LANES]

    def runs(tile, buf, fn):
        _tile_runs(tile, n_ref, carry_ref, gstart_ref, sorted_ref.at[buf], xg_ref, run_sem, tm, fn, True)

    def wait_tile(buf):
        pltpu.make_async_copy(sorted_ref.at[buf], xg_ref.at[pl.ds(0, rows * ROW_TILES)], run_sem).wait()

    @pl.when(i > 0)
    def _():
        wait_tile(1 - slot)

    runs(i, slot, lambda cp: cp.start())

    @pl.when(i == last)
    def _():
        wait_tile(slot)


def _dispatch(tile_n, tile_carry, group_start, pad_lo, pad_hi, nused, lp_t, h, cap, tm):
    t, d = h.shape
    assert t // tm >= N_EXPERTS
    grid_spec = pltpu.PrefetchScalarGridSpec(
        num_scalar_prefetch=6,
        grid=(t // tm,),
        in_specs=[
            pl.BlockSpec((TOP_K, tm), lambda i, *_: (0, i)),
            pl.BlockSpec((tm, d), lambda i, *_: (i, 0)),
        ],
        out_specs=pl.BlockSpec(memory_space=pl.ANY),
        scratch_shapes=[
            pltpu.VMEM((2, TOP_K * tm * ROW_TILES, LANES), F32),
            pltpu.VMEM((MOE_BLOCK * ROW_TILES, LANES), F32),
            pltpu.SemaphoreType.DMA, pltpu.SemaphoreType.DMA,
        ],
    )
    return pl.pallas_call(
        _dispatch_kernel,
        grid_spec=grid_spec,
        out_shape=jax.ShapeDtypeStruct((cap * ROW_TILES, LANES), F32),
        compiler_params=pltpu.CompilerParams(dimension_semantics=("arbitrary",), has_side_effects=True,
                                             vmem_limit_bytes=VMEM_LIMIT),
        name="moe_dispatch",
    )(tile_n, tile_carry, group_start, pad_lo, pad_hi, nused, lp_t, h)


def _weight_copies(w1_hbm, w2_hbm, w1buf, w2buf, sems, layer, e, slot):
    return (pltpu.make_async_copy(w1_hbm.at[layer, e], w1buf.at[slot], sems.at[0, slot]),
            pltpu.make_async_copy(w2_hbm.at[layer, e], w2buf.at[slot], sems.at[1, slot]))


def _expert_kernel(blk_e_ref, nused_ref, next_e_ref, slot_ref, blk_rows_ref, x_ref, w1_hbm, w2_hbm, bg_ref, bl_ref, b2_ref,
                   sel_ref, o_ref, w1buf, w2buf, wg_s, wl_s, w2_s, sems, *, layer):
    blk = pl.program_id(0)
    active = blk < nused_ref[0]
    e = blk_e_ref[blk]
    slot = slot_ref[e]
    new_expert = (blk == 0) | (e != blk_e_ref[jnp.maximum(blk - 1, 0)])
    copies = functools.partial(_weight_copies, w1_hbm, w2_hbm, w1buf, w2buf, sems, layer)

    @pl.when(blk == 0)
    def _():
        for cp in copies(e, slot):
            cp.start()

    @pl.when(active & new_expert)
    def _():
        nxt = next_e_ref[e]

        @pl.when(nxt >= 0)
        def _():
            for cp in copies(nxt, 1 - slot):
                cp.start()

        for cp in copies(e, slot):
            cp.wait()
        half = SPLIT_W // 2
        for j in range(w1buf.shape[2] // SPLIT_W):
            chunk = w1buf[slot, :, j * SPLIT_W:(j + 1) * SPLIT_W].astype(BF16)
            split = jnp.dot(chunk, sel_ref[...], preferred_element_type=F32)
            wg_s[:, j * half:(j + 1) * half] = split[:, :half].astype(BF16)
            wl_s[:, j * half:(j + 1) * half] = split[:, half:].astype(BF16)
        w2_s[...] = w2buf[slot].astype(BF16)

    def mlp(rows):
        x = jnp.concatenate([x_ref[pl.ds(s, rows, stride=ROW_TILES), :]
                             for s in range(ROW_TILES)], axis=1).astype(BF16)
        glu = jnp.dot(x, wg_s[...], preferred_element_type=F32) + bg_ref[0]
        lin = jnp.dot(x, wl_s[...], preferred_element_type=F32) + bl_ref[0]
        glu = jnp.minimum(glu, SWIGLU_LIMIT)
        lin = jnp.clip(lin, -SWIGLU_LIMIT, SWIGLU_LIMIT)
        act = glu * jax.nn.sigmoid(SWIGLU_ALPHA * glu) * (lin + 1.0)
        y = jnp.dot(act.astype(BF16), w2_s[...], preferred_element_type=F32) + b2_ref[0]
        for s in range(ROW_TILES):
            o_ref[pl.ds(s, rows, stride=ROW_TILES), :] = y[:, s * LANES:(s + 1) * LANES]
        if rows < MOE_BLOCK:
            o_ref[pl.ds(rows * ROW_TILES, (MOE_BLOCK - rows) * ROW_TILES), :] = jnp.zeros(
                ((MOE_BLOCK - rows) * ROW_TILES, LANES), o_ref.dtype)

    valid_rows = jnp.where(active, blk_rows_ref[blk], 0)

    @pl.when(valid_rows > MOE_SUB)
    def _():
        mlp(MOE_BLOCK)

    @pl.when((valid_rows > 0) & (valid_rows <= MOE_SUB))
    def _():
        mlp(MOE_SUB)

    @pl.when(valid_rows <= 0)
    def _():
        o_ref[...] = jnp.zeros_like(o_ref)


def _experts(layer, blk_e, nused, next_e, e_slot, blk_rows, xg_rows, w1_all, w2_all, b_glu, b_lin, b2):
    nblk = blk_e.shape[0]
    d = D_MODEL
    ff = w2_all.shape[2]
    col = jnp.arange(SPLIT_W)
    sel = (jnp.arange(SPLIT_W)[None, :] == ((col % 2) * (SPLIT_W // 2) + col // 2)[:, None]).astype(BF16)

    def blk_map(b, blk_e_ref, nused_ref, *_):
        return (jnp.minimum(b, nused_ref[0] - 1), 0)

    def e_map(b, blk_e_ref, nused_ref, *_):
        return (blk_e_ref[jnp.minimum(b, nused_ref[0] - 1)], 0, 0)

    grid_spec = pltpu.PrefetchScalarGridSpec(
        num_scalar_prefetch=5,
        grid=(nblk,),
        in_specs=[
            pl.BlockSpec((MOE_BLOCK * ROW_TILES, LANES), blk_map),
            pl.BlockSpec(memory_space=pl.ANY),
            pl.BlockSpec(memory_space=pl.ANY),
            pl.BlockSpec((1, 1, ff), e_map),
            pl.BlockSpec((1, 1, ff), e_map),
            pl.BlockSpec((1, 1, d), e_map),
            pl.BlockSpec((SPLIT_W, SPLIT_W), lambda b, *_: (0, 0)),
        ],
        out_specs=pl.BlockSpec((MOE_BLOCK * ROW_TILES, LANES), lambda b, *_: (b, 0)),
        scratch_shapes=[
            pltpu.VMEM((2, d, 2 * ff), F32), pltpu.VMEM((2, ff, d), F32),
            pltpu.VMEM((d, ff), BF16), pltpu.VMEM((d, ff), BF16), pltpu.VMEM((ff, d), BF16),
            pltpu.SemaphoreType.DMA((2, 2)),
        ],
    )
    return pl.pallas_call(
        functools.partial(_expert_kernel, layer=layer),
        grid_spec=grid_spec,
        out_shape=jax.ShapeDtypeStruct(xg_rows.shape, F32),
        compiler_params=_cparams("arbitrary"),
        name="moe_experts",
    )(blk_e, nused, next_e, e_slot, blk_rows, xg_rows, w1_all, w2_all, b_glu, b_lin, b2, sel)


def _combine_kernel(n_ref, carry_ref, gstart_ref, yg_ref, lp_ref, gate_ref, x_ref, g2_ref, o_ref, sorted_ref, sem):
    i = pl.program_id(0)
    ntiles = pl.num_programs(0)
    tm = x_ref.shape[0]
    rows = TOP_K * tm
    slot = i % 2

    def runs(tile, buf, fn):
        _tile_runs(tile, n_ref, carry_ref, gstart_ref, sorted_ref.at[buf], yg_ref, sem.at[buf], tm, fn, False)

    @pl.when(i == 0)
    def _():
        runs(0, 0, lambda cp: cp.start())

    @pl.when(i + 1 < ntiles)
    def _():
        runs(i + 1, 1 - slot, lambda cp: cp.start())

    eye = (lax.broadcasted_iota(jnp.int32, (tm, tm), 0) ==
           lax.broadcasted_iota(jnp.int32, (tm, tm), 1)).astype(F32)
    rows_t = jnp.concatenate([gate_ref[...], lp_ref[...].astype(F32)], axis=0)
    cols = lax.dot_general(eye, rows_t, (((1,), (1,)), ((), ())), preferred_element_type=F32, precision=HI)
    col_id = lax.broadcasted_iota(jnp.int32, (tm, rows), 1)
    weights = jnp.zeros((tm, rows), F32)
    for k in range(TOP_K):
        lp_col = (cols[:, TOP_K + k:TOP_K + k + 1] + 0.5).astype(jnp.int32)
        weights = weights + jnp.where(col_id == lp_col, cols[:, k:k + 1], 0.0)

    pltpu.make_async_copy(yg_ref.at[pl.ds(0, rows * ROW_TILES)], sorted_ref.at[slot], sem.at[slot]).wait()
    y = jnp.concatenate(
        [sorted_ref[slot, pl.ds(s, rows, stride=ROW_TILES), :] for s in range(ROW_TILES)], axis=1).astype(BF16)
    acc = jnp.dot(weights.astype(BF16), y, preferred_element_type=F32)
    o_ref[...] = x_ref[...] + g2_ref[0] * acc


def _combine(tile_n, tile_carry, group_start, yg_rows, lp_t, gate_t, x, gate2, seq, tm):
    t, d = x.shape
    per_b = seq // tm
    grid_spec = pltpu.PrefetchScalarGridSpec(
        num_scalar_prefetch=3,
        grid=(t // tm,),
        in_specs=[
            pl.BlockSpec(memory_space=pl.ANY),
            pl.BlockSpec((TOP_K, tm), lambda i, *_: (0, i)),
            pl.BlockSpec((TOP_K, tm), lambda i, *_: (0, i)),
            pl.BlockSpec((tm, d), lambda i, *_: (i, 0)),
            pl.BlockSpec((1, 1, d), lambda i, *_: (i // per_b, 0, 0)),
        ],
        out_specs=pl.BlockSpec((tm, d), lambda i, *_: (i, 0)),
        scratch_shapes=[pltpu.VMEM((2, TOP_K * tm * ROW_TILES, LANES), F32), pltpu.SemaphoreType.DMA((2,))],
    )
    return pl.pallas_call(
        _combine_kernel,
        grid_spec=grid_spec,
        out_shape=jax.ShapeDtypeStruct((t, d), F32),
        compiler_params=_cparams("arbitrary"),
        name="moe_combine",
    )(tile_n, tile_carry, group_start, yg_rows, lp_t, gate_t, x, gate2)


def _moe(layer, x, g, shift, scale, gate2, r_w, r_b, w1_all, b1, w2_all, b2, seq):
    t = x.shape[0]
    h, lp_t, gate_t, tile_n, tile_carry, counts = _router(x, g, shift, scale, r_w, r_b, seq, MOE_TILE)
    tile_n = tile_n.reshape(-1)
    tile_carry = tile_carry.reshape(-1)
    counts = counts.reshape(-1)
    padded = ((counts + MOE_BLOCK - 1) // MOE_BLOCK) * MOE_BLOCK
    group_end = jnp.cumsum(padded)
    group_start = (group_end - padded).astype(jnp.int32)
    cap = t * TOP_K + N_EXPERTS * MOE_BLOCK
    nblk = cap // MOE_BLOCK
    blk_start = jnp.arange(nblk, dtype=jnp.int32) * MOE_BLOCK
    blk_e = jnp.minimum(jnp.sum(blk_start[:, None] >= group_end[None, :], axis=1), N_EXPERTS - 1).astype(jnp.int32)
    nused = (group_end[-1:] // MOE_BLOCK).astype(jnp.int32)

    pad_lo = (group_start + counts).astype(jnp.int32)
    pad_hi = group_end.astype(jnp.int32)
    xg_rows = _dispatch(tile_n, tile_carry, group_start, pad_lo, pad_hi, nused, lp_t, h, cap, MOE_TILE)
    nonempty = counts > 0
    e_ids = jnp.arange(N_EXPERTS, dtype=jnp.int32)
    later = jnp.where(nonempty[None, :] & (e_ids[None, :] > e_ids[:, None]), e_ids[None, :], N_EXPERTS)
    next_e = jnp.min(later, axis=1)
    next_e = jnp.where(next_e == N_EXPERTS, -1, next_e).astype(jnp.int32)
    e_slot = ((jnp.cumsum(nonempty.astype(jnp.int32)) - 1) % 2).astype(jnp.int32)
    blk_rows = jnp.clip(pad_lo[blk_e] - blk_start, 0, MOE_BLOCK).astype(jnp.int32)
    yg_rows = _experts(layer, blk_e, nused, next_e, e_slot, blk_rows, xg_rows, w1_all, w2_all,
                       b1[:, None, 0::2], b1[:, None, 1::2], b2[:, None, :])
    return _combine(tile_n, tile_carry, group_start, yg_rows, lp_t, gate_t, x, gate2, seq, MOE_TILE)


def kernel(x, c, ada_w, ada_b, norm1_g, norm2_g, m_in_w, m_conv_w, m_conv_b, m_dt_bias, m_A_log, m_D, m_norm_g, m_out_w, a_qkv_w, a_q_norm_g, a_k_norm_g, a_sinks, a_out_w, rel_bias, r_w, r_b, e_w1, e_b1, e_w2, e_b2):
    batch, seq, d = x.shape
    depth = ada_w.shape[0]
    t = batch * seq
    xf = x.reshape(t, d)

    c_pad = jnp.zeros((SUBLANES, d), F32).at[:batch].set(c)
    mod = _adaln(c_pad, ada_w, ada_b)[:, :batch]

    for i in range(depth):
        parts = [mod[i, :, p * d:(p + 1) * d].reshape(batch, 1, d) for p in range(6)]
        sh1, sc1, g1, sh2, sc2, g2 = parts
        j = i // 2
        if i % 2 == 0:
            w_in = jnp.pad(m_in_w[j], ((0, 0), (0, SSM_IN_PAD - m_in_w.shape[2]))).astype(BF16)
            zxbcdt = _norm_matmul(xf, norm1_g[i], sh1, sc1, w_in, seq)
            y = _ssd_mixer(zxbcdt, m_conv_w[j], m_conv_b[j], m_dt_bias[j], m_A_log[j], m_D[j],
                           m_norm_g[j], batch, seq)
            xf = _matmul_residual(y, m_out_w[j].astype(BF16), xf, g1, seq)
        else:
            qkv = _norm_matmul(xf, norm1_g[i], sh1, sc1, a_qkv_w[j].astype(BF16), seq)
            y = _swa_mixer(qkv, a_q_norm_g[j], a_k_norm_g[j], a_sinks[j], rel_bias, batch, seq)
            xf = _matmul_residual(y, a_out_w[j].astype(BF16), xf, g1, seq)
        xf = _moe(i, xf, norm2_g[i], sh2, sc2, g2, r_w[i], r_b[i], e_w1, e_b1[i], e_w2, e_b2[i], seq)
    return xf.reshape(batch, seq, d)
```

```python
import functools
import math

import jax
import jax.numpy as jnp
from jax import lax
from jax.experimental import pallas as pl
from jax.experimental.pallas import tpu as pltpu

D_MODEL = 1024
EPS = 1e-6
LANES = 128
SUBLANES = 8
ROW_TILES = D_MODEL // LANES

SSM_D_INNER = 2048
SSM_HEAD_DIM = 64
SSM_HEADS = 32
SSM_GROUPS = 4
SSM_STATE = 128
SSM_CONV = 4
SSM_CHUNK = 128
SSM_GN = SSM_GROUPS * SSM_STATE
SSM_ZXBC = 2 * SSM_D_INNER + 2 * SSM_GN
SSM_GROUP_W = SSM_D_INNER // SSM_GROUPS

ATTN_HEAD_DIM = 64
ATTN_Q_HEADS = 16
ATTN_KV_HEADS = 4
ATTN_Q_PER_KV = 4
WINDOW = 128
REL_BUCKETS = 32
REL_MAX_DIST = 128

N_EXPERTS = 32
TOP_K = 4
SWIGLU_ALPHA = 1.702
SWIGLU_LIMIT = 7.0
MOE_BLOCK = 512
MOE_SUB = 256
MOE_TILE = 256
SPLIT_W = 256

VMEM_LIMIT = 56 * 1024 * 1024
HI = lax.Precision.HIGHEST
F32 = jnp.float32
BF16 = jnp.bfloat16


def _cparams(*sem):
    return pltpu.CompilerParams(dimension_semantics=sem, vmem_limit_bytes=VMEM_LIMIT)


def _norm_modulate(x, g, shift, scale):
    ms = jnp.mean(x * x, axis=-1, keepdims=True)
    return x * lax.rsqrt(ms + EPS) * g * (1.0 + scale) + shift


def _adaln_kernel(c_ref, w_ref, b_ref, o_ref):
    c = c_ref[...]
    c_act = c * jax.nn.sigmoid(c)
    o_ref[0] = jnp.dot(c_act, w_ref[0], preferred_element_type=F32, precision=HI) + b_ref[0]


def _adaln(c_pad, ada_w, ada_b):
    depth, d, n = ada_w.shape
    tn = 1536
    return pl.pallas_call(
        _adaln_kernel,
        grid=(depth, n // tn),
        in_specs=[
            pl.BlockSpec((SUBLANES, d), lambda i, j: (0, 0)),
            pl.BlockSpec((1, d, tn), lambda i, j: (i, 0, j)),
            pl.BlockSpec((1, 1, tn), lambda i, j: (i, 0, j)),
        ],
        out_specs=pl.BlockSpec((1, SUBLANES, tn), lambda i, j: (i, 0, j)),
        out_shape=jax.ShapeDtypeStruct((depth, SUBLANES, n), F32),
        compiler_params=_cparams("arbitrary", "arbitrary"),
        name="adaln",
    )(c_pad, ada_w, ada_b.reshape(depth, 1, n))


def _norm_matmul_kernel(x_ref, g_ref, sh_ref, sc_ref, *refs):
    w_refs, o_refs = refs[:len(refs) // 2], refs[len(refs) // 2:]
    h = _norm_modulate(x_ref[...], g_ref[...], sh_ref[0], sc_ref[0]).astype(BF16)
    for w_ref, o_ref in zip(w_refs, o_refs):
        o_ref[...] = jnp.dot(h, w_ref[...], preferred_element_type=F32)


def _norm_matmul(x, g, shift, scale, weights_bf16, seq, tm=512):
    t, d = x.shape
    per_b = seq // tm
    return pl.pallas_call(
        _norm_matmul_kernel,
        grid=(t // tm,),
        in_specs=[
            pl.BlockSpec((tm, d), lambda i: (i, 0)),
            pl.BlockSpec((1, d), lambda i: (0, 0)),
            pl.BlockSpec((1, 1, d), lambda i: (i // per_b, 0, 0)),
            pl.BlockSpec((1, 1, d), lambda i: (i // per_b, 0, 0)),
        ] + [pl.BlockSpec(w.shape, lambda i: (0, 0), pipeline_mode=pl.Buffered(1))
             for w in weights_bf16],
        out_specs=[pl.BlockSpec((tm, w.shape[1]), lambda i: (i, 0)) for w in weights_bf16],
        out_shape=[jax.ShapeDtypeStruct((t, w.shape[1]), F32) for w in weights_bf16],
        compiler_params=_cparams("arbitrary"),
        name="norm_matmul",
    )(x, g.reshape(1, d), shift, scale, *weights_bf16)


def _matmul_residual_kernel(y_ref, w_ref, x_ref, gate_ref, o_ref):
    acc = jnp.dot(y_ref[...], w_ref[...], preferred_element_type=F32)
    o_ref[...] = x_ref[...] + gate_ref[0] * acc


def _matmul_residual(y_bf16, w_bf16, x, gate, seq, tm=512):
    t, k = y_bf16.shape
    d = x.shape[1]
    per_b = seq // tm
    return pl.pallas_call(
        _matmul_residual_kernel,
        grid=(t // tm,),
        in_specs=[
            pl.BlockSpec((tm, k), lambda i: (i, 0)),
            pl.BlockSpec((k, d), lambda i: (0, 0)),
            pl.BlockSpec((tm, d), lambda i: (i, 0)),
            pl.BlockSpec((1, 1, d), lambda i: (i // per_b, 0, 0)),
        ],
        out_specs=pl.BlockSpec((tm, d), lambda i: (i, 0)),
        out_shape=jax.ShapeDtypeStruct((t, d), F32),
        compiler_params=_cparams("arbitrary"),
        name="matmul_residual",
    )(y_bf16, w_bf16, x, gate)


def _causal_conv_silu(cur, prev_tail, w, b):
    rows = lax.broadcasted_iota(jnp.int32, (SUBLANES, cur.shape[1]), 0)
    acc = b + w[SSM_CONV - 1:SSM_CONV] * cur
    for d in range(1, SSM_CONV):
        rolled = pltpu.roll(cur, d, axis=0)
        top = jnp.where(rows < d, pltpu.roll(prev_tail, d, axis=0), rolled[0:SUBLANES])
        shifted = jnp.concatenate([top, rolled[SUBLANES:]], axis=0)
        acc = acc + w[SSM_CONV - 1 - d:SSM_CONV - d] * shifted
    return acc * jax.nn.sigmoid(acc)


def _split3(x):
    hi = x.astype(BF16)
    rem = x - hi.astype(F32)
    mid = rem.astype(BF16)
    lo = (rem - mid.astype(F32)).astype(BF16)
    return jnp.concatenate([hi, mid, lo], axis=1)


def _ssd_kernel(z_ref, xs_ref, bc_ref, dt_ref, cwx_ref, cbx_ref, cwb_ref, cbb_ref, dtb_ref, alog_ref,
                dskip_ref, ng_ref, hexp_ref, lexp_ref, o_ref, tailx_ref, tailb_ref, state_ref):
    c = pl.program_id(1)

    @pl.when(c == 0)
    def _():
        tailx_ref[...] = jnp.zeros_like(tailx_ref)
        tailb_ref[...] = jnp.zeros_like(tailb_ref)
        state_ref[...] = jnp.zeros_like(state_ref)

    xs_raw = xs_ref[...]
    bc_raw = bc_ref[...]
    xs = _causal_conv_silu(xs_raw, tailx_ref[...], cwx_ref[...], cbx_ref[...])
    bc = _causal_conv_silu(bc_raw, tailb_ref[...], cwb_ref[...], cbb_ref[...])
    tailx_ref[...] = xs_raw[SSM_CHUNK - SUBLANES:]
    tailb_ref[...] = bc_raw[SSM_CHUNK - SUBLANES:]

    dt_in = dt_ref[...][:, :SSM_HEADS] + dtb_ref[...]
    dt = jnp.maximum(dt_in, 0.0) + jnp.log1p(jnp.exp(-jnp.abs(dt_in)))
    a_neg = -jnp.exp(alog_ref[...])
    d_a = dt * a_neg
    li = lax.broadcasted_iota(jnp.int32, (SSM_CHUNK, SSM_CHUNK), 0)
    si = lax.broadcasted_iota(jnp.int32, (SSM_CHUNK, SSM_CHUNK), 1)
    causal = li >= si
    tri = causal.astype(F32)
    a_cs = jnp.dot(tri, d_a, preferred_element_type=F32, precision=HI)
    a_cs_t = lax.dot_general(d_a, tri, (((0,), (1,)), ((), ())),
                             preferred_element_type=F32, precision=HI)
    a_last = a_cs[SSM_CHUNK - 1:SSM_CHUNK]
    e_out = jnp.exp(a_cs)
    e_state = jnp.exp(a_last - a_cs) * dt
    small = jnp.concatenate([dt, e_out, e_state], axis=0)
    wide = jnp.dot(_split3(small), hexp_ref[...], preferred_element_type=F32)
    dt_w = wide[0:SSM_CHUNK]
    e_out_w = wide[SSM_CHUNK:2 * SSM_CHUNK]
    e_state_w = wide[2 * SSM_CHUNK:]
    a_col = jnp.dot(_split3(a_cs), lexp_ref[...], preferred_element_type=F32)

    x_dt = (xs * dt_w).astype(BF16)
    x_state = (xs * e_state_w).astype(BF16)
    chunk_decay_w = e_out_w[SSM_CHUNK - 1:SSM_CHUNK]

    heads_per_group = SSM_HEADS // SSM_GROUPS
    y_parts = []
    for g in range(SSM_GROUPS):
        b_g = bc[:, g * SSM_STATE:(g + 1) * SSM_STATE].astype(BF16)
        c_g = bc[:, SSM_GN + g * SSM_STATE:SSM_GN + (g + 1) * SSM_STATE].astype(BF16)
        cb = lax.dot_general(c_g, b_g, (((1,), (1,)), ((), ())), preferred_element_type=F32)
        gsl = slice(g * SSM_GROUP_W, (g + 1) * SSM_GROUP_W)
        h_prev = state_ref[g]
        y_off = jnp.dot(c_g, h_prev.astype(BF16), preferred_element_type=F32) * e_out_w[:, gsl]
        diag = []
        for r in range(heads_per_group):
            h = g * heads_per_group + r
            seg = a_col[:, h * SSM_CHUNK:(h + 1) * SSM_CHUNK] - a_cs_t[h:h + 1, :]
            decay = jnp.exp(jnp.where(causal, seg, -jnp.inf))
            m = (cb * decay).astype(BF16)
            diag.append(jnp.dot(m, x_dt[:, h * SSM_HEAD_DIM:(h + 1) * SSM_HEAD_DIM],
                                preferred_element_type=F32))
        y_parts.append(jnp.concatenate(diag, axis=1) + y_off)
        upd = lax.dot_general(b_g, x_state[:, gsl], (((0,), (0,)), ((), ())), preferred_element_type=F32)
        state_ref[g] = h_prev * chunk_decay_w[:, gsl] + upd

    y = jnp.concatenate(y_parts, axis=1) + dskip_ref[...] * xs
    z = z_ref[...]
    y = y * (z * jax.nn.sigmoid(z))
    normed = []
    for g in range(SSM_GROUPS):
        y_g = y[:, g * SSM_GROUP_W:(g + 1) * SSM_GROUP_W]
        normed.append(y_g * lax.rsqrt(jnp.mean(y_g * y_g, axis=-1, keepdims=True) + EPS))
    o_ref[...] = (jnp.concatenate(normed, axis=1) * ng_ref[...]).astype(o_ref.dtype)


def _ssd_mixer(zxbc, dt_raw, conv_w, conv_b, dt_bias, a_log, d_skip, norm_g, batch, seq):
    t = zxbc.shape[0]
    nc = seq // SSM_CHUNK
    head_expand = jnp.tile(jnp.repeat(jnp.eye(SSM_HEADS, dtype=BF16), SSM_HEAD_DIM, axis=1), (3, 1))
    lane_expand = jnp.tile(jnp.repeat(jnp.eye(SSM_HEADS, dtype=BF16), SSM_CHUNK, axis=1), (3, 1))
    row = lambda b, c: (b * nc + c, 0)
    const2 = lambda b, c: (0, 0)
    bc_w = 2 * SSM_GN
    return pl.pallas_call(
        _ssd_kernel,
        grid=(batch, nc),
        in_specs=[
            pl.BlockSpec((SSM_CHUNK, SSM_D_INNER), row),
            pl.BlockSpec((SSM_CHUNK, SSM_D_INNER), lambda b, c: (b * nc + c, 1)),
            pl.BlockSpec((SSM_CHUNK, bc_w), lambda b, c: (b * nc + c, 2 * SSM_D_INNER // bc_w)),
            pl.BlockSpec((SSM_CHUNK, LANES), row),
            pl.BlockSpec((SSM_CONV, SSM_D_INNER), const2),
            pl.BlockSpec((1, SSM_D_INNER), const2),
            pl.BlockSpec((SSM_CONV, bc_w), const2),
            pl.BlockSpec((1, bc_w), const2),
            pl.BlockSpec((1, SSM_HEADS), const2),
            pl.BlockSpec((1, SSM_HEADS), const2),
            pl.BlockSpec((1, SSM_D_INNER), const2),
            pl.BlockSpec((1, SSM_D_INNER), const2),
            pl.BlockSpec((3 * SSM_HEADS, SSM_D_INNER), const2),
            pl.BlockSpec((3 * SSM_HEADS, SSM_HEADS * SSM_CHUNK), const2),
        ],
        out_specs=pl.BlockSpec((SSM_CHUNK, SSM_D_INNER), row),
        out_shape=jax.ShapeDtypeStruct((t, SSM_D_INNER), BF16),
        scratch_shapes=[
            pltpu.VMEM((SUBLANES, SSM_D_INNER), F32),
            pltpu.VMEM((SUBLANES, bc_w), F32),
            pltpu.VMEM((SSM_GROUPS, SSM_STATE, SSM_GROUP_W), F32),
        ],
        compiler_params=_cparams("arbitrary", "arbitrary"),
        name="ssd_mixer",
    )(zxbc, zxbc, zxbc, dt_raw,
      conv_w[:, :SSM_D_INNER], conv_b[:SSM_D_INNER].reshape(1, -1),
      conv_w[:, SSM_D_INNER:], conv_b[SSM_D_INNER:].reshape(1, -1),
      dt_bias.reshape(1, -1), a_log.reshape(1, -1),
      jnp.repeat(d_skip, SSM_HEAD_DIM).reshape(1, -1), norm_g.reshape(1, -1),
      head_expand, lane_expand)


def _head_rms(x, g):
    return x * lax.rsqrt(jnp.mean(x * x, axis=-1, keepdims=True) + EPS) * g


def _swa_kernel(q_ref, kvc_ref, kvp_ref, bucket_ref, qg_ref, kg_ref, rel_ref, sink_ref, o_ref,
                bias_ref, sinkrow_ref):
    b = pl.program_id(0)
    i = pl.program_id(1)

    @pl.when((b == 0) & (i == 0))
    def _():
        bucket = bucket_ref[...]
        kj = lax.broadcasted_iota(jnp.int32, (2 * WINDOW, WINDOW), 0)
        qi = lax.broadcasted_iota(jnp.int32, (2 * WINDOW, WINDOW), 1)
        dist = qi + WINDOW - kj
        band = (dist >= 0) & (dist < WINDOW)
        for h in range(ATTN_Q_HEADS):
            g, r = divmod(h, ATTN_Q_PER_KV)
            acc = jnp.zeros(bucket.shape, F32)
            for k in range(REL_BUCKETS):
                acc = jnp.where(bucket == k, rel_ref[k, h], acc)
            cols = slice(r * WINDOW, (r + 1) * WINDOW)
            bias_ref[0, g, :, cols] = jnp.where(band, acc, -jnp.inf)
            bias_ref[1, g, :, cols] = jnp.where(band & (kj >= WINDOW), acc, -jnp.inf)
            sinkrow_ref[g, :, cols] = jnp.full((1, WINDOW), sink_ref[h], F32)

    first = (i == 0).astype(jnp.int32)
    q_t = q_ref[...].T
    kv_c = kvc_ref[...]
    kv_p = kvp_ref[...]
    kv_w = ATTN_KV_HEADS * ATTN_HEAD_DIM
    q_gain = qg_ref[...]
    outs = []
    for g in range(ATTN_KV_HEADS):
        ksl = slice(g * ATTN_HEAD_DIM, (g + 1) * ATTN_HEAD_DIM)
        vsl = slice(kv_w + g * ATTN_HEAD_DIM, kv_w + (g + 1) * ATTN_HEAD_DIM)
        k = jnp.concatenate([kv_p[:, ksl], kv_c[:, ksl]], axis=0)
        v = jnp.concatenate([kv_p[:, vsl], kv_c[:, vsl]], axis=0).astype(BF16)
        k = _head_rms(k, kg_ref[...]).astype(BF16)
        q_heads = []
        for r in range(ATTN_Q_PER_KV):
            h = g * ATTN_Q_PER_KV + r
            q_h = q_t[h * ATTN_HEAD_DIM:(h + 1) * ATTN_HEAD_DIM]
            inv = lax.rsqrt(jnp.mean(q_h * q_h, axis=0, keepdims=True) + EPS)
            q_heads.append((q_h * inv * q_gain).astype(BF16))
        q = jnp.concatenate(q_heads, axis=1)
        s = jnp.dot(k, q, preferred_element_type=F32) + bias_ref[first, g]
        sink = sinkrow_ref[g]
        m = jnp.maximum(jnp.max(s, axis=0, keepdims=True), sink)
        p = jnp.exp(s - m)
        denom = jnp.sum(p, axis=0, keepdims=True) + jnp.exp(sink - m)
        pv = lax.dot_general(v, p.astype(BF16), (((0,), (0,)), ((), ())), preferred_element_type=F32)
        pv = pv * (1.0 / denom)
        outs.extend(pv[:, r * WINDOW:(r + 1) * WINDOW] for r in range(ATTN_Q_PER_KV))
    o_ref[...] = jnp.concatenate(outs, axis=0).T.astype(o_ref.dtype)


def _t5_causal_bucket(dist):
    max_exact = REL_BUCKETS // 2
    d = jnp.maximum(dist, 1).astype(F32)
    large = max_exact + (jnp.log(d / max_exact) / math.log(REL_MAX_DIST / max_exact)
                         * (REL_BUCKETS - max_exact)).astype(jnp.int32)
    large = jnp.minimum(large, REL_BUCKETS - 1)
    return jnp.where(dist < max_exact, dist, large)


def _swa_mixer(qkv, q_norm_g, k_norm_g, sinks, rel_bias, batch, seq):
    t = qkv.shape[0]
    nb = seq // WINDOW
    q_w = ATTN_Q_HEADS * ATTN_HEAD_DIM
    kv_w2 = 2 * ATTN_KV_HEADS * ATTN_HEAD_DIM
    kj = jnp.arange(2 * WINDOW)[:, None]
    qi = jnp.arange(WINDOW)[None, :]
    bucket = _t5_causal_bucket(jnp.maximum(qi + WINDOW - kj, 0)).astype(jnp.int32)
    q_gain = jnp.broadcast_to((q_norm_g * (ATTN_HEAD_DIM ** -0.5))[:, None], (ATTN_HEAD_DIM, WINDOW))
    const2 = lambda b, i: (0, 0)
    smem = pl.BlockSpec(memory_space=pltpu.SMEM)
    return pl.pallas_call(
        _swa_kernel,
        grid=(batch, nb),
        in_specs=[
            pl.BlockSpec((WINDOW, q_w), lambda b, i: (b * nb + i, 0)),
            pl.BlockSpec((WINDOW, kv_w2), lambda b, i: (b * nb + i, q_w // kv_w2)),
            pl.BlockSpec((WINDOW, kv_w2), lambda b, i: (b * nb + jnp.maximum(i - 1, 0), q_w // kv_w2)),
            pl.BlockSpec((2 * WINDOW, WINDOW), const2),
            pl.BlockSpec((ATTN_HEAD_DIM, WINDOW), const2),
            pl.BlockSpec((1, ATTN_HEAD_DIM), const2),
            smem,
            smem,
        ],
        out_specs=pl.BlockSpec((WINDOW, q_w), lambda b, i: (b * nb + i, 0)),
        out_shape=jax.ShapeDtypeStruct((t, q_w), BF16),
        scratch_shapes=[pltpu.VMEM((2, ATTN_KV_HEADS, 2 * WINDOW, ATTN_Q_PER_KV * WINDOW), F32),
                        pltpu.VMEM((ATTN_KV_HEADS, 1, ATTN_Q_PER_KV * WINDOW), F32)],
        compiler_params=_cparams("arbitrary", "arbitrary"),
        name="swa_mixer",
    )(qkv, qkv, qkv, bucket, q_gain, k_norm_g.reshape(1, -1), rel_bias, sinks)


def _router_kernel(x_ref, g_ref, sh_ref, sc_ref, rwt_ref, rb_ref,
                   h_ref, lp_ref, gate_ref, tile_n_ref, tile_carry_ref, cnt_ref, carry_ref):
    i = pl.program_id(0)
    tm = x_ref.shape[0]

    @pl.when(i == 0)
    def _():
        carry_ref[...] = jnp.zeros_like(carry_ref)

    h = _norm_modulate(x_ref[...], g_ref[...], sh_ref[0], sc_ref[0])
    h_ref[...] = h.astype(h_ref.dtype)

    logits = lax.dot_general(rwt_ref[...], h, (((1,), (1,)), ((), ())),
                             preferred_element_type=F32, precision=HI) + rb_ref[...]
    e_iota = lax.broadcasted_iota(jnp.int32, logits.shape, 0)
    work = logits
    sels, vals = [], []
    for k in range(TOP_K):
        m = jnp.max(work, axis=0, keepdims=True)
        idx = jnp.min(jnp.where(work == m, e_iota, N_EXPERTS), axis=0, keepdims=True)
        sel = e_iota == idx
        work = jnp.where(sel, -jnp.inf, work)
        sels.append(sel)
        vals.append(m)
    exps = [jnp.exp(v - vals[0]) for v in vals]
    denom = exps[0] + exps[1] + exps[2] + exps[3]
    for k in range(TOP_K):
        gate_ref[k:k + 1, :] = exps[k] / denom

    chosen = sels[0] | sels[1] | sels[2] | sels[3]
    t_row = lax.broadcasted_iota(jnp.int32, (tm, tm), 0)
    t_col = lax.broadcasted_iota(jnp.int32, (tm, tm), 1)
    before = (t_row < t_col).astype(BF16)
    prior = jnp.dot(chosen.astype(BF16), before, preferred_element_type=F32)
    n = jnp.sum(chosen.astype(F32), axis=1, keepdims=True)
    e_row = lax.broadcasted_iota(jnp.int32, (N_EXPERTS, N_EXPERTS), 0)
    e_col = lax.broadcasted_iota(jnp.int32, (N_EXPERTS, N_EXPERTS), 1)
    run_start = jnp.dot((e_col < e_row).astype(F32), jnp.broadcast_to(n, (N_EXPERTS, LANES)),
                        preferred_element_type=F32, precision=HI)[:, :1]
    local = run_start + prior
    for k in range(TOP_K):
        lp_ref[k:k + 1, :] = jnp.sum(jnp.where(sels[k], local, 0.0), axis=0, keepdims=True).astype(jnp.int32)
    tile_n_ref[0] = n.astype(jnp.int32)
    tile_carry_ref[0] = carry_ref[...].astype(jnp.int32)
    total = carry_ref[...] + n
    carry_ref[...] = total
    cnt_ref[...] = total.astype(jnp.int32)


def _router(x, g, shift, scale, r_w, r_b, seq, tm):
    t, d = x.shape
    per_b = seq // tm
    ntiles = t // tm
    return pl.pallas_call(
        _router_kernel,
        grid=(t // tm,),
        in_specs=[
            pl.BlockSpec((tm, d), lambda i: (i, 0)),
            pl.BlockSpec((1, d), lambda i: (0, 0)),
            pl.BlockSpec((1, 1, d), lambda i: (i // per_b, 0, 0)),
            pl.BlockSpec((1, 1, d), lambda i: (i // per_b, 0, 0)),
            pl.BlockSpec((N_EXPERTS, d), lambda i: (0, 0)),
            pl.BlockSpec((N_EXPERTS, 1), lambda i: (0, 0)),
        ],
        out_specs=[
            pl.BlockSpec((tm, d), lambda i: (i, 0)),
            pl.BlockSpec((TOP_K, tm), lambda i: (0, i)),
            pl.BlockSpec((TOP_K, tm), lambda i: (0, i)),
            pl.BlockSpec((1, N_EXPERTS, 1), lambda i: (i, 0, 0)),
            pl.BlockSpec((1, N_EXPERTS, 1), lambda i: (i, 0, 0)),
            pl.BlockSpec((N_EXPERTS, 1), lambda i: (0, 0)),
        ],
        out_shape=[
            jax.ShapeDtypeStruct((t, d), BF16),
            jax.ShapeDtypeStruct((TOP_K, t), jnp.int32),
            jax.ShapeDtypeStruct((TOP_K, t), F32),
            jax.ShapeDtypeStruct((ntiles, N_EXPERTS, 1), jnp.int32),
            jax.ShapeDtypeStruct((ntiles, N_EXPERTS, 1), jnp.int32),
            jax.ShapeDtypeStruct((N_EXPERTS, 1), jnp.int32),
        ],
        scratch_shapes=[pltpu.VMEM((N_EXPERTS, 1), F32)],
        compiler_params=_cparams("arbitrary"),
        name="moe_router",
    )(x, g.reshape(1, d), shift, scale, r_w.T, r_b.reshape(-1, 1))


def _run_copies(n, src_ref, src_row, dst_ref, dst_row, sem, max_rows, fn):
    for b in range(max_rows.bit_length()):
        size = 1 << b

        @pl.when((n & size) != 0)
        def _():
            lo = n & (size - 1)
            fn(pltpu.make_async_copy(
                src_ref.at[pl.ds(pl.multiple_of((src_row + lo) * ROW_TILES, ROW_TILES), size * ROW_TILES)],
                dst_ref.at[pl.ds(pl.multiple_of((dst_row + lo) * ROW_TILES, ROW_TILES), size * ROW_TILES)],
                sem))


def _tile_runs(tile, n_ref, carry_ref, gstart_ref, sorted_ref, grouped_ref, sem, tm, fn, to_grouped):
    def body(e, run_start):
        n = n_ref[tile * N_EXPERTS + e]
        slot0 = gstart_ref[e] + carry_ref[tile * N_EXPERTS + e]
        if to_grouped:
            _run_copies(n, sorted_ref, run_start, grouped_ref, slot0, sem, tm, fn)
        else:
            _run_copies(n, grouped_ref, slot0, sorted_ref, run_start, sem, tm, fn)
        return run_start + n

    lax.fori_loop(0, N_EXPERTS, body, 0)


def _block_copy(src_ref, dst_ref, dst_blk, sem):
    rows = MOE_BLOCK * ROW_TILES
    return pltpu.make_async_copy(src_ref, dst_ref.at[pl.ds(pl.multiple_of(dst_blk * rows, rows), rows)], sem)


def _dispatch_kernel(n_ref, carry_ref, gstart_ref, pad_lo_ref, pad_hi_ref, nused_ref, lp_ref, h_ref, xg_ref,
                     sorted_ref, zero_ref, run_sem, sem):
    i = pl.program_id(0)
    last = pl.num_programs(0) - 1
    tm = h_ref.shape[0]
    rows = TOP_K * tm
    slot = i % 2
    nblk = xg_ref.shape[0] // (MOE_BLOCK * ROW_TILES)

    @pl.when(i == 0)
    def _():
        zero_ref[...] = jnp.zeros_like(zero_ref)

        def fill(b, carry):
            _block_copy(zero_ref, xg_ref, b, sem).start()
            return carry

        def fill_wait(b, carry):
            _block_copy(zero_ref, xg_ref, b, sem).wait()
            return carry

        lax.fori_loop(nused_ref[0], nblk, fill, 0)
        lax.fori_loop(nused_ref[0], nblk, fill_wait, 0)

    @pl.when(i < N_EXPERTS)
    def _():
        n_pad = pad_hi_ref[i] - pad_lo_ref[i]
        for fn in (lambda cp: cp.start(), lambda cp: cp.wait()):
            _run_copies(n_pad, zero_ref, 0, xg_ref, pad_lo_ref[i], sem, MOE_BLOCK - 1, fn)

    row_id = lax.broadcasted_iota(jnp.int32, (rows, tm), 0)
    perm = row_id == lp_ref[0:1, :]
    for k in range(1, TOP_K):
        perm = perm | (row_id == lp_ref[k:k + 1, :])
    srt = jnp.dot(jnp.where(perm, 1.0, 0.0).astype(BF16), h_ref[...], preferred_element_type=F32)
    for s in range(ROW_TILES):
        sorted_ref[slot, pl.ds(s, rows, stride=ROW_TILES), :] = srt[:, s * LANES:(s + 1) * LANES]

    def runs(tile, buf, fn):
        _tile_runs(tile, n_ref, carry_ref, gstart_ref, sorted_ref.at[buf], xg_ref, run_sem, tm, fn, True)

    def wait_tile(buf):
        pltpu.make_async_copy(sorted_ref.at[buf], xg_ref.at[pl.ds(0, rows * ROW_TILES)], run_sem).wait()

    @pl.when(i > 0)
    def _():
        wait_tile(1 - slot)

    runs(i, slot, lambda cp: cp.start())

    @pl.when(i == last)
    def _():
        wait_tile(slot)


def _dispatch(tile_n, tile_carry, group_start, pad_lo, pad_hi, nused, lp_t, h, cap, tm):
    t, d = h.shape
    assert t // tm >= N_EXPERTS
    grid_spec = pltpu.PrefetchScalarGridSpec(
        num_scalar_prefetch=6,
        grid=(t // tm,),
        in_specs=[
            pl.BlockSpec((TOP_K, tm), lambda i, *_: (0, i)),
            pl.BlockSpec((tm, d), lambda i, *_: (i, 0)),
        ],
        out_specs=pl.BlockSpec(memory_space=pl.ANY),
        scratch_shapes=[
            pltpu.VMEM((2, TOP_K * tm * ROW_TILES, LANES), F32),
            pltpu.VMEM((MOE_BLOCK * ROW_TILES, LANES), F32),
            pltpu.SemaphoreType.DMA, pltpu.SemaphoreType.DMA,
        ],
    )
    return pl.pallas_call(
        _dispatch_kernel,
        grid_spec=grid_spec,
        out_shape=jax.ShapeDtypeStruct((cap * ROW_TILES, LANES), F32),
        compiler_params=pltpu.CompilerParams(dimension_semantics=("arbitrary",), has_side_effects=True,
                                             vmem_limit_bytes=VMEM_LIMIT),
        name="moe_dispatch",
    )(tile_n, tile_carry, group_start, pad_lo, pad_hi, nused, lp_t, h)


def _weight_copies(w1_hbm, w2_hbm, w1buf, w2buf, sems, layer, e, slot):
    return (pltpu.make_async_copy(w1_hbm.at[layer, e], w1buf.at[slot], sems.at[0, slot]),
            pltpu.make_async_copy(w2_hbm.at[layer, e], w2buf.at[slot], sems.at[1, slot]))


def _expert_kernel(blk_e_ref, nused_ref, next_e_ref, slot_ref, blk_rows_ref, x_ref, w1_hbm, w2_hbm, bg_ref, bl_ref, b2_ref,
                   sel_ref, o_ref, w1buf, w2buf, wg_s, wl_s, w2_s, sems, *, layer):
    blk = pl.program_id(0)
    active = blk < nused_ref[0]
    e = blk_e_ref[blk]
    slot = slot_ref[e]
    new_expert = (blk == 0) | (e != blk_e_ref[jnp.maximum(blk - 1, 0)])
    copies = functools.partial(_weight_copies, w1_hbm, w2_hbm, w1buf, w2buf, sems, layer)

    @pl.when(blk == 0)
    def _():
        for cp in copies(e, slot):
            cp.start()

    @pl.when(active & new_expert)
    def _():
        nxt = next_e_ref[e]

        @pl.when(nxt >= 0)
        def _():
            for cp in copies(nxt, 1 - slot):
                cp.start()

        for cp in copies(e, slot):
            cp.wait()
        half = SPLIT_W // 2
        for j in range(w1buf.shape[2] // SPLIT_W):
            chunk = w1buf[slot, :, j * SPLIT_W:(j + 1) * SPLIT_W].astype(BF16)
            split = jnp.dot(chunk, sel_ref[...], preferred_element_type=F32)
            wg_s[:, j * half:(j + 1) * half] = split[:, :half].astype(BF16)
            wl_s[:, j * half:(j + 1) * half] = split[:, half:].astype(BF16)
        w2_s[...] = w2buf[slot].astype(BF16)

    def mlp(rows):
        x = jnp.concatenate([x_ref[pl.ds(s, rows, stride=ROW_TILES), :]
                             for s in range(ROW_TILES)], axis=1).astype(BF16)
        glu = jnp.dot(x, wg_s[...], preferred_element_type=F32) + bg_ref[0]
        lin = jnp.dot(x, wl_s[...], preferred_element_type=F32) + bl_ref[0]
        glu = jnp.minimum(glu, SWIGLU_LIMIT)
        lin = jnp.clip(lin, -SWIGLU_LIMIT, SWIGLU_LIMIT)
        act = glu * jax.nn.sigmoid(SWIGLU_ALPHA * glu) * (lin + 1.0)
        y = jnp.dot(act.astype(BF16), w2_s[...], preferred_element_type=F32) + b2_ref[0]
        for s in range(ROW_TILES):
            o_ref[pl.ds(s, rows, stride=ROW_TILES), :] = y[:, s * LANES:(s + 1) * LANES]
        if rows < MOE_BLOCK:
            o_ref[pl.ds(rows * ROW_TILES, (MOE_BLOCK - rows) * ROW_TILES), :] = jnp.zeros(
                ((MOE_BLOCK - rows) * ROW_TILES, LANES), o_ref.dtype)

    valid_rows = jnp.where(active, blk_rows_ref[blk], 0)

    @pl.when(valid_rows > MOE_SUB)
    def _():
        mlp(MOE_BLOCK)

    @pl.when((valid_rows > 0) & (valid_rows <= MOE_SUB))
    def _():
        mlp(MOE_SUB)

    @pl.when(valid_rows <= 0)
    def _():
        o_ref[...] = jnp.zeros_like(o_ref)


def _experts(layer, blk_e, nused, next_e, e_slot, blk_rows, xg_rows, w1_all, w2_all, b_glu, b_lin, b2):
    nblk = blk_e.shape[0]
    d = D_MODEL
    ff = w2_all.shape[2]
    col = jnp.arange(SPLIT_W)
    sel = (jnp.arange(SPLIT_W)[None, :] == ((col % 2) * (SPLIT_W // 2) + col // 2)[:, None]).astype(BF16)

    def blk_map(b, blk_e_ref, nused_ref, *_):
        return (jnp.minimum(b, nused_ref[0] - 1), 0)

    def e_map(b, blk_e_ref, nused_ref, *_):
        return (blk_e_ref[jnp.minimum(b, nused_ref[0] - 1)], 0, 0)

    grid_spec = pltpu.PrefetchScalarGridSpec(
        num_scalar_prefetch=5,
        grid=(nblk,),
        in_specs=[
            pl.BlockSpec((MOE_BLOCK * ROW_TILES, LANES), blk_map),
            pl.BlockSpec(memory_space=pl.ANY),
            pl.BlockSpec(memory_space=pl.ANY),
            pl.BlockSpec((1, 1, ff), e_map),
            pl.BlockSpec((1, 1, ff), e_map),
            pl.BlockSpec((1, 1, d), e_map),
            pl.BlockSpec((SPLIT_W, SPLIT_W), lambda b, *_: (0, 0)),
        ],
        out_specs=pl.BlockSpec((MOE_BLOCK * ROW_TILES, LANES), lambda b, *_: (b, 0)),
        scratch_shapes=[
            pltpu.VMEM((2, d, 2 * ff), F32), pltpu.VMEM((2, ff, d), F32),
            pltpu.VMEM((d, ff), BF16), pltpu.VMEM((d, ff), BF16), pltpu.VMEM((ff, d), BF16),
            pltpu.SemaphoreType.DMA((2, 2)),
        ],
    )
    return pl.pallas_call(
        functools.partial(_expert_kernel, layer=layer),
        grid_spec=grid_spec,
        out_shape=jax.ShapeDtypeStruct(xg_rows.shape, F32),
        compiler_params=_cparams("arbitrary"),
        name="moe_experts",
    )(blk_e, nused, next_e, e_slot, blk_rows, xg_rows, w1_all, w2_all, b_glu, b_lin, b2, sel)


def _combine_kernel(n_ref, carry_ref, gstart_ref, yg_ref, lp_ref, gate_ref, x_ref, g2_ref, o_ref, sorted_ref, sem):
    i = pl.program_id(0)
    ntiles = pl.num_programs(0)
    tm = x_ref.shape[0]
    rows = TOP_K * tm
    slot = i % 2

    def runs(tile, buf, fn):
        _tile_runs(tile, n_ref, carry_ref, gstart_ref, sorted_ref.at[buf], yg_ref, sem.at[buf], tm, fn, False)

    @pl.when(i == 0)
    def _():
        runs(0, 0, lambda cp: cp.start())

    @pl.when(i + 1 < ntiles)
    def _():
        runs(i + 1, 1 - slot, lambda cp: cp.start())

    eye = (lax.broadcasted_iota(jnp.int32, (tm, tm), 0) ==
           lax.broadcasted_iota(jnp.int32, (tm, tm), 1)).astype(F32)
    rows_t = jnp.concatenate([gate_ref[...], lp_ref[...].astype(F32)], axis=0)
    cols = lax.dot_general(eye, rows_t, (((1,), (1,)), ((), ())), preferred_element_type=F32, precision=HI)
    col_id = lax.broadcasted_iota(jnp.int32, (tm, rows), 1)
    weights = jnp.zeros((tm, rows), F32)
    for k in range(TOP_K):
        lp_col = (cols[:, TOP_K + k:TOP_K + k + 1] + 0.5).astype(jnp.int32)
        weights = weights + jnp.where(col_id == lp_col, cols[:, k:k + 1], 0.0)

    pltpu.make_async_copy(yg_ref.at[pl.ds(0, rows * ROW_TILES)], sorted_ref.at[slot], sem.at[slot]).wait()
    y = jnp.concatenate(
        [sorted_ref[slot, pl.ds(s, rows, stride=ROW_TILES), :] for s in range(ROW_TILES)], axis=1).astype(BF16)
    acc = jnp.dot(weights.astype(BF16), y, preferred_element_type=F32)
    o_ref[...] = x_ref[...] + g2_ref[0] * acc


def _combine(tile_n, tile_carry, group_start, yg_rows, lp_t, gate_t, x, gate2, seq, tm):
    t, d = x.shape
    per_b = seq // tm
    grid_spec = pltpu.PrefetchScalarGridSpec(
        num_scalar_prefetch=3,
        grid=(t // tm,),
        in_specs=[
            pl.BlockSpec(memory_space=pl.ANY),
            pl.BlockSpec((TOP_K, tm), lambda i, *_: (0, i)),
            pl.BlockSpec((TOP_K, tm), lambda i, *_: (0, i)),
            pl.BlockSpec((tm, d), lambda i, *_: (i, 0)),
            pl.BlockSpec((1, 1, d), lambda i, *_: (i // per_b, 0, 0)),
        ],
        out_specs=pl.BlockSpec((tm, d), lambda i, *_: (i, 0)),
        scratch_shapes=[pltpu.VMEM((2, TOP_K * tm * ROW_TILES, LANES), F32), pltpu.SemaphoreType.DMA((2,))],
    )
    return pl.pallas_call(
        _combine_kernel,
        grid_spec=grid_spec,
        out_shape=jax.ShapeDtypeStruct((t, d), F32),
        compiler_params=_cparams("arbitrary"),
        name="moe_combine",
    )(tile_n, tile_carry, group_start, yg_rows, lp_t, gate_t, x, gate2)


def _moe(layer, x, g, shift, scale, gate2, r_w, r_b, w1_all, b1_glu, b1_lin, w2_all, b2, seq):
    t = x.shape[0]
    h, lp_t, gate_t, tile_n, tile_carry, counts = _router(x, g, shift, scale, r_w, r_b, seq, MOE_TILE)
    tile_n = tile_n.reshape(-1)
    tile_carry = tile_carry.reshape(-1)
    counts = counts.reshape(-1)
    padded = ((counts + MOE_BLOCK - 1) // MOE_BLOCK) * MOE_BLOCK
    group_end = jnp.cumsum(padded)
    group_start = (group_end - padded).astype(jnp.int32)
    cap = t * TOP_K + N_EXPERTS * MOE_BLOCK
    nblk = cap // MOE_BLOCK
    blk_start = jnp.arange(nblk, dtype=jnp.int32) * MOE_BLOCK
    e_ids = jnp.arange(N_EXPERTS, dtype=jnp.int32)
    past = (blk_start[None, :] >= group_end[:, None]).astype(jnp.int32)
    blk_e = jnp.minimum(jnp.sum(past, axis=0), N_EXPERTS - 1).astype(jnp.int32)
    nused = (group_end[-1:] // MOE_BLOCK).astype(jnp.int32)

    pad_lo = (group_start + counts).astype(jnp.int32)
    pad_hi = group_end.astype(jnp.int32)
    blk_pad_lo = jnp.sum(jnp.where(blk_e[None, :] == e_ids[:, None], pad_lo[:, None], 0), axis=0)
    blk_rows = jnp.clip(blk_pad_lo - blk_start, 0, MOE_BLOCK).astype(jnp.int32)
    xg_rows = _dispatch(tile_n, tile_carry, group_start, pad_lo, pad_hi, nused, lp_t, h, cap, MOE_TILE)
    nonempty = counts > 0
    later = jnp.where(nonempty[None, :] & (e_ids[None, :] > e_ids[:, None]), e_ids[None, :], N_EXPERTS)
    next_e = jnp.min(later, axis=1)
    next_e = jnp.where(next_e == N_EXPERTS, -1, next_e).astype(jnp.int32)
    e_slot = ((jnp.cumsum(nonempty.astype(jnp.int32)) - 1) % 2).astype(jnp.int32)
    yg_rows = _experts(layer, blk_e, nused, next_e, e_slot, blk_rows, xg_rows, w1_all, w2_all,
                       b1_glu[:, None, :], b1_lin[:, None, :], b2[:, None, :])
    return _combine(tile_n, tile_carry, group_start, yg_rows, lp_t, gate_t, x, gate2, seq, MOE_TILE)


def kernel(x, c, ada_w, ada_b, norm1_g, norm2_g, m_in_w, m_conv_w, m_conv_b, m_dt_bias, m_A_log, m_D, m_norm_g, m_out_w, a_qkv_w, a_q_norm_g, a_k_norm_g, a_sinks, a_out_w, rel_bias, r_w, r_b, e_w1, e_b1, e_w2, e_b2):
    batch, seq, d = x.shape
    depth = ada_w.shape[0]
    t = batch * seq
    xf = x.reshape(t, d)

    c_pad = jnp.zeros((SUBLANES, d), F32).at[:batch].set(c)
    mod = _adaln(c_pad, ada_w, ada_b)[:, :batch]

    b1_split = jnp.moveaxis(e_b1.reshape(depth, N_EXPERTS, -1, 2), -1, 1)

    for i in range(depth):
        parts =[mod[i, :, p * d:(p + 1) * d].reshape(batch, 1, d) for p in range(6)]
        sh1, sc1, g1, sh2, sc2, g2 = parts
        j = i // 2
        if i % 2 == 0:
            w_zxbc = m_in_w[j, :, :SSM_ZXBC].astype(BF16)
            w_dt = jnp.pad(m_in_w[j, :, SSM_ZXBC:], ((0, 0), (0, LANES - SSM_HEADS))).astype(BF16)
            zxbc, dt_raw = _norm_matmul(xf, norm1_g[i], sh1, sc1, (w_zxbc, w_dt), seq)
            y = _ssd_mixer(zxbc, dt_raw, m_conv_w[j], m_conv_b[j], m_dt_bias[j], m_A_log[j], m_D[j],
                           m_norm_g[j], batch, seq)
            xf = _matmul_residual(y, m_out_w[j].astype(BF16), xf, g1, seq)
        else:
            qkv, = _norm_matmul(xf, norm1_g[i], sh1, sc1, (a_qkv_w[j].astype(BF16),), seq)
            y = _swa_mixer(qkv, a_q_norm_g[j], a_k_norm_g[j], a_sinks[j], rel_bias, batch, seq)
            xf = _matmul_residual(y, a_out_w[j].astype(BF16), xf, g1, seq)
        xf = _moe(i, xf, norm2_g[i], sh2, sc2, g2, r_w[i], r_b[i], e_w1, b1_split[i, 0], b1_split[i, 1],
                  e_w2, e_b2[i], seq)
    return xf.reshape(batch, seq, d)
```

```python
import functools
import math

import jax
import jax.numpy as jnp
from jax import lax
from jax.experimental import pallas as pl
from jax.experimental.pallas import tpu as pltpu

D_MODEL = 1024
EPS = 1e-6
LANES = 128
SUBLANES = 8
ROW_TILES = D_MODEL // LANES

SSM_D_INNER = 2048
SSM_HEAD_DIM = 64
SSM_HEADS = 32
SSM_GROUPS = 4
SSM_STATE = 128
SSM_CONV = 4
SSM_CHUNK = 128
SSM_GN = SSM_GROUPS * SSM_STATE
SSM_ZXBC = 2 * SSM_D_INNER + 2 * SSM_GN
SSM_GROUP_W = SSM_D_INNER // SSM_GROUPS

ATTN_HEAD_DIM = 64
ATTN_Q_HEADS = 16
ATTN_KV_HEADS = 4
ATTN_Q_PER_KV = 4
WINDOW = 128
REL_BUCKETS = 32
REL_MAX_DIST = 128

N_EXPERTS = 32
TOP_K = 4
SWIGLU_ALPHA = 1.702
SWIGLU_LIMIT = 7.0
MOE_BLOCK = 512
MOE_SUB = 256
MOE_TILE = 256
SPLIT_W = 256

VMEM_LIMIT = 56 * 1024 * 1024
HI = lax.Precision.HIGHEST
F32 = jnp.float32
BF16 = jnp.bfloat16


def _cparams(*sem):
    return pltpu.CompilerParams(dimension_semantics=sem, vmem_limit_bytes=VMEM_LIMIT)


def _norm_modulate(x, g, shift, scale):
    ms = jnp.mean(x * x, axis=-1, keepdims=True)
    return x * lax.rsqrt(ms + EPS) * g * (1.0 + scale) + shift


def _adaln_kernel(c_ref, w_ref, b_ref, o_ref):
    c = c_ref[...]
    c_act = c * jax.nn.sigmoid(c)
    o_ref[0] = jnp.dot(c_act, w_ref[0], preferred_element_type=F32, precision=HI) + b_ref[0]


def _adaln(c_pad, ada_w, ada_b):
    depth, d, n = ada_w.shape
    tn = 1536
    return pl.pallas_call(
        _adaln_kernel,
        grid=(depth, n // tn),
        in_specs=[
            pl.BlockSpec((SUBLANES, d), lambda i, j: (0, 0)),
            pl.BlockSpec((1, d, tn), lambda i, j: (i, 0, j)),
            pl.BlockSpec((1, 1, tn), lambda i, j: (i, 0, j)),
        ],
        out_specs=pl.BlockSpec((1, SUBLANES, tn), lambda i, j: (i, 0, j)),
        out_shape=jax.ShapeDtypeStruct((depth, SUBLANES, n), F32),
        compiler_params=_cparams("arbitrary", "arbitrary"),
        name="adaln",
    )(c_pad, ada_w, ada_b.reshape(depth, 1, n))


def _norm_matmul_kernel(x_ref, g_ref, sh_ref, sc_ref, *refs):
    w_refs, o_refs = refs[:len(refs) // 2], refs[len(refs) // 2:]
    h = _norm_modulate(x_ref[...], g_ref[...], sh_ref[0], sc_ref[0]).astype(BF16)
    for w_ref, o_ref in zip(w_refs, o_refs):
        o_ref[...] = jnp.dot(h, w_ref[...], preferred_element_type=F32)


def _norm_matmul(x, g, shift, scale, weights_bf16, seq, tm=512):
    t, d = x.shape
    per_b = seq // tm
    return pl.pallas_call(
        _norm_matmul_kernel,
        grid=(t // tm,),
        in_specs=[
            pl.BlockSpec((tm, d), lambda i: (i, 0)),
            pl.BlockSpec((1, d), lambda i: (0, 0)),
            pl.BlockSpec((1, 1, d), lambda i: (i // per_b, 0, 0)),
            pl.BlockSpec((1, 1, d), lambda i: (i // per_b, 0, 0)),
        ] + [pl.BlockSpec(w.shape, lambda i: (0, 0), pipeline_mode=pl.Buffered(1))
             for w in weights_bf16],
        out_specs=[pl.BlockSpec((tm, w.shape[1]), lambda i: (i, 0)) for w in weights_bf16],
        out_shape=[jax.ShapeDtypeStruct((t, w.shape[1]), F32) for w in weights_bf16],
        compiler_params=_cparams("arbitrary"),
        name="norm_matmul",
    )(x, g.reshape(1, d), shift, scale, *weights_bf16)


def _matmul_residual_kernel(y_ref, w_ref, x_ref, gate_ref, o_ref):
    acc = jnp.dot(y_ref[...], w_ref[...], preferred_element_type=F32)
    o_ref[...] = x_ref[...] + gate_ref[0] * acc


def _matmul_residual(y_bf16, w_bf16, x, gate, seq, tm=512):
    t, k = y_bf16.shape
    d = x.shape[1]
    per_b = seq // tm
    return pl.pallas_call(
        _matmul_residual_kernel,
        grid=(t // tm,),
        in_specs=[
            pl.BlockSpec((tm, k), lambda i: (i, 0)),
            pl.BlockSpec((k, d), lambda i: (0, 0)),
            pl.BlockSpec((tm, d), lambda i: (i, 0)),
            pl.BlockSpec((1, 1, d), lambda i: (i // per_b, 0, 0)),
        ],
        out_specs=pl.BlockSpec((tm, d), lambda i: (i, 0)),
        out_shape=jax.ShapeDtypeStruct((t, d), F32),
        compiler_params=_cparams("arbitrary"),
        name="matmul_residual",
    )(y_bf16, w_bf16, x, gate)


def _causal_conv_silu(cur, prev_tail, w, b):
    rows = lax.broadcasted_iota(jnp.int32, (SUBLANES, cur.shape[1]), 0)
    acc = b + w[SSM_CONV - 1:SSM_CONV] * cur
    for d in range(1, SSM_CONV):
        rolled = pltpu.roll(cur, d, axis=0)
        top = jnp.where(rows < d, pltpu.roll(prev_tail, d, axis=0), rolled[0:SUBLANES])
        shifted = jnp.concatenate([top, rolled[SUBLANES:]], axis=0)
        acc = acc + w[SSM_CONV - 1 - d:SSM_CONV - d] * shifted
    return acc * jax.nn.sigmoid(acc)


def _split3(x):
    hi = x.astype(BF16)
    rem = x - hi.astype(F32)
    mid = rem.astype(BF16)
    lo = (rem - mid.astype(F32)).astype(BF16)
    return jnp.concatenate([hi, mid, lo], axis=1)


def _ssd_kernel(z_ref, xs_ref, bc_ref, dt_ref, cwx_ref, cbx_ref, cwb_ref, cbb_ref, dtb_ref, alog_ref,
                dskip_ref, ng_ref, hexp_ref, lexp_ref, o_ref, tailx_ref, tailb_ref, state_ref):
    c = pl.program_id(1)

    @pl.when(c == 0)
    def _():
        tailx_ref[...] = jnp.zeros_like(tailx_ref)
        tailb_ref[...] = jnp.zeros_like(tailb_ref)
        state_ref[...] = jnp.zeros_like(state_ref)

    xs_raw = xs_ref[...]
    bc_raw = bc_ref[...]
    xs = _causal_conv_silu(xs_raw, tailx_ref[...], cwx_ref[...], cbx_ref[...])
    bc = _causal_conv_silu(bc_raw, tailb_ref[...], cwb_ref[...], cbb_ref[...])
    tailx_ref[...] = xs_raw[SSM_CHUNK - SUBLANES:]
    tailb_ref[...] = bc_raw[SSM_CHUNK - SUBLANES:]

    dt_in = dt_ref[...][:, :SSM_HEADS] + dtb_ref[...]
    dt = jnp.maximum(dt_in, 0.0) + jnp.log1p(jnp.exp(-jnp.abs(dt_in)))
    a_neg = -jnp.exp(alog_ref[...])
    d_a = dt * a_neg
    li = lax.broadcasted_iota(jnp.int32, (SSM_CHUNK, SSM_CHUNK), 0)
    si = lax.broadcasted_iota(jnp.int32, (SSM_CHUNK, SSM_CHUNK), 1)
    causal = li >= si
    tri = causal.astype(F32)
    a_cs = jnp.dot(tri, d_a, preferred_element_type=F32, precision=HI)
    a_cs_t = lax.dot_general(d_a, tri, (((0,), (1,)), ((), ())),
                             preferred_element_type=F32, precision=HI)
    a_last = a_cs[SSM_CHUNK - 1:SSM_CHUNK]
    e_out = jnp.exp(a_cs)
    e_state = jnp.exp(a_last - a_cs) * dt
    small = jnp.concatenate([dt, e_out, e_state], axis=0)
    wide = jnp.dot(_split3(small), hexp_ref[...], preferred_element_type=F32)
    dt_w = wide[0:SSM_CHUNK]
    e_out_w = wide[SSM_CHUNK:2 * SSM_CHUNK]
    e_state_w = wide[2 * SSM_CHUNK:]
    a_col = jnp.dot(_split3(a_cs), lexp_ref[...], preferred_element_type=F32)

    x_dt = (xs * dt_w).astype(BF16)
    x_state = (xs * e_state_w).astype(BF16)
    chunk_decay_w = e_out_w[SSM_CHUNK - 1:SSM_CHUNK]

    heads_per_group = SSM_HEADS // SSM_GROUPS
    y_parts = []
    for g in range(SSM_GROUPS):
        b_g = bc[:, g * SSM_STATE:(g + 1) * SSM_STATE].astype(BF16)
        c_g = bc[:, SSM_GN + g * SSM_STATE:SSM_GN + (g + 1) * SSM_STATE].astype(BF16)
        cb = lax.dot_general(c_g, b_g, (((1,), (1,)), ((), ())), preferred_element_type=F32)
        gsl = slice(g * SSM_GROUP_W, (g + 1) * SSM_GROUP_W)
        h_prev = state_ref[g]
        y_off = jnp.dot(c_g, h_prev.astype(BF16), preferred_element_type=F32) * e_out_w[:, gsl]
        diag = []
        for r in range(heads_per_group):
            h = g * heads_per_group + r
            seg = a_col[:, h * SSM_CHUNK:(h + 1) * SSM_CHUNK] - a_cs_t[h:h + 1, :]
            decay = jnp.exp(jnp.where(causal, seg, -jnp.inf))
            m = (cb * decay).astype(BF16)
            diag.append(jnp.dot(m, x_dt[:, h * SSM_HEAD_DIM:(h + 1) * SSM_HEAD_DIM],
                                preferred_element_type=F32))
        y_parts.append(jnp.concatenate(diag, axis=1) + y_off)
        upd = lax.dot_general(b_g, x_state[:, gsl], (((0,), (0,)), ((), ())), preferred_element_type=F32)
        state_ref[g] = h_prev * chunk_decay_w[:, gsl] + upd

    y = jnp.concatenate(y_parts, axis=1) + dskip_ref[...] * xs
    z = z_ref[...]
    y = y * (z * jax.nn.sigmoid(z))
    normed = []
    for g in range(SSM_GROUPS):
        y_g = y[:, g * SSM_GROUP_W:(g + 1) * SSM_GROUP_W]
        normed.append(y_g * lax.rsqrt(jnp.mean(y_g * y_g, axis=-1, keepdims=True) + EPS))
    o_ref[...] = (jnp.concatenate(normed, axis=1) * ng_ref[...]).astype(o_ref.dtype)


def _ssd_mixer(zxbc, dt_raw, conv_w, conv_b, dt_bias, a_log, d_skip, norm_g, batch, seq):
    t = zxbc.shape[0]
    nc = seq // SSM_CHUNK
    head_expand = jnp.tile(jnp.repeat(jnp.eye(SSM_HEADS, dtype=BF16), SSM_HEAD_DIM, axis=1), (3, 1))
    lane_expand = jnp.tile(jnp.repeat(jnp.eye(SSM_HEADS, dtype=BF16), SSM_CHUNK, axis=1), (3, 1))
    row = lambda b, c: (b * nc + c, 0)
    const2 = lambda b, c: (0, 0)
    bc_w = 2 * SSM_GN
    return pl.pallas_call(
        _ssd_kernel,
        grid=(batch, nc),
        in_specs=[
            pl.BlockSpec((SSM_CHUNK, SSM_D_INNER), row),
            pl.BlockSpec((SSM_CHUNK, SSM_D_INNER), lambda b, c: (b * nc + c, 1)),
            pl.BlockSpec((SSM_CHUNK, bc_w), lambda b, c: (b * nc + c, 2 * SSM_D_INNER // bc_w)),
            pl.BlockSpec((SSM_CHUNK, LANES), row),
            pl.BlockSpec((SSM_CONV, SSM_D_INNER), const2),
            pl.BlockSpec((1, SSM_D_INNER), const2),
            pl.BlockSpec((SSM_CONV, bc_w), const2),
            pl.BlockSpec((1, bc_w), const2),
            pl.BlockSpec((1, SSM_HEADS), const2),
            pl.BlockSpec((1, SSM_HEADS), const2),
            pl.BlockSpec((1, SSM_D_INNER), const2),
            pl.BlockSpec((1, SSM_D_INNER), const2),
            pl.BlockSpec((3 * SSM_HEADS, SSM_D_INNER), const2),
            pl.BlockSpec((3 * SSM_HEADS, SSM_HEADS * SSM_CHUNK), const2),
        ],
        out_specs=pl.BlockSpec((SSM_CHUNK, SSM_D_INNER), row),
        out_shape=jax.ShapeDtypeStruct((t, SSM_D_INNER), BF16),
        scratch_shapes=[
            pltpu.VMEM((SUBLANES, SSM_D_INNER), F32),
            pltpu.VMEM((SUBLANES, bc_w), F32),
            pltpu.VMEM((SSM_GROUPS, SSM_STATE, SSM_GROUP_W), F32),
        ],
        compiler_params=_cparams("arbitrary", "arbitrary"),
        name="ssd_mixer",
    )(zxbc, zxbc, zxbc, dt_raw,
      conv_w[:, :SSM_D_INNER], conv_b[:SSM_D_INNER].reshape(1, -1),
      conv_w[:, SSM_D_INNER:], conv_b[SSM_D_INNER:].reshape(1, -1),
      dt_bias.reshape(1, -1), a_log.reshape(1, -1),
      jnp.repeat(d_skip, SSM_HEAD_DIM).reshape(1, -1), norm_g.reshape(1, -1),
      head_expand, lane_expand)


def _head_rms(x, g):
    return x * lax.rsqrt(jnp.mean(x * x, axis=-1, keepdims=True) + EPS) * g


def _swa_kernel(q_ref, kvc_ref, kvp_ref, bucket_ref, qg_ref, kg_ref, rel_ref, sink_ref, o_ref,
                bias_ref, sinkrow_ref):
    b = pl.program_id(0)
    i = pl.program_id(1)

    @pl.when((b == 0) & (i == 0))
    def _():
        bucket = bucket_ref[...]
        kj = lax.broadcasted_iota(jnp.int32, (2 * WINDOW, WINDOW), 0)
        qi = lax.broadcasted_iota(jnp.int32, (2 * WINDOW, WINDOW), 1)
        dist = qi + WINDOW - kj
        band = (dist >= 0) & (dist < WINDOW)
        for h in range(ATTN_Q_HEADS):
            g, r = divmod(h, ATTN_Q_PER_KV)
            acc = jnp.zeros(bucket.shape, F32)
            for k in range(REL_BUCKETS):
                acc = jnp.where(bucket == k, rel_ref[k, h], acc)
            cols = slice(r * WINDOW, (r + 1) * WINDOW)
            bias_ref[0, g, :, cols] = jnp.where(band, acc, -jnp.inf)
            bias_ref[1, g, :, cols] = jnp.where(band & (kj >= WINDOW), acc, -jnp.inf)
            sinkrow_ref[g, :, cols] = jnp.full((1, WINDOW), sink_ref[h], F32)

    first = (i == 0).astype(jnp.int32)
    q_t = q_ref[...].T
    kv_c = kvc_ref[...]
    kv_p = kvp_ref[...]
    kv_w = ATTN_KV_HEADS * ATTN_HEAD_DIM
    q_gain = qg_ref[...]
    outs = []
    for g in range(ATTN_KV_HEADS):
        ksl = slice(g * ATTN_HEAD_DIM, (g + 1) * ATTN_HEAD_DIM)
        vsl = slice(kv_w + g * ATTN_HEAD_DIM, kv_w + (g + 1) * ATTN_HEAD_DIM)
        k = jnp.concatenate([kv_p[:, ksl], kv_c[:, ksl]], axis=0)
        v = jnp.concatenate([kv_p[:, vsl], kv_c[:, vsl]], axis=0).astype(BF16)
        k = _head_rms(k, kg_ref[...]).astype(BF16)
        q_heads = []
        for r in range(ATTN_Q_PER_KV):
            h = g * ATTN_Q_PER_KV + r
            q_h = q_t[h * ATTN_HEAD_DIM:(h + 1) * ATTN_HEAD_DIM]
            inv = lax.rsqrt(jnp.mean(q_h * q_h, axis=0, keepdims=True) + EPS)
            q_heads.append((q_h * inv * q_gain).astype(BF16))
        q = jnp.concatenate(q_heads, axis=1)
        s = jnp.dot(k, q, preferred_element_type=F32) + bias_ref[first, g]
        sink = sinkrow_ref[g]
        m = jnp.maximum(jnp.max(s, axis=0, keepdims=True), sink)
        p = jnp.exp(s - m)
        denom = jnp.sum(p, axis=0, keepdims=True) + jnp.exp(sink - m)
        pv = lax.dot_general(v, p.astype(BF16), (((0,), (0,)), ((), ())), preferred_element_type=F32)
        pv = pv * (1.0 / denom)
        outs.extend(pv[:, r * WINDOW:(r + 1) * WINDOW] for r in range(ATTN_Q_PER_KV))
    o_ref[...] = jnp.concatenate(outs, axis=0).T.astype(o_ref.dtype)


def _t5_causal_bucket(dist):
    max_exact = REL_BUCKETS // 2
    d = jnp.maximum(dist, 1).astype(F32)
    large = max_exact + (jnp.log(d / max_exact) / math.log(REL_MAX_DIST / max_exact)
                         * (REL_BUCKETS - max_exact)).astype(jnp.int32)
    large = jnp.minimum(large, REL_BUCKETS - 1)
    return jnp.where(dist < max_exact, dist, large)


def _swa_mixer(qkv, q_norm_g, k_norm_g, sinks, rel_bias, batch, seq):
    t = qkv.shape[0]
    nb = seq // WINDOW
    q_w = ATTN_Q_HEADS * ATTN_HEAD_DIM
    kv_w2 = 2 * ATTN_KV_HEADS * ATTN_HEAD_DIM
    kj = jnp.arange(2 * WINDOW)[:, None]
    qi = jnp.arange(WINDOW)[None, :]
    bucket = _t5_causal_bucket(jnp.maximum(qi + WINDOW - kj, 0)).astype(jnp.int32)
    q_gain = jnp.broadcast_to((q_norm_g * (ATTN_HEAD_DIM ** -0.5))[:, None], (ATTN_HEAD_DIM, WINDOW))
    const2 = lambda b, i: (0, 0)
    smem = pl.BlockSpec(memory_space=pltpu.SMEM)
    return pl.pallas_call(
        _swa_kernel,
        grid=(batch, nb),
        in_specs=[
            pl.BlockSpec((WINDOW, q_w), lambda b, i: (b * nb + i, 0)),
            pl.BlockSpec((WINDOW, kv_w2), lambda b, i: (b * nb + i, q_w // kv_w2)),
            pl.BlockSpec((WINDOW, kv_w2), lambda b, i: (b * nb + jnp.maximum(i - 1, 0), q_w // kv_w2)),
            pl.BlockSpec((2 * WINDOW, WINDOW), const2),
            pl.BlockSpec((ATTN_HEAD_DIM, WINDOW), const2),
            pl.BlockSpec((1, ATTN_HEAD_DIM), const2),
            smem,
            smem,
        ],
        out_specs=pl.BlockSpec((WINDOW, q_w), lambda b, i: (b * nb + i, 0)),
        out_shape=jax.ShapeDtypeStruct((t, q_w), BF16),
        scratch_shapes=[pltpu.VMEM((2, ATTN_KV_HEADS, 2 * WINDOW, ATTN_Q_PER_KV * WINDOW), F32),
                        pltpu.VMEM((ATTN_KV_HEADS, 1, ATTN_Q_PER_KV * WINDOW), F32)],
        compiler_params=_cparams("arbitrary", "arbitrary"),
        name="swa_mixer",
    )(qkv, qkv, qkv, bucket, q_gain, k_norm_g.reshape(1, -1), rel_bias, sinks)


def _router_kernel(x_ref, g_ref, sh_ref, sc_ref, rwt_ref, rb_ref,
                   h_ref, lp_ref, gate_ref, tile_n_ref, tile_carry_ref, cnt_ref, carry_ref):
    i = pl.program_id(0)
    tm = x_ref.shape[0]

    @pl.when(i == 0)
    def _():
        carry_ref[...] = jnp.zeros_like(carry_ref)

    h = _norm_modulate(x_ref[...], g_ref[...], sh_ref[0], sc_ref[0])
    h_ref[...] = h.astype(h_ref.dtype)

    logits = lax.dot_general(rwt_ref[...], h, (((1,), (1,)), ((), ())),
                             preferred_element_type=F32, precision=HI) + rb_ref[...]
    e_iota = lax.broadcasted_iota(jnp.int32, logits.shape, 0)
    work = logits
    sels, vals = [], []
    for k in range(TOP_K):
        m = jnp.max(work, axis=0, keepdims=True)
        idx = jnp.min(jnp.where(work == m, e_iota, N_EXPERTS), axis=0, keepdims=True)
        sel = e_iota == idx
        work = jnp.where(sel, -jnp.inf, work)
        sels.append(sel)
        vals.append(m)
    exps = [jnp.exp(v - vals[0]) for v in vals]
    denom = exps[0] + exps[1] + exps[2] + exps[3]
    for k in range(TOP_K):
        gate_ref[k:k + 1, :] = exps[k] / denom

    chosen = sels[0] | sels[1] | sels[2] | sels[3]
    t_row = lax.broadcasted_iota(jnp.int32, (tm, tm), 0)
    t_col = lax.broadcasted_iota(jnp.int32, (tm, tm), 1)
    before = (t_row < t_col).astype(BF16)
    prior = jnp.dot(chosen.astype(BF16), before, preferred_element_type=F32)
    n = jnp.sum(chosen.astype(F32), axis=1, keepdims=True)
    e_row = lax.broadcasted_iota(jnp.int32, (N_EXPERTS, N_EXPERTS), 0)
    e_col = lax.broadcasted_iota(jnp.int32, (N_EXPERTS, N_EXPERTS), 1)
    run_start = jnp.dot((e_col < e_row).astype(F32), jnp.broadcast_to(n, (N_EXPERTS, LANES)),
                        preferred_element_type=F32, precision=HI)[:, :1]
    local = run_start + prior
    for k in range(TOP_K):
        lp_ref[k:k + 1, :] = jnp.sum(jnp.where(sels[k], local, 0.0), axis=0, keepdims=True).astype(jnp.int32)
    tile_n_ref[0] = n.astype(jnp.int32)
    tile_carry_ref[0] = carry_ref[...].astype(jnp.int32)
    total = carry_ref[...] + n
    carry_ref[...] = total
    cnt_ref[...] = total.astype(jnp.int32)


def _router(x, g, shift, scale, r_w, r_b, seq, tm):
    t, d = x.shape
    per_b = seq // tm
    ntiles = t // tm
    return pl.pallas_call(
        _router_kernel,
        grid=(t // tm,),
        in_specs=[
            pl.BlockSpec((tm, d), lambda i: (i, 0)),
            pl.BlockSpec((1, d), lambda i: (0, 0)),
            pl.BlockSpec((1, 1, d), lambda i: (i // per_b, 0, 0)),
            pl.BlockSpec((1, 1, d), lambda i: (i // per_b, 0, 0)),
            pl.BlockSpec((N_EXPERTS, d), lambda i: (0, 0)),
            pl.BlockSpec((N_EXPERTS, 1), lambda i: (0, 0)),
        ],
        out_specs=[
            pl.BlockSpec((tm, d), lambda i: (i, 0)),
            pl.BlockSpec((TOP_K, tm), lambda i: (0, i)),
            pl.BlockSpec((TOP_K, tm), lambda i: (0, i)),
            pl.BlockSpec((1, N_EXPERTS, 1), lambda i: (i, 0, 0)),
            pl.BlockSpec((1, N_EXPERTS, 1), lambda i: (i, 0, 0)),
            pl.BlockSpec((N_EXPERTS, 1), lambda i: (0, 0)),
        ],
        out_shape=[
            jax.ShapeDtypeStruct((t, d), BF16),
            jax.ShapeDtypeStruct((TOP_K, t), jnp.int32),
            jax.ShapeDtypeStruct((TOP_K, t), F32),
            jax.ShapeDtypeStruct((ntiles, N_EXPERTS, 1), jnp.int32),
            jax.ShapeDtypeStruct((ntiles, N_EXPERTS, 1), jnp.int32),
            jax.ShapeDtypeStruct((N_EXPERTS, 1), jnp.int32),
        ],
        scratch_shapes=[pltpu.VMEM((N_EXPERTS, 1), F32)],
        compiler_params=_cparams("arbitrary"),
        name="moe_router",
    )(x, g.reshape(1, d), shift, scale, r_w.T, r_b.reshape(-1, 1))


def _run_copies(n, src_ref, src_row, dst_ref, dst_row, sem, max_rows, fn):
    for b in range(max_rows.bit_length()):
        size = 1 << b

        @pl.when((n & size) != 0)
        def _():
            lo = n & (size - 1)
            fn(pltpu.make_async_copy(
                src_ref.at[pl.ds(pl.multiple_of((src_row + lo) * ROW_TILES, ROW_TILES), size * ROW_TILES)],
                dst_ref.at[pl.ds(pl.multiple_of((dst_row + lo) * ROW_TILES, ROW_TILES), size * ROW_TILES)],
                sem))


def _tile_runs(tile, n_ref, carry_ref, gstart_ref, sorted_ref, grouped_ref, sem, tm, fn, to_grouped, live=None):
    def body(e, run_start):
        n = n_ref[tile * N_EXPERTS + e]
        if live is not None:
            n = jnp.where(live, n, 0)
        slot0 = gstart_ref[e] + carry_ref[tile * N_EXPERTS + e]
        if to_grouped:
            _run_copies(n, sorted_ref, run_start, grouped_ref, slot0, sem, tm, fn)
        else:
            _run_copies(n, grouped_ref, slot0, sorted_ref, run_start, sem, tm, fn)
        return run_start + n

    if live is None:
        lax.fori_loop(0, N_EXPERTS, body, 0)
    else:
        run_start = 0
        for e in range(N_EXPERTS):
            run_start = body(e, run_start)


def _block_copy(src_ref, dst_ref, dst_blk, sem):
    rows = MOE_BLOCK * ROW_TILES
    return pltpu.make_async_copy(src_ref, dst_ref.at[pl.ds(pl.multiple_of(dst_blk * rows, rows), rows)], sem)


def _dispatch_kernel(n_ref, carry_ref, gstart_ref, pad_lo_ref, pad_hi_ref, nused_ref, lp_ref, h_ref, xg_ref,
                     sorted0_ref, sorted1_ref, zero_ref, run_sem, sem):
    i = pl.program_id(0)
    last = pl.num_programs(0) - 1
    tm = h_ref.shape[0]
    rows = TOP_K * tm
    nblk =xg_ref.shape[0] // (MOE_BLOCK * ROW_TILES)

    @pl.when(i == 0)
    def _():
        zero_ref[...] = jnp.zeros_like(zero_ref)

        def fill(b, carry):
            _block_copy(zero_ref, xg_ref, b, sem).start()
            return carry

        def fill_wait(b, carry):
            _block_copy(zero_ref, xg_ref, b, sem).wait()
            return carry

        lax.fori_loop(nused_ref[0], nblk, fill, 0)
        lax.fori_loop(nused_ref[0], nblk, fill_wait, 0)

    @pl.when(i < N_EXPERTS)
    def _():
        n_pad = pad_hi_ref[i] - pad_lo_ref[i]
        for fn in (lambda cp: cp.start(), lambda cp: cp.wait()):
            _run_copies(n_pad, zero_ref, 0, xg_ref, pad_lo_ref[i], sem, MOE_BLOCK - 1, fn)

    def runs(tile, buf_ref, fn, live=None):
        _tile_runs(tile, n_ref, carry_ref, gstart_ref, buf_ref, xg_ref, run_sem, tm, fn, True, live)

    def wait_tile(buf_ref):
        pltpu.make_async_copy(buf_ref, xg_ref.at[pl.ds(0, rows * ROW_TILES)], run_sem).wait()

    def step(cur_ref, prev_ref):
        @pl.when(i >= 2)
        def _():
            wait_tile(cur_ref)

        row_id = lax.broadcasted_iota(jnp.int32, (rows, tm), 0)
        perm = row_id == lp_ref[0:1, :]
        for k in range(1, TOP_K):
            perm = perm | (row_id == lp_ref[k:k + 1, :])
        srt = jnp.dot(jnp.where(perm, 1.0, 0.0).astype(BF16), h_ref[...], preferred_element_type=F32)
        for s in range(ROW_TILES):
            cur_ref[pl.ds(s, rows, stride=ROW_TILES), :] = srt[:, s * LANES:(s + 1) * LANES]

        runs(jnp.maximum(i - 1, 0), prev_ref, lambda cp: cp.start(), live=i >= 1)

        @pl.when(i == last)
        def _():
            @pl.when(i >= 1)
            def _():
                wait_tile(prev_ref)

            runs(i, cur_ref, lambda cp: cp.start())
            wait_tile(cur_ref)

    @pl.when(i % 2 == 0)
    def _():
        step(sorted0_ref, sorted1_ref)

    @pl.when(i % 2 == 1)
    def _():
        step(sorted1_ref, sorted0_ref)


def _dispatch(tile_n, tile_carry, group_start, pad_lo, pad_hi, nused, lp_t, h, cap, tm):
    t, d = h.shape
    assert t // tm >= N_EXPERTS
    grid_spec = pltpu.PrefetchScalarGridSpec(
        num_scalar_prefetch=6,
        grid=(t // tm,),
        in_specs=[
            pl.BlockSpec((TOP_K, tm), lambda i, *_: (0, i)),
            pl.BlockSpec((tm, d), lambda i, *_: (i, 0)),
        ],
        out_specs=pl.BlockSpec(memory_space=pl.ANY),
        scratch_shapes=[
            pltpu.VMEM((TOP_K * tm * ROW_TILES, LANES), F32),
            pltpu.VMEM((TOP_K * tm * ROW_TILES, LANES), F32),
            pltpu.VMEM((MOE_BLOCK * ROW_TILES, LANES), F32),
            pltpu.SemaphoreType.DMA, pltpu.SemaphoreType.DMA,
        ],
    )
    return pl.pallas_call(
        _dispatch_kernel,
        grid_spec=grid_spec,
        out_shape=jax.ShapeDtypeStruct((cap * ROW_TILES, LANES), F32),
        compiler_params=pltpu.CompilerParams(dimension_semantics=("arbitrary",), has_side_effects=True,
                                             vmem_limit_bytes=VMEM_LIMIT),
        name="moe_dispatch",
    )(tile_n, tile_carry, group_start, pad_lo, pad_hi, nused, lp_t, h)


def _weight_copies(w1_hbm, w2_hbm, w1buf, w2buf, sems, layer, e, slot):
    return (pltpu.make_async_copy(w1_hbm.at[layer, e], w1buf.at[slot], sems.at[0, slot]),
            pltpu.make_async_copy(w2_hbm.at[layer, e], w2buf.at[slot], sems.at[1, slot]))


def _expert_kernel(blk_e_ref, nused_ref, next_e_ref, slot_ref, blk_rows_ref, x_ref, w1_hbm, w2_hbm, bg_ref, bl_ref, b2_ref,
                   sel_ref, o_ref, w1buf, w2buf, wg_s, wl_s, w2_s, sems, *, layer):
    blk = pl.program_id(0)
    active = blk < nused_ref[0]
    e = blk_e_ref[blk]
    slot = slot_ref[e]
    new_expert = (blk == 0) | (e != blk_e_ref[jnp.maximum(blk - 1, 0)])
    copies = functools.partial(_weight_copies, w1_hbm, w2_hbm, w1buf, w2buf, sems, layer)

    @pl.when(blk == 0)
    def _():
        for cp in copies(e, slot):
            cp.start()

    @pl.when(active & new_expert)
    def _():
        nxt = next_e_ref[e]

        @pl.when(nxt >= 0)
        def _():
            for cp in copies(nxt, 1 - slot):
                cp.start()

        for cp in copies(e, slot):
            cp.wait()
        half = SPLIT_W // 2
        for j in range(w1buf.shape[2] // SPLIT_W):
            chunk = w1buf[slot, :, j * SPLIT_W:(j + 1) * SPLIT_W].astype(BF16)
            split = jnp.dot(chunk, sel_ref[...], preferred_element_type=F32)
            wg_s[:, j * half:(j + 1) * half] = split[:, :half].astype(BF16)
            wl_s[:, j * half:(j + 1) * half] = split[:, half:].astype(BF16)
        w2_s[...] = w2buf[slot].astype(BF16)

    def mlp(rows):
        x = jnp.concatenate([x_ref[pl.ds(s, rows, stride=ROW_TILES), :]
                             for s in range(ROW_TILES)], axis=1).astype(BF16)
        glu = jnp.dot(x, wg_s[...], preferred_element_type=F32) + bg_ref[0]
        lin = jnp.dot(x, wl_s[...], preferred_element_type=F32) + bl_ref[0]
        glu = jnp.minimum(glu, SWIGLU_LIMIT)
        lin = jnp.clip(lin, -SWIGLU_LIMIT, SWIGLU_LIMIT)
        act = glu * jax.nn.sigmoid(SWIGLU_ALPHA * glu) * (lin + 1.0)
        y = jnp.dot(act.astype(BF16), w2_s[...], preferred_element_type=F32) + b2_ref[0]
        for s in range(ROW_TILES):
            o_ref[pl.ds(s, rows, stride=ROW_TILES), :] = y[:, s * LANES:(s + 1) * LANES]
        if rows < MOE_BLOCK:
            o_ref[pl.ds(rows * ROW_TILES, (MOE_BLOCK - rows) * ROW_TILES), :] = jnp.zeros(
                ((MOE_BLOCK - rows) * ROW_TILES, LANES), o_ref.dtype)

    valid_rows = jnp.where(active, blk_rows_ref[blk], 0)

    @pl.when(valid_rows > MOE_SUB)
    def _():
        mlp(MOE_BLOCK)

    @pl.when((valid_rows > 0) & (valid_rows <= MOE_SUB))
    def _():
        mlp(MOE_SUB)

    @pl.when(valid_rows <= 0)
    def _():
        o_ref[...] = jnp.zeros_like(o_ref)


def _experts(layer, blk_e, nused, next_e, e_slot, blk_rows, xg_rows, w1_all, w2_all, b_glu, b_lin, b2):
    nblk = blk_e.shape[0]
    d = D_MODEL
    ff = w2_all.shape[2]
    col = jnp.arange(SPLIT_W)
    sel = (jnp.arange(SPLIT_W)[None, :] == ((col % 2) * (SPLIT_W // 2) + col // 2)[:, None]).astype(BF16)

    def blk_map(b, blk_e_ref, nused_ref, *_):
        return (jnp.minimum(b, nused_ref[0] - 1), 0)

    def e_map(b, blk_e_ref, nused_ref, *_):
        return (blk_e_ref[jnp.minimum(b, nused_ref[0] - 1)], 0, 0)

    grid_spec = pltpu.PrefetchScalarGridSpec(
        num_scalar_prefetch=5,
        grid=(nblk,),
        in_specs=[
            pl.BlockSpec((MOE_BLOCK * ROW_TILES, LANES), blk_map),
            pl.BlockSpec(memory_space=pl.ANY),
            pl.BlockSpec(memory_space=pl.ANY),
            pl.BlockSpec((1, 1, ff), e_map),
            pl.BlockSpec((1, 1, ff), e_map),
            pl.BlockSpec((1, 1, d), e_map),
            pl.BlockSpec((SPLIT_W, SPLIT_W), lambda b, *_: (0, 0)),
        ],
        out_specs=pl.BlockSpec((MOE_BLOCK * ROW_TILES, LANES), lambda b, *_: (b, 0)),
        scratch_shapes=[
            pltpu.VMEM((2, d, 2 * ff), F32), pltpu.VMEM((2, ff, d), F32),
            pltpu.VMEM((d, ff), BF16), pltpu.VMEM((d, ff), BF16), pltpu.VMEM((ff, d), BF16),
            pltpu.SemaphoreType.DMA((2, 2)),
        ],
    )
    return pl.pallas_call(
        functools.partial(_expert_kernel, layer=layer),
        grid_spec=grid_spec,
        out_shape=jax.ShapeDtypeStruct(xg_rows.shape, F32),
        compiler_params=_cparams("arbitrary"),
        name="moe_experts",
    )(blk_e, nused, next_e, e_slot, blk_rows, xg_rows, w1_all, w2_all, b_glu, b_lin, b2, sel)


def _combine_kernel(n_ref, carry_ref, gstart_ref, yg_ref, lp_ref, gate_ref, x_ref, g2_ref, o_ref, sorted_ref, sem):
    i = pl.program_id(0)
    ntiles = pl.num_programs(0)
    tm = x_ref.shape[0]
    rows = TOP_K * tm
    slot = i % 2

    def runs(tile, buf, fn, live=None):
        _tile_runs(tile, n_ref, carry_ref, gstart_ref, sorted_ref.at[buf], yg_ref, sem.at[buf], tm, fn, False, live)

    @pl.when(i == 0)
    def _():
        runs(0, 0, lambda cp: cp.start())

    runs(jnp.minimum(i + 1, ntiles - 1), 1 - slot, lambda cp: cp.start(), live=i + 1 < ntiles)

    eye = (lax.broadcasted_iota(jnp.int32, (tm, tm), 0) ==
           lax.broadcasted_iota(jnp.int32, (tm, tm), 1)).astype(F32)
    rows_t = jnp.concatenate([gate_ref[...], lp_ref[...].astype(F32)], axis=0)
    cols = lax.dot_general(eye, rows_t, (((1,), (1,)), ((), ())), preferred_element_type=F32, precision=HI)
    col_id = lax.broadcasted_iota(jnp.int32, (tm, rows), 1)
    weights = jnp.zeros((tm, rows), F32)
    for k in range(TOP_K):
        lp_col = (cols[:, TOP_K + k:TOP_K + k + 1] + 0.5).astype(jnp.int32)
        weights = weights + jnp.where(col_id == lp_col, cols[:, k:k + 1], 0.0)

    pltpu.make_async_copy(yg_ref.at[pl.ds(0, rows * ROW_TILES)], sorted_ref.at[slot], sem.at[slot]).wait()
    y = jnp.concatenate(
        [sorted_ref[slot, pl.ds(s, rows, stride=ROW_TILES), :] for s in range(ROW_TILES)], axis=1).astype(BF16)
    acc = jnp.dot(weights.astype(BF16), y, preferred_element_type=F32)
    o_ref[...] = x_ref[...] + g2_ref[0] * acc


def _combine(tile_n, tile_carry, group_start, yg_rows, lp_t, gate_t, x, gate2, seq, tm):
    t, d = x.shape
    per_b = seq // tm
    grid_spec = pltpu.PrefetchScalarGridSpec(
        num_scalar_prefetch=3,
        grid=(t // tm,),
        in_specs=[
            pl.BlockSpec(memory_space=pl.ANY),
            pl.BlockSpec((TOP_K, tm), lambda i, *_: (0, i)),
            pl.BlockSpec((TOP_K, tm), lambda i, *_: (0, i)),
            pl.BlockSpec((tm, d), lambda i, *_: (i, 0)),
            pl.BlockSpec((1, 1, d), lambda i, *_: (i // per_b, 0, 0)),
        ],
        out_specs=pl.BlockSpec((tm, d), lambda i, *_: (i, 0)),
        scratch_shapes=[pltpu.VMEM((2, TOP_K * tm * ROW_TILES, LANES), F32), pltpu.SemaphoreType.DMA((2,))],
    )
    return pl.pallas_call(
        _combine_kernel,
        grid_spec=grid_spec,
        out_shape=jax.ShapeDtypeStruct((t, d), F32),
        compiler_params=_cparams("arbitrary"),
        name="moe_combine",
    )(tile_n, tile_carry, group_start, yg_rows, lp_t, gate_t, x, gate2)


def _moe(layer, x, g, shift, scale, gate2, r_w, r_b, w1_all, b1_glu, b1_lin, w2_all, b2, seq):
    t = x.shape[0]
    h, lp_t, gate_t, tile_n, tile_carry, counts = _router(x, g, shift, scale, r_w, r_b, seq, MOE_TILE)
    tile_n = tile_n.reshape(-1)
    tile_carry = tile_carry.reshape(-1)
    counts = counts.reshape(-1)
    padded = ((counts + MOE_BLOCK - 1) // MOE_BLOCK) * MOE_BLOCK
    group_end = jnp.cumsum(padded)
    group_start = (group_end - padded).astype(jnp.int32)
    cap = t * TOP_K + N_EXPERTS * MOE_BLOCK
    nblk = cap // MOE_BLOCK
    blk_start = jnp.arange(nblk, dtype=jnp.int32) * MOE_BLOCK
    e_ids = jnp.arange(N_EXPERTS, dtype=jnp.int32)
    past = (blk_start[None, :] >= group_end[:, None]).astype(jnp.int32)
    blk_e = jnp.minimum(jnp.sum(past, axis=0), N_EXPERTS - 1).astype(jnp.int32)
    nused = (group_end[-1:] // MOE_BLOCK).astype(jnp.int32)

    pad_lo = (group_start + counts).astype(jnp.int32)
    pad_hi = group_end.astype(jnp.int32)
    blk_pad_lo = jnp.sum(jnp.where(blk_e[None, :] == e_ids[:, None], pad_lo[:, None], 0), axis=0)
    blk_rows = jnp.clip(blk_pad_lo - blk_start, 0, MOE_BLOCK).astype(jnp.int32)
    xg_rows = _dispatch(tile_n, tile_carry, group_start, pad_lo, pad_hi, nused, lp_t, h, cap, MOE_TILE)
    nonempty = counts > 0
    later = jnp.where(nonempty[None, :] & (e_ids[None, :] > e_ids[:, None]), e_ids[None, :], N_EXPERTS)
    next_e = jnp.min(later, axis=1)
    next_e = jnp.where(next_e == N_EXPERTS, -1, next_e).astype(jnp.int32)
    e_slot = ((jnp.cumsum(nonempty.astype(jnp.int32)) - 1) % 2).astype(jnp.int32)
    yg_rows = _experts(layer, blk_e, nused, next_e, e_slot, blk_rows, xg_rows, w1_all, w2_all,
                       b1_glu[:, None, :], b1_lin[:, None, :], b2[:, None, :])
    return _combine(tile_n, tile_carry, group_start, yg_rows, lp_t, gate_t, x, gate2, seq, MOE_TILE)


def kernel(x, c, ada_w, ada_b, norm1_g, norm2_g, m_in_w, m_conv_w, m_conv_b, m_dt_bias, m_A_log, m_D, m_norm_g, m_out_w, a_qkv_w, a_q_norm_g, a_k_norm_g, a_sinks, a_out_w, rel_bias, r_w, r_b, e_w1, e_b1, e_w2, e_b2):
    batch, seq, d = x.shape
    depth = ada_w.shape[0]
    t = batch * seq
    xf = x.reshape(t, d)

    c_pad = jnp.zeros((SUBLANES, d), F32).at[:batch].set(c)
    mod = _adaln(c_pad, ada_w, ada_b)[:, :batch]

    b1_split = jnp.moveaxis(e_b1.reshape(depth, N_EXPERTS, -1, 2), -1, 1)

    for i in range(depth):
        parts =[mod[i, :, p * d:(p + 1) * d].reshape(batch, 1, d) for p in range(6)]
        sh1, sc1, g1, sh2, sc2, g2 = parts
        j = i // 2
        if i % 2 == 0:
            w_zxbc = m_in_w[j, :, :SSM_ZXBC].astype(BF16)
            w_dt = jnp.pad(m_in_w[j, :, SSM_ZXBC:], ((0, 0), (0, LANES - SSM_HEADS))).astype(BF16)
            zxbc, dt_raw = _norm_matmul(xf, norm1_g[i], sh1, sc1, (w_zxbc, w_dt), seq)
            y = _ssd_mixer(zxbc, dt_raw, m_conv_w[j], m_conv_b[j], m_dt_bias[j], m_A_log[j], m_D[j],
                           m_norm_g[j], batch, seq)
            xf = _matmul_residual(y, m_out_w[j].astype(BF16), xf, g1, seq)
        else:
            qkv, = _norm_matmul(xf, norm1_g[i], sh1, sc1, (a_qkv_w[j].astype(BF16),), seq)
            y = _swa_mixer(qkv, a_q_norm_g[j], a_k_norm_g[j], a_sinks[j], rel_bias, batch, seq)
            xf = _matmul_residual(y, a_out_w[j].astype(BF16), xf, g1, seq)
        xf = _moe(i, xf, norm2_g[i], sh2, sc2, g2, r_w[i], r_b[i], e_w1, b1_split[i, 0], b1_split[i, 1],
                  e_w2, e_b2[i], seq)
    return xf.reshape(batch, seq, d)
```

```python
import functools
import math

import jax
import jax.numpy as jnp
from jax import lax
from jax.experimental import pallas as pl
from jax.experimental.pallas import tpu as pltpu

D_MODEL = 1024
EPS = 1e-6
LANES = 128
SUBLANES = 8
ROW_TILES = D_MODEL // LANES

SSM_D_INNER = 2048
SSM_HEAD_DIM = 64
SSM_HEADS = 32
SSM_GROUPS = 4
SSM_STATE = 128
SSM_CONV = 4
SSM_CHUNK = 128
SSM_GN = SSM_GROUPS * SSM_STATE
SSM_ZXBC = 2 * SSM_D_INNER + 2 * SSM_GN
SSM_GROUP_W = SSM_D_INNER // SSM_GROUPS

ATTN_HEAD_DIM = 64
ATTN_Q_HEADS = 16
ATTN_KV_HEADS = 4
ATTN_Q_PER_KV = 4
WINDOW = 128
REL_BUCKETS = 32
REL_MAX_DIST = 128

N_EXPERTS = 32
TOP_K = 4
SWIGLU_ALPHA = 1.702
SWIGLU_LIMIT = 7.0
MOE_BLOCK = 512
MOE_SUB = 256
MOE_TILE = 256
SPLIT_W = 256

VMEM_LIMIT = 56 * 1024 * 1024
HI = lax.Precision.HIGHEST
F32 = jnp.float32
BF16 = jnp.bfloat16


def _cparams(*sem):
    return pltpu.CompilerParams(dimension_semantics=sem, vmem_limit_bytes=VMEM_LIMIT)


def _norm_modulate(x, g, shift, scale):
    ms = jnp.mean(x * x, axis=-1, keepdims=True)
    return x * lax.rsqrt(ms + EPS) * g * (1.0 + scale) + shift


def _adaln_kernel(c_ref, w_ref, b_ref, o_ref):
    c = c_ref[...]
    c_act = c * jax.nn.sigmoid(c)
    o_ref[0] = jnp.dot(c_act, w_ref[0], preferred_element_type=F32, precision=HI) + b_ref[0]


def _adaln(c_pad, ada_w, ada_b):
    depth, d, n = ada_w.shape
    tn = 1536
    return pl.pallas_call(
        _adaln_kernel,
        grid=(depth, n // tn),
        in_specs=[
            pl.BlockSpec((SUBLANES, d), lambda i, j: (0, 0)),
            pl.BlockSpec((1, d, tn), lambda i, j: (i, 0, j)),
            pl.BlockSpec((1, 1, tn), lambda i, j: (i, 0, j)),
        ],
        out_specs=pl.BlockSpec((1, SUBLANES, tn), lambda i, j: (i, 0, j)),
        out_shape=jax.ShapeDtypeStruct((depth, SUBLANES, n), F32),
        compiler_params=_cparams("arbitrary", "arbitrary"),
        name="adaln",
    )(c_pad, ada_w, ada_b.reshape(depth, 1, n))


def _norm_matmul_kernel(x_ref, g_ref, sh_ref, sc_ref, *refs):
    w_refs, o_refs = refs[:len(refs) // 2], refs[len(refs) // 2:]
    h = _norm_modulate(x_ref[...], g_ref[...], sh_ref[0], sc_ref[0]).astype(BF16)
    for w_ref, o_ref in zip(w_refs, o_refs):
        o_ref[...] = jnp.dot(h, w_ref[...], preferred_element_type=F32)


def _norm_matmul(x, g, shift, scale, weights_bf16, seq, tm=512):
    t, d = x.shape
    per_b = seq // tm
    return pl.pallas_call(
        _norm_matmul_kernel,
        grid=(t // tm,),
        in_specs=[
            pl.BlockSpec((tm, d), lambda i: (i, 0)),
            pl.BlockSpec((1, d), lambda i: (0, 0)),
            pl.BlockSpec((1, 1, d), lambda i: (i // per_b, 0, 0)),
            pl.BlockSpec((1, 1, d), lambda i: (i // per_b, 0, 0)),
        ] + [pl.BlockSpec(w.shape, lambda i: (0, 0), pipeline_mode=pl.Buffered(1))
             for w in weights_bf16],
        out_specs=[pl.BlockSpec((tm, w.shape[1]), lambda i: (i, 0)) for w in weights_bf16],
        out_shape=[jax.ShapeDtypeStruct((t, w.shape[1]), F32) for w in weights_bf16],
        compiler_params=_cparams("arbitrary"),
        name="norm_matmul",
    )(x, g.reshape(1, d), shift, scale, *weights_bf16)


def _matmul_residual_kernel(y_ref, w_ref, x_ref, gate_ref, o_ref):
    acc = jnp.dot(y_ref[...], w_ref[...], preferred_element_type=F32)
    o_ref[...] = x_ref[...] + gate_ref[0] * acc


def _matmul_residual(y_bf16, w_bf16, x, gate, seq, tm=512):
    t, k = y_bf16.shape
    d = x.shape[1]
    per_b = seq // tm
    return pl.pallas_call(
        _matmul_residual_kernel,
        grid=(t // tm,),
        in_specs=[
            pl.BlockSpec((tm, k), lambda i: (i, 0)),
            pl.BlockSpec((k, d), lambda i: (0, 0)),
            pl.BlockSpec((tm, d), lambda i: (i, 0)),
            pl.BlockSpec((1, 1, d), lambda i: (i // per_b, 0, 0)),
        ],
        out_specs=pl.BlockSpec((tm, d), lambda i: (i, 0)),
        out_shape=jax.ShapeDtypeStruct((t, d), F32),
        compiler_params=_cparams("arbitrary"),
        name="matmul_residual",
    )(y_bf16, w_bf16, x, gate)


def _causal_conv_silu(cur, prev_tail, w, b):
    rows = lax.broadcasted_iota(jnp.int32, (SUBLANES, cur.shape[1]), 0)
    acc = b + w[SSM_CONV - 1:SSM_CONV] * cur
    for d in range(1, SSM_CONV):
        rolled = pltpu.roll(cur, d, axis=0)
        top = jnp.where(rows < d, pltpu.roll(prev_tail, d, axis=0), rolled[0:SUBLANES])
        shifted = jnp.concatenate([top, rolled[SUBLANES:]], axis=0)
        acc = acc + w[SSM_CONV - 1 - d:SSM_CONV - d] * shifted
    return acc * jax.nn.sigmoid(acc)


def _split3(x):
    hi = x.astype(BF16)
    rem = x - hi.astype(F32)
    mid = rem.astype(BF16)
    lo = (rem - mid.astype(F32)).astype(BF16)
    return jnp.concatenate([hi, mid, lo], axis=1)


def _ssd_kernel(z_ref, xs_ref, bc_ref, dt_ref, cwx_ref, cbx_ref, cwb_ref, cbb_ref, dtb_ref, alog_ref,
                dskip_ref, ng_ref, hexp_ref, lexp_ref, o_ref, tailx_ref, tailb_ref, state_ref):
    c = pl.program_id(1)

    @pl.when(c == 0)
    def _():
        tailx_ref[...] = jnp.zeros_like(tailx_ref)
        tailb_ref[...] = jnp.zeros_like(tailb_ref)
        state_ref[...] = jnp.zeros_like(state_ref)

    xs_raw = xs_ref[...]
    bc_raw = bc_ref[...]
    xs = _causal_conv_silu(xs_raw, tailx_ref[...], cwx_ref[...], cbx_ref[...])
    bc = _causal_conv_silu(bc_raw, tailb_ref[...], cwb_ref[...], cbb_ref[...])
    tailx_ref[...] = xs_raw[SSM_CHUNK - SUBLANES:]
    tailb_ref[...] = bc_raw[SSM_CHUNK - SUBLANES:]

    dt_in = dt_ref[...][:, :SSM_HEADS] + dtb_ref[...]
    dt = jnp.maximum(dt_in, 0.0) + jnp.log1p(jnp.exp(-jnp.abs(dt_in)))
    a_neg = -jnp.exp(alog_ref[...])
    d_a = dt * a_neg
    li = lax.broadcasted_iota(jnp.int32, (SSM_CHUNK, SSM_CHUNK), 0)
    si = lax.broadcasted_iota(jnp.int32, (SSM_CHUNK, SSM_CHUNK), 1)
    causal = li >= si
    tri = causal.astype(F32)
    a_cs = jnp.dot(tri, d_a, preferred_element_type=F32, precision=HI)
    a_cs_t = lax.dot_general(d_a, tri, (((0,), (1,)), ((), ())),
                             preferred_element_type=F32, precision=HI)
    a_last = a_cs[SSM_CHUNK - 1:SSM_CHUNK]
    e_out = jnp.exp(a_cs)
    e_state = jnp.exp(a_last - a_cs) * dt
    small = jnp.concatenate([dt, e_out, e_state], axis=0)
    wide = jnp.dot(_split3(small), hexp_ref[...], preferred_element_type=F32)
    dt_w = wide[0:SSM_CHUNK]
    e_out_w = wide[SSM_CHUNK:2 * SSM_CHUNK]
    e_state_w = wide[2 * SSM_CHUNK:]
    a_col = jnp.dot(_split3(a_cs), lexp_ref[...], preferred_element_type=F32)

    x_dt = (xs * dt_w).astype(BF16)
    x_state = (xs * e_state_w).astype(BF16)
    chunk_decay_w = e_out_w[SSM_CHUNK - 1:SSM_CHUNK]

    heads_per_group = SSM_HEADS // SSM_GROUPS
    y_parts = []
    for g in range(SSM_GROUPS):
        b_g = bc[:, g * SSM_STATE:(g + 1) * SSM_STATE].astype(BF16)
        c_g = bc[:, SSM_GN + g * SSM_STATE:SSM_GN + (g + 1) * SSM_STATE].astype(BF16)
        cb = lax.dot_general(c_g, b_g, (((1,), (1,)), ((), ())), preferred_element_type=F32)
        gsl = slice(g * SSM_GROUP_W, (g + 1) * SSM_GROUP_W)
        h_prev = state_ref[g]
        y_off = jnp.dot(c_g, h_prev.astype(BF16), preferred_element_type=F32) * e_out_w[:, gsl]
        diag = []
        for r in range(heads_per_group):
            h = g * heads_per_group + r
            seg = a_col[:, h * SSM_CHUNK:(h + 1) * SSM_CHUNK] - a_cs_t[h:h + 1, :]
            decay = jnp.exp(jnp.where(causal, seg, -jnp.inf))
            m = (cb * decay).astype(BF16)
            diag.append(jnp.dot(m, x_dt[:, h * SSM_HEAD_DIM:(h + 1) * SSM_HEAD_DIM],
                                preferred_element_type=F32))
        y_parts.append(jnp.concatenate(diag, axis=1) + y_off)
        upd = lax.dot_general(b_g, x_state[:, gsl], (((0,), (0,)), ((), ())), preferred_element_type=F32)
        state_ref[g] = h_prev * chunk_decay_w[:, gsl] + upd

    y = jnp.concatenate(y_parts, axis=1) + dskip_ref[...] * xs
    z = z_ref[...]
    y = y * (z * jax.nn.sigmoid(z))
    normed = []
    for g in range(SSM_GROUPS):
        y_g = y[:, g * SSM_GROUP_W:(g + 1) * SSM_GROUP_W]
        normed.append(y_g * lax.rsqrt(jnp.mean(y_g * y_g, axis=-1, keepdims=True) + EPS))
    o_ref[...] = (jnp.concatenate(normed, axis=1) * ng_ref[...]).astype(o_ref.dtype)


def _ssd_mixer(zxbc, dt_raw, conv_w, conv_b, dt_bias, a_log, d_skip, norm_g, batch, seq):
    t = zxbc.shape[0]
    nc = seq // SSM_CHUNK
    head_expand = jnp.tile(jnp.repeat(jnp.eye(SSM_HEADS, dtype=BF16), SSM_HEAD_DIM, axis=1), (3, 1))
    lane_expand = jnp.tile(jnp.repeat(jnp.eye(SSM_HEADS, dtype=BF16), SSM_CHUNK, axis=1), (3, 1))
    row = lambda b, c: (b * nc + c, 0)
    const2 = lambda b, c: (0, 0)
    bc_w = 2 * SSM_GN
    return pl.pallas_call(
        _ssd_kernel,
        grid=(batch, nc),
        in_specs=[
            pl.BlockSpec((SSM_CHUNK, SSM_D_INNER), row),
            pl.BlockSpec((SSM_CHUNK, SSM_D_INNER), lambda b, c: (b * nc + c, 1)),
            pl.BlockSpec((SSM_CHUNK, bc_w), lambda b, c: (b * nc + c, 2 * SSM_D_INNER // bc_w)),
            pl.BlockSpec((SSM_CHUNK, LANES), row),
            pl.BlockSpec((SSM_CONV, SSM_D_INNER), const2),
            pl.BlockSpec((1, SSM_D_INNER), const2),
            pl.BlockSpec((SSM_CONV, bc_w), const2),
            pl.BlockSpec((1, bc_w), const2),
            pl.BlockSpec((1, SSM_HEADS), const2),
            pl.BlockSpec((1, SSM_HEADS), const2),
            pl.BlockSpec((1, SSM_D_INNER), const2),
            pl.BlockSpec((1, SSM_D_INNER), const2),
            pl.BlockSpec((3 * SSM_HEADS, SSM_D_INNER), const2),
            pl.BlockSpec((3 * SSM_HEADS, SSM_HEADS * SSM_CHUNK), const2),
        ],
        out_specs=pl.BlockSpec((SSM_CHUNK, SSM_D_INNER), row),
        out_shape=jax.ShapeDtypeStruct((t, SSM_D_INNER), BF16),
        scratch_shapes=[
            pltpu.VMEM((SUBLANES, SSM_D_INNER), F32),
            pltpu.VMEM((SUBLANES, bc_w), F32),
            pltpu.VMEM((SSM_GROUPS, SSM_STATE, SSM_GROUP_W), F32),
        ],
        compiler_params=_cparams("arbitrary", "arbitrary"),
        name="ssd_mixer",
    )(zxbc, zxbc, zxbc, dt_raw,
      conv_w[:, :SSM_D_INNER], conv_b[:SSM_D_INNER].reshape(1, -1),
      conv_w[:, SSM_D_INNER:], conv_b[SSM_D_INNER:].reshape(1, -1),
      dt_bias.reshape(1, -1), a_log.reshape(1, -1),
      jnp.repeat(d_skip, SSM_HEAD_DIM).reshape(1, -1), norm_g.reshape(1, -1),
      head_expand, lane_expand)


def _head_rms(x, g):
    return x * lax.rsqrt(jnp.mean(x * x, axis=-1, keepdims=True) + EPS) * g


def _swa_kernel(q_ref, kvc_ref, kvp_ref, bucket_ref, qg_ref, kg_ref, rel_ref, sink_ref, o_ref,
                bias_ref, sinkrow_ref):
    b = pl.program_id(0)
    i = pl.program_id(1)

    @pl.when((b == 0) & (i == 0))
    def _():
        bucket = bucket_ref[...]
        kj = lax.broadcasted_iota(jnp.int32, (2 * WINDOW, WINDOW), 0)
        qi = lax.broadcasted_iota(jnp.int32, (2 * WINDOW, WINDOW), 1)
        dist = qi + WINDOW - kj
        band = (dist >= 0) & (dist < WINDOW)
        for h in range(ATTN_Q_HEADS):
            g, r = divmod(h, ATTN_Q_PER_KV)
            acc = jnp.zeros(bucket.shape, F32)
            for k in range(REL_BUCKETS):
                acc = jnp.where(bucket == k, rel_ref[k, h], acc)
            cols = slice(r * WINDOW, (r + 1) * WINDOW)
            bias_ref[0, g, :, cols] = jnp.where(band, acc, -jnp.inf)
            bias_ref[1, g, :, cols] = jnp.where(band & (kj >= WINDOW), acc, -jnp.inf)
            sinkrow_ref[g, :, cols] = jnp.full((1, WINDOW), sink_ref[h], F32)

    first = (i == 0).astype(jnp.int32)
    q_t = q_ref[...].T
    kv_c = kvc_ref[...]
    kv_p = kvp_ref[...]
    kv_w = ATTN_KV_HEADS * ATTN_HEAD_DIM
    q_gain = qg_ref[...]
    outs = []
    for g in range(ATTN_KV_HEADS):
        ksl = slice(g * ATTN_HEAD_DIM, (g + 1) * ATTN_HEAD_DIM)
        vsl = slice(kv_w + g * ATTN_HEAD_DIM, kv_w + (g + 1) * ATTN_HEAD_DIM)
        k = jnp.concatenate([kv_p[:, ksl], kv_c[:, ksl]], axis=0)
        v = jnp.concatenate([kv_p[:, vsl], kv_c[:, vsl]], axis=0).astype(BF16)
        k = _head_rms(k, kg_ref[...]).astype(BF16)
        q_heads = []
        for r in range(ATTN_Q_PER_KV):
            h = g * ATTN_Q_PER_KV + r
            q_h = q_t[h * ATTN_HEAD_DIM:(h + 1) * ATTN_HEAD_DIM]
            inv = lax.rsqrt(jnp.mean(q_h * q_h, axis=0, keepdims=True) + EPS)
            q_heads.append((q_h * inv * q_gain).astype(BF16))
        q = jnp.concatenate(q_heads, axis=1)
        s = jnp.dot(k, q, preferred_element_type=F32) + bias_ref[first, g]
        sink = sinkrow_ref[g]
        m = jnp.maximum(jnp.max(s, axis=0, keepdims=True), sink)
        p = jnp.exp(s - m)
        denom = jnp.sum(p, axis=0, keepdims=True) + jnp.exp(sink - m)
        pv = lax.dot_general(v, p.astype(BF16), (((0,), (0,)), ((), ())), preferred_element_type=F32)
        pv = pv * (1.0 / denom)
        outs.extend(pv[:, r * WINDOW:(r + 1) * WINDOW] for r in range(ATTN_Q_PER_KV))
    o_ref[...] = jnp.concatenate(outs, axis=0).T.astype(o_ref.dtype)


def _t5_causal_bucket(dist):
    max_exact = REL_BUCKETS // 2
    d = jnp.maximum(dist, 1).astype(F32)
    large = max_exact + (jnp.log(d / max_exact) / math.log(REL_MAX_DIST / max_exact)
                         * (REL_BUCKETS - max_exact)).astype(jnp.int32)
    large = jnp.minimum(large, REL_BUCKETS - 1)
    return jnp.where(dist < max_exact, dist, large)


def _swa_mixer(qkv, q_norm_g, k_norm_g, sinks, rel_bias, batch, seq):
    t = qkv.shape[0]
    nb = seq // WINDOW
    q_w = ATTN_Q_HEADS * ATTN_HEAD_DIM
    kv_w2 = 2 * ATTN_KV_HEADS * ATTN_HEAD_DIM
    kj = jnp.arange(2 * WINDOW)[:, None]
    qi = jnp.arange(WINDOW)[None, :]
    bucket = _t5_causal_bucket(jnp.maximum(qi + WINDOW - kj, 0)).astype(jnp.int32)
    q_gain = jnp.broadcast_to((q_norm_g * (ATTN_HEAD_DIM ** -0.5))[:, None], (ATTN_HEAD_DIM, WINDOW))
    const2 = lambda b, i: (0, 0)
    smem = pl.BlockSpec(memory_space=pltpu.SMEM)
    return pl.pallas_call(
        _swa_kernel,
        grid=(batch, nb),
        in_specs=[
            pl.BlockSpec((WINDOW, q_w), lambda b, i: (b * nb + i, 0)),
            pl.BlockSpec((WINDOW, kv_w2), lambda b, i: (b * nb + i, q_w // kv_w2)),
            pl.BlockSpec((WINDOW, kv_w2), lambda b, i: (b * nb + jnp.maximum(i - 1, 0), q_w // kv_w2)),
            pl.BlockSpec((2 * WINDOW, WINDOW), const2),
            pl.BlockSpec((ATTN_HEAD_DIM, WINDOW), const2),
            pl.BlockSpec((1, ATTN_HEAD_DIM), const2),
            smem,
            smem,
        ],
        out_specs=pl.BlockSpec((WINDOW, q_w), lambda b, i: (b * nb + i, 0)),
        out_shape=jax.ShapeDtypeStruct((t, q_w), BF16),
        scratch_shapes=[pltpu.VMEM((2, ATTN_KV_HEADS, 2 * WINDOW, ATTN_Q_PER_KV * WINDOW), F32),
                        pltpu.VMEM((ATTN_KV_HEADS, 1, ATTN_Q_PER_KV * WINDOW), F32)],
        compiler_params=_cparams("arbitrary", "arbitrary"),
        name="swa_mixer",
    )(qkv, qkv, qkv, bucket, q_gain, k_norm_g.reshape(1, -1), rel_bias, sinks)


def _router_kernel(x_ref, g_ref, sh_ref, sc_ref, rwt_ref, rb_ref,
                   h_ref, lp_ref, gate_ref, tile_n_ref, tile_carry_ref, cnt_ref, carry_ref):
    i = pl.program_id(0)
    tm = x_ref.shape[0]

    @pl.when(i == 0)
    def _():
        carry_ref[...] = jnp.zeros_like(carry_ref)

    h = _norm_modulate(x_ref[...], g_ref[...], sh_ref[0], sc_ref[0])
    h_ref[...] = h.astype(h_ref.dtype)

    logits = lax.dot_general(rwt_ref[...], h, (((1,), (1,)), ((), ())),
                             preferred_element_type=F32, precision=HI) + rb_ref[...]
    e_iota = lax.broadcasted_iota(jnp.int32, logits.shape, 0)
    work = logits
    sels, vals = [], []
    for k in range(TOP_K):
        m = jnp.max(work, axis=0, keepdims=True)
        idx = jnp.min(jnp.where(work == m, e_iota, N_EXPERTS), axis=0, keepdims=True)
        sel = e_iota == idx
        work = jnp.where(sel, -jnp.inf, work)
        sels.append(sel)
        vals.append(m)
    exps = [jnp.exp(v - vals[0]) for v in vals]
    denom = exps[0] + exps[1] + exps[2] + exps[3]
    for k in range(TOP_K):
        gate_ref[k:k + 1, :] = exps[k] / denom

    chosen = sels[0] | sels[1] | sels[2] | sels[3]
    t_row = lax.broadcasted_iota(jnp.int32, (tm, tm), 0)
    t_col = lax.broadcasted_iota(jnp.int32, (tm, tm), 1)
    before = (t_row < t_col).astype(BF16)
    prior = jnp.dot(chosen.astype(BF16), before, preferred_element_type=F32)
    n = jnp.sum(chosen.astype(F32), axis=1, keepdims=True)
    e_row = lax.broadcasted_iota(jnp.int32, (N_EXPERTS, N_EXPERTS), 0)
    e_col = lax.broadcasted_iota(jnp.int32, (N_EXPERTS, N_EXPERTS), 1)
    run_start = jnp.dot((e_col < e_row).astype(F32), jnp.broadcast_to(n, (N_EXPERTS, LANES)),
                        preferred_element_type=F32, precision=HI)[:, :1]
    local = run_start + prior
    for k in range(TOP_K):
        lp_ref[k:k + 1, :] = jnp.sum(jnp.where(sels[k], local, 0.0), axis=0, keepdims=True).astype(jnp.int32)
    tile_n_ref[0] = n.astype(jnp.int32)
    tile_carry_ref[0] = carry_ref[...].astype(jnp.int32)
    total = carry_ref[...] + n
    carry_ref[...] = total
    cnt_ref[...] = total.astype(jnp.int32)


def _router(x, g, shift, scale, r_w, r_b, seq, tm):
    t, d = x.shape
    per_b = seq // tm
    ntiles = t // tm
    return pl.pallas_call(
        _router_kernel,
        grid=(t // tm,),
        in_specs=[
            pl.BlockSpec((tm, d), lambda i: (i, 0)),
            pl.BlockSpec((1, d), lambda i: (0, 0)),
            pl.BlockSpec((1, 1, d), lambda i: (i // per_b, 0, 0)),
            pl.BlockSpec((1, 1, d), lambda i: (i // per_b, 0, 0)),
            pl.BlockSpec((N_EXPERTS, d), lambda i: (0, 0)),
            pl.BlockSpec((N_EXPERTS, 1), lambda i: (0, 0)),
        ],
        out_specs=[
            pl.BlockSpec((tm, d), lambda i: (i, 0)),
            pl.BlockSpec((TOP_K, tm), lambda i: (0, i)),
            pl.BlockSpec((TOP_K, tm), lambda i: (0, i)),
            pl.BlockSpec((1, N_EXPERTS, 1), lambda i: (i, 0, 0)),
            pl.BlockSpec((1, N_EXPERTS, 1), lambda i: (i, 0, 0)),
            pl.BlockSpec((N_EXPERTS, 1), lambda i: (0, 0)),
        ],
        out_shape=[
            jax.ShapeDtypeStruct((t, d), BF16),
            jax.ShapeDtypeStruct((TOP_K, t), jnp.int32),
            jax.ShapeDtypeStruct((TOP_K, t), F32),
            jax.ShapeDtypeStruct((ntiles, N_EXPERTS, 1), jnp.int32),
            jax.ShapeDtypeStruct((ntiles, N_EXPERTS, 1), jnp.int32),
            jax.ShapeDtypeStruct((N_EXPERTS, 1), jnp.int32),
        ],
        scratch_shapes=[pltpu.VMEM((N_EXPERTS, 1), F32)],
        compiler_params=_cparams("arbitrary"),
        name="moe_router",
    )(x, g.reshape(1, d), shift, scale, r_w.T, r_b.reshape(-1, 1))


def _run_copies(n, src_ref, src_row, dst_ref, dst_row, sem, max_rows, fn):
    for b in range(max_rows.bit_length()):
        size = 1 << b

        @pl.when((n & size) != 0)
        def _():
            lo = n & (size - 1)
            fn(pltpu.make_async_copy(
                src_ref.at[pl.ds(pl.multiple_of((src_row + lo) * ROW_TILES, ROW_TILES), size * ROW_TILES)],
                dst_ref.at[pl.ds(pl.multiple_of((dst_row + lo) * ROW_TILES, ROW_TILES), size * ROW_TILES)],
                sem), b)


def _start_piece(cp, b):
    cp.start(priority=b % 2)


def _wait_piece(cp, b):
    cp.wait()


def _tile_runs(tile, n_ref, carry_ref, gstart_ref, sorted_ref, grouped_ref, sem, tm, fn, to_grouped, live=None):
    def body(e, run_start):
        n = n_ref[tile * N_EXPERTS + e]
        if live is not None:
            n = jnp.where(live, n, 0)
        slot0 = gstart_ref[e] + carry_ref[tile * N_EXPERTS + e]
        if to_grouped:
            _run_copies(n, sorted_ref, run_start, grouped_ref, slot0, sem, tm, fn)
        else:
            _run_copies(n, grouped_ref, slot0, sorted_ref, run_start, sem, tm, fn)
        return run_start + n

    if live is None:
        lax.fori_loop(0, N_EXPERTS, body, 0)
    else:
        run_start = 0
        for e in range(N_EXPERTS):
            run_start = body(e, run_start)


def _block_copy(src_ref, dst_ref, dst_blk, sem):
    rows = MOE_BLOCK * ROW_TILES
    return pltpu.make_async_copy(src_ref, dst_ref.at[pl.ds(pl.multiple_of(dst_blk * rows, rows), rows)], sem)


def _dispatch_kernel(n_ref, carry_ref, gstart_ref, pad_lo_ref, pad_hi_ref, nused_ref, lp_ref, h_ref, xg_ref,
                     sorted0_ref, sorted1_ref, zero_ref, run_sem, sem):
    i = pl.program_id(0)
    last = pl.num_programs(0) - 1
    tm = h_ref.shape[0]
    rows = TOP_K * tm
    nblk =xg_ref.shape[0] // (MOE_BLOCK * ROW_TILES)

    @pl.when(i == 0)
    def _():
        zero_ref[...] = jnp.zeros_like(zero_ref)

        def fill(b, carry):
            _block_copy(zero_ref, xg_ref, b, sem).start()
            return carry

        def fill_wait(b, carry):
            _block_copy(zero_ref, xg_ref, b, sem).wait()
            return carry

        lax.fori_loop(nused_ref[0], nblk, fill, 0)
        lax.fori_loop(nused_ref[0], nblk, fill_wait, 0)

    @pl.when(i < N_EXPERTS)
    def _():
        n_pad = pad_hi_ref[i] - pad_lo_ref[i]
        for fn in (_start_piece, _wait_piece):
            _run_copies(n_pad, zero_ref, 0, xg_ref, pad_lo_ref[i], sem, MOE_BLOCK - 1, fn)

    def runs(tile, buf_ref, fn, live=None):
        _tile_runs(tile, n_ref, carry_ref, gstart_ref, buf_ref, xg_ref, run_sem, tm, fn, True, live)

    def wait_tile(buf_ref):
        pltpu.make_async_copy(buf_ref, xg_ref.at[pl.ds(0, rows * ROW_TILES)], run_sem).wait()

    def step(cur_ref, prev_ref):
        @pl.when(i >= 2)
        def _():
            wait_tile(cur_ref)

        row_id = lax.broadcasted_iota(jnp.int32, (rows, tm), 0)
        perm = row_id == lp_ref[0:1, :]
        for k in range(1, TOP_K):
            perm = perm | (row_id == lp_ref[k:k + 1, :])
        srt = jnp.dot(jnp.where(perm, 1.0, 0.0).astype(BF16), h_ref[...], preferred_element_type=F32)
        for s in range(ROW_TILES):
            cur_ref[pl.ds(s, rows, stride=ROW_TILES), :] = srt[:, s * LANES:(s + 1) * LANES]

        runs(jnp.maximum(i - 1, 0), prev_ref, _start_piece, live=i >= 1)

        @pl.when(i == last)
        def _():
            @pl.when(i >= 1)
            def _():
                wait_tile(prev_ref)

            runs(i, cur_ref, _start_piece)
            wait_tile(cur_ref)

    @pl.when(i % 2 == 0)
    def _():
        step(sorted0_ref, sorted1_ref)

    @pl.when(i % 2 == 1)
    def _():
        step(sorted1_ref, sorted0_ref)


def _dispatch(tile_n, tile_carry, group_start, pad_lo, pad_hi, nused, lp_t, h, cap, tm):
    t, d = h.shape
    assert t // tm >= N_EXPERTS
    grid_spec = pltpu.PrefetchScalarGridSpec(
        num_scalar_prefetch=6,
        grid=(t // tm,),
        in_specs=[
            pl.BlockSpec((TOP_K, tm), lambda i, *_: (0, i)),
            pl.BlockSpec((tm, d), lambda i, *_: (i, 0)),
        ],
        out_specs=pl.BlockSpec(memory_space=pl.ANY),
        scratch_shapes=[
            pltpu.VMEM((TOP_K * tm * ROW_TILES, LANES), F32),
            pltpu.VMEM((TOP_K * tm * ROW_TILES, LANES), F32),
            pltpu.VMEM((MOE_BLOCK * ROW_TILES, LANES), F32),
            pltpu.SemaphoreType.DMA, pltpu.SemaphoreType.DMA,
        ],
    )
    return pl.pallas_call(
        _dispatch_kernel,
        grid_spec=grid_spec,
        out_shape=jax.ShapeDtypeStruct((cap * ROW_TILES, LANES), F32),
        compiler_params=pltpu.CompilerParams(dimension_semantics=("arbitrary",), has_side_effects=True,
                                             vmem_limit_bytes=VMEM_LIMIT),
        name="moe_dispatch",
    )(tile_n, tile_carry, group_start, pad_lo, pad_hi, nused, lp_t, h)


def _weight_copies(w1_hbm, w2_hbm, w1buf, w2buf, sems, layer, e, slot):
    return (pltpu.make_async_copy(w1_hbm.at[layer, e], w1buf.at[slot], sems.at[0, slot]),
            pltpu.make_async_copy(w2_hbm.at[layer, e], w2buf.at[slot], sems.at[1, slot]))


def _expert_kernel(blk_e_ref, nused_ref, next_e_ref, slot_ref, blk_rows_ref, x_ref, w1_hbm, w2_hbm, bg_ref, bl_ref, b2_ref,
                   sel_ref, o_ref, w1buf, w2buf, wg_s, wl_s, w2_s, sems, *, layer):
    blk = pl.program_id(0)
    active = blk < nused_ref[0]
    e = blk_e_ref[blk]
    slot = slot_ref[e]
    new_expert = (blk == 0) | (e != blk_e_ref[jnp.maximum(blk - 1, 0)])
    copies = functools.partial(_weight_copies, w1_hbm, w2_hbm, w1buf, w2buf, sems, layer)

    @pl.when(blk == 0)
    def _():
        for cp in copies(e, slot):
            cp.start()

    @pl.when(active & new_expert)
    def _():
        nxt = next_e_ref[e]

        @pl.when(nxt >= 0)
        def _():
            for cp in copies(nxt, 1 - slot):
                cp.start()

        for cp in copies(e, slot):
            cp.wait()
        half = SPLIT_W // 2
        for j in range(w1buf.shape[2] // SPLIT_W):
            chunk = w1buf[slot, :, j * SPLIT_W:(j + 1) * SPLIT_W].astype(BF16)
            split = jnp.dot(chunk, sel_ref[...], preferred_element_type=F32)
            wg_s[:, j * half:(j + 1) * half] = split[:, :half].astype(BF16)
            wl_s[:, j * half:(j + 1) * half] = split[:, half:].astype(BF16)
        w2_s[...] = w2buf[slot].astype(BF16)

    def mlp(rows):
        x = jnp.concatenate([x_ref[pl.ds(s, rows, stride=ROW_TILES), :]
                             for s in range(ROW_TILES)], axis=1).astype(BF16)
        glu = jnp.dot(x, wg_s[...], preferred_element_type=F32) + bg_ref[0]
        lin = jnp.dot(x, wl_s[...], preferred_element_type=F32) + bl_ref[0]
        glu = jnp.minimum(glu, SWIGLU_LIMIT)
        lin = jnp.clip(lin, -SWIGLU_LIMIT, SWIGLU_LIMIT)
        act = glu * jax.nn.sigmoid(SWIGLU_ALPHA * glu) * (lin + 1.0)
        y = jnp.dot(act.astype(BF16), w2_s[...], preferred_element_type=F32) + b2_ref[0]
        for s in range(ROW_TILES):
            o_ref[pl.ds(s, rows, stride=ROW_TILES), :] = y[:, s * LANES:(s + 1) * LANES]
        if rows < MOE_BLOCK:
            o_ref[pl.ds(rows * ROW_TILES, (MOE_BLOCK - rows) * ROW_TILES), :] = jnp.zeros(
                ((MOE_BLOCK - rows) * ROW_TILES, LANES), o_ref.dtype)

    valid_rows = jnp.where(active, blk_rows_ref[blk], 0)

    @pl.when(valid_rows > MOE_SUB)
    def _():
        mlp(MOE_BLOCK)

    @pl.when((valid_rows > 0) & (valid_rows <= MOE_SUB))
    def _():
        mlp(MOE_SUB)

    @pl.when(valid_rows <= 0)
    def _():
        o_ref[...] = jnp.zeros_like(o_ref)


def _experts(layer, blk_e, nused, next_e, e_slot, blk_rows, xg_rows, w1_all, w2_all, b_glu, b_lin, b2):
    nblk = blk_e.shape[0]
    d = D_MODEL
    ff = w2_all.shape[2]
    col = jnp.arange(SPLIT_W)
    sel = (jnp.arange(SPLIT_W)[None, :] == ((col % 2) * (SPLIT_W // 2) + col // 2)[:, None]).astype(BF16)

    def blk_map(b, blk_e_ref, nused_ref, *_):
        return (jnp.minimum(b, nused_ref[0] - 1), 0)

    def e_map(b, blk_e_ref, nused_ref, *_):
        return (blk_e_ref[jnp.minimum(b, nused_ref[0] - 1)], 0, 0)

    grid_spec = pltpu.PrefetchScalarGridSpec(
        num_scalar_prefetch=5,
        grid=(nblk,),
        in_specs=[
            pl.BlockSpec((MOE_BLOCK * ROW_TILES, LANES), blk_map),
            pl.BlockSpec(memory_space=pl.ANY),
            pl.BlockSpec(memory_space=pl.ANY),
            pl.BlockSpec((1, 1, ff), e_map),
            pl.BlockSpec((1, 1, ff), e_map),
            pl.BlockSpec((1, 1, d), e_map),
            pl.BlockSpec((SPLIT_W, SPLIT_W), lambda b, *_: (0, 0)),
        ],
        out_specs=pl.BlockSpec((MOE_BLOCK * ROW_TILES, LANES), lambda b, *_: (b, 0)),
        scratch_shapes=[
            pltpu.VMEM((2, d, 2 * ff), F32), pltpu.VMEM((2, ff, d), F32),
            pltpu.VMEM((d, ff), BF16), pltpu.VMEM((d, ff), BF16), pltpu.VMEM((ff, d), BF16),
            pltpu.SemaphoreType.DMA((2, 2)),
        ],
    )
    return pl.pallas_call(
        functools.partial(_expert_kernel, layer=layer),
        grid_spec=grid_spec,
        out_shape=jax.ShapeDtypeStruct(xg_rows.shape, F32),
        compiler_params=_cparams("arbitrary"),
        name="moe_experts",
    )(blk_e, nused, next_e, e_slot, blk_rows, xg_rows, w1_all, w2_all, b_glu, b_lin, b2, sel)


def _combine_kernel(n_ref, carry_ref, gstart_ref, yg_ref, lp_ref, gate_ref, x_ref, g2_ref, o_ref, sorted_ref, sem):
    i = pl.program_id(0)
    ntiles = pl.num_programs(0)
    tm = x_ref.shape[0]
    rows = TOP_K * tm
    slot = i % 2

    def runs(tile, buf, fn, live=None):
        _tile_runs(tile, n_ref, carry_ref, gstart_ref, sorted_ref.at[buf], yg_ref, sem.at[buf], tm, fn, False, live)

    @pl.when(i == 0)
    def _():
        runs(0, 0, _start_piece)

    runs(jnp.minimum(i + 1, ntiles - 1), 1 - slot, _start_piece, live=i + 1 < ntiles)

    eye = (lax.broadcasted_iota(jnp.int32, (tm, tm), 0) ==
           lax.broadcasted_iota(jnp.int32, (tm, tm), 1)).astype(F32)
    rows_t = jnp.concatenate([gate_ref[...], lp_ref[...].astype(F32)], axis=0)
    cols = lax.dot_general(eye, rows_t, (((1,), (1,)), ((), ())), preferred_element_type=F32, precision=HI)
    col_id = lax.broadcasted_iota(jnp.int32, (tm, rows), 1)
    weights = jnp.zeros((tm, rows), F32)
    for k in range(TOP_K):
        lp_col = (cols[:, TOP_K + k:TOP_K + k + 1] + 0.5).astype(jnp.int32)
        weights = weights + jnp.where(col_id == lp_col, cols[:, k:k + 1], 0.0)

    pltpu.make_async_copy(yg_ref.at[pl.ds(0, rows * ROW_TILES)], sorted_ref.at[slot], sem.at[slot]).wait()
    y = jnp.concatenate(
        [sorted_ref[slot, pl.ds(s, rows, stride=ROW_TILES), :] for s in range(ROW_TILES)], axis=1).astype(BF16)
    acc = jnp.dot(weights.astype(BF16), y, preferred_element_type=F32)
    o_ref[...] = x_ref[...] + g2_ref[0] * acc


def _combine(tile_n, tile_carry, group_start, yg_rows, lp_t, gate_t, x, gate2, seq, tm):
    t, d = x.shape
    per_b = seq // tm
    grid_spec = pltpu.PrefetchScalarGridSpec(
        num_scalar_prefetch=3,
        grid=(t // tm,),
        in_specs=[
            pl.BlockSpec(memory_space=pl.ANY),
            pl.BlockSpec((TOP_K, tm), lambda i, *_: (0, i)),
            pl.BlockSpec((TOP_K, tm), lambda i, *_: (0, i)),
            pl.BlockSpec((tm, d), lambda i, *_: (i, 0)),
            pl.BlockSpec((1, 1, d), lambda i, *_: (i // per_b, 0, 0)),
        ],
        out_specs=pl.BlockSpec((tm, d), lambda i, *_: (i, 0)),
        scratch_shapes=[pltpu.VMEM((2, TOP_K * tm * ROW_TILES, LANES), F32), pltpu.SemaphoreType.DMA((2,))],
    )
    return pl.pallas_call(
        _combine_kernel,
        grid_spec=grid_spec,
        out_shape=jax.ShapeDtypeStruct((t, d), F32),
        compiler_params=_cparams("arbitrary"),
        name="moe_combine",
    )(tile_n, tile_carry, group_start, yg_rows, lp_t, gate_t, x, gate2)


def _moe(layer, x, g, shift, scale, gate2, r_w, r_b, w1_all, b1_glu, b1_lin, w2_all, b2, seq):
    t = x.shape[0]
    h, lp_t, gate_t, tile_n, tile_carry, counts = _router(x, g, shift, scale, r_w, r_b, seq, MOE_TILE)
    tile_n = tile_n.reshape(-1)
    tile_carry = tile_carry.reshape(-1)
    counts = counts.reshape(-1)
    padded = ((counts + MOE_BLOCK - 1) // MOE_BLOCK) * MOE_BLOCK
    group_end = jnp.cumsum(padded)
    group_start = (group_end - padded).astype(jnp.int32)
    cap = t * TOP_K + N_EXPERTS * MOE_BLOCK
    nblk = cap // MOE_BLOCK
    blk_start = jnp.arange(nblk, dtype=jnp.int32) * MOE_BLOCK
    e_ids = jnp.arange(N_EXPERTS, dtype=jnp.int32)
    past = (blk_start[None, :] >= group_end[:, None]).astype(jnp.int32)
    blk_e = jnp.minimum(jnp.sum(past, axis=0), N_EXPERTS - 1).astype(jnp.int32)
    nused = (group_end[-1:] // MOE_BLOCK).astype(jnp.int32)

    pad_lo = (group_start + counts).astype(jnp.int32)
    pad_hi = group_end.astype(jnp.int32)
    blk_pad_lo = jnp.sum(jnp.where(blk_e[None, :] == e_ids[:, None], pad_lo[:, None], 0), axis=0)
    blk_rows = jnp.clip(blk_pad_lo - blk_start, 0, MOE_BLOCK).astype(jnp.int32)
    xg_rows = _dispatch(tile_n, tile_carry, group_start, pad_lo, pad_hi, nused, lp_t, h, cap, MOE_TILE)
    nonempty = counts > 0
    later = jnp.where(nonempty[None, :] & (e_ids[None, :] > e_ids[:, None]), e_ids[None, :], N_EXPERTS)
    next_e = jnp.min(later, axis=1)
    next_e = jnp.where(next_e == N_EXPERTS, -1, next_e).astype(jnp.int32)
    e_slot = ((jnp.cumsum(nonempty.astype(jnp.int32)) - 1) % 2).astype(jnp.int32)
    yg_rows = _experts(layer, blk_e, nused, next_e, e_slot, blk_rows, xg_rows, w1_all, w2_all,
                       b1_glu[:, None, :], b1_lin[:, None, :], b2[:, None, :])
    return _combine(tile_n, tile_carry, group_start, yg_rows, lp_t, gate_t, x, gate2, seq, MOE_TILE)


def kernel(x, c, ada_w, ada_b, norm1_g, norm2_g, m_in_w, m_conv_w, m_conv_b, m_dt_bias, m_A_log, m_D, m_norm_g, m_out_w, a_qkv_w, a_q_norm_g, a_k_norm_g, a_sinks, a_out_w, rel_bias, r_w, r_b, e_w1, e_b1, e_w2, e_b2):
    batch, seq, d = x.shape
    depth = ada_w.shape[0]
    t = batch * seq
    xf = x.reshape(t, d)

    c_pad = jnp.zeros((SUBLANES, d), F32).at[:batch].set(c)
    mod = _adaln(c_pad, ada_w, ada_b)[:, :batch]

    b1_split = jnp.moveaxis(e_b1.reshape(depth, N_EXPERTS, -1, 2), -1, 1)

    for i in range(depth):
        parts =[mod[i, :, p * d:(p + 1) * d].reshape(batch, 1, d) for p in range(6)]
        sh1, sc1, g1, sh2, sc2, g2 = parts
        j = i // 2
        if i % 2 == 0:
            w_zxbc = m_in_w[j, :, :SSM_ZXBC].astype(BF16)
            w_dt = jnp.pad(m_in_w[j, :, SSM_ZXBC:], ((0, 0), (0, LANES - SSM_HEADS))).astype(BF16)
            zxbc, dt_raw = _norm_matmul(xf, norm1_g[i], sh1, sc1, (w_zxbc, w_dt), seq)
            y = _ssd_mixer(zxbc, dt_raw, m_conv_w[j], m_conv_b[j], m_dt_bias[j], m_A_log[j], m_D[j],
                           m_norm_g[j], batch, seq)
            xf = _matmul_residual(y, m_out_w[j].astype(BF16), xf, g1, seq)
        else:
            qkv, = _norm_matmul(xf, norm1_g[i], sh1, sc1, (a_qkv_w[j].astype(BF16),), seq)
            y = _swa_mixer(qkv, a_q_norm_g[j], a_k_norm_g[j], a_sinks[j], rel_bias, batch, seq)
            xf = _matmul_residual(y, a_out_w[j].astype(BF16), xf, g1, seq)
        xf = _moe(i, xf, norm2_g[i], sh2, sc2, g2, r_w[i], r_b[i], e_w1, b1_split[i, 0], b1_split[i, 1],
                  e_w2, e_b2[i], seq)
    return xf.reshape(batch, seq, d)
```

```python
import functools
import math

import jax
import jax.numpy as jnp
from jax import lax
from jax.experimental import pallas as pl
from jax.experimental.pallas import tpu as pltpu

D_MODEL = 1024
EPS = 1e-6
LANES = 128
SUBLANES = 8
ROW_TILES = D_MODEL // LANES

SSM_D_INNER = 2048
SSM_HEAD_DIM = 64
SSM_HEADS = 32
SSM_GROUPS = 4
SSM_STATE = 128
SSM_CONV = 4
SSM_CHUNK = 128
SSM_GN = SSM_GROUPS * SSM_STATE
SSM_ZXBC = 2 * SSM_D_INNER + 2 * SSM_GN
SSM_GROUP_W = SSM_D_INNER // SSM_GROUPS

ATTN_HEAD_DIM = 64
ATTN_Q_HEADS = 16
ATTN_KV_HEADS = 4
ATTN_Q_PER_KV = 4
WINDOW = 128
REL_BUCKETS = 32
REL_MAX_DIST = 128

N_EXPERTS = 32
TOP_K = 4
SWIGLU_ALPHA = 1.702
SWIGLU_LIMIT = 7.0
MOE_BLOCK = 512
MOE_SUB = 256
MOE_TILE = 256
SPLIT_W = 256

VMEM_LIMIT = 56 * 1024 * 1024
HI = lax.Precision.HIGHEST
F32 = jnp.float32
BF16 = jnp.bfloat16


def _cparams(*sem):
    return pltpu.CompilerParams(dimension_semantics=sem, vmem_limit_bytes=VMEM_LIMIT)


def _norm_modulate(x, g, shift, scale):
    ms = jnp.mean(x * x, axis=-1, keepdims=True)
    return x * lax.rsqrt(ms + EPS) * g * (1.0 + scale) + shift


def _adaln_kernel(c_ref, w_ref, b_ref, o_ref):
    c = c_ref[...]
    c_act = c * jax.nn.sigmoid(c)
    o_ref[0] = jnp.dot(c_act, w_ref[0], preferred_element_type=F32, precision=HI) + b_ref[0]


def _adaln(c_pad, ada_w, ada_b):
    depth, d, n = ada_w.shape
    tn = 1536
    return pl.pallas_call(
        _adaln_kernel,
        grid=(depth, n // tn),
        in_specs=[
            pl.BlockSpec((SUBLANES, d), lambda i, j: (0, 0)),
            pl.BlockSpec((1, d, tn), lambda i, j: (i, 0, j)),
            pl.BlockSpec((1, 1, tn), lambda i, j: (i, 0, j)),
        ],
        out_specs=pl.BlockSpec((1, SUBLANES, tn), lambda i, j: (i, 0, j)),
        out_shape=jax.ShapeDtypeStruct((depth, SUBLANES, n), F32),
        compiler_params=_cparams("arbitrary", "arbitrary"),
        name="adaln",
    )(c_pad, ada_w, ada_b.reshape(depth, 1, n))


def _norm_matmul_kernel(x_ref, g_ref, sh_ref, sc_ref, *refs):
    w_refs, o_refs = refs[:len(refs) // 2], refs[len(refs) // 2:]
    h = _norm_modulate(x_ref[...], g_ref[...], sh_ref[0], sc_ref[0]).astype(BF16)
    for w_ref, o_ref in zip(w_refs, o_refs):
        o_ref[...] = jnp.dot(h, w_ref[...], preferred_element_type=F32)


def _norm_matmul(x, g, shift, scale, weights_bf16, seq, tm=512):
    t, d = x.shape
    per_b = seq // tm
    return pl.pallas_call(
        _norm_matmul_kernel,
        grid=(t // tm,),
        in_specs=[
            pl.BlockSpec((tm, d), lambda i: (i, 0)),
            pl.BlockSpec((1, d), lambda i: (0, 0)),
            pl.BlockSpec((1, 1, d), lambda i: (i // per_b, 0, 0)),
            pl.BlockSpec((1, 1, d), lambda i: (i // per_b, 0, 0)),
        ] + [pl.BlockSpec(w.shape, lambda i: (0, 0), pipeline_mode=pl.Buffered(1))
             for w in weights_bf16],
        out_specs=[pl.BlockSpec((tm, w.shape[1]), lambda i: (i, 0)) for w in weights_bf16],
        out_shape=[jax.ShapeDtypeStruct((t, w.shape[1]), F32) for w in weights_bf16],
        compiler_params=_cparams("arbitrary"),
        name="norm_matmul",
    )(x, g.reshape(1, d), shift, scale, *weights_bf16)


def _matmul_residual_kernel(y_ref, w_ref, x_ref, gate_ref, o_ref):
    acc = jnp.dot(y_ref[...], w_ref[...], preferred_element_type=F32)
    o_ref[...] = x_ref[...] + gate_ref[0] * acc


def _matmul_residual(y_bf16, w_bf16, x, gate, seq, tm=512):
    t, k = y_bf16.shape
    d = x.shape[1]
    per_b = seq // tm
    return pl.pallas_call(
        _matmul_residual_kernel,
        grid=(t // tm,),
        in_specs=[
            pl.BlockSpec((tm, k), lambda i: (i, 0)),
            pl.BlockSpec((k, d), lambda i: (0, 0)),
            pl.BlockSpec((tm, d), lambda i: (i, 0)),
            pl.BlockSpec((1, 1, d), lambda i: (i // per_b, 0, 0)),
        ],
        out_specs=pl.BlockSpec((tm, d), lambda i: (i, 0)),
        out_shape=jax.ShapeDtypeStruct((t, d), F32),
        compiler_params=_cparams("arbitrary"),
        name="matmul_residual",
    )(y_bf16, w_bf16, x, gate)


def _causal_conv_silu(cur, prev_tail, w, b):
    rows = lax.broadcasted_iota(jnp.int32, (SUBLANES, cur.shape[1]), 0)
    acc = b + w[SSM_CONV - 1:SSM_CONV] * cur
    for d in range(1, SSM_CONV):
        rolled = pltpu.roll(cur, d, axis=0)
        top = jnp.where(rows < d, pltpu.roll(prev_tail, d, axis=0), rolled[0:SUBLANES])
        shifted = jnp.concatenate([top, rolled[SUBLANES:]], axis=0)
        acc = acc + w[SSM_CONV - 1 - d:SSM_CONV - d] * shifted
    return acc * jax.nn.sigmoid(acc)


def _split3(x):
    hi = x.astype(BF16)
    rem = x - hi.astype(F32)
    mid = rem.astype(BF16)
    lo = (rem - mid.astype(F32)).astype(BF16)
    return jnp.concatenate([hi, mid, lo], axis=1)


def _ssd_kernel(z_ref, xs_ref, bc_ref, dt_ref, cwx_ref, cbx_ref, cwb_ref, cbb_ref, dtb_ref, alog_ref,
                dskip_ref, ng_ref, hexp_ref, lexp_ref, o_ref, tailx_ref, tailb_ref, state_ref):
    c = pl.program_id(1)

    @pl.when(c == 0)
    def _():
        tailx_ref[...] = jnp.zeros_like(tailx_ref)
        tailb_ref[...] = jnp.zeros_like(tailb_ref)
        state_ref[...] = jnp.zeros_like(state_ref)

    xs_raw = xs_ref[...]
    bc_raw = bc_ref[...]
    xs = _causal_conv_silu(xs_raw, tailx_ref[...], cwx_ref[...], cbx_ref[...])
    bc = _causal_conv_silu(bc_raw, tailb_ref[...], cwb_ref[...], cbb_ref[...])
    tailx_ref[...] = xs_raw[SSM_CHUNK - SUBLANES:]
    tailb_ref[...] = bc_raw[SSM_CHUNK - SUBLANES:]

    dt_in = dt_ref[...][:, :SSM_HEADS] + dtb_ref[...]
    dt = jnp.maximum(dt_in, 0.0) + jnp.log1p(jnp.exp(-jnp.abs(dt_in)))
    a_neg = -jnp.exp(alog_ref[...])
    d_a = dt * a_neg
    li = lax.broadcasted_iota(jnp.int32, (SSM_CHUNK, SSM_CHUNK), 0)
    si = lax.broadcasted_iota(jnp.int32, (SSM_CHUNK, SSM_CHUNK), 1)
    causal = li >= si
    tri = causal.astype(F32)
    a_cs = jnp.dot(tri, d_a, preferred_element_type=F32, precision=HI)
    a_cs_t = lax.dot_general(d_a, tri, (((0,), (1,)), ((), ())),
                             preferred_element_type=F32, precision=HI)
    a_last = a_cs[SSM_CHUNK - 1:SSM_CHUNK]
    e_out = jnp.exp(a_cs)
    e_state = jnp.exp(a_last - a_cs) * dt
    small = jnp.concatenate([dt, e_out, e_state], axis=0)
    wide = jnp.dot(_split3(small), hexp_ref[...], preferred_element_type=F32)
    dt_w = wide[0:SSM_CHUNK]
    e_out_w = wide[SSM_CHUNK:2 * SSM_CHUNK]
    e_state_w = wide[2 * SSM_CHUNK:]
    a_col = jnp.dot(_split3(a_cs), lexp_ref[...], preferred_element_type=F32)

    x_dt = (xs * dt_w).astype(BF16)
    x_state = (xs * e_state_w).astype(BF16)
    chunk_decay_w = e_out_w[SSM_CHUNK - 1:SSM_CHUNK]

    heads_per_group = SSM_HEADS // SSM_GROUPS
    y_parts = []
    for g in range(SSM_GROUPS):
        b_g = bc[:, g * SSM_STATE:(g + 1) * SSM_STATE].astype(BF16)
        c_g = bc[:, SSM_GN + g * SSM_STATE:SSM_GN + (g + 1) * SSM_STATE].astype(BF16)
        cb = lax.dot_general(c_g, b_g, (((1,), (1,)), ((), ())), preferred_element_type=F32)
        gsl = slice(g * SSM_GROUP_W, (g + 1) * SSM_GROUP_W)
        h_prev = state_ref[g]
        y_off = jnp.dot(c_g, h_prev.astype(BF16), preferred_element_type=F32) * e_out_w[:, gsl]
        diag = []
        for r in range(heads_per_group):
            h = g * heads_per_group + r
            seg = a_col[:, h * SSM_CHUNK:(h + 1) * SSM_CHUNK] - a_cs_t[h:h + 1, :]
            decay = jnp.exp(jnp.where(causal, seg, -jnp.inf))
            m = (cb * decay).astype(BF16)
            diag.append(jnp.dot(m, x_dt[:, h * SSM_HEAD_DIM:(h + 1) * SSM_HEAD_DIM],
                                preferred_element_type=F32))
        y_parts.append(jnp.concatenate(diag, axis=1) + y_off)
        upd = lax.dot_general(b_g, x_state[:, gsl], (((0,), (0,)), ((), ())), preferred_element_type=F32)
        state_ref[g] = h_prev * chunk_decay_w[:, gsl] + upd

    y = jnp.concatenate(y_parts, axis=1) + dskip_ref[...] * xs
    z = z_ref[...]
    y = y * (z * jax.nn.sigmoid(z))
    normed = []
    for g in range(SSM_GROUPS):
        y_g = y[:, g * SSM_GROUP_W:(g + 1) * SSM_GROUP_W]
        normed.append(y_g * lax.rsqrt(jnp.mean(y_g * y_g, axis=-1, keepdims=True) + EPS))
    o_ref[...] = (jnp.concatenate(normed, axis=1) * ng_ref[...]).astype(o_ref.dtype)


def _ssd_mixer(zxbc, dt_raw, conv_w, conv_b, dt_bias, a_log, d_skip, norm_g, batch, seq):
    t = zxbc.shape[0]
    nc = seq // SSM_CHUNK
    head_expand = jnp.tile(jnp.repeat(jnp.eye(SSM_HEADS, dtype=BF16), SSM_HEAD_DIM, axis=1), (3, 1))
    lane_expand = jnp.tile(jnp.repeat(jnp.eye(SSM_HEADS, dtype=BF16), SSM_CHUNK, axis=1), (3, 1))
    row = lambda b, c: (b * nc + c, 0)
    const2 = lambda b, c: (0, 0)
    bc_w = 2 * SSM_GN
    return pl.pallas_call(
        _ssd_kernel,
        grid=(batch, nc),
        in_specs=[
            pl.BlockSpec((SSM_CHUNK, SSM_D_INNER), row),
            pl.BlockSpec((SSM_CHUNK, SSM_D_INNER), lambda b, c: (b * nc + c, 1)),
            pl.BlockSpec((SSM_CHUNK, bc_w), lambda b, c: (b * nc + c, 2 * SSM_D_INNER // bc_w)),
            pl.BlockSpec((SSM_CHUNK, LANES), row),
            pl.BlockSpec((SSM_CONV, SSM_D_INNER), const2),
            pl.BlockSpec((1, SSM_D_INNER), const2),
            pl.BlockSpec((SSM_CONV, bc_w), const2),
            pl.BlockSpec((1, bc_w), const2),
            pl.BlockSpec((1, SSM_HEADS), const2),
            pl.BlockSpec((1, SSM_HEADS), const2),
            pl.BlockSpec((1, SSM_D_INNER), const2),
            pl.BlockSpec((1, SSM_D_INNER), const2),
            pl.BlockSpec((3 * SSM_HEADS, SSM_D_INNER), const2),
            pl.BlockSpec((3 * SSM_HEADS, SSM_HEADS * SSM_CHUNK), const2),
        ],
        out_specs=pl.BlockSpec((SSM_CHUNK, SSM_D_INNER), row),
        out_shape=jax.ShapeDtypeStruct((t, SSM_D_INNER), BF16),
        scratch_shapes=[
            pltpu.VMEM((SUBLANES, SSM_D_INNER), F32),
            pltpu.VMEM((SUBLANES, bc_w), F32),
            pltpu.VMEM((SSM_GROUPS, SSM_STATE, SSM_GROUP_W), F32),
        ],
        compiler_params=_cparams("arbitrary", "arbitrary"),
        name="ssd_mixer",
    )(zxbc, zxbc, zxbc, dt_raw,
      conv_w[:, :SSM_D_INNER], conv_b[:SSM_D_INNER].reshape(1, -1),
      conv_w[:, SSM_D_INNER:], conv_b[SSM_D_INNER:].reshape(1, -1),
      dt_bias.reshape(1, -1), a_log.reshape(1, -1),
      jnp.repeat(d_skip, SSM_HEAD_DIM).reshape(1, -1), norm_g.reshape(1, -1),
      head_expand, lane_expand)


def _head_rms(x, g):
    return x * lax.rsqrt(jnp.mean(x * x, axis=-1, keepdims=True) + EPS) * g


def _swa_kernel(q_ref, kvc_ref, kvp_ref, bucket_ref, qg_ref, kg_ref, rel_ref, sink_ref, o_ref,
                bias_ref, sinkrow_ref):
    b = pl.program_id(0)
    i = pl.program_id(1)

    @pl.when((b == 0) & (i == 0))
    def _():
        bucket = bucket_ref[...]
        kj = lax.broadcasted_iota(jnp.int32, (2 * WINDOW, WINDOW), 0)
        qi = lax.broadcasted_iota(jnp.int32, (2 * WINDOW, WINDOW), 1)
        dist = qi + WINDOW - kj
        band = (dist >= 0) & (dist < WINDOW)
        for h in range(ATTN_Q_HEADS):
            g, r = divmod(h, ATTN_Q_PER_KV)
            acc = jnp.zeros(bucket.shape, F32)
            for k in range(REL_BUCKETS):
                acc = jnp.where(bucket == k, rel_ref[k, h], acc)
            cols = slice(r * WINDOW, (r + 1) * WINDOW)
            bias_ref[0, g, :, cols] = jnp.where(band, acc, -jnp.inf)
            bias_ref[1, g, :, cols] = jnp.where(band & (kj >= WINDOW), acc, -jnp.inf)
            sinkrow_ref[g, :, cols] = jnp.full((1, WINDOW), sink_ref[h], F32)

    first = (i == 0).astype(jnp.int32)
    q_t = q_ref[...].T
    kv_c = kvc_ref[...]
    kv_p = kvp_ref[...]
    kv_w = ATTN_KV_HEADS * ATTN_HEAD_DIM
    q_gain = qg_ref[...]
    outs = []
    for g in range(ATTN_KV_HEADS):
        ksl = slice(g * ATTN_HEAD_DIM, (g + 1) * ATTN_HEAD_DIM)
        vsl = slice(kv_w + g * ATTN_HEAD_DIM, kv_w + (g + 1) * ATTN_HEAD_DIM)
        k = jnp.concatenate([kv_p[:, ksl], kv_c[:, ksl]], axis=0)
        v = jnp.concatenate([kv_p[:, vsl], kv_c[:, vsl]], axis=0).astype(BF16)
        k = _head_rms(k, kg_ref[...]).astype(BF16)
        q_heads = []
        for r in range(ATTN_Q_PER_KV):
            h = g * ATTN_Q_PER_KV + r
            q_h = q_t[h * ATTN_HEAD_DIM:(h + 1) * ATTN_HEAD_DIM]
            inv = lax.rsqrt(jnp.mean(q_h * q_h, axis=0, keepdims=True) + EPS)
            q_heads.append((q_h * inv * q_gain).astype(BF16))
        q = jnp.concatenate(q_heads, axis=1)
        s = jnp.dot(k, q, preferred_element_type=F32) + bias_ref[first, g]
        sink = sinkrow_ref[g]
        m = jnp.maximum(jnp.max(s, axis=0, keepdims=True), sink)
        p = jnp.exp(s - m)
        denom = jnp.sum(p, axis=0, keepdims=True) + jnp.exp(sink - m)
        pv = lax.dot_general(v, p.astype(BF16), (((0,), (0,)), ((), ())), preferred_element_type=F32)
        pv = pv * (1.0 / denom)
        outs.extend(pv[:, r * WINDOW:(r + 1) * WINDOW] for r in range(ATTN_Q_PER_KV))
    o_ref[...] = jnp.concatenate(outs, axis=0).T.astype(o_ref.dtype)


def _t5_causal_bucket(dist):
    max_exact = REL_BUCKETS // 2
    d = jnp.maximum(dist, 1).astype(F32)
    large = max_exact + (jnp.log(d / max_exact) / math.log(REL_MAX_DIST / max_exact)
                         * (REL_BUCKETS - max_exact)).astype(jnp.int32)
    large = jnp.minimum(large, REL_BUCKETS - 1)
    return jnp.where(dist < max_exact, dist, large)


def _swa_mixer(qkv, q_norm_g, k_norm_g, sinks, rel_bias, batch, seq):
    t = qkv.shape[0]
    nb = seq // WINDOW
    q_w = ATTN_Q_HEADS * ATTN_HEAD_DIM
    kv_w2 = 2 * ATTN_KV_HEADS * ATTN_HEAD_DIM
    kj = jnp.arange(2 * WINDOW)[:, None]
    qi = jnp.arange(WINDOW)[None, :]
    bucket = _t5_causal_bucket(jnp.maximum(qi + WINDOW - kj, 0)).astype(jnp.int32)
    q_gain = jnp.broadcast_to((q_norm_g * (ATTN_HEAD_DIM ** -0.5))[:, None], (ATTN_HEAD_DIM, WINDOW))
    const2 = lambda b, i: (0, 0)
    smem = pl.BlockSpec(memory_space=pltpu.SMEM)
    return pl.pallas_call(
        _swa_kernel,
        grid=(batch, nb),
        in_specs=[
            pl.BlockSpec((WINDOW, q_w), lambda b, i: (b * nb + i, 0)),
            pl.BlockSpec((WINDOW, kv_w2), lambda b, i: (b * nb + i, q_w // kv_w2)),
            pl.BlockSpec((WINDOW, kv_w2), lambda b, i: (b * nb + jnp.maximum(i - 1, 0), q_w // kv_w2)),
            pl.BlockSpec((2 * WINDOW, WINDOW), const2),
            pl.BlockSpec((ATTN_HEAD_DIM, WINDOW), const2),
            pl.BlockSpec((1, ATTN_HEAD_DIM), const2),
            smem,
            smem,
        ],
        out_specs=pl.BlockSpec((WINDOW, q_w), lambda b, i: (b * nb + i, 0)),
        out_shape=jax.ShapeDtypeStruct((t, q_w), BF16),
        scratch_shapes=[pltpu.VMEM((2, ATTN_KV_HEADS, 2 * WINDOW, ATTN_Q_PER_KV * WINDOW), F32),
                        pltpu.VMEM((ATTN_KV_HEADS, 1, ATTN_Q_PER_KV * WINDOW), F32)],
        compiler_params=_cparams("arbitrary", "arbitrary"),
        name="swa_mixer",
    )(qkv, qkv, qkv, bucket, q_gain, k_norm_g.reshape(1, -1), rel_bias, sinks)


def _router_kernel(x_ref, g_ref, sh_ref, sc_ref, rwt_ref, rb_ref,
                   h_ref, lp_ref, gate_ref, tile_n_ref, tile_carry_ref, cnt_ref, carry_ref):
    i = pl.program_id(0)
    tm = x_ref.shape[0]

    @pl.when(i == 0)
    def _():
        carry_ref[...] = jnp.zeros_like(carry_ref)

    h = _norm_modulate(x_ref[...], g_ref[...], sh_ref[0], sc_ref[0])
    h_ref[...] = h.astype(h_ref.dtype)

    logits = lax.dot_general(rwt_ref[...], h, (((1,), (1,)), ((), ())),
                             preferred_element_type=F32, precision=HI) + rb_ref[...]
    e_iota = lax.broadcasted_iota(jnp.int32, logits.shape, 0)
    work = logits
    sels, vals = [], []
    for k in range(TOP_K):
        m = jnp.max(work, axis=0, keepdims=True)
        idx = jnp.min(jnp.where(work == m, e_iota, N_EXPERTS), axis=0, keepdims=True)
        sel = e_iota == idx
        work = jnp.where(sel, -jnp.inf, work)
        sels.append(sel)
        vals.append(m)
    exps = [jnp.exp(v - vals[0]) for v in vals]
    denom = exps[0] + exps[1] + exps[2] + exps[3]
    for k in range(TOP_K):
        gate_ref[k:k + 1, :] = exps[k] / denom

    chosen = sels[0] | sels[1] | sels[2] | sels[3]
    t_row = lax.broadcasted_iota(jnp.int32, (tm, tm), 0)
    t_col = lax.broadcasted_iota(jnp.int32, (tm, tm), 1)
    before = (t_row < t_col).astype(BF16)
    prior = jnp.dot(chosen.astype(BF16), before, preferred_element_type=F32)
    n = jnp.sum(chosen.astype(F32), axis=1, keepdims=True)
    e_row = lax.broadcasted_iota(jnp.int32, (N_EXPERTS, N_EXPERTS), 0)
    e_col = lax.broadcasted_iota(jnp.int32, (N_EXPERTS, N_EXPERTS), 1)
    run_start = jnp.dot((e_col < e_row).astype(F32), jnp.broadcast_to(n, (N_EXPERTS, LANES)),
                        preferred_element_type=F32, precision=HI)[:, :1]
    local = run_start + prior
    for k in range(TOP_K):
        lp_ref[k:k + 1, :] = jnp.sum(jnp.where(sels[k], local, 0.0), axis=0, keepdims=True).astype(jnp.int32)
    tile_n_ref[0] = n.astype(jnp.int32)
    tile_carry_ref[0] = carry_ref[...].astype(jnp.int32)
    total = carry_ref[...] + n
    carry_ref[...] = total
    cnt_ref[...] = total.astype(jnp.int32)


def _router(x, g, shift, scale, r_w, r_b, seq, tm):
    t, d = x.shape
    per_b = seq // tm
    ntiles = t // tm
    return pl.pallas_call(
        _router_kernel,
        grid=(t // tm,),
        in_specs=[
            pl.BlockSpec((tm, d), lambda i: (i, 0)),
            pl.BlockSpec((1, d), lambda i: (0, 0)),
            pl.BlockSpec((1, 1, d), lambda i: (i // per_b, 0, 0)),
            pl.BlockSpec((1, 1, d), lambda i: (i // per_b, 0, 0)),
            pl.BlockSpec((N_EXPERTS, d), lambda i: (0, 0)),
            pl.BlockSpec((N_EXPERTS, 1), lambda i: (0, 0)),
        ],
        out_specs=[
            pl.BlockSpec((tm, d), lambda i: (i, 0)),
            pl.BlockSpec((TOP_K, tm), lambda i: (0, i)),
            pl.BlockSpec((TOP_K, tm), lambda i: (0, i)),
            pl.BlockSpec((1, N_EXPERTS, 1), lambda i: (i, 0, 0)),
            pl.BlockSpec((1, N_EXPERTS, 1), lambda i: (i, 0, 0)),
            pl.BlockSpec((N_EXPERTS, 1), lambda i: (0, 0)),
        ],
        out_shape=[
            jax.ShapeDtypeStruct((t, d), BF16),
            jax.ShapeDtypeStruct((TOP_K, t), jnp.int32),
            jax.ShapeDtypeStruct((TOP_K, t), F32),
            jax.ShapeDtypeStruct((ntiles, N_EXPERTS, 1), jnp.int32),
            jax.ShapeDtypeStruct((ntiles, N_EXPERTS, 1), jnp.int32),
            jax.ShapeDtypeStruct((N_EXPERTS, 1), jnp.int32),
        ],
        scratch_shapes=[pltpu.VMEM((N_EXPERTS, 1), F32)],
        compiler_params=_cparams("arbitrary"),
        name="moe_router",
    )(x, g.reshape(1, d), shift, scale, r_w.T, r_b.reshape(-1, 1))


def _run_copies(n, src_ref, src_row, dst_ref, dst_row, sem, max_rows, fn):
    for b in range(max_rows.bit_length()):
        size = 1 << b

        @pl.when((n & size) != 0)
        def _():
            lo = n & (size - 1)
            fn(pltpu.make_async_copy(
                src_ref.at[pl.ds(pl.multiple_of((src_row + lo) * ROW_TILES, ROW_TILES), size * ROW_TILES)],
                dst_ref.at[pl.ds(pl.multiple_of((dst_row + lo) * ROW_TILES, ROW_TILES), size * ROW_TILES)],
                sem), b)


def _start_piece(cp, b):
    cp.start()


def _wait_piece(cp, b):
    cp.wait()


def _tile_runs(tile, n_ref, carry_ref, gstart_ref, sorted_ref, grouped_ref, sem, tm, fn, to_grouped, live=None):
    def body(e, run_start):
        n = n_ref[tile * N_EXPERTS + e]
        if live is not None:
            n = jnp.where(live, n, 0)
        slot0 = gstart_ref[e] + carry_ref[tile * N_EXPERTS + e]
        if to_grouped:
            _run_copies(n, sorted_ref, run_start, grouped_ref, slot0, sem, tm, fn)
        else:
            _run_copies(n, grouped_ref, slot0, sorted_ref, run_start, sem, tm, fn)
        return run_start + n

    if live is None:
        lax.fori_loop(0, N_EXPERTS, body, 0)
    else:
        run_start = 0
        for e in range(N_EXPERTS):
            run_start = body(e, run_start)


def _block_copy(src_ref, dst_ref, dst_blk, sem):
    rows = MOE_BLOCK * ROW_TILES
    return pltpu.make_async_copy(src_ref, dst_ref.at[pl.ds(pl.multiple_of(dst_blk * rows, rows), rows)], sem)


def _dispatch_kernel(n_ref, carry_ref, gstart_ref, pad_lo_ref, pad_hi_ref, nused_ref, lp_ref, h_ref, xg_ref,
                     sorted0_ref, sorted1_ref, sorted2_ref, zero_ref, run_sem, sem):
    i = pl.program_id(0)
    last = pl.num_programs(0) - 1
    tm = h_ref.shape[0]
    rows = TOP_K * tm
    nblk =xg_ref.shape[0] // (MOE_BLOCK * ROW_TILES)

    @pl.when(i == 0)
    def _():
        zero_ref[...] = jnp.zeros_like(zero_ref)

        def fill(b, carry):
            _block_copy(zero_ref, xg_ref, b, sem).start()
            return carry

        def fill_wait(b, carry):
            _block_copy(zero_ref, xg_ref, b, sem).wait()
            return carry

        lax.fori_loop(nused_ref[0], nblk, fill, 0)
        lax.fori_loop(nused_ref[0], nblk, fill_wait, 0)

    @pl.when(i < N_EXPERTS)
    def _():
        n_pad = pad_hi_ref[i] - pad_lo_ref[i]
        for fn in (_start_piece, _wait_piece):
            _run_copies(n_pad, zero_ref, 0, xg_ref, pad_lo_ref[i], sem, MOE_BLOCK - 1, fn)

    bufs = (sorted0_ref, sorted1_ref, sorted2_ref)
    nbuf = len(bufs)

    def runs(tile, b, fn, live=None):
        _tile_runs(tile, n_ref, carry_ref, gstart_ref, bufs[b], xg_ref, run_sem.at[b], tm, fn, True, live)

    def wait_tile(b):
        pltpu.make_async_copy(bufs[b], xg_ref.at[pl.ds(0, rows * ROW_TILES)], run_sem.at[b]).wait()

    def step(cur):
        prev, oldest = (cur - 1) % nbuf, (cur + 1) % nbuf
        cur_ref = bufs[cur]

        @pl.when(i >= nbuf)
        def _():
            wait_tile(cur)

        row_id = lax.broadcasted_iota(jnp.int32, (rows, tm), 0)
        perm = row_id == lp_ref[0:1, :]
        for k in range(1, TOP_K):
            perm = perm | (row_id == lp_ref[k:k + 1, :])
        srt = jnp.dot(jnp.where(perm, 1.0, 0.0).astype(BF16), h_ref[...], preferred_element_type=F32)
        for s in range(ROW_TILES):
            cur_ref[pl.ds(s, rows, stride=ROW_TILES), :] = srt[:, s * LANES:(s + 1) * LANES]

        runs(jnp.maximum(i - 1, 0), prev, _start_piece, live=i >= 1)

        @pl.when(i == last)
        def _():
            @pl.when(i >= 2)
            def _():
                wait_tile(oldest)

            @pl.when(i >= 1)
            def _():
                wait_tile(prev)

            runs(i, cur, _start_piece)
            wait_tile(cur)

    for r in range(nbuf):
        @pl.when(i % nbuf == r)
        def _():
            step(r)


def _dispatch(tile_n, tile_carry, group_start, pad_lo, pad_hi, nused, lp_t, h, cap, tm):
    t, d = h.shape
    assert t // tm >= N_EXPERTS
    grid_spec = pltpu.PrefetchScalarGridSpec(
        num_scalar_prefetch=6,
        grid=(t // tm,),
        in_specs=[
            pl.BlockSpec((TOP_K, tm), lambda i, *_: (0, i)),
            pl.BlockSpec((tm, d), lambda i, *_: (i, 0)),
        ],
        out_specs=pl.BlockSpec(memory_space=pl.ANY),
        scratch_shapes=[
            pltpu.VMEM((TOP_K * tm * ROW_TILES, LANES), F32),
            pltpu.VMEM((TOP_K * tm * ROW_TILES, LANES), F32),
            pltpu.VMEM((TOP_K * tm * ROW_TILES, LANES), F32),
            pltpu.VMEM((MOE_BLOCK * ROW_TILES, LANES), F32),
            pltpu.SemaphoreType.DMA((3,)), pltpu.SemaphoreType.DMA,
        ],
    )
    return pl.pallas_call(
        _dispatch_kernel,
        grid_spec=grid_spec,
        out_shape=jax.ShapeDtypeStruct((cap * ROW_TILES, LANES), F32),
        compiler_params=pltpu.CompilerParams(dimension_semantics=("arbitrary",), has_side_effects=True,
                                             vmem_limit_bytes=VMEM_LIMIT),
        name="moe_dispatch",
    )(tile_n, tile_carry, group_start, pad_lo, pad_hi, nused, lp_t, h)


def _weight_copies(w1_hbm, w2_hbm, w1buf, w2buf, sems, layer, e, slot):
    return (pltpu.make_async_copy(w1_hbm.at[layer, e], w1buf.at[slot], sems.at[0, slot]),
            pltpu.make_async_copy(w2_hbm.at[layer, e], w2buf.at[slot], sems.at[1, slot]))


def _expert_kernel(blk_e_ref, nused_ref, next_e_ref, slot_ref, blk_rows_ref, x_ref, w1_hbm, w2_hbm, bg_ref, bl_ref, b2_ref,
                   sel_ref, o_ref, w1buf, w2buf, wg_s, wl_s, w2_s, sems, *, layer):
    blk = pl.program_id(0)
    active = blk < nused_ref[0]
    e = blk_e_ref[blk]
    slot = slot_ref[e]
    new_expert = (blk == 0) | (e != blk_e_ref[jnp.maximum(blk - 1, 0)])
    copies = functools.partial(_weight_copies, w1_hbm, w2_hbm, w1buf, w2buf, sems, layer)

    @pl.when(blk == 0)
    def _():
        for cp in copies(e, slot):
            cp.start()

    @pl.when(active & new_expert)
    def _():
        nxt = next_e_ref[e]

        @pl.when(nxt >= 0)
        def _():
            for cp in copies(nxt, 1 - slot):
                cp.start()

        for cp in copies(e, slot):
            cp.wait()
        half = SPLIT_W // 2
        for j in range(w1buf.shape[2] // SPLIT_W):
            chunk = w1buf[slot, :, j * SPLIT_W:(j + 1) * SPLIT_W].astype(BF16)
            split = jnp.dot(chunk, sel_ref[...], preferred_element_type=F32)
            wg_s[:, j * half:(j + 1) * half] = split[:, :half].astype(BF16)
            wl_s[:, j * half:(j + 1) * half] = split[:, half:].astype(BF16)
        w2_s[...] = w2buf[slot].astype(BF16)

    def mlp(rows):
        x = jnp.concatenate([x_ref[pl.ds(s, rows, stride=ROW_TILES), :]
                             for s in range(ROW_TILES)], axis=1).astype(BF16)
        glu = jnp.dot(x, wg_s[...], preferred_element_type=F32) + bg_ref[0]
        lin = jnp.dot(x, wl_s[...], preferred_element_type=F32) + bl_ref[0]
        glu = jnp.minimum(glu, SWIGLU_LIMIT)
        lin = jnp.clip(lin, -SWIGLU_LIMIT, SWIGLU_LIMIT)
        act = glu * jax.nn.sigmoid(SWIGLU_ALPHA * glu) * (lin + 1.0)
        y = jnp.dot(act.astype(BF16), w2_s[...], preferred_element_type=F32) + b2_ref[0]
        for s in range(ROW_TILES):
            o_ref[pl.ds(s, rows, stride=ROW_TILES), :] = y[:, s * LANES:(s + 1) * LANES]
        if rows < MOE_BLOCK:
            o_ref[pl.ds(rows * ROW_TILES, (MOE_BLOCK - rows) * ROW_TILES), :] = jnp.zeros(
                ((MOE_BLOCK - rows) * ROW_TILES, LANES), o_ref.dtype)

    valid_rows = jnp.where(active, blk_rows_ref[blk], 0)

    @pl.when(valid_rows > MOE_SUB)
    def _():
        mlp(MOE_BLOCK)

    @pl.when((valid_rows > 0) & (valid_rows <= MOE_SUB))
    def _():
        mlp(MOE_SUB)

    @pl.when(valid_rows <= 0)
    def _():
        o_ref[...] = jnp.zeros_like(o_ref)


def _experts(layer, blk_e, nused, next_e, e_slot, blk_rows, xg_rows, w1_all, w2_all, b_glu, b_lin, b2):
    nblk = blk_e.shape[0]
    d = D_MODEL
    ff = w2_all.shape[2]
    col = jnp.arange(SPLIT_W)
    sel = (jnp.arange(SPLIT_W)[None, :] == ((col % 2) * (SPLIT_W // 2) + col // 2)[:, None]).astype(BF16)

    def blk_map(b, blk_e_ref, nused_ref, *_):
        return (jnp.minimum(b, nused_ref[0] - 1), 0)

    def e_map(b, blk_e_ref, nused_ref, *_):
        return (blk_e_ref[jnp.minimum(b, nused_ref[0] - 1)], 0, 0)

    grid_spec = pltpu.PrefetchScalarGridSpec(
        num_scalar_prefetch=5,
        grid=(nblk,),
        in_specs=[
            pl.BlockSpec((MOE_BLOCK * ROW_TILES, LANES), blk_map),
            pl.BlockSpec(memory_space=pl.ANY),
            pl.BlockSpec(memory_space=pl.ANY),
            pl.BlockSpec((1, 1, ff), e_map),
            pl.BlockSpec((1, 1, ff), e_map),
            pl.BlockSpec((1, 1, d), e_map),
            pl.BlockSpec((SPLIT_W, SPLIT_W), lambda b, *_: (0, 0)),
        ],
        out_specs=pl.BlockSpec((MOE_BLOCK * ROW_TILES, LANES), lambda b, *_: (b, 0)),
        scratch_shapes=[
            pltpu.VMEM((2, d, 2 * ff), F32), pltpu.VMEM((2, ff, d), F32),
            pltpu.VMEM((d, ff), BF16), pltpu.VMEM((d, ff), BF16), pltpu.VMEM((ff, d), BF16),
            pltpu.SemaphoreType.DMA((2, 2)),
        ],
    )
    return pl.pallas_call(
        functools.partial(_expert_kernel, layer=layer),
        grid_spec=grid_spec,
        out_shape=jax.ShapeDtypeStruct(xg_rows.shape, F32),
        compiler_params=_cparams("arbitrary"),
        name="moe_experts",
    )(blk_e, nused, next_e, e_slot, blk_rows, xg_rows, w1_all, w2_all, b_glu, b_lin, b2, sel)


def _combine_kernel(n_ref, carry_ref, gstart_ref, yg_ref, lp_ref, gate_ref, x_ref, g2_ref, o_ref, sorted_ref, sem):
    i = pl.program_id(0)
    ntiles = pl.num_programs(0)
    tm = x_ref.shape[0]
    rows = TOP_K * tm
    slot = i % 2

    def runs(tile, buf, fn, live=None):
        _tile_runs(tile, n_ref, carry_ref, gstart_ref, sorted_ref.at[buf], yg_ref, sem.at[buf], tm, fn, False, live)

    @pl.when(i == 0)
    def _():
        runs(0, 0, _start_piece)

    runs(jnp.minimum(i + 1, ntiles - 1), 1 - slot, _start_piece, live=i + 1 < ntiles)

    eye = (lax.broadcasted_iota(jnp.int32, (tm, tm), 0) ==
           lax.broadcasted_iota(jnp.int32, (tm, tm), 1)).astype(F32)
    rows_t = jnp.concatenate([gate_ref[...], lp_ref[...].astype(F32)], axis=0)
    cols = lax.dot_general(eye, rows_t, (((1,), (1,)), ((), ())), preferred_element_type=F32, precision=HI)
    col_id = lax.broadcasted_iota(jnp.int32, (tm, rows), 1)
    weights = jnp.zeros((tm, rows), F32)
    for k in range(TOP_K):
        lp_col = (cols[:, TOP_K + k:TOP_K + k + 1] + 0.5).astype(jnp.int32)
        weights = weights + jnp.where(col_id == lp_col, cols[:, k:k + 1], 0.0)

    pltpu.make_async_copy(yg_ref.at[pl.ds(0, rows * ROW_TILES)], sorted_ref.at[slot], sem.at[slot]).wait()
    y = jnp.concatenate(
        [sorted_ref[slot, pl.ds(s, rows, stride=ROW_TILES), :] for s in range(ROW_TILES)], axis=1).astype(BF16)
    acc = jnp.dot(weights.astype(BF16), y, preferred_element_type=F32)
    o_ref[...] = x_ref[...] + g2_ref[0] * acc


def _combine(tile_n, tile_carry, group_start, yg_rows, lp_t, gate_t, x, gate2, seq, tm):
    t, d = x.shape
    per_b = seq // tm
    grid_spec = pltpu.PrefetchScalarGridSpec(
        num_scalar_prefetch=3,
        grid=(t // tm,),
        in_specs=[
            pl.BlockSpec(memory_space=pl.ANY),
            pl.BlockSpec((TOP_K, tm), lambda i, *_: (0, i)),
            pl.BlockSpec((TOP_K, tm), lambda i, *_: (0, i)),
            pl.BlockSpec((tm, d), lambda i, *_: (i, 0)),
            pl.BlockSpec((1, 1, d), lambda i, *_: (i // per_b, 0, 0)),
        ],
        out_specs=pl.BlockSpec((tm, d), lambda i, *_: (i, 0)),
        scratch_shapes=[pltpu.VMEM((2, TOP_K * tm * ROW_TILES, LANES), F32), pltpu.SemaphoreType.DMA((2,))],
    )
    return pl.pallas_call(
        _combine_kernel,
        grid_spec=grid_spec,
        out_shape=jax.ShapeDtypeStruct((t, d), F32),
        compiler_params=_cparams("arbitrary"),
        name="moe_combine",
    )(tile_n, tile_carry, group_start, yg_rows, lp_t, gate_t, x, gate2)


def _moe(layer, x, g, shift, scale, gate2, r_w, r_b, w1_all, b1_glu, b1_lin, w2_all, b2, seq):
    t = x.shape[0]
    h, lp_t, gate_t, tile_n, tile_carry, counts = _router(x, g, shift, scale, r_w, r_b, seq, MOE_TILE)
    tile_n = tile_n.reshape(-1)
    tile_carry = tile_carry.reshape(-1)
    counts = counts.reshape(-1)
    padded = ((counts + MOE_BLOCK - 1) // MOE_BLOCK) * MOE_BLOCK
    group_end = jnp.cumsum(padded)
    group_start = (group_end - padded).astype(jnp.int32)
    cap = t * TOP_K + N_EXPERTS * MOE_BLOCK
    nblk = cap // MOE_BLOCK
    blk_start = jnp.arange(nblk, dtype=jnp.int32) * MOE_BLOCK
    e_ids = jnp.arange(N_EXPERTS, dtype=jnp.int32)
    past = (blk_start[None, :] >= group_end[:, None]).astype(jnp.int32)
    blk_e = jnp.minimum(jnp.sum(past, axis=0), N_EXPERTS - 1).astype(jnp.int32)
    nused = (group_end[-1:] // MOE_BLOCK).astype(jnp.int32)

    pad_lo = (group_start + counts).astype(jnp.int32)
    pad_hi = group_end.astype(jnp.int32)
    blk_pad_lo = jnp.sum(jnp.where(blk_e[None, :] == e_ids[:, None], pad_lo[:, None], 0), axis=0)
    blk_rows = jnp.clip(blk_pad_lo - blk_start, 0, MOE_BLOCK).astype(jnp.int32)
    xg_rows = _dispatch(tile_n, tile_carry, group_start, pad_lo, pad_hi, nused, lp_t, h, cap, MOE_TILE)
    nonempty = counts > 0
    later = jnp.where(nonempty[None, :] & (e_ids[None, :] > e_ids[:, None]), e_ids[None, :], N_EXPERTS)
    next_e = jnp.min(later, axis=1)
    next_e = jnp.where(next_e == N_EXPERTS, -1, next_e).astype(jnp.int32)
    e_slot = ((jnp.cumsum(nonempty.astype(jnp.int32)) - 1) % 2).astype(jnp.int32)
    yg_rows = _experts(layer, blk_e, nused, next_e, e_slot, blk_rows, xg_rows, w1_all, w2_all,
                       b1_glu[:, None, :], b1_lin[:, None, :], b2[:, None, :])
    return _combine(tile_n, tile_carry, group_start, yg_rows, lp_t, gate_t, x, gate2, seq, MOE_TILE)


def kernel(x, c, ada_w, ada_b, norm1_g, norm2_g, m_in_w, m_conv_w, m_conv_b, m_dt_bias, m_A_log, m_D, m_norm_g, m_out_w, a_qkv_w, a_q_norm_g, a_k_norm_g, a_sinks, a_out_w, rel_bias, r_w, r_b, e_w1, e_b1, e_w2, e_b2):
    batch, seq, d = x.shape
    depth = ada_w.shape[0]
    t = batch * seq
    xf = x.reshape(t, d)

    c_pad = jnp.zeros((SUBLANES, d), F32).at[:batch].set(c)
    mod = _adaln(c_pad, ada_w, ada_b)[:, :batch]

    b1_split = jnp.moveaxis(e_b1.reshape(depth, N_EXPERTS, -1, 2), -1, 1)

    for i in range(depth):
        parts =[mod[i, :, p * d:(p + 1) * d].reshape(batch, 1, d) for p in range(6)]
        sh1, sc1, g1, sh2, sc2, g2 = parts
        j = i // 2
        if i % 2 == 0:
            w_zxbc = m_in_w[j, :, :SSM_ZXBC].astype(BF16)
            w_dt = jnp.pad(m_in_w[j, :, SSM_ZXBC:], ((0, 0), (0, LANES - SSM_HEADS))).astype(BF16)
            zxbc, dt_raw = _norm_matmul(xf, norm1_g[i], sh1, sc1, (w_zxbc, w_dt), seq)
            y = _ssd_mixer(zxbc, dt_raw, m_conv_w[j], m_conv_b[j], m_dt_bias[j], m_A_log[j], m_D[j],
                           m_norm_g[j], batch, seq)
            xf = _matmul_residual(y, m_out_w[j].astype(BF16), xf, g1, seq)
        else:
            qkv, = _norm_matmul(xf, norm1_g[i], sh1, sc1, (a_qkv_w[j].astype(BF16),), seq)
            y = _swa_mixer(qkv, a_q_norm_g[j], a_k_norm_g[j], a_sinks[j], rel_bias, batch, seq)
            xf = _matmul_residual(y, a_out_w[j].astype(BF16), xf, g1, seq)
        xf = _moe(i, xf, norm2_g[i], sh2, sc2, g2, r_w[i], r_b[i], e_w1, b1_split[i, 0], b1_split[i, 1],
                  e_w2, e_b2[i], seq)
    return xf.reshape(batch, seq, d)
```

```python
import functools
import math

import jax
import jax.numpy as jnp
from jax import lax
from jax.experimental import pallas as pl
from jax.experimental.pallas import tpu as pltpu

D_MODEL = 1024
EPS = 1e-6
LANES = 128
SUBLANES = 8
ROW_TILES = D_MODEL // LANES

SSM_D_INNER = 2048
SSM_HEAD_DIM = 64
SSM_HEADS = 32
SSM_GROUPS = 4
SSM_STATE = 128
SSM_CONV = 4
SSM_CHUNK = 128
SSM_GN = SSM_GROUPS * SSM_STATE
SSM_ZXBC = 2 * SSM_D_INNER + 2 * SSM_GN
SSM_GROUP_W = SSM_D_INNER // SSM_GROUPS

ATTN_HEAD_DIM = 64
ATTN_Q_HEADS = 16
ATTN_KV_HEADS = 4
ATTN_Q_PER_KV = 4
WINDOW = 128
REL_BUCKETS = 32
REL_MAX_DIST = 128

N_EXPERTS = 32
TOP_K = 4
SWIGLU_ALPHA = 1.702
SWIGLU_LIMIT = 7.0
MOE_BLOCK = 512
MOE_SUB = 256
MOE_TILE = 256
SPLIT_W = 256

VMEM_LIMIT = 56 * 1024 * 1024
HI = lax.Precision.HIGHEST
F32 = jnp.float32
BF16 = jnp.bfloat16


def _cparams(*sem):
    return pltpu.CompilerParams(dimension_semantics=sem, vmem_limit_bytes=VMEM_LIMIT)


def _norm_modulate(x, g, shift, scale):
    ms = jnp.mean(x * x, axis=-1, keepdims=True)
    return x * lax.rsqrt(ms + EPS) * g * (1.0 + scale) + shift


def _adaln_kernel(c_ref, w_ref, b_ref, o_ref):
    c = c_ref[...]
    c_act = c * jax.nn.sigmoid(c)
    o_ref[0] = jnp.dot(c_act, w_ref[0], preferred_element_type=F32, precision=HI) + b_ref[0]


def _adaln(c_pad, ada_w, ada_b):
    depth, d, n = ada_w.shape
    tn = 1536
    return pl.pallas_call(
        _adaln_kernel,
        grid=(depth, n // tn),
        in_specs=[
            pl.BlockSpec((SUBLANES, d), lambda i, j: (0, 0)),
            pl.BlockSpec((1, d, tn), lambda i, j: (i, 0, j)),
            pl.BlockSpec((1, 1, tn), lambda i, j: (i, 0, j)),
        ],
        out_specs=pl.BlockSpec((1, SUBLANES, tn), lambda i, j: (i, 0, j)),
        out_shape=jax.ShapeDtypeStruct((depth, SUBLANES, n), F32),
        compiler_params=_cparams("arbitrary", "arbitrary"),
        name="adaln",
    )(c_pad, ada_w, ada_b.reshape(depth, 1, n))


def _norm_matmul_kernel(x_ref, g_ref, sh_ref, sc_ref, *refs):
    w_refs, o_refs = refs[:len(refs) // 2], refs[len(refs) // 2:]
    h = _norm_modulate(x_ref[...], g_ref[...], sh_ref[0], sc_ref[0]).astype(BF16)
    for w_ref, o_ref in zip(w_refs, o_refs):
        o_ref[...] = jnp.dot(h, w_ref[...], preferred_element_type=F32)


def _norm_matmul(x, g, shift, scale, weights_bf16, seq, tm=512):
    t, d = x.shape
    per_b = seq // tm
    return pl.pallas_call(
        _norm_matmul_kernel,
        grid=(t // tm,),
        in_specs=[
            pl.BlockSpec((tm, d), lambda i: (i, 0)),
            pl.BlockSpec((1, d), lambda i: (0, 0)),
            pl.BlockSpec((1, 1, d), lambda i: (i // per_b, 0, 0)),
            pl.BlockSpec((1, 1, d), lambda i: (i // per_b, 0, 0)),
        ] + [pl.BlockSpec(w.shape, lambda i: (0, 0), pipeline_mode=pl.Buffered(1))
             for w in weights_bf16],
        out_specs=[pl.BlockSpec((tm, w.shape[1]), lambda i: (i, 0)) for w in weights_bf16],
        out_shape=[jax.ShapeDtypeStruct((t, w.shape[1]), F32) for w in weights_bf16],
        compiler_params=_cparams("arbitrary"),
        name="norm_matmul",
    )(x, g.reshape(1, d), shift, scale, *weights_bf16)


def _matmul_residual_kernel(y_ref, w_ref, x_ref, gate_ref, o_ref):
    acc = jnp.dot(y_ref[...], w_ref[...], preferred_element_type=F32)
    o_ref[...] = x_ref[...] + gate_ref[0] * acc


def _matmul_residual(y_bf16, w_bf16, x, gate, seq, tm=512):
    t, k = y_bf16.shape
    d = x.shape[1]
    per_b = seq // tm
    return pl.pallas_call(
        _matmul_residual_kernel,
        grid=(t // tm,),
        in_specs=[
            pl.BlockSpec((tm, k), lambda i: (i, 0)),
            pl.BlockSpec((k, d), lambda i: (0, 0)),
            pl.BlockSpec((tm, d), lambda i: (i, 0)),
            pl.BlockSpec((1, 1, d), lambda i: (i // per_b, 0, 0)),
        ],
        out_specs=pl.BlockSpec((tm, d), lambda i: (i, 0)),
        out_shape=jax.ShapeDtypeStruct((t, d), F32),
        compiler_params=_cparams("arbitrary"),
        name="matmul_residual",
    )(y_bf16, w_bf16, x, gate)


def _causal_conv_silu(cur, prev_tail, w, b):
    rows = lax.broadcasted_iota(jnp.int32, (SUBLANES, cur.shape[1]), 0)
    acc = b + w[SSM_CONV - 1:SSM_CONV] * cur
    for d in range(1, SSM_CONV):
        rolled = pltpu.roll(cur, d, axis=0)
        top = jnp.where(rows < d, pltpu.roll(prev_tail, d, axis=0), rolled[0:SUBLANES])
        shifted = jnp.concatenate([top, rolled[SUBLANES:]], axis=0)
        acc = acc + w[SSM_CONV - 1 - d:SSM_CONV - d] * shifted
    return acc * jax.nn.sigmoid(acc)


def _split3(x):
    hi = x.astype(BF16)
    rem = x - hi.astype(F32)
    mid = rem.astype(BF16)
    lo = (rem - mid.astype(F32)).astype(BF16)
    return jnp.concatenate([hi, mid, lo], axis=1)


def _ssd_kernel(z_ref, xs_ref, bc_ref, dt_ref, cwx_ref, cbx_ref, cwb_ref, cbb_ref, dtb_ref, alog_ref,
                dskip_ref, ng_ref, hexp_ref, lexp_ref, o_ref, tailx_ref, tailb_ref, state_ref):
    c = pl.program_id(1)

    @pl.when(c == 0)
    def _():
        tailx_ref[...] = jnp.zeros_like(tailx_ref)
        tailb_ref[...] = jnp.zeros_like(tailb_ref)
        state_ref[...] = jnp.zeros_like(state_ref)

    xs_raw = xs_ref[...]
    bc_raw = bc_ref[...]
    xs = _causal_conv_silu(xs_raw, tailx_ref[...], cwx_ref[...], cbx_ref[...])
    bc = _causal_conv_silu(bc_raw, tailb_ref[...], cwb_ref[...], cbb_ref[...])
    tailx_ref[...] = xs_raw[SSM_CHUNK - SUBLANES:]
    tailb_ref[...] = bc_raw[SSM_CHUNK - SUBLANES:]

    dt_in = dt_ref[...][:, :SSM_HEADS] + dtb_ref[...]
    dt = jnp.maximum(dt_in, 0.0) + jnp.log1p(jnp.exp(-jnp.abs(dt_in)))
    a_neg = -jnp.exp(alog_ref[...])
    d_a = dt * a_neg
    li = lax.broadcasted_iota(jnp.int32, (SSM_CHUNK, SSM_CHUNK), 0)
    si = lax.broadcasted_iota(jnp.int32, (SSM_CHUNK, SSM_CHUNK), 1)
    causal = li >= si
    tri = causal.astype(F32)
    a_cs = jnp.dot(tri, d_a, preferred_element_type=F32, precision=HI)
    a_cs_t = lax.dot_general(d_a, tri, (((0,), (1,)), ((), ())),
                             preferred_element_type=F32, precision=HI)
    a_last = a_cs[SSM_CHUNK - 1:SSM_CHUNK]
    e_out = jnp.exp(a_cs)
    e_state = jnp.exp(a_last - a_cs) * dt
    small = jnp.concatenate([dt, e_out, e_state], axis=0)
    wide = jnp.dot(_split3(small), hexp_ref[...], preferred_element_type=F32)
    dt_w = wide[0:SSM_CHUNK]
    e_out_w = wide[SSM_CHUNK:2 * SSM_CHUNK]
    e_state_w = wide[2 * SSM_CHUNK:]
    a_col = jnp.dot(_split3(a_cs), lexp_ref[...], preferred_element_type=F32)

    x_dt = (xs * dt_w).astype(BF16)
    x_state = (xs * e_state_w).astype(BF16)
    chunk_decay_w = e_out_w[SSM_CHUNK - 1:SSM_CHUNK]

    heads_per_group = SSM_HEADS // SSM_GROUPS
    y_parts = []
    for g in range(SSM_GROUPS):
        b_g = bc[:, g * SSM_STATE:(g + 1) * SSM_STATE].astype(BF16)
        c_g = bc[:, SSM_GN + g * SSM_STATE:SSM_GN + (g + 1) * SSM_STATE].astype(BF16)
        cb = lax.dot_general(c_g, b_g, (((1,), (1,)), ((), ())), preferred_element_type=F32)
        gsl = slice(g * SSM_GROUP_W, (g + 1) * SSM_GROUP_W)
        h_prev = state_ref[g]
        y_off = jnp.dot(c_g, h_prev.astype(BF16), preferred_element_type=F32) * e_out_w[:, gsl]
        diag = []
        for r in range(heads_per_group):
            h = g * heads_per_group + r
            seg = a_col[:, h * SSM_CHUNK:(h + 1) * SSM_CHUNK] - a_cs_t[h:h + 1, :]
            decay = jnp.exp(jnp.where(causal, seg, -jnp.inf))
            m = (cb * decay).astype(BF16)
            diag.append(jnp.dot(m, x_dt[:, h * SSM_HEAD_DIM:(h + 1) * SSM_HEAD_DIM],
                                preferred_element_type=F32))
        y_parts.append(jnp.concatenate(diag, axis=1) + y_off)
        upd = lax.dot_general(b_g, x_state[:, gsl], (((0,), (0,)), ((), ())), preferred_element_type=F32)
        state_ref[g] = h_prev * chunk_decay_w[:, gsl] + upd

    y = jnp.concatenate(y_parts, axis=1) + dskip_ref[...] * xs
    z = z_ref[...]
    y = y * (z * jax.nn.sigmoid(z))
    normed = []
    for g in range(SSM_GROUPS):
        y_g = y[:, g * SSM_GROUP_W:(g + 1) * SSM_GROUP_W]
        normed.append(y_g * lax.rsqrt(jnp.mean(y_g * y_g, axis=-1, keepdims=True) + EPS))
    o_ref[...] = (jnp.concatenate(normed, axis=1) * ng_ref[...]).astype(o_ref.dtype)


def _ssd_mixer(zxbc, dt_raw, conv_w, conv_b, dt_bias, a_log, d_skip, norm_g, batch, seq):
    t = zxbc.shape[0]
    nc = seq // SSM_CHUNK
    head_expand = jnp.tile(jnp.repeat(jnp.eye(SSM_HEADS, dtype=BF16), SSM_HEAD_DIM, axis=1), (3, 1))
    lane_expand = jnp.tile(jnp.repeat(jnp.eye(SSM_HEADS, dtype=BF16), SSM_CHUNK, axis=1), (3, 1))
    row = lambda b, c: (b * nc + c, 0)
    const2 = lambda b, c: (0, 0)
    bc_w = 2 * SSM_GN
    return pl.pallas_call(
        _ssd_kernel,
        grid=(batch, nc),
        in_specs=[
            pl.BlockSpec((SSM_CHUNK, SSM_D_INNER), row),
            pl.BlockSpec((SSM_CHUNK, SSM_D_INNER), lambda b, c: (b * nc + c, 1)),
            pl.BlockSpec((SSM_CHUNK, bc_w), lambda b, c: (b * nc + c, 2 * SSM_D_INNER // bc_w)),
            pl.BlockSpec((SSM_CHUNK, LANES), row),
            pl.BlockSpec((SSM_CONV, SSM_D_INNER), const2),
            pl.BlockSpec((1, SSM_D_INNER), const2),
            pl.BlockSpec((SSM_CONV, bc_w), const2),
            pl.BlockSpec((1, bc_w), const2),
            pl.BlockSpec((1, SSM_HEADS), const2),
            pl.BlockSpec((1, SSM_HEADS), const2),
            pl.BlockSpec((1, SSM_D_INNER), const2),
            pl.BlockSpec((1, SSM_D_INNER), const2),
            pl.BlockSpec((3 * SSM_HEADS, SSM_D_INNER), const2),
            pl.BlockSpec((3 * SSM_HEADS, SSM_HEADS * SSM_CHUNK), const2),
        ],
        out_specs=pl.BlockSpec((SSM_CHUNK, SSM_D_INNER), row),
        out_shape=jax.ShapeDtypeStruct((t, SSM_D_INNER), BF16),
        scratch_shapes=[
            pltpu.VMEM((SUBLANES, SSM_D_INNER), F32),
            pltpu.VMEM((SUBLANES, bc_w), F32),
            pltpu.VMEM((SSM_GROUPS, SSM_STATE, SSM_GROUP_W), F32),
        ],
        compiler_params=_cparams("arbitrary", "arbitrary"),
        name="ssd_mixer",
    )(zxbc, zxbc, zxbc, dt_raw,
      conv_w[:, :SSM_D_INNER], conv_b[:SSM_D_INNER].reshape(1, -1),
      conv_w[:, SSM_D_INNER:], conv_b[SSM_D_INNER:].reshape(1, -1),
      dt_bias.reshape(1, -1), a_log.reshape(1, -1),
      jnp.repeat(d_skip, SSM_HEAD_DIM).reshape(1, -1), norm_g.reshape(1, -1),
      head_expand, lane_expand)


def _head_rms(x, g):
    return x * lax.rsqrt(jnp.mean(x * x, axis=-1, keepdims=True) + EPS) * g


def _swa_kernel(q_ref, kvc_ref, kvp_ref, bucket_ref, qg_ref, kg_ref, rel_ref, sink_ref, o_ref,
                bias_ref, sinkrow_ref):
    b = pl.program_id(0)
    i = pl.program_id(1)

    @pl.when((b == 0) & (i == 0))
    def _():
        bucket = bucket_ref[...]
        kj = lax.broadcasted_iota(jnp.int32, (2 * WINDOW, WINDOW), 0)
        qi = lax.broadcasted_iota(jnp.int32, (2 * WINDOW, WINDOW), 1)
        dist = qi + WINDOW - kj
        band = (dist >= 0) & (dist < WINDOW)
        for h in range(ATTN_Q_HEADS):
            g, r = divmod(h, ATTN_Q_PER_KV)
            acc = jnp.zeros(bucket.shape, F32)
            for k in range(REL_BUCKETS):
                acc = jnp.where(bucket == k, rel_ref[k, h], acc)
            cols = slice(r * WINDOW, (r + 1) * WINDOW)
            bias_ref[0, g, :, cols] = jnp.where(band, acc, -jnp.inf)
            bias_ref[1, g, :, cols] = jnp.where(band & (kj >= WINDOW), acc, -jnp.inf)
            sinkrow_ref[g, :, cols] = jnp.full((1, WINDOW), sink_ref[h], F32)

    first = (i == 0).astype(jnp.int32)
    q_t = q_ref[...].T
    kv_c = kvc_ref[...]
    kv_p = kvp_ref[...]
    kv_w = ATTN_KV_HEADS * ATTN_HEAD_DIM
    q_gain = qg_ref[...]
    outs = []
    for g in range(ATTN_KV_HEADS):
        ksl = slice(g * ATTN_HEAD_DIM, (g + 1) * ATTN_HEAD_DIM)
        vsl = slice(kv_w + g * ATTN_HEAD_DIM, kv_w + (g + 1) * ATTN_HEAD_DIM)
        k = jnp.concatenate([kv_p[:, ksl], kv_c[:, ksl]], axis=0)
        v = jnp.concatenate([kv_p[:, vsl], kv_c[:, vsl]], axis=0).astype(BF16)
        k = _head_rms(k, kg_ref[...]).astype(BF16)
        q_heads = []
        for r in range(ATTN_Q_PER_KV):
            h = g * ATTN_Q_PER_KV + r
            q_h = q_t[h * ATTN_HEAD_DIM:(h + 1) * ATTN_HEAD_DIM]
            inv = lax.rsqrt(jnp.mean(q_h * q_h, axis=0, keepdims=True) + EPS)
            q_heads.append((q_h * inv * q_gain).astype(BF16))
        q = jnp.concatenate(q_heads, axis=1)
        s = jnp.dot(k, q, preferred_element_type=F32) + bias_ref[first, g]
        sink = sinkrow_ref[g]
        m = jnp.maximum(jnp.max(s, axis=0, keepdims=True), sink)
        p = jnp.exp(s - m)
        denom = jnp.sum(p, axis=0, keepdims=True) + jnp.exp(sink - m)
        pv = lax.dot_general(v, p.astype(BF16), (((0,), (0,)), ((), ())), preferred_element_type=F32)
        pv = pv * (1.0 / denom)
        outs.extend(pv[:, r * WINDOW:(r + 1) * WINDOW] for r in range(ATTN_Q_PER_KV))
    o_ref[...] = jnp.concatenate(outs, axis=0).T.astype(o_ref.dtype)


def _t5_causal_bucket(dist):
    max_exact = REL_BUCKETS // 2
    d = jnp.maximum(dist, 1).astype(F32)
    large = max_exact + (jnp.log(d / max_exact) / math.log(REL_MAX_DIST / max_exact)
                         * (REL_BUCKETS - max_exact)).astype(jnp.int32)
    large = jnp.minimum(large, REL_BUCKETS - 1)
    return jnp.where(dist < max_exact, dist, large)


def _swa_mixer(qkv, q_norm_g, k_norm_g, sinks, rel_bias, batch, seq):
    t = qkv.shape[0]
    nb = seq // WINDOW
    q_w = ATTN_Q_HEADS * ATTN_HEAD_DIM
    kv_w2 = 2 * ATTN_KV_HEADS * ATTN_HEAD_DIM
    kj = jnp.arange(2 * WINDOW)[:, None]
    qi = jnp.arange(WINDOW)[None, :]
    bucket = _t5_causal_bucket(jnp.maximum(qi + WINDOW - kj, 0)).astype(jnp.int32)
    q_gain = jnp.broadcast_to((q_norm_g * (ATTN_HEAD_DIM ** -0.5))[:, None], (ATTN_HEAD_DIM, WINDOW))
    const2 = lambda b, i: (0, 0)
    smem = pl.BlockSpec(memory_space=pltpu.SMEM)
    return pl.pallas_call(
        _swa_kernel,
        grid=(batch, nb),
        in_specs=[
            pl.BlockSpec((WINDOW, q_w), lambda b, i: (b * nb + i, 0)),
            pl.BlockSpec((WINDOW, kv_w2), lambda b, i: (b * nb + i, q_w // kv_w2)),
            pl.BlockSpec((WINDOW, kv_w2), lambda b, i: (b * nb + jnp.maximum(i - 1, 0), q_w // kv_w2)),
            pl.BlockSpec((2 * WINDOW, WINDOW), const2),
            pl.BlockSpec((ATTN_HEAD_DIM, WINDOW), const2),
            pl.BlockSpec((1, ATTN_HEAD_DIM), const2),
            smem,
            smem,
        ],
        out_specs=pl.BlockSpec((WINDOW, q_w), lambda b, i: (b * nb + i, 0)),
        out_shape=jax.ShapeDtypeStruct((t, q_w), BF16),
        scratch_shapes=[pltpu.VMEM((2, ATTN_KV_HEADS, 2 * WINDOW, ATTN_Q_PER_KV * WINDOW), F32),
                        pltpu.VMEM((ATTN_KV_HEADS, 1, ATTN_Q_PER_KV * WINDOW), F32)],
        compiler_params=_cparams("arbitrary", "arbitrary"),
        name="swa_mixer",
    )(qkv, qkv, qkv, bucket, q_gain, k_norm_g.reshape(1, -1), rel_bias, sinks)


def _router_kernel(x_ref, g_ref, sh_ref, sc_ref, rwt_ref, rb_ref,
                   h_ref, lp_ref, gate_ref, tile_n_ref, tile_carry_ref, cnt_ref, carry_ref):
    i = pl.program_id(0)
    tm = x_ref.shape[0]

    @pl.when(i == 0)
    def _():
        carry_ref[...] = jnp.zeros_like(carry_ref)

    h = _norm_modulate(x_ref[...], g_ref[...], sh_ref[0], sc_ref[0]).astype(BF16)
    h_ref[...] = h

    logits = lax.dot_general(rwt_ref[...], h, (((1,), (1,)), ((), ())),
                             preferred_element_type=F32) + rb_ref[...]
    e_iota = lax.broadcasted_iota(jnp.int32, logits.shape, 0)
    work = logits
    sels, vals = [], []
    for k in range(TOP_K):
        m = jnp.max(work, axis=0, keepdims=True)
        idx = jnp.min(jnp.where(work == m, e_iota, N_EXPERTS), axis=0, keepdims=True)
        sel = e_iota == idx
        work = jnp.where(sel, -jnp.inf, work)
        sels.append(sel)
        vals.append(m)
    exps = [jnp.exp(v - vals[0]) for v in vals]
    denom = exps[0] + exps[1] + exps[2] + exps[3]
    for k in range(TOP_K):
        gate_ref[k:k + 1, :] = exps[k] / denom

    chosen = sels[0] | sels[1] | sels[2] | sels[3]
    t_row = lax.broadcasted_iota(jnp.int32, (tm, tm), 0)
    t_col = lax.broadcasted_iota(jnp.int32, (tm, tm), 1)
    before = (t_row < t_col).astype(BF16)
    prior = jnp.dot(chosen.astype(BF16), before, preferred_element_type=F32)
    n = jnp.sum(chosen.astype(F32), axis=1, keepdims=True)
    e_row = lax.broadcasted_iota(jnp.int32, (N_EXPERTS, N_EXPERTS), 0)
    e_col = lax.broadcasted_iota(jnp.int32, (N_EXPERTS, N_EXPERTS), 1)
    run_start = jnp.dot((e_col < e_row).astype(BF16), jnp.broadcast_to(n, (N_EXPERTS, LANES)).astype(BF16),
                        preferred_element_type=F32)[:, :1]
    local = run_start + prior
    for k in range(TOP_K):
        lp_ref[k:k + 1, :] = jnp.sum(jnp.where(sels[k], local, 0.0), axis=0, keepdims=True).astype(jnp.int32)
    tile_n_ref[0] = n.astype(jnp.int32)
    tile_carry_ref[0] = carry_ref[...].astype(jnp.int32)
    total = carry_ref[...] + n
    carry_ref[...] = total
    cnt_ref[...] = total.astype(jnp.int32)


def _router(x, g, shift, scale, r_w, r_b, seq, tm):
    t, d = x.shape
    per_b = seq // tm
    ntiles = t // tm
    return pl.pallas_call(
        _router_kernel,
        grid=(t // tm,),
        in_specs=[
            pl.BlockSpec((tm, d), lambda i: (i, 0)),
            pl.BlockSpec((1, d), lambda i: (0, 0)),
            pl.BlockSpec((1, 1, d), lambda i: (i // per_b, 0, 0)),
            pl.BlockSpec((1, 1, d), lambda i: (i // per_b, 0, 0)),
            pl.BlockSpec((N_EXPERTS, d), lambda i: (0, 0)),
            pl.BlockSpec((N_EXPERTS, 1), lambda i: (0, 0)),
        ],
        out_specs=[
            pl.BlockSpec((tm, d), lambda i: (i, 0)),
            pl.BlockSpec((TOP_K, tm), lambda i: (0, i)),
            pl.BlockSpec((TOP_K, tm), lambda i: (0, i)),
            pl.BlockSpec((1, N_EXPERTS, 1), lambda i: (i, 0, 0)),
            pl.BlockSpec((1, N_EXPERTS, 1), lambda i: (i, 0, 0)),
            pl.BlockSpec((N_EXPERTS, 1), lambda i: (0, 0)),
        ],
        out_shape=[
            jax.ShapeDtypeStruct((t, d), BF16),
            jax.ShapeDtypeStruct((TOP_K, t), jnp.int32),
            jax.ShapeDtypeStruct((TOP_K, t), F32),
            jax.ShapeDtypeStruct((ntiles, N_EXPERTS, 1), jnp.int32),
            jax.ShapeDtypeStruct((ntiles, N_EXPERTS, 1), jnp.int32),
            jax.ShapeDtypeStruct((N_EXPERTS, 1), jnp.int32),
        ],
        scratch_shapes=[pltpu.VMEM((N_EXPERTS, 1), F32)],
        compiler_params=_cparams("arbitrary"),
        name="moe_router",
    )(x, g.reshape(1, d), shift, scale, r_w.T.astype(BF16), r_b.reshape(-1, 1))


def _run_copies(n, src_ref, src_row, dst_ref, dst_row, sem, max_rows, fn):
    for b in range(max_rows.bit_length()):
        size = 1 << b

        @pl.when((n & size) != 0)
        def _():
            lo = n & (size - 1)
            fn(pltpu.make_async_copy(
                src_ref.at[pl.ds(pl.multiple_of((src_row + lo) * ROW_TILES, ROW_TILES), size * ROW_TILES)],
                dst_ref.at[pl.ds(pl.multiple_of((dst_row + lo) * ROW_TILES, ROW_TILES), size * ROW_TILES)],
                sem), b)


def _start_piece(cp, b):
    cp.start()


def _wait_piece(cp, b):
    cp.wait()


def _tile_runs(tile, n_ref, carry_ref, gstart_ref, sorted_ref, grouped_ref, sem, tm, fn, to_grouped, live=None):
    def body(e, run_start):
        n = n_ref[tile * N_EXPERTS + e]
        if live is not None:
            n = jnp.where(live, n, 0)
        slot0 = gstart_ref[e] + carry_ref[tile * N_EXPERTS + e]
        if to_grouped:
            _run_copies(n, sorted_ref, run_start, grouped_ref, slot0, sem, tm, fn)
        else:
            _run_copies(n, grouped_ref, slot0, sorted_ref, run_start, sem, tm, fn)
        return run_start + n

    if live is None:
        lax.fori_loop(0, N_EXPERTS, body, 0)
    else:
        run_start = 0
        for e in range(N_EXPERTS):
            run_start = body(e, run_start)


def _block_copy(src_ref, dst_ref, dst_blk, sem):
    rows = MOE_BLOCK * ROW_TILES
    return pltpu.make_async_copy(src_ref, dst_ref.at[pl.ds(pl.multiple_of(dst_blk * rows, rows), rows)], sem)


def _dispatch_kernel(n_ref, carry_ref, gstart_ref, pad_lo_ref, pad_hi_ref, nused_ref, lp_ref, h_ref, xg_ref,
                     sorted0_ref, sorted1_ref, sorted2_ref, zero_ref, run_sem, sem):
    i = pl.program_id(0)
    last = pl.num_programs(0) - 1
    tm = h_ref.shape[0]
    rows = TOP_K * tm
    nblk =xg_ref.shape[0] // (MOE_BLOCK * ROW_TILES)

    @pl.when(i == 0)
    def _():
        zero_ref[...] = jnp.zeros_like(zero_ref)

        def fill(b, carry):
            _block_copy(zero_ref, xg_ref, b, sem).start()
            return carry

        def fill_wait(b, carry):
            _block_copy(zero_ref, xg_ref, b, sem).wait()
            return carry

        lax.fori_loop(nused_ref[0], nblk, fill, 0)
        lax.fori_loop(nused_ref[0], nblk, fill_wait, 0)

    @pl.when(i < N_EXPERTS)
    def _():
        n_pad = pad_hi_ref[i] - pad_lo_ref[i]
        for fn in (_start_piece, _wait_piece):
            _run_copies(n_pad, zero_ref, 0, xg_ref, pad_lo_ref[i], sem, MOE_BLOCK - 1, fn)

    bufs = (sorted0_ref, sorted1_ref, sorted2_ref)
    nbuf = len(bufs)

    def runs(tile, b, fn, live=None):
        _tile_runs(tile, n_ref, carry_ref, gstart_ref, bufs[b], xg_ref, run_sem.at[b], tm, fn, True, live)

    def wait_tile(b):
        pltpu.make_async_copy(bufs[b], xg_ref.at[pl.ds(0, rows * ROW_TILES)], run_sem.at[b]).wait()

    def step(cur):
        prev, oldest = (cur - 1) % nbuf, (cur + 1) % nbuf
        cur_ref = bufs[cur]

        @pl.when(i >= nbuf)
        def _():
            wait_tile(cur)

        row_id = lax.broadcasted_iota(jnp.int32, (rows, tm), 0)
        perm = row_id == lp_ref[0:1, :]
        for k in range(1, TOP_K):
            perm = perm | (row_id == lp_ref[k:k + 1, :])
        srt = jnp.dot(jnp.where(perm, 1.0, 0.0).astype(BF16), h_ref[...], preferred_element_type=F32)
        for s in range(ROW_TILES):
            cur_ref[pl.ds(s, rows, stride=ROW_TILES), :] = srt[:, s * LANES:(s + 1) * LANES]

        runs(jnp.maximum(i - 1, 0), prev, _start_piece, live=i >= 1)

        @pl.when(i == last)
        def _():
            @pl.when(i >= 2)
            def _():
                wait_tile(oldest)

            @pl.when(i >= 1)
            def _():
                wait_tile(prev)

            runs(i, cur, _start_piece)
            wait_tile(cur)

    for r in range(nbuf):
        @pl.when(i % nbuf == r)
        def _():
            step(r)


def _dispatch(tile_n, tile_carry, group_start, pad_lo, pad_hi, nused, lp_t, h, cap, tm):
    t, d = h.shape
    assert t // tm >= N_EXPERTS
    grid_spec = pltpu.PrefetchScalarGridSpec(
        num_scalar_prefetch=6,
        grid=(t // tm,),
        in_specs=[
            pl.BlockSpec((TOP_K, tm), lambda i, *_: (0, i)),
            pl.BlockSpec((tm, d), lambda i, *_: (i, 0)),
        ],
        out_specs=pl.BlockSpec(memory_space=pl.ANY),
        scratch_shapes=[
            pltpu.VMEM((TOP_K * tm * ROW_TILES, LANES), F32),
            pltpu.VMEM((TOP_K * tm * ROW_TILES, LANES), F32),
            pltpu.VMEM((TOP_K * tm * ROW_TILES, LANES), F32),
            pltpu.VMEM((MOE_BLOCK * ROW_TILES, LANES), F32),
            pltpu.SemaphoreType.DMA((3,)), pltpu.SemaphoreType.DMA,
        ],
    )
    return pl.pallas_call(
        _dispatch_kernel,
        grid_spec=grid_spec,
        out_shape=jax.ShapeDtypeStruct((cap * ROW_TILES, LANES), F32),
        compiler_params=pltpu.CompilerParams(dimension_semantics=("arbitrary",), has_side_effects=True,
                                             vmem_limit_bytes=VMEM_LIMIT),
        name="moe_dispatch",
    )(tile_n, tile_carry, group_start, pad_lo, pad_hi, nused, lp_t, h)


def _weight_copies(w1_hbm, w2_hbm, w1buf, w2buf, sems, layer, e, slot):
    return (pltpu.make_async_copy(w1_hbm.at[layer, e], w1buf.at[slot], sems.at[0, slot]),
            pltpu.make_async_copy(w2_hbm.at[layer, e], w2buf.at[slot], sems.at[1, slot]))


def _expert_kernel(blk_e_ref, nused_ref, next_e_ref, slot_ref, blk_rows_ref, x_ref, w1_hbm, w2_hbm, bg_ref, bl_ref, b2_ref,
                   sel_ref, o_ref, w1buf, w2buf, wg_s, wl_s, w2_s, sems, *, layer):
    blk = pl.program_id(0)
    active = blk < nused_ref[0]
    e = blk_e_ref[blk]
    slot = slot_ref[e]
    new_expert = (blk == 0) | (e != blk_e_ref[jnp.maximum(blk - 1, 0)])
    copies = functools.partial(_weight_copies, w1_hbm, w2_hbm, w1buf, w2buf, sems, layer)

    @pl.when(blk == 0)
    def _():
        for cp in copies(e, slot):
            cp.start()

    @pl.when(active & new_expert)
    def _():
        nxt = next_e_ref[e]

        @pl.when(nxt >= 0)
        def _():
            for cp in copies(nxt, 1 - slot):
                cp.start()

        for cp in copies(e, slot):
            cp.wait()
        half = SPLIT_W // 2
        for j in range(w1buf.shape[2] // SPLIT_W):
            chunk = w1buf[slot, :, j * SPLIT_W:(j + 1) * SPLIT_W].astype(BF16)
            split = jnp.dot(chunk, sel_ref[...], preferred_element_type=F32)
            wg_s[:, j * half:(j + 1) * half] = split[:, :half].astype(BF16)
            wl_s[:, j * half:(j + 1) * half] = split[:, half:].astype(BF16)
        w2_s[...] = w2buf[slot].astype(BF16)

    def mlp(rows):
        x = jnp.concatenate([x_ref[pl.ds(s, rows, stride=ROW_TILES), :]
                             for s in range(ROW_TILES)], axis=1).astype(BF16)
        glu = jnp.dot(x, wg_s[...], preferred_element_type=F32) + bg_ref[0]
        lin = jnp.dot(x, wl_s[...], preferred_element_type=F32) + bl_ref[0]
        glu = jnp.minimum(glu, SWIGLU_LIMIT)
        lin = jnp.clip(lin, -SWIGLU_LIMIT, SWIGLU_LIMIT)
        act = glu * jax.nn.sigmoid(SWIGLU_ALPHA * glu) * (lin + 1.0)
        y = jnp.dot(act.astype(BF16), w2_s[...], preferred_element_type=F32) + b2_ref[0]
        for s in range(ROW_TILES):
            o_ref[pl.ds(s, rows, stride=ROW_TILES), :] = y[:, s * LANES:(s + 1) * LANES]
        if rows < MOE_BLOCK:
            o_ref[pl.ds(rows * ROW_TILES, (MOE_BLOCK - rows) * ROW_TILES), :] = jnp.zeros(
                ((MOE_BLOCK - rows) * ROW_TILES, LANES), o_ref.dtype)

    valid_rows = jnp.where(active, blk_rows_ref[blk], 0)

    @pl.when(valid_rows > MOE_SUB)
    def _():
        mlp(MOE_BLOCK)

    @pl.when((valid_rows > 0) & (valid_rows <= MOE_SUB))
    def _():
        mlp(MOE_SUB)

    @pl.when(valid_rows <= 0)
    def _():
        o_ref[...] = jnp.zeros_like(o_ref)


def _experts(layer, blk_e, nused, next_e, e_slot, blk_rows, xg_rows, w1_all, w2_all, b_glu, b_lin, b2):
    nblk = blk_e.shape[0]
    d = D_MODEL
    ff = w2_all.shape[2]
    col = jnp.arange(SPLIT_W)
    sel = (jnp.arange(SPLIT_W)[None, :] == ((col % 2) * (SPLIT_W // 2) + col // 2)[:, None]).astype(BF16)

    def blk_map(b, blk_e_ref, nused_ref, *_):
        return (jnp.minimum(b, nused_ref[0] - 1), 0)

    def e_map(b, blk_e_ref, nused_ref, *_):
        return (blk_e_ref[jnp.minimum(b, nused_ref[0] - 1)], 0, 0)

    grid_spec = pltpu.PrefetchScalarGridSpec(
        num_scalar_prefetch=5,
        grid=(nblk,),
        in_specs=[
            pl.BlockSpec((MOE_BLOCK * ROW_TILES, LANES), blk_map),
            pl.BlockSpec(memory_space=pl.ANY),
            pl.BlockSpec(memory_space=pl.ANY),
            pl.BlockSpec((1, 1, ff), e_map),
            pl.BlockSpec((1, 1, ff), e_map),
            pl.BlockSpec((1, 1, d), e_map),
            pl.BlockSpec((SPLIT_W, SPLIT_W), lambda b, *_: (0, 0)),
        ],
        out_specs=pl.BlockSpec((MOE_BLOCK * ROW_TILES, LANES), lambda b, *_: (b, 0)),
        scratch_shapes=[
            pltpu.VMEM((2, d, 2 * ff), F32), pltpu.VMEM((2, ff, d), F32),
            pltpu.VMEM((d, ff), BF16), pltpu.VMEM((d, ff), BF16), pltpu.VMEM((ff, d), BF16),
            pltpu.SemaphoreType.DMA((2, 2)),
        ],
    )
    return pl.pallas_call(
        functools.partial(_expert_kernel, layer=layer),
        grid_spec=grid_spec,
        out_shape=jax.ShapeDtypeStruct(xg_rows.shape, F32),
        compiler_params=_cparams("arbitrary"),
        name="moe_experts",
    )(blk_e, nused, next_e, e_slot, blk_rows, xg_rows, w1_all, w2_all, b_glu, b_lin, b2, sel)


def _combine_kernel(n_ref, carry_ref, gstart_ref, yg_ref, lp_ref, gate_ref, x_ref, g2_ref, o_ref, sorted_ref, sem):
    i = pl.program_id(0)
    ntiles = pl.num_programs(0)
    tm = x_ref.shape[0]
    rows = TOP_K * tm
    slot = i % 2

    def runs(tile, buf, fn, live=None):
        _tile_runs(tile, n_ref, carry_ref, gstart_ref, sorted_ref.at[buf], yg_ref, sem.at[buf], tm, fn, False, live)

    @pl.when(i == 0)
    def _():
        runs(0, 0, _start_piece)

    runs(jnp.minimum(i + 1, ntiles - 1), 1 - slot, _start_piece, live=i + 1 < ntiles)

    eye = (lax.broadcasted_iota(jnp.int32, (tm, tm), 0) ==
           lax.broadcasted_iota(jnp.int32, (tm, tm), 1)).astype(F32)
    rows_t = jnp.concatenate([gate_ref[...], lp_ref[...].astype(F32)], axis=0)
    cols = lax.dot_general(eye, rows_t, (((1,), (1,)), ((), ())), preferred_element_type=F32, precision=HI)
    col_id = lax.broadcasted_iota(jnp.int32, (tm, rows), 1)
    weights = jnp.zeros((tm, rows), F32)
    for k in range(TOP_K):
        lp_col = (cols[:, TOP_K + k:TOP_K + k + 1] + 0.5).astype(jnp.int32)
        weights = weights + jnp.where(col_id == lp_col, cols[:, k:k + 1], 0.0)

    pltpu.make_async_copy(yg_ref.at[pl.ds(0, rows * ROW_TILES)], sorted_ref.at[slot], sem.at[slot]).wait()
    y = jnp.concatenate(
        [sorted_ref[slot, pl.ds(s, rows, stride=ROW_TILES), :] for s in range(ROW_TILES)], axis=1).astype(BF16)
    acc = jnp.dot(weights.astype(BF16), y, preferred_element_type=F32)
    o_ref[...] = x_ref[...] + g2_ref[0] * acc


def _combine(tile_n, tile_carry, group_start, yg_rows, lp_t, gate_t, x, gate2, seq, tm):
    t, d = x.shape
    per_b = seq // tm
    grid_spec = pltpu.PrefetchScalarGridSpec(
        num_scalar_prefetch=3,
        grid=(t // tm,),
        in_specs=[
            pl.BlockSpec(memory_space=pl.ANY),
            pl.BlockSpec((TOP_K, tm), lambda i, *_: (0, i)),
            pl.BlockSpec((TOP_K, tm), lambda i, *_: (0, i)),
            pl.BlockSpec((tm, d), lambda i, *_: (i, 0)),
            pl.BlockSpec((1, 1, d), lambda i, *_: (i // per_b, 0, 0)),
        ],
        out_specs=pl.BlockSpec((tm, d), lambda i, *_: (i, 0)),
        scratch_shapes=[pltpu.VMEM((2, TOP_K * tm * ROW_TILES, LANES), F32), pltpu.SemaphoreType.DMA((2,))],
    )
    return pl.pallas_call(
        _combine_kernel,
        grid_spec=grid_spec,
        out_shape=jax.ShapeDtypeStruct((t, d), F32),
        compiler_params=_cparams("arbitrary"),
        name="moe_combine",
    )(tile_n, tile_carry, group_start, yg_rows, lp_t, gate_t, x, gate2)


def _moe(layer, x, g, shift, scale, gate2, r_w, r_b, w1_all, b1_glu, b1_lin, w2_all, b2, seq):
    t = x.shape[0]
    h, lp_t, gate_t, tile_n, tile_carry, counts = _router(x, g, shift, scale, r_w, r_b, seq, MOE_TILE)
    tile_n = tile_n.reshape(-1)
    tile_carry = tile_carry.reshape(-1)
    counts = counts.reshape(-1)
    padded = ((counts + MOE_BLOCK - 1) // MOE_BLOCK) * MOE_BLOCK
    group_end = jnp.cumsum(padded)
    group_start = (group_end - padded).astype(jnp.int32)
    cap = t * TOP_K + N_EXPERTS * MOE_BLOCK
    nblk = cap // MOE_BLOCK
    blk_start = jnp.arange(nblk, dtype=jnp.int32) * MOE_BLOCK
    e_ids = jnp.arange(N_EXPERTS, dtype=jnp.int32)
    past = (blk_start[None, :] >= group_end[:, None]).astype(jnp.int32)
    blk_e = jnp.minimum(jnp.sum(past, axis=0), N_EXPERTS - 1).astype(jnp.int32)
    nused = (group_end[-1:] // MOE_BLOCK).astype(jnp.int32)

    pad_lo = (group_start + counts).astype(jnp.int32)
    pad_hi = group_end.astype(jnp.int32)
    blk_pad_lo = jnp.sum(jnp.where(blk_e[None, :] == e_ids[:, None], pad_lo[:, None], 0), axis=0)
    blk_rows = jnp.clip(blk_pad_lo - blk_start, 0, MOE_BLOCK).astype(jnp.int32)
    xg_rows = _dispatch(tile_n, tile_carry, group_start, pad_lo, pad_hi, nused, lp_t, h, cap, MOE_TILE)
    nonempty = counts > 0
    later = jnp.where(nonempty[None, :] & (e_ids[None, :] > e_ids[:, None]), e_ids[None, :], N_EXPERTS)
    next_e = jnp.min(later, axis=1)
    next_e = jnp.where(next_e == N_EXPERTS, -1, next_e).astype(jnp.int32)
    e_slot = ((jnp.cumsum(nonempty.astype(jnp.int32)) - 1) % 2).astype(jnp.int32)
    yg_rows = _experts(layer, blk_e, nused, next_e, e_slot, blk_rows, xg_rows, w1_all, w2_all,
                       b1_glu[:, None, :], b1_lin[:, None, :], b2[:, None, :])
    return _combine(tile_n, tile_carry, group_start, yg_rows, lp_t, gate_t, x, gate2, seq, MOE_TILE)


def kernel(x, c, ada_w, ada_b, norm1_g, norm2_g, m_in_w, m_conv_w, m_conv_b, m_dt_bias, m_A_log, m_D, m_norm_g, m_out_w, a_qkv_w, a_q_norm_g, a_k_norm_g, a_sinks, a_out_w, rel_bias, r_w, r_b, e_w1, e_b1, e_w2, e_b2):
    batch, seq, d = x.shape
    depth = ada_w.shape[0]
    t = batch * seq
    xf = x.reshape(t, d)

    c_pad = jnp.zeros((SUBLANES, d), F32).at[:batch].set(c)
    mod = _adaln(c_pad, ada_w, ada_b)[:, :batch]

    b1_split = jnp.moveaxis(e_b1.reshape(depth, N_EXPERTS, -1, 2), -1, 1)

    for i in range(depth):
        parts =[mod[i, :, p * d:(p + 1) * d].reshape(batch, 1, d) for p in range(6)]
        sh1, sc1, g1, sh2, sc2, g2 = parts
        j = i // 2
        if i % 2 == 0:
            w_zxbc = m_in_w[j, :, :SSM_ZXBC].astype(BF16)
            w_dt = jnp.pad(m_in_w[j, :, SSM_ZXBC:], ((0, 0), (0, LANES - SSM_HEADS))).astype(BF16)
            zxbc, dt_raw = _norm_matmul(xf, norm1_g[i], sh1, sc1, (w_zxbc, w_dt), seq)
            y = _ssd_mixer(zxbc, dt_raw, m_conv_w[j], m_conv_b[j], m_dt_bias[j], m_A_log[j], m_D[j],
                           m_norm_g[j], batch, seq)
            xf = _matmul_residual(y, m_out_w[j].astype(BF16), xf, g1, seq)
        else:
            qkv, = _norm_matmul(xf, norm1_g[i], sh1, sc1, (a_qkv_w[j].astype(BF16),), seq)
            y = _swa_mixer(qkv, a_q_norm_g[j], a_k_norm_g[j], a_sinks[j], rel_bias, batch, seq)
            xf = _matmul_residual(y, a_out_w[j].astype(BF16), xf, g1, seq)
        xf = _moe(i, xf, norm2_g[i], sh2, sc2, g2, r_w[i], r_b[i], e_w1, b1_split[i, 0], b1_split[i, 1],
                  e_w2, e_b2[i], seq)
    return xf.reshape(batch, seq, d)
```

```python
import functools
import math

import jax
import jax.numpy as jnp
from jax import lax
from jax.experimental import pallas as pl
from jax.experimental.pallas import tpu as pltpu

D_MODEL = 1024
EPS = 1e-6
LANES = 128
SUBLANES = 8
ROW_TILES = D_MODEL // LANES

SSM_D_INNER = 2048
SSM_HEAD_DIM = 64
SSM_HEADS = 32
SSM_GROUPS = 4
SSM_STATE = 128
SSM_CONV = 4
SSM_CHUNK = 128
SSM_GN = SSM_GROUPS * SSM_STATE
SSM_ZXBC = 2 * SSM_D_INNER + 2 * SSM_GN
SSM_GROUP_W = SSM_D_INNER // SSM_GROUPS

ATTN_HEAD_DIM = 64
ATTN_Q_HEADS = 16
ATTN_KV_HEADS = 4
ATTN_Q_PER_KV = 4
WINDOW = 128
REL_BUCKETS = 32
REL_MAX_DIST = 128

N_EXPERTS = 32
TOP_K = 4
SWIGLU_ALPHA = 1.702
SWIGLU_LIMIT = 7.0
MOE_BLOCK = 512
MOE_SUB = 128
MOE_TILE = 256
SPLIT_W = 256

VMEM_LIMIT = 56 * 1024 * 1024
HI = lax.Precision.HIGHEST
F32 = jnp.float32
BF16 = jnp.bfloat16


def _cparams(*sem):
    return pltpu.CompilerParams(dimension_semantics=sem, vmem_limit_bytes=VMEM_LIMIT)


def _norm_modulate(x, g, shift, scale):
    ms = jnp.mean(x * x, axis=-1, keepdims=True)
    return x * lax.rsqrt(ms + EPS) * g * (1.0 + scale) + shift


def _adaln_kernel(c_ref, w_ref, b_ref, o_ref):
    c = c_ref[...]
    c_act = c * jax.nn.sigmoid(c)
    o_ref[0] = jnp.dot(c_act, w_ref[0], preferred_element_type=F32, precision=HI) + b_ref[0]


def _adaln(c_pad, ada_w, ada_b):
    depth, d, n = ada_w.shape
    tn = 1536
    return pl.pallas_call(
        _adaln_kernel,
        grid=(depth, n // tn),
        in_specs=[
            pl.BlockSpec((SUBLANES, d), lambda i, j: (0, 0)),
            pl.BlockSpec((1, d, tn), lambda i, j: (i, 0, j)),
            pl.BlockSpec((1, 1, tn), lambda i, j: (i, 0, j)),
        ],
        out_specs=pl.BlockSpec((1, SUBLANES, tn), lambda i, j: (i, 0, j)),
        out_shape=jax.ShapeDtypeStruct((depth, SUBLANES, n), F32),
        compiler_params=_cparams("arbitrary", "arbitrary"),
        name="adaln",
    )(c_pad, ada_w, ada_b.reshape(depth, 1, n))


def _norm_matmul_kernel(x_ref, g_ref, sh_ref, sc_ref, *refs):
    w_refs, o_refs = refs[:len(refs) // 2], refs[len(refs) // 2:]
    h = _norm_modulate(x_ref[...], g_ref[...], sh_ref[0], sc_ref[0]).astype(BF16)
    for w_ref, o_ref in zip(w_refs, o_refs):
        o_ref[...] = jnp.dot(h, w_ref[...], preferred_element_type=F32)


def _norm_matmul(x, g, shift, scale, weights_bf16, seq, tm=512):
    t, d = x.shape
    per_b = seq // tm
    return pl.pallas_call(
        _norm_matmul_kernel,
        grid=(t // tm,),
        in_specs=[
            pl.BlockSpec((tm, d), lambda i: (i, 0)),
            pl.BlockSpec((1, d), lambda i: (0, 0)),
            pl.BlockSpec((1, 1, d), lambda i: (i // per_b, 0, 0)),
            pl.BlockSpec((1, 1, d), lambda i: (i // per_b, 0, 0)),
        ] + [pl.BlockSpec(w.shape, lambda i: (0, 0), pipeline_mode=pl.Buffered(1))
             for w in weights_bf16],
        out_specs=[pl.BlockSpec((tm, w.shape[1]), lambda i: (i, 0)) for w in weights_bf16],
        out_shape=[jax.ShapeDtypeStruct((t, w.shape[1]), F32) for w in weights_bf16],
        compiler_params=_cparams("arbitrary"),
        name="norm_matmul",
    )(x, g.reshape(1, d), shift, scale, *weights_bf16)


def _matmul_residual_kernel(y_ref, w_ref, x_ref, gate_ref, o_ref):
    acc = jnp.dot(y_ref[...], w_ref[...], preferred_element_type=F32)
    o_ref[...] = x_ref[...] + gate_ref[0] * acc


def _matmul_residual(y_bf16, w_bf16, x, gate, seq, tm=512):
    t, k = y_bf16.shape
    d = x.shape[1]
    per_b = seq // tm
    return pl.pallas_call(
        _matmul_residual_kernel,
        grid=(t // tm,),
        in_specs=[
            pl.BlockSpec((tm, k), lambda i: (i, 0)),
            pl.BlockSpec((k, d), lambda i: (0, 0)),
            pl.BlockSpec((tm, d), lambda i: (i, 0)),
            pl.BlockSpec((1, 1, d), lambda i: (i // per_b, 0, 0)),
        ],
        out_specs=pl.BlockSpec((tm, d), lambda i: (i, 0)),
        out_shape=jax.ShapeDtypeStruct((t, d), F32),
        compiler_params=_cparams("arbitrary"),
        name="matmul_residual",
    )(y_bf16, w_bf16, x, gate)


def _causal_conv_silu(cur, prev_tail, w, b):
    rows = lax.broadcasted_iota(jnp.int32, (SUBLANES, cur.shape[1]), 0)
    acc = b + w[SSM_CONV - 1:SSM_CONV] * cur
    for d in range(1, SSM_CONV):
        rolled = pltpu.roll(cur, d, axis=0)
        top = jnp.where(rows < d, pltpu.roll(prev_tail, d, axis=0), rolled[0:SUBLANES])
        shifted = jnp.concatenate([top, rolled[SUBLANES:]], axis=0)
        acc = acc + w[SSM_CONV - 1 - d:SSM_CONV - d] * shifted
    return acc * jax.nn.sigmoid(acc)


def _split3(x):
    hi = x.astype(BF16)
    rem = x - hi.astype(F32)
    mid = rem.astype(BF16)
    lo = (rem - mid.astype(F32)).astype(BF16)
    return jnp.concatenate([hi, mid, lo], axis=1)


def _ssd_kernel(z_ref, xs_ref, bc_ref, dt_ref, cwx_ref, cbx_ref, cwb_ref, cbb_ref, dtb_ref, alog_ref,
                dskip_ref, ng_ref, hexp_ref, lexp_ref, o_ref, tailx_ref, tailb_ref, state_ref):
    c = pl.program_id(1)

    @pl.when(c == 0)
    def _():
        tailx_ref[...] = jnp.zeros_like(tailx_ref)
        tailb_ref[...] = jnp.zeros_like(tailb_ref)
        state_ref[...] = jnp.zeros_like(state_ref)

    xs_raw = xs_ref[...]
    bc_raw = bc_ref[...]
    xs = _causal_conv_silu(xs_raw, tailx_ref[...], cwx_ref[...], cbx_ref[...])
    bc = _causal_conv_silu(bc_raw, tailb_ref[...], cwb_ref[...], cbb_ref[...])
    tailx_ref[...] = xs_raw[SSM_CHUNK - SUBLANES:]
    tailb_ref[...] = bc_raw[SSM_CHUNK - SUBLANES:]

    dt_in = dt_ref[...][:, :SSM_HEADS] + dtb_ref[...]
    dt = jnp.maximum(dt_in, 0.0) + jnp.log1p(jnp.exp(-jnp.abs(dt_in)))
    a_neg = -jnp.exp(alog_ref[...])
    d_a = dt * a_neg
    li = lax.broadcasted_iota(jnp.int32, (SSM_CHUNK, SSM_CHUNK), 0)
    si = lax.broadcasted_iota(jnp.int32, (SSM_CHUNK, SSM_CHUNK), 1)
    causal = li >= si
    tri = causal.astype(F32)
    a_cs = jnp.dot(tri, d_a, preferred_element_type=F32, precision=HI)
    a_cs_t = lax.dot_general(d_a, tri, (((0,), (1,)), ((), ())),
                             preferred_element_type=F32, precision=HI)
    a_last = a_cs[SSM_CHUNK - 1:SSM_CHUNK]
    e_out = jnp.exp(a_cs)
    e_state = jnp.exp(a_last - a_cs) * dt
    small = jnp.concatenate([dt, e_out, e_state], axis=0)
    wide = jnp.dot(_split3(small), hexp_ref[...], preferred_element_type=F32)
    dt_w = wide[0:SSM_CHUNK]
    e_out_w = wide[SSM_CHUNK:2 * SSM_CHUNK]
    e_state_w = wide[2 * SSM_CHUNK:]
    a_col = jnp.dot(_split3(a_cs), lexp_ref[...], preferred_element_type=F32)

    x_dt = (xs * dt_w).astype(BF16)
    x_state = (xs * e_state_w).astype(BF16)
    chunk_decay_w = e_out_w[SSM_CHUNK - 1:SSM_CHUNK]

    heads_per_group = SSM_HEADS // SSM_GROUPS
    y_parts = []
    for g in range(SSM_GROUPS):
        b_g = bc[:, g * SSM_STATE:(g + 1) * SSM_STATE].astype(BF16)
        c_g = bc[:, SSM_GN + g * SSM_STATE:SSM_GN + (g + 1) * SSM_STATE].astype(BF16)
        cb = lax.dot_general(c_g, b_g, (((1,), (1,)), ((), ())), preferred_element_type=F32)
        gsl = slice(g * SSM_GROUP_W, (g + 1) * SSM_GROUP_W)
        h_prev = state_ref[g]
        y_off = jnp.dot(c_g, h_prev.astype(BF16), preferred_element_type=F32) * e_out_w[:, gsl]
        diag = []
        for r in range(heads_per_group):
            h = g * heads_per_group + r
            seg = a_col[:, h * SSM_CHUNK:(h + 1) * SSM_CHUNK] - a_cs_t[h:h + 1, :]
            decay = jnp.exp(jnp.where(causal, seg, -jnp.inf))
            m = (cb * decay).astype(BF16)
            diag.append(jnp.dot(m, x_dt[:, h * SSM_HEAD_DIM:(h + 1) * SSM_HEAD_DIM],
                                preferred_element_type=F32))
        y_parts.append(jnp.concatenate(diag, axis=1) + y_off)
        upd = lax.dot_general(b_g, x_state[:, gsl], (((0,), (0,)), ((), ())), preferred_element_type=F32)
        state_ref[g] = h_prev * chunk_decay_w[:, gsl] + upd

    y = jnp.concatenate(y_parts, axis=1) + dskip_ref[...] * xs
    z = z_ref[...]
    y = y * (z * jax.nn.sigmoid(z))
    normed = []
    for g in range(SSM_GROUPS):
        y_g = y[:, g * SSM_GROUP_W:(g + 1) * SSM_GROUP_W]
        normed.append(y_g * lax.rsqrt(jnp.mean(y_g * y_g, axis=-1, keepdims=True) + EPS))
    o_ref[...] = (jnp.concatenate(normed, axis=1) * ng_ref[...]).astype(o_ref.dtype)


def _ssd_mixer(zxbc, dt_raw, conv_w, conv_b, dt_bias, a_log, d_skip, norm_g, batch, seq):
    t = zxbc.shape[0]
    nc = seq // SSM_CHUNK
    head_expand = jnp.tile(jnp.repeat(jnp.eye(SSM_HEADS, dtype=BF16), SSM_HEAD_DIM, axis=1), (3, 1))
    lane_expand = jnp.tile(jnp.repeat(jnp.eye(SSM_HEADS, dtype=BF16), SSM_CHUNK, axis=1), (3, 1))
    row = lambda b, c: (b * nc + c, 0)
    const2 = lambda b, c: (0, 0)
    bc_w = 2 * SSM_GN
    return pl.pallas_call(
        _ssd_kernel,
        grid=(batch, nc),
        in_specs=[
            pl.BlockSpec((SSM_CHUNK, SSM_D_INNER), row),
            pl.BlockSpec((SSM_CHUNK, SSM_D_INNER), lambda b, c: (b * nc + c, 1)),
            pl.BlockSpec((SSM_CHUNK, bc_w), lambda b, c: (b * nc + c, 2 * SSM_D_INNER // bc_w)),
            pl.BlockSpec((SSM_CHUNK, LANES), row),
            pl.BlockSpec((SSM_CONV, SSM_D_INNER), const2),
            pl.BlockSpec((1, SSM_D_INNER), const2),
            pl.BlockSpec((SSM_CONV, bc_w), const2),
            pl.BlockSpec((1, bc_w), const2),
            pl.BlockSpec((1, SSM_HEADS), const2),
            pl.BlockSpec((1, SSM_HEADS), const2),
            pl.BlockSpec((1, SSM_D_INNER), const2),
            pl.BlockSpec((1, SSM_D_INNER), const2),
            pl.BlockSpec((3 * SSM_HEADS, SSM_D_INNER), const2),
            pl.BlockSpec((3 * SSM_HEADS, SSM_HEADS * SSM_CHUNK), const2),
        ],
        out_specs=pl.BlockSpec((SSM_CHUNK, SSM_D_INNER), row),
        out_shape=jax.ShapeDtypeStruct((t, SSM_D_INNER), BF16),
        scratch_shapes=[
            pltpu.VMEM((SUBLANES, SSM_D_INNER), F32),
            pltpu.VMEM((SUBLANES, bc_w), F32),
            pltpu.VMEM((SSM_GROUPS, SSM_STATE, SSM_GROUP_W), F32),
        ],
        compiler_params=_cparams("arbitrary", "arbitrary"),
        name="ssd_mixer",
    )(zxbc, zxbc, zxbc, dt_raw,
      conv_w[:, :SSM_D_INNER], conv_b[:SSM_D_INNER].reshape(1, -1),
      conv_w[:, SSM_D_INNER:], conv_b[SSM_D_INNER:].reshape(1, -1),
      dt_bias.reshape(1, -1), a_log.reshape(1, -1),
      jnp.repeat(d_skip, SSM_HEAD_DIM).reshape(1, -1), norm_g.reshape(1, -1),
      head_expand, lane_expand)


def _head_rms(x, g):
    return x * lax.rsqrt(jnp.mean(x * x, axis=-1, keepdims=True) + EPS) * g


def _swa_kernel(q_ref, kvc_ref, kvp_ref, bucket_ref, qg_ref, kg_ref, rel_ref, sink_ref, o_ref,
                bias_ref, sinkrow_ref):
    b = pl.program_id(0)
    i = pl.program_id(1)

    @pl.when((b == 0) & (i == 0))
    def _():
        bucket = bucket_ref[...]
        kj = lax.broadcasted_iota(jnp.int32, (2 * WINDOW, WINDOW), 0)
        qi = lax.broadcasted_iota(jnp.int32, (2 * WINDOW, WINDOW), 1)
        dist = qi + WINDOW - kj
        band = (dist >= 0) & (dist < WINDOW)
        for h in range(ATTN_Q_HEADS):
            g, r = divmod(h, ATTN_Q_PER_KV)
            acc = jnp.zeros(bucket.shape, F32)
            for k in range(REL_BUCKETS):
                acc = jnp.where(bucket == k, rel_ref[k, h], acc)
            cols = slice(r * WINDOW, (r + 1) * WINDOW)
            bias_ref[0, g, :, cols] = jnp.where(band, acc, -jnp.inf)
            bias_ref[1, g, :, cols] = jnp.where(band & (kj >= WINDOW), acc, -jnp.inf)
            sinkrow_ref[g, :, cols] = jnp.full((1, WINDOW), sink_ref[h], F32)

    first = (i == 0).astype(jnp.int32)
    q_t = q_ref[...].T
    kv_c = kvc_ref[...]
    kv_p = kvp_ref[...]
    kv_w = ATTN_KV_HEADS * ATTN_HEAD_DIM
    q_gain = qg_ref[...]
    outs = []
    for g in range(ATTN_KV_HEADS):
        ksl = slice(g * ATTN_HEAD_DIM, (g + 1) * ATTN_HEAD_DIM)
        vsl = slice(kv_w + g * ATTN_HEAD_DIM, kv_w + (g + 1) * ATTN_HEAD_DIM)
        k = jnp.concatenate([kv_p[:, ksl], kv_c[:, ksl]], axis=0)
        v = jnp.concatenate([kv_p[:, vsl], kv_c[:, vsl]], axis=0).astype(BF16)
        k = _head_rms(k, kg_ref[...]).astype(BF16)
        q_heads = []
        for r in range(ATTN_Q_PER_KV):
            h = g * ATTN_Q_PER_KV + r
            q_h = q_t[h * ATTN_HEAD_DIM:(h + 1) * ATTN_HEAD_DIM]
            inv = lax.rsqrt(jnp.mean(q_h * q_h, axis=0, keepdims=True) + EPS)
            q_heads.append((q_h * inv * q_gain).astype(BF16))
        q = jnp.concatenate(q_heads, axis=1)
        s = jnp.dot(k, q, preferred_element_type=F32) + bias_ref[first, g]
        sink = sinkrow_ref[g]
        m = jnp.maximum(jnp.max(s, axis=0, keepdims=True), sink)
        p = jnp.exp(s - m)
        denom = jnp.sum(p, axis=0, keepdims=True) + jnp.exp(sink - m)
        pv = lax.dot_general(v, p.astype(BF16), (((0,), (0,)), ((), ())), preferred_element_type=F32)
        pv = pv * (1.0 / denom)
        outs.extend(pv[:, r * WINDOW:(r + 1) * WINDOW] for r in range(ATTN_Q_PER_KV))
    o_ref[...] = jnp.concatenate(outs, axis=0).T.astype(o_ref.dtype)


def _t5_causal_bucket(dist):
    max_exact = REL_BUCKETS // 2
    d = jnp.maximum(dist, 1).astype(F32)
    large = max_exact + (jnp.log(d / max_exact) / math.log(REL_MAX_DIST / max_exact)
                         * (REL_BUCKETS - max_exact)).astype(jnp.int32)
    large = jnp.minimum(large, REL_BUCKETS - 1)
    return jnp.where(dist < max_exact, dist, large)


def _swa_mixer(qkv, q_norm_g, k_norm_g, sinks, rel_bias, batch, seq):
    t = qkv.shape[0]
    nb = seq // WINDOW
    q_w = ATTN_Q_HEADS * ATTN_HEAD_DIM
    kv_w2 = 2 * ATTN_KV_HEADS * ATTN_HEAD_DIM
    kj = jnp.arange(2 * WINDOW)[:, None]
    qi = jnp.arange(WINDOW)[None, :]
    bucket = _t5_causal_bucket(jnp.maximum(qi + WINDOW - kj, 0)).astype(jnp.int32)
    q_gain = jnp.broadcast_to((q_norm_g * (ATTN_HEAD_DIM ** -0.5))[:, None], (ATTN_HEAD_DIM, WINDOW))
    const2 = lambda b, i: (0, 0)
    smem = pl.BlockSpec(memory_space=pltpu.SMEM)
    return pl.pallas_call(
        _swa_kernel,
        grid=(batch, nb),
        in_specs=[
            pl.BlockSpec((WINDOW, q_w), lambda b, i: (b * nb + i, 0)),
            pl.BlockSpec((WINDOW, kv_w2), lambda b, i: (b * nb + i, q_w // kv_w2)),
            pl.BlockSpec((WINDOW, kv_w2), lambda b, i: (b * nb + jnp.maximum(i - 1, 0), q_w // kv_w2)),
            pl.BlockSpec((2 * WINDOW, WINDOW), const2),
            pl.BlockSpec((ATTN_HEAD_DIM, WINDOW), const2),
            pl.BlockSpec((1, ATTN_HEAD_DIM), const2),
            smem,
            smem,
        ],
        out_specs=pl.BlockSpec((WINDOW, q_w), lambda b, i: (b * nb + i, 0)),
        out_shape=jax.ShapeDtypeStruct((t, q_w), BF16),
        scratch_shapes=[pltpu.VMEM((2, ATTN_KV_HEADS, 2 * WINDOW, ATTN_Q_PER_KV * WINDOW), F32),
                        pltpu.VMEM((ATTN_KV_HEADS, 1, ATTN_Q_PER_KV * WINDOW), F32)],
        compiler_params=_cparams("arbitrary", "arbitrary"),
        name="swa_mixer",
    )(qkv, qkv, qkv, bucket, q_gain, k_norm_g.reshape(1, -1), rel_bias, sinks)


def _router_kernel(x_ref, g_ref, sh_ref, sc_ref, rwt_ref, rb_ref,
                   h_ref, lp_ref, gate_ref, tile_n_ref, tile_carry_ref, cnt_ref, carry_ref):
    i = pl.program_id(0)
    tm = x_ref.shape[0]

    @pl.when(i == 0)
    def _():
        carry_ref[...] = jnp.zeros_like(carry_ref)

    h = _norm_modulate(x_ref[...], g_ref[...], sh_ref[0], sc_ref[0]).astype(BF16)
    h_ref[...] = h

    logits = lax.dot_general(rwt_ref[...], h, (((1,), (1,)), ((), ())),
                             preferred_element_type=F32) + rb_ref[...]
    e_iota = lax.broadcasted_iota(jnp.int32, logits.shape, 0)
    work = logits
    sels, vals = [], []
    for k in range(TOP_K):
        m = jnp.max(work, axis=0, keepdims=True)
        idx = jnp.min(jnp.where(work == m, e_iota, N_EXPERTS), axis=0, keepdims=True)
        sel = e_iota == idx
        work = jnp.where(sel, -jnp.inf, work)
        sels.append(sel)
        vals.append(m)
    exps = [jnp.exp(v - vals[0]) for v in vals]
    denom = exps[0] + exps[1] + exps[2] + exps[3]
    for k in range(TOP_K):
        gate_ref[k:k + 1, :] = exps[k] / denom

    chosen = sels[0] | sels[1] | sels[2] | sels[3]
    t_row = lax.broadcasted_iota(jnp.int32, (tm, tm), 0)
    t_col = lax.broadcasted_iota(jnp.int32, (tm, tm), 1)
    before = (t_row < t_col).astype(BF16)
    prior = jnp.dot(chosen.astype(BF16), before, preferred_element_type=F32)
    n = jnp.sum(chosen.astype(F32), axis=1, keepdims=True)
    e_row = lax.broadcasted_iota(jnp.int32, (N_EXPERTS, N_EXPERTS), 0)
    e_col = lax.broadcasted_iota(jnp.int32, (N_EXPERTS, N_EXPERTS), 1)
    run_start = jnp.dot((e_col < e_row).astype(BF16), jnp.broadcast_to(n, (N_EXPERTS, LANES)).astype(BF16),
                        preferred_element_type=F32)[:, :1]
    local = run_start + prior
    for k in range(TOP_K):
        lp_ref[k:k + 1, :] = jnp.sum(jnp.where(sels[k], local, 0.0), axis=0, keepdims=True).astype(jnp.int32)
    tile_n_ref[0] = n.astype(jnp.int32)
    tile_carry_ref[0] = carry_ref[...].astype(jnp.int32)
    total = carry_ref[...] + n
    carry_ref[...] = total
    cnt_ref[...] = total.astype(jnp.int32)


def _router(x, g, shift, scale, r_w, r_b, seq, tm):
    t, d = x.shape
    per_b = seq // tm
    ntiles = t // tm
    return pl.pallas_call(
        _router_kernel,
        grid=(t // tm,),
        in_specs=[
            pl.BlockSpec((tm, d), lambda i: (i, 0)),
            pl.BlockSpec((1, d), lambda i: (0, 0)),
            pl.BlockSpec((1, 1, d), lambda i: (i // per_b, 0, 0)),
            pl.BlockSpec((1, 1, d), lambda i: (i // per_b, 0, 0)),
            pl.BlockSpec((N_EXPERTS, d), lambda i: (0, 0)),
            pl.BlockSpec((N_EXPERTS, 1), lambda i: (0, 0)),
        ],
        out_specs=[
            pl.BlockSpec((tm, d), lambda i: (i, 0)),
            pl.BlockSpec((TOP_K, tm), lambda i: (0, i)),
            pl.BlockSpec((TOP_K, tm), lambda i: (0, i)),
            pl.BlockSpec((1, N_EXPERTS, 1), lambda i: (i, 0, 0)),
            pl.BlockSpec((1, N_EXPERTS, 1), lambda i: (i, 0, 0)),
            pl.BlockSpec((N_EXPERTS, 1), lambda i: (0, 0)),
        ],
        out_shape=[
            jax.ShapeDtypeStruct((t, d), BF16),
            jax.ShapeDtypeStruct((TOP_K, t), jnp.int32),
            jax.ShapeDtypeStruct((TOP_K, t), F32),
            jax.ShapeDtypeStruct((ntiles, N_EXPERTS, 1), jnp.int32),
            jax.ShapeDtypeStruct((ntiles, N_EXPERTS, 1), jnp.int32),
            jax.ShapeDtypeStruct((N_EXPERTS, 1), jnp.int32),
        ],
        scratch_shapes=[pltpu.VMEM((N_EXPERTS, 1), F32)],
        compiler_params=_cparams("arbitrary"),
        name="moe_router",
    )(x, g.reshape(1, d), shift, scale, r_w.T.astype(BF16), r_b.reshape(-1, 1))


def _run_copies(n, src_ref, src_row, dst_ref, dst_row, sem, max_rows, fn):
    for b in range(max_rows.bit_length()):
        size = 1 << b

        @pl.when((n & size) != 0)
        def _():
            lo = n & (size - 1)
            fn(pltpu.make_async_copy(
                src_ref.at[pl.ds(pl.multiple_of((src_row + lo) * ROW_TILES, ROW_TILES), size * ROW_TILES)],
                dst_ref.at[pl.ds(pl.multiple_of((dst_row + lo) * ROW_TILES, ROW_TILES), size * ROW_TILES)],
                sem), b)


def _start_piece(cp, b):
    cp.start()


def _wait_piece(cp, b):
    cp.wait()


def _tile_runs(tile, n_ref, carry_ref, gstart_ref, sorted_ref, grouped_ref, sem, tm, fn, to_grouped, live=None):
    def body(e, run_start):
        n = n_ref[tile * N_EXPERTS + e]
        if live is not None:
            n = jnp.where(live, n, 0)
        slot0 = gstart_ref[e] + carry_ref[tile * N_EXPERTS + e]
        if to_grouped:
            _run_copies(n, sorted_ref, run_start, grouped_ref, slot0, sem, tm, fn)
        else:
            _run_copies(n, grouped_ref, slot0, sorted_ref, run_start, sem, tm, fn)
        return run_start + n

    if live is None:
        lax.fori_loop(0, N_EXPERTS, body, 0)
    else:
        run_start = 0
        for e in range(N_EXPERTS):
            run_start = body(e, run_start)


def _block_copy(src_ref, dst_ref, dst_blk, sem):
    rows = MOE_BLOCK * ROW_TILES
    return pltpu.make_async_copy(src_ref, dst_ref.at[pl.ds(pl.multiple_of(dst_blk * rows, rows), rows)], sem)


def _dispatch_kernel(n_ref, carry_ref, gstart_ref, pad_lo_ref, pad_hi_ref, nused_ref, lp_ref, h_ref, xg_ref,
                     sorted0_ref, sorted1_ref, sorted2_ref, zero_ref, run_sem, sem):
    i = pl.program_id(0)
    last = pl.num_programs(0) - 1
    tm = h_ref.shape[0]
    rows = TOP_K * tm
    nblk =xg_ref.shape[0] // (MOE_BLOCK * ROW_TILES)

    @pl.when(i == 0)
    def _():
        zero_ref[...] = jnp.zeros_like(zero_ref)

        def fill(b, carry):
            _block_copy(zero_ref, xg_ref, b, sem).start()
            return carry

        def fill_wait(b, carry):
            _block_copy(zero_ref, xg_ref, b, sem).wait()
            return carry

        lax.fori_loop(nused_ref[0], nblk, fill, 0)
        lax.fori_loop(nused_ref[0], nblk, fill_wait, 0)

    @pl.when(i < N_EXPERTS)
    def _():
        n_pad = pad_hi_ref[i] - pad_lo_ref[i]
        for fn in (_start_piece, _wait_piece):
            _run_copies(n_pad, zero_ref, 0, xg_ref, pad_lo_ref[i], sem, MOE_BLOCK - 1, fn)

    bufs = (sorted0_ref, sorted1_ref, sorted2_ref)
    nbuf = len(bufs)

    def runs(tile, b, fn, live=None):
        _tile_runs(tile, n_ref, carry_ref, gstart_ref, bufs[b], xg_ref, run_sem.at[b], tm, fn, True, live)

    def wait_tile(b):
        pltpu.make_async_copy(bufs[b], xg_ref.at[pl.ds(0, rows * ROW_TILES)], run_sem.at[b]).wait()

    def step(cur):
        prev, oldest = (cur - 1) % nbuf, (cur + 1) % nbuf
        cur_ref = bufs[cur]

        @pl.when(i >= nbuf)
        def _():
            wait_tile(cur)

        row_id = lax.broadcasted_iota(jnp.int32, (rows, tm), 0)
        perm = row_id == lp_ref[0:1, :]
        for k in range(1, TOP_K):
            perm = perm | (row_id == lp_ref[k:k + 1, :])
        srt = jnp.dot(jnp.where(perm, 1.0, 0.0).astype(BF16), h_ref[...], preferred_element_type=F32)
        for s in range(ROW_TILES):
            cur_ref[pl.ds(s, rows, stride=ROW_TILES), :] = srt[:, s * LANES:(s + 1) * LANES]

        runs(jnp.maximum(i - 1, 0), prev, _start_piece, live=i >= 1)

        @pl.when(i == last)
        def _():
            @pl.when(i >= 2)
            def _():
                wait_tile(oldest)

            @pl.when(i >= 1)
            def _():
                wait_tile(prev)

            runs(i, cur, _start_piece)
            wait_tile(cur)

    for r in range(nbuf):
        @pl.when(i % nbuf == r)
        def _():
            step(r)


def _dispatch(tile_n, tile_carry, group_start, pad_lo, pad_hi, nused, lp_t, h, cap, tm):
    t, d = h.shape
    assert t // tm >= N_EXPERTS
    grid_spec = pltpu.PrefetchScalarGridSpec(
        num_scalar_prefetch=6,
        grid=(t // tm,),
        in_specs=[
            pl.BlockSpec((TOP_K, tm), lambda i, *_: (0, i)),
            pl.BlockSpec((tm, d), lambda i, *_: (i, 0)),
        ],
        out_specs=pl.BlockSpec(memory_space=pl.ANY),
        scratch_shapes=[
            pltpu.VMEM((TOP_K * tm * ROW_TILES, LANES), F32),
            pltpu.VMEM((TOP_K * tm * ROW_TILES, LANES), F32),
            pltpu.VMEM((TOP_K * tm * ROW_TILES, LANES), F32),
            pltpu.VMEM((MOE_BLOCK * ROW_TILES, LANES), F32),
            pltpu.SemaphoreType.DMA((3,)), pltpu.SemaphoreType.DMA,
        ],
    )
    return pl.pallas_call(
        _dispatch_kernel,
        grid_spec=grid_spec,
        out_shape=jax.ShapeDtypeStruct((cap * ROW_TILES, LANES), F32),
        compiler_params=pltpu.CompilerParams(dimension_semantics=("arbitrary",), has_side_effects=True,
                                             vmem_limit_bytes=VMEM_LIMIT),
        name="moe_dispatch",
    )(tile_n, tile_carry, group_start, pad_lo, pad_hi, nused, lp_t, h)


def _weight_copies(w1_hbm, w2_hbm, w1buf, w2buf, sems, layer, e, slot):
    return (pltpu.make_async_copy(w1_hbm.at[layer, e], w1buf.at[slot], sems.at[0, slot]),
            pltpu.make_async_copy(w2_hbm.at[layer, e], w2buf.at[slot], sems.at[1, slot]))


def _expert_kernel(blk_e_ref, nused_ref, next_e_ref, slot_ref, blk_rows_ref, x_ref, w1_hbm, w2_hbm, bg_ref, bl_ref, b2_ref,
                   sel_ref, o_ref, w1buf, w2buf, wg_s, wl_s, w2_s, sems, *, layer):
    blk = pl.program_id(0)
    active = blk < nused_ref[0]
    e = blk_e_ref[blk]
    slot = slot_ref[e]
    new_expert = (blk == 0) | (e != blk_e_ref[jnp.maximum(blk - 1, 0)])
    copies = functools.partial(_weight_copies, w1_hbm, w2_hbm, w1buf, w2buf, sems, layer)

    @pl.when(blk == 0)
    def _():
        for cp in copies(e, slot):
            cp.start()

    @pl.when(active & new_expert)
    def _():
        nxt = next_e_ref[e]

        @pl.when(nxt >= 0)
        def _():
            for cp in copies(nxt, 1 - slot):
                cp.start()

        for cp in copies(e, slot):
            cp.wait()
        half = SPLIT_W // 2
        for j in range(w1buf.shape[2] // SPLIT_W):
            chunk = w1buf[slot, :, j * SPLIT_W:(j + 1) * SPLIT_W].astype(BF16)
            split = jnp.dot(chunk, sel_ref[...], preferred_element_type=F32)
            wg_s[:, j * half:(j + 1) * half] = split[:, :half].astype(BF16)
            wl_s[:, j * half:(j + 1) * half] = split[:, half:].astype(BF16)
        w2_s[...] = w2buf[slot].astype(BF16)

    def mlp(rows):
        x = jnp.concatenate([x_ref[pl.ds(s, rows, stride=ROW_TILES), :]
                             for s in range(ROW_TILES)], axis=1).astype(BF16)
        glu = jnp.dot(x, wg_s[...], preferred_element_type=F32) + bg_ref[0]
        lin = jnp.dot(x, wl_s[...], preferred_element_type=F32) + bl_ref[0]
        glu = jnp.minimum(glu, SWIGLU_LIMIT)
        lin = jnp.clip(lin, -SWIGLU_LIMIT, SWIGLU_LIMIT)
        act = glu * jax.nn.sigmoid(SWIGLU_ALPHA * glu) * (lin + 1.0)
        y = jnp.dot(act.astype(BF16), w2_s[...], preferred_element_type=F32) + b2_ref[0]
        for s in range(ROW_TILES):
            o_ref[pl.ds(s, rows, stride=ROW_TILES), :] = y[:, s * LANES:(s + 1) * LANES]
        if rows < MOE_BLOCK:
            o_ref[pl.ds(rows * ROW_TILES, (MOE_BLOCK - rows) * ROW_TILES), :] = jnp.zeros(
                ((MOE_BLOCK - rows) * ROW_TILES, LANES), o_ref.dtype)

    valid_rows = jnp.where(active, blk_rows_ref[blk], 0)
    for rows in range(MOE_SUB, MOE_BLOCK + 1, MOE_SUB):
        @pl.when((valid_rows > rows - MOE_SUB) & (valid_rows <= rows))
        def _():
            mlp(rows)

    @pl.when(valid_rows <= 0)
    def _():
        o_ref[...] = jnp.zeros_like(o_ref)


def _experts(layer, blk_e, nused, next_e, e_slot, blk_rows, xg_rows, w1_all, w2_all, b_glu, b_lin, b2):
    nblk = blk_e.shape[0]
    d = D_MODEL
    ff = w2_all.shape[2]
    col = jnp.arange(SPLIT_W)
    sel = (jnp.arange(SPLIT_W)[None, :] == ((col % 2) * (SPLIT_W // 2) + col // 2)[:, None]).astype(BF16)

    def blk_map(b, blk_e_ref, nused_ref, *_):
        return (jnp.minimum(b, nused_ref[0] - 1), 0)

    def e_map(b, blk_e_ref, nused_ref, *_):
        return (blk_e_ref[jnp.minimum(b, nused_ref[0] - 1)], 0, 0)

    grid_spec = pltpu.PrefetchScalarGridSpec(
        num_scalar_prefetch=5,
        grid=(nblk,),
        in_specs=[
            pl.BlockSpec((MOE_BLOCK * ROW_TILES, LANES), blk_map),
            pl.BlockSpec(memory_space=pl.ANY),
            pl.BlockSpec(memory_space=pl.ANY),
            pl.BlockSpec((1, 1, ff), e_map),
            pl.BlockSpec((1, 1, ff), e_map),
            pl.BlockSpec((1, 1, d), e_map),
            pl.BlockSpec((SPLIT_W, SPLIT_W), lambda b, *_: (0, 0)),
        ],
        out_specs=pl.BlockSpec((MOE_BLOCK * ROW_TILES, LANES), lambda b, *_: (b, 0)),
        scratch_shapes=[
            pltpu.VMEM((2, d, 2 * ff), F32), pltpu.VMEM((2, ff, d), F32),
            pltpu.VMEM((d, ff), BF16), pltpu.VMEM((d, ff), BF16), pltpu.VMEM((ff, d), BF16),
            pltpu.SemaphoreType.DMA((2, 2)),
        ],
    )
    return pl.pallas_call(
        functools.partial(_expert_kernel, layer=layer),
        grid_spec=grid_spec,
        out_shape=jax.ShapeDtypeStruct(xg_rows.shape, F32),
        compiler_params=_cparams("arbitrary"),
        name="moe_experts",
    )(blk_e, nused, next_e, e_slot, blk_rows, xg_rows, w1_all, w2_all, b_glu, b_lin, b2, sel)


def _combine_kernel(n_ref, carry_ref, gstart_ref, yg_ref, lp_ref, gate_ref, x_ref, g2_ref, o_ref, sorted_ref, sem):
    i = pl.program_id(0)
    ntiles = pl.num_programs(0)
    tm = x_ref.shape[0]
    rows = TOP_K * tm
    slot = i % 2

    def runs(tile, buf, fn, live=None):
        _tile_runs(tile, n_ref, carry_ref, gstart_ref, sorted_ref.at[buf], yg_ref, sem.at[buf], tm, fn, False, live)

    @pl.when(i == 0)
    def _():
        runs(0, 0, _start_piece)

    runs(jnp.minimum(i + 1, ntiles - 1), 1 - slot, _start_piece, live=i + 1 < ntiles)

    eye = (lax.broadcasted_iota(jnp.int32, (tm, tm), 0) ==
           lax.broadcasted_iota(jnp.int32, (tm, tm), 1)).astype(F32)
    rows_t = jnp.concatenate([gate_ref[...], lp_ref[...].astype(F32)], axis=0)
    cols = lax.dot_general(eye, rows_t, (((1,), (1,)), ((), ())), preferred_element_type=F32, precision=HI)
    col_id = lax.broadcasted_iota(jnp.int32, (tm, rows), 1)
    weights = jnp.zeros((tm, rows), F32)
    for k in range(TOP_K):
        lp_col = (cols[:, TOP_K + k:TOP_K + k + 1] + 0.5).astype(jnp.int32)
        weights = weights + jnp.where(col_id == lp_col, cols[:, k:k + 1], 0.0)

    pltpu.make_async_copy(yg_ref.at[pl.ds(0, rows * ROW_TILES)], sorted_ref.at[slot], sem.at[slot]).wait()
    y = jnp.concatenate(
        [sorted_ref[slot, pl.ds(s, rows, stride=ROW_TILES), :] for s in range(ROW_TILES)], axis=1).astype(BF16)
    acc = jnp.dot(weights.astype(BF16), y, preferred_element_type=F32)
    o_ref[...] = x_ref[...] + g2_ref[0] * acc


def _combine(tile_n, tile_carry, group_start, yg_rows, lp_t, gate_t, x, gate2, seq, tm):
    t, d = x.shape
    per_b = seq // tm
    grid_spec = pltpu.PrefetchScalarGridSpec(
        num_scalar_prefetch=3,
        grid=(t // tm,),
        in_specs=[
            pl.BlockSpec(memory_space=pl.ANY),
            pl.BlockSpec((TOP_K, tm), lambda i, *_: (0, i)),
            pl.BlockSpec((TOP_K, tm), lambda i, *_: (0, i)),
            pl.BlockSpec((tm, d), lambda i, *_: (i, 0)),
            pl.BlockSpec((1, 1, d), lambda i, *_: (i // per_b, 0, 0)),
        ],
        out_specs=pl.BlockSpec((tm, d), lambda i, *_: (i, 0)),
        scratch_shapes=[pltpu.VMEM((2, TOP_K * tm * ROW_TILES, LANES), F32), pltpu.SemaphoreType.DMA((2,))],
    )
    return pl.pallas_call(
        _combine_kernel,
        grid_spec=grid_spec,
        out_shape=jax.ShapeDtypeStruct((t, d), F32),
        compiler_params=_cparams("arbitrary"),
        name="moe_combine",
    )(tile_n, tile_carry, group_start, yg_rows, lp_t, gate_t, x, gate2)


def _moe(layer, x, g, shift, scale, gate2, r_w, r_b, w1_all, b1_glu, b1_lin, w2_all, b2, seq):
    t = x.shape[0]
    h, lp_t, gate_t, tile_n, tile_carry, counts = _router(x, g, shift, scale, r_w, r_b, seq, MOE_TILE)
    tile_n = tile_n.reshape(-1)
    tile_carry = tile_carry.reshape(-1)
    counts = counts.reshape(-1)
    padded = ((counts + MOE_BLOCK - 1) // MOE_BLOCK) * MOE_BLOCK
    group_end = jnp.cumsum(padded)
    group_start = (group_end - padded).astype(jnp.int32)
    cap = t * TOP_K + N_EXPERTS * MOE_BLOCK
    nblk = cap // MOE_BLOCK
    blk_start = jnp.arange(nblk, dtype=jnp.int32) * MOE_BLOCK
    e_ids = jnp.arange(N_EXPERTS, dtype=jnp.int32)
    past = (blk_start[None, :] >= group_end[:, None]).astype(jnp.int32)
    blk_e = jnp.minimum(jnp.sum(past, axis=0), N_EXPERTS - 1).astype(jnp.int32)
    nused = (group_end[-1:] // MOE_BLOCK).astype(jnp.int32)

    pad_lo = (group_start + counts).astype(jnp.int32)
    pad_hi = group_end.astype(jnp.int32)
    blk_pad_lo = jnp.sum(jnp.where(blk_e[None, :] == e_ids[:, None], pad_lo[:, None], 0), axis=0)
    blk_rows = jnp.clip(blk_pad_lo - blk_start, 0, MOE_BLOCK).astype(jnp.int32)
    xg_rows = _dispatch(tile_n, tile_carry, group_start, pad_lo, pad_hi, nused, lp_t, h, cap, MOE_TILE)
    nonempty = counts > 0
    later = jnp.where(nonempty[None, :] & (e_ids[None, :] > e_ids[:, None]), e_ids[None, :], N_EXPERTS)
    next_e = jnp.min(later, axis=1)
    next_e = jnp.where(next_e == N_EXPERTS, -1, next_e).astype(jnp.int32)
    e_slot = ((jnp.cumsum(nonempty.astype(jnp.int32)) - 1) % 2).astype(jnp.int32)
    yg_rows = _experts(layer, blk_e, nused, next_e, e_slot, blk_rows, xg_rows, w1_all, w2_all,
                       b1_glu[:, None, :], b1_lin[:, None, :], b2[:, None, :])
    return _combine(tile_n, tile_carry, group_start, yg_rows, lp_t, gate_t, x, gate2, seq, MOE_TILE)


def kernel(x, c, ada_w, ada_b, norm1_g, norm2_g, m_in_w, m_conv_w, m_conv_b, m_dt_bias, m_A_log, m_D, m_norm_g, m_out_w, a_qkv_w, a_q_norm_g, a_k_norm_g, a_sinks, a_out_w, rel_bias, r_w, r_b, e_w1, e_b1, e_w2, e_b2):
    batch, seq, d = x.shape
    depth = ada_w.shape[0]
    t = batch * seq
    xf = x.reshape(t, d)

    c_pad = jnp.zeros((SUBLANES, d), F32).at[:batch].set(c)
    mod = _adaln(c_pad, ada_w, ada_b)[:, :batch]

    b1_split = jnp.moveaxis(e_b1.reshape(depth, N_EXPERTS, -1, 2), -1, 1)

    for i in range(depth):
        parts =[mod[i, :, p * d:(p + 1) * d].reshape(batch, 1, d) for p in range(6)]
        sh1, sc1, g1, sh2, sc2, g2 = parts
        j = i // 2
        if i % 2 == 0:
            w_zxbc = m_in_w[j, :, :SSM_ZXBC].astype(BF16)
            w_dt = jnp.pad(m_in_w[j, :, SSM_ZXBC:], ((0, 0), (0, LANES - SSM_HEADS))).astype(BF16)
            zxbc, dt_raw = _norm_matmul(xf, norm1_g[i], sh1, sc1, (w_zxbc, w_dt), seq)
            y = _ssd_mixer(zxbc, dt_raw, m_conv_w[j], m_conv_b[j], m_dt_bias[j], m_A_log[j], m_D[j],
                           m_norm_g[j], batch, seq)
            xf = _matmul_residual(y, m_out_w[j].astype(BF16), xf, g1, seq)
        else:
            qkv, = _norm_matmul(xf, norm1_g[i], sh1, sc1, (a_qkv_w[j].astype(BF16),), seq)
            y = _swa_mixer(qkv, a_q_norm_g[j], a_k_norm_g[j], a_sinks[j], rel_bias, batch, seq)
            xf = _matmul_residual(y, a_out_w[j].astype(BF16), xf, g1, seq)
        xf = _moe(i, xf, norm2_g[i], sh2, sc2, g2, r_w[i], r_b[i], e_w1, b1_split[i, 0], b1_split[i, 1],
                  e_w2, e_b2[i], seq)
    return xf.reshape(batch, seq, d)
```

```python
import functools
import math

import jax
import jax.numpy as jnp
from jax import lax
from jax.experimental import pallas as pl
from jax.experimental.pallas import tpu as pltpu

D_MODEL = 1024
EPS = 1e-6
LANES = 128
SUBLANES = 8
ROW_TILES = D_MODEL // LANES

SSM_D_INNER = 2048
SSM_HEAD_DIM = 64
SSM_HEADS = 32
SSM_GROUPS = 4
SSM_STATE = 128
SSM_CONV = 4
SSM_CHUNK = 128
SSM_GN = SSM_GROUPS * SSM_STATE
SSM_ZXBC = 2 * SSM_D_INNER + 2 * SSM_GN
SSM_GROUP_W = SSM_D_INNER // SSM_GROUPS

ATTN_HEAD_DIM = 64
ATTN_Q_HEADS = 16
ATTN_KV_HEADS = 4
ATTN_Q_PER_KV = 4
WINDOW = 128
REL_BUCKETS = 32
REL_MAX_DIST = 128

N_EXPERTS = 32
TOP_K = 4
SWIGLU_ALPHA = 1.702
SWIGLU_LIMIT = 7.0
MOE_BLOCK = 512
MOE_SUB = 128
MOE_TILE = 256
SPLIT_W = 256

VMEM_LIMIT = 56 * 1024 * 1024
HI = lax.Precision.HIGHEST
F32 = jnp.float32
BF16 = jnp.bfloat16


def _cparams(*sem):
    return pltpu.CompilerParams(dimension_semantics=sem, vmem_limit_bytes=VMEM_LIMIT)


def _norm_modulate(x, g, shift, scale):
    ms = jnp.mean(x * x, axis=-1, keepdims=True)
    return x * lax.rsqrt(ms + EPS) * g * (1.0 + scale) + shift


def _adaln_kernel(ct_ref, w_ref, b_ref, o_ref, *, batch):
    c_t = ct_ref[...]
    act_t = c_t * jax.nn.sigmoid(c_t)
    w = w_ref[0]
    rows = [jnp.sum(act_t[:, b:b + 1] * w, axis=0, keepdims=True) + b_ref[0] for b in range(batch)]
    rows.append(jnp.zeros((SUBLANES - batch, w.shape[1]), F32))
    o_ref[0] = jnp.concatenate(rows, axis=0)


def _adaln(c, ada_w, ada_b):
    depth, d, n = ada_w.shape
    batch = c.shape[0]
    tn = 1536
    c_t = jnp.zeros((d, SUBLANES), F32).at[:, :batch].set(c.T)
    return pl.pallas_call(
        functools.partial(_adaln_kernel, batch=batch),
        grid=(depth, n // tn),
        in_specs=[
            pl.BlockSpec((d, SUBLANES), lambda i, j: (0, 0)),
            pl.BlockSpec((1, d, tn), lambda i, j: (i, 0, j)),
            pl.BlockSpec((1, 1, tn), lambda i, j: (i, 0, j)),
        ],
        out_specs=pl.BlockSpec((1, SUBLANES, tn), lambda i, j: (i, 0, j)),
        out_shape=jax.ShapeDtypeStruct((depth, SUBLANES, n), F32),
        compiler_params=_cparams("arbitrary", "arbitrary"),
        name="adaln",
    )(c_t, ada_w, ada_b.reshape(depth, 1, n))


def _norm_matmul_kernel(x_ref, g_ref, sh_ref, sc_ref, *refs):
    w_refs, o_refs = refs[:len(refs) // 2], refs[len(refs) // 2:]
    h = _norm_modulate(x_ref[...], g_ref[...], sh_ref[0], sc_ref[0]).astype(BF16)
    for w_ref, o_ref in zip(w_refs, o_refs):
        o_ref[...] = jnp.dot(h, w_ref[...].astype(BF16), preferred_element_type=F32)


def _norm_matmul(x, g, shift, scale, weights_bf16, seq, tm=512):
    t, d = x.shape
    per_b = seq // tm
    return pl.pallas_call(
        _norm_matmul_kernel,
        grid=(t // tm,),
        in_specs=[
            pl.BlockSpec((tm, d), lambda i: (i, 0)),
            pl.BlockSpec((1, d), lambda i: (0, 0)),
            pl.BlockSpec((1, 1, d), lambda i: (i // per_b, 0, 0)),
            pl.BlockSpec((1, 1, d), lambda i: (i // per_b, 0, 0)),
        ] + [pl.BlockSpec(w.shape, lambda i: (0, 0), pipeline_mode=pl.Buffered(1))
             for w in weights_bf16],
        out_specs=[pl.BlockSpec((tm, w.shape[1]), lambda i: (i, 0)) for w in weights_bf16],
        out_shape=[jax.ShapeDtypeStruct((t, w.shape[1]), F32) for w in weights_bf16],
        compiler_params=_cparams("arbitrary"),
        name="norm_matmul",
    )(x, g.reshape(1, d), shift, scale, *weights_bf16)


def _matmul_residual_kernel(y_ref, w_ref, x_ref, gate_ref, o_ref):
    acc = jnp.dot(y_ref[...], w_ref[...], preferred_element_type=F32)
    o_ref[...] = x_ref[...] + gate_ref[0] * acc


def _matmul_residual(y_bf16, w_bf16, x, gate, seq, tm=512):
    t, k = y_bf16.shape
    d = x.shape[1]
    per_b = seq // tm
    return pl.pallas_call(
        _matmul_residual_kernel,
        grid=(t // tm,),
        in_specs=[
            pl.BlockSpec((tm, k), lambda i: (i, 0)),
            pl.BlockSpec((k, d), lambda i: (0, 0)),
            pl.BlockSpec((tm, d), lambda i: (i, 0)),
            pl.BlockSpec((1, 1, d), lambda i: (i // per_b, 0, 0)),
        ],
        out_specs=pl.BlockSpec((tm, d), lambda i: (i, 0)),
        out_shape=jax.ShapeDtypeStruct((t, d), F32),
        compiler_params=_cparams("arbitrary"),
        name="matmul_residual",
    )(y_bf16, w_bf16, x, gate)


def _causal_conv_silu(cur, prev_tail, w, b):
    rows = lax.broadcasted_iota(jnp.int32, (SUBLANES, cur.shape[1]), 0)
    acc = b + w[SSM_CONV - 1:SSM_CONV] * cur
    for d in range(1, SSM_CONV):
        rolled = pltpu.roll(cur, d, axis=0)
        top = jnp.where(rows < d, pltpu.roll(prev_tail, d, axis=0), rolled[0:SUBLANES])
        shifted = jnp.concatenate([top, rolled[SUBLANES:]], axis=0)
        acc = acc + w[SSM_CONV - 1 - d:SSM_CONV - d] * shifted
    return acc * jax.nn.sigmoid(acc)


def _split3(x):
    hi = x.astype(BF16)
    rem = x - hi.astype(F32)
    mid = rem.astype(BF16)
    lo = (rem - mid.astype(F32)).astype(BF16)
    return jnp.concatenate([hi, mid, lo], axis=1)


def _ssd_kernel(z_ref, xs_ref, bc_ref, dt_ref, cwx_ref, cbx_ref, cwb_ref, cbb_ref, dtb_ref, alog_ref,
                dskip_ref, ng_ref, hexp_ref, lexp_ref, o_ref, tailx_ref, tailb_ref, state_ref):
    c = pl.program_id(1)

    @pl.when(c == 0)
    def _():
        tailx_ref[...] = jnp.zeros_like(tailx_ref)
        tailb_ref[...] = jnp.zeros_like(tailb_ref)
        state_ref[...] = jnp.zeros_like(state_ref)

    xs_raw = xs_ref[...]
    bc_raw = bc_ref[...]
    xs = _causal_conv_silu(xs_raw, tailx_ref[...], cwx_ref[...], cbx_ref[...])
    bc = _causal_conv_silu(bc_raw, tailb_ref[...], cwb_ref[...], cbb_ref[...])
    tailx_ref[...] = xs_raw[SSM_CHUNK - SUBLANES:]
    tailb_ref[...] = bc_raw[SSM_CHUNK - SUBLANES:]

    dt_in = dt_ref[...][:, :SSM_HEADS] + dtb_ref[...]
    dt = jnp.maximum(dt_in, 0.0) + jnp.log1p(jnp.exp(-jnp.abs(dt_in)))
    a_neg = -jnp.exp(alog_ref[...])
    d_a = dt * a_neg
    li = lax.broadcasted_iota(jnp.int32, (SSM_CHUNK, SSM_CHUNK), 0)
    si = lax.broadcasted_iota(jnp.int32, (SSM_CHUNK, SSM_CHUNK), 1)
    causal = li >= si
    tri = causal.astype(F32)
    a_cs = jnp.dot(tri, d_a, preferred_element_type=F32, precision=HI)
    a_cs_t = lax.dot_general(d_a, tri, (((0,), (1,)), ((), ())),
                             preferred_element_type=F32, precision=HI)
    a_last = a_cs[SSM_CHUNK - 1:SSM_CHUNK]
    e_out = jnp.exp(a_cs)
    e_state = jnp.exp(a_last - a_cs) * dt
    small = jnp.concatenate([dt, e_out, e_state], axis=0)
    wide = jnp.dot(_split3(small), hexp_ref[...], preferred_element_type=F32)
    dt_w = wide[0:SSM_CHUNK]
    e_out_w = wide[SSM_CHUNK:2 * SSM_CHUNK]
    e_state_w = wide[2 * SSM_CHUNK:]
    a_col = jnp.dot(_split3(a_cs), lexp_ref[...], preferred_element_type=F32)

    x_dt = (xs * dt_w).astype(BF16)
    x_state = (xs * e_state_w).astype(BF16)
    chunk_decay_w = e_out_w[SSM_CHUNK - 1:SSM_CHUNK]

    heads_per_group = SSM_HEADS // SSM_GROUPS
    y_parts = []
    for g in range(SSM_GROUPS):
        b_g = bc[:, g * SSM_STATE:(g + 1) * SSM_STATE].astype(BF16)
        c_g = bc[:, SSM_GN + g * SSM_STATE:SSM_GN + (g + 1) * SSM_STATE].astype(BF16)
        cb = lax.dot_general(c_g, b_g, (((1,), (1,)), ((), ())), preferred_element_type=F32)
        gsl = slice(g * SSM_GROUP_W, (g + 1) * SSM_GROUP_W)
        h_prev = state_ref[g]
        y_off = jnp.dot(c_g, h_prev.astype(BF16), preferred_element_type=F32) * e_out_w[:, gsl]
        diag = []
        for r in range(heads_per_group):
            h = g * heads_per_group + r
            seg = a_col[:, h * SSM_CHUNK:(h + 1) * SSM_CHUNK] - a_cs_t[h:h + 1, :]
            decay = jnp.exp(jnp.where(causal, seg, -jnp.inf))
            m = (cb * decay).astype(BF16)
            diag.append(jnp.dot(m, x_dt[:, h * SSM_HEAD_DIM:(h + 1) * SSM_HEAD_DIM],
                                preferred_element_type=F32))
        y_parts.append(jnp.concatenate(diag, axis=1) + y_off)
        upd = lax.dot_general(b_g, x_state[:, gsl], (((0,), (0,)), ((), ())), preferred_element_type=F32)
        state_ref[g] = h_prev * chunk_decay_w[:, gsl] + upd

    y = jnp.concatenate(y_parts, axis=1) + dskip_ref[...] * xs
    z = z_ref[...]
    y = y * (z * jax.nn.sigmoid(z))
    normed = []
    for g in range(SSM_GROUPS):
        y_g = y[:, g * SSM_GROUP_W:(g + 1) * SSM_GROUP_W]
        normed.append(y_g * lax.rsqrt(jnp.mean(y_g * y_g, axis=-1, keepdims=True) + EPS))
    o_ref[...] = (jnp.concatenate(normed, axis=1) * ng_ref[...]).astype(o_ref.dtype)


def _ssd_mixer(zxbc, dt_raw, conv_w, conv_b, dt_bias, a_log, d_skip, norm_g, batch, seq):
    t = zxbc.shape[0]
    nc = seq // SSM_CHUNK
    head_expand = jnp.tile(jnp.repeat(jnp.eye(SSM_HEADS, dtype=BF16), SSM_HEAD_DIM, axis=1), (3, 1))
    lane_expand = jnp.tile(jnp.repeat(jnp.eye(SSM_HEADS, dtype=BF16), SSM_CHUNK, axis=1), (3, 1))
    row = lambda b, c: (b * nc + c, 0)
    const2 = lambda b, c: (0, 0)
    bc_w = 2 * SSM_GN
    return pl.pallas_call(
        _ssd_kernel,
        grid=(batch, nc),
        in_specs=[
            pl.BlockSpec((SSM_CHUNK, SSM_D_INNER), row),
            pl.BlockSpec((SSM_CHUNK, SSM_D_INNER), lambda b, c: (b * nc + c, 1)),
            pl.BlockSpec((SSM_CHUNK, bc_w), lambda b, c: (b * nc + c, 2 * SSM_D_INNER // bc_w)),
            pl.BlockSpec((SSM_CHUNK, LANES), row),
            pl.BlockSpec((SSM_CONV, SSM_D_INNER), const2),
            pl.BlockSpec((1, SSM_D_INNER), const2),
            pl.BlockSpec((SSM_CONV, bc_w), const2),
            pl.BlockSpec((1, bc_w), const2),
            pl.BlockSpec((1, SSM_HEADS), const2),
            pl.BlockSpec((1, SSM_HEADS), const2),
            pl.BlockSpec((1, SSM_D_INNER), const2),
            pl.BlockSpec((1, SSM_D_INNER), const2),
            pl.BlockSpec((3 * SSM_HEADS, SSM_D_INNER), const2),
            pl.BlockSpec((3 * SSM_HEADS, SSM_HEADS * SSM_CHUNK), const2),
        ],
        out_specs=pl.BlockSpec((SSM_CHUNK, SSM_D_INNER), row),
        out_shape=jax.ShapeDtypeStruct((t, SSM_D_INNER), BF16),
        scratch_shapes=[
            pltpu.VMEM((SUBLANES, SSM_D_INNER), F32),
            pltpu.VMEM((SUBLANES, bc_w), F32),
            pltpu.VMEM((SSM_GROUPS, SSM_STATE, SSM_GROUP_W), F32),
        ],
        compiler_params=_cparams("arbitrary", "arbitrary"),
        name="ssd_mixer",
    )(zxbc, zxbc, zxbc, dt_raw,
      conv_w[:, :SSM_D_INNER], conv_b[:SSM_D_INNER].reshape(1, -1),
      conv_w[:, SSM_D_INNER:], conv_b[SSM_D_INNER:].reshape(1, -1),
      dt_bias.reshape(1, -1), a_log.reshape(1, -1),
      jnp.repeat(d_skip, SSM_HEAD_DIM).reshape(1, -1), norm_g.reshape(1, -1),
      head_expand, lane_expand)


def _head_rms(x, g):
    return x * lax.rsqrt(jnp.mean(x * x, axis=-1, keepdims=True) + EPS) * g


def _swa_kernel(q_ref, kvc_ref, kvp_ref, bucket_ref, qg_ref, kg_ref, rel_ref, sink_ref, o_ref,
                bias_ref, sinkrow_ref):
    b = pl.program_id(0)
    i = pl.program_id(1)

    @pl.when((b == 0) & (i == 0))
    def _():
        bucket = bucket_ref[...]
        kj = lax.broadcasted_iota(jnp.int32, (2 * WINDOW, WINDOW), 0)
        qi = lax.broadcasted_iota(jnp.int32, (2 * WINDOW, WINDOW), 1)
        dist = qi + WINDOW - kj
        band = (dist >= 0) & (dist < WINDOW)
        for h in range(ATTN_Q_HEADS):
            g, r = divmod(h, ATTN_Q_PER_KV)
            acc = jnp.zeros(bucket.shape, F32)
            for k in range(REL_BUCKETS):
                acc = jnp.where(bucket == k, rel_ref[k, h], acc)
            cols = slice(r * WINDOW, (r + 1) * WINDOW)
            bias_ref[0, g, :, cols] = jnp.where(band, acc, -jnp.inf)
            bias_ref[1, g, :, cols] = jnp.where(band & (kj >= WINDOW), acc, -jnp.inf)
            sinkrow_ref[g, :, cols] = jnp.full((1, WINDOW), sink_ref[h], F32)

    first = (i == 0).astype(jnp.int32)
    q_t = q_ref[...].T
    kv_c = kvc_ref[...]
    kv_p = kvp_ref[...]
    kv_w = ATTN_KV_HEADS * ATTN_HEAD_DIM
    q_gain = qg_ref[...]
    outs = []
    for g in range(ATTN_KV_HEADS):
        ksl = slice(g * ATTN_HEAD_DIM, (g + 1) * ATTN_HEAD_DIM)
        vsl = slice(kv_w + g * ATTN_HEAD_DIM, kv_w + (g + 1) * ATTN_HEAD_DIM)
        k = jnp.concatenate([kv_p[:, ksl], kv_c[:, ksl]], axis=0)
        v = jnp.concatenate([kv_p[:, vsl], kv_c[:, vsl]], axis=0).astype(BF16)
        k = _head_rms(k, kg_ref[...]).astype(BF16)
        q_heads = []
        for r in range(ATTN_Q_PER_KV):
            h = g * ATTN_Q_PER_KV + r
            q_h = q_t[h * ATTN_HEAD_DIM:(h + 1) * ATTN_HEAD_DIM]
            inv = lax.rsqrt(jnp.mean(q_h * q_h, axis=0, keepdims=True) + EPS)
            q_heads.append((q_h * inv * q_gain).astype(BF16))
        q = jnp.concatenate(q_heads, axis=1)
        s = jnp.dot(k, q, preferred_element_type=F32) + bias_ref[first, g]
        sink = sinkrow_ref[g]
        m = jnp.maximum(jnp.max(s, axis=0, keepdims=True), sink)
        p = jnp.exp(s - m)
        denom = jnp.sum(p, axis=0, keepdims=True) + jnp.exp(sink - m)
        pv = lax.dot_general(v, p.astype(BF16), (((0,), (0,)), ((), ())), preferred_element_type=F32)
        pv = pv * (1.0 / denom)
        outs.extend(pv[:, r * WINDOW:(r + 1) * WINDOW] for r in range(ATTN_Q_PER_KV))
    o_ref[...] = jnp.concatenate(outs, axis=0).T.astype(o_ref.dtype)


def _t5_causal_bucket(dist):
    max_exact = REL_BUCKETS // 2
    d = jnp.maximum(dist, 1).astype(F32)
    large = max_exact + (jnp.log(d / max_exact) / math.log(REL_MAX_DIST / max_exact)
                         * (REL_BUCKETS - max_exact)).astype(jnp.int32)
    large = jnp.minimum(large, REL_BUCKETS - 1)
    return jnp.where(dist < max_exact, dist, large)


def _swa_mixer(qkv, q_norm_g, k_norm_g, sinks, rel_bias, batch, seq):
    t = qkv.shape[0]
    nb = seq // WINDOW
    q_w = ATTN_Q_HEADS * ATTN_HEAD_DIM
    kv_w2 = 2 * ATTN_KV_HEADS * ATTN_HEAD_DIM
    kj = jnp.arange(2 * WINDOW)[:, None]
    qi = jnp.arange(WINDOW)[None, :]
    bucket = _t5_causal_bucket(jnp.maximum(qi + WINDOW - kj, 0)).astype(jnp.int32)
    q_gain = jnp.broadcast_to((q_norm_g * (ATTN_HEAD_DIM ** -0.5))[:, None], (ATTN_HEAD_DIM, WINDOW))
    const2 = lambda b, i: (0, 0)
    smem = pl.BlockSpec(memory_space=pltpu.SMEM)
    return pl.pallas_call(
        _swa_kernel,
        grid=(batch, nb),
        in_specs=[
            pl.BlockSpec((WINDOW, q_w), lambda b, i: (b * nb + i, 0)),
            pl.BlockSpec((WINDOW, kv_w2), lambda b, i: (b * nb + i, q_w // kv_w2)),
            pl.BlockSpec((WINDOW, kv_w2), lambda b, i: (b * nb + jnp.maximum(i - 1, 0), q_w // kv_w2)),
            pl.BlockSpec((2 * WINDOW, WINDOW), const2),
            pl.BlockSpec((ATTN_HEAD_DIM, WINDOW), const2),
            pl.BlockSpec((1, ATTN_HEAD_DIM), const2),
            smem,
            smem,
        ],
        out_specs=pl.BlockSpec((WINDOW, q_w), lambda b, i: (b * nb + i, 0)),
        out_shape=jax.ShapeDtypeStruct((t, q_w), BF16),
        scratch_shapes=[pltpu.VMEM((2, ATTN_KV_HEADS, 2 * WINDOW, ATTN_Q_PER_KV * WINDOW), F32),
                        pltpu.VMEM((ATTN_KV_HEADS, 1, ATTN_Q_PER_KV * WINDOW), F32)],
        compiler_params=_cparams("arbitrary", "arbitrary"),
        name="swa_mixer",
    )(qkv, qkv, qkv, bucket, q_gain, k_norm_g.reshape(1, -1), rel_bias, sinks)


def _router_kernel(x_ref, g_ref, sh_ref, sc_ref, rwt_ref, rb_ref,
                   h_ref, lp_ref, gate_ref, tile_n_ref, tile_carry_ref, cnt_ref, carry_ref):
    i = pl.program_id(0)
    tm = x_ref.shape[0]

    @pl.when(i == 0)
    def _():
        carry_ref[...] = jnp.zeros_like(carry_ref)

    h = _norm_modulate(x_ref[...], g_ref[...], sh_ref[0], sc_ref[0]).astype(BF16)
    h_ref[...] = h

    logits = lax.dot_general(rwt_ref[...], h, (((1,), (1,)), ((), ())),
                             preferred_element_type=F32) + rb_ref[...]
    e_iota = lax.broadcasted_iota(jnp.int32, logits.shape, 0)
    work = logits
    sels, vals = [], []
    for k in range(TOP_K):
        m = jnp.max(work, axis=0, keepdims=True)
        idx = jnp.min(jnp.where(work == m, e_iota, N_EXPERTS), axis=0, keepdims=True)
        sel = e_iota == idx
        work = jnp.where(sel, -jnp.inf, work)
        sels.append(sel)
        vals.append(m)
    exps = [jnp.exp(v - vals[0]) for v in vals]
    denom = exps[0] + exps[1] + exps[2] + exps[3]
    for k in range(TOP_K):
        gate_ref[k:k + 1, :] = exps[k] / denom

    chosen = sels[0] | sels[1] | sels[2] | sels[3]
    t_row = lax.broadcasted_iota(jnp.int32, (tm, tm), 0)
    t_col = lax.broadcasted_iota(jnp.int32, (tm, tm), 1)
    before = (t_row < t_col).astype(BF16)
    prior = jnp.dot(chosen.astype(BF16), before, preferred_element_type=F32)
    n = jnp.sum(chosen.astype(F32), axis=1, keepdims=True)
    e_row = lax.broadcasted_iota(jnp.int32, (N_EXPERTS, N_EXPERTS), 0)
    e_col = lax.broadcasted_iota(jnp.int32, (N_EXPERTS, N_EXPERTS), 1)
    run_start = jnp.dot((e_col < e_row).astype(BF16), jnp.broadcast_to(n, (N_EXPERTS, LANES)).astype(BF16),
                        preferred_element_type=F32)[:, :1]
    local = run_start + prior
    for k in range(TOP_K):
        lp_ref[k:k + 1, :] = jnp.sum(jnp.where(sels[k], local, 0.0), axis=0, keepdims=True).astype(jnp.int32)
    tile_n_ref[0] = n.astype(jnp.int32)
    tile_carry_ref[0] = carry_ref[...].astype(jnp.int32)
    total = carry_ref[...] + n
    carry_ref[...] = total
    cnt_ref[...] = total.astype(jnp.int32)


def _router(x, g, shift, scale, r_w, r_b, seq, tm):
    t, d = x.shape
    per_b = seq // tm
    ntiles = t // tm
    return pl.pallas_call(
        _router_kernel,
        grid=(t // tm,),
        in_specs=[
            pl.BlockSpec((tm, d), lambda i: (i, 0)),
            pl.BlockSpec((1, d), lambda i: (0, 0)),
            pl.BlockSpec((1, 1, d), lambda i: (i // per_b, 0, 0)),
            pl.BlockSpec((1, 1, d), lambda i: (i // per_b, 0, 0)),
            pl.BlockSpec((N_EXPERTS, d), lambda i: (0, 0)),
            pl.BlockSpec((N_EXPERTS, 1), lambda i: (0, 0)),
        ],
        out_specs=[
            pl.BlockSpec((tm, d), lambda i: (i, 0)),
            pl.BlockSpec((TOP_K, tm), lambda i: (0, i)),
            pl.BlockSpec((TOP_K, tm), lambda i: (0, i)),
            pl.BlockSpec((1, N_EXPERTS, 1), lambda i: (i, 0, 0)),
            pl.BlockSpec((1, N_EXPERTS, 1), lambda i: (i, 0, 0)),
            pl.BlockSpec((N_EXPERTS, 1), lambda i: (0, 0)),
        ],
        out_shape=[
            jax.ShapeDtypeStruct((t, d), BF16),
            jax.ShapeDtypeStruct((TOP_K, t), jnp.int32),
            jax.ShapeDtypeStruct((TOP_K, t), F32),
            jax.ShapeDtypeStruct((ntiles, N_EXPERTS, 1), jnp.int32),
            jax.ShapeDtypeStruct((ntiles, N_EXPERTS, 1), jnp.int32),
            jax.ShapeDtypeStruct((N_EXPERTS, 1), jnp.int32),
        ],
        scratch_shapes=[pltpu.VMEM((N_EXPERTS, 1), F32)],
        compiler_params=_cparams("arbitrary"),
        name="moe_router",
    )(x, g.reshape(1, d), shift, scale, r_w.T.astype(BF16), r_b.reshape(-1, 1))


def _run_copies(n, src_ref, src_row, dst_ref, dst_row, sem, max_rows, fn):
    for b in range(max_rows.bit_length()):
        size = 1 << b

        @pl.when((n & size) != 0)
        def _():
            lo = n & (size - 1)
            fn(pltpu.make_async_copy(
                src_ref.at[pl.ds(pl.multiple_of((src_row + lo) * ROW_TILES, ROW_TILES), size * ROW_TILES)],
                dst_ref.at[pl.ds(pl.multiple_of((dst_row + lo) * ROW_TILES, ROW_TILES), size * ROW_TILES)],
                sem), b)


def _start_piece(cp, b):
    cp.start()


def _wait_piece(cp, b):
    cp.wait()


def _tile_runs(tile, n_ref, carry_ref, gstart_ref, sorted_ref, grouped_ref, sem, tm, fn, to_grouped, live=None):
    def body(e, run_start):
        n = n_ref[tile * N_EXPERTS + e]
        if live is not None:
            n = jnp.where(live, n, 0)
        slot0 = gstart_ref[e] + carry_ref[tile * N_EXPERTS + e]
        if to_grouped:
            _run_copies(n, sorted_ref, run_start, grouped_ref, slot0, sem, tm, fn)
        else:
            _run_copies(n, grouped_ref, slot0, sorted_ref, run_start, sem, tm, fn)
        return run_start + n

    if live is None:
        lax.fori_loop(0, N_EXPERTS, body, 0)
    else:
        run_start = 0
        for e in range(N_EXPERTS):
            run_start = body(e, run_start)


def _block_copy(src_ref, dst_ref, dst_blk, sem):
    rows = MOE_BLOCK * ROW_TILES
    return pltpu.make_async_copy(src_ref, dst_ref.at[pl.ds(pl.multiple_of(dst_blk * rows, rows), rows)], sem)


def _dispatch_kernel(n_ref, carry_ref, gstart_ref, pad_lo_ref, pad_hi_ref, nused_ref, lp_ref, h_ref, xg_ref,
                     sorted0_ref, sorted1_ref, sorted2_ref, zero_ref, run_sem, sem):
    i = pl.program_id(0)
    last = pl.num_programs(0) - 1
    tm = h_ref.shape[0]
    rows = TOP_K * tm
    nblk =xg_ref.shape[0] // (MOE_BLOCK * ROW_TILES)

    @pl.when(i == 0)
    def _():
        zero_ref[...] = jnp.zeros_like(zero_ref)

        def fill(b, carry):
            _block_copy(zero_ref, xg_ref, b, sem).start()
            return carry

        def fill_wait(b, carry):
            _block_copy(zero_ref, xg_ref, b, sem).wait()
            return carry

        lax.fori_loop(nused_ref[0], nblk, fill, 0)
        lax.fori_loop(nused_ref[0], nblk, fill_wait, 0)

    @pl.when(i < N_EXPERTS)
    def _():
        n_pad = pad_hi_ref[i] - pad_lo_ref[i]
        for fn in (_start_piece, _wait_piece):
            _run_copies(n_pad, zero_ref, 0, xg_ref, pad_lo_ref[i], sem, MOE_BLOCK - 1, fn)

    bufs = (sorted0_ref, sorted1_ref, sorted2_ref)
    nbuf = len(bufs)

    def runs(tile, b, fn, live=None):
        _tile_runs(tile, n_ref, carry_ref, gstart_ref, bufs[b], xg_ref, run_sem.at[b], tm, fn, True, live)

    def wait_tile(b):
        pltpu.make_async_copy(bufs[b], xg_ref.at[pl.ds(0, rows * ROW_TILES)], run_sem.at[b]).wait()

    def step(cur):
        prev, oldest = (cur - 1) % nbuf, (cur + 1) % nbuf
        cur_ref = bufs[cur]

        @pl.when(i >= nbuf)
        def _():
            wait_tile(cur)

        row_id = lax.broadcasted_iota(jnp.int32, (rows, tm), 0)
        perm = row_id == lp_ref[0:1, :]
        for k in range(1, TOP_K):
            perm = perm | (row_id == lp_ref[k:k + 1, :])
        srt = jnp.dot(jnp.where(perm, 1.0, 0.0).astype(BF16), h_ref[...], preferred_element_type=F32)
        for s in range(ROW_TILES):
            cur_ref[pl.ds(s, rows, stride=ROW_TILES), :] = srt[:, s * LANES:(s + 1) * LANES]

        runs(jnp.maximum(i - 1, 0), prev, _start_piece, live=i >= 1)

        @pl.when(i == last)
        def _():
            @pl.when(i >= 2)
            def _():
                wait_tile(oldest)

            @pl.when(i >= 1)
            def _():
                wait_tile(prev)

            runs(i, cur, _start_piece)
            wait_tile(cur)

    for r in range(nbuf):
        @pl.when(i % nbuf == r)
        def _():
            step(r)


def _dispatch(tile_n, tile_carry, group_start, pad_lo, pad_hi, nused, lp_t, h, cap, tm):
    t, d = h.shape
    assert t // tm >= N_EXPERTS
    grid_spec = pltpu.PrefetchScalarGridSpec(
        num_scalar_prefetch=6,
        grid=(t // tm,),
        in_specs=[
            pl.BlockSpec((TOP_K, tm), lambda i, *_: (0, i)),
            pl.BlockSpec((tm, d), lambda i, *_: (i, 0)),
        ],
        out_specs=pl.BlockSpec(memory_space=pl.ANY),
        scratch_shapes=[
            pltpu.VMEM((TOP_K * tm * ROW_TILES, LANES), F32),
            pltpu.VMEM((TOP_K * tm * ROW_TILES, LANES), F32),
            pltpu.VMEM((TOP_K * tm * ROW_TILES, LANES), F32),
            pltpu.VMEM((MOE_BLOCK * ROW_TILES, LANES), F32),
            pltpu.SemaphoreType.DMA((3,)), pltpu.SemaphoreType.DMA,
        ],
    )
    return pl.pallas_call(
        _dispatch_kernel,
        grid_spec=grid_spec,
        out_shape=jax.ShapeDtypeStruct((cap * ROW_TILES, LANES), F32),
        compiler_params=pltpu.CompilerParams(dimension_semantics=("arbitrary",), has_side_effects=True,
                                             vmem_limit_bytes=VMEM_LIMIT),
        name="moe_dispatch",
    )(tile_n, tile_carry, group_start, pad_lo, pad_hi, nused, lp_t, h)


def _weight_copies(w1_hbm, w2_hbm, w1buf, w2buf, sems, layer, e, slot):
    return (pltpu.make_async_copy(w1_hbm.at[layer, e], w1buf.at[slot], sems.at[0, slot]),
            pltpu.make_async_copy(w2_hbm.at[layer, e], w2buf.at[slot], sems.at[1, slot]))


def _expert_kernel(blk_e_ref, nused_ref, next_e_ref, slot_ref, blk_rows_ref, x_ref, w1_hbm, w2_hbm, bg_ref, bl_ref, b2_ref,
                   sel_ref, o_ref, w1buf, w2buf, wg_s, wl_s, w2_s, sems, *, layer):
    blk = pl.program_id(0)
    active = blk < nused_ref[0]
    e = blk_e_ref[blk]
    slot = slot_ref[e]
    new_expert = (blk == 0) | (e != blk_e_ref[jnp.maximum(blk - 1, 0)])
    copies = functools.partial(_weight_copies, w1_hbm, w2_hbm, w1buf, w2buf, sems, layer)

    @pl.when(blk == 0)
    def _():
        for cp in copies(e, slot):
            cp.start()

    @pl.when(active & new_expert)
    def _():
        nxt = next_e_ref[e]

        @pl.when(nxt >= 0)
        def _():
            for cp in copies(nxt, 1 - slot):
                cp.start()

        for cp in copies(e, slot):
            cp.wait()
        half = SPLIT_W // 2
        for j in range(w1buf.shape[2] // SPLIT_W):
            chunk = w1buf[slot, :, j * SPLIT_W:(j + 1) * SPLIT_W].astype(BF16)
            split = jnp.dot(chunk, sel_ref[...], preferred_element_type=F32)
            wg_s[:, j * half:(j + 1) * half] = split[:, :half].astype(BF16)
            wl_s[:, j * half:(j + 1) * half] = split[:, half:].astype(BF16)
        w2_s[...] = w2buf[slot].astype(BF16)

    def mlp(rows):
        x = jnp.concatenate([x_ref[pl.ds(s, rows, stride=ROW_TILES), :]
                             for s in range(ROW_TILES)], axis=1).astype(BF16)
        glu = jnp.dot(x, wg_s[...], preferred_element_type=F32) + bg_ref[0]
        lin = jnp.dot(x, wl_s[...], preferred_element_type=F32) + bl_ref[0]
        glu = jnp.minimum(glu, SWIGLU_LIMIT)
        lin = jnp.clip(lin, -SWIGLU_LIMIT, SWIGLU_LIMIT)
        act = glu * jax.nn.sigmoid(SWIGLU_ALPHA * glu) * (lin + 1.0)
        y = jnp.dot(act.astype(BF16), w2_s[...], preferred_element_type=F32) + b2_ref[0]
        for s in range(ROW_TILES):
            o_ref[pl.ds(s, rows, stride=ROW_TILES), :] = y[:, s * LANES:(s + 1) * LANES]
        if rows < MOE_BLOCK:
            o_ref[pl.ds(rows * ROW_TILES, (MOE_BLOCK - rows) * ROW_TILES), :] = jnp.zeros(
                ((MOE_BLOCK - rows) * ROW_TILES, LANES), o_ref.dtype)

    valid_rows = jnp.where(active, blk_rows_ref[blk], 0)
    for rows in range(MOE_SUB, MOE_BLOCK + 1, MOE_SUB):
        @pl.when((valid_rows > rows - MOE_SUB) & (valid_rows <= rows))
        def _():
            mlp(rows)

    @pl.when(valid_rows <= 0)
    def _():
        o_ref[...] = jnp.zeros_like(o_ref)


def _experts(layer, blk_e, nused, next_e, e_slot, blk_rows, xg_rows, w1_all, w2_all, b_glu, b_lin, b2):
    nblk = blk_e.shape[0]
    d = D_MODEL
    ff = w2_all.shape[2]
    col = jnp.arange(SPLIT_W)
    sel = (jnp.arange(SPLIT_W)[None, :] == ((col % 2) * (SPLIT_W // 2) + col // 2)[:, None]).astype(BF16)

    def blk_map(b, blk_e_ref, nused_ref, *_):
        return (jnp.minimum(b, nused_ref[0] - 1), 0)

    def e_map(b, blk_e_ref, nused_ref, *_):
        return (blk_e_ref[jnp.minimum(b, nused_ref[0] - 1)], 0, 0)

    grid_spec = pltpu.PrefetchScalarGridSpec(
        num_scalar_prefetch=5,
        grid=(nblk,),
        in_specs=[
            pl.BlockSpec((MOE_BLOCK * ROW_TILES, LANES), blk_map),
            pl.BlockSpec(memory_space=pl.ANY),
            pl.BlockSpec(memory_space=pl.ANY),
            pl.BlockSpec((1, 1, ff), e_map),
            pl.BlockSpec((1, 1, ff), e_map),
            pl.BlockSpec((1, 1, d), e_map),
            pl.BlockSpec((SPLIT_W, SPLIT_W), lambda b, *_: (0, 0)),
        ],
        out_specs=pl.BlockSpec((MOE_BLOCK * ROW_TILES, LANES), lambda b, *_: (b, 0)),
        scratch_shapes=[
            pltpu.VMEM((2, d, 2 * ff), F32), pltpu.VMEM((2, ff, d), F32),
            pltpu.VMEM((d, ff), BF16), pltpu.VMEM((d, ff), BF16), pltpu.VMEM((ff, d), BF16),
            pltpu.SemaphoreType.DMA((2, 2)),
        ],
    )
    return pl.pallas_call(
        functools.partial(_expert_kernel, layer=layer),
        grid_spec=grid_spec,
        out_shape=jax.ShapeDtypeStruct(xg_rows.shape, F32),
        compiler_params=_cparams("arbitrary"),
        name="moe_experts",
    )(blk_e, nused, next_e, e_slot, blk_rows, xg_rows, w1_all, w2_all, b_glu, b_lin, b2, sel)


def _combine_kernel(n_ref, carry_ref, gstart_ref, yg_ref, lp_ref, gate_ref, x_ref, g2_ref, o_ref, sorted_ref, sem):
    i = pl.program_id(0)
    ntiles = pl.num_programs(0)
    tm = x_ref.shape[0]
    rows = TOP_K * tm
    slot = i % 2

    def runs(tile, buf, fn, live=None):
        _tile_runs(tile, n_ref, carry_ref, gstart_ref, sorted_ref.at[buf], yg_ref, sem.at[buf], tm, fn, False, live)

    @pl.when(i == 0)
    def _():
        runs(0, 0, _start_piece)

    runs(jnp.minimum(i + 1, ntiles - 1), 1 - slot, _start_piece, live=i + 1 < ntiles)

    eye = (lax.broadcasted_iota(jnp.int32, (tm, tm), 0) ==
           lax.broadcasted_iota(jnp.int32, (tm, tm), 1)).astype(F32)
    rows_t = jnp.concatenate([gate_ref[...], lp_ref[...].astype(F32)], axis=0)
    cols = lax.dot_general(eye, rows_t, (((1,), (1,)), ((), ())), preferred_element_type=F32, precision=HI)
    col_id = lax.broadcasted_iota(jnp.int32, (tm, rows), 1)
    weights = jnp.zeros((tm, rows), F32)
    for k in range(TOP_K):
        lp_col = (cols[:, TOP_K + k:TOP_K + k + 1] + 0.5).astype(jnp.int32)
        weights = weights + jnp.where(col_id == lp_col, cols[:, k:k + 1], 0.0)

    pltpu.make_async_copy(yg_ref.at[pl.ds(0, rows * ROW_TILES)], sorted_ref.at[slot], sem.at[slot]).wait()
    y = jnp.concatenate(
        [sorted_ref[slot, pl.ds(s, rows, stride=ROW_TILES), :] for s in range(ROW_TILES)], axis=1).astype(BF16)
    acc = jnp.dot(weights.astype(BF16), y, preferred_element_type=F32)
    o_ref[...] = x_ref[...] + g2_ref[0] * acc


def _combine(tile_n, tile_carry, group_start, yg_rows, lp_t, gate_t, x, gate2, seq, tm):
    t, d = x.shape
    per_b = seq // tm
    grid_spec = pltpu.PrefetchScalarGridSpec(
        num_scalar_prefetch=3,
        grid=(t // tm,),
        in_specs=[
            pl.BlockSpec(memory_space=pl.ANY),
            pl.BlockSpec((TOP_K, tm), lambda i, *_: (0, i)),
            pl.BlockSpec((TOP_K, tm), lambda i, *_: (0, i)),
            pl.BlockSpec((tm, d), lambda i, *_: (i, 0)),
            pl.BlockSpec((1, 1, d), lambda i, *_: (i // per_b, 0, 0)),
        ],
        out_specs=pl.BlockSpec((tm, d), lambda i, *_: (i, 0)),
        scratch_shapes=[pltpu.VMEM((2, TOP_K * tm * ROW_TILES, LANES), F32), pltpu.SemaphoreType.DMA((2,))],
    )
    return pl.pallas_call(
        _combine_kernel,
        grid_spec=grid_spec,
        out_shape=jax.ShapeDtypeStruct((t, d), F32),
        compiler_params=_cparams("arbitrary"),
        name="moe_combine",
    )(tile_n, tile_carry, group_start, yg_rows, lp_t, gate_t, x, gate2)


def _moe(layer, x, g, shift, scale, gate2, r_w, r_b, w1_all, b1_glu, b1_lin, w2_all, b2, seq):
    t = x.shape[0]
    h, lp_t, gate_t, tile_n, tile_carry, counts = _router(x, g, shift, scale, r_w, r_b, seq, MOE_TILE)
    tile_n = tile_n.reshape(-1)
    tile_carry = tile_carry.reshape(-1)
    counts = counts.reshape(-1)
    padded = ((counts + MOE_BLOCK - 1) // MOE_BLOCK) * MOE_BLOCK
    group_end = jnp.cumsum(padded)
    group_start = (group_end - padded).astype(jnp.int32)
    cap = t * TOP_K + N_EXPERTS * MOE_BLOCK
    nblk = cap // MOE_BLOCK
    blk_start = jnp.arange(nblk, dtype=jnp.int32) * MOE_BLOCK
    e_ids = jnp.arange(N_EXPERTS, dtype=jnp.int32)
    past = (blk_start[None, :] >= group_end[:, None]).astype(jnp.int32)
    blk_e = jnp.minimum(jnp.sum(past, axis=0), N_EXPERTS - 1).astype(jnp.int32)
    nused = (group_end[-1:] // MOE_BLOCK).astype(jnp.int32)

    pad_lo = (group_start + counts).astype(jnp.int32)
    pad_hi = group_end.astype(jnp.int32)
    blk_pad_lo = jnp.sum(jnp.where(blk_e[None, :] == e_ids[:, None], pad_lo[:, None], 0), axis=0)
    blk_rows = jnp.clip(blk_pad_lo - blk_start, 0, MOE_BLOCK).astype(jnp.int32)
    xg_rows = _dispatch(tile_n, tile_carry, group_start, pad_lo, pad_hi, nused, lp_t, h, cap, MOE_TILE)
    nonempty = counts > 0
    later = jnp.where(nonempty[None, :] & (e_ids[None, :] > e_ids[:, None]), e_ids[None, :], N_EXPERTS)
    next_e = jnp.min(later, axis=1)
    next_e = jnp.where(next_e == N_EXPERTS, -1, next_e).astype(jnp.int32)
    e_slot = ((jnp.cumsum(nonempty.astype(jnp.int32)) - 1) % 2).astype(jnp.int32)
    yg_rows = _experts(layer, blk_e, nused, next_e, e_slot, blk_rows, xg_rows, w1_all, w2_all,
                       b1_glu[:, None, :], b1_lin[:, None, :], b2[:, None, :])
    return _combine(tile_n, tile_carry, group_start, yg_rows, lp_t, gate_t, x, gate2, seq, MOE_TILE)


def kernel(x, c, ada_w, ada_b, norm1_g, norm2_g, m_in_w, m_conv_w, m_conv_b, m_dt_bias, m_A_log, m_D, m_norm_g, m_out_w, a_qkv_w, a_q_norm_g, a_k_norm_g, a_sinks, a_out_w, rel_bias, r_w, r_b, e_w1, e_b1, e_w2, e_b2):
    batch, seq, d = x.shape
    depth = ada_w.shape[0]
    t = batch * seq
    xf = x.reshape(t, d)

    mod = _adaln(c, ada_w, ada_b)[:, :batch]

    b1_split = jnp.moveaxis(e_b1.reshape(depth, N_EXPERTS, -1, 2), -1, 1)

    for i in range(depth):
        parts =[mod[i, :, p * d:(p + 1) * d].reshape(batch, 1, d) for p in range(6)]
        sh1, sc1, g1, sh2, sc2, g2 = parts
        j = i // 2
        if i % 2 == 0:
            w_zxbc = m_in_w[j, :, :SSM_ZXBC].astype(BF16)
            w_dt = jnp.pad(m_in_w[j, :, SSM_ZXBC:], ((0, 0), (0, LANES - SSM_HEADS)))
            zxbc, dt_raw = _norm_matmul(xf, norm1_g[i], sh1, sc1, (w_zxbc, w_dt), seq)
            y = _ssd_mixer(zxbc, dt_raw, m_conv_w[j], m_conv_b[j], m_dt_bias[j], m_A_log[j], m_D[j],
                           m_norm_g[j], batch, seq)
            xf = _matmul_residual(y, m_out_w[j].astype(BF16), xf, g1, seq)
        else:
            qkv, = _norm_matmul(xf, norm1_g[i], sh1, sc1, (a_qkv_w[j].astype(BF16),), seq)
            y = _swa_mixer(qkv, a_q_norm_g[j], a_k_norm_g[j], a_sinks[j], rel_bias, batch, seq)
            xf = _matmul_residual(y, a_out_w[j].astype(BF16), xf, g1, seq)
        xf = _moe(i, xf, norm2_g[i], sh2, sc2, g2, r_w[i], r_b[i], e_w1, b1_split[i, 0], b1_split[i, 1],
                  e_w2, e_b2[i], seq)
    return xf.reshape(batch, seq, d)
```

```python
import functools
import math

import jax
import jax.numpy as jnp
from jax import lax
from jax.experimental import pallas as pl
from jax.experimental.pallas import tpu as pltpu

D_MODEL = 1024
EPS = 1e-6
LANES = 128
SUBLANES = 8
ROW_TILES = D_MODEL // LANES

SSM_D_INNER = 2048
SSM_HEAD_DIM = 64
SSM_HEADS = 32
SSM_GROUPS = 4
SSM_STATE = 128
SSM_CONV = 4
SSM_CHUNK = 128
SSM_STEP_CHUNKS = 2
SSM_GN = SSM_GROUPS * SSM_STATE
SSM_ZXBC = 2 * SSM_D_INNER + 2 * SSM_GN
SSM_GROUP_W = SSM_D_INNER // SSM_GROUPS

ATTN_HEAD_DIM = 64
ATTN_Q_HEADS = 16
ATTN_KV_HEADS = 4
ATTN_Q_PER_KV = 4
WINDOW = 128
SWA_STEP_BLOCKS = 2
REL_BUCKETS = 32
REL_MAX_DIST = 128

N_EXPERTS = 32
TOP_K = 4
SWIGLU_ALPHA = 1.702
SWIGLU_LIMIT = 7.0
MOE_BLOCK = 512
MOE_SUB = 128
MOE_TILE = 256
SPLIT_W = 256

VMEM_LIMIT = 56 * 1024 * 1024
HI = lax.Precision.HIGHEST
F32 = jnp.float32
BF16 = jnp.bfloat16


def _cparams(*sem):
    return pltpu.CompilerParams(dimension_semantics=sem, vmem_limit_bytes=VMEM_LIMIT)


def _norm_modulate(x, g, shift, scale):
    ms = jnp.mean(x * x, axis=-1, keepdims=True)
    return x * lax.rsqrt(ms + EPS) * g * (1.0 + scale) + shift


def _adaln_kernel(ct_ref, w_ref, b_ref, o_ref, *, batch):
    c_t = ct_ref[...]
    act_t = c_t * jax.nn.sigmoid(c_t)
    w = w_ref[0]
    rows = [jnp.sum(act_t[:, b:b + 1] * w, axis=0, keepdims=True) + b_ref[0] for b in range(batch)]
    rows.append(jnp.zeros((SUBLANES - batch, w.shape[1]), F32))
    o_ref[0] = jnp.concatenate(rows, axis=0)


def _adaln(c, ada_w, ada_b):
    depth, d, n = ada_w.shape
    batch = c.shape[0]
    tn = 1536
    c_t = jnp.zeros((d, SUBLANES), F32).at[:, :batch].set(c.T)
    return pl.pallas_call(
        functools.partial(_adaln_kernel, batch=batch),
        grid=(depth, n // tn),
        in_specs=[
            pl.BlockSpec((d, SUBLANES), lambda i, j: (0, 0)),
            pl.BlockSpec((1, d, tn), lambda i, j: (i, 0, j)),
            pl.BlockSpec((1, 1, tn), lambda i, j: (i, 0, j)),
        ],
        out_specs=pl.BlockSpec((1, SUBLANES, tn), lambda i, j: (i, 0, j)),
        out_shape=jax.ShapeDtypeStruct((depth, SUBLANES, n), F32),
        compiler_params=_cparams("arbitrary", "arbitrary"),
        name="adaln",
    )(c_t, ada_w, ada_b.reshape(depth, 1, n))


def _norm_matmul_kernel(x_ref, g_ref, sh_ref, sc_ref, *refs):
    w_refs, o_refs = refs[:len(refs) // 2], refs[len(refs) // 2:]
    h = _norm_modulate(x_ref[...], g_ref[...], sh_ref[0], sc_ref[0]).astype(BF16)
    for w_ref, o_ref in zip(w_refs, o_refs):
        o_ref[...] = jnp.dot(h, w_ref[...].astype(BF16), preferred_element_type=F32)


def _norm_matmul(x, g, shift, scale, weights_bf16, seq, tm=512):
    t, d = x.shape
    per_b = seq // tm
    return pl.pallas_call(
        _norm_matmul_kernel,
        grid=(t // tm,),
        in_specs=[
            pl.BlockSpec((tm, d), lambda i: (i, 0)),
            pl.BlockSpec((1, d), lambda i: (0, 0)),
            pl.BlockSpec((1, 1, d), lambda i: (i // per_b, 0, 0)),
            pl.BlockSpec((1, 1, d), lambda i: (i // per_b, 0, 0)),
        ] + [pl.BlockSpec(w.shape, lambda i: (0, 0), pipeline_mode=pl.Buffered(1))
             for w in weights_bf16],
        out_specs=[pl.BlockSpec((tm, w.shape[1]), lambda i: (i, 0)) for w in weights_bf16],
        out_shape=[jax.ShapeDtypeStruct((t, w.shape[1]), F32) for w in weights_bf16],
        compiler_params=_cparams("arbitrary"),
        name="norm_matmul",
    )(x, g.reshape(1, d), shift, scale, *weights_bf16)


def _matmul_residual_kernel(y_ref, w_ref, x_ref, gate_ref, o_ref):
    acc = jnp.dot(y_ref[...], w_ref[...], preferred_element_type=F32)
    o_ref[...] = x_ref[...] + gate_ref[0] * acc


def _matmul_residual(y_bf16, w_bf16, x, gate, seq, tm=512):
    t, k = y_bf16.shape
    d = x.shape[1]
    per_b = seq // tm
    return pl.pallas_call(
        _matmul_residual_kernel,
        grid=(t // tm,),
        in_specs=[
            pl.BlockSpec((tm, k), lambda i: (i, 0)),
            pl.BlockSpec((k, d), lambda i: (0, 0)),
            pl.BlockSpec((tm, d), lambda i: (i, 0)),
            pl.BlockSpec((1, 1, d), lambda i: (i // per_b, 0, 0)),
        ],
        out_specs=pl.BlockSpec((tm, d), lambda i: (i, 0)),
        out_shape=jax.ShapeDtypeStruct((t, d), F32),
        compiler_params=_cparams("arbitrary"),
        name="matmul_residual",
    )(y_bf16, w_bf16, x, gate)


def _causal_conv_silu(cur, prev_tail, w, b):
    rows = lax.broadcasted_iota(jnp.int32, (SUBLANES, cur.shape[1]), 0)
    acc = b + w[SSM_CONV - 1:SSM_CONV] * cur
    for d in range(1, SSM_CONV):
        rolled = pltpu.roll(cur, d, axis=0)
        top = jnp.where(rows < d, pltpu.roll(prev_tail, d, axis=0), rolled[0:SUBLANES])
        shifted = jnp.concatenate([top, rolled[SUBLANES:]], axis=0)
        acc = acc + w[SSM_CONV - 1 - d:SSM_CONV - d] * shifted
    return acc * jax.nn.sigmoid(acc)


def _split3(x):
    hi = x.astype(BF16)
    rem = x - hi.astype(F32)
    mid = rem.astype(BF16)
    lo = (rem - mid.astype(F32)).astype(BF16)
    return jnp.concatenate([hi, mid, lo], axis=1)


def _ssd_kernel(z_ref, xs_ref, bc_ref, dt_ref, *refs):
    o_ref, tailx_ref, tailb_ref, state_ref = refs[-4:]

    @pl.when(pl.program_id(1) == 0)
    def _():
        tailx_ref[...] = jnp.zeros_like(tailx_ref)
        tailb_ref[...] = jnp.zeros_like(tailb_ref)
        state_ref[...] = jnp.zeros_like(state_ref)

    for u in range(SSM_STEP_CHUNKS):
        rows = pl.ds(u * SSM_CHUNK, SSM_CHUNK)
        _ssd_chunk(z_ref.at[rows], xs_ref.at[rows], bc_ref.at[rows], dt_ref.at[rows], *refs[:-4],
                   o_ref.at[rows], tailx_ref, tailb_ref, state_ref)


def _ssd_chunk(z_ref, xs_ref, bc_ref, dt_ref, cwx_ref, cbx_ref, cwb_ref, cbb_ref, dtb_ref, alog_ref,
               dskip_ref, ng_ref, hexp_ref, lexp_ref, o_ref, tailx_ref, tailb_ref, state_ref):
    xs_raw = xs_ref[...]
    bc_raw = bc_ref[...]
    xs = _causal_conv_silu(xs_raw, tailx_ref[...], cwx_ref[...], cbx_ref[...])
    bc = _causal_conv_silu(bc_raw, tailb_ref[...], cwb_ref[...], cbb_ref[...])
    tailx_ref[...] = xs_raw[SSM_CHUNK - SUBLANES:]
    tailb_ref[...] = bc_raw[SSM_CHUNK - SUBLANES:]

    dt_in = dt_ref[...][:, :SSM_HEADS] + dtb_ref[...]
    dt = jnp.maximum(dt_in, 0.0) + jnp.log1p(jnp.exp(-jnp.abs(dt_in)))
    a_neg = -jnp.exp(alog_ref[...])
    d_a = dt * a_neg
    li = lax.broadcasted_iota(jnp.int32, (SSM_CHUNK, SSM_CHUNK), 0)
    si = lax.broadcasted_iota(jnp.int32, (SSM_CHUNK, SSM_CHUNK), 1)
    causal = li >= si
    tri = causal.astype(F32)
    a_cs = jnp.dot(tri, d_a, preferred_element_type=F32, precision=HI)
    a_cs_t = lax.dot_general(d_a, tri, (((0,), (1,)), ((), ())),
                             preferred_element_type=F32, precision=HI)
    a_last = a_cs[SSM_CHUNK - 1:SSM_CHUNK]
    e_out = jnp.exp(a_cs)
    e_state = jnp.exp(a_last - a_cs) * dt
    small = jnp.concatenate([dt, e_out, e_state], axis=0)
    wide = jnp.dot(_split3(small), hexp_ref[...], preferred_element_type=F32)
    dt_w = wide[0:SSM_CHUNK]
    e_out_w = wide[SSM_CHUNK:2 * SSM_CHUNK]
    e_state_w = wide[2 * SSM_CHUNK:]
    a_col = jnp.dot(_split3(a_cs), lexp_ref[...], preferred_element_type=F32)

    x_dt = (xs * dt_w).astype(BF16)
    x_state = (xs * e_state_w).astype(BF16)
    chunk_decay_w = e_out_w[SSM_CHUNK - 1:SSM_CHUNK]

    heads_per_group = SSM_HEADS // SSM_GROUPS
    y_parts = []
    for g in range(SSM_GROUPS):
        b_g = bc[:, g * SSM_STATE:(g + 1) * SSM_STATE].astype(BF16)
        c_g = bc[:, SSM_GN + g * SSM_STATE:SSM_GN + (g + 1) * SSM_STATE].astype(BF16)
        cb = lax.dot_general(c_g, b_g, (((1,), (1,)), ((), ())), preferred_element_type=F32)
        gsl = slice(g * SSM_GROUP_W, (g + 1) * SSM_GROUP_W)
        h_prev = state_ref[g]
        y_off = jnp.dot(c_g, h_prev.astype(BF16), preferred_element_type=F32) * e_out_w[:, gsl]
        diag = []
        for r in range(heads_per_group):
            h = g * heads_per_group + r
            seg = a_col[:, h * SSM_CHUNK:(h + 1) * SSM_CHUNK] - a_cs_t[h:h + 1, :]
            decay = jnp.exp(jnp.where(causal, seg, -jnp.inf))
            m = (cb * decay).astype(BF16)
            diag.append(jnp.dot(m, x_dt[:, h * SSM_HEAD_DIM:(h + 1) * SSM_HEAD_DIM],
                                preferred_element_type=F32))
        y_parts.append(jnp.concatenate(diag, axis=1) + y_off)
        upd = lax.dot_general(b_g, x_state[:, gsl], (((0,), (0,)), ((), ())), preferred_element_type=F32)
        state_ref[g] = h_prev * chunk_decay_w[:, gsl] + upd

    y = jnp.concatenate(y_parts, axis=1) + dskip_ref[...] * xs
    z = z_ref[...]
    y = y * (z * jax.nn.sigmoid(z))
    normed = []
    for g in range(SSM_GROUPS):
        y_g = y[:, g * SSM_GROUP_W:(g + 1) * SSM_GROUP_W]
        normed.append(y_g * lax.rsqrt(jnp.mean(y_g * y_g, axis=-1, keepdims=True) + EPS))
    o_ref[...] = (jnp.concatenate(normed, axis=1) * ng_ref[...]).astype(o_ref.dtype)


def _ssd_mixer(zxbc, dt_raw, conv_w, conv_b, dt_bias, a_log, d_skip, norm_g, batch, seq):
    t = zxbc.shape[0]
    step_rows = SSM_CHUNK * SSM_STEP_CHUNKS
    nc = seq // step_rows
    head_expand = jnp.tile(jnp.repeat(jnp.eye(SSM_HEADS, dtype=BF16), SSM_HEAD_DIM, axis=1), (3, 1))
    lane_expand = jnp.tile(jnp.repeat(jnp.eye(SSM_HEADS, dtype=BF16), SSM_CHUNK, axis=1), (3, 1))
    row = lambda b, c: (b * nc + c, 0)
    const2 = lambda b, c: (0, 0)
    bc_w = 2 * SSM_GN
    return pl.pallas_call(
        _ssd_kernel,
        grid=(batch, nc),
        in_specs=[
            pl.BlockSpec((step_rows, SSM_D_INNER), row),
            pl.BlockSpec((step_rows, SSM_D_INNER), lambda b, c: (b * nc + c, 1)),
            pl.BlockSpec((step_rows, bc_w), lambda b, c: (b * nc + c, 2 * SSM_D_INNER // bc_w)),
            pl.BlockSpec((step_rows, LANES), row),
            pl.BlockSpec((SSM_CONV, SSM_D_INNER), const2),
            pl.BlockSpec((1, SSM_D_INNER), const2),
            pl.BlockSpec((SSM_CONV, bc_w), const2),
            pl.BlockSpec((1, bc_w), const2),
            pl.BlockSpec((1, SSM_HEADS), const2),
            pl.BlockSpec((1, SSM_HEADS), const2),
            pl.BlockSpec((1, SSM_D_INNER), const2),
            pl.BlockSpec((1, SSM_D_INNER), const2),
            pl.BlockSpec((3 * SSM_HEADS, SSM_D_INNER), const2),
            pl.BlockSpec((3 * SSM_HEADS, SSM_HEADS * SSM_CHUNK), const2),
        ],
        out_specs=pl.BlockSpec((step_rows, SSM_D_INNER), row),
        out_shape=jax.ShapeDtypeStruct((t, SSM_D_INNER), BF16),
        scratch_shapes=[
            pltpu.VMEM((SUBLANES, SSM_D_INNER), F32),
            pltpu.VMEM((SUBLANES, bc_w), F32),
            pltpu.VMEM((SSM_GROUPS, SSM_STATE, SSM_GROUP_W), F32),
        ],
        compiler_params=_cparams("arbitrary", "arbitrary"),
        name="ssd_mixer",
    )(zxbc, zxbc, zxbc, dt_raw,
      conv_w[:, :SSM_D_INNER], conv_b[:SSM_D_INNER].reshape(1, -1),
      conv_w[:, SSM_D_INNER:], conv_b[SSM_D_INNER:].reshape(1, -1),
      dt_bias.reshape(1, -1), a_log.reshape(1, -1),
      jnp.repeat(d_skip, SSM_HEAD_DIM).reshape(1, -1), norm_g.reshape(1, -1),
      head_expand, lane_expand)


def _head_rms(x, g):
    return x * lax.rsqrt(jnp.mean(x * x, axis=-1, keepdims=True) + EPS) * g


def _swa_kernel(q_ref, kvc_ref, kvp_ref, bucket_ref, qg_ref, kg_ref, rel_ref, sink_ref, o_ref,
                bias_ref, sinkrow_ref):
    b = pl.program_id(0)
    i = pl.program_id(1)

    @pl.when((b == 0) & (i == 0))
    def _():
        bucket = bucket_ref[...]
        kj = lax.broadcasted_iota(jnp.int32, (2 * WINDOW, WINDOW), 0)
        qi = lax.broadcasted_iota(jnp.int32, (2 * WINDOW, WINDOW), 1)
        dist = qi + WINDOW - kj
        band = (dist >= 0) & (dist < WINDOW)
        for h in range(ATTN_Q_HEADS):
            g, r = divmod(h, ATTN_Q_PER_KV)
            acc = jnp.zeros(bucket.shape, F32)
            for k in range(REL_BUCKETS):
                acc = jnp.where(bucket == k, rel_ref[k, h], acc)
            cols = slice(r * WINDOW, (r + 1) * WINDOW)
            bias_ref[0, g, :, cols] = jnp.where(band, acc, -jnp.inf)
            bias_ref[1, g, :, cols] = jnp.where(band & (kj >= WINDOW), acc, -jnp.inf)
            sinkrow_ref[g, :, cols] = jnp.full((1, WINDOW), sink_ref[h], F32)

    for u in range(SWA_STEP_BLOCKS):
        rows = slice(u * WINDOW, (u + 1) * WINDOW)
        kv_p = kvp_ref[...] if u == 0 else kvc_ref[(u - 1) * WINDOW:u * WINDOW, :]
        first = (i == 0).astype(jnp.int32) if u == 0 else 0
        o_ref[rows, :] = _swa_block(q_ref[rows, :], kvc_ref[rows, :], kv_p, first,
                                    qg_ref, kg_ref, bias_ref, sinkrow_ref).astype(o_ref.dtype)


def _swa_block(q_rows, kv_c, kv_p, first, qg_ref, kg_ref, bias_ref, sinkrow_ref):
    q_t = q_rows.T
    kv_w = ATTN_KV_HEADS * ATTN_HEAD_DIM
    q_gain = qg_ref[...]
    outs = []
    for g in range(ATTN_KV_HEADS):
        ksl = slice(g * ATTN_HEAD_DIM, (g + 1) * ATTN_HEAD_DIM)
        vsl = slice(kv_w + g * ATTN_HEAD_DIM, kv_w + (g + 1) * ATTN_HEAD_DIM)
        k = jnp.concatenate([kv_p[:, ksl], kv_c[:, ksl]], axis=0)
        v = jnp.concatenate([kv_p[:, vsl], kv_c[:, vsl]], axis=0).astype(BF16)
        k = _head_rms(k, kg_ref[...]).astype(BF16)
        q_heads = []
        for r in range(ATTN_Q_PER_KV):
            h = g * ATTN_Q_PER_KV + r
            q_h = q_t[h * ATTN_HEAD_DIM:(h + 1) * ATTN_HEAD_DIM]
            inv = lax.rsqrt(jnp.mean(q_h * q_h, axis=0, keepdims=True) + EPS)
            q_heads.append((q_h * inv * q_gain).astype(BF16))
        q = jnp.concatenate(q_heads, axis=1)
        s = jnp.dot(k, q, preferred_element_type=F32) + bias_ref[first, g]
        sink = sinkrow_ref[g]
        m = jnp.maximum(jnp.max(s, axis=0, keepdims=True), sink)
        p = jnp.exp(s - m)
        denom = jnp.sum(p, axis=0, keepdims=True) + jnp.exp(sink - m)
        pv = lax.dot_general(v, p.astype(BF16), (((0,), (0,)), ((), ())), preferred_element_type=F32)
        pv = pv * (1.0 / denom)
        outs.extend(pv[:, r * WINDOW:(r + 1) * WINDOW] for r in range(ATTN_Q_PER_KV))
    return jnp.concatenate(outs, axis=0).T


def _t5_causal_bucket(dist):
    max_exact = REL_BUCKETS // 2
    d = jnp.maximum(dist, 1).astype(F32)
    large = max_exact + (jnp.log(d / max_exact) / math.log(REL_MAX_DIST / max_exact)
                         * (REL_BUCKETS - max_exact)).astype(jnp.int32)
    large = jnp.minimum(large, REL_BUCKETS - 1)
    return jnp.where(dist < max_exact, dist, large)


def _swa_mixer(qkv, q_norm_g, k_norm_g, sinks, rel_bias, batch, seq):
    t = qkv.shape[0]
    nb = seq // WINDOW
    step_rows = WINDOW * SWA_STEP_BLOCKS
    ns = seq // step_rows
    q_w = ATTN_Q_HEADS * ATTN_HEAD_DIM
    kv_w2 = 2 * ATTN_KV_HEADS * ATTN_HEAD_DIM
    kj = jnp.arange(2 * WINDOW)[:, None]
    qi = jnp.arange(WINDOW)[None, :]
    bucket = _t5_causal_bucket(jnp.maximum(qi + WINDOW - kj, 0)).astype(jnp.int32)
    q_gain = jnp.broadcast_to((q_norm_g * (ATTN_HEAD_DIM ** -0.5))[:, None], (ATTN_HEAD_DIM, WINDOW))
    const2 = lambda b, i: (0, 0)
    smem = pl.BlockSpec(memory_space=pltpu.SMEM)
    return pl.pallas_call(
        _swa_kernel,
        grid=(batch, ns),
        in_specs=[
            pl.BlockSpec((step_rows, q_w), lambda b, i: (b * ns + i, 0)),
            pl.BlockSpec((step_rows, kv_w2), lambda b, i: (b * ns + i, q_w // kv_w2)),
            pl.BlockSpec((WINDOW, kv_w2),
                         lambda b, i: (b * nb + jnp.maximum(i * SWA_STEP_BLOCKS - 1, 0), q_w // kv_w2)),
            pl.BlockSpec((2 * WINDOW, WINDOW), const2),
            pl.BlockSpec((ATTN_HEAD_DIM, WINDOW), const2),
            pl.BlockSpec((1, ATTN_HEAD_DIM), const2),
            smem,
            smem,
        ],
        out_specs=pl.BlockSpec((step_rows, q_w), lambda b, i: (b * ns + i, 0)),
        out_shape=jax.ShapeDtypeStruct((t, q_w), BF16),
        scratch_shapes=[pltpu.VMEM((2, ATTN_KV_HEADS, 2 * WINDOW, ATTN_Q_PER_KV * WINDOW), F32),
                        pltpu.VMEM((ATTN_KV_HEADS, 1, ATTN_Q_PER_KV * WINDOW), F32)],
        compiler_params=_cparams("arbitrary", "arbitrary"),
        name="swa_mixer",
    )(qkv, qkv, qkv, bucket, q_gain, k_norm_g.reshape(1, -1), rel_bias, sinks)


def _router_kernel(x_ref, g_ref, sh_ref, sc_ref, rwt_ref, rb_ref,
                   h_ref, lp_ref, gate_ref, tile_n_ref, tile_carry_ref, cnt_ref, carry_ref):
    i = pl.program_id(0)
    tm = x_ref.shape[0]

    @pl.when(i == 0)
    def _():
        carry_ref[...] = jnp.zeros_like(carry_ref)

    h = _norm_modulate(x_ref[...], g_ref[...], sh_ref[0], sc_ref[0]).astype(BF16)
    h_ref[...] = h

    logits = lax.dot_general(rwt_ref[...], h, (((1,), (1,)), ((), ())),
                             preferred_element_type=F32) + rb_ref[...]
    e_iota = lax.broadcasted_iota(jnp.int32, logits.shape, 0)
    work = logits
    sels, vals = [], []
    for k in range(TOP_K):
        m = jnp.max(work, axis=0, keepdims=True)
        idx = jnp.min(jnp.where(work == m, e_iota, N_EXPERTS), axis=0, keepdims=True)
        sel = e_iota == idx
        work = jnp.where(sel, -jnp.inf, work)
        sels.append(sel)
        vals.append(m)
    exps = [jnp.exp(v - vals[0]) for v in vals]
    denom = exps[0] + exps[1] + exps[2] + exps[3]
    for k in range(TOP_K):
        gate_ref[k:k + 1, :] = exps[k] / denom

    chosen = sels[0] | sels[1] | sels[2] | sels[3]
    t_row = lax.broadcasted_iota(jnp.int32, (tm, tm), 0)
    t_col = lax.broadcasted_iota(jnp.int32, (tm, tm), 1)
    before = (t_row < t_col).astype(BF16)
    prior = jnp.dot(chosen.astype(BF16), before, preferred_element_type=F32)
    n = jnp.sum(chosen.astype(F32), axis=1, keepdims=True)
    e_row = lax.broadcasted_iota(jnp.int32, (N_EXPERTS, N_EXPERTS), 0)
    e_col = lax.broadcasted_iota(jnp.int32, (N_EXPERTS, N_EXPERTS), 1)
    run_start = jnp.dot((e_col < e_row).astype(BF16), jnp.broadcast_to(n, (N_EXPERTS, LANES)).astype(BF16),
                        preferred_element_type=F32)[:, :1]
    local = run_start + prior
    for k in range(TOP_K):
        lp_ref[k:k + 1, :] = jnp.sum(jnp.where(sels[k], local, 0.0), axis=0, keepdims=True).astype(jnp.int32)
    tile_n_ref[0] = n.astype(jnp.int32)
    tile_carry_ref[0] = carry_ref[...].astype(jnp.int32)
    total = carry_ref[...] + n
    carry_ref[...] = total
    cnt_ref[...] = total.astype(jnp.int32)


def _router(x, g, shift, scale, r_w, r_b, seq, tm):
    t, d = x.shape
    per_b = seq // tm
    ntiles = t // tm
    return pl.pallas_call(
        _router_kernel,
        grid=(t // tm,),
        in_specs=[
            pl.BlockSpec((tm, d), lambda i: (i, 0)),
            pl.BlockSpec((1, d), lambda i: (0, 0)),
            pl.BlockSpec((1, 1, d), lambda i: (i // per_b, 0, 0)),
            pl.BlockSpec((1, 1, d), lambda i: (i // per_b, 0, 0)),
            pl.BlockSpec((N_EXPERTS, d), lambda i: (0, 0)),
            pl.BlockSpec((N_EXPERTS, 1), lambda i: (0, 0)),
        ],
        out_specs=[
            pl.BlockSpec((tm, d), lambda i: (i, 0)),
            pl.BlockSpec((TOP_K, tm), lambda i: (0, i)),
            pl.BlockSpec((TOP_K, tm), lambda i: (0, i)),
            pl.BlockSpec((1, N_EXPERTS, 1), lambda i: (i, 0, 0)),
            pl.BlockSpec((1, N_EXPERTS, 1), lambda i: (i, 0, 0)),
            pl.BlockSpec((N_EXPERTS, 1), lambda i: (0, 0)),
        ],
        out_shape=[
            jax.ShapeDtypeStruct((t, d), BF16),
            jax.ShapeDtypeStruct((TOP_K, t), jnp.int32),
            jax.ShapeDtypeStruct((TOP_K, t), F32),
            jax.ShapeDtypeStruct((ntiles, N_EXPERTS, 1), jnp.int32),
            jax.ShapeDtypeStruct((ntiles, N_EXPERTS, 1), jnp.int32),
            jax.ShapeDtypeStruct((N_EXPERTS, 1), jnp.int32),
        ],
        scratch_shapes=[pltpu.VMEM((N_EXPERTS, 1), F32)],
        compiler_params=_cparams("arbitrary"),
        name="moe_router",
    )(x, g.reshape(1, d), shift, scale, r_w.T.astype(BF16), r_b.reshape(-1, 1))


def _run_copies(n, src_ref, src_row, dst_ref, dst_row, sem, max_rows, fn):
    for b in range(max_rows.bit_length()):
        size = 1 << b

        @pl.when((n & size) != 0)
        def _():
            lo = n & (size - 1)
            fn(pltpu.make_async_copy(
                src_ref.at[pl.ds(pl.multiple_of((src_row + lo) * ROW_TILES, ROW_TILES), size * ROW_TILES)],
                dst_ref.at[pl.ds(pl.multiple_of((dst_row + lo) * ROW_TILES, ROW_TILES), size * ROW_TILES)],
                sem), b)


def _start_piece(cp, b):
    cp.start()


def _wait_piece(cp, b):
    cp.wait()


def _tile_runs(tile, n_ref, carry_ref, gstart_ref, sorted_ref, grouped_ref, sem, tm, fn, to_grouped, live=None):
    def body(e, run_start):
        n = n_ref[tile * N_EXPERTS + e]
        if live is not None:
            n = jnp.where(live, n, 0)
        slot0 = gstart_ref[e] + carry_ref[tile * N_EXPERTS + e]
        if to_grouped:
            _run_copies(n, sorted_ref, run_start, grouped_ref, slot0, sem, tm, fn)
        else:
            _run_copies(n, grouped_ref, slot0, sorted_ref, run_start, sem, tm, fn)
        return run_start + n

    if live is None:
        lax.fori_loop(0, N_EXPERTS, body, 0)
    else:
        run_start = 0
        for e in range(N_EXPERTS):
            run_start = body(e, run_start)


def _block_copy(src_ref, dst_ref, dst_blk, sem):
    rows = MOE_BLOCK * ROW_TILES
    return pltpu.make_async_copy(src_ref, dst_ref.at[pl.ds(pl.multiple_of(dst_blk * rows, rows), rows)], sem)


def _dispatch_kernel(n_ref, carry_ref, gstart_ref, pad_lo_ref, pad_hi_ref, nused_ref, lp_ref, h_ref, xg_ref,
                     sorted0_ref, sorted1_ref, sorted2_ref, zero_ref, run_sem, sem):
    i = pl.program_id(0)
    last = pl.num_programs(0) - 1
    tm = h_ref.shape[0]
    rows = TOP_K * tm
    nblk =xg_ref.shape[0] // (MOE_BLOCK * ROW_TILES)

    @pl.when(i == 0)
    def _():
        zero_ref[...] = jnp.zeros_like(zero_ref)

        def fill(b, carry):
            _block_copy(zero_ref, xg_ref, b, sem).start()
            return carry

        def fill_wait(b, carry):
            _block_copy(zero_ref, xg_ref, b, sem).wait()
            return carry

        lax.fori_loop(nused_ref[0], nblk, fill, 0)
        lax.fori_loop(nused_ref[0], nblk, fill_wait, 0)

    @pl.when(i < N_EXPERTS)
    def _():
        n_pad = pad_hi_ref[i] - pad_lo_ref[i]
        for fn in (_start_piece, _wait_piece):
            _run_copies(n_pad, zero_ref, 0, xg_ref, pad_lo_ref[i], sem, MOE_BLOCK - 1, fn)

    bufs = (sorted0_ref, sorted1_ref, sorted2_ref)
    nbuf = len(bufs)

    def runs(tile, b, fn, live=None):
        _tile_runs(tile, n_ref, carry_ref, gstart_ref, bufs[b], xg_ref, run_sem.at[b], tm, fn, True, live)

    def wait_tile(b):
        pltpu.make_async_copy(bufs[b], xg_ref.at[pl.ds(0, rows * ROW_TILES)], run_sem.at[b]).wait()

    def step(cur):
        prev, oldest = (cur - 1) % nbuf, (cur + 1) % nbuf
        cur_ref = bufs[cur]

        @pl.when(i >= nbuf)
        def _():
            wait_tile(cur)

        row_id = lax.broadcasted_iota(jnp.int32, (rows, tm), 0)
        perm = row_id == lp_ref[0:1, :]
        for k in range(1, TOP_K):
            perm = perm | (row_id == lp_ref[k:k + 1, :])
        srt = jnp.dot(jnp.where(perm, 1.0, 0.0).astype(BF16), h_ref[...], preferred_element_type=F32)
        for s in range(ROW_TILES):
            cur_ref[pl.ds(s, rows, stride=ROW_TILES), :] = srt[:, s * LANES:(s + 1) * LANES]

        runs(jnp.maximum(i - 1, 0), prev, _start_piece, live=i >= 1)

        @pl.when(i == last)
        def _():
            @pl.when(i >= 2)
            def _():
                wait_tile(oldest)

            @pl.when(i >= 1)
            def _():
                wait_tile(prev)

            runs(i, cur, _start_piece)
            wait_tile(cur)

    for r in range(nbuf):
        @pl.when(i % nbuf == r)
        def _():
            step(r)


def _dispatch(tile_n, tile_carry, group_start, pad_lo, pad_hi, nused, lp_t, h, cap, tm):
    t, d = h.shape
    assert t // tm >= N_EXPERTS
    grid_spec = pltpu.PrefetchScalarGridSpec(
        num_scalar_prefetch=6,
        grid=(t // tm,),
        in_specs=[
            pl.BlockSpec((TOP_K, tm), lambda i, *_: (0, i)),
            pl.BlockSpec((tm, d), lambda i, *_: (i, 0)),
        ],
        out_specs=pl.BlockSpec(memory_space=pl.ANY),
        scratch_shapes=[
            pltpu.VMEM((TOP_K * tm * ROW_TILES, LANES), F32),
            pltpu.VMEM((TOP_K * tm * ROW_TILES, LANES), F32),
            pltpu.VMEM((TOP_K * tm * ROW_TILES, LANES), F32),
            pltpu.VMEM((MOE_BLOCK * ROW_TILES, LANES), F32),
            pltpu.SemaphoreType.DMA((3,)), pltpu.SemaphoreType.DMA,
        ],
    )
    return pl.pallas_call(
        _dispatch_kernel,
        grid_spec=grid_spec,
        out_shape=jax.ShapeDtypeStruct((cap * ROW_TILES, LANES), F32),
        compiler_params=pltpu.CompilerParams(dimension_semantics=("arbitrary",), has_side_effects=True,
                                             vmem_limit_bytes=VMEM_LIMIT),
        name="moe_dispatch",
    )(tile_n, tile_carry, group_start, pad_lo, pad_hi, nused, lp_t, h)


def _weight_copies(w1_hbm, w2_hbm, w1buf, w2buf, sems, layer, e, slot):
    return (pltpu.make_async_copy(w1_hbm.at[layer, e], w1buf.at[slot], sems.at[0, slot]),
            pltpu.make_async_copy(w2_hbm.at[layer, e], w2buf.at[slot], sems.at[1, slot]))


def _expert_kernel(blk_e_ref, nused_ref, next_e_ref, slot_ref, blk_rows_ref, x_ref, w1_hbm, w2_hbm, bg_ref, bl_ref, b2_ref,
                   sel_ref, o_ref, w1buf, w2buf, wg_s, wl_s, w2_s, sems, *, layer):
    blk = pl.program_id(0)
    active = blk < nused_ref[0]
    e = blk_e_ref[blk]
    slot = slot_ref[e]
    new_expert = (blk == 0) | (e != blk_e_ref[jnp.maximum(blk - 1, 0)])
    copies = functools.partial(_weight_copies, w1_hbm, w2_hbm, w1buf, w2buf, sems, layer)

    @pl.when(blk == 0)
    def _():
        for cp in copies(e, slot):
            cp.start()

    @pl.when(active & new_expert)
    def _():
        nxt = next_e_ref[e]

        @pl.when(nxt >= 0)
        def _():
            for cp in copies(nxt, 1 - slot):
                cp.start()

        for cp in copies(e, slot):
            cp.wait()
        half = SPLIT_W // 2
        for j in range(w1buf.shape[2] // SPLIT_W):
            chunk = w1buf[slot, :, j * SPLIT_W:(j + 1) * SPLIT_W].astype(BF16)
            split = jnp.dot(chunk, sel_ref[...], preferred_element_type=F32)
            wg_s[:, j * half:(j + 1) * half] = split[:, :half].astype(BF16)
            wl_s[:, j * half:(j + 1) * half] = split[:, half:].astype(BF16)
        w2_s[...] = w2buf[slot].astype(BF16)

    def mlp(rows):
        x = jnp.concatenate([x_ref[pl.ds(s, rows, stride=ROW_TILES), :]
                             for s in range(ROW_TILES)], axis=1).astype(BF16)
        glu = jnp.dot(x, wg_s[...], preferred_element_type=F32) + bg_ref[0]
        lin = jnp.dot(x, wl_s[...], preferred_element_type=F32) + bl_ref[0]
        glu = jnp.minimum(glu, SWIGLU_LIMIT)
        lin = jnp.clip(lin, -SWIGLU_LIMIT, SWIGLU_LIMIT)
        act = glu * jax.nn.sigmoid(SWIGLU_ALPHA * glu) * (lin + 1.0)
        y = jnp.dot(act.astype(BF16), w2_s[...], preferred_element_type=F32) + b2_ref[0]
        for s in range(ROW_TILES):
            o_ref[pl.ds(s, rows, stride=ROW_TILES), :] = y[:, s * LANES:(s + 1) * LANES]
        if rows < MOE_BLOCK:
            o_ref[pl.ds(rows * ROW_TILES, (MOE_BLOCK - rows) * ROW_TILES), :] = jnp.zeros(
                ((MOE_BLOCK - rows) * ROW_TILES, LANES), o_ref.dtype)

    valid_rows = jnp.where(active, blk_rows_ref[blk], 0)
    for rows in range(MOE_SUB, MOE_BLOCK + 1, MOE_SUB):
        @pl.when((valid_rows > rows - MOE_SUB) & (valid_rows <= rows))
        def _():
            mlp(rows)

    @pl.when(valid_rows <= 0)
    def _():
        o_ref[...] = jnp.zeros_like(o_ref)


def _experts(layer, blk_e, nused, next_e, e_slot, blk_rows, xg_rows, w1_all, w2_all, b_glu, b_lin, b2):
    nblk = blk_e.shape[0]
    d = D_MODEL
    ff = w2_all.shape[2]
    col = jnp.arange(SPLIT_W)
    sel = (jnp.arange(SPLIT_W)[None, :] == ((col % 2) * (SPLIT_W // 2) + col // 2)[:, None]).astype(BF16)

    def blk_map(b, blk_e_ref, nused_ref, *_):
        return (jnp.minimum(b, nused_ref[0] - 1), 0)

    def e_map(b, blk_e_ref, nused_ref, *_):
        return (blk_e_ref[jnp.minimum(b, nused_ref[0] - 1)], 0, 0)

    grid_spec = pltpu.PrefetchScalarGridSpec(
        num_scalar_prefetch=5,
        grid=(nblk,),
        in_specs=[
            pl.BlockSpec((MOE_BLOCK * ROW_TILES, LANES), blk_map),
            pl.BlockSpec(memory_space=pl.ANY),
            pl.BlockSpec(memory_space=pl.ANY),
            pl.BlockSpec((1, 1, ff), e_map),
            pl.BlockSpec((1, 1, ff), e_map),
            pl.BlockSpec((1, 1, d), e_map),
            pl.BlockSpec((SPLIT_W, SPLIT_W), lambda b, *_: (0, 0)),
        ],
        out_specs=pl.BlockSpec((MOE_BLOCK * ROW_TILES, LANES), lambda b, *_: (b, 0)),
        scratch_shapes=[
            pltpu.VMEM((2, d, 2 * ff), F32), pltpu.VMEM((2, ff, d), F32),
            pltpu.VMEM((d, ff), BF16), pltpu.VMEM((d, ff), BF16), pltpu.VMEM((ff, d), BF16),
            pltpu.SemaphoreType.DMA((2, 2)),
        ],
    )
    return pl.pallas_call(
        functools.partial(_expert_kernel, layer=layer),
        grid_spec=grid_spec,
        out_shape=jax.ShapeDtypeStruct(xg_rows.shape, F32),
        compiler_params=_cparams("arbitrary"),
        name="moe_experts",
    )(blk_e, nused, next_e, e_slot, blk_rows, xg_rows, w1_all, w2_all, b_glu, b_lin, b2, sel)


def _combine_kernel(n_ref, carry_ref, gstart_ref, yg_ref, lp_ref, gate_ref, x_ref, g2_ref, o_ref, sorted_ref, sem):
    i = pl.program_id(0)
    ntiles = pl.num_programs(0)
    tm = x_ref.shape[0]
    rows = TOP_K * tm
    slot = i % 2

    def runs(tile, buf, fn, live=None):
        _tile_runs(tile, n_ref, carry_ref, gstart_ref, sorted_ref.at[buf], yg_ref, sem.at[buf], tm, fn, False, live)

    @pl.when(i == 0)
    def _():
        runs(0, 0, _start_piece)

    runs(jnp.minimum(i + 1, ntiles - 1), 1 - slot, _start_piece, live=i + 1 < ntiles)

    eye = (lax.broadcasted_iota(jnp.int32, (tm, tm), 0) ==
           lax.broadcasted_iota(jnp.int32, (tm, tm), 1)).astype(F32)
    rows_t = jnp.concatenate([gate_ref[...], lp_ref[...].astype(F32)], axis=0)
    cols = lax.dot_general(eye, rows_t, (((1,), (1,)), ((), ())), preferred_element_type=F32, precision=HI)
    col_id = lax.broadcasted_iota(jnp.int32, (tm, rows), 1)
    weights = jnp.zeros((tm, rows), F32)
    for k in range(TOP_K):
        lp_col = (cols[:, TOP_K + k:TOP_K + k + 1] + 0.5).astype(jnp.int32)
        weights = weights + jnp.where(col_id == lp_col, cols[:, k:k + 1], 0.0)

    pltpu.make_async_copy(yg_ref.at[pl.ds(0, rows * ROW_TILES)], sorted_ref.at[slot], sem.at[slot]).wait()
    y = jnp.concatenate(
        [sorted_ref[slot, pl.ds(s, rows, stride=ROW_TILES), :] for s in range(ROW_TILES)], axis=1).astype(BF16)
    acc = jnp.dot(weights.astype(BF16), y, preferred_element_type=F32)
    o_ref[...] = x_ref[...] + g2_ref[0] * acc


def _combine(tile_n, tile_carry, group_start, yg_rows, lp_t, gate_t, x, gate2, seq, tm):
    t, d = x.shape
    per_b = seq // tm
    grid_spec = pltpu.PrefetchScalarGridSpec(
        num_scalar_prefetch=3,
        grid=(t // tm,),
        in_specs=[
            pl.BlockSpec(memory_space=pl.ANY),
            pl.BlockSpec((TOP_K, tm), lambda i, *_: (0, i)),
            pl.BlockSpec((TOP_K, tm), lambda i, *_: (0, i)),
            pl.BlockSpec((tm, d), lambda i, *_: (i, 0)),
            pl.BlockSpec((1, 1, d), lambda i, *_: (i // per_b, 0, 0)),
        ],
        out_specs=pl.BlockSpec((tm, d), lambda i, *_: (i, 0)),
        scratch_shapes=[pltpu.VMEM((2, TOP_K * tm * ROW_TILES, LANES), F32), pltpu.SemaphoreType.DMA((2,))],
    )
    return pl.pallas_call(
        _combine_kernel,
        grid_spec=grid_spec,
        out_shape=jax.ShapeDtypeStruct((t, d), F32),
        compiler_params=_cparams("arbitrary"),
        name="moe_combine",
    )(tile_n, tile_carry, group_start, yg_rows, lp_t, gate_t, x, gate2)


def _moe(layer, x, g, shift, scale, gate2, r_w, r_b, w1_all, b1_glu, b1_lin, w2_all, b2, seq):
    t = x.shape[0]
    h, lp_t, gate_t, tile_n, tile_carry, counts = _router(x, g, shift, scale, r_w, r_b, seq, MOE_TILE)
    tile_n = tile_n.reshape(-1)
    tile_carry = tile_carry.reshape(-1)
    counts = counts.reshape(-1)
    padded = ((counts + MOE_BLOCK - 1) // MOE_BLOCK) * MOE_BLOCK
    group_end = jnp.cumsum(padded)
    group_start = (group_end - padded).astype(jnp.int32)
    cap = t * TOP_K + N_EXPERTS * MOE_BLOCK
    nblk = cap // MOE_BLOCK
    blk_start = jnp.arange(nblk, dtype=jnp.int32) * MOE_BLOCK
    e_ids = jnp.arange(N_EXPERTS, dtype=jnp.int32)
    past = (blk_start[None, :] >= group_end[:, None]).astype(jnp.int32)
    blk_e = jnp.minimum(jnp.sum(past, axis=0), N_EXPERTS - 1).astype(jnp.int32)
    nused = (group_end[-1:] // MOE_BLOCK).astype(jnp.int32)

    pad_lo = (group_start + counts).astype(jnp.int32)
    pad_hi = group_end.astype(jnp.int32)
    blk_pad_lo = jnp.sum(jnp.where(blk_e[None, :] == e_ids[:, None], pad_lo[:, None], 0), axis=0)
    blk_rows = jnp.clip(blk_pad_lo - blk_start, 0, MOE_BLOCK).astype(jnp.int32)
    xg_rows = _dispatch(tile_n, tile_carry, group_start, pad_lo, pad_hi, nused, lp_t, h, cap, MOE_TILE)
    nonempty = counts > 0
    later = jnp.where(nonempty[None, :] & (e_ids[None, :] > e_ids[:, None]), e_ids[None, :], N_EXPERTS)
    next_e = jnp.min(later, axis=1)
    next_e = jnp.where(next_e == N_EXPERTS, -1, next_e).astype(jnp.int32)
    e_slot = ((jnp.cumsum(nonempty.astype(jnp.int32)) - 1) % 2).astype(jnp.int32)
    yg_rows = _experts(layer, blk_e, nused, next_e, e_slot, blk_rows, xg_rows, w1_all, w2_all,
                       b1_glu[:, None, :], b1_lin[:, None, :], b2[:, None, :])
    return _combine(tile_n, tile_carry, group_start, yg_rows, lp_t, gate_t, x, gate2, seq, MOE_TILE)


def kernel(x, c, ada_w, ada_b, norm1_g, norm2_g, m_in_w, m_conv_w, m_conv_b, m_dt_bias, m_A_log, m_D, m_norm_g, m_out_w, a_qkv_w, a_q_norm_g, a_k_norm_g, a_sinks, a_out_w, rel_bias, r_w, r_b, e_w1, e_b1, e_w2, e_b2):
    batch, seq, d = x.shape
    depth = ada_w.shape[0]
    t = batch * seq
    xf = x.reshape(t, d)

    mod = _adaln(c, ada_w, ada_b)[:, :batch]

    b1_split = jnp.moveaxis(e_b1.reshape(depth, N_EXPERTS, -1, 2), -1, 1)

    for i in range(depth):
        parts =[mod[i, :, p * d:(p + 1) * d].reshape(batch, 1, d) for p in range(6)]
        sh1, sc1, g1, sh2, sc2, g2 = parts
        j = i // 2
        if i % 2 == 0:
            w_zxbc = m_in_w[j, :, :SSM_ZXBC].astype(BF16)
            w_dt = jnp.pad(m_in_w[j, :, SSM_ZXBC:], ((0, 0), (0, LANES - SSM_HEADS)))
            zxbc, dt_raw = _norm_matmul(xf, norm1_g[i], sh1, sc1, (w_zxbc, w_dt), seq)
            y = _ssd_mixer(zxbc, dt_raw, m_conv_w[j], m_conv_b[j], m_dt_bias[j], m_A_log[j], m_D[j],
                           m_norm_g[j], batch, seq)
            xf = _matmul_residual(y, m_out_w[j].astype(BF16), xf, g1, seq)
        else:
            qkv, = _norm_matmul(xf, norm1_g[i], sh1, sc1, (a_qkv_w[j].astype(BF16),), seq)
            y = _swa_mixer(qkv, a_q_norm_g[j], a_k_norm_g[j], a_sinks[j], rel_bias, batch, seq)
            xf = _matmul_residual(y, a_out_w[j].astype(BF16), xf, g1, seq)
        xf = _moe(i, xf, norm2_g[i], sh2, sc2, g2, r_w[i], r_b[i], e_w1, b1_split[i, 0], b1_split[i, 1],
                  e_w2, e_b2[i], seq)
    return xf.reshape(batch, seq, d)
```

```python
import functools
import math

import jax
import jax.numpy as jnp
from jax import lax
from jax.experimental import pallas as pl
from jax.experimental.pallas import tpu as pltpu

D_MODEL = 1024
EPS = 1e-6
LANES = 128
SUBLANES = 8
ROW_TILES = D_MODEL // LANES

SSM_D_INNER = 2048
SSM_HEAD_DIM = 64
SSM_HEADS = 32
SSM_GROUPS = 4
SSM_STATE = 128
SSM_CONV = 4
SSM_CHUNK = 128
SSM_STEP_CHUNKS = 1
SSM_GN = SSM_GROUPS * SSM_STATE
SSM_ZXBC = 2 * SSM_D_INNER + 2 * SSM_GN
SSM_GROUP_W = SSM_D_INNER // SSM_GROUPS

ATTN_HEAD_DIM = 64
ATTN_Q_HEADS = 16
ATTN_KV_HEADS = 4
ATTN_Q_PER_KV = 4
WINDOW = 128
SWA_STEP_BLOCKS = 2
REL_BUCKETS = 32
REL_MAX_DIST = 128

N_EXPERTS = 32
TOP_K = 4
SWIGLU_ALPHA = 1.702
SWIGLU_LIMIT = 7.0
MOE_BLOCK = 512
MOE_SUB = 128
MOE_TILE = 256
SPLIT_W = 256

VMEM_LIMIT = 56 * 1024 * 1024
HI = lax.Precision.HIGHEST
F32 = jnp.float32
BF16 = jnp.bfloat16


def _cparams(*sem):
    return pltpu.CompilerParams(dimension_semantics=sem, vmem_limit_bytes=VMEM_LIMIT)


def _norm_modulate(x, g, shift, scale):
    ms = jnp.mean(x * x, axis=-1, keepdims=True)
    return x * lax.rsqrt(ms + EPS) * g * (1.0 + scale) + shift


def _adaln_kernel(ct_ref, w_ref, b_ref, o_ref, *, batch):
    c_t = ct_ref[...]
    act_t = c_t * jax.nn.sigmoid(c_t)
    w = w_ref[0]
    rows = [jnp.sum(act_t[:, b:b + 1] * w, axis=0, keepdims=True) + b_ref[0] for b in range(batch)]
    rows.append(jnp.zeros((SUBLANES - batch, w.shape[1]), F32))
    o_ref[0] = jnp.concatenate(rows, axis=0)


def _adaln(c, ada_w, ada_b):
    depth, d, n = ada_w.shape
    batch = c.shape[0]
    tn = 1536
    c_t = jnp.zeros((d, SUBLANES), F32).at[:, :batch].set(c.T)
    return pl.pallas_call(
        functools.partial(_adaln_kernel, batch=batch),
        grid=(depth, n // tn),
        in_specs=[
            pl.BlockSpec((d, SUBLANES), lambda i, j: (0, 0)),
            pl.BlockSpec((1, d, tn), lambda i, j: (i, 0, j)),
            pl.BlockSpec((1, 1, tn), lambda i, j: (i, 0, j)),
        ],
        out_specs=pl.BlockSpec((1, SUBLANES, tn), lambda i, j: (i, 0, j)),
        out_shape=jax.ShapeDtypeStruct((depth, SUBLANES, n), F32),
        compiler_params=_cparams("arbitrary", "arbitrary"),
        name="adaln",
    )(c_t, ada_w, ada_b.reshape(depth, 1, n))


def _norm_matmul_kernel(x_ref, g_ref, sh_ref, sc_ref, *refs):
    w_refs, o_refs = refs[:len(refs) // 2], refs[len(refs) // 2:]
    h = _norm_modulate(x_ref[...], g_ref[...], sh_ref[0], sc_ref[0]).astype(BF16)
    for w_ref, o_ref in zip(w_refs, o_refs):
        o_ref[...] = jnp.dot(h, w_ref[...].astype(BF16), preferred_element_type=F32)


def _norm_matmul(x, g, shift, scale, weights_bf16, seq, tm=512):
    t, d = x.shape
    per_b = seq // tm
    return pl.pallas_call(
        _norm_matmul_kernel,
        grid=(t // tm,),
        in_specs=[
            pl.BlockSpec((tm, d), lambda i: (i, 0)),
            pl.BlockSpec((1, d), lambda i: (0, 0)),
            pl.BlockSpec((1, 1, d), lambda i: (i // per_b, 0, 0)),
            pl.BlockSpec((1, 1, d), lambda i: (i // per_b, 0, 0)),
        ] + [pl.BlockSpec(w.shape, lambda i: (0, 0), pipeline_mode=pl.Buffered(1))
             for w in weights_bf16],
        out_specs=[pl.BlockSpec((tm, w.shape[1]), lambda i: (i, 0)) for w in weights_bf16],
        out_shape=[jax.ShapeDtypeStruct((t, w.shape[1]), F32) for w in weights_bf16],
        compiler_params=_cparams("arbitrary"),
        name="norm_matmul",
    )(x, g.reshape(1, d), shift, scale, *weights_bf16)


def _matmul_residual_kernel(y_ref, w_ref, x_ref, gate_ref, o_ref):
    acc = jnp.dot(y_ref[...], w_ref[...], preferred_element_type=F32)
    o_ref[...] = x_ref[...] + gate_ref[0] * acc


def _matmul_residual(y_bf16, w_bf16, x, gate, seq, tm=512):
    t, k = y_bf16.shape
    d = x.shape[1]
    per_b = seq // tm
    return pl.pallas_call(
        _matmul_residual_kernel,
        grid=(t // tm,),
        in_specs=[
            pl.BlockSpec((tm, k), lambda i: (i, 0)),
            pl.BlockSpec((k, d), lambda i: (0, 0)),
            pl.BlockSpec((tm, d), lambda i: (i, 0)),
            pl.BlockSpec((1, 1, d), lambda i: (i // per_b, 0, 0)),
        ],
        out_specs=pl.BlockSpec((tm, d), lambda i: (i, 0)),
        out_shape=jax.ShapeDtypeStruct((t, d), F32),
        compiler_params=_cparams("arbitrary"),
        name="matmul_residual",
    )(y_bf16, w_bf16, x, gate)


def _causal_conv_silu(cur, prev_tail, w, b):
    rows = lax.broadcasted_iota(jnp.int32, (SUBLANES, cur.shape[1]), 0)
    acc = b + w[SSM_CONV - 1:SSM_CONV] * cur
    for d in range(1, SSM_CONV):
        rolled = pltpu.roll(cur, d, axis=0)
        top = jnp.where(rows < d, pltpu.roll(prev_tail, d, axis=0), rolled[0:SUBLANES])
        shifted = jnp.concatenate([top, rolled[SUBLANES:]], axis=0)
        acc = acc + w[SSM_CONV - 1 - d:SSM_CONV - d] * shifted
    return acc * jax.nn.sigmoid(acc)


def _split3(x):
    hi = x.astype(BF16)
    rem = x - hi.astype(F32)
    mid = rem.astype(BF16)
    lo = (rem - mid.astype(F32)).astype(BF16)
    return jnp.concatenate([hi, mid, lo], axis=1)


def _ssd_kernel(z_ref, xs_ref, bc_ref, dt_ref, *refs):
    o_ref, tailx_ref, tailb_ref, state_ref = refs[-4:]

    @pl.when(pl.program_id(1) == 0)
    def _():
        tailx_ref[...] = jnp.zeros_like(tailx_ref)
        tailb_ref[...] = jnp.zeros_like(tailb_ref)
        state_ref[...] = jnp.zeros_like(state_ref)

    for u in range(SSM_STEP_CHUNKS):
        rows = pl.ds(u * SSM_CHUNK, SSM_CHUNK)
        _ssd_chunk(z_ref.at[rows], xs_ref.at[rows], bc_ref.at[rows], dt_ref.at[rows], *refs[:-4],
                   o_ref.at[rows], tailx_ref, tailb_ref, state_ref)


def _ssd_chunk(z_ref, xs_ref, bc_ref, dt_ref, cwx_ref, cbx_ref, cwb_ref, cbb_ref, dtb_ref, alog_ref,
               dskip_ref, ng_ref, hexp_ref, lexp_ref, o_ref, tailx_ref, tailb_ref, state_ref):
    xs_raw = xs_ref[...]
    bc_raw = bc_ref[...]
    xs = _causal_conv_silu(xs_raw, tailx_ref[...], cwx_ref[...], cbx_ref[...])
    bc = _causal_conv_silu(bc_raw, tailb_ref[...], cwb_ref[...], cbb_ref[...])
    tailx_ref[...] = xs_raw[SSM_CHUNK - SUBLANES:]
    tailb_ref[...] = bc_raw[SSM_CHUNK - SUBLANES:]

    dt_in = dt_ref[...][:, :SSM_HEADS] + dtb_ref[...]
    dt = jnp.maximum(dt_in, 0.0) + jnp.log1p(jnp.exp(-jnp.abs(dt_in)))
    a_neg = -jnp.exp(alog_ref[...])
    d_a = dt * a_neg
    li = lax.broadcasted_iota(jnp.int32, (SSM_CHUNK, SSM_CHUNK), 0)
    si = lax.broadcasted_iota(jnp.int32, (SSM_CHUNK, SSM_CHUNK), 1)
    causal = li >= si
    tri = causal.astype(F32)
    a_cs = jnp.dot(tri, d_a, preferred_element_type=F32, precision=HI)
    a_cs_t = lax.dot_general(d_a, tri, (((0,), (1,)), ((), ())),
                             preferred_element_type=F32, precision=HI)
    a_last = a_cs[SSM_CHUNK - 1:SSM_CHUNK]
    e_out = jnp.exp(a_cs)
    e_state = jnp.exp(a_last - a_cs) * dt
    small = jnp.concatenate([dt, e_out, e_state], axis=0)
    wide = jnp.dot(_split3(small), hexp_ref[...], preferred_element_type=F32)
    dt_w = wide[0:SSM_CHUNK]
    e_out_w = wide[SSM_CHUNK:2 * SSM_CHUNK]
    e_state_w = wide[2 * SSM_CHUNK:]
    a_col = jnp.dot(_split3(a_cs), lexp_ref[...], preferred_element_type=F32)

    x_dt = (xs * dt_w).astype(BF16)
    x_state = (xs * e_state_w).astype(BF16)
    chunk_decay_w = e_out_w[SSM_CHUNK - 1:SSM_CHUNK]

    heads_per_group = SSM_HEADS // SSM_GROUPS
    y_parts = []
    for g in range(SSM_GROUPS):
        b_g = bc[:, g * SSM_STATE:(g + 1) * SSM_STATE].astype(BF16)
        c_g = bc[:, SSM_GN + g * SSM_STATE:SSM_GN + (g + 1) * SSM_STATE].astype(BF16)
        cb = lax.dot_general(c_g, b_g, (((1,), (1,)), ((), ())), preferred_element_type=F32)
        gsl = slice(g * SSM_GROUP_W, (g + 1) * SSM_GROUP_W)
        h_prev = state_ref[g]
        y_off = jnp.dot(c_g, h_prev.astype(BF16), preferred_element_type=F32) * e_out_w[:, gsl]
        diag = []
        for r in range(heads_per_group):
            h = g * heads_per_group + r
            seg = a_col[:, h * SSM_CHUNK:(h + 1) * SSM_CHUNK] - a_cs_t[h:h + 1, :]
            decay = jnp.exp(jnp.where(causal, seg, -jnp.inf))
            m = (cb * decay).astype(BF16)
            diag.append(jnp.dot(m, x_dt[:, h * SSM_HEAD_DIM:(h + 1) * SSM_HEAD_DIM],
                                preferred_element_type=F32))
        y_parts.append(jnp.concatenate(diag, axis=1) + y_off)
        upd = lax.dot_general(b_g, x_state[:, gsl], (((0,), (0,)), ((), ())), preferred_element_type=F32)
        state_ref[g] = h_prev * chunk_decay_w[:, gsl] + upd

    y = jnp.concatenate(y_parts, axis=1) + dskip_ref[...] * xs
    z = z_ref[...]
    y = y * (z * jax.nn.sigmoid(z))
    normed = []
    for g in range(SSM_GROUPS):
        y_g = y[:, g * SSM_GROUP_W:(g + 1) * SSM_GROUP_W]
        normed.append(y_g * lax.rsqrt(jnp.mean(y_g * y_g, axis=-1, keepdims=True) + EPS))
    o_ref[...] = (jnp.concatenate(normed, axis=1) * ng_ref[...]).astype(o_ref.dtype)


def _ssd_mixer(zxbc, dt_raw, conv_w, conv_b, dt_bias, a_log, d_skip, norm_g, batch, seq):
    t = zxbc.shape[0]
    step_rows = SSM_CHUNK * SSM_STEP_CHUNKS
    nc = seq // step_rows
    head_expand = jnp.tile(jnp.repeat(jnp.eye(SSM_HEADS, dtype=BF16), SSM_HEAD_DIM, axis=1), (3, 1))
    lane_expand = jnp.tile(jnp.repeat(jnp.eye(SSM_HEADS, dtype=BF16), SSM_CHUNK, axis=1), (3, 1))
    row = lambda b, c: (b * nc + c, 0)
    const2 = lambda b, c: (0, 0)
    bc_w = 2 * SSM_GN
    return pl.pallas_call(
        _ssd_kernel,
        grid=(batch, nc),
        in_specs=[
            pl.BlockSpec((step_rows, SSM_D_INNER), row),
            pl.BlockSpec((step_rows, SSM_D_INNER), lambda b, c: (b * nc + c, 1)),
            pl.BlockSpec((step_rows, bc_w), lambda b, c: (b * nc + c, 2 * SSM_D_INNER // bc_w)),
            pl.BlockSpec((step_rows, LANES), row),
            pl.BlockSpec((SSM_CONV, SSM_D_INNER), const2),
            pl.BlockSpec((1, SSM_D_INNER), const2),
            pl.BlockSpec((SSM_CONV, bc_w), const2),
            pl.BlockSpec((1, bc_w), const2),
            pl.BlockSpec((1, SSM_HEADS), const2),
            pl.BlockSpec((1, SSM_HEADS), const2),
            pl.BlockSpec((1, SSM_D_INNER), const2),
            pl.BlockSpec((1, SSM_D_INNER), const2),
            pl.BlockSpec((3 * SSM_HEADS, SSM_D_INNER), const2),
            pl.BlockSpec((3 * SSM_HEADS, SSM_HEADS * SSM_CHUNK), const2),
        ],
        out_specs=pl.BlockSpec((step_rows, SSM_D_INNER), row),
        out_shape=jax.ShapeDtypeStruct((t, SSM_D_INNER), BF16),
        scratch_shapes=[
            pltpu.VMEM((SUBLANES, SSM_D_INNER), F32),
            pltpu.VMEM((SUBLANES, bc_w), F32),
            pltpu.VMEM((SSM_GROUPS, SSM_STATE, SSM_GROUP_W), F32),
        ],
        compiler_params=_cparams("arbitrary", "arbitrary"),
        name="ssd_mixer",
    )(zxbc, zxbc, zxbc, dt_raw,
      conv_w[:, :SSM_D_INNER], conv_b[:SSM_D_INNER].reshape(1, -1),
      conv_w[:, SSM_D_INNER:], conv_b[SSM_D_INNER:].reshape(1, -1),
      dt_bias.reshape(1, -1), a_log.reshape(1, -1),
      jnp.repeat(d_skip, SSM_HEAD_DIM).reshape(1, -1), norm_g.reshape(1, -1),
      head_expand, lane_expand)


def _head_rms(x, g):
    return x * lax.rsqrt(jnp.mean(x * x, axis=-1, keepdims=True) + EPS) * g


def _swa_kernel(q_ref, kvc_ref, kvp_ref, bucket_ref, qg_ref, kg_ref, rel_ref, sink_ref, o_ref,
                bias_ref, sinkrow_ref):
    b = pl.program_id(0)
    i = pl.program_id(1)

    @pl.when((b == 0) & (i == 0))
    def _():
        bucket = bucket_ref[...]
        kj = lax.broadcasted_iota(jnp.int32, (2 * WINDOW, WINDOW), 0)
        qi = lax.broadcasted_iota(jnp.int32, (2 * WINDOW, WINDOW), 1)
        dist = qi + WINDOW - kj
        band = (dist >= 0) & (dist < WINDOW)
        for h in range(ATTN_Q_HEADS):
            g, r = divmod(h, ATTN_Q_PER_KV)
            acc = jnp.zeros(bucket.shape, F32)
            for k in range(REL_BUCKETS):
                acc = jnp.where(bucket == k, rel_ref[k, h], acc)
            cols = slice(r * WINDOW, (r + 1) * WINDOW)
            bias_ref[0, g, :, cols] = jnp.where(band, acc, -jnp.inf)
            bias_ref[1, g, :, cols] = jnp.where(band & (kj >= WINDOW), acc, -jnp.inf)
            sinkrow_ref[g, :, cols] = jnp.full((1, WINDOW), sink_ref[h], F32)

    for u in range(SWA_STEP_BLOCKS):
        rows = slice(u * WINDOW, (u + 1) * WINDOW)
        kv_p = kvp_ref[...] if u == 0 else kvc_ref[(u - 1) * WINDOW:u * WINDOW, :]
        first = (i == 0).astype(jnp.int32) if u == 0 else 0
        o_ref[rows, :] = _swa_block(q_ref[rows, :], kvc_ref[rows, :], kv_p, first,
                                    qg_ref, kg_ref, bias_ref, sinkrow_ref).astype(o_ref.dtype)


def _swa_block(q_rows, kv_c, kv_p, first, qg_ref, kg_ref, bias_ref, sinkrow_ref):
    q_t = q_rows.T
    kv_w = ATTN_KV_HEADS * ATTN_HEAD_DIM
    q_gain = qg_ref[...]
    outs = []
    for g in range(ATTN_KV_HEADS):
        ksl = slice(g * ATTN_HEAD_DIM, (g + 1) * ATTN_HEAD_DIM)
        vsl = slice(kv_w + g * ATTN_HEAD_DIM, kv_w + (g + 1) * ATTN_HEAD_DIM)
        k = jnp.concatenate([kv_p[:, ksl], kv_c[:, ksl]], axis=0)
        v = jnp.concatenate([kv_p[:, vsl], kv_c[:, vsl]], axis=0).astype(BF16)
        k = _head_rms(k, kg_ref[...]).astype(BF16)
        q_heads = []
        for r in range(ATTN_Q_PER_KV):
            h = g * ATTN_Q_PER_KV + r
            q_h = q_t[h * ATTN_HEAD_DIM:(h + 1) * ATTN_HEAD_DIM]
            inv = lax.rsqrt(jnp.mean(q_h * q_h, axis=0, keepdims=True) + EPS)
            q_heads.append((q_h * inv * q_gain).astype(BF16))
        q = jnp.concatenate(q_heads, axis=1)
        s = jnp.dot(k, q, preferred_element_type=F32) + bias_ref[first, g]
        sink = sinkrow_ref[g]
        m = jnp.maximum(jnp.max(s, axis=0, keepdims=True), sink)
        p = jnp.exp(s - m)
        denom = jnp.sum(p, axis=0, keepdims=True) + jnp.exp(sink - m)
        pv = lax.dot_general(v, p.astype(BF16), (((0,), (0,)), ((), ())), preferred_element_type=F32)
        pv = pv * (1.0 / denom)
        outs.extend(pv[:, r * WINDOW:(r + 1) * WINDOW] for r in range(ATTN_Q_PER_KV))
    return jnp.concatenate(outs, axis=0).T


def _t5_causal_bucket(dist):
    max_exact = REL_BUCKETS // 2
    d = jnp.maximum(dist, 1).astype(F32)
    large = max_exact + (jnp.log(d / max_exact) / math.log(REL_MAX_DIST / max_exact)
                         * (REL_BUCKETS - max_exact)).astype(jnp.int32)
    large = jnp.minimum(large, REL_BUCKETS - 1)
    return jnp.where(dist < max_exact, dist, large)


def _swa_mixer(qkv, q_norm_g, k_norm_g, sinks, rel_bias, batch, seq):
    t = qkv.shape[0]
    nb = seq // WINDOW
    step_rows = WINDOW * SWA_STEP_BLOCKS
    ns = seq // step_rows
    q_w = ATTN_Q_HEADS * ATTN_HEAD_DIM
    kv_w2 = 2 * ATTN_KV_HEADS * ATTN_HEAD_DIM
    kj = jnp.arange(2 * WINDOW)[:, None]
    qi = jnp.arange(WINDOW)[None, :]
    bucket = _t5_causal_bucket(jnp.maximum(qi + WINDOW - kj, 0)).astype(jnp.int32)
    q_gain = jnp.broadcast_to((q_norm_g * (ATTN_HEAD_DIM ** -0.5))[:, None], (ATTN_HEAD_DIM, WINDOW))
    const2 = lambda b, i: (0, 0)
    smem = pl.BlockSpec(memory_space=pltpu.SMEM)
    return pl.pallas_call(
        _swa_kernel,
        grid=(batch, ns),
        in_specs=[
            pl.BlockSpec((step_rows, q_w), lambda b, i: (b * ns + i, 0)),
            pl.BlockSpec((step_rows, kv_w2), lambda b, i: (b * ns + i, q_w // kv_w2)),
            pl.BlockSpec((WINDOW, kv_w2),
                         lambda b, i: (b * nb + jnp.maximum(i * SWA_STEP_BLOCKS - 1, 0), q_w // kv_w2)),
            pl.BlockSpec((2 * WINDOW, WINDOW), const2),
            pl.BlockSpec((ATTN_HEAD_DIM, WINDOW), const2),
            pl.BlockSpec((1, ATTN_HEAD_DIM), const2),
            smem,
            smem,
        ],
        out_specs=pl.BlockSpec((step_rows, q_w), lambda b, i: (b * ns + i, 0)),
        out_shape=jax.ShapeDtypeStruct((t, q_w), BF16),
        scratch_shapes=[pltpu.VMEM((2, ATTN_KV_HEADS, 2 * WINDOW, ATTN_Q_PER_KV * WINDOW), F32),
                        pltpu.VMEM((ATTN_KV_HEADS, 1, ATTN_Q_PER_KV * WINDOW), F32)],
        compiler_params=_cparams("arbitrary", "arbitrary"),
        name="swa_mixer",
    )(qkv, qkv, qkv, bucket, q_gain, k_norm_g.reshape(1, -1), rel_bias, sinks)


def _router_kernel(x_ref, g_ref, sh_ref, sc_ref, rwt_ref, rb_ref,
                   h_ref, lp_ref, gate_ref, tile_n_ref, tile_carry_ref, cnt_ref, carry_ref):
    i = pl.program_id(0)
    tm = x_ref.shape[0]

    @pl.when(i == 0)
    def _():
        carry_ref[...] = jnp.zeros_like(carry_ref)

    h = _norm_modulate(x_ref[...], g_ref[...], sh_ref[0], sc_ref[0]).astype(BF16)
    h_ref[...] = h

    logits = lax.dot_general(rwt_ref[...], h, (((1,), (1,)), ((), ())),
                             preferred_element_type=F32) + rb_ref[...]
    e_iota = lax.broadcasted_iota(jnp.int32, logits.shape, 0)
    work = logits
    sels, vals = [], []
    for k in range(TOP_K):
        m = jnp.max(work, axis=0, keepdims=True)
        idx = jnp.min(jnp.where(work == m, e_iota, N_EXPERTS), axis=0, keepdims=True)
        sel = e_iota == idx
        work = jnp.where(sel, -jnp.inf, work)
        sels.append(sel)
        vals.append(m)
    exps = [jnp.exp(v - vals[0]) for v in vals]
    denom = exps[0] + exps[1] + exps[2] + exps[3]
    for k in range(TOP_K):
        gate_ref[k:k + 1, :] = exps[k] / denom

    chosen = sels[0] | sels[1] | sels[2] | sels[3]
    t_row = lax.broadcasted_iota(jnp.int32, (tm, tm), 0)
    t_col = lax.broadcasted_iota(jnp.int32, (tm, tm), 1)
    before = (t_row < t_col).astype(BF16)
    prior = jnp.dot(chosen.astype(BF16), before, preferred_element_type=F32)
    n = jnp.sum(chosen.astype(F32), axis=1, keepdims=True)
    e_row = lax.broadcasted_iota(jnp.int32, (N_EXPERTS, N_EXPERTS), 0)
    e_col = lax.broadcasted_iota(jnp.int32, (N_EXPERTS, N_EXPERTS), 1)
    run_start = jnp.dot((e_col < e_row).astype(BF16), jnp.broadcast_to(n, (N_EXPERTS, LANES)).astype(BF16),
                        preferred_element_type=F32)[:, :1]
    local = run_start + prior
    for k in range(TOP_K):
        lp_ref[k:k + 1, :] = jnp.sum(jnp.where(sels[k], local, 0.0), axis=0, keepdims=True).astype(jnp.int32)
    tile_n_ref[0] = n.astype(jnp.int32)
    tile_carry_ref[0] = carry_ref[...].astype(jnp.int32)
    total = carry_ref[...] + n
    carry_ref[...] = total
    cnt_ref[...] = total.astype(jnp.int32)


def _router(x, g, shift, scale, r_w, r_b, seq, tm):
    t, d = x.shape
    per_b = seq // tm
    ntiles = t // tm
    return pl.pallas_call(
        _router_kernel,
        grid=(t // tm,),
        in_specs=[
            pl.BlockSpec((tm, d), lambda i: (i, 0)),
            pl.BlockSpec((1, d), lambda i: (0, 0)),
            pl.BlockSpec((1, 1, d), lambda i: (i // per_b, 0, 0)),
            pl.BlockSpec((1, 1, d), lambda i: (i // per_b, 0, 0)),
            pl.BlockSpec((N_EXPERTS, d), lambda i: (0, 0)),
            pl.BlockSpec((N_EXPERTS, 1), lambda i: (0, 0)),
        ],
        out_specs=[
            pl.BlockSpec((tm, d), lambda i: (i, 0)),
            pl.BlockSpec((TOP_K, tm), lambda i: (0, i)),
            pl.BlockSpec((TOP_K, tm), lambda i: (0, i)),
            pl.BlockSpec((1, N_EXPERTS, 1), lambda i: (i, 0, 0)),
            pl.BlockSpec((1, N_EXPERTS, 1), lambda i: (i, 0, 0)),
            pl.BlockSpec((N_EXPERTS, 1), lambda i: (0, 0)),
        ],
        out_shape=[
            jax.ShapeDtypeStruct((t, d), BF16),
            jax.ShapeDtypeStruct((TOP_K, t), jnp.int32),
            jax.ShapeDtypeStruct((TOP_K, t), F32),
            jax.ShapeDtypeStruct((ntiles, N_EXPERTS, 1), jnp.int32),
            jax.ShapeDtypeStruct((ntiles, N_EXPERTS, 1), jnp.int32),
            jax.ShapeDtypeStruct((N_EXPERTS, 1), jnp.int32),
        ],
        scratch_shapes=[pltpu.VMEM((N_EXPERTS, 1), F32)],
        compiler_params=_cparams("arbitrary"),
        name="moe_router",
    )(x, g.reshape(1, d), shift, scale, r_w.T.astype(BF16), r_b.reshape(-1, 1))


def _run_copies(n, src_ref, src_row, dst_ref, dst_row, sem, max_rows, fn):
    for b in range(max_rows.bit_length()):
        size = 1 << b

        @pl.when((n & size) != 0)
        def _():
            lo = n & (size - 1)
            fn(pltpu.make_async_copy(
                src_ref.at[pl.ds(pl.multiple_of((src_row + lo) * ROW_TILES, ROW_TILES), size * ROW_TILES)],
                dst_ref.at[pl.ds(pl.multiple_of((dst_row + lo) * ROW_TILES, ROW_TILES), size * ROW_TILES)],
                sem), b)


def _start_piece(cp, b):
    cp.start()


def _wait_piece(cp, b):
    cp.wait()


def _tile_runs(tile, n_ref, carry_ref, gstart_ref, sorted_ref, grouped_ref, sem, tm, fn, to_grouped, live=None):
    def body(e, run_start):
        n = n_ref[tile * N_EXPERTS + e]
        if live is not None:
            n = jnp.where(live, n, 0)
        slot0 = gstart_ref[e] + carry_ref[tile * N_EXPERTS + e]
        if to_grouped:
            _run_copies(n, sorted_ref, run_start, grouped_ref, slot0, sem, tm, fn)
        else:
            _run_copies(n, grouped_ref, slot0, sorted_ref, run_start, sem, tm, fn)
        return run_start + n

    if live is None:
        lax.fori_loop(0, N_EXPERTS, body, 0)
    else:
        run_start = 0
        for e in range(N_EXPERTS):
            run_start = body(e, run_start)


def _block_copy(src_ref, dst_ref, dst_blk, sem):
    rows = MOE_BLOCK * ROW_TILES
    return pltpu.make_async_copy(src_ref, dst_ref.at[pl.ds(pl.multiple_of(dst_blk * rows, rows), rows)], sem)


def _dispatch_kernel(n_ref, carry_ref, gstart_ref, pad_lo_ref, pad_hi_ref, nused_ref, lp_ref, h_ref, xg_ref,
                     sorted0_ref, sorted1_ref, sorted2_ref, zero_ref, run_sem, sem):
    i = pl.program_id(0)
    last = pl.num_programs(0) - 1
    tm = h_ref.shape[0]
    rows = TOP_K * tm
    nblk =xg_ref.shape[0] // (MOE_BLOCK * ROW_TILES)

    @pl.when(i == 0)
    def _():
        zero_ref[...] = jnp.zeros_like(zero_ref)

        def fill(b, carry):
            _block_copy(zero_ref, xg_ref, b, sem).start()
            return carry

        def fill_wait(b, carry):
            _block_copy(zero_ref, xg_ref, b, sem).wait()
            return carry

        lax.fori_loop(nused_ref[0], nblk, fill, 0)
        lax.fori_loop(nused_ref[0], nblk, fill_wait, 0)

    @pl.when(i < N_EXPERTS)
    def _():
        n_pad = pad_hi_ref[i] - pad_lo_ref[i]
        for fn in (_start_piece, _wait_piece):
            _run_copies(n_pad, zero_ref, 0, xg_ref, pad_lo_ref[i], sem, MOE_BLOCK - 1, fn)

    bufs = (sorted0_ref, sorted1_ref, sorted2_ref)
    nbuf = len(bufs)

    def runs(tile, b, fn, live=None):
        _tile_runs(tile, n_ref, carry_ref, gstart_ref, bufs[b], xg_ref, run_sem.at[b], tm, fn, True, live)

    def wait_tile(b):
        pltpu.make_async_copy(bufs[b], xg_ref.at[pl.ds(0, rows * ROW_TILES)], run_sem.at[b]).wait()

    def step(cur):
        prev, oldest = (cur - 1) % nbuf, (cur + 1) % nbuf
        cur_ref = bufs[cur]

        @pl.when(i >= nbuf)
        def _():
            wait_tile(cur)

        row_id = lax.broadcasted_iota(jnp.int32, (rows, tm), 0)
        perm = row_id == lp_ref[0:1, :]
        for k in range(1, TOP_K):
            perm = perm | (row_id == lp_ref[k:k + 1, :])
        srt = jnp.dot(jnp.where(perm, 1.0, 0.0).astype(BF16), h_ref[...], preferred_element_type=F32)
        for s in range(ROW_TILES):
            cur_ref[pl.ds(s, rows, stride=ROW_TILES), :] = srt[:, s * LANES:(s + 1) * LANES]

        runs(jnp.maximum(i - 1, 0), prev, _start_piece, live=i >= 1)

        @pl.when(i == last)
        def _():
            @pl.when(i >= 2)
            def _():
                wait_tile(oldest)

            @pl.when(i >= 1)
            def _():
                wait_tile(prev)

            runs(i, cur, _start_piece)
            wait_tile(cur)

    for r in range(nbuf):
        @pl.when(i % nbuf == r)
        def _():
            step(r)


def _dispatch(tile_n, tile_carry, group_start, pad_lo, pad_hi, nused, lp_t, h, cap, tm):
    t, d = h.shape
    assert t // tm >= N_EXPERTS
    grid_spec = pltpu.PrefetchScalarGridSpec(
        num_scalar_prefetch=6,
        grid=(t // tm,),
        in_specs=[
            pl.BlockSpec((TOP_K, tm), lambda i, *_: (0, i)),
            pl.BlockSpec((tm, d), lambda i, *_: (i, 0)),
        ],
        out_specs=pl.BlockSpec(memory_space=pl.ANY),
        scratch_shapes=[
            pltpu.VMEM((TOP_K * tm * ROW_TILES, LANES), F32),
            pltpu.VMEM((TOP_K * tm * ROW_TILES, LANES), F32),
            pltpu.VMEM((TOP_K * tm * ROW_TILES, LANES), F32),
            pltpu.VMEM((MOE_BLOCK * ROW_TILES, LANES), F32),
            pltpu.SemaphoreType.DMA((3,)), pltpu.SemaphoreType.DMA,
        ],
    )
    return pl.pallas_call(
        _dispatch_kernel,
        grid_spec=grid_spec,
        out_shape=jax.ShapeDtypeStruct((cap * ROW_TILES, LANES), F32),
        compiler_params=pltpu.CompilerParams(dimension_semantics=("arbitrary",), has_side_effects=True,
                                             vmem_limit_bytes=VMEM_LIMIT),
        name="moe_dispatch",
    )(tile_n, tile_carry, group_start, pad_lo, pad_hi, nused, lp_t, h)


def _weight_copies(w1_hbm, w2_hbm, w1buf, w2buf, sems, layer, e, slot):
    return (pltpu.make_async_copy(w1_hbm.at[layer, e], w1buf.at[slot], sems.at[0, slot]),
            pltpu.make_async_copy(w2_hbm.at[layer, e], w2buf.at[slot], sems.at[1, slot]))


def _expert_kernel(blk_e_ref, nused_ref, next_e_ref, slot_ref, blk_rows_ref, x_ref, w1_hbm, w2_hbm, bg_ref, bl_ref, b2_ref,
                   sel_ref, o_ref, w1buf, w2buf, wg_s, wl_s, w2_s, sems, *, layer):
    blk = pl.program_id(0)
    active = blk < nused_ref[0]
    e = blk_e_ref[blk]
    slot = slot_ref[e]
    new_expert = (blk == 0) | (e != blk_e_ref[jnp.maximum(blk - 1, 0)])
    copies = functools.partial(_weight_copies, w1_hbm, w2_hbm, w1buf, w2buf, sems, layer)

    @pl.when(blk == 0)
    def _():
        for cp in copies(e, slot):
            cp.start()

    @pl.when(active & new_expert)
    def _():
        nxt = next_e_ref[e]

        @pl.when(nxt >= 0)
        def _():
            for cp in copies(nxt, 1 - slot):
                cp.start()

        for cp in copies(e, slot):
            cp.wait()
        half = SPLIT_W // 2
        for j in range(w1buf.shape[2] // SPLIT_W):
            chunk = w1buf[slot, :, j * SPLIT_W:(j + 1) * SPLIT_W].astype(BF16)
            split = jnp.dot(chunk, sel_ref[...], preferred_element_type=F32)
            wg_s[:, j * half:(j + 1) * half] = split[:, :half].astype(BF16)
            wl_s[:, j * half:(j + 1) * half] = split[:, half:].astype(BF16)
        w2_s[...] = w2buf[slot].astype(BF16)

    def mlp(rows):
        x = jnp.concatenate([x_ref[pl.ds(s, rows, stride=ROW_TILES), :]
                             for s in range(ROW_TILES)], axis=1).astype(BF16)
        glu = jnp.dot(x, wg_s[...], preferred_element_type=F32) + bg_ref[0]
        lin = jnp.dot(x, wl_s[...], preferred_element_type=F32) + bl_ref[0]
        glu = jnp.minimum(glu, SWIGLU_LIMIT)
        lin = jnp.clip(lin, -SWIGLU_LIMIT, SWIGLU_LIMIT)
        act = glu * jax.nn.sigmoid(SWIGLU_ALPHA * glu) * (lin + 1.0)
        y = jnp.dot(act.astype(BF16), w2_s[...], preferred_element_type=F32) + b2_ref[0]
        for s in range(ROW_TILES):
            o_ref[pl.ds(s, rows, stride=ROW_TILES), :] = y[:, s * LANES:(s + 1) * LANES]
        if rows < MOE_BLOCK:
            o_ref[pl.ds(rows * ROW_TILES, (MOE_BLOCK - rows) * ROW_TILES), :] = jnp.zeros(
                ((MOE_BLOCK - rows) * ROW_TILES, LANES), o_ref.dtype)

    valid_rows = jnp.where(active, blk_rows_ref[blk], 0)
    for rows in range(MOE_SUB, MOE_BLOCK + 1, MOE_SUB):
        @pl.when((valid_rows > rows - MOE_SUB) & (valid_rows <= rows))
        def _():
            mlp(rows)

    @pl.when(valid_rows <= 0)
    def _():
        o_ref[...] = jnp.zeros_like(o_ref)


def _experts(layer, blk_e, nused, next_e, e_slot, blk_rows, xg_rows, w1_all, w2_all, b_glu, b_lin, b2):
    nblk = blk_e.shape[0]
    d = D_MODEL
    ff = w2_all.shape[2]
    col = jnp.arange(SPLIT_W)
    sel = (jnp.arange(SPLIT_W)[None, :] == ((col % 2) * (SPLIT_W // 2) + col // 2)[:, None]).astype(BF16)

    def blk_map(b, blk_e_ref, nused_ref, *_):
        return (jnp.minimum(b, nused_ref[0] - 1), 0)

    def e_map(b, blk_e_ref, nused_ref, *_):
        return (blk_e_ref[jnp.minimum(b, nused_ref[0] - 1)], 0, 0)

    grid_spec = pltpu.PrefetchScalarGridSpec(
        num_scalar_prefetch=5,
        grid=(nblk,),
        in_specs=[
            pl.BlockSpec((MOE_BLOCK * ROW_TILES, LANES), blk_map),
            pl.BlockSpec(memory_space=pl.ANY),
            pl.BlockSpec(memory_space=pl.ANY),
            pl.BlockSpec((1, 1, ff), e_map),
            pl.BlockSpec((1, 1, ff), e_map),
            pl.BlockSpec((1, 1, d), e_map),
            pl.BlockSpec((SPLIT_W, SPLIT_W), lambda b, *_: (0, 0)),
        ],
        out_specs=pl.BlockSpec((MOE_BLOCK * ROW_TILES, LANES), lambda b, *_: (b, 0)),
        scratch_shapes=[
            pltpu.VMEM((2, d, 2 * ff), F32), pltpu.VMEM((2, ff, d), F32),
            pltpu.VMEM((d, ff), BF16), pltpu.VMEM((d, ff), BF16), pltpu.VMEM((ff, d), BF16),
            pltpu.SemaphoreType.DMA((2, 2)),
        ],
    )
    return pl.pallas_call(
        functools.partial(_expert_kernel, layer=layer),
        grid_spec=grid_spec,
        out_shape=jax.ShapeDtypeStruct(xg_rows.shape, F32),
        compiler_params=_cparams("arbitrary"),
        name="moe_experts",
    )(blk_e, nused, next_e, e_slot, blk_rows, xg_rows, w1_all, w2_all, b_glu, b_lin, b2, sel)


def _combine_kernel(n_ref, carry_ref, gstart_ref, yg_ref, lp_ref, gate_ref, x_ref, g2_ref, o_ref, sorted_ref, sem):
    i = pl.program_id(0)
    ntiles = pl.num_programs(0)
    tm = x_ref.shape[0]
    rows = TOP_K * tm
    slot = i % 2

    def runs(tile, buf, fn, live=None):
        _tile_runs(tile, n_ref, carry_ref, gstart_ref, sorted_ref.at[buf], yg_ref, sem.at[buf], tm, fn, False, live)

    @pl.when(i == 0)
    def _():
        runs(0, 0, _start_piece)

    runs(jnp.minimum(i + 1, ntiles - 1), 1 - slot, _start_piece, live=i + 1 < ntiles)

    eye = (lax.broadcasted_iota(jnp.int32, (tm, tm), 0) ==
           lax.broadcasted_iota(jnp.int32, (tm, tm), 1)).astype(F32)
    rows_t = jnp.concatenate([gate_ref[...], lp_ref[...].astype(F32)], axis=0)
    cols = lax.dot_general(eye, rows_t, (((1,), (1,)), ((), ())), preferred_element_type=F32, precision=HI)
    col_id = lax.broadcasted_iota(jnp.int32, (tm, rows), 1)
    weights = jnp.zeros((tm, rows), F32)
    for k in range(TOP_K):
        lp_col = (cols[:, TOP_K + k:TOP_K + k + 1] + 0.5).astype(jnp.int32)
        weights = weights + jnp.where(col_id == lp_col, cols[:, k:k + 1], 0.0)

    pltpu.make_async_copy(yg_ref.at[pl.ds(0, rows * ROW_TILES)], sorted_ref.at[slot], sem.at[slot]).wait()
    y = jnp.concatenate(
        [sorted_ref[slot, pl.ds(s, rows, stride=ROW_TILES), :] for s in range(ROW_TILES)], axis=1).astype(BF16)
    acc = jnp.dot(weights.astype(BF16), y, preferred_element_type=F32)
    o_ref[...] = x_ref[...] + g2_ref[0] * acc


def _combine(tile_n, tile_carry, group_start, yg_rows, lp_t, gate_t, x, gate2, seq, tm):
    t, d = x.shape
    per_b = seq // tm
    grid_spec = pltpu.PrefetchScalarGridSpec(
        num_scalar_prefetch=3,
        grid=(t // tm,),
        in_specs=[
            pl.BlockSpec(memory_space=pl.ANY),
            pl.BlockSpec((TOP_K, tm), lambda i, *_: (0, i)),
            pl.BlockSpec((TOP_K, tm), lambda i, *_: (0, i)),
            pl.BlockSpec((tm, d), lambda i, *_: (i, 0)),
            pl.BlockSpec((1, 1, d), lambda i, *_: (i // per_b, 0, 0)),
        ],
        out_specs=pl.BlockSpec((tm, d), lambda i, *_: (i, 0)),
        scratch_shapes=[pltpu.VMEM((2, TOP_K * tm * ROW_TILES, LANES), F32), pltpu.SemaphoreType.DMA((2,))],
    )
    return pl.pallas_call(
        _combine_kernel,
        grid_spec=grid_spec,
        out_shape=jax.ShapeDtypeStruct((t, d), F32),
        compiler_params=_cparams("arbitrary"),
        name="moe_combine",
    )(tile_n, tile_carry, group_start, yg_rows, lp_t, gate_t, x, gate2)


def _moe(layer, x, g, shift, scale, gate2, r_w, r_b, w1_all, b1_glu, b1_lin, w2_all, b2, seq):
    t = x.shape[0]
    h, lp_t, gate_t, tile_n, tile_carry, counts = _router(x, g, shift, scale, r_w, r_b, seq, MOE_TILE)
    tile_n = tile_n.reshape(-1)
    tile_carry = tile_carry.reshape(-1)
    counts = counts.reshape(-1)
    padded = ((counts + MOE_BLOCK - 1) // MOE_BLOCK) * MOE_BLOCK
    group_end = jnp.cumsum(padded)
    group_start = (group_end - padded).astype(jnp.int32)
    cap = t * TOP_K + N_EXPERTS * MOE_BLOCK
    nblk = cap // MOE_BLOCK
    blk_start = jnp.arange(nblk, dtype=jnp.int32) * MOE_BLOCK
    e_ids = jnp.arange(N_EXPERTS, dtype=jnp.int32)
    past = (blk_start[None, :] >= group_end[:, None]).astype(jnp.int32)
    blk_e = jnp.minimum(jnp.sum(past, axis=0), N_EXPERTS - 1).astype(jnp.int32)
    nused = (group_end[-1:] // MOE_BLOCK).astype(jnp.int32)

    pad_lo = (group_start + counts).astype(jnp.int32)
    pad_hi = group_end.astype(jnp.int32)
    blk_pad_lo = jnp.sum(jnp.where(blk_e[None, :] == e_ids[:, None], pad_lo[:, None], 0), axis=0)
    blk_rows = jnp.clip(blk_pad_lo - blk_start, 0, MOE_BLOCK).astype(jnp.int32)
    xg_rows = _dispatch(tile_n, tile_carry, group_start, pad_lo, pad_hi, nused, lp_t, h, cap, MOE_TILE)
    nonempty = counts > 0
    later = jnp.where(nonempty[None, :] & (e_ids[None, :] > e_ids[:, None]), e_ids[None, :], N_EXPERTS)
    next_e = jnp.min(later, axis=1)
    next_e = jnp.where(next_e == N_EXPERTS, -1, next_e).astype(jnp.int32)
    e_slot = ((jnp.cumsum(nonempty.astype(jnp.int32)) - 1) % 2).astype(jnp.int32)
    yg_rows = _experts(layer, blk_e, nused, next_e, e_slot, blk_rows, xg_rows, w1_all, w2_all,
                       b1_glu[:, None, :], b1_lin[:, None, :], b2[:, None, :])
    return _combine(tile_n, tile_carry, group_start, yg_rows, lp_t, gate_t, x, gate2, seq, MOE_TILE)


def kernel(x, c, ada_w, ada_b, norm1_g, norm2_g, m_in_w, m_conv_w, m_conv_b, m_dt_bias, m_A_log, m_D, m_norm_g, m_out_w, a_qkv_w, a_q_norm_g, a_k_norm_g, a_sinks, a_out_w, rel_bias, r_w, r_b, e_w1, e_b1, e_w2, e_b2):
    batch, seq, d = x.shape
    depth = ada_w.shape[0]
    t = batch * seq
    xf = x.reshape(t, d)

    mod = _adaln(c, ada_w, ada_b)[:, :batch]

    b1_split = jnp.moveaxis(e_b1.reshape(depth, N_EXPERTS, -1, 2), -1, 1)

    for i in range(depth):
        parts =[mod[i, :, p * d:(p + 1) * d].reshape(batch, 1, d) for p in range(6)]
        sh1, sc1, g1, sh2, sc2, g2 = parts
        j = i // 2
        if i % 2 == 0:
            w_zxbc = m_in_w[j, :, :SSM_ZXBC].astype(BF16)
            w_dt = jnp.pad(m_in_w[j, :, SSM_ZXBC:], ((0, 0), (0, LANES - SSM_HEADS)))
            zxbc, dt_raw = _norm_matmul(xf, norm1_g[i], sh1, sc1, (w_zxbc, w_dt), seq)
            y = _ssd_mixer(zxbc, dt_raw, m_conv_w[j], m_conv_b[j], m_dt_bias[j], m_A_log[j], m_D[j],
                           m_norm_g[j], batch, seq)
            xf = _matmul_residual(y, m_out_w[j].astype(BF16), xf, g1, seq)
        else:
            qkv, = _norm_matmul(xf, norm1_g[i], sh1, sc1, (a_qkv_w[j].astype(BF16),), seq)
            y = _swa_mixer(qkv, a_q_norm_g[j], a_k_norm_g[j], a_sinks[j], rel_bias, batch, seq)
            xf = _matmul_residual(y, a_out_w[j].astype(BF16), xf, g1, seq)
        xf = _moe(i, xf, norm2_g[i], sh2, sc2, g2, r_w[i], r_b[i], e_w1, b1_split[i, 0], b1_split[i, 1],
                  e_w2, e_b2[i], seq)
    return xf.reshape(batch, seq, d)
```

```python
import functools
import math

import jax
import jax.numpy as jnp
from jax import lax
from jax.experimental import pallas as pl
from jax.experimental.pallas import tpu as pltpu

D_MODEL = 1024
EPS = 1e-6
LANES = 128
SUBLANES = 8
ROW_TILES = D_MODEL // LANES

SSM_D_INNER = 2048
SSM_HEAD_DIM = 64
SSM_HEADS = 32
SSM_GROUPS = 4
SSM_STATE = 128
SSM_CONV = 4
SSM_CHUNK = 128
SSM_STEP_CHUNKS = 1
SSM_GN = SSM_GROUPS * SSM_STATE
SSM_ZXBC = 2 * SSM_D_INNER + 2 * SSM_GN
SSM_GROUP_W = SSM_D_INNER // SSM_GROUPS

ATTN_HEAD_DIM = 64
ATTN_Q_HEADS = 16
ATTN_KV_HEADS = 4
ATTN_Q_PER_KV = 4
WINDOW = 128
SWA_STEP_BLOCKS = 4
REL_BUCKETS = 32
REL_MAX_DIST = 128

N_EXPERTS = 32
TOP_K = 4
SWIGLU_ALPHA = 1.702
SWIGLU_LIMIT = 7.0
MOE_BLOCK = 512
MOE_SUB = 128
MOE_TILE = 256
SPLIT_W = 256

VMEM_LIMIT = 56 * 1024 * 1024
HI = lax.Precision.HIGHEST
F32 = jnp.float32
BF16 = jnp.bfloat16


def _cparams(*sem):
    return pltpu.CompilerParams(dimension_semantics=sem, vmem_limit_bytes=VMEM_LIMIT)


def _norm_modulate(x, g, shift, scale):
    ms = jnp.mean(x * x, axis=-1, keepdims=True)
    return x * lax.rsqrt(ms + EPS) * g * (1.0 + scale) + shift


def _adaln_kernel(ct_ref, w_ref, b_ref, o_ref, *, batch):
    c_t = ct_ref[...]
    act_t = c_t * jax.nn.sigmoid(c_t)
    w = w_ref[0]
    rows = [jnp.sum(act_t[:, b:b + 1] * w, axis=0, keepdims=True) + b_ref[0] for b in range(batch)]
    rows.append(jnp.zeros((SUBLANES - batch, w.shape[1]), F32))
    o_ref[0] = jnp.concatenate(rows, axis=0)


def _adaln(c, ada_w, ada_b):
    depth, d, n = ada_w.shape
    batch = c.shape[0]
    tn = 1536
    c_t = jnp.zeros((d, SUBLANES), F32).at[:, :batch].set(c.T)
    return pl.pallas_call(
        functools.partial(_adaln_kernel, batch=batch),
        grid=(depth, n // tn),
        in_specs=[
            pl.BlockSpec((d, SUBLANES), lambda i, j: (0, 0)),
            pl.BlockSpec((1, d, tn), lambda i, j: (i, 0, j)),
            pl.BlockSpec((1, 1, tn), lambda i, j: (i, 0, j)),
        ],
        out_specs=pl.BlockSpec((1, SUBLANES, tn), lambda i, j: (i, 0, j)),
        out_shape=jax.ShapeDtypeStruct((depth, SUBLANES, n), F32),
        compiler_params=_cparams("arbitrary", "arbitrary"),
        name="adaln",
    )(c_t, ada_w, ada_b.reshape(depth, 1, n))


def _norm_matmul_kernel(x_ref, g_ref, sh_ref, sc_ref, *refs):
    w_refs, o_refs = refs[:len(refs) // 2], refs[len(refs) // 2:]
    h = _norm_modulate(x_ref[...], g_ref[...], sh_ref[0], sc_ref[0]).astype(BF16)
    for w_ref, o_ref in zip(w_refs, o_refs):
        o_ref[...] = jnp.dot(h, w_ref[...].astype(BF16), preferred_element_type=F32)


def _norm_matmul(x, g, shift, scale, weights_bf16, seq, tm=512):
    t, d = x.shape
    per_b = seq // tm
    return pl.pallas_call(
        _norm_matmul_kernel,
        grid=(t // tm,),
        in_specs=[
            pl.BlockSpec((tm, d), lambda i: (i, 0)),
            pl.BlockSpec((1, d), lambda i: (0, 0)),
            pl.BlockSpec((1, 1, d), lambda i: (i // per_b, 0, 0)),
            pl.BlockSpec((1, 1, d), lambda i: (i // per_b, 0, 0)),
        ] + [pl.BlockSpec(w.shape, lambda i: (0, 0), pipeline_mode=pl.Buffered(1))
             for w in weights_bf16],
        out_specs=[pl.BlockSpec((tm, w.shape[1]), lambda i: (i, 0)) for w in weights_bf16],
        out_shape=[jax.ShapeDtypeStruct((t, w.shape[1]), F32) for w in weights_bf16],
        compiler_params=_cparams("arbitrary"),
        name="norm_matmul",
    )(x, g.reshape(1, d), shift, scale, *weights_bf16)


def _matmul_residual_kernel(y_ref, w_ref, x_ref, gate_ref, o_ref):
    acc = jnp.dot(y_ref[...], w_ref[...], preferred_element_type=F32)
    o_ref[...] = x_ref[...] + gate_ref[0] * acc


def _matmul_residual(y_bf16, w_bf16, x, gate, seq, tm=1024):
    t, k = y_bf16.shape
    d = x.shape[1]
    per_b = seq // tm
    return pl.pallas_call(
        _matmul_residual_kernel,
        grid=(t // tm,),
        in_specs=[
            pl.BlockSpec((tm, k), lambda i: (i, 0)),
            pl.BlockSpec((k, d), lambda i: (0, 0)),
            pl.BlockSpec((tm, d), lambda i: (i, 0)),
            pl.BlockSpec((1, 1, d), lambda i: (i // per_b, 0, 0)),
        ],
        out_specs=pl.BlockSpec((tm, d), lambda i: (i, 0)),
        out_shape=jax.ShapeDtypeStruct((t, d), F32),
        compiler_params=_cparams("arbitrary"),
        name="matmul_residual",
    )(y_bf16, w_bf16, x, gate)


def _causal_conv_silu(cur, prev_tail, w, b):
    rows = lax.broadcasted_iota(jnp.int32, (SUBLANES, cur.shape[1]), 0)
    acc = b + w[SSM_CONV - 1:SSM_CONV] * cur
    for d in range(1, SSM_CONV):
        rolled = pltpu.roll(cur, d, axis=0)
        top = jnp.where(rows < d, pltpu.roll(prev_tail, d, axis=0), rolled[0:SUBLANES])
        shifted = jnp.concatenate([top, rolled[SUBLANES:]], axis=0)
        acc = acc + w[SSM_CONV - 1 - d:SSM_CONV - d] * shifted
    return acc * jax.nn.sigmoid(acc)


def _split3(x):
    hi = x.astype(BF16)
    rem = x - hi.astype(F32)
    mid = rem.astype(BF16)
    lo = (rem - mid.astype(F32)).astype(BF16)
    return jnp.concatenate([hi, mid, lo], axis=1)


def _ssd_kernel(z_ref, xs_ref, bc_ref, dt_ref, *refs):
    o_ref, tailx_ref, tailb_ref, state_ref = refs[-4:]

    @pl.when(pl.program_id(1) == 0)
    def _():
        tailx_ref[...] = jnp.zeros_like(tailx_ref)
        tailb_ref[...] = jnp.zeros_like(tailb_ref)
        state_ref[...] = jnp.zeros_like(state_ref)

    for u in range(SSM_STEP_CHUNKS):
        rows = pl.ds(u * SSM_CHUNK, SSM_CHUNK)
        _ssd_chunk(z_ref.at[rows], xs_ref.at[rows], bc_ref.at[rows], dt_ref.at[rows], *refs[:-4],
                   o_ref.at[rows], tailx_ref, tailb_ref, state_ref)


def _ssd_chunk(z_ref, xs_ref, bc_ref, dt_ref, cwx_ref, cbx_ref, cwb_ref, cbb_ref, dtb_ref, alog_ref,
               dskip_ref, ng_ref, hexp_ref, lexp_ref, o_ref, tailx_ref, tailb_ref, state_ref):
    xs_raw = xs_ref[...]
    bc_raw = bc_ref[...]
    xs = _causal_conv_silu(xs_raw, tailx_ref[...], cwx_ref[...], cbx_ref[...])
    bc = _causal_conv_silu(bc_raw, tailb_ref[...], cwb_ref[...], cbb_ref[...])
    tailx_ref[...] = xs_raw[SSM_CHUNK - SUBLANES:]
    tailb_ref[...] = bc_raw[SSM_CHUNK - SUBLANES:]

    dt_in = dt_ref[...][:, :SSM_HEADS] + dtb_ref[...]
    dt = jnp.maximum(dt_in, 0.0) + jnp.log1p(jnp.exp(-jnp.abs(dt_in)))
    a_neg = -jnp.exp(alog_ref[...])
    d_a = dt * a_neg
    li = lax.broadcasted_iota(jnp.int32, (SSM_CHUNK, SSM_CHUNK), 0)
    si = lax.broadcasted_iota(jnp.int32, (SSM_CHUNK, SSM_CHUNK), 1)
    causal = li >= si
    tri = causal.astype(F32)
    a_cs = jnp.dot(tri, d_a, preferred_element_type=F32, precision=HI)
    a_cs_t = lax.dot_general(d_a, tri, (((0,), (1,)), ((), ())),
                             preferred_element_type=F32, precision=HI)
    a_last = a_cs[SSM_CHUNK - 1:SSM_CHUNK]
    e_out = jnp.exp(a_cs)
    e_state = jnp.exp(a_last - a_cs) * dt
    small = jnp.concatenate([dt, e_out, e_state], axis=0)
    wide = jnp.dot(_split3(small), hexp_ref[...], preferred_element_type=F32)
    dt_w = wide[0:SSM_CHUNK]
    e_out_w = wide[SSM_CHUNK:2 * SSM_CHUNK]
    e_state_w = wide[2 * SSM_CHUNK:]
    a_col = jnp.dot(_split3(a_cs), lexp_ref[...], preferred_element_type=F32)

    x_dt = (xs * dt_w).astype(BF16)
    x_state = (xs * e_state_w).astype(BF16)
    chunk_decay_w = e_out_w[SSM_CHUNK - 1:SSM_CHUNK]

    heads_per_group = SSM_HEADS // SSM_GROUPS
    y_parts = []
    for g in range(SSM_GROUPS):
        b_g = bc[:, g * SSM_STATE:(g + 1) * SSM_STATE].astype(BF16)
        c_g = bc[:, SSM_GN + g * SSM_STATE:SSM_GN + (g + 1) * SSM_STATE].astype(BF16)
        cb = lax.dot_general(c_g, b_g, (((1,), (1,)), ((), ())), preferred_element_type=F32)
        gsl = slice(g * SSM_GROUP_W, (g + 1) * SSM_GROUP_W)
        h_prev = state_ref[g]
        y_off = jnp.dot(c_g, h_prev.astype(BF16), preferred_element_type=F32) * e_out_w[:, gsl]
        diag = []
        for r in range(heads_per_group):
            h = g * heads_per_group + r
            seg = a_col[:, h * SSM_CHUNK:(h + 1) * SSM_CHUNK] - a_cs_t[h:h + 1, :]
            decay = jnp.exp(jnp.where(causal, seg, -jnp.inf))
            m = (cb * decay).astype(BF16)
            diag.append(jnp.dot(m, x_dt[:, h * SSM_HEAD_DIM:(h + 1) * SSM_HEAD_DIM],
                                preferred_element_type=F32))
        y_parts.append(jnp.concatenate(diag, axis=1) + y_off)
        upd = lax.dot_general(b_g, x_state[:, gsl], (((0,), (0,)), ((), ())), preferred_element_type=F32)
        state_ref[g] = h_prev * chunk_decay_w[:, gsl] + upd

    y = jnp.concatenate(y_parts, axis=1) + dskip_ref[...] * xs
    z = z_ref[...]
    y = y * (z * jax.nn.sigmoid(z))
    normed = []
    for g in range(SSM_GROUPS):
        y_g = y[:, g * SSM_GROUP_W:(g + 1) * SSM_GROUP_W]
        normed.append(y_g * lax.rsqrt(jnp.mean(y_g * y_g, axis=-1, keepdims=True) + EPS))
    o_ref[...] = (jnp.concatenate(normed, axis=1) * ng_ref[...]).astype(o_ref.dtype)


def _ssd_mixer(zxbc, dt_raw, conv_w, conv_b, dt_bias, a_log, d_skip, norm_g, batch, seq):
    t = zxbc.shape[0]
    step_rows = SSM_CHUNK * SSM_STEP_CHUNKS
    nc = seq // step_rows
    head_expand = jnp.tile(jnp.repeat(jnp.eye(SSM_HEADS, dtype=BF16), SSM_HEAD_DIM, axis=1), (3, 1))
    lane_expand = jnp.tile(jnp.repeat(jnp.eye(SSM_HEADS, dtype=BF16), SSM_CHUNK, axis=1), (3, 1))
    row = lambda b, c: (b * nc + c, 0)
    const2 = lambda b, c: (0, 0)
    bc_w = 2 * SSM_GN
    return pl.pallas_call(
        _ssd_kernel,
        grid=(batch, nc),
        in_specs=[
            pl.BlockSpec((step_rows, SSM_D_INNER), row),
            pl.BlockSpec((step_rows, SSM_D_INNER), lambda b, c: (b * nc + c, 1)),
            pl.BlockSpec((step_rows, bc_w), lambda b, c: (b * nc + c, 2 * SSM_D_INNER // bc_w)),
            pl.BlockSpec((step_rows, LANES), row),
            pl.BlockSpec((SSM_CONV, SSM_D_INNER), const2),
            pl.BlockSpec((1, SSM_D_INNER), const2),
            pl.BlockSpec((SSM_CONV, bc_w), const2),
            pl.BlockSpec((1, bc_w), const2),
            pl.BlockSpec((1, SSM_HEADS), const2),
            pl.BlockSpec((1, SSM_HEADS), const2),
            pl.BlockSpec((1, SSM_D_INNER), const2),
            pl.BlockSpec((1, SSM_D_INNER), const2),
            pl.BlockSpec((3 * SSM_HEADS, SSM_D_INNER), const2),
            pl.BlockSpec((3 * SSM_HEADS, SSM_HEADS * SSM_CHUNK), const2),
        ],
        out_specs=pl.BlockSpec((step_rows, SSM_D_INNER), row),
        out_shape=jax.ShapeDtypeStruct((t, SSM_D_INNER), BF16),
        scratch_shapes=[
            pltpu.VMEM((SUBLANES, SSM_D_INNER), F32),
            pltpu.VMEM((SUBLANES, bc_w), F32),
            pltpu.VMEM((SSM_GROUPS, SSM_STATE, SSM_GROUP_W), F32),
        ],
        compiler_params=_cparams("arbitrary", "arbitrary"),
        name="ssd_mixer",
    )(zxbc, zxbc, zxbc, dt_raw,
      conv_w[:, :SSM_D_INNER], conv_b[:SSM_D_INNER].reshape(1, -1),
      conv_w[:, SSM_D_INNER:], conv_b[SSM_D_INNER:].reshape(1, -1),
      dt_bias.reshape(1, -1), a_log.reshape(1, -1),
      jnp.repeat(d_skip, SSM_HEAD_DIM).reshape(1, -1), norm_g.reshape(1, -1),
      head_expand, lane_expand)


def _head_rms(x, g):
    return x * lax.rsqrt(jnp.mean(x * x, axis=-1, keepdims=True) + EPS) * g


def _swa_kernel(q_ref, kvc_ref, kvp_ref, bucket_ref, qg_ref, kg_ref, rel_ref, sink_ref, o_ref,
                bias_ref, sinkrow_ref):
    b = pl.program_id(0)
    i = pl.program_id(1)

    @pl.when((b == 0) & (i == 0))
    def _():
        bucket = bucket_ref[...]
        kj = lax.broadcasted_iota(jnp.int32, (2 * WINDOW, WINDOW), 0)
        qi = lax.broadcasted_iota(jnp.int32, (2 * WINDOW, WINDOW), 1)
        dist = qi + WINDOW - kj
        band = (dist >= 0) & (dist < WINDOW)
        for h in range(ATTN_Q_HEADS):
            g, r = divmod(h, ATTN_Q_PER_KV)
            acc = jnp.zeros(bucket.shape, F32)
            for k in range(REL_BUCKETS):
                acc = jnp.where(bucket == k, rel_ref[k, h], acc)
            cols = slice(r * WINDOW, (r + 1) * WINDOW)
            bias_ref[0, g, :, cols] = jnp.where(band, acc, -jnp.inf)
            bias_ref[1, g, :, cols] = jnp.where(band & (kj >= WINDOW), acc, -jnp.inf)
            sinkrow_ref[g, :, cols] = jnp.full((1, WINDOW), sink_ref[h], F32)

    for u in range(SWA_STEP_BLOCKS):
        rows = slice(u * WINDOW, (u + 1) * WINDOW)
        kv_p = kvp_ref[...] if u == 0 else kvc_ref[(u - 1) * WINDOW:u * WINDOW, :]
        first = (i == 0).astype(jnp.int32) if u == 0 else 0
        o_ref[rows, :] = _swa_block(q_ref[rows, :], kvc_ref[rows, :], kv_p, first,
                                    qg_ref, kg_ref, bias_ref, sinkrow_ref).astype(o_ref.dtype)


def _swa_block(q_rows, kv_c, kv_p, first, qg_ref, kg_ref, bias_ref, sinkrow_ref):
    q_t = q_rows.T
    kv_w = ATTN_KV_HEADS * ATTN_HEAD_DIM
    q_gain = qg_ref[...]
    outs = []
    for g in range(ATTN_KV_HEADS):
        ksl = slice(g * ATTN_HEAD_DIM, (g + 1) * ATTN_HEAD_DIM)
        vsl = slice(kv_w + g * ATTN_HEAD_DIM, kv_w + (g + 1) * ATTN_HEAD_DIM)
        k = jnp.concatenate([kv_p[:, ksl], kv_c[:, ksl]], axis=0)
        v = jnp.concatenate([kv_p[:, vsl], kv_c[:, vsl]], axis=0).astype(BF16)
        k = _head_rms(k, kg_ref[...]).astype(BF16)
        q_heads = []
        for r in range(ATTN_Q_PER_KV):
            h = g * ATTN_Q_PER_KV + r
            q_h = q_t[h * ATTN_HEAD_DIM:(h + 1) * ATTN_HEAD_DIM]
            inv = lax.rsqrt(jnp.mean(q_h * q_h, axis=0, keepdims=True) + EPS)
            q_heads.append((q_h * inv * q_gain).astype(BF16))
        q = jnp.concatenate(q_heads, axis=1)
        s = jnp.dot(k, q, preferred_element_type=F32) + bias_ref[first, g]
        sink = sinkrow_ref[g]
        m = jnp.maximum(jnp.max(s, axis=0, keepdims=True), sink)
        p = jnp.exp(s - m)
        denom = jnp.sum(p, axis=0, keepdims=True) + jnp.exp(sink - m)
        pv = lax.dot_general(v, p.astype(BF16), (((0,), (0,)), ((), ())), preferred_element_type=F32)
        pv = pv * (1.0 / denom)
        outs.extend(pv[:, r * WINDOW:(r + 1) * WINDOW] for r in range(ATTN_Q_PER_KV))
    return jnp.concatenate(outs, axis=0).T


def _t5_causal_bucket(dist):
    max_exact = REL_BUCKETS // 2
    d = jnp.maximum(dist, 1).astype(F32)
    large = max_exact + (jnp.log(d / max_exact) / math.log(REL_MAX_DIST / max_exact)
                         * (REL_BUCKETS - max_exact)).astype(jnp.int32)
    large = jnp.minimum(large, REL_BUCKETS - 1)
    return jnp.where(dist < max_exact, dist, large)


def _swa_mixer(qkv, q_norm_g, k_norm_g, sinks, rel_bias, batch, seq):
    t = qkv.shape[0]
    nb = seq // WINDOW
    step_rows = WINDOW * SWA_STEP_BLOCKS
    ns = seq // step_rows
    q_w = ATTN_Q_HEADS * ATTN_HEAD_DIM
    kv_w2 = 2 * ATTN_KV_HEADS * ATTN_HEAD_DIM
    kj = jnp.arange(2 * WINDOW)[:, None]
    qi = jnp.arange(WINDOW)[None, :]
    bucket = _t5_causal_bucket(jnp.maximum(qi + WINDOW - kj, 0)).astype(jnp.int32)
    q_gain = jnp.broadcast_to((q_norm_g * (ATTN_HEAD_DIM ** -0.5))[:, None], (ATTN_HEAD_DIM, WINDOW))
    const2 = lambda b, i: (0, 0)
    smem = pl.BlockSpec(memory_space=pltpu.SMEM)
    return pl.pallas_call(
        _swa_kernel,
        grid=(batch, ns),
        in_specs=[
            pl.BlockSpec((step_rows, q_w), lambda b, i: (b * ns + i, 0)),
            pl.BlockSpec((step_rows, kv_w2), lambda b, i: (b * ns + i, q_w // kv_w2)),
            pl.BlockSpec((WINDOW, kv_w2),
                         lambda b, i: (b * nb + jnp.maximum(i * SWA_STEP_BLOCKS - 1, 0), q_w // kv_w2)),
            pl.BlockSpec((2 * WINDOW, WINDOW), const2),
            pl.BlockSpec((ATTN_HEAD_DIM, WINDOW), const2),
            pl.BlockSpec((1, ATTN_HEAD_DIM), const2),
            smem,
            smem,
        ],
        out_specs=pl.BlockSpec((step_rows, q_w), lambda b, i: (b * ns + i, 0)),
        out_shape=jax.ShapeDtypeStruct((t, q_w), BF16),
        scratch_shapes=[pltpu.VMEM((2, ATTN_KV_HEADS, 2 * WINDOW, ATTN_Q_PER_KV * WINDOW), F32),
                        pltpu.VMEM((ATTN_KV_HEADS, 1, ATTN_Q_PER_KV * WINDOW), F32)],
        compiler_params=_cparams("arbitrary", "arbitrary"),
        name="swa_mixer",
    )(qkv, qkv, qkv, bucket, q_gain, k_norm_g.reshape(1, -1), rel_bias, sinks)


def _router_kernel(x_ref, g_ref, sh_ref, sc_ref, rwt_ref, rb_ref,
                   h_ref, lp_ref, gate_ref, tile_n_ref, tile_carry_ref, cnt_ref, carry_ref):
    i = pl.program_id(0)
    tm = x_ref.shape[0]

    @pl.when(i == 0)
    def _():
        carry_ref[...] = jnp.zeros_like(carry_ref)

    h = _norm_modulate(x_ref[...], g_ref[...], sh_ref[0], sc_ref[0]).astype(BF16)
    h_ref[...] = h

    logits = lax.dot_general(rwt_ref[...], h, (((1,), (1,)), ((), ())),
                             preferred_element_type=F32) + rb_ref[...]
    e_iota = lax.broadcasted_iota(jnp.int32, logits.shape, 0)
    work = logits
    sels, vals = [], []
    for k in range(TOP_K):
        m = jnp.max(work, axis=0, keepdims=True)
        idx = jnp.min(jnp.where(work == m, e_iota, N_EXPERTS), axis=0, keepdims=True)
        sel = e_iota == idx
        work = jnp.where(sel, -jnp.inf, work)
        sels.append(sel)
        vals.append(m)
    exps = [jnp.exp(v - vals[0]) for v in vals]
    denom = exps[0] + exps[1] + exps[2] + exps[3]
    for k in range(TOP_K):
        gate_ref[k:k + 1, :] = exps[k] / denom

    chosen = sels[0] | sels[1] | sels[2] | sels[3]
    t_row = lax.broadcasted_iota(jnp.int32, (tm, tm), 0)
    t_col = lax.broadcasted_iota(jnp.int32, (tm, tm), 1)
    before = (t_row < t_col).astype(BF16)
    prior = jnp.dot(chosen.astype(BF16), before, preferred_element_type=F32)
    n = jnp.sum(chosen.astype(F32), axis=1, keepdims=True)
    e_row = lax.broadcasted_iota(jnp.int32, (N_EXPERTS, N_EXPERTS), 0)
    e_col = lax.broadcasted_iota(jnp.int32, (N_EXPERTS, N_EXPERTS), 1)
    run_start = jnp.dot((e_col < e_row).astype(BF16), jnp.broadcast_to(n, (N_EXPERTS, LANES)).astype(BF16),
                        preferred_element_type=F32)[:, :1]
    local = run_start + prior
    for k in range(TOP_K):
        lp_ref[k:k + 1, :] = jnp.sum(jnp.where(sels[k], local, 0.0), axis=0, keepdims=True).astype(jnp.int32)
    tile_n_ref[0] = n.astype(jnp.int32)
    tile_carry_ref[0] = carry_ref[...].astype(jnp.int32)
    total = carry_ref[...] + n
    carry_ref[...] = total
    cnt_ref[...] = total.astype(jnp.int32)


def _router(x, g, shift, scale, r_w, r_b, seq, tm):
    t, d = x.shape
    per_b = seq // tm
    ntiles = t // tm
    return pl.pallas_call(
        _router_kernel,
        grid=(t // tm,),
        in_specs=[
            pl.BlockSpec((tm, d), lambda i: (i, 0)),
            pl.BlockSpec((1, d), lambda i: (0, 0)),
            pl.BlockSpec((1, 1, d), lambda i: (i // per_b, 0, 0)),
            pl.BlockSpec((1, 1, d), lambda i: (i // per_b, 0, 0)),
            pl.BlockSpec((N_EXPERTS, d), lambda i: (0, 0)),
            pl.BlockSpec((N_EXPERTS, 1), lambda i: (0, 0)),
        ],
        out_specs=[
            pl.BlockSpec((tm, d), lambda i: (i, 0)),
            pl.BlockSpec((TOP_K, tm), lambda i: (0, i)),
            pl.BlockSpec((TOP_K, tm), lambda i: (0, i)),
            pl.BlockSpec((1, N_EXPERTS, 1), lambda i: (i, 0, 0)),
            pl.BlockSpec((1, N_EXPERTS, 1), lambda i: (i, 0, 0)),
            pl.BlockSpec((N_EXPERTS, 1), lambda i: (0, 0)),
        ],
        out_shape=[
            jax.ShapeDtypeStruct((t, d), BF16),
            jax.ShapeDtypeStruct((TOP_K, t), jnp.int32),
            jax.ShapeDtypeStruct((TOP_K, t), F32),
            jax.ShapeDtypeStruct((ntiles, N_EXPERTS, 1), jnp.int32),
            jax.ShapeDtypeStruct((ntiles, N_EXPERTS, 1), jnp.int32),
            jax.ShapeDtypeStruct((N_EXPERTS, 1), jnp.int32),
        ],
        scratch_shapes=[pltpu.VMEM((N_EXPERTS, 1), F32)],
        compiler_params=_cparams("arbitrary"),
        name="moe_router",
    )(x, g.reshape(1, d), shift, scale, r_w.T.astype(BF16), r_b.reshape(-1, 1))


def _run_copies(n, src_ref, src_row, dst_ref, dst_row, sem, max_rows, fn):
    for b in range(max_rows.bit_length()):
        size = 1 << b

        @pl.when((n & size) != 0)
        def _():
            lo = n & (size - 1)
            fn(pltpu.make_async_copy(
                src_ref.at[pl.ds(pl.multiple_of((src_row + lo) * ROW_TILES, ROW_TILES), size * ROW_TILES)],
                dst_ref.at[pl.ds(pl.multiple_of((dst_row + lo) * ROW_TILES, ROW_TILES), size * ROW_TILES)],
                sem), b)


def _start_piece(cp, b):
    cp.start()


def _wait_piece(cp, b):
    cp.wait()


def _tile_runs(tile, n_ref, carry_ref, gstart_ref, sorted_ref, grouped_ref, sem, tm, fn, to_grouped, live=None):
    def body(e, run_start):
        n = n_ref[tile * N_EXPERTS + e]
        if live is not None:
            n = jnp.where(live, n, 0)
        slot0 = gstart_ref[e] + carry_ref[tile * N_EXPERTS + e]
        if to_grouped:
            _run_copies(n, sorted_ref, run_start, grouped_ref, slot0, sem, tm, fn)
        else:
            _run_copies(n, grouped_ref, slot0, sorted_ref, run_start, sem, tm, fn)
        return run_start + n

    if live is None:
        lax.fori_loop(0, N_EXPERTS, body, 0)
    else:
        run_start = 0
        for e in range(N_EXPERTS):
            run_start = body(e, run_start)


def _block_copy(src_ref, dst_ref, dst_blk, sem):
    rows = MOE_BLOCK * ROW_TILES
    return pltpu.make_async_copy(src_ref, dst_ref.at[pl.ds(pl.multiple_of(dst_blk * rows, rows), rows)], sem)


def _dispatch_kernel(n_ref, carry_ref, gstart_ref, pad_lo_ref, pad_hi_ref, nused_ref, lp_ref, h_ref, xg_ref,
                     sorted0_ref, sorted1_ref, sorted2_ref, zero_ref, run_sem, sem):
    i = pl.program_id(0)
    last = pl.num_programs(0) - 1
    tm = h_ref.shape[0]
    rows = TOP_K * tm
    nblk =xg_ref.shape[0] // (MOE_BLOCK * ROW_TILES)

    @pl.when(i == 0)
    def _():
        zero_ref[...] = jnp.zeros_like(zero_ref)

        def fill(b, carry):
            _block_copy(zero_ref, xg_ref, b, sem).start()
            return carry

        def fill_wait(b, carry):
            _block_copy(zero_ref, xg_ref, b, sem).wait()
            return carry

        lax.fori_loop(nused_ref[0], nblk, fill, 0)
        lax.fori_loop(nused_ref[0], nblk, fill_wait, 0)

    @pl.when(i < N_EXPERTS)
    def _():
        n_pad = pad_hi_ref[i] - pad_lo_ref[i]
        for fn in (_start_piece, _wait_piece):
            _run_copies(n_pad, zero_ref, 0, xg_ref, pad_lo_ref[i], sem, MOE_BLOCK - 1, fn)

    bufs = (sorted0_ref, sorted1_ref, sorted2_ref)
    nbuf = len(bufs)

    def runs(tile, b, fn, live=None):
        _tile_runs(tile, n_ref, carry_ref, gstart_ref, bufs[b], xg_ref, run_sem.at[b], tm, fn, True, live)

    def wait_tile(b):
        pltpu.make_async_copy(bufs[b], xg_ref.at[pl.ds(0, rows * ROW_TILES)], run_sem.at[b]).wait()

    def step(cur):
        prev, oldest = (cur - 1) % nbuf, (cur + 1) % nbuf
        cur_ref = bufs[cur]

        @pl.when(i >= nbuf)
        def _():
            wait_tile(cur)

        row_id = lax.broadcasted_iota(jnp.int32, (rows, tm), 0)
        perm = row_id == lp_ref[0:1, :]
        for k in range(1, TOP_K):
            perm = perm | (row_id == lp_ref[k:k + 1, :])
        srt = jnp.dot(jnp.where(perm, 1.0, 0.0).astype(BF16), h_ref[...], preferred_element_type=F32)
        for s in range(ROW_TILES):
            cur_ref[pl.ds(s, rows, stride=ROW_TILES), :] = srt[:, s * LANES:(s + 1) * LANES]

        runs(jnp.maximum(i - 1, 0), prev, _start_piece, live=i >= 1)

        @pl.when(i == last)
        def _():
            @pl.when(i >= 2)
            def _():
                wait_tile(oldest)

            @pl.when(i >= 1)
            def _():
                wait_tile(prev)

            runs(i, cur, _start_piece)
            wait_tile(cur)

    for r in range(nbuf):
        @pl.when(i % nbuf == r)
        def _():
            step(r)


def _dispatch(tile_n, tile_carry, group_start, pad_lo, pad_hi, nused, lp_t, h, cap, tm):
    t, d = h.shape
    assert t // tm >= N_EXPERTS
    grid_spec = pltpu.PrefetchScalarGridSpec(
        num_scalar_prefetch=6,
        grid=(t // tm,),
        in_specs=[
            pl.BlockSpec((TOP_K, tm), lambda i, *_: (0, i)),
            pl.BlockSpec((tm, d), lambda i, *_: (i, 0)),
        ],
        out_specs=pl.BlockSpec(memory_space=pl.ANY),
        scratch_shapes=[
            pltpu.VMEM((TOP_K * tm * ROW_TILES, LANES), F32),
            pltpu.VMEM((TOP_K * tm * ROW_TILES, LANES), F32),
            pltpu.VMEM((TOP_K * tm * ROW_TILES, LANES), F32),
            pltpu.VMEM((MOE_BLOCK * ROW_TILES, LANES), F32),
            pltpu.SemaphoreType.DMA((3,)), pltpu.SemaphoreType.DMA,
        ],
    )
    return pl.pallas_call(
        _dispatch_kernel,
        grid_spec=grid_spec,
        out_shape=jax.ShapeDtypeStruct((cap * ROW_TILES, LANES), F32),
        compiler_params=pltpu.CompilerParams(dimension_semantics=("arbitrary",), has_side_effects=True,
                                             vmem_limit_bytes=VMEM_LIMIT),
        name="moe_dispatch",
    )(tile_n, tile_carry, group_start, pad_lo, pad_hi, nused, lp_t, h)


def _weight_copies(w1_hbm, w2_hbm, w1buf, w2buf, sems, layer, e, slot):
    return (pltpu.make_async_copy(w1_hbm.at[layer, e], w1buf.at[slot], sems.at[0, slot]),
            pltpu.make_async_copy(w2_hbm.at[layer, e], w2buf.at[slot], sems.at[1, slot]))


def _expert_kernel(blk_e_ref, nused_ref, next_e_ref, slot_ref, blk_rows_ref, x_ref, w1_hbm, w2_hbm, bg_ref, bl_ref, b2_ref,
                   sel_ref, o_ref, w1buf, w2buf, wg_s, wl_s, w2_s, sems, *, layer):
    blk = pl.program_id(0)
    active = blk < nused_ref[0]
    e = blk_e_ref[blk]
    slot = slot_ref[e]
    new_expert = (blk == 0) | (e != blk_e_ref[jnp.maximum(blk - 1, 0)])
    copies = functools.partial(_weight_copies, w1_hbm, w2_hbm, w1buf, w2buf, sems, layer)

    @pl.when(blk == 0)
    def _():
        for cp in copies(e, slot):
            cp.start()

    @pl.when(active & new_expert)
    def _():
        nxt = next_e_ref[e]

        @pl.when(nxt >= 0)
        def _():
            for cp in copies(nxt, 1 - slot):
                cp.start()

        for cp in copies(e, slot):
            cp.wait()
        half = SPLIT_W // 2
        for j in range(w1buf.shape[2] // SPLIT_W):
            chunk = w1buf[slot, :, j * SPLIT_W:(j + 1) * SPLIT_W].astype(BF16)
            split = jnp.dot(chunk, sel_ref[...], preferred_element_type=F32)
            wg_s[:, j * half:(j + 1) * half] = split[:, :half].astype(BF16)
            wl_s[:, j * half:(j + 1) * half] = split[:, half:].astype(BF16)
        w2_s[...] = w2buf[slot].astype(BF16)

    def mlp(rows):
        x = jnp.concatenate([x_ref[pl.ds(s, rows, stride=ROW_TILES), :]
                             for s in range(ROW_TILES)], axis=1).astype(BF16)
        glu = jnp.dot(x, wg_s[...], preferred_element_type=F32) + bg_ref[0]
        lin = jnp.dot(x, wl_s[...], preferred_element_type=F32) + bl_ref[0]
        glu = jnp.minimum(glu, SWIGLU_LIMIT)
        lin = jnp.clip(lin, -SWIGLU_LIMIT, SWIGLU_LIMIT)
        act = glu * jax.nn.sigmoid(SWIGLU_ALPHA * glu) * (lin + 1.0)
        y = jnp.dot(act.astype(BF16), w2_s[...], preferred_element_type=F32) + b2_ref[0]
        for s in range(ROW_TILES):
            o_ref[pl.ds(s, rows, stride=ROW_TILES), :] = y[:, s * LANES:(s + 1) * LANES]
        if rows < MOE_BLOCK:
            o_ref[pl.ds(rows * ROW_TILES, (MOE_BLOCK - rows) * ROW_TILES), :] = jnp.zeros(
                ((MOE_BLOCK - rows) * ROW_TILES, LANES), o_ref.dtype)

    valid_rows = jnp.where(active, blk_rows_ref[blk], 0)
    for rows in range(MOE_SUB, MOE_BLOCK + 1, MOE_SUB):
        @pl.when((valid_rows > rows - MOE_SUB) & (valid_rows <= rows))
        def _():
            mlp(rows)

    @pl.when(valid_rows <= 0)
    def _():
        o_ref[...] = jnp.zeros_like(o_ref)


def _experts(layer, blk_e, nused, next_e, e_slot, blk_rows, xg_rows, w1_all, w2_all, b_glu, b_lin, b2):
    nblk = blk_e.shape[0]
    d = D_MODEL
    ff = w2_all.shape[2]
    col = jnp.arange(SPLIT_W)
    sel = (jnp.arange(SPLIT_W)[None, :] == ((col % 2) * (SPLIT_W // 2) + col // 2)[:, None]).astype(BF16)

    def blk_map(b, blk_e_ref, nused_ref, *_):
        return (jnp.minimum(b, nused_ref[0] - 1), 0)

    def e_map(b, blk_e_ref, nused_ref, *_):
        return (blk_e_ref[jnp.minimum(b, nused_ref[0] - 1)], 0, 0)

    grid_spec = pltpu.PrefetchScalarGridSpec(
        num_scalar_prefetch=5,
        grid=(nblk,),
        in_specs=[
            pl.BlockSpec((MOE_BLOCK * ROW_TILES, LANES), blk_map),
            pl.BlockSpec(memory_space=pl.ANY),
            pl.BlockSpec(memory_space=pl.ANY),
            pl.BlockSpec((1, 1, ff), e_map),
            pl.BlockSpec((1, 1, ff), e_map),
            pl.BlockSpec((1, 1, d), e_map),
            pl.BlockSpec((SPLIT_W, SPLIT_W), lambda b, *_: (0, 0)),
        ],
        out_specs=pl.BlockSpec((MOE_BLOCK * ROW_TILES, LANES), lambda b, *_: (b, 0)),
        scratch_shapes=[
            pltpu.VMEM((2, d, 2 * ff), F32), pltpu.VMEM((2, ff, d), F32),
            pltpu.VMEM((d, ff), BF16), pltpu.VMEM((d, ff), BF16), pltpu.VMEM((ff, d), BF16),
            pltpu.SemaphoreType.DMA((2, 2)),
        ],
    )
    return pl.pallas_call(
        functools.partial(_expert_kernel, layer=layer),
        grid_spec=grid_spec,
        out_shape=jax.ShapeDtypeStruct(xg_rows.shape, F32),
        compiler_params=_cparams("arbitrary"),
        name="moe_experts",
    )(blk_e, nused, next_e, e_slot, blk_rows, xg_rows, w1_all, w2_all, b_glu, b_lin, b2, sel)


def _combine_kernel(n_ref, carry_ref, gstart_ref, yg_ref, lp_ref, gate_ref, x_ref, g2_ref, o_ref, sorted_ref, sem):
    i = pl.program_id(0)
    ntiles = pl.num_programs(0)
    tm = x_ref.shape[0]
    rows = TOP_K * tm
    slot = i % 2

    def runs(tile, buf, fn, live=None):
        _tile_runs(tile, n_ref, carry_ref, gstart_ref, sorted_ref.at[buf], yg_ref, sem.at[buf], tm, fn, False, live)

    @pl.when(i == 0)
    def _():
        runs(0, 0, _start_piece)

    runs(jnp.minimum(i + 1, ntiles - 1), 1 - slot, _start_piece, live=i + 1 < ntiles)

    eye = (lax.broadcasted_iota(jnp.int32, (tm, tm), 0) ==
           lax.broadcasted_iota(jnp.int32, (tm, tm), 1)).astype(F32)
    rows_t = jnp.concatenate([gate_ref[...], lp_ref[...].astype(F32)], axis=0)
    cols = lax.dot_general(eye, rows_t, (((1,), (1,)), ((), ())), preferred_element_type=F32, precision=HI)
    col_id = lax.broadcasted_iota(jnp.int32, (tm, rows), 1)
    weights = jnp.zeros((tm, rows), F32)
    for k in range(TOP_K):
        lp_col = (cols[:, TOP_K + k:TOP_K + k + 1] + 0.5).astype(jnp.int32)
        weights = weights + jnp.where(col_id == lp_col, cols[:, k:k + 1], 0.0)

    pltpu.make_async_copy(yg_ref.at[pl.ds(0, rows * ROW_TILES)], sorted_ref.at[slot], sem.at[slot]).wait()
    y = jnp.concatenate(
        [sorted_ref[slot, pl.ds(s, rows, stride=ROW_TILES), :] for s in range(ROW_TILES)], axis=1).astype(BF16)
    acc = jnp.dot(weights.astype(BF16), y, preferred_element_type=F32)
    o_ref[...] = x_ref[...] + g2_ref[0] * acc


def _combine(tile_n, tile_carry, group_start, yg_rows, lp_t, gate_t, x, gate2, seq, tm):
    t, d = x.shape
    per_b = seq // tm
    grid_spec = pltpu.PrefetchScalarGridSpec(
        num_scalar_prefetch=3,
        grid=(t // tm,),
        in_specs=[
            pl.BlockSpec(memory_space=pl.ANY),
            pl.BlockSpec((TOP_K, tm), lambda i, *_: (0, i)),
            pl.BlockSpec((TOP_K, tm), lambda i, *_: (0, i)),
            pl.BlockSpec((tm, d), lambda i, *_: (i, 0)),
            pl.BlockSpec((1, 1, d), lambda i, *_: (i // per_b, 0, 0)),
        ],
        out_specs=pl.BlockSpec((tm, d), lambda i, *_: (i, 0)),
        scratch_shapes=[pltpu.VMEM((2, TOP_K * tm * ROW_TILES, LANES), F32), pltpu.SemaphoreType.DMA((2,))],
    )
    return pl.pallas_call(
        _combine_kernel,
        grid_spec=grid_spec,
        out_shape=jax.ShapeDtypeStruct((t, d), F32),
        compiler_params=_cparams("arbitrary"),
        name="moe_combine",
    )(tile_n, tile_carry, group_start, yg_rows, lp_t, gate_t, x, gate2)


def _moe(layer, x, g, shift, scale, gate2, r_w, r_b, w1_all, b1_glu, b1_lin, w2_all, b2, seq):
    t = x.shape[0]
    h, lp_t, gate_t, tile_n, tile_carry, counts = _router(x, g, shift, scale, r_w, r_b, seq, MOE_TILE)
    tile_n = tile_n.reshape(-1)
    tile_carry = tile_carry.reshape(-1)
    counts = counts.reshape(-1)
    padded = ((counts + MOE_BLOCK - 1) // MOE_BLOCK) * MOE_BLOCK
    group_end = jnp.cumsum(padded)
    group_start = (group_end - padded).astype(jnp.int32)
    cap = t * TOP_K + N_EXPERTS * MOE_BLOCK
    nblk = cap // MOE_BLOCK
    blk_start = jnp.arange(nblk, dtype=jnp.int32) * MOE_BLOCK
    e_ids = jnp.arange(N_EXPERTS, dtype=jnp.int32)
    past = (blk_start[None, :] >= group_end[:, None]).astype(jnp.int32)
    blk_e = jnp.minimum(jnp.sum(past, axis=0), N_EXPERTS - 1).astype(jnp.int32)
    nused = (group_end[-1:] // MOE_BLOCK).astype(jnp.int32)

    pad_lo = (group_start + counts).astype(jnp.int32)
    pad_hi = group_end.astype(jnp.int32)
    blk_pad_lo = jnp.sum(jnp.where(blk_e[None, :] == e_ids[:, None], pad_lo[:, None], 0), axis=0)
    blk_rows = jnp.clip(blk_pad_lo - blk_start, 0, MOE_BLOCK).astype(jnp.int32)
    xg_rows = _dispatch(tile_n, tile_carry, group_start, pad_lo, pad_hi, nused, lp_t, h, cap, MOE_TILE)
    nonempty = counts > 0
    later = jnp.where(nonempty[None, :] & (e_ids[None, :] > e_ids[:, None]), e_ids[None, :], N_EXPERTS)
    next_e = jnp.min(later, axis=1)
    next_e = jnp.where(next_e == N_EXPERTS, -1, next_e).astype(jnp.int32)
    e_slot = ((jnp.cumsum(nonempty.astype(jnp.int32)) - 1) % 2).astype(jnp.int32)
    yg_rows = _experts(layer, blk_e, nused, next_e, e_slot, blk_rows, xg_rows, w1_all, w2_all,
                       b1_glu[:, None, :], b1_lin[:, None, :], b2[:, None, :])
    return _combine(tile_n, tile_carry, group_start, yg_rows, lp_t, gate_t, x, gate2, seq, MOE_TILE)


def kernel(x, c, ada_w, ada_b, norm1_g, norm2_g, m_in_w, m_conv_w, m_conv_b, m_dt_bias, m_A_log, m_D, m_norm_g, m_out_w, a_qkv_w, a_q_norm_g, a_k_norm_g, a_sinks, a_out_w, rel_bias, r_w, r_b, e_w1, e_b1, e_w2, e_b2):
    batch, seq, d = x.shape
    depth = ada_w.shape[0]
    t = batch * seq
    xf = x.reshape(t, d)

    mod = _adaln(c, ada_w, ada_b)[:, :batch]

    b1_split = jnp.moveaxis(e_b1.reshape(depth, N_EXPERTS, -1, 2), -1, 1)

    for i in range(depth):
        parts =[mod[i, :, p * d:(p + 1) * d].reshape(batch, 1, d) for p in range(6)]
        sh1, sc1, g1, sh2, sc2, g2 = parts
        j = i // 2
        if i % 2 == 0:
            w_zxbc = m_in_w[j, :, :SSM_ZXBC].astype(BF16)
            w_dt = jnp.pad(m_in_w[j, :, SSM_ZXBC:], ((0, 0), (0, LANES - SSM_HEADS)))
            zxbc, dt_raw = _norm_matmul(xf, norm1_g[i], sh1, sc1, (w_zxbc, w_dt), seq)
            y = _ssd_mixer(zxbc, dt_raw, m_conv_w[j], m_conv_b[j], m_dt_bias[j], m_A_log[j], m_D[j],
                           m_norm_g[j], batch, seq)
            xf = _matmul_residual(y, m_out_w[j].astype(BF16), xf, g1, seq)
        else:
            qkv, = _norm_matmul(xf, norm1_g[i], sh1, sc1, (a_qkv_w[j].astype(BF16),), seq)
            y = _swa_mixer(qkv, a_q_norm_g[j], a_k_norm_g[j], a_sinks[j], rel_bias, batch, seq)
            xf = _matmul_residual(y, a_out_w[j].astype(BF16), xf, g1, seq)
        xf = _moe(i, xf, norm2_g[i], sh2, sc2, g2, r_w[i], r_b[i], e_w1, b1_split[i, 0], b1_split[i, 1],
                  e_w2, e_b2[i], seq)
    return xf.reshape(batch, seq, d)
```

```python
import functools
import math

import jax
import jax.numpy as jnp
from jax import lax
from jax.experimental import pallas as pl
from jax.experimental.pallas import tpu as pltpu

D_MODEL = 1024
EPS = 1e-6
LANES = 128
SUBLANES = 8
ROW_TILES = D_MODEL // LANES

SSM_D_INNER = 2048
SSM_HEAD_DIM = 64
SSM_HEADS = 32
SSM_GROUPS = 4
SSM_STATE = 128
SSM_CONV = 4
SSM_CHUNK = 128
SSM_STEP_CHUNKS = 1
SSM_GN = SSM_GROUPS * SSM_STATE
SSM_ZXBC = 2 * SSM_D_INNER + 2 * SSM_GN
SSM_GROUP_W = SSM_D_INNER // SSM_GROUPS

ATTN_HEAD_DIM = 64
ATTN_Q_HEADS = 16
ATTN_KV_HEADS = 4
ATTN_Q_PER_KV = 4
WINDOW = 128
SWA_STEP_BLOCKS = 4
REL_BUCKETS = 32
REL_MAX_DIST = 128

N_EXPERTS = 32
TOP_K = 4
SWIGLU_ALPHA = 1.702
SWIGLU_LIMIT = 7.0
MOE_BLOCK = 512
MOE_SUB = 128
MOE_TILE = 256
SPLIT_W = 256

VMEM_LIMIT = 56 * 1024 * 1024
HI = lax.Precision.HIGHEST
F32 = jnp.float32
BF16 = jnp.bfloat16


def _cparams(*sem):
    return pltpu.CompilerParams(dimension_semantics=sem, vmem_limit_bytes=VMEM_LIMIT)


def _norm_modulate(x, g, shift, scale):
    ms = jnp.mean(x * x, axis=-1, keepdims=True)
    return x * lax.rsqrt(ms + EPS) * g * (1.0 + scale) + shift


def _adaln_kernel(ct_ref, w_ref, b_ref, o_ref, *, batch):
    c_t = ct_ref[...]
    act_t = c_t * jax.nn.sigmoid(c_t)
    w = w_ref[0]
    rows = [jnp.sum(act_t[:, b:b + 1] * w, axis=0, keepdims=True) + b_ref[0] for b in range(batch)]
    rows.append(jnp.zeros((SUBLANES - batch, w.shape[1]), F32))
    o_ref[0] = jnp.concatenate(rows, axis=0)


def _adaln(c, ada_w, ada_b):
    depth, d, n = ada_w.shape
    batch = c.shape[0]
    tn = 1536
    c_t = jnp.zeros((d, SUBLANES), F32).at[:, :batch].set(c.T)
    return pl.pallas_call(
        functools.partial(_adaln_kernel, batch=batch),
        grid=(depth, n // tn),
        in_specs=[
            pl.BlockSpec((d, SUBLANES), lambda i, j: (0, 0)),
            pl.BlockSpec((1, d, tn), lambda i, j: (i, 0, j)),
            pl.BlockSpec((1, 1, tn), lambda i, j: (i, 0, j)),
        ],
        out_specs=pl.BlockSpec((1, SUBLANES, tn), lambda i, j: (i, 0, j)),
        out_shape=jax.ShapeDtypeStruct((depth, SUBLANES, n), F32),
        compiler_params=_cparams("arbitrary", "arbitrary"),
        name="adaln",
    )(c_t, ada_w, ada_b.reshape(depth, 1, n))


def _norm_matmul_kernel(x_ref, g_ref, sh_ref, sc_ref, *refs):
    w_refs, o_refs = refs[:len(refs) // 2], refs[len(refs) // 2:]
    h = _norm_modulate(x_ref[...], g_ref[...], sh_ref[0], sc_ref[0]).astype(BF16)
    for w_ref, o_ref in zip(w_refs, o_refs):
        o_ref[...] = jnp.dot(h, w_ref[...].astype(BF16), preferred_element_type=F32)


def _norm_matmul(x, g, shift, scale, weights_bf16, seq, tm=512):
    t, d = x.shape
    per_b = seq // tm
    return pl.pallas_call(
        _norm_matmul_kernel,
        grid=(t // tm,),
        in_specs=[
            pl.BlockSpec((tm, d), lambda i: (i, 0)),
            pl.BlockSpec((1, d), lambda i: (0, 0)),
            pl.BlockSpec((1, 1, d), lambda i: (i // per_b, 0, 0)),
            pl.BlockSpec((1, 1, d), lambda i: (i // per_b, 0, 0)),
        ] + [pl.BlockSpec(w.shape, lambda i: (0, 0), pipeline_mode=pl.Buffered(1))
             for w in weights_bf16],
        out_specs=[pl.BlockSpec((tm, w.shape[1]), lambda i: (i, 0)) for w in weights_bf16],
        out_shape=[jax.ShapeDtypeStruct((t, w.shape[1]), F32) for w in weights_bf16],
        compiler_params=_cparams("arbitrary"),
        name="norm_matmul",
    )(x, g.reshape(1, d), shift, scale, *weights_bf16)


def _matmul_residual_kernel(y_ref, w_ref, x_ref, gate_ref, o_ref):
    acc = jnp.dot(y_ref[...], w_ref[...], preferred_element_type=F32)
    o_ref[...] = x_ref[...] + gate_ref[0] * acc


def _matmul_residual(y_bf16, w_bf16, x, gate, seq, tm=1024):
    t, k = y_bf16.shape
    d = x.shape[1]
    per_b = seq // tm
    return pl.pallas_call(
        _matmul_residual_kernel,
        grid=(t // tm,),
        in_specs=[
            pl.BlockSpec((tm, k), lambda i: (i, 0)),
            pl.BlockSpec((k, d), lambda i: (0, 0)),
            pl.BlockSpec((tm, d), lambda i: (i, 0)),
            pl.BlockSpec((1, 1, d), lambda i: (i // per_b, 0, 0)),
        ],
        out_specs=pl.BlockSpec((tm, d), lambda i: (i, 0)),
        out_shape=jax.ShapeDtypeStruct((t, d), F32),
        compiler_params=_cparams("arbitrary"),
        name="matmul_residual",
    )(y_bf16, w_bf16, x, gate)


def _causal_conv_silu(cur, prev_tail, w, b):
    rows = lax.broadcasted_iota(jnp.int32, (SUBLANES, cur.shape[1]), 0)
    acc = b + w[SSM_CONV - 1:SSM_CONV] * cur
    for d in range(1, SSM_CONV):
        rolled = pltpu.roll(cur, d, axis=0)
        top = jnp.where(rows < d, pltpu.roll(prev_tail, d, axis=0), rolled[0:SUBLANES])
        shifted = jnp.concatenate([top, rolled[SUBLANES:]], axis=0)
        acc = acc + w[SSM_CONV - 1 - d:SSM_CONV - d] * shifted
    return acc * jax.nn.sigmoid(acc)


def _split3(x):
    hi = x.astype(BF16)
    rem = x - hi.astype(F32)
    mid = rem.astype(BF16)
    lo = (rem - mid.astype(F32)).astype(BF16)
    return jnp.concatenate([hi, mid, lo], axis=1)


def _ssd_kernel(z_ref, xs_ref, bc_ref, dt_ref, *refs):
    o_ref, tailx_ref, tailb_ref, state_ref = refs[-4:]

    @pl.when(pl.program_id(1) == 0)
    def _():
        tailx_ref[...] = jnp.zeros_like(tailx_ref)
        tailb_ref[...] = jnp.zeros_like(tailb_ref)
        state_ref[...] = jnp.zeros_like(state_ref)

    for u in range(SSM_STEP_CHUNKS):
        rows = pl.ds(u * SSM_CHUNK, SSM_CHUNK)
        _ssd_chunk(z_ref.at[rows], xs_ref.at[rows], bc_ref.at[rows], dt_ref.at[rows], *refs[:-4],
                   o_ref.at[rows], tailx_ref, tailb_ref, state_ref)


def _ssd_chunk(z_ref, xs_ref, bc_ref, dt_ref, cwx_ref, cbx_ref, cwb_ref, cbb_ref, dtb_ref, alog_ref,
               dskip_ref, ng_ref, hexp_ref, lexp_ref, o_ref, tailx_ref, tailb_ref, state_ref):
    xs_raw = xs_ref[...]
    bc_raw = bc_ref[...]
    xs = _causal_conv_silu(xs_raw, tailx_ref[...], cwx_ref[...], cbx_ref[...])
    bc = _causal_conv_silu(bc_raw, tailb_ref[...], cwb_ref[...], cbb_ref[...])
    tailx_ref[...] = xs_raw[SSM_CHUNK - SUBLANES:]
    tailb_ref[...] = bc_raw[SSM_CHUNK - SUBLANES:]

    dt_in = dt_ref[...][:, :SSM_HEADS] + dtb_ref[...]
    dt = jnp.maximum(dt_in, 0.0) + jnp.log1p(jnp.exp(-jnp.abs(dt_in)))
    a_neg = -jnp.exp(alog_ref[...])
    d_a = dt * a_neg
    li = lax.broadcasted_iota(jnp.int32, (SSM_CHUNK, SSM_CHUNK), 0)
    si = lax.broadcasted_iota(jnp.int32, (SSM_CHUNK, SSM_CHUNK), 1)
    causal = li >= si
    tri = causal.astype(F32)
    a_cs = jnp.dot(tri, d_a, preferred_element_type=F32, precision=HI)
    a_cs_t = lax.dot_general(d_a, tri, (((0,), (1,)), ((), ())),
                             preferred_element_type=F32, precision=HI)
    a_last = a_cs[SSM_CHUNK - 1:SSM_CHUNK]
    e_out = jnp.exp(a_cs)
    e_state = jnp.exp(a_last - a_cs) * dt
    small = jnp.concatenate([dt, e_out, e_state], axis=0)
    wide = jnp.dot(_split3(small), hexp_ref[...], preferred_element_type=F32)
    dt_w = wide[0:SSM_CHUNK]
    e_out_w = wide[SSM_CHUNK:2 * SSM_CHUNK]
    e_state_w = wide[2 * SSM_CHUNK:]
    a_col = jnp.dot(_split3(a_cs), lexp_ref[...], preferred_element_type=F32)

    x_dt = (xs * dt_w).astype(BF16)
    x_state = (xs * e_state_w).astype(BF16)
    chunk_decay_w = e_out_w[SSM_CHUNK - 1:SSM_CHUNK]

    heads_per_group = SSM_HEADS // SSM_GROUPS
    y_parts = []
    for g in range(SSM_GROUPS):
        b_g = bc[:, g * SSM_STATE:(g + 1) * SSM_STATE].astype(BF16)
        c_g = bc[:, SSM_GN + g * SSM_STATE:SSM_GN + (g + 1) * SSM_STATE].astype(BF16)
        cb = lax.dot_general(c_g, b_g, (((1,), (1,)), ((), ())), preferred_element_type=F32)
        gsl = slice(g * SSM_GROUP_W, (g + 1) * SSM_GROUP_W)
        h_prev = state_ref[g]
        y_off = jnp.dot(c_g, h_prev.astype(BF16), preferred_element_type=F32) * e_out_w[:, gsl]
        diag = []
        for r in range(heads_per_group):
            h = g * heads_per_group + r
            seg = a_col[:, h * SSM_CHUNK:(h + 1) * SSM_CHUNK] - a_cs_t[h:h + 1, :]
            decay = jnp.exp(jnp.where(causal, seg, -jnp.inf))
            m = (cb * decay).astype(BF16)
            diag.append(jnp.dot(m, x_dt[:, h * SSM_HEAD_DIM:(h + 1) * SSM_HEAD_DIM],
                                preferred_element_type=F32))
        y_parts.append(jnp.concatenate(diag, axis=1) + y_off)
        upd = lax.dot_general(b_g, x_state[:, gsl], (((0,), (0,)), ((), ())), preferred_element_type=F32)
        state_ref[g] = h_prev * chunk_decay_w[:, gsl] + upd

    y = jnp.concatenate(y_parts, axis=1) + dskip_ref[...] * xs
    z = z_ref[...]
    y = y * (z * jax.nn.sigmoid(z))
    normed = []
    for g in range(SSM_GROUPS):
        y_g = y[:, g * SSM_GROUP_W:(g + 1) * SSM_GROUP_W]
        normed.append(y_g * lax.rsqrt(jnp.mean(y_g * y_g, axis=-1, keepdims=True) + EPS))
    o_ref[...] = (jnp.concatenate(normed, axis=1) * ng_ref[...]).astype(o_ref.dtype)


def _ssd_mixer(zxbc, dt_raw, conv_w, conv_b, dt_bias, a_log, d_skip, norm_g, batch, seq):
    t = zxbc.shape[0]
    step_rows = SSM_CHUNK * SSM_STEP_CHUNKS
    nc = seq // step_rows
    head_expand = jnp.tile(jnp.repeat(jnp.eye(SSM_HEADS, dtype=BF16), SSM_HEAD_DIM, axis=1), (3, 1))
    lane_expand = jnp.tile(jnp.repeat(jnp.eye(SSM_HEADS, dtype=BF16), SSM_CHUNK, axis=1), (3, 1))
    row = lambda b, c: (b * nc + c, 0)
    const2 = lambda b, c: (0, 0)
    bc_w = 2 * SSM_GN
    return pl.pallas_call(
        _ssd_kernel,
        grid=(batch, nc),
        in_specs=[
            pl.BlockSpec((step_rows, SSM_D_INNER), row),
            pl.BlockSpec((step_rows, SSM_D_INNER), lambda b, c: (b * nc + c, 1)),
            pl.BlockSpec((step_rows, bc_w), lambda b, c: (b * nc + c, 2 * SSM_D_INNER // bc_w)),
            pl.BlockSpec((step_rows, LANES), row),
            pl.BlockSpec((SSM_CONV, SSM_D_INNER), const2),
            pl.BlockSpec((1, SSM_D_INNER), const2),
            pl.BlockSpec((SSM_CONV, bc_w), const2),
            pl.BlockSpec((1, bc_w), const2),
            pl.BlockSpec((1, SSM_HEADS), const2),
            pl.BlockSpec((1, SSM_HEADS), const2),
            pl.BlockSpec((1, SSM_D_INNER), const2),
            pl.BlockSpec((1, SSM_D_INNER), const2),
            pl.BlockSpec((3 * SSM_HEADS, SSM_D_INNER), const2),
            pl.BlockSpec((3 * SSM_HEADS, SSM_HEADS * SSM_CHUNK), const2),
        ],
        out_specs=pl.BlockSpec((step_rows, SSM_D_INNER), row),
        out_shape=jax.ShapeDtypeStruct((t, SSM_D_INNER), BF16),
        scratch_shapes=[
            pltpu.VMEM((SUBLANES, SSM_D_INNER), F32),
            pltpu.VMEM((SUBLANES, bc_w), F32),
            pltpu.VMEM((SSM_GROUPS, SSM_STATE, SSM_GROUP_W), F32),
        ],
        compiler_params=_cparams("arbitrary", "arbitrary"),
        name="ssd_mixer",
    )(zxbc, zxbc, zxbc, dt_raw,
      conv_w[:, :SSM_D_INNER], conv_b[:SSM_D_INNER].reshape(1, -1),
      conv_w[:, SSM_D_INNER:], conv_b[SSM_D_INNER:].reshape(1, -1),
      dt_bias.reshape(1, -1), a_log.reshape(1, -1),
      jnp.repeat(d_skip, SSM_HEAD_DIM).reshape(1, -1), norm_g.reshape(1, -1),
      head_expand, lane_expand)


def _head_rms(x, g):
    return x * lax.rsqrt(jnp.mean(x * x, axis=-1, keepdims=True) + EPS) * g


def _swa_kernel(q_ref, kvc_ref, kvp_ref, bucket_ref, qg_ref, kg_ref, rel_ref, sink_ref, o_ref,
                bias_ref, sinkrow_ref):
    b = pl.program_id(0)
    i = pl.program_id(1)

    @pl.when((b == 0) & (i == 0))
    def _():
        bucket = bucket_ref[...]
        kj = lax.broadcasted_iota(jnp.int32, (2 * WINDOW, WINDOW), 0)
        qi = lax.broadcasted_iota(jnp.int32, (2 * WINDOW, WINDOW), 1)
        dist = qi + WINDOW - kj
        band = (dist >= 0) & (dist < WINDOW)
        for h in range(ATTN_Q_HEADS):
            g, r = divmod(h, ATTN_Q_PER_KV)
            acc = jnp.zeros(bucket.shape, F32)
            for k in range(REL_BUCKETS):
                acc = jnp.where(bucket == k, rel_ref[k, h], acc)
            cols = slice(r * WINDOW, (r + 1) * WINDOW)
            bias_ref[0, g, :, cols] = jnp.where(band, acc, -jnp.inf)
            bias_ref[1, g, :, cols] = jnp.where(band & (kj >= WINDOW), acc, -jnp.inf)
            sinkrow_ref[g, :, cols] = jnp.full((1, WINDOW), sink_ref[h], F32)

    for u in range(SWA_STEP_BLOCKS):
        rows = slice(u * WINDOW, (u + 1) * WINDOW)
        kv_p = kvp_ref[...] if u == 0 else kvc_ref[(u - 1) * WINDOW:u * WINDOW, :]
        first = (i == 0).astype(jnp.int32) if u == 0 else 0
        o_ref[rows, :] = _swa_block(q_ref[rows, :], kvc_ref[rows, :], kv_p, first,
                                    qg_ref, kg_ref, bias_ref, sinkrow_ref).astype(o_ref.dtype)


def _swa_block(q_rows, kv_c, kv_p, first, qg_ref, kg_ref, bias_ref, sinkrow_ref):
    q_t = q_rows.T
    kv_w = ATTN_KV_HEADS * ATTN_HEAD_DIM
    q_gain = qg_ref[...]
    outs = []
    for g in range(ATTN_KV_HEADS):
        ksl = slice(g * ATTN_HEAD_DIM, (g + 1) * ATTN_HEAD_DIM)
        vsl = slice(kv_w + g * ATTN_HEAD_DIM, kv_w + (g + 1) * ATTN_HEAD_DIM)
        k = jnp.concatenate([kv_p[:, ksl], kv_c[:, ksl]], axis=0)
        v = jnp.concatenate([kv_p[:, vsl], kv_c[:, vsl]], axis=0).astype(BF16)
        k = _head_rms(k, kg_ref[...]).astype(BF16)
        q_heads = []
        for r in range(ATTN_Q_PER_KV):
            h = g * ATTN_Q_PER_KV + r
            q_h = q_t[h * ATTN_HEAD_DIM:(h + 1) * ATTN_HEAD_DIM]
            inv = lax.rsqrt(jnp.mean(q_h * q_h, axis=0, keepdims=True) + EPS)
            q_heads.append((q_h * inv * q_gain).astype(BF16))
        q = jnp.concatenate(q_heads, axis=1)
        s = jnp.dot(k, q, preferred_element_type=F32) + bias_ref[first, g]
        sink = sinkrow_ref[g]
        m = jnp.maximum(jnp.max(s, axis=0, keepdims=True), sink)
        p = jnp.exp(s - m)
        denom = jnp.sum(p, axis=0, keepdims=True) + jnp.exp(sink - m)
        pv = lax.dot_general(v, p.astype(BF16), (((0,), (0,)), ((), ())), preferred_element_type=F32)
        pv = pv * (1.0 / denom)
        outs.extend(pv[:, r * WINDOW:(r + 1) * WINDOW] for r in range(ATTN_Q_PER_KV))
    return jnp.concatenate(outs, axis=0).T


def _t5_causal_bucket(dist):
    max_exact = REL_BUCKETS // 2
    d = jnp.maximum(dist, 1).astype(F32)
    large = max_exact + (jnp.log(d / max_exact) / math.log(REL_MAX_DIST / max_exact)
                         * (REL_BUCKETS - max_exact)).astype(jnp.int32)
    large = jnp.minimum(large, REL_BUCKETS - 1)
    return jnp.where(dist < max_exact, dist, large)


def _swa_mixer(qkv, q_norm_g, k_norm_g, sinks, rel_bias, batch, seq):
    t = qkv.shape[0]
    nb = seq // WINDOW
    step_rows = WINDOW * SWA_STEP_BLOCKS
    ns = seq // step_rows
    q_w = ATTN_Q_HEADS * ATTN_HEAD_DIM
    kv_w2 = 2 * ATTN_KV_HEADS * ATTN_HEAD_DIM
    kj = jnp.arange(2 * WINDOW)[:, None]
    qi = jnp.arange(WINDOW)[None, :]
    bucket = _t5_causal_bucket(jnp.maximum(qi + WINDOW - kj, 0)).astype(jnp.int32)
    q_gain = jnp.broadcast_to((q_norm_g * (ATTN_HEAD_DIM ** -0.5))[:, None], (ATTN_HEAD_DIM, WINDOW))
    const2 = lambda b, i: (0, 0)
    smem = pl.BlockSpec(memory_space=pltpu.SMEM)
    return pl.pallas_call(
        _swa_kernel,
        grid=(batch, ns),
        in_specs=[
            pl.BlockSpec((step_rows, q_w), lambda b, i: (b * ns + i, 0)),
            pl.BlockSpec((step_rows, kv_w2), lambda b, i: (b * ns + i, q_w // kv_w2)),
            pl.BlockSpec((WINDOW, kv_w2),
                         lambda b, i: (b * nb + jnp.maximum(i * SWA_STEP_BLOCKS - 1, 0), q_w // kv_w2)),
            pl.BlockSpec((2 * WINDOW, WINDOW), const2),
            pl.BlockSpec((ATTN_HEAD_DIM, WINDOW), const2),
            pl.BlockSpec((1, ATTN_HEAD_DIM), const2),
            smem,
            smem,
        ],
        out_specs=pl.BlockSpec((step_rows, q_w), lambda b, i: (b * ns + i, 0)),
        out_shape=jax.ShapeDtypeStruct((t, q_w), BF16),
        scratch_shapes=[pltpu.VMEM((2, ATTN_KV_HEADS, 2 * WINDOW, ATTN_Q_PER_KV * WINDOW), F32),
                        pltpu.VMEM((ATTN_KV_HEADS, 1, ATTN_Q_PER_KV * WINDOW), F32)],
        compiler_params=_cparams("arbitrary", "arbitrary"),
        name="swa_mixer",
    )(qkv, qkv, qkv, bucket, q_gain, k_norm_g.reshape(1, -1), rel_bias, sinks)


def _router_kernel(x_ref, g_ref, sh_ref, sc_ref, rwt_ref, rb_ref,
                   h_ref, lp_ref, gate_ref, tile_n_ref, tile_carry_ref, cnt_ref, carry_ref):
    i = pl.program_id(0)
    tm = x_ref.shape[0]

    @pl.when(i == 0)
    def _():
        carry_ref[...] = jnp.zeros_like(carry_ref)

    h = _norm_modulate(x_ref[...], g_ref[...], sh_ref[0], sc_ref[0]).astype(BF16)
    h_ref[...] = h

    logits = lax.dot_general(rwt_ref[...], h, (((1,), (1,)), ((), ())),
                             preferred_element_type=F32) + rb_ref[...]
    e_iota = lax.broadcasted_iota(jnp.int32, logits.shape, 0)
    work = logits
    sels, vals = [], []
    for k in range(TOP_K):
        m = jnp.max(work, axis=0, keepdims=True)
        idx = jnp.min(jnp.where(work == m, e_iota, N_EXPERTS), axis=0, keepdims=True)
        sel = e_iota == idx
        work = jnp.where(sel, -jnp.inf, work)
        sels.append(sel)
        vals.append(m)
    exps = [jnp.exp(v - vals[0]) for v in vals]
    denom = exps[0] + exps[1] + exps[2] + exps[3]
    for k in range(TOP_K):
        gate_ref[k:k + 1, :] = exps[k] / denom

    chosen = sels[0] | sels[1] | sels[2] | sels[3]
    t_row = lax.broadcasted_iota(jnp.int32, (tm, tm), 0)
    t_col = lax.broadcasted_iota(jnp.int32, (tm, tm), 1)
    before = (t_row < t_col).astype(BF16)
    prior = jnp.dot(chosen.astype(BF16), before, preferred_element_type=F32)
    n = jnp.sum(chosen.astype(F32), axis=1, keepdims=True)
    e_row = lax.broadcasted_iota(jnp.int32, (N_EXPERTS, N_EXPERTS), 0)
    e_col = lax.broadcasted_iota(jnp.int32, (N_EXPERTS, N_EXPERTS), 1)
    run_start = jnp.dot((e_col < e_row).astype(BF16), jnp.broadcast_to(n, (N_EXPERTS, LANES)).astype(BF16),
                        preferred_element_type=F32)[:, :1]
    local = run_start + prior
    for k in range(TOP_K):
        lp_ref[k:k + 1, :] = jnp.sum(jnp.where(sels[k], local, 0.0), axis=0, keepdims=True).astype(jnp.int32)
    tile_n_ref[0] = n.astype(jnp.int32)
    tile_carry_ref[0] = carry_ref[...].astype(jnp.int32)
    total = carry_ref[...] + n
    carry_ref[...] = total
    cnt_ref[...] = total.astype(jnp.int32)


def _router(x, g, shift, scale, r_w, r_b, seq, tm):
    t, d = x.shape
    per_b = seq // tm
    ntiles = t // tm
    return pl.pallas_call(
        _router_kernel,
        grid=(t // tm,),
        in_specs=[
            pl.BlockSpec((tm, d), lambda i: (i, 0)),
            pl.BlockSpec((1, d), lambda i: (0, 0)),
            pl.BlockSpec((1, 1, d), lambda i: (i // per_b, 0, 0)),
            pl.BlockSpec((1, 1, d), lambda i: (i // per_b, 0, 0)),
            pl.BlockSpec((N_EXPERTS, d), lambda i: (0, 0)),
            pl.BlockSpec((N_EXPERTS, 1), lambda i: (0, 0)),
        ],
        out_specs=[
            pl.BlockSpec((tm, d), lambda i: (i, 0)),
            pl.BlockSpec((TOP_K, tm), lambda i: (0, i)),
            pl.BlockSpec((TOP_K, tm), lambda i: (0, i)),
            pl.BlockSpec((1, N_EXPERTS, 1), lambda i: (i, 0, 0)),
            pl.BlockSpec((1, N_EXPERTS, 1), lambda i: (i, 0, 0)),
            pl.BlockSpec((N_EXPERTS, 1), lambda i: (0, 0)),
        ],
        out_shape=[
            jax.ShapeDtypeStruct((t, d), BF16),
            jax.ShapeDtypeStruct((TOP_K, t), jnp.int32),
            jax.ShapeDtypeStruct((TOP_K, t), F32),
            jax.ShapeDtypeStruct((ntiles, N_EXPERTS, 1), jnp.int32),
            jax.ShapeDtypeStruct((ntiles, N_EXPERTS, 1), jnp.int32),
            jax.ShapeDtypeStruct((N_EXPERTS, 1), jnp.int32),
        ],
        scratch_shapes=[pltpu.VMEM((N_EXPERTS, 1), F32)],
        compiler_params=_cparams("arbitrary"),
        name="moe_router",
    )(x, g.reshape(1, d), shift, scale, r_w.T.astype(BF16), r_b.reshape(-1, 1))


def _run_copies(n, src_ref, src_row, dst_ref, dst_row, sem, max_rows, fn):
    for b in range(max_rows.bit_length()):
        size = 1 << b

        @pl.when((n & size) != 0)
        def _():
            lo = n & (size - 1)
            fn(pltpu.make_async_copy(
                src_ref.at[pl.ds(pl.multiple_of((src_row + lo) * ROW_TILES, ROW_TILES), size * ROW_TILES)],
                dst_ref.at[pl.ds(pl.multiple_of((dst_row + lo) * ROW_TILES, ROW_TILES), size * ROW_TILES)],
                sem), b)


def _start_piece(cp, b):
    cp.start()


def _wait_piece(cp, b):
    cp.wait()


def _tile_runs(tile, n_ref, carry_ref, gstart_ref, sorted_ref, grouped_ref, sem, tm, fn, to_grouped, live=None):
    def body(e, run_start):
        n = n_ref[tile * N_EXPERTS + e]
        if live is not None:
            n = jnp.where(live, n, 0)
        slot0 = gstart_ref[e] + carry_ref[tile * N_EXPERTS + e]
        if to_grouped:
            _run_copies(n, sorted_ref, run_start, grouped_ref, slot0, sem, tm, fn)
        else:
            _run_copies(n, grouped_ref, slot0, sorted_ref, run_start, sem, tm, fn)
        return run_start + n

    if live is None:
        lax.fori_loop(0, N_EXPERTS, body, 0)
    else:
        run_start = 0
        for e in range(N_EXPERTS):
            run_start = body(e, run_start)


def _block_copy(src_ref, dst_ref, dst_blk, sem):
    rows = MOE_BLOCK * ROW_TILES
    return pltpu.make_async_copy(src_ref, dst_ref.at[pl.ds(pl.multiple_of(dst_blk * rows, rows), rows)], sem)


def _dispatch_kernel(n_ref, carry_ref, gstart_ref, pad_lo_ref, pad_hi_ref, nused_ref, lp_ref, h_ref, xg_ref,
                     sorted0_ref, sorted1_ref, sorted2_ref, zero_ref, run_sem, sem):
    i = pl.program_id(0)
    last = pl.num_programs(0) - 1
    tm = h_ref.shape[0]
    rows = TOP_K * tm
    nblk =xg_ref.shape[0] // (MOE_BLOCK * ROW_TILES)

    @pl.when(i == 0)
    def _():
        zero_ref[...] = jnp.zeros_like(zero_ref)

        def fill(b, carry):
            _block_copy(zero_ref, xg_ref, b, sem).start()
            return carry

        def fill_wait(b, carry):
            _block_copy(zero_ref, xg_ref, b, sem).wait()
            return carry

        lax.fori_loop(nused_ref[0], nblk, fill, 0)
        lax.fori_loop(nused_ref[0], nblk, fill_wait, 0)

    @pl.when(i < N_EXPERTS)
    def _():
        n_pad = pad_hi_ref[i] - pad_lo_ref[i]
        for fn in (_start_piece, _wait_piece):
            _run_copies(n_pad, zero_ref, 0, xg_ref, pad_lo_ref[i], sem, MOE_BLOCK - 1, fn)

    bufs = (sorted0_ref, sorted1_ref, sorted2_ref)
    nbuf = len(bufs)

    def runs(tile, b, fn, live=None):
        _tile_runs(tile, n_ref, carry_ref, gstart_ref, bufs[b], xg_ref, run_sem.at[b], tm, fn, True, live)

    def wait_tile(b):
        pltpu.make_async_copy(bufs[b], xg_ref.at[pl.ds(0, rows * ROW_TILES)], run_sem.at[b]).wait()

    def step(cur):
        prev, oldest = (cur - 1) % nbuf, (cur + 1) % nbuf
        cur_ref = bufs[cur]

        @pl.when(i >= nbuf)
        def _():
            wait_tile(cur)

        row_id = lax.broadcasted_iota(jnp.int32, (rows, tm), 0)
        perm = row_id == lp_ref[0:1, :]
        for k in range(1, TOP_K):
            perm = perm | (row_id == lp_ref[k:k + 1, :])
        srt = jnp.dot(jnp.where(perm, 1.0, 0.0).astype(BF16), h_ref[...], preferred_element_type=F32)
        for s in range(ROW_TILES):
            cur_ref[pl.ds(s, rows, stride=ROW_TILES), :] = srt[:, s * LANES:(s + 1) * LANES]

        runs(jnp.maximum(i - 1, 0), prev, _start_piece, live=i >= 1)

        @pl.when(i == last)
        def _():
            @pl.when(i >= 2)
            def _():
                wait_tile(oldest)

            @pl.when(i >= 1)
            def _():
                wait_tile(prev)

            runs(i, cur, _start_piece)
            wait_tile(cur)

    for r in range(nbuf):
        @pl.when(i % nbuf == r)
        def _():
            step(r)


def _dispatch(tile_n, tile_carry, group_start, pad_lo, pad_hi, nused, lp_t, h, cap, tm):
    t, d = h.shape
    assert t // tm >= N_EXPERTS
    grid_spec = pltpu.PrefetchScalarGridSpec(
        num_scalar_prefetch=6,
        grid=(t // tm,),
        in_specs=[
            pl.BlockSpec((TOP_K, tm), lambda i, *_: (0, i)),
            pl.BlockSpec((tm, d), lambda i, *_: (i, 0)),
        ],
        out_specs=pl.BlockSpec(memory_space=pl.ANY),
        scratch_shapes=[
            pltpu.VMEM((TOP_K * tm * ROW_TILES, LANES), F32),
            pltpu.VMEM((TOP_K * tm * ROW_TILES, LANES), F32),
            pltpu.VMEM((TOP_K * tm * ROW_TILES, LANES), F32),
            pltpu.VMEM((MOE_BLOCK * ROW_TILES, LANES), F32),
            pltpu.SemaphoreType.DMA((3,)), pltpu.SemaphoreType.DMA,
        ],
    )
    return pl.pallas_call(
        _dispatch_kernel,
        grid_spec=grid_spec,
        out_shape=jax.ShapeDtypeStruct((cap * ROW_TILES, LANES), F32),
        compiler_params=pltpu.CompilerParams(dimension_semantics=("arbitrary",), has_side_effects=True,
                                             vmem_limit_bytes=VMEM_LIMIT),
        name="moe_dispatch",
    )(tile_n, tile_carry, group_start, pad_lo, pad_hi, nused, lp_t, h)


def _weight_copies(w1_hbm, w2_hbm, w1buf, w2buf, sems, layer, e, slot):
    return (pltpu.make_async_copy(w1_hbm.at[layer, e], w1buf.at[slot], sems.at[0, slot]),
            pltpu.make_async_copy(w2_hbm.at[layer, e], w2buf.at[slot], sems.at[1, slot]))


def _expert_kernel(blk_e_ref, nused_ref, next_e_ref, slot_ref, blk_rows_ref, x_ref, w1_hbm, w2_hbm, bg_ref, bl_ref, b2_ref,
                   sel_ref, o_ref, w1buf, w2buf, wg_s, wl_s, w2_s, sems, *, layer):
    blk = pl.program_id(0)
    active = blk < nused_ref[0]
    e = blk_e_ref[blk]
    slot = slot_ref[e]
    new_expert = (blk == 0) | (e != blk_e_ref[jnp.maximum(blk - 1, 0)])
    copies = functools.partial(_weight_copies, w1_hbm, w2_hbm, w1buf, w2buf, sems, layer)

    @pl.when(blk == 0)
    def _():
        for cp in copies(e, slot):
            cp.start()

    @pl.when(active & new_expert)
    def _():
        nxt = next_e_ref[e]

        @pl.when(nxt >= 0)
        def _():
            for cp in copies(nxt, 1 - slot):
                cp.start()

        for cp in copies(e, slot):
            cp.wait()
        half = SPLIT_W // 2
        for j in range(w1buf.shape[2] // SPLIT_W):
            chunk = w1buf[slot, :, j * SPLIT_W:(j + 1) * SPLIT_W].astype(BF16)
            split = jnp.dot(chunk, sel_ref[...], preferred_element_type=F32)
            wg_s[:, j * half:(j + 1) * half] = split[:, :half].astype(BF16)
            wl_s[:, j * half:(j + 1) * half] = split[:, half:].astype(BF16)
        w2_s[...] = w2buf[slot].astype(BF16)

    def mlp(rows):
        x = jnp.concatenate([x_ref[pl.ds(s, rows, stride=ROW_TILES), :]
                             for s in range(ROW_TILES)], axis=1).astype(BF16)
        glu = jnp.dot(x, wg_s[...], preferred_element_type=F32) + bg_ref[0]
        lin = jnp.dot(x, wl_s[...], preferred_element_type=F32) + bl_ref[0]
        glu = jnp.minimum(glu, SWIGLU_LIMIT)
        lin = jnp.clip(lin, -SWIGLU_LIMIT, SWIGLU_LIMIT)
        act = glu * jax.nn.sigmoid(SWIGLU_ALPHA * glu) * (lin + 1.0)
        y = jnp.dot(act.astype(BF16), w2_s[...], preferred_element_type=F32) + b2_ref[0]
        for s in range(ROW_TILES):
            o_ref[pl.ds(s, rows, stride=ROW_TILES), :] = y[:, s * LANES:(s + 1) * LANES]
        if rows < MOE_BLOCK:
            o_ref[pl.ds(rows * ROW_TILES, (MOE_BLOCK - rows) * ROW_TILES), :] = jnp.zeros(
                ((MOE_BLOCK - rows) * ROW_TILES, LANES), o_ref.dtype)

    valid_rows = jnp.where(active, blk_rows_ref[blk], 0)
    for rows in range(MOE_SUB, MOE_BLOCK + 1, MOE_SUB):
        @pl.when((valid_rows > rows - MOE_SUB) & (valid_rows <= rows))
        def _():
            mlp(rows)


def _experts(layer, blk_e, nused, next_e, e_slot, blk_rows, xg_rows, w1_all, w2_all, b_glu, b_lin, b2):
    nblk = blk_e.shape[0]
    d = D_MODEL
    ff = w2_all.shape[2]
    col = jnp.arange(SPLIT_W)
    sel = (jnp.arange(SPLIT_W)[None, :] == ((col % 2) * (SPLIT_W // 2) + col // 2)[:, None]).astype(BF16)

    def blk_map(b, blk_e_ref, nused_ref, *_):
        return (jnp.minimum(b, nused_ref[0] - 1), 0)

    def e_map(b, blk_e_ref, nused_ref, *_):
        return (blk_e_ref[jnp.minimum(b, nused_ref[0] - 1)], 0, 0)

    grid_spec = pltpu.PrefetchScalarGridSpec(
        num_scalar_prefetch=5,
        grid=(nblk,),
        in_specs=[
            pl.BlockSpec((MOE_BLOCK * ROW_TILES, LANES), blk_map),
            pl.BlockSpec(memory_space=pl.ANY),
            pl.BlockSpec(memory_space=pl.ANY),
            pl.BlockSpec((1, 1, ff), e_map),
            pl.BlockSpec((1, 1, ff), e_map),
            pl.BlockSpec((1, 1, d), e_map),
            pl.BlockSpec((SPLIT_W, SPLIT_W), lambda b, *_: (0, 0)),
        ],
        out_specs=pl.BlockSpec((MOE_BLOCK * ROW_TILES, LANES), blk_map),
        scratch_shapes=[
            pltpu.VMEM((2, d, 2 * ff), F32), pltpu.VMEM((2, ff, d), F32),
            pltpu.VMEM((d, ff), BF16), pltpu.VMEM((d, ff), BF16), pltpu.VMEM((ff, d), BF16),
            pltpu.SemaphoreType.DMA((2, 2)),
        ],
    )
    return pl.pallas_call(
        functools.partial(_expert_kernel, layer=layer),
        grid_spec=grid_spec,
        out_shape=jax.ShapeDtypeStruct(xg_rows.shape, F32),
        input_output_aliases={5: 0},
        compiler_params=_cparams("arbitrary"),
        name="moe_experts",
    )(blk_e, nused, next_e, e_slot, blk_rows, xg_rows, w1_all, w2_all, b_glu, b_lin, b2, sel)


def _combine_kernel(n_ref, carry_ref, gstart_ref, yg_ref, lp_ref, gate_ref, x_ref, g2_ref, o_ref, sorted_ref, sem):
    i = pl.program_id(0)
    ntiles = pl.num_programs(0)
    tm = x_ref.shape[0]
    rows = TOP_K * tm
    slot = i % 2

    def runs(tile, buf, fn, live=None):
        _tile_runs(tile, n_ref, carry_ref, gstart_ref, sorted_ref.at[buf], yg_ref, sem.at[buf], tm, fn, False, live)

    @pl.when(i == 0)
    def _():
        runs(0, 0, _start_piece)

    runs(jnp.minimum(i + 1, ntiles - 1), 1 - slot, _start_piece, live=i + 1 < ntiles)

    eye = (lax.broadcasted_iota(jnp.int32, (tm, tm), 0) ==
           lax.broadcasted_iota(jnp.int32, (tm, tm), 1)).astype(F32)
    rows_t = jnp.concatenate([gate_ref[...], lp_ref[...].astype(F32)], axis=0)
    cols = lax.dot_general(eye, rows_t, (((1,), (1,)), ((), ())), preferred_element_type=F32, precision=HI)
    col_id = lax.broadcasted_iota(jnp.int32, (tm, rows), 1)
    weights = jnp.zeros((tm, rows), F32)
    for k in range(TOP_K):
        lp_col = (cols[:, TOP_K + k:TOP_K + k + 1] + 0.5).astype(jnp.int32)
        weights = weights + jnp.where(col_id == lp_col, cols[:, k:k + 1], 0.0)

    pltpu.make_async_copy(yg_ref.at[pl.ds(0, rows * ROW_TILES)], sorted_ref.at[slot], sem.at[slot]).wait()
    y = jnp.concatenate(
        [sorted_ref[slot, pl.ds(s, rows, stride=ROW_TILES), :] for s in range(ROW_TILES)], axis=1).astype(BF16)
    acc = jnp.dot(weights.astype(BF16), y, preferred_element_type=F32)
    o_ref[...] = x_ref[...] + g2_ref[0] * acc


def _combine(tile_n, tile_carry, group_start, yg_rows, lp_t, gate_t, x, gate2, seq, tm):
    t, d = x.shape
    per_b = seq // tm
    grid_spec = pltpu.PrefetchScalarGridSpec(
        num_scalar_prefetch=3,
        grid=(t // tm,),
        in_specs=[
            pl.BlockSpec(memory_space=pl.ANY),
            pl.BlockSpec((TOP_K, tm), lambda i, *_: (0, i)),
            pl.BlockSpec((TOP_K, tm), lambda i, *_: (0, i)),
            pl.BlockSpec((tm, d), lambda i, *_: (i, 0)),
            pl.BlockSpec((1, 1, d), lambda i, *_: (i // per_b, 0, 0)),
        ],
        out_specs=pl.BlockSpec((tm, d), lambda i, *_: (i, 0)),
        scratch_shapes=[pltpu.VMEM((2, TOP_K * tm * ROW_TILES, LANES), F32), pltpu.SemaphoreType.DMA((2,))],
    )
    return pl.pallas_call(
        _combine_kernel,
        grid_spec=grid_spec,
        out_shape=jax.ShapeDtypeStruct((t, d), F32),
        compiler_params=_cparams("arbitrary"),
        name="moe_combine",
    )(tile_n, tile_carry, group_start, yg_rows, lp_t, gate_t, x, gate2)


def _moe(layer, x, g, shift, scale, gate2, r_w, r_b, w1_all, b1_glu, b1_lin, w2_all, b2, seq):
    t = x.shape[0]
    h, lp_t, gate_t, tile_n, tile_carry, counts = _router(x, g, shift, scale, r_w, r_b, seq, MOE_TILE)
    tile_n = tile_n.reshape(-1)
    tile_carry = tile_carry.reshape(-1)
    counts = counts.reshape(-1)
    padded = ((counts + MOE_BLOCK - 1) // MOE_BLOCK) * MOE_BLOCK
    group_end = jnp.cumsum(padded)
    group_start = (group_end - padded).astype(jnp.int32)
    cap = t * TOP_K + N_EXPERTS * MOE_BLOCK
    nblk = cap // MOE_BLOCK
    blk_start = jnp.arange(nblk, dtype=jnp.int32) * MOE_BLOCK
    e_ids = jnp.arange(N_EXPERTS, dtype=jnp.int32)
    past = (blk_start[None, :] >= group_end[:, None]).astype(jnp.int32)
    blk_e = jnp.minimum(jnp.sum(past, axis=0), N_EXPERTS - 1).astype(jnp.int32)
    nused = (group_end[-1:] // MOE_BLOCK).astype(jnp.int32)

    pad_lo = (group_start + counts).astype(jnp.int32)
    pad_hi = group_end.astype(jnp.int32)
    blk_pad_lo = jnp.sum(jnp.where(blk_e[None, :] == e_ids[:, None], pad_lo[:, None], 0), axis=0)
    blk_rows = jnp.clip(blk_pad_lo - blk_start, 0, MOE_BLOCK).astype(jnp.int32)
    xg_rows = _dispatch(tile_n, tile_carry, group_start, pad_lo, pad_hi, nused, lp_t, h, cap, MOE_TILE)
    nonempty = counts > 0
    later = jnp.where(nonempty[None, :] & (e_ids[None, :] > e_ids[:, None]), e_ids[None, :], N_EXPERTS)
    next_e = jnp.min(later, axis=1)
    next_e = jnp.where(next_e == N_EXPERTS, -1, next_e).astype(jnp.int32)
    e_slot = ((jnp.cumsum(nonempty.astype(jnp.int32)) - 1) % 2).astype(jnp.int32)
    yg_rows = _experts(layer, blk_e, nused, next_e, e_slot, blk_rows, xg_rows, w1_all, w2_all,
                       b1_glu[:, None, :], b1_lin[:, None, :], b2[:, None, :])
    return _combine(tile_n, tile_carry, group_start, yg_rows, lp_t, gate_t, x, gate2, seq, MOE_TILE)


def kernel(x, c, ada_w, ada_b, norm1_g, norm2_g, m_in_w, m_conv_w, m_conv_b, m_dt_bias, m_A_log, m_D, m_norm_g, m_out_w, a_qkv_w, a_q_norm_g, a_k_norm_g, a_sinks, a_out_w, rel_bias, r_w, r_b, e_w1, e_b1, e_w2, e_b2):
    batch, seq, d = x.shape
    depth = ada_w.shape[0]
    t = batch * seq
    xf = x.reshape(t, d)

    mod = _adaln(c, ada_w, ada_b)[:, :batch]

    b1_split = jnp.moveaxis(e_b1.reshape(depth, N_EXPERTS, -1, 2), -1, 1)

    for i in range(depth):
        parts =[mod[i, :, p * d:(p + 1) * d].reshape(batch, 1, d) for p in range(6)]
        sh1, sc1, g1, sh2, sc2, g2 = parts
        j = i // 2
        if i % 2 == 0:
            w_zxbc = m_in_w[j, :, :SSM_ZXBC].astype(BF16)
            w_dt = jnp.pad(m_in_w[j, :, SSM_ZXBC:], ((0, 0), (0, LANES - SSM_HEADS)))
            zxbc, dt_raw = _norm_matmul(xf, norm1_g[i], sh1, sc1, (w_zxbc, w_dt), seq)
            y = _ssd_mixer(zxbc, dt_raw, m_conv_w[j], m_conv_b[j], m_dt_bias[j], m_A_log[j], m_D[j],
                           m_norm_g[j], batch, seq)
            xf = _matmul_residual(y, m_out_w[j].astype(BF16), xf, g1, seq)
        else:
            qkv, = _norm_matmul(xf, norm1_g[i], sh1, sc1, (a_qkv_w[j].astype(BF16),), seq)
            y = _swa_mixer(qkv, a_q_norm_g[j], a_k_norm_g[j], a_sinks[j], rel_bias, batch, seq)
            xf = _matmul_residual(y, a_out_w[j].astype(BF16), xf, g1, seq)
        xf = _moe(i, xf, norm2_g[i], sh2, sc2, g2, r_w[i], r_b[i], e_w1, b1_split[i, 0], b1_split[i, 1],
                  e_w2, e_b2[i], seq)
    return xf.reshape(batch, seq, d)
```

```python
import functools
import math

import jax
import jax.numpy as jnp
from jax import lax
from jax.experimental import pallas as pl
from jax.experimental.pallas import tpu as pltpu

D_MODEL = 1024
EPS = 1e-6
LANES = 128
SUBLANES = 8
ROW_TILES = D_MODEL // LANES

SSM_D_INNER = 2048
SSM_HEAD_DIM = 64
SSM_HEADS = 32
SSM_GROUPS = 4
SSM_STATE = 128
SSM_CONV = 4
SSM_CHUNK = 128
SSM_STEP_CHUNKS = 1
SSM_GN = SSM_GROUPS * SSM_STATE
SSM_ZXBC = 2 * SSM_D_INNER + 2 * SSM_GN
SSM_GROUP_W = SSM_D_INNER // SSM_GROUPS

ATTN_HEAD_DIM = 64
ATTN_Q_HEADS = 16
ATTN_KV_HEADS = 4
ATTN_Q_PER_KV = 4
WINDOW = 128
SWA_STEP_BLOCKS = 4
REL_BUCKETS = 32
REL_MAX_DIST = 128

N_EXPERTS = 32
TOP_K = 4
SWIGLU_ALPHA = 1.702
SWIGLU_LIMIT = 7.0
MOE_BLOCK = 512
MOE_SUB = 128
MOE_TILE = 256
SPLIT_W = 256

VMEM_LIMIT = 56 * 1024 * 1024
HI = lax.Precision.HIGHEST
F32 = jnp.float32
BF16 = jnp.bfloat16


def _cparams(*sem):
    return pltpu.CompilerParams(dimension_semantics=sem, vmem_limit_bytes=VMEM_LIMIT)


def _norm_modulate(x, g, shift, scale):
    ms = jnp.mean(x * x, axis=-1, keepdims=True)
    return x * lax.rsqrt(ms + EPS) * g * (1.0 + scale) + shift


def _adaln_kernel(ct_ref, w_ref, b_ref, o_ref, *, batch):
    c_t = ct_ref[...]
    act_t = c_t * jax.nn.sigmoid(c_t)
    w = w_ref[0]
    rows = [jnp.sum(act_t[:, b:b + 1] * w, axis=0, keepdims=True) + b_ref[0] for b in range(batch)]
    rows.append(jnp.zeros((SUBLANES - batch, w.shape[1]), F32))
    o_ref[0] = jnp.concatenate(rows, axis=0)


def _adaln(c, ada_w, ada_b):
    depth, d, n = ada_w.shape
    batch = c.shape[0]
    tn = 1536
    c_t = jnp.zeros((d, SUBLANES), F32).at[:, :batch].set(c.T)
    return pl.pallas_call(
        functools.partial(_adaln_kernel, batch=batch),
        grid=(depth, n // tn),
        in_specs=[
            pl.BlockSpec((d, SUBLANES), lambda i, j: (0, 0)),
            pl.BlockSpec((1, d, tn), lambda i, j: (i, 0, j)),
            pl.BlockSpec((1, 1, tn), lambda i, j: (i, 0, j)),
        ],
        out_specs=pl.BlockSpec((1, SUBLANES, tn), lambda i, j: (i, 0, j)),
        out_shape=jax.ShapeDtypeStruct((depth, SUBLANES, n), F32),
        compiler_params=_cparams("arbitrary", "arbitrary"),
        name="adaln",
    )(c_t, ada_w, ada_b.reshape(depth, 1, n))


def _norm_matmul_kernel(x_ref, g_ref, sh_ref, sc_ref, *refs):
    w_refs, o_refs = refs[:len(refs) // 2], refs[len(refs) // 2:]
    h = _norm_modulate(x_ref[...], g_ref[...], sh_ref[0], sc_ref[0]).astype(BF16)
    for w_ref, o_ref in zip(w_refs, o_refs):
        o_ref[...] = jnp.dot(h, w_ref[...].astype(BF16), preferred_element_type=F32)


def _norm_matmul(x, g, shift, scale, weights_bf16, seq, tm=512):
    t, d = x.shape
    per_b = seq // tm
    return pl.pallas_call(
        _norm_matmul_kernel,
        grid=(t // tm,),
        in_specs=[
            pl.BlockSpec((tm, d), lambda i: (i, 0)),
            pl.BlockSpec((1, d), lambda i: (0, 0)),
            pl.BlockSpec((1, 1, d), lambda i: (i // per_b, 0, 0)),
            pl.BlockSpec((1, 1, d), lambda i: (i // per_b, 0, 0)),
        ] + [pl.BlockSpec(w.shape, lambda i: (0, 0), pipeline_mode=pl.Buffered(1))
             for w in weights_bf16],
        out_specs=[pl.BlockSpec((tm, w.shape[1]), lambda i: (i, 0)) for w in weights_bf16],
        out_shape=[jax.ShapeDtypeStruct((t, w.shape[1]), F32) for w in weights_bf16],
        compiler_params=_cparams("arbitrary"),
        name="norm_matmul",
    )(x, g.reshape(1, d), shift, scale, *weights_bf16)


def _matmul_residual_kernel(y_ref, w_ref, x_ref, gate_ref, o_ref):
    acc = jnp.dot(y_ref[...], w_ref[...], preferred_element_type=F32)
    o_ref[...] = x_ref[...] + gate_ref[0] * acc


def _matmul_residual(y_bf16, w_bf16, x, gate, seq, tm=1024):
    t, k = y_bf16.shape
    d = x.shape[1]
    per_b = seq // tm
    return pl.pallas_call(
        _matmul_residual_kernel,
        grid=(t // tm,),
        in_specs=[
            pl.BlockSpec((tm, k), lambda i: (i, 0)),
            pl.BlockSpec((k, d), lambda i: (0, 0)),
            pl.BlockSpec((tm, d), lambda i: (i, 0)),
            pl.BlockSpec((1, 1, d), lambda i: (i // per_b, 0, 0)),
        ],
        out_specs=pl.BlockSpec((tm, d), lambda i: (i, 0)),
        out_shape=jax.ShapeDtypeStruct((t, d), F32),
        compiler_params=_cparams("arbitrary"),
        name="matmul_residual",
    )(y_bf16, w_bf16, x, gate)


def _causal_conv_silu(cur, prev_tail, w, b):
    rows = lax.broadcasted_iota(jnp.int32, (SUBLANES, cur.shape[1]), 0)
    acc = b + w[SSM_CONV - 1:SSM_CONV] * cur
    for d in range(1, SSM_CONV):
        rolled = pltpu.roll(cur, d, axis=0)
        top = jnp.where(rows < d, pltpu.roll(prev_tail, d, axis=0), rolled[0:SUBLANES])
        shifted = jnp.concatenate([top, rolled[SUBLANES:]], axis=0)
        acc = acc + w[SSM_CONV - 1 - d:SSM_CONV - d] * shifted
    return acc * jax.nn.sigmoid(acc)


def _split3(x):
    hi = x.astype(BF16)
    rem = x - hi.astype(F32)
    mid = rem.astype(BF16)
    lo = (rem - mid.astype(F32)).astype(BF16)
    return jnp.concatenate([hi, mid, lo], axis=1)


def _ssd_kernel(z_ref, xs_ref, bc_ref, dt_ref, *refs):
    o_ref, tailx_ref, tailb_ref, state_ref = refs[-4:]

    @pl.when(pl.program_id(1) == 0)
    def _():
        tailx_ref[...] = jnp.zeros_like(tailx_ref)
        tailb_ref[...] = jnp.zeros_like(tailb_ref)
        state_ref[...] = jnp.zeros_like(state_ref)

    for u in range(SSM_STEP_CHUNKS):
        rows = pl.ds(u * SSM_CHUNK, SSM_CHUNK)
        _ssd_chunk(z_ref.at[rows], xs_ref.at[rows], bc_ref.at[rows], dt_ref.at[rows], *refs[:-4],
                   o_ref.at[rows], tailx_ref, tailb_ref, state_ref)


def _ssd_chunk(z_ref, xs_ref, bc_ref, dt_ref, cwx_ref, cbx_ref, cwb_ref, cbb_ref, dtb_ref, alog_ref,
               dskip_ref, ng_ref, hexp_ref, lexp_ref, o_ref, tailx_ref, tailb_ref, state_ref):
    xs_raw = xs_ref[...]
    bc_raw = bc_ref[...]
    xs = _causal_conv_silu(xs_raw, tailx_ref[...], cwx_ref[...], cbx_ref[...])
    bc = _causal_conv_silu(bc_raw, tailb_ref[...], cwb_ref[...], cbb_ref[...])
    tailx_ref[...] = xs_raw[SSM_CHUNK - SUBLANES:]
    tailb_ref[...] = bc_raw[SSM_CHUNK - SUBLANES:]

    dt_in = dt_ref[...][:, :SSM_HEADS] + dtb_ref[...]
    dt = jnp.maximum(dt_in, 0.0) + jnp.log1p(jnp.exp(-jnp.abs(dt_in)))
    a_neg = -jnp.exp(alog_ref[...])
    d_a = dt * a_neg
    li = lax.broadcasted_iota(jnp.int32, (SSM_CHUNK, SSM_CHUNK), 0)
    si = lax.broadcasted_iota(jnp.int32, (SSM_CHUNK, SSM_CHUNK), 1)
    causal = li >= si
    tri = causal.astype(F32)
    a_cs = jnp.dot(tri, d_a, preferred_element_type=F32, precision=HI)
    a_cs_t = lax.dot_general(d_a, tri, (((0,), (1,)), ((), ())),
                             preferred_element_type=F32, precision=HI)
    a_last = a_cs[SSM_CHUNK - 1:SSM_CHUNK]
    e_out = jnp.exp(a_cs)
    e_state = jnp.exp(a_last - a_cs) * dt
    small = jnp.concatenate([dt, e_out, e_state], axis=0)
    wide = jnp.dot(_split3(small), hexp_ref[...], preferred_element_type=F32)
    dt_w = wide[0:SSM_CHUNK]
    e_out_w = wide[SSM_CHUNK:2 * SSM_CHUNK]
    e_state_w = wide[2 * SSM_CHUNK:]
    a_col = jnp.dot(_split3(a_cs), lexp_ref[...], preferred_element_type=F32)

    x_dt = (xs * dt_w).astype(BF16)
    x_state = (xs * e_state_w).astype(BF16)
    chunk_decay_w = e_out_w[SSM_CHUNK - 1:SSM_CHUNK]

    heads_per_group = SSM_HEADS // SSM_GROUPS
    y_parts = []
    for g in range(SSM_GROUPS):
        b_g = bc[:, g * SSM_STATE:(g + 1) * SSM_STATE].astype(BF16)
        c_g = bc[:, SSM_GN + g * SSM_STATE:SSM_GN + (g + 1) * SSM_STATE].astype(BF16)
        cb = lax.dot_general(c_g, b_g, (((1,), (1,)), ((), ())), preferred_element_type=F32)
        gsl = slice(g * SSM_GROUP_W, (g + 1) * SSM_GROUP_W)
        h_prev = state_ref[g]
        y_off = jnp.dot(c_g, h_prev.astype(BF16), preferred_element_type=F32) * e_out_w[:, gsl]
        diag = []
        for r in range(heads_per_group):
            h = g * heads_per_group + r
            seg = a_col[:, h * SSM_CHUNK:(h + 1) * SSM_CHUNK] - a_cs_t[h:h + 1, :]
            decay = jnp.exp(jnp.where(causal, seg, -jnp.inf))
            m = (cb * decay).astype(BF16)
            diag.append(jnp.dot(m, x_dt[:, h * SSM_HEAD_DIM:(h + 1) * SSM_HEAD_DIM],
                                preferred_element_type=F32))
        y_parts.append(jnp.concatenate(diag, axis=1) + y_off)
        upd = lax.dot_general(b_g, x_state[:, gsl], (((0,), (0,)), ((), ())), preferred_element_type=F32)
        state_ref[g] = h_prev * chunk_decay_w[:, gsl] + upd

    y = jnp.concatenate(y_parts, axis=1) + dskip_ref[...] * xs
    z = z_ref[...]
    y = y * (z * jax.nn.sigmoid(z))
    normed = []
    for g in range(SSM_GROUPS):
        y_g = y[:, g * SSM_GROUP_W:(g + 1) * SSM_GROUP_W]
        normed.append(y_g * lax.rsqrt(jnp.mean(y_g * y_g, axis=-1, keepdims=True) + EPS))
    o_ref[...] = (jnp.concatenate(normed, axis=1) * ng_ref[...]).astype(o_ref.dtype)


def _ssd_mixer(zxbc, dt_raw, conv_w, conv_b, dt_bias, a_log, d_skip, norm_g, batch, seq):
    t = zxbc.shape[0]
    step_rows = SSM_CHUNK * SSM_STEP_CHUNKS
    nc = seq // step_rows
    head_expand = jnp.tile(jnp.repeat(jnp.eye(SSM_HEADS, dtype=BF16), SSM_HEAD_DIM, axis=1), (3, 1))
    lane_expand = jnp.tile(jnp.repeat(jnp.eye(SSM_HEADS, dtype=BF16), SSM_CHUNK, axis=1), (3, 1))
    row = lambda b, c: (b * nc + c, 0)
    const2 = lambda b, c: (0, 0)
    bc_w = 2 * SSM_GN
    return pl.pallas_call(
        _ssd_kernel,
        grid=(batch, nc),
        in_specs=[
            pl.BlockSpec((step_rows, SSM_D_INNER), row),
            pl.BlockSpec((step_rows, SSM_D_INNER), lambda b, c: (b * nc + c, 1)),
            pl.BlockSpec((step_rows, bc_w), lambda b, c: (b * nc + c, 2 * SSM_D_INNER // bc_w)),
            pl.BlockSpec((step_rows, LANES), row),
            pl.BlockSpec((SSM_CONV, SSM_D_INNER), const2),
            pl.BlockSpec((1, SSM_D_INNER), const2),
            pl.BlockSpec((SSM_CONV, bc_w), const2),
            pl.BlockSpec((1, bc_w), const2),
            pl.BlockSpec((1, SSM_HEADS), const2),
            pl.BlockSpec((1, SSM_HEADS), const2),
            pl.BlockSpec((1, SSM_D_INNER), const2),
            pl.BlockSpec((1, SSM_D_INNER), const2),
            pl.BlockSpec((3 * SSM_HEADS, SSM_D_INNER), const2),
            pl.BlockSpec((3 * SSM_HEADS, SSM_HEADS * SSM_CHUNK), const2),
        ],
        out_specs=pl.BlockSpec((step_rows, SSM_D_INNER), row),
        out_shape=jax.ShapeDtypeStruct((t, SSM_D_INNER), BF16),
        scratch_shapes=[
            pltpu.VMEM((SUBLANES, SSM_D_INNER), F32),
            pltpu.VMEM((SUBLANES, bc_w), F32),
            pltpu.VMEM((SSM_GROUPS, SSM_STATE, SSM_GROUP_W), F32),
        ],
        compiler_params=_cparams("arbitrary", "arbitrary"),
        name="ssd_mixer",
    )(zxbc, zxbc, zxbc, dt_raw,
      conv_w[:, :SSM_D_INNER], conv_b[:SSM_D_INNER].reshape(1, -1),
      conv_w[:, SSM_D_INNER:], conv_b[SSM_D_INNER:].reshape(1, -1),
      dt_bias.reshape(1, -1), a_log.reshape(1, -1),
      jnp.repeat(d_skip, SSM_HEAD_DIM).reshape(1, -1), norm_g.reshape(1, -1),
      head_expand, lane_expand)


def _head_rms(x, g):
    return x * lax.rsqrt(jnp.mean(x * x, axis=-1, keepdims=True) + EPS) * g


def _swa_kernel(q_ref, kvc_ref, kvp_ref, bucket_ref, qg_ref, kg_ref, rel_ref, sink_ref, o_ref,
                bias_ref, sinkrow_ref):
    b = pl.program_id(0)
    i = pl.program_id(1)

    @pl.when((b == 0) & (i == 0))
    def _():
        bucket = bucket_ref[...]
        kj = lax.broadcasted_iota(jnp.int32, (2 * WINDOW, WINDOW), 0)
        qi = lax.broadcasted_iota(jnp.int32, (2 * WINDOW, WINDOW), 1)
        dist = qi + WINDOW - kj
        band = (dist >= 0) & (dist < WINDOW)
        for h in range(ATTN_Q_HEADS):
            g, r = divmod(h, ATTN_Q_PER_KV)
            acc = jnp.zeros(bucket.shape, F32)
            for k in range(REL_BUCKETS):
                acc = jnp.where(bucket == k, rel_ref[k, h], acc)
            cols = slice(r * WINDOW, (r + 1) * WINDOW)
            bias_ref[0, g, :, cols] = jnp.where(band, acc, -jnp.inf)
            bias_ref[1, g, :, cols] = jnp.where(band & (kj >= WINDOW), acc, -jnp.inf)
            sinkrow_ref[g, :, cols] = jnp.full((1, WINDOW), sink_ref[h], F32)

    for u in range(SWA_STEP_BLOCKS):
        rows = slice(u * WINDOW, (u + 1) * WINDOW)
        kv_p = kvp_ref[...] if u == 0 else kvc_ref[(u - 1) * WINDOW:u * WINDOW, :]
        first = (i == 0).astype(jnp.int32) if u == 0 else 0
        o_ref[rows, :] = _swa_block(q_ref[rows, :], kvc_ref[rows, :], kv_p, first,
                                    qg_ref, kg_ref, bias_ref, sinkrow_ref).astype(o_ref.dtype)


def _swa_block(q_rows, kv_c, kv_p, first, qg_ref, kg_ref, bias_ref, sinkrow_ref):
    q_t = q_rows.T
    kv_w = ATTN_KV_HEADS * ATTN_HEAD_DIM
    q_gain = qg_ref[...]
    outs = []
    for g in range(ATTN_KV_HEADS):
        ksl = slice(g * ATTN_HEAD_DIM, (g + 1) * ATTN_HEAD_DIM)
        vsl = slice(kv_w + g * ATTN_HEAD_DIM, kv_w + (g + 1) * ATTN_HEAD_DIM)
        k = jnp.concatenate([kv_p[:, ksl], kv_c[:, ksl]], axis=0)
        v = jnp.concatenate([kv_p[:, vsl], kv_c[:, vsl]], axis=0).astype(BF16)
        k = _head_rms(k, kg_ref[...]).astype(BF16)
        q_heads = []
        for r in range(ATTN_Q_PER_KV):
            h = g * ATTN_Q_PER_KV + r
            q_h = q_t[h * ATTN_HEAD_DIM:(h + 1) * ATTN_HEAD_DIM]
            inv = lax.rsqrt(jnp.mean(q_h * q_h, axis=0, keepdims=True) + EPS)
            q_heads.append((q_h * inv * q_gain).astype(BF16))
        q = jnp.concatenate(q_heads, axis=1)
        s = jnp.dot(k, q, preferred_element_type=F32) + bias_ref[first, g]
        sink = sinkrow_ref[g]
        m = jnp.maximum(jnp.max(s, axis=0, keepdims=True), sink)
        p = jnp.exp(s - m)
        denom = jnp.sum(p, axis=0, keepdims=True) + jnp.exp(sink - m)
        pv = lax.dot_general(v, p.astype(BF16), (((0,), (0,)), ((), ())), preferred_element_type=F32)
        pv = pv * (1.0 / denom)
        outs.extend(pv[:, r * WINDOW:(r + 1) * WINDOW] for r in range(ATTN_Q_PER_KV))
    return jnp.concatenate(outs, axis=0).T


def _t5_causal_bucket(dist):
    max_exact = REL_BUCKETS // 2
    d = jnp.maximum(dist, 1).astype(F32)
    large = max_exact + (jnp.log(d / max_exact) / math.log(REL_MAX_DIST / max_exact)
                         * (REL_BUCKETS - max_exact)).astype(jnp.int32)
    large = jnp.minimum(large, REL_BUCKETS - 1)
    return jnp.where(dist < max_exact, dist, large)


def _swa_mixer(qkv, q_norm_g, k_norm_g, sinks, rel_bias, batch, seq):
    t = qkv.shape[0]
    nb = seq // WINDOW
    step_rows = WINDOW * SWA_STEP_BLOCKS
    ns = seq // step_rows
    q_w = ATTN_Q_HEADS * ATTN_HEAD_DIM
    kv_w2 = 2 * ATTN_KV_HEADS * ATTN_HEAD_DIM
    kj = jnp.arange(2 * WINDOW)[:, None]
    qi = jnp.arange(WINDOW)[None, :]
    bucket = _t5_causal_bucket(jnp.maximum(qi + WINDOW - kj, 0)).astype(jnp.int32)
    q_gain = jnp.broadcast_to((q_norm_g * (ATTN_HEAD_DIM ** -0.5))[:, None], (ATTN_HEAD_DIM, WINDOW))
    const2 = lambda b, i: (0, 0)
    smem = pl.BlockSpec(memory_space=pltpu.SMEM)
    return pl.pallas_call(
        _swa_kernel,
        grid=(batch, ns),
        in_specs=[
            pl.BlockSpec((step_rows, q_w), lambda b, i: (b * ns + i, 0)),
            pl.BlockSpec((step_rows, kv_w2), lambda b, i: (b * ns + i, q_w // kv_w2)),
            pl.BlockSpec((WINDOW, kv_w2),
                         lambda b, i: (b * nb + jnp.maximum(i * SWA_STEP_BLOCKS - 1, 0), q_w // kv_w2)),
            pl.BlockSpec((2 * WINDOW, WINDOW), const2),
            pl.BlockSpec((ATTN_HEAD_DIM, WINDOW), const2),
            pl.BlockSpec((1, ATTN_HEAD_DIM), const2),
            smem,
            smem,
        ],
        out_specs=pl.BlockSpec((step_rows, q_w), lambda b, i: (b * ns + i, 0)),
        out_shape=jax.ShapeDtypeStruct((t, q_w), BF16),
        scratch_shapes=[pltpu.VMEM((2, ATTN_KV_HEADS, 2 * WINDOW, ATTN_Q_PER_KV * WINDOW), F32),
                        pltpu.VMEM((ATTN_KV_HEADS, 1, ATTN_Q_PER_KV * WINDOW), F32)],
        compiler_params=_cparams("arbitrary", "arbitrary"),
        name="swa_mixer",
    )(qkv, qkv, qkv, bucket, q_gain, k_norm_g.reshape(1, -1), rel_bias, sinks)


def _router_kernel(x_ref, g_ref, sh_ref, sc_ref, rwt_ref, rb_ref,
                   h_ref, lp_ref, gate_ref, tile_n_ref, tile_carry_ref, cnt_ref, carry_ref):
    i = pl.program_id(0)
    tm = x_ref.shape[0]

    @pl.when(i == 0)
    def _():
        carry_ref[...] = jnp.zeros_like(carry_ref)

    h = _norm_modulate(x_ref[...], g_ref[...], sh_ref[0], sc_ref[0]).astype(BF16)
    h_ref[...] = h

    logits = lax.dot_general(rwt_ref[...], h, (((1,), (1,)), ((), ())),
                             preferred_element_type=F32) + rb_ref[...]
    e_iota = lax.broadcasted_iota(jnp.int32, logits.shape, 0)
    work = logits
    sels, vals = [], []
    for k in range(TOP_K):
        m = jnp.max(work, axis=0, keepdims=True)
        idx = jnp.min(jnp.where(work == m, e_iota, N_EXPERTS), axis=0, keepdims=True)
        sel = e_iota == idx
        work = jnp.where(sel, -jnp.inf, work)
        sels.append(sel)
        vals.append(m)
    exps = [jnp.exp(v - vals[0]) for v in vals]
    denom = exps[0] + exps[1] + exps[2] + exps[3]
    for k in range(TOP_K):
        gate_ref[k:k + 1, :] = exps[k] / denom

    chosen = sels[0] | sels[1] | sels[2] | sels[3]
    t_row = lax.broadcasted_iota(jnp.int32, (tm, tm), 0)
    t_col = lax.broadcasted_iota(jnp.int32, (tm, tm), 1)
    before = (t_row < t_col).astype(BF16)
    prior = jnp.dot(chosen.astype(BF16), before, preferred_element_type=F32)
    n = jnp.sum(chosen.astype(F32), axis=1, keepdims=True)
    e_row = lax.broadcasted_iota(jnp.int32, (N_EXPERTS, N_EXPERTS), 0)
    e_col = lax.broadcasted_iota(jnp.int32, (N_EXPERTS, N_EXPERTS), 1)
    run_start = jnp.dot((e_col < e_row).astype(BF16), jnp.broadcast_to(n, (N_EXPERTS, LANES)).astype(BF16),
                        preferred_element_type=F32)[:, :1]
    local = run_start + prior
    for k in range(TOP_K):
        lp_ref[k:k + 1, :] = jnp.sum(jnp.where(sels[k], local, 0.0), axis=0, keepdims=True).astype(jnp.int32)
    tile_n_ref[0] = n.astype(jnp.int32)
    tile_carry_ref[0] = carry_ref[...].astype(jnp.int32)
    total = carry_ref[...] + n
    carry_ref[...] = total
    cnt_ref[...] = total.astype(jnp.int32)


def _router(x, g, shift, scale, r_w, r_b, seq, tm):
    t, d = x.shape
    per_b = seq // tm
    ntiles = t // tm
    return pl.pallas_call(
        _router_kernel,
        grid=(t // tm,),
        in_specs=[
            pl.BlockSpec((tm, d), lambda i: (i, 0)),
            pl.BlockSpec((1, d), lambda i: (0, 0)),
            pl.BlockSpec((1, 1, d), lambda i: (i // per_b, 0, 0)),
            pl.BlockSpec((1, 1, d), lambda i: (i // per_b, 0, 0)),
            pl.BlockSpec((N_EXPERTS, d), lambda i: (0, 0)),
            pl.BlockSpec((N_EXPERTS, 1), lambda i: (0, 0)),
        ],
        out_specs=[
            pl.BlockSpec((tm, d), lambda i: (i, 0)),
            pl.BlockSpec((TOP_K, tm), lambda i: (0, i)),
            pl.BlockSpec((TOP_K, tm), lambda i: (0, i)),
            pl.BlockSpec((1, N_EXPERTS, 1), lambda i: (i, 0, 0)),
            pl.BlockSpec((1, N_EXPERTS, 1), lambda i: (i, 0, 0)),
            pl.BlockSpec((N_EXPERTS, 1), lambda i: (0, 0)),
        ],
        out_shape=[
            jax.ShapeDtypeStruct((t, d), BF16),
            jax.ShapeDtypeStruct((TOP_K, t), jnp.int32),
            jax.ShapeDtypeStruct((TOP_K, t), F32),
            jax.ShapeDtypeStruct((ntiles, N_EXPERTS, 1), jnp.int32),
            jax.ShapeDtypeStruct((ntiles, N_EXPERTS, 1), jnp.int32),
            jax.ShapeDtypeStruct((N_EXPERTS, 1), jnp.int32),
        ],
        scratch_shapes=[pltpu.VMEM((N_EXPERTS, 1), F32)],
        compiler_params=_cparams("arbitrary"),
        name="moe_router",
    )(x, g.reshape(1, d), shift, scale, r_w.T.astype(BF16), r_b.reshape(-1, 1))


def _run_copies(n, src_ref, src_row, dst_ref, dst_row, sem, max_rows, fn):
    for b in range(max_rows.bit_length()):
        size = 1 << b

        @pl.when((n & size) != 0)
        def _():
            lo = n & (size - 1)
            fn(pltpu.make_async_copy(
                src_ref.at[pl.ds(pl.multiple_of((src_row + lo) * ROW_TILES, ROW_TILES), size * ROW_TILES)],
                dst_ref.at[pl.ds(pl.multiple_of((dst_row + lo) * ROW_TILES, ROW_TILES), size * ROW_TILES)],
                sem), b)


def _start_piece(cp, b):
    cp.start()


def _wait_piece(cp, b):
    cp.wait()


def _tile_runs(tile, n_ref, carry_ref, gstart_ref, sorted_ref, grouped_ref, sem, tm, fn, to_grouped, live=None):
    def body(e, run_start):
        n = n_ref[tile * N_EXPERTS + e]
        if live is not None:
            n = jnp.where(live, n, 0)
        slot0 = gstart_ref[e] + carry_ref[tile * N_EXPERTS + e]
        if to_grouped:
            _run_copies(n, sorted_ref, run_start, grouped_ref, slot0, sem, tm, fn)
        else:
            _run_copies(n, grouped_ref, slot0, sorted_ref, run_start, sem, tm, fn)
        return run_start + n

    if live is None:
        lax.fori_loop(0, N_EXPERTS, body, 0)
    else:
        run_start = 0
        for e in range(N_EXPERTS):
            run_start = body(e, run_start)


def _block_copy(src_ref, dst_ref, dst_blk, sem):
    rows = MOE_BLOCK * ROW_TILES
    return pltpu.make_async_copy(src_ref, dst_ref.at[pl.ds(pl.multiple_of(dst_blk * rows, rows), rows)], sem)


def _dispatch_kernel(n_ref, carry_ref, gstart_ref, pad_lo_ref, pad_hi_ref, nused_ref, lp_ref, h_ref, xg_ref,
                     sorted0_ref, sorted1_ref, sorted2_ref, zero_ref, run_sem, sem):
    i = pl.program_id(0)
    last = pl.num_programs(0) - 1
    tm = h_ref.shape[0]
    rows = TOP_K * tm
    nblk =xg_ref.shape[0] // (MOE_BLOCK * ROW_TILES)

    @pl.when(i == 0)
    def _():
        zero_ref[...] = jnp.zeros_like(zero_ref)

        def fill(b, carry):
            _block_copy(zero_ref, xg_ref, b, sem).start()
            return carry

        def fill_wait(b, carry):
            _block_copy(zero_ref, xg_ref, b, sem).wait()
            return carry

        lax.fori_loop(nused_ref[0], nblk, fill, 0)
        lax.fori_loop(nused_ref[0], nblk, fill_wait, 0)

    @pl.when(i < N_EXPERTS)
    def _():
        n_pad = pad_hi_ref[i] - pad_lo_ref[i]
        for fn in (_start_piece, _wait_piece):
            _run_copies(n_pad, zero_ref, 0, xg_ref, pad_lo_ref[i], sem, MOE_BLOCK - 1, fn)

    bufs = (sorted0_ref, sorted1_ref, sorted2_ref)
    nbuf = len(bufs)

    def runs(tile, b, fn, live=None):
        _tile_runs(tile, n_ref, carry_ref, gstart_ref, bufs[b], xg_ref, run_sem.at[b], tm, fn, True, live)

    def wait_tile(b):
        pltpu.make_async_copy(bufs[b], xg_ref.at[pl.ds(0, rows * ROW_TILES)], run_sem.at[b]).wait()

    def step(cur):
        prev, oldest = (cur - 1) % nbuf, (cur + 1) % nbuf
        cur_ref = bufs[cur]

        @pl.when(i >= nbuf)
        def _():
            wait_tile(cur)

        row_id = lax.broadcasted_iota(jnp.int32, (rows, tm), 0)
        perm = row_id == lp_ref[0:1, :]
        for k in range(1, TOP_K):
            perm = perm | (row_id == lp_ref[k:k + 1, :])
        srt = jnp.dot(jnp.where(perm, 1.0, 0.0).astype(BF16), h_ref[...], preferred_element_type=F32)
        for s in range(ROW_TILES):
            cur_ref[pl.ds(s, rows, stride=ROW_TILES), :] = srt[:, s * LANES:(s + 1) * LANES]

        runs(jnp.maximum(i - 1, 0), prev, _start_piece, live=i >= 1)

        @pl.when(i == last)
        def _():
            @pl.when(i >= 2)
            def _():
                wait_tile(oldest)

            @pl.when(i >= 1)
            def _():
                wait_tile(prev)

            runs(i, cur, _start_piece)
            wait_tile(cur)

    for r in range(nbuf):
        @pl.when(i % nbuf == r)
        def _():
            step(r)


def _dispatch(tile_n, tile_carry, group_start, pad_lo, pad_hi, nused, lp_t, h, cap, tm):
    t, d = h.shape
    assert t // tm >= N_EXPERTS
    grid_spec = pltpu.PrefetchScalarGridSpec(
        num_scalar_prefetch=6,
        grid=(t // tm,),
        in_specs=[
            pl.BlockSpec((TOP_K, tm), lambda i, *_: (0, i)),
            pl.BlockSpec((tm, d), lambda i, *_: (i, 0)),
        ],
        out_specs=pl.BlockSpec(memory_space=pl.ANY),
        scratch_shapes=[
            pltpu.VMEM((TOP_K * tm * ROW_TILES, LANES), F32),
            pltpu.VMEM((TOP_K * tm * ROW_TILES, LANES), F32),
            pltpu.VMEM((TOP_K * tm * ROW_TILES, LANES), F32),
            pltpu.VMEM((MOE_BLOCK * ROW_TILES, LANES), F32),
            pltpu.SemaphoreType.DMA((3,)), pltpu.SemaphoreType.DMA,
        ],
    )
    return pl.pallas_call(
        _dispatch_kernel,
        grid_spec=grid_spec,
        out_shape=jax.ShapeDtypeStruct((cap * ROW_TILES, LANES), F32),
        compiler_params=pltpu.CompilerParams(dimension_semantics=("arbitrary",), has_side_effects=True,
                                             vmem_limit_bytes=VMEM_LIMIT),
        name="moe_dispatch",
    )(tile_n, tile_carry, group_start, pad_lo, pad_hi, nused, lp_t, h)


def _weight_copies(w1_hbm, w2_hbm, w1buf, w2buf, sems, layer, e, slot):
    return (pltpu.make_async_copy(w1_hbm.at[layer, e], w1buf.at[slot], sems.at[0, slot]),
            pltpu.make_async_copy(w2_hbm.at[layer, e], w2buf.at[slot], sems.at[1, slot]))


def _expert_kernel(blk_e_ref, nused_ref, next_e_ref, slot_ref, blk_rows_ref, x_ref, w1_hbm, w2_hbm, bg_ref, bl_ref, b2_ref,
                   sel_ref, o_ref, w1buf, w2buf, wg_s, wl_s, w2_s, sems, *, layer):
    blk = pl.program_id(0)
    active = blk < nused_ref[0]
    e = blk_e_ref[blk]
    slot = slot_ref[e]
    new_expert = (blk == 0) | (e != blk_e_ref[jnp.maximum(blk - 1, 0)])
    copies = functools.partial(_weight_copies, w1_hbm, w2_hbm, w1buf, w2buf, sems, layer)

    @pl.when(blk == 0)
    def _():
        for cp in copies(e, slot):
            cp.start()

    @pl.when(active & new_expert)
    def _():
        nxt = next_e_ref[e]

        @pl.when(nxt >= 0)
        def _():
            for cp in copies(nxt, 1 - slot):
                cp.start()

        for cp in copies(e, slot):
            cp.wait()
        half = SPLIT_W // 2
        for j in range(w1buf.shape[2] // SPLIT_W):
            chunk = w1buf[slot, :, j * SPLIT_W:(j + 1) * SPLIT_W].astype(BF16)
            split = jnp.dot(chunk, sel_ref[...], preferred_element_type=F32)
            wg_s[:, j * half:(j + 1) * half] = split[:, :half].astype(BF16)
            wl_s[:, j * half:(j + 1) * half] = split[:, half:].astype(BF16)
        w2_s[...] = w2buf[slot].astype(BF16)

    def mlp(rows):
        x = jnp.concatenate([x_ref[pl.ds(s, rows, stride=ROW_TILES), :]
                             for s in range(ROW_TILES)], axis=1).astype(BF16)
        glu = jnp.dot(x, wg_s[...], preferred_element_type=F32) + bg_ref[0]
        lin = jnp.dot(x, wl_s[...], preferred_element_type=F32) + bl_ref[0]
        glu = jnp.minimum(glu, SWIGLU_LIMIT)
        lin = jnp.clip(lin, -SWIGLU_LIMIT, SWIGLU_LIMIT)
        act = glu * jax.nn.sigmoid(SWIGLU_ALPHA * glu) * (lin + 1.0)
        y = jnp.dot(act.astype(BF16), w2_s[...], preferred_element_type=F32) + b2_ref[0]
        for s in range(ROW_TILES):
            o_ref[pl.ds(s, rows, stride=ROW_TILES), :] = y[:, s * LANES:(s + 1) * LANES]
        if rows < MOE_BLOCK:
            o_ref[pl.ds(rows * ROW_TILES, (MOE_BLOCK - rows) * ROW_TILES), :] = jnp.zeros(
                ((MOE_BLOCK - rows) * ROW_TILES, LANES), o_ref.dtype)

    valid_rows = jnp.where(active, blk_rows_ref[blk], 0)
    for rows in range(MOE_SUB, MOE_BLOCK + 1, MOE_SUB):
        @pl.when((valid_rows > rows - MOE_SUB) & (valid_rows <= rows))
        def _():
            mlp(rows)

    @pl.when(valid_rows <= 0)
    def _():
        o_ref[...] = jnp.zeros_like(o_ref)


def _experts(layer, blk_e, nused, next_e, e_slot, blk_rows, xg_rows, w1_all, w2_all, b_glu, b_lin, b2):
    nblk = blk_e.shape[0]
    d = D_MODEL
    ff = w2_all.shape[2]
    col = jnp.arange(SPLIT_W)
    sel = (jnp.arange(SPLIT_W)[None, :] == ((col % 2) * (SPLIT_W // 2) + col // 2)[:, None]).astype(BF16)

    def blk_map(b, blk_e_ref, nused_ref, *_):
        return (jnp.minimum(b, nused_ref[0] - 1), 0)

    def e_map(b, blk_e_ref, nused_ref, *_):
        return (blk_e_ref[jnp.minimum(b, nused_ref[0] - 1)], 0, 0)

    grid_spec = pltpu.PrefetchScalarGridSpec(
        num_scalar_prefetch=5,
        grid=(nblk,),
        in_specs=[
            pl.BlockSpec((MOE_BLOCK * ROW_TILES, LANES), blk_map),
            pl.BlockSpec(memory_space=pl.ANY),
            pl.BlockSpec(memory_space=pl.ANY),
            pl.BlockSpec((1, 1, ff), e_map),
            pl.BlockSpec((1, 1, ff), e_map),
            pl.BlockSpec((1, 1, d), e_map),
            pl.BlockSpec((SPLIT_W, SPLIT_W), lambda b, *_: (0, 0)),
        ],
        out_specs=pl.BlockSpec((MOE_BLOCK * ROW_TILES, LANES), lambda b, *_: (b, 0)),
        scratch_shapes=[
            pltpu.VMEM((2, d, 2 * ff), F32), pltpu.VMEM((2, ff, d), F32),
            pltpu.VMEM((d, ff), BF16), pltpu.VMEM((d, ff), BF16), pltpu.VMEM((ff, d), BF16),
            pltpu.SemaphoreType.DMA((2, 2)),
        ],
    )
    return pl.pallas_call(
        functools.partial(_expert_kernel, layer=layer),
        grid_spec=grid_spec,
        out_shape=jax.ShapeDtypeStruct(xg_rows.shape, F32),
        compiler_params=_cparams("arbitrary"),
        name="moe_experts",
    )(blk_e, nused, next_e, e_slot, blk_rows, xg_rows, w1_all, w2_all, b_glu, b_lin, b2, sel)


def _combine_kernel(n_ref, carry_ref, gstart_ref, yg_ref, lp_ref, gate_ref, x_ref, g2_ref, o_ref, sorted_ref, sem):
    i = pl.program_id(0)
    ntiles = pl.num_programs(0)
    tm = x_ref.shape[0]
    rows = TOP_K * tm
    slot = i % 2

    def runs(tile, buf, fn, live=None):
        _tile_runs(tile, n_ref, carry_ref, gstart_ref, sorted_ref.at[buf], yg_ref, sem.at[buf], tm, fn, False, live)

    @pl.when(i == 0)
    def _():
        runs(0, 0, _start_piece)

    runs(jnp.minimum(i + 1, ntiles - 1), 1 - slot, _start_piece, live=i + 1 < ntiles)

    eye = (lax.broadcasted_iota(jnp.int32, (tm, tm), 0) ==
           lax.broadcasted_iota(jnp.int32, (tm, tm), 1)).astype(BF16)
    rows_t = jnp.concatenate([gate_ref[...], lp_ref[...].astype(F32)], axis=0)
    parts = _split3(rows_t)
    cols = sum(lax.dot_general(eye, parts[:, j * tm:(j + 1) * tm], (((1,), (1,)), ((), ())),
                               preferred_element_type=F32) for j in range(3))
    col_id = lax.broadcasted_iota(jnp.int32, (tm, rows), 1)
    weights = jnp.zeros((tm, rows), F32)
    for k in range(TOP_K):
        lp_col = (cols[:, TOP_K + k:TOP_K + k + 1] + 0.5).astype(jnp.int32)
        weights = weights + jnp.where(col_id == lp_col, cols[:, k:k + 1], 0.0)

    pltpu.make_async_copy(yg_ref.at[pl.ds(0, rows * ROW_TILES)], sorted_ref.at[slot], sem.at[slot]).wait()
    y = jnp.concatenate(
        [sorted_ref[slot, pl.ds(s, rows, stride=ROW_TILES), :] for s in range(ROW_TILES)], axis=1).astype(BF16)
    acc = jnp.dot(weights.astype(BF16), y, preferred_element_type=F32)
    o_ref[...] = x_ref[...] + g2_ref[0] * acc


def _combine(tile_n, tile_carry, group_start, yg_rows, lp_t, gate_t, x, gate2, seq, tm):
    t, d = x.shape
    per_b = seq // tm
    grid_spec = pltpu.PrefetchScalarGridSpec(
        num_scalar_prefetch=3,
        grid=(t // tm,),
        in_specs=[
            pl.BlockSpec(memory_space=pl.ANY),
            pl.BlockSpec((TOP_K, tm), lambda i, *_: (0, i)),
            pl.BlockSpec((TOP_K, tm), lambda i, *_: (0, i)),
            pl.BlockSpec((tm, d), lambda i, *_: (i, 0)),
            pl.BlockSpec((1, 1, d), lambda i, *_: (i // per_b, 0, 0)),
        ],
        out_specs=pl.BlockSpec((tm, d), lambda i, *_: (i, 0)),
        scratch_shapes=[pltpu.VMEM((2, TOP_K * tm * ROW_TILES, LANES), F32), pltpu.SemaphoreType.DMA((2,))],
    )
    return pl.pallas_call(
        _combine_kernel,
        grid_spec=grid_spec,
        out_shape=jax.ShapeDtypeStruct((t, d), F32),
        compiler_params=_cparams("arbitrary"),
        name="moe_combine",
    )(tile_n, tile_carry, group_start, yg_rows, lp_t, gate_t, x, gate2)


def _moe(layer, x, g, shift, scale, gate2, r_w, r_b, w1_all, b1_glu, b1_lin, w2_all, b2, seq):
    t = x.shape[0]
    h, lp_t, gate_t, tile_n, tile_carry, counts = _router(x, g, shift, scale, r_w, r_b, seq, MOE_TILE)
    tile_n = tile_n.reshape(-1)
    tile_carry = tile_carry.reshape(-1)
    counts = counts.reshape(-1)
    padded = ((counts + MOE_BLOCK - 1) // MOE_BLOCK) * MOE_BLOCK
    group_end = jnp.cumsum(padded)
    group_start = (group_end - padded).astype(jnp.int32)
    cap = t * TOP_K + N_EXPERTS * MOE_BLOCK
    nblk = cap // MOE_BLOCK
    blk_start = jnp.arange(nblk, dtype=jnp.int32) * MOE_BLOCK
    e_ids = jnp.arange(N_EXPERTS, dtype=jnp.int32)
    past = (blk_start[None, :] >= group_end[:, None]).astype(jnp.int32)
    blk_e = jnp.minimum(jnp.sum(past, axis=0), N_EXPERTS - 1).astype(jnp.int32)
    nused = (group_end[-1:] // MOE_BLOCK).astype(jnp.int32)

    pad_lo = (group_start + counts).astype(jnp.int32)
    pad_hi = group_end.astype(jnp.int32)
    blk_pad_lo = jnp.sum(jnp.where(blk_e[None, :] == e_ids[:, None], pad_lo[:, None], 0), axis=0)
    blk_rows = jnp.clip(blk_pad_lo - blk_start, 0, MOE_BLOCK).astype(jnp.int32)
    xg_rows = _dispatch(tile_n, tile_carry, group_start, pad_lo, pad_hi, nused, lp_t, h, cap, MOE_TILE)
    nonempty = counts > 0
    later = jnp.where(nonempty[None, :] & (e_ids[None, :] > e_ids[:, None]), e_ids[None, :], N_EXPERTS)
    next_e = jnp.min(later, axis=1)
    next_e = jnp.where(next_e == N_EXPERTS, -1, next_e).astype(jnp.int32)
    e_slot = ((jnp.cumsum(nonempty.astype(jnp.int32)) - 1) % 2).astype(jnp.int32)
    yg_rows = _experts(layer, blk_e, nused, next_e, e_slot, blk_rows, xg_rows, w1_all, w2_all,
                       b1_glu[:, None, :], b1_lin[:, None, :], b2[:, None, :])
    return _combine(tile_n, tile_carry, group_start, yg_rows, lp_t, gate_t, x, gate2, seq, MOE_TILE)


def kernel(x, c, ada_w, ada_b, norm1_g, norm2_g, m_in_w, m_conv_w, m_conv_b, m_dt_bias, m_A_log, m_D, m_norm_g, m_out_w, a_qkv_w, a_q_norm_g, a_k_norm_g, a_sinks, a_out_w, rel_bias, r_w, r_b, e_w1, e_b1, e_w2, e_b2):
    batch, seq, d = x.shape
    depth = ada_w.shape[0]
    t = batch * seq
    xf = x.reshape(t, d)

    mod = _adaln(c, ada_w, ada_b)[:, :batch]

    b1_split = jnp.moveaxis(e_b1.reshape(depth, N_EXPERTS, -1, 2), -1, 1)

    for i in range(depth):
        parts =[mod[i, :, p * d:(p + 1) * d].reshape(batch, 1, d) for p in range(6)]
        sh1, sc1, g1, sh2, sc2, g2 = parts
        j = i // 2
        if i % 2 == 0:
            w_zxbc = m_in_w[j, :, :SSM_ZXBC].astype(BF16)
            w_dt = jnp.pad(m_in_w[j, :, SSM_ZXBC:], ((0, 0), (0, LANES - SSM_HEADS)))
            zxbc, dt_raw = _norm_matmul(xf, norm1_g[i], sh1, sc1, (w_zxbc, w_dt), seq)
            y = _ssd_mixer(zxbc, dt_raw, m_conv_w[j], m_conv_b[j], m_dt_bias[j], m_A_log[j], m_D[j],
                           m_norm_g[j], batch, seq)
            xf = _matmul_residual(y, m_out_w[j].astype(BF16), xf, g1, seq)
        else:
            qkv, = _norm_matmul(xf, norm1_g[i], sh1, sc1, (a_qkv_w[j].astype(BF16),), seq)
            y = _swa_mixer(qkv, a_q_norm_g[j], a_k_norm_g[j], a_sinks[j], rel_bias, batch, seq)
            xf = _matmul_residual(y, a_out_w[j].astype(BF16), xf, g1, seq)
        xf = _moe(i, xf, norm2_g[i], sh2, sc2, g2, r_w[i], r_b[i], e_w1, b1_split[i, 0], b1_split[i, 1],
                  e_w2, e_b2[i], seq)
    return xf.reshape(batch, seq, d)
```

```python
import functools
import math

import jax
import jax.numpy as jnp
from jax import lax
from jax.experimental import pallas as pl
from jax.experimental.pallas import tpu as pltpu

D_MODEL = 1024
EPS = 1e-6
LANES = 128
SUBLANES = 8
ROW_TILES = D_MODEL // LANES

SSM_D_INNER = 2048
SSM_HEAD_DIM = 64
SSM_HEADS = 32
SSM_GROUPS = 4
SSM_STATE = 128
SSM_CONV = 4
SSM_CHUNK = 128
SSM_STEP_CHUNKS = 1
SSM_GN = SSM_GROUPS * SSM_STATE
SSM_ZXBC = 2 * SSM_D_INNER + 2 * SSM_GN
SSM_GROUP_W = SSM_D_INNER // SSM_GROUPS

ATTN_HEAD_DIM = 64
ATTN_Q_HEADS = 16
ATTN_KV_HEADS = 4
ATTN_Q_PER_KV = 4
WINDOW = 128
SWA_STEP_BLOCKS = 4
REL_BUCKETS = 32
REL_MAX_DIST = 128

N_EXPERTS = 32
TOP_K = 4
SWIGLU_ALPHA = 1.702
SWIGLU_LIMIT = 7.0
MOE_BLOCK = 512
MOE_SUB = 128
MOE_TILE = 256
SPLIT_W = 256

VMEM_LIMIT = 56 * 1024 * 1024
HI = lax.Precision.HIGHEST
F32 = jnp.float32
BF16 = jnp.bfloat16


def _cparams(*sem):
    return pltpu.CompilerParams(dimension_semantics=sem, vmem_limit_bytes=VMEM_LIMIT)


def _norm_modulate(x, g, shift, scale):
    ms = jnp.mean(x * x, axis=-1, keepdims=True)
    return x * lax.rsqrt(ms + EPS) * g * (1.0 + scale) + shift


def _adaln_kernel(ct_ref, w_ref, b_ref, o_ref, *, batch):
    c_t = ct_ref[...]
    act_t = c_t * jax.nn.sigmoid(c_t)
    w = w_ref[0]
    rows = [jnp.sum(act_t[:, b:b + 1] * w, axis=0, keepdims=True) + b_ref[0] for b in range(batch)]
    rows.append(jnp.zeros((SUBLANES - batch, w.shape[1]), F32))
    o_ref[0] = jnp.concatenate(rows, axis=0)


def _adaln(c, ada_w, ada_b):
    depth, d, n = ada_w.shape
    batch = c.shape[0]
    tn = 1536
    c_t = jnp.zeros((d, SUBLANES), F32).at[:, :batch].set(c.T)
    return pl.pallas_call(
        functools.partial(_adaln_kernel, batch=batch),
        grid=(depth, n // tn),
        in_specs=[
            pl.BlockSpec((d, SUBLANES), lambda i, j: (0, 0)),
            pl.BlockSpec((1, d, tn), lambda i, j: (i, 0, j)),
            pl.BlockSpec((1, 1, tn), lambda i, j: (i, 0, j)),
        ],
        out_specs=pl.BlockSpec((1, SUBLANES, tn), lambda i, j: (i, 0, j)),
        out_shape=jax.ShapeDtypeStruct((depth, SUBLANES, n), F32),
        compiler_params=_cparams("arbitrary", "arbitrary"),
        name="adaln",
    )(c_t, ada_w, ada_b.reshape(depth, 1, n))


def _norm_matmul_kernel(x_ref, g_ref, sh_ref, sc_ref, *refs):
    w_refs, o_refs = refs[:len(refs) // 2], refs[len(refs) // 2:]
    h = _norm_modulate(x_ref[...], g_ref[...], sh_ref[0], sc_ref[0]).astype(BF16)
    for w_ref, o_ref in zip(w_refs, o_refs):
        o_ref[...] = jnp.dot(h, w_ref[...].astype(BF16), preferred_element_type=F32)


def _norm_matmul(x, g, shift, scale, weights_bf16, seq, tm=512):
    t, d = x.shape
    per_b = seq // tm
    return pl.pallas_call(
        _norm_matmul_kernel,
        grid=(t // tm,),
        in_specs=[
            pl.BlockSpec((tm, d), lambda i: (i, 0)),
            pl.BlockSpec((1, d), lambda i: (0, 0)),
            pl.BlockSpec((1, 1, d), lambda i: (i // per_b, 0, 0)),
            pl.BlockSpec((1, 1, d), lambda i: (i // per_b, 0, 0)),
        ] + [pl.BlockSpec(w.shape, lambda i: (0, 0), pipeline_mode=pl.Buffered(1))
             for w in weights_bf16],
        out_specs=[pl.BlockSpec((tm, w.shape[1]), lambda i: (i, 0)) for w in weights_bf16],
        out_shape=[jax.ShapeDtypeStruct((t, w.shape[1]), F32) for w in weights_bf16],
        compiler_params=_cparams("arbitrary"),
        name="norm_matmul",
    )(x, g.reshape(1, d), shift, scale, *weights_bf16)


def _matmul_residual_kernel(y_ref, w_ref, x_ref, gate_ref, o_ref):
    acc = jnp.dot(y_ref[...], w_ref[...], preferred_element_type=F32)
    o_ref[...] = x_ref[...] + gate_ref[0] * acc


def _matmul_residual(y_bf16, w_bf16, x, gate, seq, tm=1024):
    t, k = y_bf16.shape
    d = x.shape[1]
    per_b = seq // tm
    return pl.pallas_call(
        _matmul_residual_kernel,
        grid=(t // tm,),
        in_specs=[
            pl.BlockSpec((tm, k), lambda i: (i, 0)),
            pl.BlockSpec((k, d), lambda i: (0, 0)),
            pl.BlockSpec((tm, d), lambda i: (i, 0)),
            pl.BlockSpec((1, 1, d), lambda i: (i // per_b, 0, 0)),
        ],
        out_specs=pl.BlockSpec((tm, d), lambda i: (i, 0)),
        out_shape=jax.ShapeDtypeStruct((t, d), F32),
        compiler_params=_cparams("arbitrary"),
        name="matmul_residual",
    )(y_bf16, w_bf16, x, gate)


def _causal_conv_silu(cur, prev_tail, w, b):
    rows = lax.broadcasted_iota(jnp.int32, (SUBLANES, cur.shape[1]), 0)
    acc = b + w[SSM_CONV - 1:SSM_CONV] * cur
    for d in range(1, SSM_CONV):
        rolled = pltpu.roll(cur, d, axis=0)
        top = jnp.where(rows < d, pltpu.roll(prev_tail, d, axis=0), rolled[0:SUBLANES])
        shifted = jnp.concatenate([top, rolled[SUBLANES:]], axis=0)
        acc = acc + w[SSM_CONV - 1 - d:SSM_CONV - d] * shifted
    return acc * jax.nn.sigmoid(acc)


def _split3(x):
    hi = x.astype(BF16)
    rem = x - hi.astype(F32)
    mid = rem.astype(BF16)
    lo = (rem - mid.astype(F32)).astype(BF16)
    return jnp.concatenate([hi, mid, lo], axis=1)


def _ssd_kernel(z_ref, xs_ref, bc_ref, dt_ref, *refs):
    o_ref, tailx_ref, tailb_ref, state_ref = refs[-4:]

    @pl.when(pl.program_id(1) == 0)
    def _():
        tailx_ref[...] = jnp.zeros_like(tailx_ref)
        tailb_ref[...] = jnp.zeros_like(tailb_ref)
        state_ref[...] = jnp.zeros_like(state_ref)

    for u in range(SSM_STEP_CHUNKS):
        rows = pl.ds(u * SSM_CHUNK, SSM_CHUNK)
        _ssd_chunk(z_ref.at[rows], xs_ref.at[rows], bc_ref.at[rows], dt_ref.at[rows], *refs[:-4],
                   o_ref.at[rows], tailx_ref, tailb_ref, state_ref)


def _ssd_chunk(z_ref, xs_ref, bc_ref, dt_ref, cwx_ref, cbx_ref, cwb_ref, cbb_ref, dtb_ref, alog_ref,
               dskip_ref, ng_ref, hexp_ref, lexp_ref, o_ref, tailx_ref, tailb_ref, state_ref):
    xs_raw = xs_ref[...]
    bc_raw = bc_ref[...]
    xs = _causal_conv_silu(xs_raw, tailx_ref[...], cwx_ref[...], cbx_ref[...])
    bc = _causal_conv_silu(bc_raw, tailb_ref[...], cwb_ref[...], cbb_ref[...])
    tailx_ref[...] = xs_raw[SSM_CHUNK - SUBLANES:]
    tailb_ref[...] = bc_raw[SSM_CHUNK - SUBLANES:]

    dt_in = dt_ref[...][:, :SSM_HEADS] + dtb_ref[...]
    dt = jnp.maximum(dt_in, 0.0) + jnp.log1p(jnp.exp(-jnp.abs(dt_in)))
    a_neg = -jnp.exp(alog_ref[...])
    d_a = dt * a_neg
    li = lax.broadcasted_iota(jnp.int32, (SSM_CHUNK, SSM_CHUNK), 0)
    si = lax.broadcasted_iota(jnp.int32, (SSM_CHUNK, SSM_CHUNK), 1)
    causal = li >= si
    tri = causal.astype(F32)
    a_cs = jnp.dot(tri, d_a, preferred_element_type=F32, precision=HI)
    a_cs_t = lax.dot_general(d_a, tri, (((0,), (1,)), ((), ())),
                             preferred_element_type=F32, precision=HI)
    a_last = a_cs[SSM_CHUNK - 1:SSM_CHUNK]
    e_out = jnp.exp(a_cs)
    e_state = jnp.exp(a_last - a_cs) * dt
    small = jnp.concatenate([dt, e_out, e_state], axis=0)
    wide = jnp.dot(_split3(small), hexp_ref[...], preferred_element_type=F32)
    dt_w = wide[0:SSM_CHUNK]
    e_out_w = wide[SSM_CHUNK:2 * SSM_CHUNK]
    e_state_w = wide[2 * SSM_CHUNK:]
    a_col = jnp.dot(_split3(a_cs), lexp_ref[...], preferred_element_type=F32)

    x_dt = (xs * dt_w).astype(BF16)
    x_state = (xs * e_state_w).astype(BF16)
    chunk_decay_w = e_out_w[SSM_CHUNK - 1:SSM_CHUNK]

    heads_per_group = SSM_HEADS // SSM_GROUPS
    y_parts = []
    for g in range(SSM_GROUPS):
        b_g = bc[:, g * SSM_STATE:(g + 1) * SSM_STATE].astype(BF16)
        c_g = bc[:, SSM_GN + g * SSM_STATE:SSM_GN + (g + 1) * SSM_STATE].astype(BF16)
        cb = lax.dot_general(c_g, b_g, (((1,), (1,)), ((), ())), preferred_element_type=F32)
        gsl = slice(g * SSM_GROUP_W, (g + 1) * SSM_GROUP_W)
        h_prev = state_ref[g]
        y_off = jnp.dot(c_g, h_prev.astype(BF16), preferred_element_type=F32) * e_out_w[:, gsl]
        diag = []
        for r in range(heads_per_group):
            h = g * heads_per_group + r
            seg = a_col[:, h * SSM_CHUNK:(h + 1) * SSM_CHUNK] - a_cs_t[h:h + 1, :]
            decay = jnp.exp(jnp.where(causal, seg, -jnp.inf))
            m = (cb * decay).astype(BF16)
            diag.append(jnp.dot(m, x_dt[:, h * SSM_HEAD_DIM:(h + 1) * SSM_HEAD_DIM],
                                preferred_element_type=F32))
        y_parts.append(jnp.concatenate(diag, axis=1) + y_off)
        upd = lax.dot_general(b_g, x_state[:, gsl], (((0,), (0,)), ((), ())), preferred_element_type=F32)
        state_ref[g] = h_prev * chunk_decay_w[:, gsl] + upd

    y = jnp.concatenate(y_parts, axis=1) + dskip_ref[...] * xs
    z = z_ref[...]
    y = y * (z * jax.nn.sigmoid(z))
    normed = []
    for g in range(SSM_GROUPS):
        y_g = y[:, g * SSM_GROUP_W:(g + 1) * SSM_GROUP_W]
        normed.append(y_g * lax.rsqrt(jnp.mean(y_g * y_g, axis=-1, keepdims=True) + EPS))
    o_ref[...] = (jnp.concatenate(normed, axis=1) * ng_ref[...]).astype(o_ref.dtype)


def _ssd_mixer(zxbc, dt_raw, conv_w, conv_b, dt_bias, a_log, d_skip, norm_g, batch, seq):
    t = zxbc.shape[0]
    step_rows = SSM_CHUNK * SSM_STEP_CHUNKS
    nc = seq // step_rows
    head_expand = jnp.tile(jnp.repeat(jnp.eye(SSM_HEADS, dtype=BF16), SSM_HEAD_DIM, axis=1), (3, 1))
    lane_expand = jnp.tile(jnp.repeat(jnp.eye(SSM_HEADS, dtype=BF16), SSM_CHUNK, axis=1), (3, 1))
    row = lambda b, c: (b * nc + c, 0)
    const2 = lambda b, c: (0, 0)
    bc_w = 2 * SSM_GN
    return pl.pallas_call(
        _ssd_kernel,
        grid=(batch, nc),
        in_specs=[
            pl.BlockSpec((step_rows, SSM_D_INNER), row),
            pl.BlockSpec((step_rows, SSM_D_INNER), lambda b, c: (b * nc + c, 1)),
            pl.BlockSpec((step_rows, bc_w), lambda b, c: (b * nc + c, 2 * SSM_D_INNER // bc_w)),
            pl.BlockSpec((step_rows, LANES), row),
            pl.BlockSpec((SSM_CONV, SSM_D_INNER), const2),
            pl.BlockSpec((1, SSM_D_INNER), const2),
            pl.BlockSpec((SSM_CONV, bc_w), const2),
            pl.BlockSpec((1, bc_w), const2),
            pl.BlockSpec((1, SSM_HEADS), const2),
            pl.BlockSpec((1, SSM_HEADS), const2),
            pl.BlockSpec((1, SSM_D_INNER), const2),
            pl.BlockSpec((1, SSM_D_INNER), const2),
            pl.BlockSpec((3 * SSM_HEADS, SSM_D_INNER), const2),
            pl.BlockSpec((3 * SSM_HEADS, SSM_HEADS * SSM_CHUNK), const2),
        ],
        out_specs=pl.BlockSpec((step_rows, SSM_D_INNER), row),
        out_shape=jax.ShapeDtypeStruct((t, SSM_D_INNER), BF16),
        scratch_shapes=[
            pltpu.VMEM((SUBLANES, SSM_D_INNER), F32),
            pltpu.VMEM((SUBLANES, bc_w), F32),
            pltpu.VMEM((SSM_GROUPS, SSM_STATE, SSM_GROUP_W), F32),
        ],
        compiler_params=_cparams("arbitrary", "arbitrary"),
        name="ssd_mixer",
    )(zxbc, zxbc, zxbc, dt_raw,
      conv_w[:, :SSM_D_INNER], conv_b[:SSM_D_INNER].reshape(1, -1),
      conv_w[:, SSM_D_INNER:], conv_b[SSM_D_INNER:].reshape(1, -1),
      dt_bias.reshape(1, -1), a_log.reshape(1, -1),
      jnp.repeat(d_skip, SSM_HEAD_DIM).reshape(1, -1), norm_g.reshape(1, -1),
      head_expand, lane_expand)


def _head_rms(x, g):
    return x * lax.rsqrt(jnp.mean(x * x, axis=-1, keepdims=True) + EPS) * g


def _swa_kernel(q_ref, kvc_ref, kvp_ref, bucket_ref, qg_ref, kg_ref, rel_ref, sink_ref, o_ref,
                bias_ref, sinkrow_ref):
    b = pl.program_id(0)
    i = pl.program_id(1)

    @pl.when((b == 0) & (i == 0))
    def _():
        bucket = bucket_ref[...]
        kj = lax.broadcasted_iota(jnp.int32, (2 * WINDOW, WINDOW), 0)
        qi = lax.broadcasted_iota(jnp.int32, (2 * WINDOW, WINDOW), 1)
        dist = qi + WINDOW - kj
        band = (dist >= 0) & (dist < WINDOW)
        for h in range(ATTN_Q_HEADS):
            g, r = divmod(h, ATTN_Q_PER_KV)
            acc = jnp.zeros(bucket.shape, F32)
            for k in range(REL_BUCKETS):
                acc = jnp.where(bucket == k, rel_ref[k, h], acc)
            cols = slice(r * WINDOW, (r + 1) * WINDOW)
            bias_ref[0, g, :, cols] = jnp.where(band, acc, -jnp.inf)
            bias_ref[1, g, :, cols] = jnp.where(band & (kj >= WINDOW), acc, -jnp.inf)
            sinkrow_ref[g, :, cols] = jnp.full((1, WINDOW), sink_ref[h], F32)

    for u in range(SWA_STEP_BLOCKS):
        rows = slice(u * WINDOW, (u + 1) * WINDOW)
        kv_p = kvp_ref[...] if u == 0 else kvc_ref[(u - 1) * WINDOW:u * WINDOW, :]
        first = (i == 0).astype(jnp.int32) if u == 0 else 0
        o_ref[rows, :] = _swa_block(q_ref[rows, :], kvc_ref[rows, :], kv_p, first,
                                    qg_ref, kg_ref, bias_ref, sinkrow_ref).astype(o_ref.dtype)


def _swa_block(q_rows, kv_c, kv_p, first, qg_ref, kg_ref, bias_ref, sinkrow_ref):
    q_t = q_rows.T
    kv_w = ATTN_KV_HEADS * ATTN_HEAD_DIM
    q_gain = qg_ref[...]
    outs = []
    for g in range(ATTN_KV_HEADS):
        ksl = slice(g * ATTN_HEAD_DIM, (g + 1) * ATTN_HEAD_DIM)
        vsl = slice(kv_w + g * ATTN_HEAD_DIM, kv_w + (g + 1) * ATTN_HEAD_DIM)
        k = jnp.concatenate([kv_p[:, ksl], kv_c[:, ksl]], axis=0)
        v = jnp.concatenate([kv_p[:, vsl], kv_c[:, vsl]], axis=0).astype(BF16)
        k = _head_rms(k, kg_ref[...]).astype(BF16)
        q_heads = []
        for r in range(ATTN_Q_PER_KV):
            h = g * ATTN_Q_PER_KV + r
            q_h = q_t[h * ATTN_HEAD_DIM:(h + 1) * ATTN_HEAD_DIM]
            inv = lax.rsqrt(jnp.mean(q_h * q_h, axis=0, keepdims=True) + EPS)
            q_heads.append((q_h * inv * q_gain).astype(BF16))
        q = jnp.concatenate(q_heads, axis=1)
        s = jnp.dot(k, q, preferred_element_type=F32) + bias_ref[first, g]
        sink = sinkrow_ref[g]
        m = jnp.maximum(jnp.max(s, axis=0, keepdims=True), sink)
        p = jnp.exp(s - m)
        denom = jnp.sum(p, axis=0, keepdims=True) + jnp.exp(sink - m)
        pv = lax.dot_general(v, p.astype(BF16), (((0,), (0,)), ((), ())), preferred_element_type=F32)
        pv = pv * (1.0 / denom)
        outs.extend(pv[:, r * WINDOW:(r + 1) * WINDOW] for r in range(ATTN_Q_PER_KV))
    return jnp.concatenate(outs, axis=0).T


def _t5_causal_bucket(dist):
    max_exact = REL_BUCKETS // 2
    d = jnp.maximum(dist, 1).astype(F32)
    large = max_exact + (jnp.log(d / max_exact) / math.log(REL_MAX_DIST / max_exact)
                         * (REL_BUCKETS - max_exact)).astype(jnp.int32)
    large = jnp.minimum(large, REL_BUCKETS - 1)
    return jnp.where(dist < max_exact, dist, large)


def _swa_mixer(qkv, q_norm_g, k_norm_g, sinks, rel_bias, batch, seq):
    t = qkv.shape[0]
    nb = seq // WINDOW
    step_rows = WINDOW * SWA_STEP_BLOCKS
    ns = seq // step_rows
    q_w = ATTN_Q_HEADS * ATTN_HEAD_DIM
    kv_w2 = 2 * ATTN_KV_HEADS * ATTN_HEAD_DIM
    kj = jnp.arange(2 * WINDOW)[:, None]
    qi = jnp.arange(WINDOW)[None, :]
    bucket = _t5_causal_bucket(jnp.maximum(qi + WINDOW - kj, 0)).astype(jnp.int32)
    q_gain = jnp.broadcast_to((q_norm_g * (ATTN_HEAD_DIM ** -0.5))[:, None], (ATTN_HEAD_DIM, WINDOW))
    const2 = lambda b, i: (0, 0)
    smem = pl.BlockSpec(memory_space=pltpu.SMEM)
    return pl.pallas_call(
        _swa_kernel,
        grid=(batch, ns),
        in_specs=[
            pl.BlockSpec((step_rows, q_w), lambda b, i: (b * ns + i, 0)),
            pl.BlockSpec((step_rows, kv_w2), lambda b, i: (b * ns + i, q_w // kv_w2)),
            pl.BlockSpec((WINDOW, kv_w2),
                         lambda b, i: (b * nb + jnp.maximum(i * SWA_STEP_BLOCKS - 1, 0), q_w // kv_w2)),
            pl.BlockSpec((2 * WINDOW, WINDOW), const2),
            pl.BlockSpec((ATTN_HEAD_DIM, WINDOW), const2),
            pl.BlockSpec((1, ATTN_HEAD_DIM), const2),
            smem,
            smem,
        ],
        out_specs=pl.BlockSpec((step_rows, q_w), lambda b, i: (b * ns + i, 0)),
        out_shape=jax.ShapeDtypeStruct((t, q_w), BF16),
        scratch_shapes=[pltpu.VMEM((2, ATTN_KV_HEADS, 2 * WINDOW, ATTN_Q_PER_KV * WINDOW), F32),
                        pltpu.VMEM((ATTN_KV_HEADS, 1, ATTN_Q_PER_KV * WINDOW), F32)],
        compiler_params=_cparams("arbitrary", "arbitrary"),
        name="swa_mixer",
    )(qkv, qkv, qkv, bucket, q_gain, k_norm_g.reshape(1, -1), rel_bias, sinks)


def _router_kernel(x_ref, g_ref, sh_ref, sc_ref, rwt_ref, rb_ref,
                   h_ref, lp_ref, gate_ref, tile_n_ref, tile_carry_ref, cnt_ref, carry_ref):
    i = pl.program_id(0)
    tm = x_ref.shape[0]

    @pl.when(i == 0)
    def _():
        carry_ref[...] = jnp.zeros_like(carry_ref)

    h = _norm_modulate(x_ref[...], g_ref[...], sh_ref[0], sc_ref[0]).astype(BF16)
    h_ref[...] = h

    logits = lax.dot_general(rwt_ref[...], h, (((1,), (1,)), ((), ())),
                             preferred_element_type=F32) + rb_ref[...]
    e_iota = lax.broadcasted_iota(jnp.int32, logits.shape, 0)
    work = logits
    sels, vals = [], []
    for k in range(TOP_K):
        m = jnp.max(work, axis=0, keepdims=True)
        idx = jnp.min(jnp.where(work == m, e_iota, N_EXPERTS), axis=0, keepdims=True)
        sel = e_iota == idx
        work = jnp.where(sel, -jnp.inf, work)
        sels.append(sel)
        vals.append(m)
    exps = [jnp.exp(v - vals[0]) for v in vals]
    denom = exps[0] + exps[1] + exps[2] + exps[3]
    for k in range(TOP_K):
        gate_ref[k:k + 1, :] = exps[k] / denom

    chosen = sels[0] | sels[1] | sels[2] | sels[3]
    t_row = lax.broadcasted_iota(jnp.int32, (tm, tm), 0)
    t_col = lax.broadcasted_iota(jnp.int32, (tm, tm), 1)
    before = (t_row < t_col).astype(BF16)
    prior = jnp.dot(chosen.astype(BF16), before, preferred_element_type=F32)
    n = jnp.sum(chosen.astype(F32), axis=1, keepdims=True)
    e_row = lax.broadcasted_iota(jnp.int32, (N_EXPERTS, N_EXPERTS), 0)
    e_col = lax.broadcasted_iota(jnp.int32, (N_EXPERTS, N_EXPERTS), 1)
    run_start = jnp.dot((e_col < e_row).astype(BF16), jnp.broadcast_to(n, (N_EXPERTS, LANES)).astype(BF16),
                        preferred_element_type=F32)[:, :1]
    local = run_start + prior
    for k in range(TOP_K):
        lp_ref[k:k + 1, :] = jnp.sum(jnp.where(sels[k], local, 0.0), axis=0, keepdims=True).astype(jnp.int32)
    tile_n_ref[0] = n.astype(jnp.int32)
    tile_carry_ref[0] = carry_ref[...].astype(jnp.int32)
    total = carry_ref[...] + n
    carry_ref[...] = total
    cnt_ref[...] = total.astype(jnp.int32)


def _router(x, g, shift, scale, r_w, r_b, seq, tm):
    t, d = x.shape
    per_b = seq // tm
    ntiles = t // tm
    return pl.pallas_call(
        _router_kernel,
        grid=(t // tm,),
        in_specs=[
            pl.BlockSpec((tm, d), lambda i: (i, 0)),
            pl.BlockSpec((1, d), lambda i: (0, 0)),
            pl.BlockSpec((1, 1, d), lambda i: (i // per_b, 0, 0)),
            pl.BlockSpec((1, 1, d), lambda i: (i // per_b, 0, 0)),
            pl.BlockSpec((N_EXPERTS, d), lambda i: (0, 0)),
            pl.BlockSpec((N_EXPERTS, 1), lambda i: (0, 0)),
        ],
        out_specs=[
            pl.BlockSpec((tm, d), lambda i: (i, 0)),
            pl.BlockSpec((TOP_K, tm), lambda i: (0, i)),
            pl.BlockSpec((TOP_K, tm), lambda i: (0, i)),
            pl.BlockSpec((1, N_EXPERTS, 1), lambda i: (i, 0, 0)),
            pl.BlockSpec((1, N_EXPERTS, 1), lambda i: (i, 0, 0)),
            pl.BlockSpec((N_EXPERTS, 1), lambda i: (0, 0)),
        ],
        out_shape=[
            jax.ShapeDtypeStruct((t, d), BF16),
            jax.ShapeDtypeStruct((TOP_K, t), jnp.int32),
            jax.ShapeDtypeStruct((TOP_K, t), F32),
            jax.ShapeDtypeStruct((ntiles, N_EXPERTS, 1), jnp.int32),
            jax.ShapeDtypeStruct((ntiles, N_EXPERTS, 1), jnp.int32),
            jax.ShapeDtypeStruct((N_EXPERTS, 1), jnp.int32),
        ],
        scratch_shapes=[pltpu.VMEM((N_EXPERTS, 1), F32)],
        compiler_params=_cparams("arbitrary"),
        name="moe_router",
    )(x, g.reshape(1, d), shift, scale, r_w.T.astype(BF16), r_b.reshape(-1, 1))


def _run_copies(n, src_ref, src_row, dst_ref, dst_row, sem, max_rows, fn):
    for b in range(max_rows.bit_length()):
        size = 1 << b

        @pl.when((n & size) != 0)
        def _():
            lo = n & (size - 1)
            fn(pltpu.make_async_copy(
                src_ref.at[pl.ds(pl.multiple_of((src_row + lo) * ROW_TILES, ROW_TILES), size * ROW_TILES)],
                dst_ref.at[pl.ds(pl.multiple_of((dst_row + lo) * ROW_TILES, ROW_TILES), size * ROW_TILES)],
                sem), b)


def _start_piece(cp, b):
    cp.start()


def _wait_piece(cp, b):
    cp.wait()


def _tile_runs(tile, n_ref, carry_ref, gstart_ref, sorted_ref, grouped_ref, sem, tm, fn, to_grouped, live=None):
    def body(e, run_start):
        n = n_ref[tile * N_EXPERTS + e]
        if live is not None:
            n = jnp.where(live, n, 0)
        slot0 = gstart_ref[e] + carry_ref[tile * N_EXPERTS + e]
        if to_grouped:
            _run_copies(n, sorted_ref, run_start, grouped_ref, slot0, sem, tm, fn)
        else:
            _run_copies(n, grouped_ref, slot0, sorted_ref, run_start, sem, tm, fn)
        return run_start + n

    if live is None:
        lax.fori_loop(0, N_EXPERTS, body, 0)
    else:
        run_start = 0
        for e in range(N_EXPERTS):
            run_start = body(e, run_start)


def _block_copy(src_ref, dst_ref, dst_blk, sem):
    rows = MOE_BLOCK * ROW_TILES
    return pltpu.make_async_copy(src_ref, dst_ref.at[pl.ds(pl.multiple_of(dst_blk * rows, rows), rows)], sem)


def _dispatch_kernel(n_ref, carry_ref, gstart_ref, pad_lo_ref, pad_hi_ref, nused_ref, lp_ref, h_ref, xg_ref,
                     sorted0_ref, sorted1_ref, sorted2_ref, zero_ref, run_sem, sem):
    i = pl.program_id(0)
    last = pl.num_programs(0) - 1
    tm = h_ref.shape[0]
    rows = TOP_K * tm
    nblk =xg_ref.shape[0] // (MOE_BLOCK * ROW_TILES)

    @pl.when(i == 0)
    def _():
        zero_ref[...] = jnp.zeros_like(zero_ref)

        def fill(b, carry):
            _block_copy(zero_ref, xg_ref, b, sem).start()
            return carry

        def fill_wait(b, carry):
            _block_copy(zero_ref, xg_ref, b, sem).wait()
            return carry

        lax.fori_loop(nused_ref[0], nblk, fill, 0)
        lax.fori_loop(nused_ref[0], nblk, fill_wait, 0)

    @pl.when(i < N_EXPERTS)
    def _():
        n_pad = pad_hi_ref[i] - pad_lo_ref[i]
        for fn in (_start_piece, _wait_piece):
            _run_copies(n_pad, zero_ref, 0, xg_ref, pad_lo_ref[i], sem, MOE_BLOCK - 1, fn)

    bufs = (sorted0_ref, sorted1_ref, sorted2_ref)
    nbuf = len(bufs)

    def runs(tile, b, fn, live=None):
        _tile_runs(tile, n_ref, carry_ref, gstart_ref, bufs[b], xg_ref, run_sem.at[b], tm, fn, True, live)

    def wait_tile(b):
        pltpu.make_async_copy(bufs[b], xg_ref.at[pl.ds(0, rows * ROW_TILES)], run_sem.at[b]).wait()

    def step(cur):
        prev, oldest = (cur - 1) % nbuf, (cur + 1) % nbuf
        cur_ref = bufs[cur]

        @pl.when(i >= nbuf)
        def _():
            wait_tile(cur)

        row_id = lax.broadcasted_iota(jnp.int32, (rows, tm), 0)
        perm = row_id == lp_ref[0:1, :]
        for k in range(1, TOP_K):
            perm = perm | (row_id == lp_ref[k:k + 1, :])
        srt = jnp.dot(jnp.where(perm, 1.0, 0.0).astype(BF16), h_ref[...], preferred_element_type=F32)
        for s in range(ROW_TILES):
            cur_ref[pl.ds(s, rows, stride=ROW_TILES), :] = srt[:, s * LANES:(s + 1) * LANES]

        runs(jnp.maximum(i - 1, 0), prev, _start_piece, live=i >= 1)

        @pl.when(i == last)
        def _():
            @pl.when(i >= 2)
            def _():
                wait_tile(oldest)

            @pl.when(i >= 1)
            def _():
                wait_tile(prev)

            runs(i, cur, _start_piece)
            wait_tile(cur)

    for r in range(nbuf):
        @pl.when(i % nbuf == r)
        def _():
            step(r)


def _dispatch(tile_n, tile_carry, group_start, pad_lo, pad_hi, nused, lp_t, h, cap, tm):
    t, d = h.shape
    assert t // tm >= N_EXPERTS
    grid_spec = pltpu.PrefetchScalarGridSpec(
        num_scalar_prefetch=6,
        grid=(t // tm,),
        in_specs=[
            pl.BlockSpec((TOP_K, tm), lambda i, *_: (0, i)),
            pl.BlockSpec((tm, d), lambda i, *_: (i, 0)),
        ],
        out_specs=pl.BlockSpec(memory_space=pl.ANY),
        scratch_shapes=[
            pltpu.VMEM((TOP_K * tm * ROW_TILES, LANES), F32),
            pltpu.VMEM((TOP_K * tm * ROW_TILES, LANES), F32),
            pltpu.VMEM((TOP_K * tm * ROW_TILES, LANES), F32),
            pltpu.VMEM((MOE_BLOCK * ROW_TILES, LANES), F32),
            pltpu.SemaphoreType.DMA((3,)), pltpu.SemaphoreType.DMA,
        ],
    )
    return pl.pallas_call(
        _dispatch_kernel,
        grid_spec=grid_spec,
        out_shape=jax.ShapeDtypeStruct((cap * ROW_TILES, LANES), F32),
        compiler_params=pltpu.CompilerParams(dimension_semantics=("arbitrary",), has_side_effects=True,
                                             vmem_limit_bytes=VMEM_LIMIT),
        name="moe_dispatch",
    )(tile_n, tile_carry, group_start, pad_lo, pad_hi, nused, lp_t, h)


def _weight_copies(w1_hbm, w2_hbm, w1buf, w2buf, sems, layer, e, slot):
    return (pltpu.make_async_copy(w1_hbm.at[layer, e], w1buf.at[slot], sems.at[0, slot]),
            pltpu.make_async_copy(w2_hbm.at[layer, e], w2buf.at[slot], sems.at[1, slot]))


def _expert_kernel(blk_e_ref, nused_ref, next_e_ref, slot_ref, blk_rows_ref, x_ref, w1_hbm, w2_hbm, bg_ref, bl_ref, b2_ref,
                   sel_ref, o_ref, w1buf, w2buf, wg_s, wl_s, w2_s, sems, *, layer):
    blk = pl.program_id(0)
    active = blk < nused_ref[0]
    e = blk_e_ref[blk]
    slot = slot_ref[e]
    new_expert = (blk == 0) | (e != blk_e_ref[jnp.maximum(blk - 1, 0)])
    copies = functools.partial(_weight_copies, w1_hbm, w2_hbm, w1buf, w2buf, sems, layer)

    @pl.when(blk == 0)
    def _():
        for cp in copies(e, slot):
            cp.start()

    @pl.when(active & new_expert)
    def _():
        nxt = next_e_ref[e]

        @pl.when(nxt >= 0)
        def _():
            for cp in copies(nxt, 1 - slot):
                cp.start()

        for cp in copies(e, slot):
            cp.wait()
        half = SPLIT_W // 2
        for j in range(w1buf.shape[2] // SPLIT_W):
            chunk = w1buf[slot, :, j * SPLIT_W:(j + 1) * SPLIT_W].astype(BF16)
            split = jnp.dot(chunk, sel_ref[...], preferred_element_type=F32)
            wg_s[:, j * half:(j + 1) * half] = split[:, :half].astype(BF16)
            wl_s[:, j * half:(j + 1) * half] = split[:, half:].astype(BF16)
        w2_s[...] = w2buf[slot].astype(BF16)

    def mlp(rows):
        x = jnp.concatenate([x_ref[pl.ds(s, rows, stride=ROW_TILES), :]
                             for s in range(ROW_TILES)], axis=1).astype(BF16)
        glu = jnp.dot(x, wg_s[...], preferred_element_type=F32) + bg_ref[0]
        lin = jnp.dot(x, wl_s[...], preferred_element_type=F32) + bl_ref[0]
        glu = jnp.minimum(glu, SWIGLU_LIMIT)
        lin = jnp.clip(lin, -SWIGLU_LIMIT, SWIGLU_LIMIT)
        act = glu * jax.nn.sigmoid(SWIGLU_ALPHA * glu) * (lin + 1.0)
        y = jnp.dot(act.astype(BF16), w2_s[...], preferred_element_type=F32) + b2_ref[0]
        for s in range(ROW_TILES):
            o_ref[pl.ds(s, rows, stride=ROW_TILES), :] = y[:, s * LANES:(s + 1) * LANES]
        if rows < MOE_BLOCK:
            o_ref[pl.ds(rows * ROW_TILES, (MOE_BLOCK - rows) * ROW_TILES), :] = jnp.zeros(
                ((MOE_BLOCK - rows) * ROW_TILES, LANES), o_ref.dtype)

    valid_rows = jnp.where(active, blk_rows_ref[blk], 0)
    for rows in range(MOE_SUB, MOE_BLOCK + 1, MOE_SUB):
        @pl.when((valid_rows > rows - MOE_SUB) & (valid_rows <= rows))
        def _():
            mlp(rows)


def _experts(layer, blk_e, nused, next_e, e_slot, blk_rows, xg_rows, w1_all, w2_all, b_glu, b_lin, b2):
    nblk = blk_e.shape[0]
    d = D_MODEL
    ff = w2_all.shape[2]
    col = jnp.arange(SPLIT_W)
    sel = (jnp.arange(SPLIT_W)[None, :] == ((col % 2) * (SPLIT_W // 2) + col // 2)[:, None]).astype(BF16)

    def blk_map(b, blk_e_ref, nused_ref, *_):
        return (jnp.minimum(b, nused_ref[0] - 1), 0)

    def e_map(b, blk_e_ref, nused_ref, *_):
        return (blk_e_ref[jnp.minimum(b, nused_ref[0] - 1)], 0, 0)

    grid_spec = pltpu.PrefetchScalarGridSpec(
        num_scalar_prefetch=5,
        grid=(nblk,),
        in_specs=[
            pl.BlockSpec((MOE_BLOCK * ROW_TILES, LANES), blk_map),
            pl.BlockSpec(memory_space=pl.ANY),
            pl.BlockSpec(memory_space=pl.ANY),
            pl.BlockSpec((1, 1, ff), e_map),
            pl.BlockSpec((1, 1, ff), e_map),
            pl.BlockSpec((1, 1, d), e_map),
            pl.BlockSpec((SPLIT_W, SPLIT_W), lambda b, *_: (0, 0)),
        ],
        out_specs=pl.BlockSpec((MOE_BLOCK * ROW_TILES, LANES), blk_map),
        scratch_shapes=[
            pltpu.VMEM((2, d, 2 * ff), F32), pltpu.VMEM((2, ff, d), F32),
            pltpu.VMEM((d, ff), BF16), pltpu.VMEM((d, ff), BF16), pltpu.VMEM((ff, d), BF16),
            pltpu.SemaphoreType.DMA((2, 2)),
        ],
    )
    return pl.pallas_call(
        functools.partial(_expert_kernel, layer=layer),
        grid_spec=grid_spec,
        out_shape=jax.ShapeDtypeStruct(xg_rows.shape, F32),
        input_output_aliases={5: 0},
        compiler_params=_cparams("arbitrary"),
        name="moe_experts",
    )(blk_e, nused, next_e, e_slot, blk_rows, xg_rows, w1_all, w2_all, b_glu, b_lin, b2, sel)


def _combine_kernel(n_ref, carry_ref, gstart_ref, yg_ref, lp_ref, gate_ref, x_ref, g2_ref, o_ref, sorted_ref, sem):
    i = pl.program_id(0)
    ntiles = pl.num_programs(0)
    tm = x_ref.shape[0]
    rows = TOP_K * tm
    slot = i % 2

    def runs(tile, buf, fn, live=None):
        _tile_runs(tile, n_ref, carry_ref, gstart_ref, sorted_ref.at[buf], yg_ref, sem.at[buf], tm, fn, False, live)

    @pl.when(i == 0)
    def _():
        runs(0, 0, _start_piece)

    runs(jnp.minimum(i + 1, ntiles - 1), 1 - slot, _start_piece, live=i + 1 < ntiles)

    eye = (lax.broadcasted_iota(jnp.int32, (tm, tm), 0) ==
           lax.broadcasted_iota(jnp.int32, (tm, tm), 1)).astype(BF16)
    rows_t = jnp.concatenate([gate_ref[...], lp_ref[...].astype(F32)], axis=0)
    parts = _split3(rows_t)
    cols = sum(lax.dot_general(eye, parts[:, j * tm:(j + 1) * tm], (((1,), (1,)), ((), ())),
                               preferred_element_type=F32) for j in range(3))
    col_id = lax.broadcasted_iota(jnp.int32, (tm, rows), 1)
    weights = jnp.zeros((tm, rows), F32)
    for k in range(TOP_K):
        lp_col = (cols[:, TOP_K + k:TOP_K + k + 1] + 0.5).astype(jnp.int32)
        weights = weights + jnp.where(col_id == lp_col, cols[:, k:k + 1], 0.0)

    pltpu.make_async_copy(yg_ref.at[pl.ds(0, rows * ROW_TILES)], sorted_ref.at[slot], sem.at[slot]).wait()
    y = jnp.concatenate(
        [sorted_ref[slot, pl.ds(s, rows, stride=ROW_TILES), :] for s in range(ROW_TILES)], axis=1).astype(BF16)
    acc = jnp.dot(weights.astype(BF16), y, preferred_element_type=F32)
    o_ref[...] = x_ref[...] + g2_ref[0] * acc


def _combine(tile_n, tile_carry, group_start, yg_rows, lp_t, gate_t, x, gate2, seq, tm):
    t, d = x.shape
    per_b = seq // tm
    grid_spec = pltpu.PrefetchScalarGridSpec(
        num_scalar_prefetch=3,
        grid=(t // tm,),
        in_specs=[
            pl.BlockSpec(memory_space=pl.ANY),
            pl.BlockSpec((TOP_K, tm), lambda i, *_: (0, i)),
            pl.BlockSpec((TOP_K, tm), lambda i, *_: (0, i)),
            pl.BlockSpec((tm, d), lambda i, *_: (i, 0)),
            pl.BlockSpec((1, 1, d), lambda i, *_: (i // per_b, 0, 0)),
        ],
        out_specs=pl.BlockSpec((tm, d), lambda i, *_: (i, 0)),
        scratch_shapes=[pltpu.VMEM((2, TOP_K * tm * ROW_TILES, LANES), F32), pltpu.SemaphoreType.DMA((2,))],
    )
    return pl.pallas_call(
        _combine_kernel,
        grid_spec=grid_spec,
        out_shape=jax.ShapeDtypeStruct((t, d), F32),
        compiler_params=_cparams("arbitrary"),
        name="moe_combine",
    )(tile_n, tile_carry, group_start, yg_rows, lp_t, gate_t, x, gate2)


def _moe(layer, x, g, shift, scale, gate2, r_w, r_b, w1_all, b1_glu, b1_lin, w2_all, b2, seq):
    t = x.shape[0]
    h, lp_t, gate_t, tile_n, tile_carry, counts = _router(x, g, shift, scale, r_w, r_b, seq, MOE_TILE)
    tile_n = tile_n.reshape(-1)
    tile_carry = tile_carry.reshape(-1)
    counts = counts.reshape(-1)
    padded = ((counts + MOE_BLOCK - 1) // MOE_BLOCK) * MOE_BLOCK
    group_end = jnp.cumsum(padded)
    group_start = (group_end - padded).astype(jnp.int32)
    cap = t * TOP_K + N_EXPERTS * MOE_BLOCK
    nblk = cap // MOE_BLOCK
    blk_start = jnp.arange(nblk, dtype=jnp.int32) * MOE_BLOCK
    e_ids = jnp.arange(N_EXPERTS, dtype=jnp.int32)
    past = (blk_start[None, :] >= group_end[:, None]).astype(jnp.int32)
    blk_e = jnp.minimum(jnp.sum(past, axis=0), N_EXPERTS - 1).astype(jnp.int32)
    nused = (group_end[-1:] // MOE_BLOCK).astype(jnp.int32)

    pad_lo = (group_start + counts).astype(jnp.int32)
    pad_hi = group_end.astype(jnp.int32)
    blk_pad_lo = jnp.sum(jnp.where(blk_e[None, :] == e_ids[:, None], pad_lo[:, None], 0), axis=0)
    blk_rows = jnp.clip(blk_pad_lo - blk_start, 0, MOE_BLOCK).astype(jnp.int32)
    xg_rows = _dispatch(tile_n, tile_carry, group_start, pad_lo, pad_hi, nused, lp_t, h, cap, MOE_TILE)
    nonempty = counts > 0
    later = jnp.where(nonempty[None, :] & (e_ids[None, :] > e_ids[:, None]), e_ids[None, :], N_EXPERTS)
    next_e = jnp.min(later, axis=1)
    next_e = jnp.where(next_e == N_EXPERTS, -1, next_e).astype(jnp.int32)
    e_slot = ((jnp.cumsum(nonempty.astype(jnp.int32)) - 1) % 2).astype(jnp.int32)
    yg_rows = _experts(layer, blk_e, nused, next_e, e_slot, blk_rows, xg_rows, w1_all, w2_all,
                       b1_glu[:, None, :], b1_lin[:, None, :], b2[:, None, :])
    return _combine(tile_n, tile_carry, group_start, yg_rows, lp_t, gate_t, x, gate2, seq, MOE_TILE)


def kernel(x, c, ada_w, ada_b, norm1_g, norm2_g, m_in_w, m_conv_w, m_conv_b, m_dt_bias, m_A_log, m_D, m_norm_g, m_out_w, a_qkv_w, a_q_norm_g, a_k_norm_g, a_sinks, a_out_w, rel_bias, r_w, r_b, e_w1, e_b1, e_w2, e_b2):
    batch, seq, d = x.shape
    depth = ada_w.shape[0]
    t = batch * seq
    xf = x.reshape(t, d)

    mod = _adaln(c, ada_w, ada_b)[:, :batch]

    b1_split = jnp.moveaxis(e_b1.reshape(depth, N_EXPERTS, -1, 2), -1, 1)

    for i in range(depth):
        parts =[mod[i, :, p * d:(p + 1) * d].reshape(batch, 1, d) for p in range(6)]
        sh1, sc1, g1, sh2, sc2, g2 = parts
        j = i // 2
        if i % 2 == 0:
            w_zxbc = m_in_w[j, :, :SSM_ZXBC].astype(BF16)
            w_dt = jnp.pad(m_in_w[j, :, SSM_ZXBC:], ((0, 0), (0, LANES - SSM_HEADS)))
            zxbc, dt_raw = _norm_matmul(xf, norm1_g[i], sh1, sc1, (w_zxbc, w_dt), seq)
            y = _ssd_mixer(zxbc, dt_raw, m_conv_w[j], m_conv_b[j], m_dt_bias[j], m_A_log[j], m_D[j],
                           m_norm_g[j], batch, seq)
            xf = _matmul_residual(y, m_out_w[j].astype(BF16), xf, g1, seq)
        else:
            qkv, = _norm_matmul(xf, norm1_g[i], sh1, sc1, (a_qkv_w[j].astype(BF16),), seq)
            y = _swa_mixer(qkv, a_q_norm_g[j], a_k_norm_g[j], a_sinks[j], rel_bias, batch, seq)
            xf = _matmul_residual(y, a_out_w[j].astype(BF16), xf, g1, seq)
        xf = _moe(i, xf, norm2_g[i], sh2, sc2, g2, r_w[i], r_b[i], e_w1, b1_split[i, 0], b1_split[i, 1],
                  e_w2, e_b2[i], seq)
    return xf.reshape(batch, seq, d)
```
